```python
import math
import jax, jax.numpy as jnp
from jax import lax
import numpy as np

D_MODEL = 1024
BATCH = 8
SEQ = 4096
DEPTH = 2

CHUNK = 64
PLE_DIM = 256
D_MIX = D_MODEL
SSM_WIDTH = D_MIX // 2
SSM_GROUP = 16
SSM_GROUPS = SSM_WIDTH // SSM_GROUP
SSM_STATE = 64
ATTN_WIDTH = D_MIX - SSM_WIDTH
ATTN_HEADS = 8
HEAD_DIM = ATTN_WIDTH // ATTN_HEADS
Q_BLOCK = 128
RMS_EPS = 1e-6
DT_MIN = 1e-3
DT_MAX = 1e-1
IN_COLS = 2 * SSM_WIDTH + 4 * ATTN_WIDTH
SPLITS = [SSM_WIDTH, 2 * SSM_WIDTH, 2 * SSM_WIDTH + ATTN_WIDTH,
          2 * SSM_WIDTH + 2 * ATTN_WIDTH, 2 * SSM_WIDTH + 3 * ATTN_WIDTH]

kernel_name = "hymba_s5_stickbreaking_ple_block"


def rmsnorm(x, g):
    xf = x.astype(jnp.float32)
    xf = xf * lax.rsqrt(jnp.mean(xf * xf, axis=-1, keepdims=True) + RMS_EPS)
    return (xf * g.astype(jnp.float32)).astype(x.dtype)


def _complex_scan_combine(left, right):
    a1r, a1i, x1r, x1i = left
    a2r, a2i, x2r, x2i = right
    ar = a2r * a1r - a2i * a1i
    ai = a2r * a1i + a2i * a1r
    xr = a2r * x1r - a2i * x1i + x2r
    xi = a2r * x1i + a2i * x1r + x2i
    return ar, ai, xr, xi


def s5_branch(u, a_re, a_im, log_dt, b_re, b_im, c_re, c_im, d_skip, w_glu, b_glu):
    f32 = jnp.float32
    bsz, seq, _ = u.shape
    uf = u.astype(f32).reshape(bsz, seq, SSM_GROUPS, SSM_GROUP)
    dt = jnp.exp(log_dt.astype(f32))[:, None]
    lr = a_re.astype(f32)
    li = a_im.astype(f32)
    mag = jnp.exp(lr * dt)
    ab_re = mag * jnp.cos(li * dt)
    ab_im = mag * jnp.sin(li * dt)
    num_re = ab_re - 1.0
    num_im = ab_im
    den = lr * lr + li * li
    f_re = (num_re * lr + num_im * li) / den
    f_im = (num_im * lr - num_re * li) / den
    br = b_re.astype(f32)
    bi = b_im.astype(f32)
    bb_re = f_re[..., None] * br - f_im[..., None] * bi
    bb_im = f_re[..., None] * bi + f_im[..., None] * br
    cr = c_re.astype(f32)
    ci = c_im.astype(f32)

    n_chunks = seq // CHUNK
    u_chunks = uf.reshape(bsz, n_chunks, CHUNK, SSM_GROUPS, SSM_GROUP).transpose(1, 2, 0, 3, 4)
    a_re_l = jnp.broadcast_to(ab_re[None, None], (CHUNK, 1, SSM_GROUPS, SSM_STATE))
    a_im_l = jnp.broadcast_to(ab_im[None, None], (CHUNK, 1, SSM_GROUPS, SSM_STATE))

    def chunk_step(carry, u_c):
        h_re, h_im = carry
        bu_re = jnp.einsum('lbgh,gph->lbgp', u_c, bb_re)
        bu_im = jnp.einsum('lbgh,gph->lbgp', u_c, bb_im)
        p_re, p_im, x_re, x_im = lax.associative_scan(
            _complex_scan_combine, (a_re_l, a_im_l, bu_re, bu_im), axis=0)
        x_re = x_re + p_re * h_re - p_im * h_im
        x_im = x_im + p_re * h_im + p_im * h_re
        y_c = jnp.einsum('lbgp,ghp->lbgh', x_re, cr) - jnp.einsum('lbgp,ghp->lbgh', x_im, ci)
        return (x_re[-1], x_im[-1]), y_c

    h0 = jnp.zeros((bsz, SSM_GROUPS, SSM_STATE), f32)
    _, y = lax.scan(chunk_step, (h0, h0), u_chunks)
    y = y.transpose(2, 0, 1, 3, 4).reshape(bsz, seq, SSM_GROUPS, SSM_GROUP)
    y = (y + d_skip.astype(f32) * uf).reshape(bsz, seq, SSM_WIDTH)
    z = jax.nn.gelu(y)
    zz = z @ w_glu.astype(f32) + b_glu.astype(f32)
    val, gate = jnp.split(zz, 2, axis=-1)
    return (val * jax.nn.sigmoid(gate)).astype(u.dtype)


def stick_breaking_branch(q, k, v, q_g, k_g):
    f32 = jnp.float32
    bsz, seq, _ = q.shape

    def heads(t):
        return t.reshape(bsz, seq, ATTN_HEADS, HEAD_DIM).transpose(0, 2, 1, 3)

    qh = rmsnorm(heads(q), q_g).astype(f32)
    kh = rmsnorm(heads(k), k_g).astype(f32)
    vh = heads(v).astype(f32)
    scale = HEAD_DIM ** -0.5
    outs = []
    for blk in range(seq // Q_BLOCK):
        q0 = blk * Q_BLOCK
        kv_len = q0 + Q_BLOCK
        qb = qh[:, :, q0:kv_len]
        kb = kh[:, :, :kv_len]
        vb = vh[:, :, :kv_len]
        z = jnp.einsum('bhqd,bhkd->bhqk', qb, kb) * scale
        t_pos = q0 + jnp.arange(Q_BLOCK)[:, None]
        s_pos = jnp.arange(kv_len)[None, :]
        strict = s_pos < t_pos
        log_stay = jnp.where(strict, jax.nn.log_sigmoid(-z), 0.0)
        later = lax.cumsum(log_stay, axis=3, reverse=True) - log_stay
        weights = jnp.where(strict, jnp.exp(jax.nn.log_sigmoid(z) + later), 0.0)
        outs.append(jnp.einsum('bhqk,bhkd->bhqd', weights, vb))
    o = jnp.concatenate(outs, axis=2)
    return o.transpose(0, 2, 1, 3).reshape(bsz, seq, ATTN_WIDTH).astype(q.dtype)


def _fwd_setup_inputs(seed: int = 0) -> dict:
    key = jax.random.key(seed)
    ks = jax.random.split(key, 20)
    f32 = jnp.float32
    G, P, H = SSM_GROUPS, SSM_STATE, SSM_GROUP
    n_idx = jnp.arange(P, dtype=f32)
    inputs = {
        "x": jax.random.normal(ks[0], (BATCH, SEQ, D_MODEL), f32),
        "p": jax.random.normal(ks[1], (DEPTH, BATCH, SEQ, PLE_DIM), f32),
        "mix_norm_g": 1.0 + 0.02 * jax.random.normal(ks[2], (DEPTH, D_MODEL), f32),
        "w_in": jax.random.normal(ks[3], (DEPTH, D_MODEL, IN_COLS), f32) * D_MODEL ** -0.5,
        "ssm_a_re": -0.5 + 0.01 * jax.random.normal(ks[4], (DEPTH, G, P), f32),
        "ssm_a_im": math.pi * n_idx + 0.01 * jax.random.normal(ks[5], (DEPTH, G, P), f32),
        "ssm_log_dt": jax.random.uniform(ks[6], (DEPTH, G), f32, math.log(DT_MIN), math.log(DT_MAX)),
        "ssm_b_re": jax.random.normal(ks[7], (DEPTH, G, P, H), f32) * (2.0 * H) ** -0.5,
        "ssm_b_im": jax.random.normal(ks[8], (DEPTH, G, P, H), f32) * (2.0 * H) ** -0.5,
        "ssm_c_re": jax.random.normal(ks[9], (DEPTH, G, H, P), f32) * (2.0 * P) ** -0.5,
        "ssm_c_im": jax.random.normal(ks[10], (DEPTH, G, H, P), f32) * (2.0 * P) ** -0.5,
        "ssm_d": jax.random.normal(ks[11], (DEPTH, G, H), f32),
        "ssm_w_glu": jax.random.normal(ks[12], (DEPTH, SSM_WIDTH, 2 * SSM_WIDTH), f32) * SSM_WIDTH ** -0.5,
        "ssm_b_glu": 0.01 * jax.random.normal(ks[13], (DEPTH, 2 * SSM_WIDTH), f32),
        "q_norm_g": 1.0 + 0.02 * jax.random.normal(ks[14], (DEPTH, HEAD_DIM), f32),
        "k_norm_g": 1.0 + 0.02 * jax.random.normal(ks[15], (DEPTH, HEAD_DIM), f32),
        "w_out": jax.random.normal(ks[16], (DEPTH, D_MIX, D_MODEL), f32) * D_MIX ** -0.5,
        "ple_norm_g": 1.0 + 0.02 * jax.random.normal(ks[17], (DEPTH, D_MODEL), f32),
        "w_ple_gate": jax.random.normal(ks[18], (DEPTH, D_MODEL, D_MODEL), f32) * D_MODEL ** -0.5,
        "w_ple_proj": jax.random.normal(ks[19], (DEPTH, PLE_DIM, D_MODEL), f32) * PLE_DIM ** -0.5,
    }
    return inputs


def _fwd_reference(x, p, mix_norm_g, w_in, ssm_a_re, ssm_a_im, ssm_log_dt, ssm_b_re, ssm_b_im,
              ssm_c_re, ssm_c_im, ssm_d, ssm_w_glu, ssm_b_glu, q_norm_g, k_norm_g, w_out,
              ple_norm_g, w_ple_gate, w_ple_proj):
    h = x
    for i in range(DEPTH):
        hn = rmsnorm(h, mix_norm_g[i])
        proj = hn @ w_in[i]
        u, g_ssm, q, k, v, g_attn = jnp.split(proj, SPLITS, axis=-1)
        y_ssm = s5_branch(u, ssm_a_re[i], ssm_a_im[i], ssm_log_dt[i], ssm_b_re[i], ssm_b_im[i],
                          ssm_c_re[i], ssm_c_im[i], ssm_d[i], ssm_w_glu[i], ssm_b_glu[i])
        y_ssm = y_ssm * jax.nn.silu(g_ssm)
        y_att = stick_breaking_branch(q, k, v, q_norm_g[i], k_norm_g[i]) * jax.nn.silu(g_attn)
        h = h + jnp.concatenate([y_ssm, y_att], axis=-1) @ w_out[i]
        ple_gate = jax.nn.sigmoid(rmsnorm(h, ple_norm_g[i]) @ w_ple_gate[i])
        h = h + ple_gate * (p[i] @ w_ple_proj[i])
    return h


import jax as _jax
import jax.numpy as _jnp

TWIN_FORMAT = 'train_step'
FWD_PARAMS = ['x', 'p', 'mix_norm_g', 'w_in', 'ssm_a_re', 'ssm_a_im', 'ssm_log_dt', 'ssm_b_re', 'ssm_b_im', 'ssm_c_re', 'ssm_c_im', 'ssm_d', 'ssm_w_glu', 'ssm_b_glu', 'q_norm_g', 'k_norm_g', 'w_out', 'ple_norm_g', 'w_ple_gate', 'w_ple_proj']
TWIN_WEIGHTS = ['mix_norm_g', 'w_in', 'ssm_a_re', 'ssm_a_im', 'ssm_log_dt', 'ssm_b_re', 'ssm_b_im', 'ssm_c_re', 'ssm_c_im', 'ssm_d', 'ssm_w_glu', 'ssm_b_glu', 'q_norm_g', 'k_norm_g', 'w_out', 'ple_norm_g', 'w_ple_gate', 'w_ple_proj']
TWIN_DIFF_INPUT = 'x'
TWIN_INPUTS = ['x', 'p', 'mix_norm_g', 'w_in', 'ssm_a_re', 'ssm_a_im', 'ssm_log_dt', 'ssm_b_re', 'ssm_b_im', 'ssm_c_re', 'ssm_c_im', 'ssm_d', 'ssm_w_glu', 'ssm_b_glu', 'q_norm_g', 'k_norm_g', 'w_out', 'ple_norm_g', 'w_ple_gate', 'w_ple_proj', 'loss_target', 'm_mix_norm_g', 'm_w_in', 'm_ssm_a_re', 'm_ssm_a_im', 'm_ssm_log_dt', 'm_ssm_b_re', 'm_ssm_b_im', 'm_ssm_c_re', 'm_ssm_c_im', 'm_ssm_d', 'm_ssm_w_glu', 'm_ssm_b_glu', 'm_q_norm_g', 'm_k_norm_g', 'm_w_out', 'm_ple_norm_g', 'm_w_ple_gate', 'm_w_ple_proj', 'v_mix_norm_g', 'v_w_in', 'v_ssm_a_re', 'v_ssm_a_im', 'v_ssm_log_dt', 'v_ssm_b_re', 'v_ssm_b_im', 'v_ssm_c_re', 'v_ssm_c_im', 'v_ssm_d', 'v_ssm_w_glu', 'v_ssm_b_glu', 'v_q_norm_g', 'v_k_norm_g', 'v_w_out', 'v_ple_norm_g', 'v_w_ple_gate', 'v_w_ple_proj']
TWIN_OUTPUTS = ['loss', 'grad_x', 'grad_mix_norm_g', 'grad_w_in', 'grad_ssm_a_re', 'grad_ssm_a_im', 'grad_ssm_log_dt', 'grad_ssm_b_re', 'grad_ssm_b_im', 'grad_ssm_c_re', 'grad_ssm_c_im', 'grad_ssm_d', 'grad_ssm_w_glu', 'grad_ssm_b_glu', 'grad_q_norm_g', 'grad_k_norm_g', 'grad_w_out', 'grad_ple_norm_g', 'grad_w_ple_gate', 'grad_w_ple_proj', 'delta_mix_norm_g', 'delta_w_in', 'delta_ssm_a_re', 'delta_ssm_a_im', 'delta_ssm_log_dt', 'delta_ssm_b_re', 'delta_ssm_b_im', 'delta_ssm_c_re', 'delta_ssm_c_im', 'delta_ssm_d', 'delta_ssm_w_glu', 'delta_ssm_b_glu', 'delta_q_norm_g', 'delta_k_norm_g', 'delta_w_out', 'delta_ple_norm_g', 'delta_w_ple_gate', 'delta_w_ple_proj', 'new_m_mix_norm_g', 'new_m_w_in', 'new_m_ssm_a_re', 'new_m_ssm_a_im', 'new_m_ssm_log_dt', 'new_m_ssm_b_re', 'new_m_ssm_b_im', 'new_m_ssm_c_re', 'new_m_ssm_c_im', 'new_m_ssm_d', 'new_m_ssm_w_glu', 'new_m_ssm_b_glu', 'new_m_q_norm_g', 'new_m_k_norm_g', 'new_m_w_out', 'new_m_ple_norm_g', 'new_m_w_ple_gate', 'new_m_w_ple_proj', 'new_v_mix_norm_g', 'new_v_w_in', 'new_v_ssm_a_re', 'new_v_ssm_a_im', 'new_v_ssm_log_dt', 'new_v_ssm_b_re', 'new_v_ssm_b_im', 'new_v_ssm_c_re', 'new_v_ssm_c_im', 'new_v_ssm_d', 'new_v_ssm_w_glu', 'new_v_ssm_b_glu', 'new_v_q_norm_g', 'new_v_k_norm_g', 'new_v_w_out', 'new_v_ple_norm_g', 'new_v_w_ple_gate', 'new_v_w_ple_proj']
TWIN_LEAF_KINDS = {'loss': 'loss', 'grad_x': 'grad_x', 'grad_mix_norm_g': 'grad_w', 'grad_w_in': 'grad_w', 'grad_ssm_a_re': 'grad_w', 'grad_ssm_a_im': 'grad_w', 'grad_ssm_log_dt': 'grad_w', 'grad_ssm_b_re': 'grad_w', 'grad_ssm_b_im': 'grad_w', 'grad_ssm_c_re': 'grad_w', 'grad_ssm_c_im': 'grad_w', 'grad_ssm_d': 'grad_w', 'grad_ssm_w_glu': 'grad_w', 'grad_ssm_b_glu': 'grad_w', 'grad_q_norm_g': 'grad_w', 'grad_k_norm_g': 'grad_w', 'grad_w_out': 'grad_w', 'grad_ple_norm_g': 'grad_w', 'grad_w_ple_gate': 'grad_w', 'grad_w_ple_proj': 'grad_w', 'delta_mix_norm_g': 'delta_w', 'delta_w_in': 'delta_w', 'delta_ssm_a_re': 'delta_w', 'delta_ssm_a_im': 'delta_w', 'delta_ssm_log_dt': 'delta_w', 'delta_ssm_b_re': 'delta_w', 'delta_ssm_b_im': 'delta_w', 'delta_ssm_c_re': 'delta_w', 'delta_ssm_c_im': 'delta_w', 'delta_ssm_d': 'delta_w', 'delta_ssm_w_glu': 'delta_w', 'delta_ssm_b_glu': 'delta_w', 'delta_q_norm_g': 'delta_w', 'delta_k_norm_g': 'delta_w', 'delta_w_out': 'delta_w', 'delta_ple_norm_g': 'delta_w', 'delta_w_ple_gate': 'delta_w', 'delta_w_ple_proj': 'delta_w', 'new_m_mix_norm_g': 'new_m', 'new_m_w_in': 'new_m', 'new_m_ssm_a_re': 'new_m', 'new_m_ssm_a_im': 'new_m', 'new_m_ssm_log_dt': 'new_m', 'new_m_ssm_b_re': 'new_m', 'new_m_ssm_b_im': 'new_m', 'new_m_ssm_c_re': 'new_m', 'new_m_ssm_c_im': 'new_m', 'new_m_ssm_d': 'new_m', 'new_m_ssm_w_glu': 'new_m', 'new_m_ssm_b_glu': 'new_m', 'new_m_q_norm_g': 'new_m', 'new_m_k_norm_g': 'new_m', 'new_m_w_out': 'new_m', 'new_m_ple_norm_g': 'new_m', 'new_m_w_ple_gate': 'new_m', 'new_m_w_ple_proj': 'new_m', 'new_v_mix_norm_g': 'new_v', 'new_v_w_in': 'new_v', 'new_v_ssm_a_re': 'new_v', 'new_v_ssm_a_im': 'new_v', 'new_v_ssm_log_dt': 'new_v', 'new_v_ssm_b_re': 'new_v', 'new_v_ssm_b_im': 'new_v', 'new_v_ssm_c_re': 'new_v', 'new_v_ssm_c_im': 'new_v', 'new_v_ssm_d': 'new_v', 'new_v_ssm_w_glu': 'new_v', 'new_v_ssm_b_glu': 'new_v', 'new_v_q_norm_g': 'new_v', 'new_v_k_norm_g': 'new_v', 'new_v_w_out': 'new_v', 'new_v_ple_norm_g': 'new_v', 'new_v_w_ple_gate': 'new_v', 'new_v_w_ple_proj': 'new_v'}


def _forward(args):
    return _fwd_reference(*[args[k] for k in FWD_PARAMS])


def _output_shape():
    out = _jax.eval_shape(lambda: _forward(_fwd_setup_inputs(0)))
    return out.shape, out.dtype

N_MICROBATCH = 1
ADAM_LR = 0.001
ADAM_B1 = 0.9
ADAM_B2 = 0.999
ADAM_EPS = 1e-08
ADAM_WD = 0.01
ADAM_STEP = 10
PER_EXAMPLE_BATCH_AXIS = {'x': 0, 'p': 1, 'loss_target': 0}
SHARED_INPUTS = []
_WEIGHT_DTYPES = {'mix_norm_g': _jnp.float32, 'w_in': _jnp.float32, 'ssm_a_re': _jnp.float32, 'ssm_a_im': _jnp.float32, 'ssm_log_dt': _jnp.float32, 'ssm_b_re': _jnp.float32, 'ssm_b_im': _jnp.float32, 'ssm_c_re': _jnp.float32, 'ssm_c_im': _jnp.float32, 'ssm_d': _jnp.float32, 'ssm_w_glu': _jnp.float32, 'ssm_b_glu': _jnp.float32, 'q_norm_g': _jnp.float32, 'k_norm_g': _jnp.float32, 'w_out': _jnp.float32, 'ple_norm_g': _jnp.float32, 'w_ple_gate': _jnp.float32, 'w_ple_proj': _jnp.float32}
MOMENT_SCALE = {'mix_norm_g': 6.758749e+00, 'w_in': 1.273279e-01, 'ssm_a_re': 5.547057e-03, 'ssm_a_im': 4.937925e-03, 'ssm_log_dt': 1.727148e+00, 'ssm_b_re': 3.366141e-03, 'ssm_b_im': 3.408226e-03, 'ssm_c_re': 7.018219e-03, 'ssm_c_im': 6.919373e-03, 'ssm_d': 1.685525e+00, 'ssm_w_glu': 2.652259e-01, 'ssm_b_glu': 8.981405e-01, 'q_norm_g': 5.739757e+00, 'k_norm_g': 5.746797e+00, 'w_out': 1.402934e-01, 'ple_norm_g': 9.478171e-01, 'w_ple_gate': 8.903263e-02, 'w_ple_proj': 4.206466e-01}


def _to_microbatches(a, axis):
    t = _jnp.moveaxis(a, axis, 0)
    t = t.reshape((N_MICROBATCH, t.shape[0] // N_MICROBATCH) + t.shape[1:])
    return _jnp.moveaxis(t, 1, axis + 1)


def setup_inputs(seed: int = 0) -> dict:
    inp = _fwd_setup_inputs(seed)
    key = _jax.random.fold_in(_jax.random.key(seed), 7919)
    shape, _ = _output_shape()
    out = dict(inp)
    out["loss_target"] = _jax.random.normal(_jax.random.fold_in(key, 0), shape, _jnp.float32)
    for i, name in enumerate(TWIN_WEIGHTS):
        w = inp[name].astype(_jnp.float32)
        if MOMENT_SCALE is None:
            s = _jnp.sqrt(_jnp.mean(_jnp.square(w)) + 1e-30)
        else:
            s = MOMENT_SCALE[name]
        km, kv = _jax.random.split(_jax.random.fold_in(key, i + 1))
        out[name] = w
        out["m_" + name] = s * _jax.random.normal(km, w.shape, _jnp.float32)
        out["v_" + name] = (s * s) * _jax.random.uniform(kv, w.shape, _jnp.float32, 0.5, 1.5)
    if N_MICROBATCH > 1:
        for name, axis in PER_EXAMPLE_BATCH_AXIS.items():
            out[name] = _to_microbatches(out[name], axis)
    return {'x': out['x'], 'p': out['p'], 'mix_norm_g': out['mix_norm_g'], 'w_in': out['w_in'], 'ssm_a_re': out['ssm_a_re'], 'ssm_a_im': out['ssm_a_im'], 'ssm_log_dt': out['ssm_log_dt'], 'ssm_b_re': out['ssm_b_re'], 'ssm_b_im': out['ssm_b_im'], 'ssm_c_re': out['ssm_c_re'], 'ssm_c_im': out['ssm_c_im'], 'ssm_d': out['ssm_d'], 'ssm_w_glu': out['ssm_w_glu'], 'ssm_b_glu': out['ssm_b_glu'], 'q_norm_g': out['q_norm_g'], 'k_norm_g': out['k_norm_g'], 'w_out': out['w_out'], 'ple_norm_g': out['ple_norm_g'], 'w_ple_gate': out['w_ple_gate'], 'w_ple_proj': out['w_ple_proj'], 'loss_target': out['loss_target'], 'm_mix_norm_g': out['m_mix_norm_g'], 'm_w_in': out['m_w_in'], 'm_ssm_a_re': out['m_ssm_a_re'], 'm_ssm_a_im': out['m_ssm_a_im'], 'm_ssm_log_dt': out['m_ssm_log_dt'], 'm_ssm_b_re': out['m_ssm_b_re'], 'm_ssm_b_im': out['m_ssm_b_im'], 'm_ssm_c_re': out['m_ssm_c_re'], 'm_ssm_c_im': out['m_ssm_c_im'], 'm_ssm_d': out['m_ssm_d'], 'm_ssm_w_glu': out['m_ssm_w_glu'], 'm_ssm_b_glu': out['m_ssm_b_glu'], 'm_q_norm_g': out['m_q_norm_g'], 'm_k_norm_g': out['m_k_norm_g'], 'm_w_out': out['m_w_out'], 'm_ple_norm_g': out['m_ple_norm_g'], 'm_w_ple_gate': out['m_w_ple_gate'], 'm_w_ple_proj': out['m_w_ple_proj'], 'v_mix_norm_g': out['v_mix_norm_g'], 'v_w_in': out['v_w_in'], 'v_ssm_a_re': out['v_ssm_a_re'], 'v_ssm_a_im': out['v_ssm_a_im'], 'v_ssm_log_dt': out['v_ssm_log_dt'], 'v_ssm_b_re': out['v_ssm_b_re'], 'v_ssm_b_im': out['v_ssm_b_im'], 'v_ssm_c_re': out['v_ssm_c_re'], 'v_ssm_c_im': out['v_ssm_c_im'], 'v_ssm_d': out['v_ssm_d'], 'v_ssm_w_glu': out['v_ssm_w_glu'], 'v_ssm_b_glu': out['v_ssm_b_glu'], 'v_q_norm_g': out['v_q_norm_g'], 'v_k_norm_g': out['v_k_norm_g'], 'v_w_out': out['v_w_out'], 'v_ple_norm_g': out['v_ple_norm_g'], 'v_w_ple_gate': out['v_w_ple_gate'], 'v_w_ple_proj': out['v_w_ple_proj']}


def _loss(weights, diff, rest, loss_target):
    with _jax.named_scope("forward"):
        args = {**rest, TWIN_DIFF_INPUT: diff, **{k: w.astype(_WEIGHT_DTYPES[k]) for k, w in weights.items()}}
        y = _forward(args)
    with _jax.named_scope("loss_head"):
        err = _jnp.square(y.astype(_jnp.float32) - loss_target)
        return 0.5 * _jnp.sum(_jnp.mean(err, axis=-1)) if err.ndim else 0.5 * err


def _adamw(w, g, m, v):
    m = ADAM_B1 * m + (1.0 - ADAM_B1) * g
    v = ADAM_B2 * v + (1.0 - ADAM_B2) * _jnp.square(g)
    m_hat = m / (1.0 - ADAM_B1 ** ADAM_STEP)
    v_hat = v / (1.0 - ADAM_B2 ** ADAM_STEP)
    delta = -ADAM_LR * (m_hat / (_jnp.sqrt(v_hat) + ADAM_EPS) + ADAM_WD * w)
    return delta, m, v


def reference(x, p, mix_norm_g, w_in, ssm_a_re, ssm_a_im, ssm_log_dt, ssm_b_re, ssm_b_im, ssm_c_re, ssm_c_im, ssm_d, ssm_w_glu, ssm_b_glu, q_norm_g, k_norm_g, w_out, ple_norm_g, w_ple_gate, w_ple_proj, loss_target, m_mix_norm_g, m_w_in, m_ssm_a_re, m_ssm_a_im, m_ssm_log_dt, m_ssm_b_re, m_ssm_b_im, m_ssm_c_re, m_ssm_c_im, m_ssm_d, m_ssm_w_glu, m_ssm_b_glu, m_q_norm_g, m_k_norm_g, m_w_out, m_ple_norm_g, m_w_ple_gate, m_w_ple_proj, v_mix_norm_g, v_w_in, v_ssm_a_re, v_ssm_a_im, v_ssm_log_dt, v_ssm_b_re, v_ssm_b_im, v_ssm_c_re, v_ssm_c_im, v_ssm_d, v_ssm_w_glu, v_ssm_b_glu, v_q_norm_g, v_k_norm_g, v_w_out, v_ple_norm_g, v_w_ple_gate, v_w_ple_proj):
    given = dict(x=x, p=p, mix_norm_g=mix_norm_g, w_in=w_in, ssm_a_re=ssm_a_re, ssm_a_im=ssm_a_im, ssm_log_dt=ssm_log_dt, ssm_b_re=ssm_b_re, ssm_b_im=ssm_b_im, ssm_c_re=ssm_c_re, ssm_c_im=ssm_c_im, ssm_d=ssm_d, ssm_w_glu=ssm_w_glu, ssm_b_glu=ssm_b_glu, q_norm_g=q_norm_g, k_norm_g=k_norm_g, w_out=w_out, ple_norm_g=ple_norm_g, w_ple_gate=w_ple_gate, w_ple_proj=w_ple_proj, loss_target=loss_target, m_mix_norm_g=m_mix_norm_g, m_w_in=m_w_in, m_ssm_a_re=m_ssm_a_re, m_ssm_a_im=m_ssm_a_im, m_ssm_log_dt=m_ssm_log_dt, m_ssm_b_re=m_ssm_b_re, m_ssm_b_im=m_ssm_b_im, m_ssm_c_re=m_ssm_c_re, m_ssm_c_im=m_ssm_c_im, m_ssm_d=m_ssm_d, m_ssm_w_glu=m_ssm_w_glu, m_ssm_b_glu=m_ssm_b_glu, m_q_norm_g=m_q_norm_g, m_k_norm_g=m_k_norm_g, m_w_out=m_w_out, m_ple_norm_g=m_ple_norm_g, m_w_ple_gate=m_w_ple_gate, m_w_ple_proj=m_w_ple_proj, v_mix_norm_g=v_mix_norm_g, v_w_in=v_w_in, v_ssm_a_re=v_ssm_a_re, v_ssm_a_im=v_ssm_a_im, v_ssm_log_dt=v_ssm_log_dt, v_ssm_b_re=v_ssm_b_re, v_ssm_b_im=v_ssm_b_im, v_ssm_c_re=v_ssm_c_re, v_ssm_c_im=v_ssm_c_im, v_ssm_d=v_ssm_d, v_ssm_w_glu=v_ssm_w_glu, v_ssm_b_glu=v_ssm_b_glu, v_q_norm_g=v_q_norm_g, v_k_norm_g=v_k_norm_g, v_w_out=v_w_out, v_ple_norm_g=v_ple_norm_g, v_w_ple_gate=v_w_ple_gate, v_w_ple_proj=v_w_ple_proj)
    weights = {n: given[n] for n in TWIN_WEIGHTS}
    shared = {n: given[n] for n in SHARED_INPUTS}
    per_example = {n: given[n] for n in ['x', 'p']}
    grad_fn = _jax.value_and_grad(_loss, argnums=(0, 1))

    def one_microbatch(ex, loss_target):
        ex = dict(ex)
        diff = ex.pop(TWIN_DIFF_INPUT)
        return grad_fn(weights, diff, {**shared, **ex}, loss_target)

    if N_MICROBATCH == 1:
        loss, (grad_w, grad_x) = one_microbatch(per_example, given["loss_target"])
    else:
        def body(carry, xs):
            loss_sum, grad_sum = carry
            l_k, (gw_k, gx_k) = one_microbatch(xs[0], xs[1])
            with _jax.named_scope("update"):
                return (loss_sum + l_k, _jax.tree.map(_jnp.add, grad_sum, gw_k)), gx_k

        init = (_jnp.zeros((), _jnp.float32), _jax.tree.map(_jnp.zeros_like, weights))
        (loss, grad_w), grad_x = _jax.lax.scan(body, init, (per_example, given["loss_target"]))
    with _jax.named_scope("update"):
        delta_w, new_m, new_v = {}, {}, {}
        for n in TWIN_WEIGHTS:
            delta_w[n], new_m[n], new_v[n] = _adamw(weights[n], grad_w[n], given["m_" + n], given["v_" + n])
    return (loss, grad_x, *[grad_w[n] for n in TWIN_WEIGHTS], *[delta_w[n] for n in TWIN_WEIGHTS],
            *[new_m[n] for n in TWIN_WEIGHTS], *[new_v[n] for n in TWIN_WEIGHTS])
```

```python
import functools
import math

import jax
import jax.numpy as jnp
from jax import lax
from jax.experimental import pallas as pl
from jax.experimental.pallas import tpu as pltpu

F32 = jnp.float32
BF16 = jnp.bfloat16

D_MODEL = 1024
N_LAYERS = 2
N_CHIPS = 4
IN_COLS = 3072
IN_SHARD = IN_COLS // N_CHIPS
SSM_WIDTH = 512
SSM_GROUP = 16
SSM_GROUPS = 32
SSM_STATE = 64
N_STATES = SSM_GROUPS * SSM_STATE
SSM_CHUNKS = 4
CH_W = SSM_WIDTH // SSM_CHUNKS
CH_S = N_STATES // SSM_CHUNKS
ATTN_WIDTH = 512
HEAD_DIM = 64
PLE_DIM = 256
ROW_SHARD = 256
RMS_EPS = 1e-6
ATTN_SCALE = HEAD_DIM ** -0.5
ATTN_BLOCK = 128
EXP_ZERO = -104.0
SUBLANES = 8
V7X_VMEM_LIMIT = 52 * 1024 * 1024

ADAM_LR = 0.001
ADAM_B1 = 0.9
ADAM_B2 = 0.999
ADAM_EPS = 1e-08
ADAM_WD = 0.01
ADAM_STEP = 10

MESH = pl.DeviceIdType.MESH
ANY = pl.BlockSpec(memory_space=pl.ANY)


def _cparams(n_grid=0, parallel=0):
    sem = tuple(["parallel"] * parallel + ["arbitrary"] * (n_grid - parallel))
    return pltpu.CompilerParams(dimension_semantics=sem, vmem_limit_bytes=V7X_VMEM_LIMIT)


def _dot(a, b):
    return jnp.dot(a, b, preferred_element_type=F32)


def _dot_nt(a, b):
    return lax.dot_general(a, b, (((1,), (1,)), ((), ())), preferred_element_type=F32)


def _dot_tn(a, b):
    return lax.dot_general(a, b, (((0,), (0,)), ((), ())), preferred_element_type=F32)


def _split_hilo(a):
    hi = a.astype(BF16)
    lo = (a - hi.astype(F32)).astype(BF16)
    return hi, lo


def _dot_hilo(a, b):
    hi, lo = _split_hilo(a)
    return _dot(hi, b) + _dot(lo, b)


def _sigmoid(x):
    return 0.5 * (jnp.tanh(0.5 * x) + 1.0)


_GELU_C = math.sqrt(2.0 / math.pi)


def _gelu(x):
    return 0.5 * x * (1.0 + jnp.tanh(_GELU_C * (x + 0.044715 * (x * x * x))))


def _gelu_grad(x):
    t = jnp.tanh(_GELU_C * (x + 0.044715 * (x * x * x)))
    return 0.5 * (1.0 + t) + 0.5 * x * (1.0 - t * t) * (_GELU_C * (1.0 + 3.0 * 0.044715 * (x * x)))


def _row_tile(s, want):
    for t in range(min(s, want), 7, -1):
        if s % t == 0 and t % SUBLANES == 0:
            return t
    return s


def _coords():
    return lax.axis_index("x"), lax.axis_index("y"), lax.axis_index("c")


def _other_chips(x, y):
    return [(1 - x, y), (x, 1 - y), (1 - x, 1 - y)]


def _remote(src, dst, send_sem, recv_sem, dev):
    return pltpu.make_async_remote_copy(src_ref=src, dst_ref=dst, send_sem=send_sem, recv_sem=recv_sem,
                                        device_id=dev, device_id_type=MESH)


def _chip_gather(arrs, name):
    n = len(arrs)

    def body(*refs):
        ins, outs = refs[:n], refs[n:2 * n]
        send_sems, recv_sems, fwd_send, fwd_recv, loc_sems = refs[2 * n:]
        x, y, c = _coords()
        me_chip = 2 * x + y
        chips = _other_chips(x, y)
        sibling = (x, y, 1 - c)
        local = [pltpu.make_async_copy(ins[k], outs[k].at[me_chip], loc_sems.at[k]) for k in range(n)]
        for cp in local:
            cp.start()
        first = []
        for k in range(n):
            for j, (cx, cy) in enumerate(chips):
                cp = _remote(ins[k].at[c], outs[k].at[me_chip, c], send_sems.at[3 * k + j], recv_sems.at[3 * k + j],
                             (cx, cy, c))
                cp.start()
                first.append(cp)
        passed = []
        for k in range(n):
            for j, (cx, cy) in enumerate(chips):
                blk = outs[k].at[2 * cx + cy, c]
                _remote(blk, blk, send_sems.at[3 * k + j], recv_sems.at[3 * k + j], (cx, cy, c)).wait_recv()
                cp = _remote(blk, blk, fwd_send.at[3 * k + j], fwd_recv.at[3 * k + j], sibling)
                cp.start()
                passed.append(cp)
        for k in range(n):
            for j, (cx, cy) in enumerate(chips):
                blk = outs[k].at[2 * cx + cy, 1 - c]
                _remote(blk, blk, fwd_send.at[3 * k + j], fwd_recv.at[3 * k + j], sibling).wait_recv()
        for cp in first + passed:
            cp.wait_send()
        for cp in local:
            cp.wait()

    return pl.pallas_call(
        body, name=name,
        out_shape=[jax.ShapeDtypeStruct((N_CHIPS,) + a.shape, a.dtype) for a in arrs],
        in_specs=[ANY] * n, out_specs=[ANY] * n,
        scratch_shapes=[pltpu.SemaphoreType.DMA((3 * n,)), pltpu.SemaphoreType.DMA((3 * n,)),
                        pltpu.SemaphoreType.DMA((3 * n,)), pltpu.SemaphoreType.DMA((3 * n,)),
                        pltpu.SemaphoreType.DMA((n,))],
    )(*arrs)


def _sibling_send_other_half(arrs, name):
    n = len(arrs)

    def body(*refs):
        ins, outs = refs[:n], refs[n:2 * n]
        send_sems, recv_sems = refs[2 * n:]
        x, y, c = _coords()
        cps = [_remote(ins[k].at[1 - c], outs[k], send_sems.at[k], recv_sems.at[k], (x, y, 1 - c)) for k in range(n)]
        for cp in cps:
            cp.start()
        for cp in cps:
            cp.wait_recv()
        for cp in cps:
            cp.wait_send()

    return pl.pallas_call(
        body, name=name,
        out_shape=[jax.ShapeDtypeStruct(a.shape[1:], a.dtype) for a in arrs],
        in_specs=[ANY] * n, out_specs=[ANY] * n,
        scratch_shapes=[pltpu.SemaphoreType.DMA((n,)), pltpu.SemaphoreType.DMA((n,))],
    )(*arrs)


def _sibling_join_halves(arrs, name):
    n = len(arrs)

    def body(*refs):
        ins, outs = refs[:n], refs[n:2 * n]
        send_sems, recv_sems, loc_sems = refs[2 * n:]
        x, y, c = _coords()
        local = [pltpu.make_async_copy(ins[k], outs[k].at[c], loc_sems.at[k]) for k in range(n)]
        for cp in local:
            cp.start()
        cps = [_remote(ins[k], outs[k].at[c], send_sems.at[k], recv_sems.at[k], (x, y, 1 - c)) for k in range(n)]
        for cp in cps:
            cp.start()
        for k in range(n):
            blk = outs[k].at[1 - c]
            _remote(blk, blk, send_sems.at[k], recv_sems.at[k], (x, y, 1 - c)).wait_recv()
        for cp in cps:
            cp.wait_send()
        for cp in local:
            cp.wait()

    return pl.pallas_call(
        body, name=name,
        out_shape=[jax.ShapeDtypeStruct((2,) + a.shape, a.dtype) for a in arrs],
        in_specs=[ANY] * n, out_specs=[ANY] * n,
        scratch_shapes=[pltpu.SemaphoreType.DMA((n,)), pltpu.SemaphoreType.DMA((n,)), pltpu.SemaphoreType.DMA((n,))],
    )(*arrs)


def _chip_scatter(arrs, name):
    n = len(arrs)

    def body(*refs):
        ins, outs = refs[:n], refs[n:2 * n]
        send_sems, recv_sems, loc_sems = refs[2 * n:]
        x, y, c = _coords()
        me_chip = 2 * x + y
        chips = _other_chips(x, y)
        local = [pltpu.make_async_copy(ins[k].at[me_chip], outs[k].at[me_chip], loc_sems.at[k]) for k in range(n)]
        for cp in local:
            cp.start()
        cps = []
        for k in range(n):
            for j, (cx, cy) in enumerate(chips):
                cp = _remote(ins[k].at[2 * cx + cy], outs[k].at[me_chip], send_sems.at[3 * k + j],
                             recv_sems.at[3 * k + j], (cx, cy, c))
                cp.start()
                cps.append(cp)
        for k in range(n):
            for j, (cx, cy) in enumerate(chips):
                blk = outs[k].at[2 * cx + cy]
                _remote(blk, blk, send_sems.at[3 * k + j], recv_sems.at[3 * k + j], (cx, cy, c)).wait_recv()
        for cp in cps:
            cp.wait_send()
        for cp in local:
            cp.wait()

    return pl.pallas_call(
        body, name=name,
        out_shape=[jax.ShapeDtypeStruct(a.shape, a.dtype) for a in arrs],
        in_specs=[ANY] * n, out_specs=[ANY] * n,
        scratch_shapes=[pltpu.SemaphoreType.DMA((3 * n,)), pltpu.SemaphoreType.DMA((3 * n,)),
                        pltpu.SemaphoreType.DMA((n,))],
    )(*arrs)


def _as_rows(a, lanes):
    return a.reshape(-1, lanes)


def _add_half(full, recv, out_dtype, name):
    _, r, cdim = full.shape
    tr = _row_tile(r, 512)

    def body(c_ref, a_ref, b_ref, o_ref):
        o_ref[...] = (a_ref[...] + b_ref[...]).astype(out_dtype)

    c = lax.axis_index("c").astype(jnp.int32).reshape(1)
    return pl.pallas_call(
        body, name=name,
        grid_spec=pltpu.PrefetchScalarGridSpec(
            num_scalar_prefetch=1, grid=(r // tr,),
            in_specs=[pl.BlockSpec((None, tr, cdim), lambda i, c_ref: (c_ref[0], i, 0)),
                      pl.BlockSpec((tr, cdim), lambda i, c_ref: (i, 0))],
            out_specs=pl.BlockSpec((tr, cdim), lambda i, c_ref: (i, 0))),
        out_shape=jax.ShapeDtypeStruct((r, cdim), out_dtype),
        compiler_params=_cparams(1),
    )(c, full, recv)


def _sum4(parts, name):
    _, r, cdim = parts.shape
    tr = _row_tile(r, 512)

    def body(p_ref, o_ref):
        acc = p_ref[0].astype(F32) + p_ref[1].astype(F32)
        acc = acc + p_ref[2].astype(F32)
        o_ref[...] = acc + p_ref[3].astype(F32)

    return pl.pallas_call(
        body, name=name, grid=(r // tr,),
        in_specs=[pl.BlockSpec((N_CHIPS, tr, cdim), lambda i: (0, i, 0))],
        out_specs=pl.BlockSpec((tr, cdim), lambda i: (i, 0)),
        out_shape=jax.ShapeDtypeStruct((r, cdim), F32),
        compiler_params=_cparams(1),
    )(parts)


def _adamw(w, g, m, v, name):
    r, cdim = w.shape
    tr = _row_tile(r, 256)
    c1 = 1.0 - ADAM_B1 ** ADAM_STEP
    c2 = 1.0 - ADAM_B2 ** ADAM_STEP

    def body(w_ref, g_ref, m_ref, v_ref, d_ref, nm_ref, nv_ref):
        gg = g_ref[...]
        nm = ADAM_B1 * m_ref[...] + (1.0 - ADAM_B1) * gg
        nv = ADAM_B2 * v_ref[...] + (1.0 - ADAM_B2) * (gg * gg)
        m_hat = nm / c1
        v_hat = nv / c2
        d_ref[...] = -ADAM_LR * (m_hat / (jnp.sqrt(v_hat) + ADAM_EPS) + ADAM_WD * w_ref[...])
        nm_ref[...] = nm
        nv_ref[...] = nv

    spec = pl.BlockSpec((tr, cdim), lambda i: (i, 0))
    return pl.pallas_call(
        body, name=name, grid=(r // tr,),
        in_specs=[spec] * 4, out_specs=[spec] * 3,
        out_shape=[jax.ShapeDtypeStruct((r, cdim), F32)] * 3,
        compiler_params=_cparams(1),
    )(w, g, m, v)


def _discretise(a_re, a_im, log_dt, b_re, b_im):
    dt = jnp.exp(log_dt)
    mag = jnp.exp(a_re * dt)
    ab_re = mag * jnp.cos(a_im * dt)
    ab_im = mag * jnp.sin(a_im * dt)
    num_re = ab_re - 1.0
    num_im = ab_im
    den = a_re * a_re + a_im * a_im
    f_re = (num_re * a_re + num_im * a_im) / den
    f_im = (num_im * a_re - num_re * a_im) / den
    bb_re = f_re * b_re - f_im * b_im
    bb_im = f_re * b_im + f_im * b_re
    return ab_re, ab_im, bb_re, bb_im


def _disc_shapes():
    col = jax.ShapeDtypeStruct((N_STATES, 1), F32)
    mat = jax.ShapeDtypeStruct((N_STATES, SSM_GROUP), F32)
    return col, mat


def _disc_fwd(a_re, a_im, log_dt, b_re, b_im):
    col, mat = _disc_shapes()

    def body(ar, ai, ld, br, bi, o0, o1, o2, o3):
        outs = _discretise(ar[...], ai[...], ld[...], br[...], bi[...])
        for o, val in zip((o0, o1, o2, o3), outs):
            o[...] = val

    return pl.pallas_call(body, name="ssm_discretise", out_shape=[col, col, mat, mat],
                          compiler_params=_cparams())(a_re, a_im, log_dt, b_re, b_im)


def _disc_bwd(a_re, a_im, log_dt, b_re, b_im, g_ab_re, g_ab_im, g_bb_re, g_bb_im):
    col, mat = _disc_shapes()

    def body(ar, ai, ld, br, bi, g0, g1, g2, g3, o0, o1, o2, o3, o4):
        _, vjp = jax.vjp(_discretise, ar[...], ai[...], ld[...], br[...], bi[...])
        grads = vjp((g0[...], g1[...], g2[...], g3[...]))
        for o, val in zip((o0, o1, o2, o3, o4), grads):
            o[...] = val

    return pl.pallas_call(body, name="ssm_discretise_bwd", out_shape=[col, col, col, mat, mat],
                          compiler_params=_cparams())(a_re, a_im, log_dt, b_re, b_im, g_ab_re, g_ab_im, g_bb_re, g_bb_im)


def _cmul(ar, ai, br, bi):
    return ar * br - ai * bi, ar * bi + ai * br


def _scan_coefs(ab_re, ab_im, reverse):
    ar = ab_re.reshape(1, N_STATES)
    ai = ab_im.reshape(1, N_STATES)
    if reverse:
        ai = -ai
    a2 = _cmul(ar, ai, ar, ai)
    a4 = _cmul(*a2, *a2)
    rows = jnp.arange(SUBLANES)[:, None]
    out = []
    for (pr, pi), sh in (((ar, ai), 1), (a2, 2), (a4, 4)):
        keep = (rows <= SUBLANES - 1 - sh) if reverse else (rows >= sh)
        out += [jnp.where(keep, pr, 0.0), jnp.where(keep, pi, 0.0)]
    pows = [(ar, ai)]
    for _ in range(SUBLANES - 1):
        pows.append(_cmul(*pows[-1], ar, ai))
    order = pows[::-1] if reverse else pows
    out += [jnp.concatenate([p[0] for p in order], 0), jnp.concatenate([p[1] for p in order], 0)]
    t = jnp.stack(out, 0)
    return t.reshape(8, SUBLANES, SSM_CHUNKS, CH_S).transpose(2, 0, 1, 3)


def _block_diag_in(bb):
    t = bb.reshape(SSM_CHUNKS, 8, SSM_STATE, SSM_GROUP)
    eye = jnp.eye(8, dtype=bb.dtype)
    return jnp.einsum("jgph,gk->jghkp", t, eye).reshape(SSM_CHUNKS, CH_W, CH_S)


def _block_diag_in_t(d):
    t = d.reshape(SSM_CHUNKS, 8, SSM_GROUP, 8, SSM_STATE)
    return jnp.einsum("jghgp->jgph", t).reshape(N_STATES, SSM_GROUP)


def _block_diag_out(c):
    t = c.reshape(SSM_CHUNKS, 8, SSM_GROUP, SSM_STATE)
    eye = jnp.eye(8, dtype=c.dtype)
    return jnp.einsum("jghp,gk->jgpkh", t, eye).reshape(SSM_CHUNKS, CH_S, CH_W)


def _block_diag_out_t(d):
    t = d.reshape(SSM_CHUNKS, 8, SSM_STATE, 8, SSM_GROUP)
    return jnp.einsum("jgpgh->jghp", t).reshape(SSM_GROUPS, SSM_GROUP, SSM_STATE)


def _head_ones():
    r = jnp.arange(ATTN_WIDTH) // HEAD_DIM
    return jnp.where(r[:, None] == r[None, :], 1.0 / HEAD_DIM, 0.0).astype(BF16)


def _in_proj(h, g1, w_in_all, layer, qg, kg):
    s = h.shape[0]
    tm = _row_tile(s, 256)

    def body(h_ref, g_ref, w_ref, qg_ref, kg_ref, ones_ref, proj_ref, qkv_ref):
        x = h_ref[...]
        r = lax.rsqrt(jnp.mean(x * x, axis=-1, keepdims=True) + RMS_EPS)
        hn = (x * r * g_ref[...]).astype(BF16)
        for sh in range(N_CHIPS):
            proj_ref[:, IN_SHARD * sh:IN_SHARD * (sh + 1)] = _dot(hn, w_ref[sh])
        ones = ones_ref[...]
        q = proj_ref[:, 1024:1536]
        k = proj_ref[:, 1536:2048]
        rq = lax.rsqrt(_dot_hilo(q * q, ones) + RMS_EPS)
        rk = lax.rsqrt(_dot_hilo(k * k, ones) + RMS_EPS)
        qkv_ref[:, 0:512] = (q * rq * qg_ref[...] * ATTN_SCALE).astype(BF16)
        qkv_ref[:, 512:1024] = (k * rk * kg_ref[...]).astype(BF16)
        qkv_ref[:, 1024:1536] = proj_ref[:, 2048:2560].astype(BF16)

    full = lambda shape: pl.BlockSpec(shape, lambda i: (0,) * len(shape))
    return pl.pallas_call(
        body, name="in_proj", grid=(s // tm,),
        in_specs=[pl.BlockSpec((tm, D_MODEL), lambda i: (i, 0)), full((1, D_MODEL)),
                  pl.BlockSpec((N_CHIPS, None, D_MODEL, IN_SHARD), lambda i: (0, layer, 0, 0)),
                  full((1, ATTN_WIDTH)), full((1, ATTN_WIDTH)), full((ATTN_WIDTH, ATTN_WIDTH))],
        out_specs=[pl.BlockSpec((tm, IN_COLS), lambda i: (i, 0)), pl.BlockSpec((tm, 3 * ATTN_WIDTH), lambda i: (i, 0))],
        out_shape=[jax.ShapeDtypeStruct((s, IN_COLS), F32), jax.ShapeDtypeStruct((s, 3 * ATTN_WIDTH), BF16)],
        compiler_params=_cparams(1),
    )(h, g1, w_in_all, qg, kg, _head_ones())


def _scan_rows(x_ref, coef_ref, carry_ref, n_blocks, reverse, extra=None):
    c = [coef_ref[a] for a in range(8)]
    shifts = (7, 6, 4) if reverse else (1, 2, 4)
    edge = 0 if reverse else SUBLANES - 1

    def blk(b, carry):
        bb = (n_blocks - 1 - b) if reverse else b
        r0 = pl.multiple_of(bb * SUBLANES, SUBLANES)
        xr = x_ref[pl.ds(r0, SUBLANES), 0:CH_S]
        xi = x_ref[pl.ds(r0, SUBLANES), CH_S:2 * CH_S]
        for lvl, sh in enumerate(shifts):
            ar, ai = c[2 * lvl], c[2 * lvl + 1]
            sr = pltpu.roll(xr, sh, 0)
            si = pltpu.roll(xi, sh, 0)
            xr, xi = xr + (ar * sr - ai * si), xi + (ar * si + ai * sr)
        cr, ci = carry
        xr, xi = xr + (c[6] * cr - c[7] * ci), xi + (c[6] * ci + c[7] * cr)
        x_ref[pl.ds(r0, SUBLANES), 0:CH_S] = xr
        x_ref[pl.ds(r0, SUBLANES), CH_S:2 * CH_S] = xi
        if extra is not None:
            extra(r0, xr, xi, cr, ci)
        return (jnp.broadcast_to(xr[edge:edge + 1, :], (SUBLANES, CH_S)),
                jnp.broadcast_to(xi[edge:edge + 1, :], (SUBLANES, CH_S)))

    cr, ci = lax.fori_loop(0, n_blocks, blk, (carry_ref[:, 0:CH_S], carry_ref[:, CH_S:2 * CH_S]))
    carry_ref[:, 0:CH_S] = cr
    carry_ref[:, CH_S:2 * CH_S] = ci


def _ssm_scan_fwd(proj, wb, coef, wc):
    s = proj.shape[0]
    tm = _row_tile(s, 512)

    def body(u_ref, wb_ref, coef_ref, wc_ref, xs_ref, y_ref, carry_ref):
        @pl.when(pl.program_id(1) == 0)
        def _():
            carry_ref[...] = jnp.zeros_like(carry_ref)

        xs_ref[...] = _dot(u_ref[...].astype(BF16), wb_ref[...])
        _scan_rows(xs_ref, coef_ref, carry_ref, tm // SUBLANES, reverse=False)
        y_ref[...] = _dot(xs_ref[...].astype(BF16), wc_ref[...])

    return pl.pallas_call(
        body, name="ssm_scan", grid=(SSM_CHUNKS, s // tm),
        in_specs=[pl.BlockSpec((tm, CH_W), lambda j, i: (i, j)),
                  pl.BlockSpec((None, CH_W, 2 * CH_S), lambda j, i: (j, 0, 0)),
                  pl.BlockSpec((None, 8, SUBLANES, CH_S), lambda j, i: (j, 0, 0, 0)),
                  pl.BlockSpec((None, 2 * CH_S, CH_W), lambda j, i: (j, 0, 0))],
        out_specs=[pl.BlockSpec((None, tm, 2 * CH_S), lambda j, i: (j, i, 0)),
                   pl.BlockSpec((tm, CH_W), lambda j, i: (i, j))],
        out_shape=[jax.ShapeDtypeStruct((SSM_CHUNKS, s, 2 * CH_S), F32), jax.ShapeDtypeStruct((s, SSM_WIDTH), F32)],
        scratch_shapes=[pltpu.VMEM((SUBLANES, 2 * CH_S), F32)],
        compiler_params=_cparams(2),
    )(proj, wb, coef, wc)


def _glu_forward(y, u, d, wg_ref, bg):
    yf = y + d * u
    z = _gelu(yf)
    zb = z.astype(BF16)
    zz = jnp.concatenate([_dot(zb, wg_ref[sh]) for sh in range(N_CHIPS)], axis=-1) + bg
    return yf, z, zz[:, 0:SSM_WIDTH], zz[:, SSM_WIDTH:2 * SSM_WIDTH]


def _ssm_glu_fwd(y, proj, d, w_glu_all, layer, b_glu):
    s = y.shape[0]
    tm = _row_tile(s, 512)

    def body(y_ref, u_ref, gs_ref, d_ref, wg_ref, bg_ref, o_ref):
        _, _, val, gate = _glu_forward(y_ref[...], u_ref[...], d_ref[...], wg_ref, bg_ref[...])
        gs = gs_ref[...]
        o_ref[...] = val * _sigmoid(gate) * (gs * _sigmoid(gs))

    row = lambda i: (i, 0)
    return pl.pallas_call(
        body, name="ssm_glu", grid=(s // tm,),
        in_specs=[pl.BlockSpec((tm, SSM_WIDTH), row), pl.BlockSpec((tm, SSM_WIDTH), row),
                  pl.BlockSpec((tm, SSM_WIDTH), lambda i: (i, 1)), pl.BlockSpec((1, SSM_WIDTH), lambda i: (0, 0)),
                  pl.BlockSpec((N_CHIPS, None, SSM_WIDTH, ROW_SHARD), lambda i: (0, layer, 0, 0)),
                  pl.BlockSpec((1, 2 * SSM_WIDTH), lambda i: (0, 0))],
        out_specs=pl.BlockSpec((tm, SSM_WIDTH), row),
        out_shape=jax.ShapeDtypeStruct((s, SSM_WIDTH), F32),
        compiler_params=_cparams(1),
    )(y, proj, proj, d, w_glu_all, b_glu)


def _tri(kind):
    r = jnp.arange(ATTN_BLOCK)
    if kind == "suffix_incl":
        m = r[:, None] >= r[None, :]
    else:
        m = r[:, None] < r[None, :]
    return m.astype(BF16)


def _head_masks():
    lane = lax.broadcasted_iota(jnp.int32, (1, 2 * HEAD_DIM), 1)
    return [lane < HEAD_DIM, lane >= HEAD_DIM]


def _stick_block(qh, kb, lsum, tri_ref, strict):
    z = _dot_nt(qh, kb)
    sp = jnp.maximum(z, 0.0) + jnp.log1p(jnp.exp(-jnp.abs(z)))
    ls = -sp
    if strict is not None:
        ls = jnp.where(strict, ls, 0.0)
    incl = _dot_hilo(ls, tri_ref[...])
    w = jnp.exp((z - sp) + (incl - ls) + lsum)
    if strict is not None:
        w = jnp.where(strict, w, 0.0)
    return z, w, lsum + incl[:, 0:1]


def _attn_fwd(qkv, proj):
    s = qkv.shape[0]
    nq = s // ATTN_BLOCK
    tb = ATTN_BLOCK

    def body(q_ref, k_ref, v_ref, g_ref, tri_ref, o_ref, ya_ref):
        i = pl.program_id(1)
        q = q_ref[...]
        row = lax.broadcasted_iota(jnp.int32, (tb, tb), 0)
        col = lax.broadcasted_iota(jnp.int32, (tb, tb), 1)
        strict = col < row
        acc = jnp.zeros((tb, 2 * HEAD_DIM), F32)
        for mask in _head_masks():
            qh = jnp.where(mask, q, jnp.zeros_like(q))

            def block(j, lsum, acc, strict_mask):
                r0 = pl.multiple_of(j * tb, tb)
                kb = k_ref[pl.ds(r0, tb), :]
                vb = v_ref[pl.ds(r0, tb), :]
                vb = jnp.where(mask, vb, jnp.zeros_like(vb))
                _, w, lsum = _stick_block(qh, kb, lsum, tri_ref, strict_mask)
                return lsum, acc + _dot(w.astype(BF16), vb)

            lsum, acc = block(i, jnp.zeros((tb, 1), F32), acc, strict)

            def cond(carry):
                j, lsum, _ = carry
                return jnp.logical_and(j >= 0, jnp.max(lsum) > EXP_ZERO)

            def step(carry):
                j, lsum, acc = carry
                lsum, acc = block(j, lsum, acc, None)
                return j - 1, lsum, acc

            _, _, acc = lax.while_loop(cond, step, (i - 1, lsum, acc))
        o_ref[...] = acc
        g = g_ref[...]
        ya_ref[...] = acc * (g * _sigmoid(g))

    hp_blk = lambda off: pl.BlockSpec((tb, 2 * HEAD_DIM), lambda hp, i: (i, off + hp))
    res = lambda off: pl.BlockSpec((s, 2 * HEAD_DIM), lambda hp, i: (0, off + hp))
    return pl.pallas_call(
        body, name="attn_fwd", grid=(ATTN_WIDTH // (2 * HEAD_DIM), nq),
        in_specs=[hp_blk(0), res(4), res(8), hp_blk(20), pl.BlockSpec((tb, tb), lambda hp, i: (0, 0))],
        out_specs=[hp_blk(0), hp_blk(0)],
        out_shape=[jax.ShapeDtypeStruct((s, ATTN_WIDTH), F32)] * 2,
        compiler_params=_cparams(2),
    )(qkv, qkv, qkv, proj, _tri("suffix_incl"))


def _rms_rows(x, g):
    r = lax.rsqrt(jnp.mean(x * x, axis=-1, keepdims=True) + RMS_EPS)
    return r, x * r * g


def _ple_forward(h1, p, g2, wpg_ref, wpp_ref):
    r2, hn2 = _rms_rows(h1, g2)
    hb = hn2.astype(BF16)
    gpre = _dot(hb[:, 0:ROW_SHARD], wpg_ref[0])
    for sh in range(1, N_CHIPS):
        gpre = gpre + _dot(hb[:, ROW_SHARD * sh:ROW_SHARD * (sh + 1)], wpg_ref[sh])
    gate = _sigmoid(gpre)
    pb = p.astype(BF16)
    pp = jnp.concatenate([_dot(pb, wpp_ref[sh]) for sh in range(N_CHIPS)], axis=-1)
    return r2, hb, gate, pp


def _out_ple(h, ys, ya, p, g2, w_out_all, w_pg_all, w_pp_all, layer):
    s = h.shape[0]
    tm = _row_tile(s, 256)

    def body(h_ref, ys_ref, ya_ref, p_ref, g_ref, wo_ref, wpg_ref, wpp_ref, h1_ref, h2_ref):
        ysb = ys_ref[...].astype(BF16)
        yab = ya_ref[...].astype(BF16)
        h1 = h_ref[...]
        for sh, src in enumerate((ysb[:, 0:ROW_SHARD], ysb[:, ROW_SHARD:], yab[:, 0:ROW_SHARD], yab[:, ROW_SHARD:])):
            h1 = h1 + _dot(src, wo_ref[sh])
        _, _, gate, pp = _ple_forward(h1, p_ref[...], g_ref[...], wpg_ref, wpp_ref)
        h1_ref[...] = h1
        h2_ref[...] = h1 + gate * pp

    row = lambda i: (i, 0)
    wspec = lambda r, cdim: pl.BlockSpec((N_CHIPS, None, r, cdim), lambda i: (0, layer, 0, 0))
    return pl.pallas_call(
        body, name="out_ple", grid=(s // tm,),
        in_specs=[pl.BlockSpec((tm, D_MODEL), row), pl.BlockSpec((tm, SSM_WIDTH), row), pl.BlockSpec((tm, ATTN_WIDTH), row),
                  pl.BlockSpec((tm, PLE_DIM), row), pl.BlockSpec((1, D_MODEL), lambda i: (0, 0)),
                  wspec(ROW_SHARD, D_MODEL), wspec(ROW_SHARD, D_MODEL), wspec(PLE_DIM, ROW_SHARD)],
        out_specs=[pl.BlockSpec((tm, D_MODEL), row)] * 2,
        out_shape=[jax.ShapeDtypeStruct((s, D_MODEL), F32)] * 2,
        compiler_params=_cparams(1),
    )(h, ys, ya, p, g2, w_out_all, w_pg_all, w_pp_all)


def _loss_grad(y, target):
    s = y.shape[0]
    tm = _row_tile(s, 512)

    def body(y_ref, t_ref, dy_ref, acc_ref):
        @pl.when(pl.program_id(0) == 0)
        def _():
            acc_ref[...] = jnp.zeros_like(acc_ref)

        e = y_ref[...] - t_ref[...]
        dy_ref[...] = e / D_MODEL
        sq = (e * e).reshape(tm // SUBLANES, SUBLANES, D_MODEL).sum(axis=0)
        part = sq[:, 0:128]
        for b in range(1, D_MODEL // 128):
            part = part + sq[:, 128 * b:128 * (b + 1)]
        acc_ref[...] += part

    row = lambda i: (i, 0)
    return pl.pallas_call(
        body, name="loss_grad", grid=(s // tm,),
        in_specs=[pl.BlockSpec((tm, D_MODEL), row)] * 2,
        out_specs=[pl.BlockSpec((tm, D_MODEL), row), pl.BlockSpec((SUBLANES, 128), lambda i: (0, 0))],
        out_shape=[jax.ShapeDtypeStruct((s, D_MODEL), F32), jax.ShapeDtypeStruct((SUBLANES, 128), F32)],
        compiler_params=_cparams(1),
    )(y, target)


def _rms_bwd(x, r, g, dy):
    gdy = g * dy
    dx = r * gdy - x * (r * r * r) * jnp.mean(x * gdy, axis=-1, keepdims=True)
    return dx, x * r * dy


def _colsum8(a):
    t = a.shape[0]
    return a.reshape(t // SUBLANES, SUBLANES, a.shape[1]).sum(axis=0)


def _out_ple_bwd(dh2, h1, p, g2, w_out_all, w_pg_all, w_pp_all, layer):
    s = h1.shape[0]
    tm = _row_tile(s, 256)

    def body(dh2_ref, h1_ref, p_ref, g_ref, wo_ref, wpg_ref, wpp_ref,
             dh1_ref, dmix_ref, hn_ref, dgp_ref, dpp_ref, dh1b_ref, dg_ref):
        @pl.when(pl.program_id(0) == 0)
        def _():
            dg_ref[...] = jnp.zeros_like(dg_ref)

        h1 = h1_ref[...]
        dh2 = dh2_ref[...]
        g2v = g_ref[...]
        r2, hb, gate, pp = _ple_forward(h1, p_ref[...], g2v, wpg_ref, wpp_ref)
        dgp = (dh2 * pp) * gate * (1.0 - gate)
        dgpb = dgp.astype(BF16)
        dhn = jnp.concatenate([_dot_nt(dgpb, wpg_ref[sh]) for sh in range(N_CHIPS)], axis=-1)
        dx, dgrow = _rms_bwd(h1, r2, g2v, dhn)
        dh1 = dh2 + dx
        dh1b = dh1.astype(BF16)
        dh1_ref[...] = dh1
        dh1b_ref[...] = dh1b
        hn_ref[...] = hb
        dgp_ref[...] = dgpb
        dpp_ref[...] = (dh2 * gate).astype(BF16)
        dg_ref[...] += _colsum8(dgrow)
        for sh in range(N_CHIPS):
            dmix_ref[:, ROW_SHARD * sh:ROW_SHARD * (sh + 1)] = _dot_nt(dh1b, wo_ref[sh])

    row = lambda i: (i, 0)
    wspec = lambda r, cdim: pl.BlockSpec((N_CHIPS, None, r, cdim), lambda i: (0, layer, 0, 0))
    big = pl.BlockSpec((tm, D_MODEL), row)
    return pl.pallas_call(
        body, name="out_ple_bwd", grid=(s // tm,),
        in_specs=[big, big, pl.BlockSpec((tm, PLE_DIM), row), pl.BlockSpec((1, D_MODEL), lambda i: (0, 0)),
                  wspec(ROW_SHARD, D_MODEL), wspec(ROW_SHARD, D_MODEL), wspec(PLE_DIM, ROW_SHARD)],
        out_specs=[big] * 6 + [pl.BlockSpec((SUBLANES, D_MODEL), lambda i: (0, 0))],
        out_shape=[jax.ShapeDtypeStruct((s, D_MODEL), F32)] * 2 + [jax.ShapeDtypeStruct((s, D_MODEL), BF16)] * 4
        + [jax.ShapeDtypeStruct((SUBLANES, D_MODEL), F32)],
        compiler_params=_cparams(1),
    )(dh2, h1, p, g2, w_out_all, w_pg_all, w_pp_all)


def _tn_matmul(a, b, n_blocks, block_a, name):
    s = a.shape[0]
    tk = _row_tile(s, 512)
    ka, nb = a.shape[1], b.shape[1]
    if block_a:
        ka //= n_blocks
    else:
        nb //= n_blocks

    def body(a_ref, b_ref, o_ref):
        @pl.when(pl.program_id(1) == 0)
        def _():
            o_ref[...] = jnp.zeros_like(o_ref)

        o_ref[...] += _dot_tn(a_ref[...].astype(BF16), b_ref[...].astype(BF16))

    a_map = (lambda sh, i: (i, sh)) if block_a else (lambda sh, i: (i, 0))
    b_map = (lambda sh, i: (i, 0)) if block_a else (lambda sh, i: (i, sh))
    return pl.pallas_call(
        body, name=name, grid=(n_blocks, s // tk),
        in_specs=[pl.BlockSpec((tk, ka), a_map), pl.BlockSpec((tk, nb), b_map)],
        out_specs=pl.BlockSpec((None, ka, nb), lambda sh, i: (sh, 0, 0)),
        out_shape=jax.ShapeDtypeStruct((n_blocks, ka, nb), F32),
        compiler_params=_cparams(2),
    )(a, b)


def _attn_bwd(qkv, o, proj, dmix):
    s = qkv.shape[0]
    nq = s // ATTN_BLOCK
    tb = ATTN_BLOCK

    def body(q_ref, k_ref, v_ref, o_ref, g_ref, dya_ref, tri_s_ref, tri_p_ref,
             dq_ref, dk_ref, dv_ref, dg_ref, z_scr, w_scr):
        i = pl.program_id(1)

        @pl.when(i == 0)
        def _():
            dk_ref[...] = jnp.zeros_like(dk_ref)
            dv_ref[...] = jnp.zeros_like(dv_ref)

        g = g_ref[...]
        sg = _sigmoid(g)
        dya = dya_ref[...]
        do = dya * (g * sg)
        dg_ref[...] = dya * o_ref[...] * (sg * (1.0 + g * (1.0 - sg)))
        q = q_ref[...]
        row = lax.broadcasted_iota(jnp.int32, (tb, tb), 0)
        col = lax.broadcasted_iota(jnp.int32, (tb, tb), 1)
        strict = col < row
        dq_acc = jnp.zeros((tb, 2 * HEAD_DIM), F32)
        for mask in _head_masks():
            qh = jnp.where(mask, q, jnp.zeros_like(q))
            doh = jnp.where(mask, do, 0.0).astype(BF16)

            def sweep(j, lsum, strict_mask):
                r0 = pl.multiple_of(j * tb, tb)
                z, w, lsum = _stick_block(qh, k_ref[pl.ds(r0, tb), :], lsum, tri_s_ref, strict_mask)
                z_scr[i - j] = z
                w_scr[i - j] = w
                return lsum

            lsum = sweep(i, jnp.zeros((tb, 1), F32), strict)

            def cond(carry):
                j, lsum = carry
                return jnp.logical_and(j >= 0, jnp.max(lsum) > EXP_ZERO)

            def step(carry):
                j, lsum = carry
                return j - 1, sweep(j, lsum, None)

            j_end, _ = lax.while_loop(cond, step, (i - 1, lsum))

            def back(j, carry):
                gsum, dq_acc = carry
                r0 = pl.multiple_of(j * tb, tb)
                kb = k_ref[pl.ds(r0, tb), :]
                kb = jnp.where(mask, kb, jnp.zeros_like(kb))
                z = z_scr[i - j]
                w = w_scr[i - j]
                gw = w * _dot_nt(doh, v_ref[pl.ds(r0, tb), :])
                pre = _dot_hilo(gw, tri_p_ref[...])
                sig = _sigmoid(z)
                dz = gw - (gw + pre + gsum) * sig
                dz = jnp.where(col < row + (i - j) * tb, dz, 0.0)
                dzb = dz.astype(BF16)
                dk_ref[pl.ds(r0, tb), :] += _dot_tn(dzb, qh)
                dv_ref[pl.ds(r0, tb), :] += _dot_tn(w.astype(BF16), doh)
                gsum = gsum + pre[:, tb - 1:tb] + gw[:, tb - 1:tb]
                return gsum, dq_acc + _dot(dzb, kb)

            _, dq_acc = lax.fori_loop(j_end + 1, i + 1, back, (jnp.zeros((tb, 1), F32), dq_acc))
        dq_ref[...] = dq_acc

    hp_blk = lambda off: pl.BlockSpec((tb, 2 * HEAD_DIM), lambda hp, i: (i, off + hp))
    res = lambda off: pl.BlockSpec((s, 2 * HEAD_DIM), lambda hp, i: (0, off + hp))
    tri = pl.BlockSpec((tb, tb), lambda hp, i: (0, 0))
    return pl.pallas_call(
        body, name="attn_bwd", grid=(ATTN_WIDTH // (2 * HEAD_DIM), nq),
        in_specs=[hp_blk(0), res(4), res(8), hp_blk(0), hp_blk(20), hp_blk(4), tri, tri],
        out_specs=[hp_blk(0), res(0), res(0), hp_blk(0)],
        out_shape=[jax.ShapeDtypeStruct((s, ATTN_WIDTH), F32)] * 4,
        scratch_shapes=[pltpu.VMEM((nq, tb, tb), F32), pltpu.VMEM((nq, tb, tb), F32)],
        compiler_params=_cparams(2),
    )(qkv, qkv, qkv, o, proj, dmix, _tri("suffix_incl"), _tri("prefix_strict"))


def _ssm_glu_bwd(dmix, y, proj, d, w_glu_all, layer, b_glu):
    s = y.shape[0]
    tm = _row_tile(s, 512)

    def body(dys_ref, y_ref, u_ref, gs_ref, d_ref, wg_ref, bg_ref,
             dyf_ref, du_ref, dgs_ref, z_ref, dzz_ref, dd_ref, db_ref):
        @pl.when(pl.program_id(0) == 0)
        def _():
            dd_ref[...] = jnp.zeros_like(dd_ref)
            db_ref[...] = jnp.zeros_like(db_ref)

        u = u_ref[...]
        dv = d_ref[...]
        yf, z, val, gate = _glu_forward(y_ref[...], u, dv, wg_ref, bg_ref[...])
        gs = gs_ref[...]
        sgs = _sigmoid(gs)
        sgate = _sigmoid(gate)
        dys = dys_ref[...]
        dgv = dys * (gs * sgs)
        dgs_ref[...] = dys * (val * sgate) * (sgs * (1.0 + gs * (1.0 - sgs)))
        dzz = jnp.concatenate([dgv * sgate, dgv * val * sgate * (1.0 - sgate)], axis=-1)
        dzzb = dzz.astype(BF16)
        dz = _dot_nt(dzzb[:, 0:ROW_SHARD], wg_ref[0])
        for sh in range(1, N_CHIPS):
            dz = dz + _dot_nt(dzzb[:, ROW_SHARD * sh:ROW_SHARD * (sh + 1)], wg_ref[sh])
        dyf = dz * _gelu_grad(yf)
        dyf_ref[...] = dyf
        du_ref[...] = dyf * dv
        z_ref[...] = z.astype(BF16)
        dzz_ref[...] = dzzb
        dd_ref[...] += _colsum8(dyf * u)
        db_ref[...] += _colsum8(dzz)

    row = lambda i: (i, 0)
    half = pl.BlockSpec((tm, SSM_WIDTH), row)
    return pl.pallas_call(
        body, name="ssm_glu_bwd", grid=(s // tm,),
        in_specs=[half, half, half, pl.BlockSpec((tm, SSM_WIDTH), lambda i: (i, 1)),
                  pl.BlockSpec((1, SSM_WIDTH), lambda i: (0, 0)),
                  pl.BlockSpec((N_CHIPS, None, SSM_WIDTH, ROW_SHARD), lambda i: (0, layer, 0, 0)),
                  pl.BlockSpec((1, 2 * SSM_WIDTH), lambda i: (0, 0))],
        out_specs=[half, half, half, half, pl.BlockSpec((tm, 2 * SSM_WIDTH), row),
                   pl.BlockSpec((SUBLANES, SSM_WIDTH), lambda i: (0, 0)),
                   pl.BlockSpec((SUBLANES, 2 * SSM_WIDTH), lambda i: (0, 0))],
        out_shape=[jax.ShapeDtypeStruct((s, SSM_WIDTH), F32)] * 3
        + [jax.ShapeDtypeStruct((s, SSM_WIDTH), BF16), jax.ShapeDtypeStruct((s, 2 * SSM_WIDTH), BF16),
           jax.ShapeDtypeStruct((SUBLANES, SSM_WIDTH), F32), jax.ShapeDtypeStruct((SUBLANES, 2 * SSM_WIDTH), F32)],
        compiler_params=_cparams(1),
    )(dmix, y, proj, proj, d, w_glu_all, b_glu)


def _ssm_scan_bwd(dyf, xs, proj, wct, coef_rev, wbt):
    s = dyf.shape[0]
    tm = _row_tile(s, 512)
    nt = s // tm

    def body(dy_ref, xs_ref, u_ref, wct_ref, coef_ref, wbt_ref, du_ref, dwc_ref, dwb_ref, da_ref, lam_ref, carry_ref):
        @pl.when(pl.program_id(1) == 0)
        def _():
            carry_ref[...] = jnp.zeros_like(carry_ref)
            dwc_ref[...] = jnp.zeros_like(dwc_ref)
            dwb_ref[...] = jnp.zeros_like(dwb_ref)
            da_ref[...] = jnp.zeros_like(da_ref)

        dyb = dy_ref[...].astype(BF16)
        lam_ref[...] = _dot(dyb, wct_ref[...])
        rows = lax.broadcasted_iota(jnp.int32, (SUBLANES, CH_S), 0)
        last = rows == SUBLANES - 1

        def extra(r0, lr, li, cr, ci):
            er = jnp.where(last, cr, pltpu.roll(lr, SUBLANES - 1, 0))
            ei = jnp.where(last, ci, pltpu.roll(li, SUBLANES - 1, 0))
            xr = xs_ref[pl.ds(r0, SUBLANES), 0:CH_S]
            xi = xs_ref[pl.ds(r0, SUBLANES), CH_S:2 * CH_S]
            da_ref[:, 0:CH_S] += xr * er + xi * ei
            da_ref[:, CH_S:2 * CH_S] += xr * ei - xi * er

        _scan_rows(lam_ref, coef_ref, carry_ref, tm // SUBLANES, reverse=True, extra=extra)
        lamb = lam_ref[...].astype(BF16)
        du_ref[...] = _dot(lamb, wbt_ref[...])
        dwc_ref[...] += _dot_tn(xs_ref[...].astype(BF16), dyb)
        dwb_ref[...] += _dot_tn(u_ref[...].astype(BF16), lamb)

    rev = lambda j, i: (nt - 1 - i, j)
    return pl.pallas_call(
        body, name="ssm_scan_bwd", grid=(SSM_CHUNKS, nt),
        in_specs=[pl.BlockSpec((tm, CH_W), rev),
                  pl.BlockSpec((None, tm, 2 * CH_S), lambda j, i: (j, nt - 1 - i, 0)),
                  pl.BlockSpec((tm, CH_W), rev),
                  pl.BlockSpec((None, CH_W, 2 * CH_S), lambda j, i: (j, 0, 0)),
                  pl.BlockSpec((None, 8, SUBLANES, CH_S), lambda j, i: (j, 0, 0, 0)),
                  pl.BlockSpec((None, 2 * CH_S, CH_W), lambda j, i: (j, 0, 0))],
        out_specs=[pl.BlockSpec((tm, CH_W), rev),
                   pl.BlockSpec((None, 2 * CH_S, CH_W), lambda j, i: (j, 0, 0)),
                   pl.BlockSpec((None, CH_W, 2 * CH_S), lambda j, i: (j, 0, 0)),
                   pl.BlockSpec((None, SUBLANES, 2 * CH_S), lambda j, i: (j, 0, 0))],
        out_shape=[jax.ShapeDtypeStruct((s, SSM_WIDTH), F32),
                   jax.ShapeDtypeStruct((SSM_CHUNKS, 2 * CH_S, CH_W), F32),
                   jax.ShapeDtypeStruct((SSM_CHUNKS, CH_W, 2 * CH_S), F32),
                   jax.ShapeDtypeStruct((SSM_CHUNKS, SUBLANES, 2 * CH_S), F32)],
        scratch_shapes=[pltpu.VMEM((tm, 2 * CH_S), F32), pltpu.VMEM((SUBLANES, 2 * CH_S), F32)],
        compiler_params=_cparams(2),
    )(dyf, xs, proj, wct, coef_rev, wbt)


def _in_proj_bwd(h, g1, w_in_all, layer, qg, kg, proj, du_a, du_b, dgs, dq, dk, dv, dga, dh1):
    s = h.shape[0]
    tm = _row_tile(s, 256)

    def body(h_ref, g_ref, w_ref, qg_ref, kg_ref, ones_ref, q_ref, k_ref, dua_ref, dub_ref, dgs_ref, dq_ref, dk_ref,
             dv_ref, dga_ref, dh1_ref, dh_ref, hn_ref, dp_ref, dg1_ref, dqg_ref, dkg_ref):
        @pl.when(pl.program_id(0) == 0)
        def _():
            dg1_ref[...] = jnp.zeros_like(dg1_ref)
            dqg_ref[...] = jnp.zeros_like(dqg_ref)
            dkg_ref[...] = jnp.zeros_like(dkg_ref)

        ones = ones_ref[...]

        def head_norm_bwd(x, gain, dy):
            r = lax.rsqrt(_dot_hilo(x * x, ones) + RMS_EPS)
            gdy = gain * dy
            dx = r * gdy - x * (r * r * r) * _dot_hilo(x * gdy, ones)
            return dx, x * r * dy

        dqr, dqg_rows = head_norm_bwd(q_ref[...], qg_ref[...], dq_ref[...] * ATTN_SCALE)
        dkr, dkg_rows = head_norm_bwd(k_ref[...], kg_ref[...], dk_ref[...])
        dqg_ref[...] += _colsum8(dqg_rows)
        dkg_ref[...] += _colsum8(dkg_rows)
        dp_ref[:, 0:512] = (dua_ref[...] + dub_ref[...]).astype(BF16)
        dp_ref[:, 512:1024] = dgs_ref[...].astype(BF16)
        dp_ref[:, 1024:1536] = dqr.astype(BF16)
        dp_ref[:, 1536:2048] = dkr.astype(BF16)
        dp_ref[:, 2048:2560] = dv_ref[...].astype(BF16)
        dp_ref[:, 2560:3072] = dga_ref[...].astype(BF16)
        dhn = _dot_nt(dp_ref[:, 0:IN_SHARD], w_ref[0])
        for sh in range(1, N_CHIPS):
            dhn = dhn + _dot_nt(dp_ref[:, IN_SHARD * sh:IN_SHARD * (sh + 1)], w_ref[sh])
        x = h_ref[...]
        gv = g_ref[...]
        r, hn = _rms_rows(x, gv)
        dx, dg_rows = _rms_bwd(x, r, gv, dhn)
        dh_ref[...] = dh1_ref[...] + dx
        hn_ref[...] = hn.astype(BF16)
        dg1_ref[...] += _colsum8(dg_rows)

    row = lambda i: (i, 0)
    full = lambda shape: pl.BlockSpec(shape, lambda i: (0,) * len(shape))
    big = pl.BlockSpec((tm, D_MODEL), row)
    half = pl.BlockSpec((tm, 512), row)
    return pl.pallas_call(
        body, name="in_proj_bwd", grid=(s // tm,),
        in_specs=[big, full((1, D_MODEL)), pl.BlockSpec((N_CHIPS, None, D_MODEL, IN_SHARD), lambda i: (0, layer, 0, 0)),
                  full((1, ATTN_WIDTH)), full((1, ATTN_WIDTH)), full((ATTN_WIDTH, ATTN_WIDTH)),
                  pl.BlockSpec((tm, 512), lambda i: (i, 2)), pl.BlockSpec((tm, 512), lambda i: (i, 3)),
                  half, half, half, half, half, half, half, big],
        out_specs=[big, big, pl.BlockSpec((tm, IN_COLS), row), pl.BlockSpec((SUBLANES, D_MODEL), lambda i: (0, 0)),
                   pl.BlockSpec((SUBLANES, ATTN_WIDTH), lambda i: (0, 0)), pl.BlockSpec((SUBLANES, ATTN_WIDTH), lambda i: (0, 0))],
        out_shape=[jax.ShapeDtypeStruct((s, D_MODEL), F32), jax.ShapeDtypeStruct((s, D_MODEL), BF16),
                   jax.ShapeDtypeStruct((s, IN_COLS), BF16), jax.ShapeDtypeStruct((SUBLANES, D_MODEL), F32),
                   jax.ShapeDtypeStruct((SUBLANES, ATTN_WIDTH), F32), jax.ShapeDtypeStruct((SUBLANES, ATTN_WIDTH), F32)],
        compiler_params=_cparams(1),
    )(h, g1, w_in_all, qg, kg, _head_ones(), proj, proj, du_a, du_b, dgs, dq, dk, dv, dga, dh1)


SMALL_NAMES = ("mix_norm_g", "ssm_a_re", "ssm_a_im", "ssm_log_dt", "ssm_b_re", "ssm_b_im", "ssm_c_re", "ssm_c_im",
               "ssm_d", "ssm_b_glu", "q_norm_g", "k_norm_g", "ple_norm_g")
BIG_NAMES = ("w_in", "ssm_w_glu", "w_out", "w_ple_gate", "w_ple_proj")


def _ssm_setup(sm, layer):
    col = lambda a: a[layer].reshape(N_STATES, 1)
    a_re, a_im = col(sm["ssm_a_re"]), col(sm["ssm_a_im"])
    log_dt = jnp.repeat(sm["ssm_log_dt"][layer], SSM_STATE).reshape(N_STATES, 1)
    b_re = sm["ssm_b_re"][layer].reshape(N_STATES, SSM_GROUP)
    b_im = sm["ssm_b_im"][layer].reshape(N_STATES, SSM_GROUP)
    disc_in = (a_re, a_im, log_dt, b_re, b_im)
    ab_re, ab_im, bb_re, bb_im = _disc_fwd(*disc_in)
    wb = jnp.concatenate([_block_diag_in(bb_re), _block_diag_in(bb_im)], axis=-1)
    wc = jnp.concatenate([_block_diag_out(sm["ssm_c_re"][layer]), -_block_diag_out(sm["ssm_c_im"][layer])], axis=1)
    return dict(disc_in=disc_in, wb=wb.astype(BF16), wbt=wb.transpose(0, 2, 1).astype(BF16),
                wc=wc.astype(BF16), wct=wc.transpose(0, 2, 1).astype(BF16),
                coef=_scan_coefs(ab_re, ab_im, False), coef_rev=_scan_coefs(ab_re, ab_im, True))


def _local_step(x, p, target, sm, wg):
    tile8 = lambda a: jnp.tile(a, ATTN_WIDTH // HEAD_DIM).reshape(1, ATTN_WIDTH)
    saved = []
    h = x
    for l in range(N_LAYERS):
        ssm = _ssm_setup(sm, l)
        g1 = sm["mix_norm_g"][l].reshape(1, D_MODEL)
        g2 = sm["ple_norm_g"][l].reshape(1, D_MODEL)
        qg, kg = tile8(sm["q_norm_g"][l]), tile8(sm["k_norm_g"][l])
        dsk = sm["ssm_d"][l].reshape(1, SSM_WIDTH)
        bgl = sm["ssm_b_glu"][l].reshape(1, 2 * SSM_WIDTH)
        proj, qkv = _in_proj(h, g1, wg["w_in"], l, qg, kg)
        xs, y = _ssm_scan_fwd(proj, ssm["wb"], ssm["coef"], ssm["wc"])
        ys = _ssm_glu_fwd(y, proj, dsk, wg["ssm_w_glu"], l, bgl)
        o, ya = _attn_fwd(qkv, proj)
        h1, h2 = _out_ple(h, ys, ya, p[l], g2, wg["w_out"], wg["w_ple_gate"], wg["w_ple_proj"], l)
        saved.append(dict(ssm=ssm, g1=g1, g2=g2, qg=qg, kg=kg, dsk=dsk, bgl=bgl, h=h, proj=proj, qkv=qkv, xs=xs, y=y,
                          ys=ys, o=o, ya=ya, h1=h1))
        h = h2
    dh, sq = _loss_grad(h, target)
    loss = 0.5 * jnp.sum(sq) / D_MODEL

    gbig = {n: [None] * N_LAYERS for n in BIG_NAMES}
    gsm = {n: [None] * N_LAYERS for n in SMALL_NAMES}
    for l in reversed(range(N_LAYERS)):
        sv = saved[l]
        ssm = sv["ssm"]
        dh1, dmix, hn2b, dgpb, dppb, dh1b, dg2 = _out_ple_bwd(dh, sv["h1"], p[l], sv["g2"], wg["w_out"],
                                                              wg["w_ple_gate"], wg["w_ple_proj"], l)
        gsm["ple_norm_g"][l] = dg2.sum(0)
        gbig["w_ple_proj"][l] = _tn_matmul(p[l], dppb, N_CHIPS, False, "dw_ple_proj")
        gbig["w_ple_gate"][l] = _tn_matmul(hn2b, dgpb, N_CHIPS, True, "dw_ple_gate")
        dwo_s = _tn_matmul(sv["ys"], dh1b, 2, True, "dw_out_ssm")
        dwo_a = _tn_matmul(sv["ya"], dh1b, 2, True, "dw_out_attn")
        gbig["w_out"][l] = jnp.concatenate([dwo_s, dwo_a], axis=0)
        dqs, dkn, dv, dga = _attn_bwd(sv["qkv"], sv["o"], sv["proj"], dmix)
        dyf, du_a, dgs, zb, dzzb, dd, dbg = _ssm_glu_bwd(dmix, sv["y"], sv["proj"], sv["dsk"], wg["ssm_w_glu"], l, sv["bgl"])
        gsm["ssm_d"][l] = dd.sum(0).reshape(SSM_GROUPS, SSM_GROUP)
        gsm["ssm_b_glu"][l] = dbg.sum(0)
        gbig["ssm_w_glu"][l] = _tn_matmul(zb, dzzb, N_CHIPS, False, "dw_glu")
        du_b, dwc, dwb, da = _ssm_scan_bwd(dyf, sv["xs"], sv["proj"], ssm["wct"], ssm["coef_rev"], ssm["wbt"])
        gsm["ssm_c_re"][l] = _block_diag_out_t(dwc[:, 0:CH_S, :])
        gsm["ssm_c_im"][l] = -_block_diag_out_t(dwc[:, CH_S:, :])
        da = da.sum(1)
        g_ab_re = da[:, 0:CH_S].reshape(N_STATES, 1)
        g_ab_im = da[:, CH_S:].reshape(N_STATES, 1)
        g_bb_re = _block_diag_in_t(dwb[:, :, 0:CH_S])
        g_bb_im = _block_diag_in_t(dwb[:, :, CH_S:])
        d_are, d_aim, d_ldt, d_bre, d_bim = _disc_bwd(*ssm["disc_in"], g_ab_re, g_ab_im, g_bb_re, g_bb_im)
        gsm["ssm_a_re"][l] = d_are.reshape(SSM_GROUPS, SSM_STATE)
        gsm["ssm_a_im"][l] = d_aim.reshape(SSM_GROUPS, SSM_STATE)
        gsm["ssm_log_dt"][l] = d_ldt.reshape(SSM_GROUPS, SSM_STATE).sum(1)
        gsm["ssm_b_re"][l] = d_bre.reshape(SSM_GROUPS, SSM_STATE, SSM_GROUP)
        gsm["ssm_b_im"][l] = d_bim.reshape(SSM_GROUPS, SSM_STATE, SSM_GROUP)
        dh, hnb, dprojb, dg1, dqg, dkg = _in_proj_bwd(sv["h"], sv["g1"], wg["w_in"], l, sv["qg"], sv["kg"], sv["proj"],
                                                      du_a, du_b, dgs, dqs, dkn, dv, dga, dh1)
        gsm["mix_norm_g"][l] = dg1.sum(0)
        gsm["q_norm_g"][l] = dqg.sum(0).reshape(-1, HEAD_DIM).sum(0)
        gsm["k_norm_g"][l] = dkg.sum(0).reshape(-1, HEAD_DIM).sum(0)
        gbig["w_in"][l] = _tn_matmul(hnb, dprojb, N_CHIPS, False, "dw_in")
    gbig = {n: jnp.stack(v, 0) for n, v in gbig.items()}
    gsm = {n: jnp.stack(v, 0) for n, v in gsm.items()}
    return loss, dh, gbig, gsm


_SMALL_PAD = 8 * 8 * 128


def _pack_small(d):
    flat = jnp.concatenate([d[n].reshape(-1) for n in SMALL_NAMES])
    n = flat.shape[0]
    padded = -(-n // _SMALL_PAD) * _SMALL_PAD
    return jnp.pad(flat, (0, padded - n))


def _unpack_small(flat, like):
    out, off = {}, 0
    for n in SMALL_NAMES:
        size = like[n].size
        out[n] = flat[off:off + size].reshape(like[n].shape)
        off += size
    return out


def _reduce_grads(parts, wire_dtypes):
    n = len(parts)
    flat = [a.reshape(2, -1, a.shape[-1]) for a in parts]
    recv = _sibling_send_other_half(flat, "grad_sibling_send")
    chip = [_add_half(flat[k], recv[k], wire_dtypes[k], "grad_sibling_add").reshape(parts[k].shape[1:])
            for k in range(n)]
    got = _chip_scatter(chip, "grad_chip_scatter")
    tot = [_sum4(got[k], "grad_chip_sum") for k in range(n)]
    return _sibling_join_halves(tot, "grad_sibling_join")


def kernel(x, p, mix_norm_g, w_in, ssm_a_re, ssm_a_im, ssm_log_dt, ssm_b_re, ssm_b_im, ssm_c_re, ssm_c_im, ssm_d, ssm_w_glu, ssm_b_glu, q_norm_g, k_norm_g, w_out, ple_norm_g, w_ple_gate, w_ple_proj, loss_target, m_mix_norm_g, m_w_in, m_ssm_a_re, m_ssm_a_im, m_ssm_log_dt, m_ssm_b_re, m_ssm_b_im, m_ssm_c_re, m_ssm_c_im, m_ssm_d, m_ssm_w_glu, m_ssm_b_glu, m_q_norm_g, m_k_norm_g, m_w_out, m_ple_norm_g, m_w_ple_gate, m_w_ple_proj, v_mix_norm_g, v_w_in, v_ssm_a_re, v_ssm_a_im, v_ssm_log_dt, v_ssm_b_re, v_ssm_b_im, v_ssm_c_re, v_ssm_c_im, v_ssm_d, v_ssm_w_glu, v_ssm_b_glu, v_q_norm_g, v_k_norm_g, v_w_out, v_ple_norm_g, v_w_ple_gate, v_w_ple_proj):
    args = dict(locals())
    names = ("mix_norm_g", "w_in", "ssm_a_re", "ssm_a_im", "ssm_log_dt", "ssm_b_re", "ssm_b_im", "ssm_c_re", "ssm_c_im",
             "ssm_d", "ssm_w_glu", "ssm_b_glu", "q_norm_g", "k_norm_g", "w_out", "ple_norm_g", "w_ple_gate", "w_ple_proj")
    w = {n: args[n] for n in names}
    m = {n: args["m_" + n] for n in names}
    v = {n: args["v_" + n] for n in names}

    gathered = _chip_gather([w[n].astype(BF16) for n in BIG_NAMES], "weight_gather")
    wg = dict(zip(BIG_NAMES, gathered))
    sm = {n: w[n] for n in SMALL_NAMES}
    loss, dx, gbig, gsm = _local_step(x[0], p[:, 0], loss_target[0], sm, wg)
    loss = lax.psum(loss, ("x", "y", "c"))

    small = _pack_small(gsm)
    parts = [gbig[n] for n in BIG_NAMES] + [small.reshape(2, N_CHIPS, SUBLANES, -1)]
    red = _reduce_grads(parts, [BF16] * len(BIG_NAMES) + [F32])
    small_mine = red[-1]
    small_all = _chip_gather([small_mine], "small_grad_gather")[0]
    small_tot = small_all.transpose(1, 0, 2, 3).reshape(-1)
    g = dict(zip(BIG_NAMES, [r.reshape(w[n].shape) for r, n in zip(red[:-1], BIG_NAMES)]))
    g.update(_unpack_small(small_tot, sm))

    delta, new_m, new_v = {}, {}, {}
    for n in BIG_NAMES:
        lanes = w[n].shape[-1]
        outs = _adamw(_as_rows(w[n], lanes), _as_rows(g[n], lanes), _as_rows(m[n], lanes), _as_rows(v[n], lanes), "adamw_" + n)
        delta[n], new_m[n], new_v[n] = [o.reshape(w[n].shape) for o in outs]
    packed = [_pack_small(d).reshape(-1, 1024) for d in (sm, {n: g[n] for n in SMALL_NAMES},
                                                         {n: m[n] for n in SMALL_NAMES}, {n: v[n] for n in SMALL_NAMES})]
    outs = _adamw(*packed, "adamw_small")
    for d, o in zip((delta, new_m, new_v), outs):
        d.update(_unpack_small(o.reshape(-1), sm))

    return (loss, dx[None], *[g[n] for n in names], *[delta[n] for n in names],
            *[new_m[n] for n in names], *[new_v[n] for n in names])
```

```python
import functools
import math

import jax
import jax.numpy as jnp
from jax import lax
from jax.experimental import pallas as pl
from jax.experimental.pallas import tpu as pltpu

F32 = jnp.float32
BF16 = jnp.bfloat16

D_MODEL = 1024
N_LAYERS = 2
N_CHIPS = 4
IN_COLS = 3072
IN_SHARD = IN_COLS // N_CHIPS
SSM_WIDTH = 512
SSM_GROUP = 16
SSM_GROUPS = 32
SSM_STATE = 64
N_STATES = SSM_GROUPS * SSM_STATE
SSM_CHUNKS = 4
CH_W = SSM_WIDTH // SSM_CHUNKS
CH_S = N_STATES // SSM_CHUNKS
ATTN_WIDTH = 512
HEAD_DIM = 64
PLE_DIM = 256
ROW_SHARD = 256
RMS_EPS = 1e-6
ATTN_SCALE = HEAD_DIM ** -0.5
ATTN_BLOCK = 128
EXP_ZERO = -104.0
SUBLANES = 8
V7X_VMEM_LIMIT = 52 * 1024 * 1024

ADAM_LR = 0.001
ADAM_B1 = 0.9
ADAM_B2 = 0.999
ADAM_EPS = 1e-08
ADAM_WD = 0.01
ADAM_STEP = 10

MESH = pl.DeviceIdType.MESH
ANY = pl.BlockSpec(memory_space=pl.ANY)


def _cparams(n_grid=0, parallel=0):
    sem = tuple(["parallel"] * parallel + ["arbitrary"] * (n_grid - parallel))
    return pltpu.CompilerParams(dimension_semantics=sem, vmem_limit_bytes=V7X_VMEM_LIMIT)


def _dot(a, b):
    return jnp.dot(a, b, preferred_element_type=F32)


def _dot_nt(a, b):
    return lax.dot_general(a, b, (((1,), (1,)), ((), ())), preferred_element_type=F32)


def _dot_tn(a, b):
    return lax.dot_general(a, b, (((0,), (0,)), ((), ())), preferred_element_type=F32)


def _split_hilo(a):
    hi = a.astype(BF16)
    lo = (a - hi.astype(F32)).astype(BF16)
    return hi, lo


def _dot_hilo(a, b):
    hi, lo = _split_hilo(a)
    return _dot(hi, b) + _dot(lo, b)


def _sigmoid(x):
    return 0.5 * (jnp.tanh(0.5 * x) + 1.0)


_GELU_C = math.sqrt(2.0 / math.pi)


def _gelu(x):
    return 0.5 * x * (1.0 + jnp.tanh(_GELU_C * (x + 0.044715 * (x * x * x))))


def _gelu_grad(x):
    t = jnp.tanh(_GELU_C * (x + 0.044715 * (x * x * x)))
    return 0.5 * (1.0 + t) + 0.5 * x * (1.0 - t * t) * (_GELU_C * (1.0 + 3.0 * 0.044715 * (x * x)))


def _row_tile(s, want):
    for t in range(min(s, want), 7, -1):
        if s % t == 0 and t % SUBLANES == 0:
            return t
    return s


def _coords():
    return lax.axis_index("x"), lax.axis_index("y"), lax.axis_index("c")


def _other_chips(x, y):
    return [(1 - x, y), (x, 1 - y), (1 - x, 1 - y)]


def _remote(src, dst, send_sem, recv_sem, dev):
    return pltpu.make_async_remote_copy(src_ref=src, dst_ref=dst, send_sem=send_sem, recv_sem=recv_sem,
                                        device_id=dev, device_id_type=MESH)


def _chip_gather(arrs, name):
    n = len(arrs)

    def body(*refs):
        ins, outs = refs[:n], refs[n:2 * n]
        send_sems, recv_sems, fwd_send, fwd_recv, loc_sems = refs[2 * n:]
        x, y, c = _coords()
        me_chip = 2 * x + y
        chips = _other_chips(x, y)
        sibling = (x, y, 1 - c)
        local = [pltpu.make_async_copy(ins[k], outs[k].at[me_chip], loc_sems.at[k]) for k in range(n)]
        for cp in local:
            cp.start()
        first = []
        for k in range(n):
            for j, (cx, cy) in enumerate(chips):
                cp = _remote(ins[k].at[c], outs[k].at[me_chip, c], send_sems.at[3 * k + j], recv_sems.at[3 * k + j],
                             (cx, cy, c))
                cp.start()
                first.append(cp)
        passed = []
        for k in range(n):
            for j, (cx, cy) in enumerate(chips):
                blk = outs[k].at[2 * cx + cy, c]
                _remote(blk, blk, send_sems.at[3 * k + j], recv_sems.at[3 * k + j], (cx, cy, c)).wait_recv()
                cp = _remote(blk, blk, fwd_send.at[3 * k + j], fwd_recv.at[3 * k + j], sibling)
                cp.start()
                passed.append(cp)
        for k in range(n):
            for j, (cx, cy) in enumerate(chips):
                blk = outs[k].at[2 * cx + cy, 1 - c]
                _remote(blk, blk, fwd_send.at[3 * k + j], fwd_recv.at[3 * k + j], sibling).wait_recv()
        for cp in first + passed:
            cp.wait_send()
        for cp in local:
            cp.wait()

    return pl.pallas_call(
        body, name=name,
        out_shape=[jax.ShapeDtypeStruct((N_CHIPS,) + a.shape, a.dtype) for a in arrs],
        in_specs=[ANY] * n, out_specs=[ANY] * n,
        scratch_shapes=[pltpu.SemaphoreType.DMA((3 * n,)), pltpu.SemaphoreType.DMA((3 * n,)),
                        pltpu.SemaphoreType.DMA((3 * n,)), pltpu.SemaphoreType.DMA((3 * n,)),
                        pltpu.SemaphoreType.DMA((n,))],
    )(*arrs)


def _sibling_send_other_half(arrs, name):
    n = len(arrs)

    def body(*refs):
        ins, outs = refs[:n], refs[n:2 * n]
        send_sems, recv_sems = refs[2 * n:]
        x, y, c = _coords()
        cps = [_remote(ins[k].at[1 - c], outs[k], send_sems.at[k], recv_sems.at[k], (x, y, 1 - c)) for k in range(n)]
        for cp in cps:
            cp.start()
        for cp in cps:
            cp.wait_recv()
        for cp in cps:
            cp.wait_send()

    return pl.pallas_call(
        body, name=name,
        out_shape=[jax.ShapeDtypeStruct(a.shape[1:], a.dtype) for a in arrs],
        in_specs=[ANY] * n, out_specs=[ANY] * n,
        scratch_shapes=[pltpu.SemaphoreType.DMA((n,)), pltpu.SemaphoreType.DMA((n,))],
    )(*arrs)


def _sibling_join_halves(arrs, name):
    n = len(arrs)

    def body(*refs):
        ins, outs = refs[:n], refs[n:2 * n]
        send_sems, recv_sems, loc_sems = refs[2 * n:]
        x, y, c = _coords()
        local = [pltpu.make_async_copy(ins[k], outs[k].at[c], loc_sems.at[k]) for k in range(n)]
        for cp in local:
            cp.start()
        cps = [_remote(ins[k], outs[k].at[c], send_sems.at[k], recv_sems.at[k], (x, y, 1 - c)) for k in range(n)]
        for cp in cps:
            cp.start()
        for k in range(n):
            blk = outs[k].at[1 - c]
            _remote(blk, blk, send_sems.at[k], recv_sems.at[k], (x, y, 1 - c)).wait_recv()
        for cp in cps:
            cp.wait_send()
        for cp in local:
            cp.wait()

    return pl.pallas_call(
        body, name=name,
        out_shape=[jax.ShapeDtypeStruct((2,) + a.shape, a.dtype) for a in arrs],
        in_specs=[ANY] * n, out_specs=[ANY] * n,
        scratch_shapes=[pltpu.SemaphoreType.DMA((n,)), pltpu.SemaphoreType.DMA((n,)), pltpu.SemaphoreType.DMA((n,))],
    )(*arrs)


def _chip_scatter(arrs, name):
    n = len(arrs)

    def body(*refs):
        ins, outs = refs[:n], refs[n:2 * n]
        send_sems, recv_sems, loc_sems = refs[2 * n:]
        x, y, c = _coords()
        me_chip = 2 * x + y
        chips = _other_chips(x, y)
        local = [pltpu.make_async_copy(ins[k].at[me_chip], outs[k].at[me_chip], loc_sems.at[k]) for k in range(n)]
        for cp in local:
            cp.start()
        cps = []
        for k in range(n):
            for j, (cx, cy) in enumerate(chips):
                cp = _remote(ins[k].at[2 * cx + cy], outs[k].at[me_chip], send_sems.at[3 * k + j],
                             recv_sems.at[3 * k + j], (cx, cy, c))
                cp.start()
                cps.append(cp)
        for k in range(n):
            for j, (cx, cy) in enumerate(chips):
                blk = outs[k].at[2 * cx + cy]
                _remote(blk, blk, send_sems.at[3 * k + j], recv_sems.at[3 * k + j], (cx, cy, c)).wait_recv()
        for cp in cps:
            cp.wait_send()
        for cp in local:
            cp.wait()

    return pl.pallas_call(
        body, name=name,
        out_shape=[jax.ShapeDtypeStruct(a.shape, a.dtype) for a in arrs],
        in_specs=[ANY] * n, out_specs=[ANY] * n,
        scratch_shapes=[pltpu.SemaphoreType.DMA((3 * n,)), pltpu.SemaphoreType.DMA((3 * n,)),
                        pltpu.SemaphoreType.DMA((n,))],
    )(*arrs)


def _as_rows(a, lanes):
    return a.reshape(-1, lanes)


def _add_half(full, recv, out_dtype, name):
    _, r, cdim = full.shape
    tr = _row_tile(r, 512)

    def body(c_ref, a_ref, b_ref, o_ref):
        o_ref[...] = (a_ref[...] + b_ref[...]).astype(out_dtype)

    c = lax.axis_index("c").astype(jnp.int32).reshape(1)
    return pl.pallas_call(
        body, name=name,
        grid_spec=pltpu.PrefetchScalarGridSpec(
            num_scalar_prefetch=1, grid=(r // tr,),
            in_specs=[pl.BlockSpec((None, tr, cdim), lambda i, c_ref: (c_ref[0], i, 0)),
                      pl.BlockSpec((tr, cdim), lambda i, c_ref: (i, 0))],
            out_specs=pl.BlockSpec((tr, cdim), lambda i, c_ref: (i, 0))),
        out_shape=jax.ShapeDtypeStruct((r, cdim), out_dtype),
        compiler_params=_cparams(1),
    )(c, full, recv)


def _sum4(parts, name):
    _, r, cdim = parts.shape
    tr = _row_tile(r, 512)

    def body(p_ref, o_ref):
        acc = p_ref[0].astype(F32) + p_ref[1].astype(F32)
        acc = acc + p_ref[2].astype(F32)
        o_ref[...] = acc + p_ref[3].astype(F32)

    return pl.pallas_call(
        body, name=name, grid=(r // tr,),
        in_specs=[pl.BlockSpec((N_CHIPS, tr, cdim), lambda i: (0, i, 0))],
        out_specs=pl.BlockSpec((tr, cdim), lambda i: (i, 0)),
        out_shape=jax.ShapeDtypeStruct((r, cdim), F32),
        compiler_params=_cparams(1),
    )(parts)


def _adamw(w, g, m, v, name):
    r, cdim = w.shape
    tr = _row_tile(r, 256)
    c1 = 1.0 - ADAM_B1 ** ADAM_STEP
    c2 = 1.0 - ADAM_B2 ** ADAM_STEP

    def body(w_ref, g_ref, m_ref, v_ref, d_ref, nm_ref, nv_ref):
        gg = g_ref[...]
        nm = ADAM_B1 * m_ref[...] + (1.0 - ADAM_B1) * gg
        nv = ADAM_B2 * v_ref[...] + (1.0 - ADAM_B2) * (gg * gg)
        m_hat = nm / c1
        v_hat = nv / c2
        d_ref[...] = -ADAM_LR * (m_hat / (jnp.sqrt(v_hat) + ADAM_EPS) + ADAM_WD * w_ref[...])
        nm_ref[...] = nm
        nv_ref[...] = nv

    spec = pl.BlockSpec((tr, cdim), lambda i: (i, 0))
    return pl.pallas_call(
        body, name=name, grid=(r // tr,),
        in_specs=[spec] * 4, out_specs=[spec] * 3,
        out_shape=[jax.ShapeDtypeStruct((r, cdim), F32)] * 3,
        compiler_params=_cparams(1),
    )(w, g, m, v)


def _discretise(a_re, a_im, log_dt, b_re, b_im):
    dt = jnp.exp(log_dt)
    mag = jnp.exp(a_re * dt)
    ab_re = mag * jnp.cos(a_im * dt)
    ab_im = mag * jnp.sin(a_im * dt)
    num_re = ab_re - 1.0
    num_im = ab_im
    den = a_re * a_re + a_im * a_im
    f_re = (num_re * a_re + num_im * a_im) / den
    f_im = (num_im * a_re - num_re * a_im) / den
    bb_re = f_re * b_re - f_im * b_im
    bb_im = f_re * b_im + f_im * b_re
    return ab_re, ab_im, bb_re, bb_im


def _disc_shapes():
    col = jax.ShapeDtypeStruct((N_STATES, 1), F32)
    mat = jax.ShapeDtypeStruct((N_STATES, SSM_GROUP), F32)
    return col, mat


def _disc_fwd(a_re, a_im, log_dt, b_re, b_im):
    col, mat = _disc_shapes()

    def body(ar, ai, ld, br, bi, o0, o1, o2, o3):
        outs = _discretise(ar[...], ai[...], ld[...], br[...], bi[...])
        for o, val in zip((o0, o1, o2, o3), outs):
            o[...] = val

    return pl.pallas_call(body, name="ssm_discretise", out_shape=[col, col, mat, mat],
                          compiler_params=_cparams())(a_re, a_im, log_dt, b_re, b_im)


def _disc_bwd(a_re, a_im, log_dt, b_re, b_im, g_ab_re, g_ab_im, g_bb_re, g_bb_im):
    col, mat = _disc_shapes()

    def body(ar, ai, ld, br, bi, g0, g1, g2, g3, o0, o1, o2, o3, o4):
        _, vjp = jax.vjp(_discretise, ar[...], ai[...], ld[...], br[...], bi[...])
        grads = vjp((g0[...], g1[...], g2[...], g3[...]))
        for o, val in zip((o0, o1, o2, o3, o4), grads):
            o[...] = val

    return pl.pallas_call(body, name="ssm_discretise_bwd", out_shape=[col, col, col, mat, mat],
                          compiler_params=_cparams())(a_re, a_im, log_dt, b_re, b_im, g_ab_re, g_ab_im, g_bb_re, g_bb_im)


def _cmul(ar, ai, br, bi):
    return ar * br - ai * bi, ar * bi + ai * br


def _scan_coefs(ab_re, ab_im, reverse):
    ar = ab_re.reshape(1, N_STATES)
    ai = ab_im.reshape(1, N_STATES)
    if reverse:
        ai = -ai
    a2 = _cmul(ar, ai, ar, ai)
    a4 = _cmul(*a2, *a2)
    rows = jnp.arange(SUBLANES)[:, None]
    out = []
    for (pr, pi), sh in (((ar, ai), 1), (a2, 2), (a4, 4)):
        keep = (rows <= SUBLANES - 1 - sh) if reverse else (rows >= sh)
        out += [jnp.where(keep, pr, 0.0), jnp.where(keep, pi, 0.0)]
    pows = [(ar, ai)]
    for _ in range(SUBLANES - 1):
        pows.append(_cmul(*pows[-1], ar, ai))
    order = pows[::-1] if reverse else pows
    out += [jnp.concatenate([p[0] for p in order], 0), jnp.concatenate([p[1] for p in order], 0)]
    t = jnp.stack(out, 0)
    return t.reshape(8, SUBLANES, SSM_CHUNKS, CH_S).transpose(2, 0, 1, 3)


def _block_diag_in(bb):
    t = bb.reshape(SSM_CHUNKS, 8, SSM_STATE, SSM_GROUP)
    eye = jnp.eye(8, dtype=bb.dtype)
    return jnp.einsum("jgph,gk->jghkp", t, eye).reshape(SSM_CHUNKS, CH_W, CH_S)


def _block_diag_in_t(d):
    t = d.reshape(SSM_CHUNKS, 8, SSM_GROUP, 8, SSM_STATE)
    return jnp.einsum("jghgp->jgph", t).reshape(N_STATES, SSM_GROUP)


def _block_diag_out(c):
    t = c.reshape(SSM_CHUNKS, 8, SSM_GROUP, SSM_STATE)
    eye = jnp.eye(8, dtype=c.dtype)
    return jnp.einsum("jghp,gk->jgpkh", t, eye).reshape(SSM_CHUNKS, CH_S, CH_W)


def _block_diag_out_t(d):
    t = d.reshape(SSM_CHUNKS, 8, SSM_STATE, 8, SSM_GROUP)
    return jnp.einsum("jgpgh->jghp", t).reshape(SSM_GROUPS, SSM_GROUP, SSM_STATE)


def _head_ones():
    r = jnp.arange(ATTN_WIDTH) // HEAD_DIM
    return jnp.where(r[:, None] == r[None, :], 1.0 / HEAD_DIM, 0.0).astype(BF16)


def _in_proj(h, g1, w_in_all, layer, qg, kg):
    s = h.shape[0]
    tm = _row_tile(s, 256)

    def body(h_ref, g_ref, w_ref, qg_ref, kg_ref, ones_ref, proj_ref, qkv_ref):
        x = h_ref[...]
        r = lax.rsqrt(jnp.mean(x * x, axis=-1, keepdims=True) + RMS_EPS)
        hn = (x * r * g_ref[...]).astype(BF16)
        for sh in range(N_CHIPS):
            proj_ref[:, IN_SHARD * sh:IN_SHARD * (sh + 1)] = _dot(hn, w_ref[sh])
        ones = ones_ref[...]
        q = proj_ref[:, 1024:1536]
        k = proj_ref[:, 1536:2048]
        rq = lax.rsqrt(_dot_hilo(q * q, ones) + RMS_EPS)
        rk = lax.rsqrt(_dot_hilo(k * k, ones) + RMS_EPS)
        qkv_ref[:, 0:512] = (q * rq * qg_ref[...] * ATTN_SCALE).astype(BF16)
        qkv_ref[:, 512:1024] = (k * rk * kg_ref[...]).astype(BF16)
        qkv_ref[:, 1024:1536] = proj_ref[:, 2048:2560].astype(BF16)

    full = lambda shape: pl.BlockSpec(shape, lambda i: (0,) * len(shape))
    return pl.pallas_call(
        body, name="in_proj", grid=(s // tm,),
        in_specs=[pl.BlockSpec((tm, D_MODEL), lambda i: (i, 0)), full((1, D_MODEL)),
                  pl.BlockSpec((N_CHIPS, None, D_MODEL, IN_SHARD), lambda i: (0, layer, 0, 0)),
                  full((1, ATTN_WIDTH)), full((1, ATTN_WIDTH)), full((ATTN_WIDTH, ATTN_WIDTH))],
        out_specs=[pl.BlockSpec((tm, IN_COLS), lambda i: (i, 0)), pl.BlockSpec((tm, 3 * ATTN_WIDTH), lambda i: (i, 0))],
        out_shape=[jax.ShapeDtypeStruct((s, IN_COLS), F32), jax.ShapeDtypeStruct((s, 3 * ATTN_WIDTH), BF16)],
        compiler_params=_cparams(1),
    )(h, g1, w_in_all, qg, kg, _head_ones())


def _scan_rows(x_ref, coef_ref, carry_ref, n_blocks, reverse, extra=None):
    c = [coef_ref[a] for a in range(8)]
    shifts = (7, 6, 4) if reverse else (1, 2, 4)
    edge = 0 if reverse else SUBLANES - 1

    def blk(b, carry):
        bb = (n_blocks - 1 - b) if reverse else b
        r0 = pl.multiple_of(bb * SUBLANES, SUBLANES)
        xr = x_ref[pl.ds(r0, SUBLANES), 0:CH_S]
        xi = x_ref[pl.ds(r0, SUBLANES), CH_S:2 * CH_S]
        for lvl, sh in enumerate(shifts):
            ar, ai = c[2 * lvl], c[2 * lvl + 1]
            sr = pltpu.roll(xr, sh, 0)
            si = pltpu.roll(xi, sh, 0)
            xr, xi = xr + (ar * sr - ai * si), xi + (ar * si + ai * sr)
        cr, ci = carry
        xr, xi = xr + (c[6] * cr - c[7] * ci), xi + (c[6] * ci + c[7] * cr)
        x_ref[pl.ds(r0, SUBLANES), 0:CH_S] = xr
        x_ref[pl.ds(r0, SUBLANES), CH_S:2 * CH_S] = xi
        if extra is not None:
            extra(r0, xr, xi, cr, ci)
        return (jnp.broadcast_to(xr[edge:edge + 1, :], (SUBLANES, CH_S)),
                jnp.broadcast_to(xi[edge:edge + 1, :], (SUBLANES, CH_S)))

    cr, ci = lax.fori_loop(0, n_blocks, blk, (carry_ref[:, 0:CH_S], carry_ref[:, CH_S:2 * CH_S]))
    carry_ref[:, 0:CH_S] = cr
    carry_ref[:, CH_S:2 * CH_S] = ci


def _ssm_scan_fwd(proj, wb, coef, wc):
    s = proj.shape[0]
    tm = _row_tile(s, 512)

    def body(u_ref, wb_ref, coef_ref, wc_ref, xs_ref, y_ref, carry_ref):
        @pl.when(pl.program_id(1) == 0)
        def _():
            carry_ref[...] = jnp.zeros_like(carry_ref)

        xs_ref[...] = _dot(u_ref[...].astype(BF16), wb_ref[...])
        _scan_rows(xs_ref, coef_ref, carry_ref, tm // SUBLANES, reverse=False)
        y_ref[...] = _dot(xs_ref[...].astype(BF16), wc_ref[...])

    return pl.pallas_call(
        body, name="ssm_scan", grid=(SSM_CHUNKS, s // tm),
        in_specs=[pl.BlockSpec((tm, CH_W), lambda j, i: (i, j)),
                  pl.BlockSpec((None, CH_W, 2 * CH_S), lambda j, i: (j, 0, 0)),
                  pl.BlockSpec((None, 8, SUBLANES, CH_S), lambda j, i: (j, 0, 0, 0)),
                  pl.BlockSpec((None, 2 * CH_S, CH_W), lambda j, i: (j, 0, 0))],
        out_specs=[pl.BlockSpec((None, tm, 2 * CH_S), lambda j, i: (j, i, 0)),
                   pl.BlockSpec((tm, CH_W), lambda j, i: (i, j))],
        out_shape=[jax.ShapeDtypeStruct((SSM_CHUNKS, s, 2 * CH_S), F32), jax.ShapeDtypeStruct((s, SSM_WIDTH), F32)],
        scratch_shapes=[pltpu.VMEM((SUBLANES, 2 * CH_S), F32)],
        compiler_params=_cparams(2),
    )(proj, wb, coef, wc)


def _glu_forward(y, u, d, wg_ref, bg):
    yf = y + d * u
    z = _gelu(yf)
    zb = z.astype(BF16)
    zz = jnp.concatenate([_dot(zb, wg_ref[sh]) for sh in range(N_CHIPS)], axis=-1) + bg
    return yf, z, zz[:, 0:SSM_WIDTH], zz[:, SSM_WIDTH:2 * SSM_WIDTH]


def _ssm_glu_fwd(y, proj, d, w_glu_all, layer, b_glu):
    s = y.shape[0]
    tm = _row_tile(s, 512)

    def body(y_ref, u_ref, gs_ref, d_ref, wg_ref, bg_ref, o_ref):
        _, _, val, gate = _glu_forward(y_ref[...], u_ref[...], d_ref[...], wg_ref, bg_ref[...])
        gs = gs_ref[...]
        o_ref[...] = val * _sigmoid(gate) * (gs * _sigmoid(gs))

    row = lambda i: (i, 0)
    return pl.pallas_call(
        body, name="ssm_glu", grid=(s // tm,),
        in_specs=[pl.BlockSpec((tm, SSM_WIDTH), row), pl.BlockSpec((tm, SSM_WIDTH), row),
                  pl.BlockSpec((tm, SSM_WIDTH), lambda i: (i, 1)), pl.BlockSpec((1, SSM_WIDTH), lambda i: (0, 0)),
                  pl.BlockSpec((N_CHIPS, None, SSM_WIDTH, ROW_SHARD), lambda i: (0, layer, 0, 0)),
                  pl.BlockSpec((1, 2 * SSM_WIDTH), lambda i: (0, 0))],
        out_specs=pl.BlockSpec((tm, SSM_WIDTH), row),
        out_shape=jax.ShapeDtypeStruct((s, SSM_WIDTH), F32),
        compiler_params=_cparams(1),
    )(y, proj, proj, d, w_glu_all, b_glu)


def _tri(kind):
    r = jnp.arange(ATTN_BLOCK)
    if kind == "suffix_incl":
        m = r[:, None] >= r[None, :]
    else:
        m = r[:, None] < r[None, :]
    return jnp.concatenate([m, jnp.ones_like(m)], axis=1).astype(BF16)


def _head_masks():
    lane = lax.broadcasted_iota(jnp.int32, (1, 2 * HEAD_DIM), 1)
    return [lane < HEAD_DIM, lane >= HEAD_DIM]


def _stick_block(qh, kb, lsum, tri_ref, strict):
    tb = ATTN_BLOCK
    z = _dot_nt(qh, kb)
    sp = jnp.maximum(z, 0.0) + jnp.log1p(jnp.exp(-jnp.abs(z)))
    ls = -sp
    if strict is not None:
        ls = jnp.where(strict, ls, 0.0)
    sums = _dot_hilo(ls, tri_ref[...])
    w = jnp.exp((z - sp) + (sums[:, 0:tb] - ls) + lsum)
    if strict is not None:
        w = jnp.where(strict, w, 0.0)
    return z, w, lsum + sums[:, tb:2 * tb]


def _chain_step(t, base, n_sub, first, q_ref, k_ref, tri_ref, l_scr, per_chain):
    tb = ATTN_BLOCK
    row = lax.broadcasted_iota(jnp.int32, (tb, tb), 0)
    col = lax.broadcasted_iota(jnp.int32, (tb, tb), 1)
    strict = (col < row) if first else None
    masks = _head_masks()
    top = None
    for a in range(n_sub):
        blk = base + a - t
        r0 = pl.multiple_of(jnp.maximum(blk, 0) * tb, tb)
        kb = k_ref[pl.ds(r0, tb), :]
        qa = q_ref[a * tb:(a + 1) * tb, :]
        for h, mask in enumerate(masks):
            c = 2 * a + h
            qh = jnp.where(mask, qa, jnp.zeros_like(qa))
            if first:
                lsum = jnp.zeros((tb, tb), F32)
            else:
                lsum = l_scr[c] + jnp.where(blk >= 0, 0.0, -1e30)
            z, w, lsum = _stick_block(qh, kb, lsum, tri_ref, strict)
            l_scr[c] = lsum
            per_chain(a, h, c, r0, qh, z, w)
            top = lsum if top is None else jnp.maximum(top, lsum)
    return jnp.max(top)


def _chain_sweep(base, n_sub, q_ref, k_ref, tri_ref, l_scr, per_chain):
    top = _chain_step(0, base, n_sub, True, q_ref, k_ref, tri_ref, l_scr, functools.partial(per_chain, 0))

    def cond(carry):
        t, top = carry
        return jnp.logical_and(t <= base + n_sub - 1, top > EXP_ZERO)

    def step(carry):
        t, _ = carry
        return t + 1, _chain_step(t, base, n_sub, False, q_ref, k_ref, tri_ref, l_scr, functools.partial(per_chain, t))

    steps, _ = lax.while_loop(cond, step, (jnp.int32(1), top))
    return steps


ATTN_SUB_FWD = 4
ATTN_SUB_BWD = 2


def _attn_fwd(qkv, proj):
    s = qkv.shape[0]
    tb = ATTN_BLOCK
    n_sub = min(ATTN_SUB_FWD, s // tb)
    tq = n_sub * tb

    def body(q_ref, k_ref, v_ref, g_ref, tri_ref, o_ref, ya_ref, l_scr):
        i = pl.program_id(1)
        masks = _head_masks()
        o_ref[...] = jnp.zeros_like(o_ref)

        def per_chain(t, a, h, c, r0, qh, z, w):
            vb = v_ref[pl.ds(r0, tb), :]
            vb = jnp.where(masks[h], vb, jnp.zeros_like(vb))
            o_ref[a * tb:(a + 1) * tb, :] += _dot(w.astype(BF16), vb)

        _chain_sweep(i * n_sub, n_sub, q_ref, k_ref, tri_ref, l_scr, per_chain)
        g = g_ref[...]
        ya_ref[...] = o_ref[...] * (g * _sigmoid(g))

    hp_blk = lambda off: pl.BlockSpec((tq, 2 * HEAD_DIM), lambda hp, i: (i, off + hp))
    res = lambda off: pl.BlockSpec((s, 2 * HEAD_DIM), lambda hp, i: (0, off + hp))
    return pl.pallas_call(
        body, name="attn_fwd", grid=(ATTN_WIDTH // (2 * HEAD_DIM), s // tq),
        in_specs=[hp_blk(0), res(4), res(8), hp_blk(20), pl.BlockSpec((tb, 2 * tb), lambda hp, i: (0, 0))],
        out_specs=[hp_blk(0), hp_blk(0)],
        out_shape=[jax.ShapeDtypeStruct((s, ATTN_WIDTH), F32)] * 2,
        scratch_shapes=[pltpu.VMEM((2 * n_sub, tb, tb), F32)],
        compiler_params=_cparams(2),
    )(qkv, qkv, qkv, proj, _tri("suffix_incl"))


def _rms_rows(x, g):
    r = lax.rsqrt(jnp.mean(x * x, axis=-1, keepdims=True) + RMS_EPS)
    return r, x * r * g


def _ple_forward(h1, p, g2, wpg_ref, wpp_ref):
    r2, hn2 = _rms_rows(h1, g2)
    hb = hn2.astype(BF16)
    gpre = _dot(hb[:, 0:ROW_SHARD], wpg_ref[0])
    for sh in range(1, N_CHIPS):
        gpre = gpre + _dot(hb[:, ROW_SHARD * sh:ROW_SHARD * (sh + 1)], wpg_ref[sh])
    gate = _sigmoid(gpre)
    pb = p.astype(BF16)
    pp = jnp.concatenate([_dot(pb, wpp_ref[sh]) for sh in range(N_CHIPS)], axis=-1)
    return r2, hb, gate, pp


def _out_ple(h, ys, ya, p, g2, w_out_all, w_pg_all, w_pp_all, layer):
    s = h.shape[0]
    tm = _row_tile(s, 256)

    def body(h_ref, ys_ref, ya_ref, p_ref, g_ref, wo_ref, wpg_ref, wpp_ref, h1_ref, h2_ref):
        ysb = ys_ref[...].astype(BF16)
        yab = ya_ref[...].astype(BF16)
        h1 = h_ref[...]
        for sh, src in enumerate((ysb[:, 0:ROW_SHARD], ysb[:, ROW_SHARD:], yab[:, 0:ROW_SHARD], yab[:, ROW_SHARD:])):
            h1 = h1 + _dot(src, wo_ref[sh])
        _, _, gate, pp = _ple_forward(h1, p_ref[...], g_ref[...], wpg_ref, wpp_ref)
        h1_ref[...] = h1
        h2_ref[...] = h1 + gate * pp

    row = lambda i: (i, 0)
    wspec = lambda r, cdim: pl.BlockSpec((N_CHIPS, None, r, cdim), lambda i: (0, layer, 0, 0))
    return pl.pallas_call(
        body, name="out_ple", grid=(s // tm,),
        in_specs=[pl.BlockSpec((tm, D_MODEL), row), pl.BlockSpec((tm, SSM_WIDTH), row), pl.BlockSpec((tm, ATTN_WIDTH), row),
                  pl.BlockSpec((tm, PLE_DIM), row), pl.BlockSpec((1, D_MODEL), lambda i: (0, 0)),
                  wspec(ROW_SHARD, D_MODEL), wspec(ROW_SHARD, D_MODEL), wspec(PLE_DIM, ROW_SHARD)],
        out_specs=[pl.BlockSpec((tm, D_MODEL), row)] * 2,
        out_shape=[jax.ShapeDtypeStruct((s, D_MODEL), F32)] * 2,
        compiler_params=_cparams(1),
    )(h, ys, ya, p, g2, w_out_all, w_pg_all, w_pp_all)


def _loss_grad(y, target):
    s = y.shape[0]
    tm = _row_tile(s, 512)

    def body(y_ref, t_ref, dy_ref, acc_ref):
        @pl.when(pl.program_id(0) == 0)
        def _():
            acc_ref[...] = jnp.zeros_like(acc_ref)

        e = y_ref[...] - t_ref[...]
        dy_ref[...] = e / D_MODEL
        sq = (e * e).reshape(tm // SUBLANES, SUBLANES, D_MODEL).sum(axis=0)
        part = sq[:, 0:128]
        for b in range(1, D_MODEL // 128):
            part = part + sq[:, 128 * b:128 * (b + 1)]
        acc_ref[...] += part

    row = lambda i: (i, 0)
    return pl.pallas_call(
        body, name="loss_grad", grid=(s // tm,),
        in_specs=[pl.BlockSpec((tm, D_MODEL), row)] * 2,
        out_specs=[pl.BlockSpec((tm, D_MODEL), row), pl.BlockSpec((SUBLANES, 128), lambda i: (0, 0))],
        out_shape=[jax.ShapeDtypeStruct((s, D_MODEL), F32), jax.ShapeDtypeStruct((SUBLANES, 128), F32)],
        compiler_params=_cparams(1),
    )(y, target)


def _rms_bwd(x, r, g, dy):
    gdy = g * dy
    dx = r * gdy - x * (r * r * r) * jnp.mean(x * gdy, axis=-1, keepdims=True)
    return dx, x * r * dy


def _colsum8(a):
    t = a.shape[0]
    return a.reshape(t // SUBLANES, SUBLANES, a.shape[1]).sum(axis=0)


def _out_ple_bwd(dh2, h1, p, g2, w_out_all, w_pg_all, w_pp_all, layer):
    s = h1.shape[0]
    tm = _row_tile(s, 256)

    def body(dh2_ref, h1_ref, p_ref, g_ref, wo_ref, wpg_ref, wpp_ref,
             dh1_ref, dmix_ref, hn_ref, dgp_ref, dpp_ref, dh1b_ref, dg_ref):
        @pl.when(pl.program_id(0) == 0)
        def _():
            dg_ref[...] = jnp.zeros_like(dg_ref)

        h1 = h1_ref[...]
        dh2 = dh2_ref[...]
        g2v = g_ref[...]
        r2, hb, gate, pp = _ple_forward(h1, p_ref[...], g2v, wpg_ref, wpp_ref)
        dgp = (dh2 * pp) * gate * (1.0 - gate)
        dgpb = dgp.astype(BF16)
        dhn = jnp.concatenate([_dot_nt(dgpb, wpg_ref[sh]) for sh in range(N_CHIPS)], axis=-1)
        dx, dgrow = _rms_bwd(h1, r2, g2v, dhn)
        dh1 = dh2 + dx
        dh1b = dh1.astype(BF16)
        dh1_ref[...] = dh1
        dh1b_ref[...] = dh1b
        hn_ref[...] = hb
        dgp_ref[...] = dgpb
        dpp_ref[...] = (dh2 * gate).astype(BF16)
        dg_ref[...] += _colsum8(dgrow)
        for sh in range(N_CHIPS):
            dmix_ref[:, ROW_SHARD * sh:ROW_SHARD * (sh + 1)] = _dot_nt(dh1b, wo_ref[sh])

    row = lambda i: (i, 0)
    wspec = lambda r, cdim: pl.BlockSpec((N_CHIPS, None, r, cdim), lambda i: (0, layer, 0, 0))
    big = pl.BlockSpec((tm, D_MODEL), row)
    return pl.pallas_call(
        body, name="out_ple_bwd", grid=(s // tm,),
        in_specs=[big, big, pl.BlockSpec((tm, PLE_DIM), row), pl.BlockSpec((1, D_MODEL), lambda i: (0, 0)),
                  wspec(ROW_SHARD, D_MODEL), wspec(ROW_SHARD, D_MODEL), wspec(PLE_DIM, ROW_SHARD)],
        out_specs=[big] * 6 + [pl.BlockSpec((SUBLANES, D_MODEL), lambda i: (0, 0))],
        out_shape=[jax.ShapeDtypeStruct((s, D_MODEL), F32)] * 2 + [jax.ShapeDtypeStruct((s, D_MODEL), BF16)] * 4
        + [jax.ShapeDtypeStruct((SUBLANES, D_MODEL), F32)],
        compiler_params=_cparams(1),
    )(dh2, h1, p, g2, w_out_all, w_pg_all, w_pp_all)


def _tn_matmul(a, b, n_blocks, block_a, name):
    s = a.shape[0]
    tk = _row_tile(s, 512)
    ka, nb = a.shape[1], b.shape[1]
    if block_a:
        ka //= n_blocks
    else:
        nb //= n_blocks

    def body(a_ref, b_ref, o_ref):
        @pl.when(pl.program_id(1) == 0)
        def _():
            o_ref[...] = jnp.zeros_like(o_ref)

        o_ref[...] += _dot_tn(a_ref[...].astype(BF16), b_ref[...].astype(BF16))

    a_map = (lambda sh, i: (i, sh)) if block_a else (lambda sh, i: (i, 0))
    b_map = (lambda sh, i: (i, 0)) if block_a else (lambda sh, i: (i, sh))
    return pl.pallas_call(
        body, name=name, grid=(n_blocks, s // tk),
        in_specs=[pl.BlockSpec((tk, ka), a_map), pl.BlockSpec((tk, nb), b_map)],
        out_specs=pl.BlockSpec((None, ka, nb), lambda sh, i: (sh, 0, 0)),
        out_shape=jax.ShapeDtypeStruct((n_blocks, ka, nb), F32),
        compiler_params=_cparams(2),
    )(a, b)


def _attn_bwd(qkv, o, proj, dmix):
    s = qkv.shape[0]
    tb = ATTN_BLOCK
    nq = s // tb
    n_sub = min(ATTN_SUB_BWD, nq)
    tq = n_sub * tb
    n_chain = 2 * n_sub

    def body(q_ref, k_ref, v_ref, o_ref, g_ref, dya_ref, tri_s_ref, tri_p_ref,
             dq_ref, dk_ref, dv_ref, dg_ref, do_scr, l_scr, g_scr, z_scr, w_scr):
        i = pl.program_id(1)
        base = i * n_sub

        @pl.when(i == 0)
        def _():
            dk_ref[...] = jnp.zeros_like(dk_ref)
            dv_ref[...] = jnp.zeros_like(dv_ref)

        g = g_ref[...]
        sg = _sigmoid(g)
        dya = dya_ref[...]
        do_scr[...] = (dya * (g * sg)).astype(BF16)
        dg_ref[...] = dya * o_ref[...] * (sg * (1.0 + g * (1.0 - sg)))
        dq_ref[...] = jnp.zeros_like(dq_ref)
        g_scr[...] = jnp.zeros_like(g_scr)
        masks = _head_masks()

        def keep(t, a, h, c, r0, qh, z, w):
            z_scr[c, t] = z
            w_scr[c, t] = w

        steps = _chain_sweep(base, n_sub, q_ref, k_ref, tri_s_ref, l_scr, keep)
        row = lax.broadcasted_iota(jnp.int32, (tb, tb), 0)
        col = lax.broadcasted_iota(jnp.int32, (tb, tb), 1)

        def back(it, carry):
            t = steps - 1 - it
            for a in range(n_sub):
                blk = base + a - t
                r0 = pl.multiple_of(jnp.maximum(blk, 0) * tb, tb)
                kb = k_ref[pl.ds(r0, tb), :]
                vb = v_ref[pl.ds(r0, tb), :]
                qa = q_ref[a * tb:(a + 1) * tb, :]
                doa = do_scr[a * tb:(a + 1) * tb, :]
                for h, mask in enumerate(masks):
                    c = 2 * a + h
                    qh = jnp.where(mask, qa, jnp.zeros_like(qa))
                    doh = jnp.where(mask, doa, jnp.zeros_like(doa))
                    kh = jnp.where(mask, kb, jnp.zeros_like(kb))
                    z = z_scr[c, t]
                    w = w_scr[c, t]
                    gw = w * _dot_nt(doh, vb)
                    sums = _dot_hilo(gw, tri_p_ref[...])
                    gsum = g_scr[c]
                    dz = gw - (gw + sums[:, 0:tb] + gsum) * _sigmoid(z)
                    dz = jnp.where(col < row + t * tb, dz, 0.0)
                    g_scr[c] = gsum + sums[:, tb:2 * tb]
                    dzb = dz.astype(BF16)
                    dk_ref[pl.ds(r0, tb), :] += _dot_tn(dzb, qh)
                    dv_ref[pl.ds(r0, tb), :] += _dot_tn(w.astype(BF16), doh)
                    dq_ref[a * tb:(a + 1) * tb, :] += _dot(dzb, kh)
            return carry

        lax.fori_loop(0, steps, back, 0)

    hp_blk = lambda off: pl.BlockSpec((tq, 2 * HEAD_DIM), lambda hp, i: (i, off + hp))
    res = lambda off: pl.BlockSpec((s, 2 * HEAD_DIM), lambda hp, i: (0, off + hp))
    tri = pl.BlockSpec((tb, 2 * tb), lambda hp, i: (0, 0))
    return pl.pallas_call(
        body, name="attn_bwd", grid=(ATTN_WIDTH // (2 * HEAD_DIM), s // tq),
        in_specs=[hp_blk(0), res(4), res(8), hp_blk(0), hp_blk(20), hp_blk(4), tri, tri],
        out_specs=[hp_blk(0), res(0), res(0), hp_blk(0)],
        out_shape=[jax.ShapeDtypeStruct((s, ATTN_WIDTH), F32)] * 4,
        scratch_shapes=[pltpu.VMEM((tq, 2 * HEAD_DIM), BF16), pltpu.VMEM((n_chain, tb, tb), F32),
                        pltpu.VMEM((n_chain, tb, tb), F32), pltpu.VMEM((n_chain, nq, tb, tb), F32),
                        pltpu.VMEM((n_chain, nq, tb, tb), F32)],
        compiler_params=_cparams(2),
    )(qkv, qkv, qkv, o, proj, dmix, _tri("suffix_incl"), _tri("prefix_strict"))


def _ssm_glu_bwd(dmix, y, proj, d, w_glu_all, layer, b_glu):
    s = y.shape[0]
    tm = _row_tile(s, 512)

    def body(dys_ref, y_ref, u_ref, gs_ref, d_ref, wg_ref, bg_ref,
             dyf_ref, du_ref, dgs_ref, z_ref, dzz_ref, dd_ref, db_ref):
        @pl.when(pl.program_id(0) == 0)
        def _():
            dd_ref[...] = jnp.zeros_like(dd_ref)
            db_ref[...] = jnp.zeros_like(db_ref)

        u = u_ref[...]
        dv = d_ref[...]
        yf, z, val, gate = _glu_forward(y_ref[...], u, dv, wg_ref, bg_ref[...])
        gs = gs_ref[...]
        sgs = _sigmoid(gs)
        sgate = _sigmoid(gate)
        dys = dys_ref[...]
        dgv = dys * (gs * sgs)
        dgs_ref[...] = dys * (val * sgate) * (sgs * (1.0 + gs * (1.0 - sgs)))
        dzz = jnp.concatenate([dgv * sgate, dgv * val * sgate * (1.0 - sgate)], axis=-1)
        dzzb = dzz.astype(BF16)
        dz = _dot_nt(dzzb[:, 0:ROW_SHARD], wg_ref[0])
        for sh in range(1, N_CHIPS):
            dz = dz + _dot_nt(dzzb[:, ROW_SHARD * sh:ROW_SHARD * (sh + 1)], wg_ref[sh])
        dyf = dz * _gelu_grad(yf)
        dyf_ref[...] = dyf
        du_ref[...] = dyf * dv
        z_ref[...] = z.astype(BF16)
        dzz_ref[...] = dzzb
        dd_ref[...] += _colsum8(dyf * u)
        db_ref[...] += _colsum8(dzz)

    row = lambda i: (i, 0)
    half = pl.BlockSpec((tm, SSM_WIDTH), row)
    return pl.pallas_call(
        body, name="ssm_glu_bwd", grid=(s // tm,),
        in_specs=[half, half, half, pl.BlockSpec((tm, SSM_WIDTH), lambda i: (i, 1)),
                  pl.BlockSpec((1, SSM_WIDTH), lambda i: (0, 0)),
                  pl.BlockSpec((N_CHIPS, None, SSM_WIDTH, ROW_SHARD), lambda i: (0, layer, 0, 0)),
                  pl.BlockSpec((1, 2 * SSM_WIDTH), lambda i: (0, 0))],
        out_specs=[half, half, half, half, pl.BlockSpec((tm, 2 * SSM_WIDTH), row),
                   pl.BlockSpec((SUBLANES, SSM_WIDTH), lambda i: (0, 0)),
                   pl.BlockSpec((SUBLANES, 2 * SSM_WIDTH), lambda i: (0, 0))],
        out_shape=[jax.ShapeDtypeStruct((s, SSM_WIDTH), F32)] * 3
        + [jax.ShapeDtypeStruct((s, SSM_WIDTH), BF16), jax.ShapeDtypeStruct((s, 2 * SSM_WIDTH), BF16),
           jax.ShapeDtypeStruct((SUBLANES, SSM_WIDTH), F32), jax.ShapeDtypeStruct((SUBLANES, 2 * SSM_WIDTH), F32)],
        compiler_params=_cparams(1),
    )(dmix, y, proj, proj, d, w_glu_all, b_glu)


def _ssm_scan_bwd(dyf, xs, proj, wct, coef_rev, wbt):
    s = dyf.shape[0]
    tm = _row_tile(s, 512)
    nt = s // tm

    def body(dy_ref, xs_ref, u_ref, wct_ref, coef_ref, wbt_ref, du_ref, dwc_ref, dwb_ref, da_ref, lam_ref, carry_ref):
        @pl.when(pl.program_id(1) == 0)
        def _():
            carry_ref[...] = jnp.zeros_like(carry_ref)
            dwc_ref[...] = jnp.zeros_like(dwc_ref)
            dwb_ref[...] = jnp.zeros_like(dwb_ref)
            da_ref[...] = jnp.zeros_like(da_ref)

        dyb = dy_ref[...].astype(BF16)
        lam_ref[...] = _dot(dyb, wct_ref[...])
        rows = lax.broadcasted_iota(jnp.int32, (SUBLANES, CH_S), 0)
        last = rows == SUBLANES - 1

        def extra(r0, lr, li, cr, ci):
            er = jnp.where(last, cr, pltpu.roll(lr, SUBLANES - 1, 0))
            ei = jnp.where(last, ci, pltpu.roll(li, SUBLANES - 1, 0))
            xr = xs_ref[pl.ds(r0, SUBLANES), 0:CH_S]
            xi = xs_ref[pl.ds(r0, SUBLANES), CH_S:2 * CH_S]
            da_ref[:, 0:CH_S] += xr * er + xi * ei
            da_ref[:, CH_S:2 * CH_S] += xr * ei - xi * er

        _scan_rows(lam_ref, coef_ref, carry_ref, tm // SUBLANES, reverse=True, extra=extra)
        lamb = lam_ref[...].astype(BF16)
        du_ref[...] = _dot(lamb, wbt_ref[...])
        dwc_ref[...] += _dot_tn(xs_ref[...].astype(BF16), dyb)
        dwb_ref[...] += _dot_tn(u_ref[...].astype(BF16), lamb)

    rev = lambda j, i: (nt - 1 - i, j)
    return pl.pallas_call(
        body, name="ssm_scan_bwd", grid=(SSM_CHUNKS, nt),
        in_specs=[pl.BlockSpec((tm, CH_W), rev),
                  pl.BlockSpec((None, tm, 2 * CH_S), lambda j, i: (j, nt - 1 - i, 0)),
                  pl.BlockSpec((tm, CH_W), rev),
                  pl.BlockSpec((None, CH_W, 2 * CH_S), lambda j, i: (j, 0, 0)),
                  pl.BlockSpec((None, 8, SUBLANES, CH_S), lambda j, i: (j, 0, 0, 0)),
                  pl.BlockSpec((None, 2 * CH_S, CH_W), lambda j, i: (j, 0, 0))],
        out_specs=[pl.BlockSpec((tm, CH_W), rev),
                   pl.BlockSpec((None, 2 * CH_S, CH_W), lambda j, i: (j, 0, 0)),
                   pl.BlockSpec((None, CH_W, 2 * CH_S), lambda j, i: (j, 0, 0)),
                   pl.BlockSpec((None, SUBLANES, 2 * CH_S), lambda j, i: (j, 0, 0))],
        out_shape=[jax.ShapeDtypeStruct((s, SSM_WIDTH), F32),
                   jax.ShapeDtypeStruct((SSM_CHUNKS, 2 * CH_S, CH_W), F32),
                   jax.ShapeDtypeStruct((SSM_CHUNKS, CH_W, 2 * CH_S), F32),
                   jax.ShapeDtypeStruct((SSM_CHUNKS, SUBLANES, 2 * CH_S), F32)],
        scratch_shapes=[pltpu.VMEM((tm, 2 * CH_S), F32), pltpu.VMEM((SUBLANES, 2 * CH_S), F32)],
        compiler_params=_cparams(2),
    )(dyf, xs, proj, wct, coef_rev, wbt)


def _in_proj_bwd(h, g1, w_in_all, layer, qg, kg, proj, du_a, du_b, dgs, dq, dk, dv, dga, dh1):
    s = h.shape[0]
    tm = _row_tile(s, 256)

    def body(h_ref, g_ref, w_ref, qg_ref, kg_ref, ones_ref, q_ref, k_ref, dua_ref, dub_ref, dgs_ref, dq_ref, dk_ref,
             dv_ref, dga_ref, dh1_ref, dh_ref, hn_ref, dp_ref, dg1_ref, dqg_ref, dkg_ref):
        @pl.when(pl.program_id(0) == 0)
        def _():
            dg1_ref[...] = jnp.zeros_like(dg1_ref)
            dqg_ref[...] = jnp.zeros_like(dqg_ref)
            dkg_ref[...] = jnp.zeros_like(dkg_ref)

        ones = ones_ref[...]

        def head_norm_bwd(x, gain, dy):
            r = lax.rsqrt(_dot_hilo(x * x, ones) + RMS_EPS)
            gdy = gain * dy
            dx = r * gdy - x * (r * r * r) * _dot_hilo(x * gdy, ones)
            return dx, x * r * dy

        dqr, dqg_rows = head_norm_bwd(q_ref[...], qg_ref[...], dq_ref[...] * ATTN_SCALE)
        dkr, dkg_rows = head_norm_bwd(k_ref[...], kg_ref[...], dk_ref[...])
        dqg_ref[...] += _colsum8(dqg_rows)
        dkg_ref[...] += _colsum8(dkg_rows)
        dp_ref[:, 0:512] = (dua_ref[...] + dub_ref[...]).astype(BF16)
        dp_ref[:, 512:1024] = dgs_ref[...].astype(BF16)
        dp_ref[:, 1024:1536] = dqr.astype(BF16)
        dp_ref[:, 1536:2048] = dkr.astype(BF16)
        dp_ref[:, 2048:2560] = dv_ref[...].astype(BF16)
        dp_ref[:, 2560:3072] = dga_ref[...].astype(BF16)
        dhn = _dot_nt(dp_ref[:, 0:IN_SHARD], w_ref[0])
        for sh in range(1, N_CHIPS):
            dhn = dhn + _dot_nt(dp_ref[:, IN_SHARD * sh:IN_SHARD * (sh + 1)], w_ref[sh])
        x = h_ref[...]
        gv = g_ref[...]
        r, hn = _rms_rows(x, gv)
        dx, dg_rows = _rms_bwd(x, r, gv, dhn)
        dh_ref[...] = dh1_ref[...] + dx
        hn_ref[...] = hn.astype(BF16)
        dg1_ref[...] += _colsum8(dg_rows)

    row = lambda i: (i, 0)
    full = lambda shape: pl.BlockSpec(shape, lambda i: (0,) * len(shape))
    big = pl.BlockSpec((tm, D_MODEL), row)
    half = pl.BlockSpec((tm, 512), row)
    return pl.pallas_call(
        body, name="in_proj_bwd", grid=(s // tm,),
        in_specs=[big, full((1, D_MODEL)), pl.BlockSpec((N_CHIPS, None, D_MODEL, IN_SHARD), lambda i: (0, layer, 0, 0)),
                  full((1, ATTN_WIDTH)), full((1, ATTN_WIDTH)), full((ATTN_WIDTH, ATTN_WIDTH)),
                  pl.BlockSpec((tm, 512), lambda i: (i, 2)), pl.BlockSpec((tm, 512), lambda i: (i, 3)),
                  half, half, half, half, half, half, half, big],
        out_specs=[big, big, pl.BlockSpec((tm, IN_COLS), row), pl.BlockSpec((SUBLANES, D_MODEL), lambda i: (0, 0)),
                   pl.BlockSpec((SUBLANES, ATTN_WIDTH), lambda i: (0, 0)), pl.BlockSpec((SUBLANES, ATTN_WIDTH), lambda i: (0, 0))],
        out_shape=[jax.ShapeDtypeStruct((s, D_MODEL), F32), jax.ShapeDtypeStruct((s, D_MODEL), BF16),
                   jax.ShapeDtypeStruct((s, IN_COLS), BF16), jax.ShapeDtypeStruct((SUBLANES, D_MODEL), F32),
                   jax.ShapeDtypeStruct((SUBLANES, ATTN_WIDTH), F32), jax.ShapeDtypeStruct((SUBLANES, ATTN_WIDTH), F32)],
        compiler_params=_cparams(1),
    )(h, g1, w_in_all, qg, kg, _head_ones(), proj, proj, du_a, du_b, dgs, dq, dk, dv, dga, dh1)


SMALL_NAMES = ("mix_norm_g", "ssm_a_re", "ssm_a_im", "ssm_log_dt", "ssm_b_re", "ssm_b_im", "ssm_c_re", "ssm_c_im",
               "ssm_d", "ssm_b_glu", "q_norm_g", "k_norm_g", "ple_norm_g")
BIG_NAMES = ("w_in", "ssm_w_glu", "w_out", "w_ple_gate", "w_ple_proj")


def _ssm_setup(sm, layer):
    col = lambda a: a[layer].reshape(N_STATES, 1)
    a_re, a_im = col(sm["ssm_a_re"]), col(sm["ssm_a_im"])
    log_dt = jnp.repeat(sm["ssm_log_dt"][layer], SSM_STATE).reshape(N_STATES, 1)
    b_re = sm["ssm_b_re"][layer].reshape(N_STATES, SSM_GROUP)
    b_im = sm["ssm_b_im"][layer].reshape(N_STATES, SSM_GROUP)
    disc_in = (a_re, a_im, log_dt, b_re, b_im)
    ab_re, ab_im, bb_re, bb_im = _disc_fwd(*disc_in)
    wb = jnp.concatenate([_block_diag_in(bb_re), _block_diag_in(bb_im)], axis=-1)
    wc = jnp.concatenate([_block_diag_out(sm["ssm_c_re"][layer]), -_block_diag_out(sm["ssm_c_im"][layer])], axis=1)
    return dict(disc_in=disc_in, wb=wb.astype(BF16), wbt=wb.transpose(0, 2, 1).astype(BF16),
                wc=wc.astype(BF16), wct=wc.transpose(0, 2, 1).astype(BF16),
                coef=_scan_coefs(ab_re, ab_im, False), coef_rev=_scan_coefs(ab_re, ab_im, True))


def _local_step(x, p, target, sm, wg):
    tile8 = lambda a: jnp.tile(a, ATTN_WIDTH // HEAD_DIM).reshape(1, ATTN_WIDTH)
    saved = []
    h = x
    for l in range(N_LAYERS):
        ssm = _ssm_setup(sm, l)
        g1 = sm["mix_norm_g"][l].reshape(1, D_MODEL)
        g2 = sm["ple_norm_g"][l].reshape(1, D_MODEL)
        qg, kg = tile8(sm["q_norm_g"][l]), tile8(sm["k_norm_g"][l])
        dsk = sm["ssm_d"][l].reshape(1, SSM_WIDTH)
        bgl = sm["ssm_b_glu"][l].reshape(1, 2 * SSM_WIDTH)
        proj, qkv = _in_proj(h, g1, wg["w_in"], l, qg, kg)
        xs, y = _ssm_scan_fwd(proj, ssm["wb"], ssm["coef"], ssm["wc"])
        ys = _ssm_glu_fwd(y, proj, dsk, wg["ssm_w_glu"], l, bgl)
        o, ya = _attn_fwd(qkv, proj)
        h1, h2 = _out_ple(h, ys, ya, p[l], g2, wg["w_out"], wg["w_ple_gate"], wg["w_ple_proj"], l)
        saved.append(dict(ssm=ssm, g1=g1, g2=g2, qg=qg, kg=kg, dsk=dsk, bgl=bgl, h=h, proj=proj, qkv=qkv, xs=xs, y=y,
                          ys=ys, o=o, ya=ya, h1=h1))
        h = h2
    dh, sq = _loss_grad(h, target)
    loss = 0.5 * jnp.sum(sq) / D_MODEL

    gbig = {n: [None] * N_LAYERS for n in BIG_NAMES}
    gsm = {n: [None] * N_LAYERS for n in SMALL_NAMES}
    for l in reversed(range(N_LAYERS)):
        sv = saved[l]
        ssm = sv["ssm"]
        dh1, dmix, hn2b, dgpb, dppb, dh1b, dg2 = _out_ple_bwd(dh, sv["h1"], p[l], sv["g2"], wg["w_out"],
                                                              wg["w_ple_gate"], wg["w_ple_proj"], l)
        gsm["ple_norm_g"][l] = dg2.sum(0)
        gbig["w_ple_proj"][l] = _tn_matmul(p[l], dppb, N_CHIPS, False, "dw_ple_proj")
        gbig["w_ple_gate"][l] = _tn_matmul(hn2b, dgpb, N_CHIPS, True, "dw_ple_gate")
        dwo_s = _tn_matmul(sv["ys"], dh1b, 2, True, "dw_out_ssm")
        dwo_a = _tn_matmul(sv["ya"], dh1b, 2, True, "dw_out_attn")
        gbig["w_out"][l] = jnp.concatenate([dwo_s, dwo_a], axis=0)
        dqs, dkn, dv, dga = _attn_bwd(sv["qkv"], sv["o"], sv["proj"], dmix)
        dyf, du_a, dgs, zb, dzzb, dd, dbg = _ssm_glu_bwd(dmix, sv["y"], sv["proj"], sv["dsk"], wg["ssm_w_glu"], l, sv["bgl"])
        gsm["ssm_d"][l] = dd.sum(0).reshape(SSM_GROUPS, SSM_GROUP)
        gsm["ssm_b_glu"][l] = dbg.sum(0)
        gbig["ssm_w_glu"][l] = _tn_matmul(zb, dzzb, N_CHIPS, False, "dw_glu")
        du_b, dwc, dwb, da = _ssm_scan_bwd(dyf, sv["xs"], sv["proj"], ssm["wct"], ssm["coef_rev"], ssm["wbt"])
        gsm["ssm_c_re"][l] = _block_diag_out_t(dwc[:, 0:CH_S, :])
        gsm["ssm_c_im"][l] = -_block_diag_out_t(dwc[:, CH_S:, :])
        da = da.sum(1)
        g_ab_re = da[:, 0:CH_S].reshape(N_STATES, 1)
        g_ab_im = da[:, CH_S:].reshape(N_STATES, 1)
        g_bb_re = _block_diag_in_t(dwb[:, :, 0:CH_S])
        g_bb_im = _block_diag_in_t(dwb[:, :, CH_S:])
        d_are, d_aim, d_ldt, d_bre, d_bim = _disc_bwd(*ssm["disc_in"], g_ab_re, g_ab_im, g_bb_re, g_bb_im)
        gsm["ssm_a_re"][l] = d_are.reshape(SSM_GROUPS, SSM_STATE)
        gsm["ssm_a_im"][l] = d_aim.reshape(SSM_GROUPS, SSM_STATE)
        gsm["ssm_log_dt"][l] = d_ldt.reshape(SSM_GROUPS, SSM_STATE).sum(1)
        gsm["ssm_b_re"][l] = d_bre.reshape(SSM_GROUPS, SSM_STATE, SSM_GROUP)
        gsm["ssm_b_im"][l] = d_bim.reshape(SSM_GROUPS, SSM_STATE, SSM_GROUP)
        dh, hnb, dprojb, dg1, dqg, dkg = _in_proj_bwd(sv["h"], sv["g1"], wg["w_in"], l, sv["qg"], sv["kg"], sv["proj"],
                                                      du_a, du_b, dgs, dqs, dkn, dv, dga, dh1)
        gsm["mix_norm_g"][l] = dg1.sum(0)
        gsm["q_norm_g"][l] = dqg.sum(0).reshape(-1, HEAD_DIM).sum(0)
        gsm["k_norm_g"][l] = dkg.sum(0).reshape(-1, HEAD_DIM).sum(0)
        gbig["w_in"][l] = _tn_matmul(hnb, dprojb, N_CHIPS, False, "dw_in")
    gbig = {n: jnp.stack(v, 0) for n, v in gbig.items()}
    gsm = {n: jnp.stack(v, 0) for n, v in gsm.items()}
    return loss, dh, gbig, gsm


_SMALL_PAD = 8 * 8 * 128


def _pack_small(d):
    flat = jnp.concatenate([d[n].reshape(-1) for n in SMALL_NAMES])
    n = flat.shape[0]
    padded = -(-n // _SMALL_PAD) * _SMALL_PAD
    return jnp.pad(flat, (0, padded - n))


def _unpack_small(flat, like):
    out, off = {}, 0
    for n in SMALL_NAMES:
        size = like[n].size
        out[n] = flat[off:off + size].reshape(like[n].shape)
        off += size
    return out


def _reduce_grads(parts, wire_dtypes):
    n = len(parts)
    flat = [a.reshape(2, -1, a.shape[-1]) for a in parts]
    recv = _sibling_send_other_half(flat, "grad_sibling_send")
    chip = [_add_half(flat[k], recv[k], wire_dtypes[k], "grad_sibling_add").reshape(parts[k].shape[1:])
            for k in range(n)]
    got = _chip_scatter(chip, "grad_chip_scatter")
    tot = [_sum4(got[k], "grad_chip_sum") for k in range(n)]
    return _sibling_join_halves(tot, "grad_sibling_join")


def kernel(x, p, mix_norm_g, w_in, ssm_a_re, ssm_a_im, ssm_log_dt, ssm_b_re, ssm_b_im, ssm_c_re, ssm_c_im, ssm_d, ssm_w_glu, ssm_b_glu, q_norm_g, k_norm_g, w_out, ple_norm_g, w_ple_gate, w_ple_proj, loss_target, m_mix_norm_g, m_w_in, m_ssm_a_re, m_ssm_a_im, m_ssm_log_dt, m_ssm_b_re, m_ssm_b_im, m_ssm_c_re, m_ssm_c_im, m_ssm_d, m_ssm_w_glu, m_ssm_b_glu, m_q_norm_g, m_k_norm_g, m_w_out, m_ple_norm_g, m_w_ple_gate, m_w_ple_proj, v_mix_norm_g, v_w_in, v_ssm_a_re, v_ssm_a_im, v_ssm_log_dt, v_ssm_b_re, v_ssm_b_im, v_ssm_c_re, v_ssm_c_im, v_ssm_d, v_ssm_w_glu, v_ssm_b_glu, v_q_norm_g, v_k_norm_g, v_w_out, v_ple_norm_g, v_w_ple_gate, v_w_ple_proj):
    args = dict(locals())
    names = ("mix_norm_g", "w_in", "ssm_a_re", "ssm_a_im", "ssm_log_dt", "ssm_b_re", "ssm_b_im", "ssm_c_re", "ssm_c_im",
             "ssm_d", "ssm_w_glu", "ssm_b_glu", "q_norm_g", "k_norm_g", "w_out", "ple_norm_g", "w_ple_gate", "w_ple_proj")
    w = {n: args[n] for n in names}
    m = {n: args["m_" + n] for n in names}
    v = {n: args["v_" + n] for n in names}

    gathered = _chip_gather([w[n].astype(BF16) for n in BIG_NAMES], "weight_gather")
    wg = dict(zip(BIG_NAMES, gathered))
    sm = {n: w[n] for n in SMALL_NAMES}
    loss, dx, gbig, gsm = _local_step(x[0], p[:, 0], loss_target[0], sm, wg)
    loss = lax.psum(loss, ("x", "y", "c"))

    small = _pack_small(gsm)
    parts = [gbig[n] for n in BIG_NAMES] + [small.reshape(2, N_CHIPS, SUBLANES, -1)]
    red = _reduce_grads(parts, [BF16] * len(BIG_NAMES) + [F32])
    small_mine = red[-1]
    small_all = _chip_gather([small_mine], "small_grad_gather")[0]
    small_tot = small_all.transpose(1, 0, 2, 3).reshape(-1)
    g = dict(zip(BIG_NAMES, [r.reshape(w[n].shape) for r, n in zip(red[:-1], BIG_NAMES)]))
    g.update(_unpack_small(small_tot, sm))

    delta, new_m, new_v = {}, {}, {}
    for n in BIG_NAMES:
        lanes = w[n].shape[-1]
        outs = _adamw(_as_rows(w[n], lanes), _as_rows(g[n], lanes), _as_rows(m[n], lanes), _as_rows(v[n], lanes), "adamw_" + n)
        delta[n], new_m[n], new_v[n] = [o.reshape(w[n].shape) for o in outs]
    packed = [_pack_small(d).reshape(-1, 1024) for d in (sm, {n: g[n] for n in SMALL_NAMES},
                                                         {n: m[n] for n in SMALL_NAMES}, {n: v[n] for n in SMALL_NAMES})]
    outs = _adamw(*packed, "adamw_small")
    for d, o in zip((delta, new_m, new_v), outs):
        d.update(_unpack_small(o.reshape(-1), sm))

    return (loss, dx[None], *[g[n] for n in names], *[delta[n] for n in names],
            *[new_m[n] for n in names], *[new_v[n] for n in names])
```

```python
import functools
import math

import jax
import jax.numpy as jnp
from jax import lax
from jax.experimental import pallas as pl
from jax.experimental.pallas import tpu as pltpu

F32 = jnp.float32
BF16 = jnp.bfloat16

D_MODEL = 1024
N_LAYERS = 2
N_CHIPS = 4
IN_COLS = 3072
IN_SHARD = IN_COLS // N_CHIPS
SSM_WIDTH = 512
SSM_GROUP = 16
SSM_GROUPS = 32
SSM_STATE = 64
N_STATES = SSM_GROUPS * SSM_STATE
SSM_CHUNKS = 4
CH_W = SSM_WIDTH // SSM_CHUNKS
CH_S = N_STATES // SSM_CHUNKS
ATTN_WIDTH = 512
HEAD_DIM = 64
PLE_DIM = 256
ROW_SHARD = 256
RMS_EPS = 1e-6
ATTN_SCALE = HEAD_DIM ** -0.5
ATTN_BLOCK = 128
EXP_ZERO = -104.0
SUBLANES = 8
V7X_VMEM_LIMIT = 52 * 1024 * 1024

ADAM_LR = 0.001
ADAM_B1 = 0.9
ADAM_B2 = 0.999
ADAM_EPS = 1e-08
ADAM_WD = 0.01
ADAM_STEP = 10

MESH = pl.DeviceIdType.MESH
ANY = pl.BlockSpec(memory_space=pl.ANY)


def _cparams(n_grid=0, parallel=0):
    sem = tuple(["parallel"] * parallel + ["arbitrary"] * (n_grid - parallel))
    return pltpu.CompilerParams(dimension_semantics=sem, vmem_limit_bytes=V7X_VMEM_LIMIT)


def _dot(a, b):
    return jnp.dot(a, b, preferred_element_type=F32)


def _dot_nt(a, b):
    return lax.dot_general(a, b, (((1,), (1,)), ((), ())), preferred_element_type=F32)


def _dot_tn(a, b):
    return lax.dot_general(a, b, (((0,), (0,)), ((), ())), preferred_element_type=F32)


def _split_hilo(a):
    hi = a.astype(BF16)
    lo = (a - hi.astype(F32)).astype(BF16)
    return hi, lo


def _dot_hilo(a, b):
    hi, lo = _split_hilo(a)
    return _dot(hi, b) + _dot(lo, b)


def _sigmoid(x):
    return 0.5 * (jnp.tanh(0.5 * x) + 1.0)


_GELU_C = math.sqrt(2.0 / math.pi)


def _gelu(x):
    return 0.5 * x * (1.0 + jnp.tanh(_GELU_C * (x + 0.044715 * (x * x * x))))


def _gelu_grad(x):
    t = jnp.tanh(_GELU_C * (x + 0.044715 * (x * x * x)))
    return 0.5 * (1.0 + t) + 0.5 * x * (1.0 - t * t) * (_GELU_C * (1.0 + 3.0 * 0.044715 * (x * x)))


def _row_tile(s, want):
    for t in range(min(s, want), 7, -1):
        if s % t == 0 and t % SUBLANES == 0:
            return t
    return s


def _coords():
    return lax.axis_index("x"), lax.axis_index("y"), lax.axis_index("c")


def _other_chips(x, y):
    return [(1 - x, y), (x, 1 - y), (1 - x, 1 - y)]


def _remote(src, dst, send_sem, recv_sem, dev):
    return pltpu.make_async_remote_copy(src_ref=src, dst_ref=dst, send_sem=send_sem, recv_sem=recv_sem,
                                        device_id=dev, device_id_type=MESH)


def _set_block(buf, block, index):
    return lax.dynamic_update_index_in_dim(buf, block, index, 0)


def _chip_gather(arrs, name):
    n = len(arrs)

    def body(*refs):
        ins, outs = refs[:n], refs[n:2 * n]
        send_sems, recv_sems, fwd_send, fwd_recv = refs[2 * n:]
        x, y, c = _coords()
        me_chip = 2 * x + y
        chips = _other_chips(x, y)
        sibling = (x, y, 1 - c)
        first = []
        for k in range(n):
            for j, (cx, cy) in enumerate(chips):
                cp = _remote(ins[k].at[c], outs[k].at[me_chip, c], send_sems.at[3 * k + j], recv_sems.at[3 * k + j],
                             (cx, cy, c))
                cp.start()
                first.append(cp)
        passed = []
        for k in range(n):
            for j, (cx, cy) in enumerate(chips):
                blk = outs[k].at[2 * cx + cy, c]
                _remote(blk, blk, send_sems.at[3 * k + j], recv_sems.at[3 * k + j], (cx, cy, c)).wait_recv()
                cp = _remote(blk, blk, fwd_send.at[3 * k + j], fwd_recv.at[3 * k + j], sibling)
                cp.start()
                passed.append(cp)
        for k in range(n):
            for j, (cx, cy) in enumerate(chips):
                blk = outs[k].at[2 * cx + cy, 1 - c]
                _remote(blk, blk, fwd_send.at[3 * k + j], fwd_recv.at[3 * k + j], sibling).wait_recv()
        for cp in first + passed:
            cp.wait_send()

    outs = pl.pallas_call(
        body, name=name,
        out_shape=[jax.ShapeDtypeStruct((N_CHIPS,) + a.shape, a.dtype) for a in arrs],
        in_specs=[ANY] * n, out_specs=[ANY] * n,
        scratch_shapes=[pltpu.SemaphoreType.DMA((3 * n,)), pltpu.SemaphoreType.DMA((3 * n,)),
                        pltpu.SemaphoreType.DMA((3 * n,)), pltpu.SemaphoreType.DMA((3 * n,))],
    )(*arrs)
    me_chip = 2 * lax.axis_index("x") + lax.axis_index("y")
    return [_set_block(o, a, me_chip) for o, a in zip(outs, arrs)]


def _sibling_send_other_half(arrs, name):
    n = len(arrs)

    def body(*refs):
        ins, outs = refs[:n], refs[n:2 * n]
        send_sems, recv_sems = refs[2 * n:]
        x, y, c = _coords()
        cps = [_remote(ins[k].at[1 - c], outs[k], send_sems.at[k], recv_sems.at[k], (x, y, 1 - c)) for k in range(n)]
        for cp in cps:
            cp.start()
        for cp in cps:
            cp.wait_recv()
        for cp in cps:
            cp.wait_send()

    return pl.pallas_call(
        body, name=name,
        out_shape=[jax.ShapeDtypeStruct(a.shape[1:], a.dtype) for a in arrs],
        in_specs=[ANY] * n, out_specs=[ANY] * n,
        scratch_shapes=[pltpu.SemaphoreType.DMA((n,)), pltpu.SemaphoreType.DMA((n,))],
    )(*arrs)


def _sibling_join_halves(arrs, name):
    n = len(arrs)

    def body(*refs):
        ins, outs = refs[:n], refs[n:2 * n]
        send_sems, recv_sems = refs[2 * n:]
        x, y, c = _coords()
        cps = [_remote(ins[k], outs[k].at[c], send_sems.at[k], recv_sems.at[k], (x, y, 1 - c)) for k in range(n)]
        for cp in cps:
            cp.start()
        for k in range(n):
            blk = outs[k].at[1 - c]
            _remote(blk, blk, send_sems.at[k], recv_sems.at[k], (x, y, 1 - c)).wait_recv()
        for cp in cps:
            cp.wait_send()

    outs = pl.pallas_call(
        body, name=name,
        out_shape=[jax.ShapeDtypeStruct((2,) + a.shape, a.dtype) for a in arrs],
        in_specs=[ANY] * n, out_specs=[ANY] * n,
        scratch_shapes=[pltpu.SemaphoreType.DMA((n,)), pltpu.SemaphoreType.DMA((n,))],
    )(*arrs)
    c = lax.axis_index("c")
    return [_set_block(o, a, c) for o, a in zip(outs, arrs)]


def _chip_scatter(arrs, name):
    n = len(arrs)

    def body(*refs):
        ins, outs = refs[:n], refs[n:2 * n]
        send_sems, recv_sems = refs[2 * n:]
        x, y, c = _coords()
        me_chip = 2 * x + y
        chips = _other_chips(x, y)
        cps = []
        for k in range(n):
            for j, (cx, cy) in enumerate(chips):
                cp = _remote(ins[k].at[2 * cx + cy], outs[k].at[me_chip], send_sems.at[3 * k + j],
                             recv_sems.at[3 * k + j], (cx, cy, c))
                cp.start()
                cps.append(cp)
        for k in range(n):
            for j, (cx, cy) in enumerate(chips):
                blk = outs[k].at[2 * cx + cy]
                _remote(blk, blk, send_sems.at[3 * k + j], recv_sems.at[3 * k + j], (cx, cy, c)).wait_recv()
        for cp in cps:
            cp.wait_send()

    outs = pl.pallas_call(
        body, name=name,
        out_shape=[jax.ShapeDtypeStruct(a.shape, a.dtype) for a in arrs],
        in_specs=[ANY] * n, out_specs=[ANY] * n,
        scratch_shapes=[pltpu.SemaphoreType.DMA((3 * n,)), pltpu.SemaphoreType.DMA((3 * n,))],
    )(*arrs)
    me_chip = 2 * lax.axis_index("x") + lax.axis_index("y")
    return [_set_block(o, lax.dynamic_index_in_dim(a, me_chip, 0, keepdims=False), me_chip) for o, a in zip(outs, arrs)]


def _as_rows(a, lanes):
    return a.reshape(-1, lanes)


def _add_half(full, recv, out_dtype, name):
    _, r, cdim = full.shape
    tr = _row_tile(r, 512)

    def body(c_ref, a_ref, b_ref, o_ref):
        o_ref[...] = (a_ref[...] + b_ref[...]).astype(out_dtype)

    c = lax.axis_index("c").astype(jnp.int32).reshape(1)
    return pl.pallas_call(
        body, name=name,
        grid_spec=pltpu.PrefetchScalarGridSpec(
            num_scalar_prefetch=1, grid=(r // tr,),
            in_specs=[pl.BlockSpec((None, tr, cdim), lambda i, c_ref: (c_ref[0], i, 0)),
                      pl.BlockSpec((tr, cdim), lambda i, c_ref: (i, 0))],
            out_specs=pl.BlockSpec((tr, cdim), lambda i, c_ref: (i, 0))),
        out_shape=jax.ShapeDtypeStruct((r, cdim), out_dtype),
        compiler_params=_cparams(1),
    )(c, full, recv)


def _sum4(parts, name):
    _, r, cdim = parts.shape
    tr = _row_tile(r, 512)

    def body(p_ref, o_ref):
        acc = p_ref[0].astype(F32) + p_ref[1].astype(F32)
        acc = acc + p_ref[2].astype(F32)
        o_ref[...] = acc + p_ref[3].astype(F32)

    return pl.pallas_call(
        body, name=name, grid=(r // tr,),
        in_specs=[pl.BlockSpec((N_CHIPS, tr, cdim), lambda i: (0, i, 0))],
        out_specs=pl.BlockSpec((tr, cdim), lambda i: (i, 0)),
        out_shape=jax.ShapeDtypeStruct((r, cdim), F32),
        compiler_params=_cparams(1),
    )(parts)


def _adamw_math(w, g, m, v):
    c1 = 1.0 - ADAM_B1 ** ADAM_STEP
    c2 = 1.0 - ADAM_B2 ** ADAM_STEP
    nm = ADAM_B1 * m + (1.0 - ADAM_B1) * g
    nv = ADAM_B2 * v + (1.0 - ADAM_B2) * (g * g)
    delta = -ADAM_LR * ((nm / c1) / (jnp.sqrt(nv / c2) + ADAM_EPS) + ADAM_WD * w)
    return delta, nm, nv


def _adamw(w, g, m, v, name):
    r, cdim = w.shape
    tr = _row_tile(r, 256)

    def body(w_ref, g_ref, m_ref, v_ref, d_ref, nm_ref, nv_ref):
        d_ref[...], nm_ref[...], nv_ref[...] = _adamw_math(w_ref[...], g_ref[...], m_ref[...], v_ref[...])

    spec = pl.BlockSpec((tr, cdim), lambda i: (i, 0))
    return pl.pallas_call(
        body, name=name, grid=(r // tr,),
        in_specs=[spec] * 4, out_specs=[spec] * 3,
        out_shape=[jax.ShapeDtypeStruct((r, cdim), F32)] * 3,
        compiler_params=_cparams(1),
    )(w, g, m, v)


def _adamw_many(ws, gs, ms, vs, name, per_layer):
    n = len(ws)

    def body(*refs):
        for k in range(n):
            w, g, m, v = (refs[j * n + k][...] for j in range(4))
            outs = _adamw_math(w, g, m, v)
            for j in range(3):
                refs[(4 + j) * n + k][...] = outs[j]

    shapes = [jax.ShapeDtypeStruct(w.shape, F32) for w in ws]
    if per_layer:
        specs = [pl.BlockSpec((None,) + w.shape[1:], lambda l, nd=w.ndim: (l,) + (0,) * (nd - 1)) for w in ws]
        call = pl.pallas_call(body, name=name, grid=(N_LAYERS,), in_specs=specs * 4, out_specs=specs * 3,
                              out_shape=shapes * 3, compiler_params=_cparams(1))
    else:
        call = pl.pallas_call(body, name=name, out_shape=shapes * 3, compiler_params=_cparams())
    outs = call(*ws, *gs, *ms, *vs)
    return outs[0:n], outs[n:2 * n], outs[2 * n:3 * n]


def _discretise(a_re, a_im, log_dt, b_re, b_im):
    dt = jnp.exp(log_dt)
    mag = jnp.exp(a_re * dt)
    ab_re = mag * jnp.cos(a_im * dt)
    ab_im = mag * jnp.sin(a_im * dt)
    num_re = ab_re - 1.0
    num_im = ab_im
    den = a_re * a_re + a_im * a_im
    f_re = (num_re * a_re + num_im * a_im) / den
    f_im = (num_im * a_re - num_re * a_im) / den
    bb_re = f_re * b_re - f_im * b_im
    bb_im = f_re * b_im + f_im * b_re
    return ab_re, ab_im, bb_re, bb_im


def _disc_shapes():
    col = jax.ShapeDtypeStruct((N_STATES, 1), F32)
    mat = jax.ShapeDtypeStruct((N_STATES, SSM_GROUP), F32)
    return col, mat


def _disc_fwd(a_re, a_im, log_dt, b_re, b_im):
    col, mat = _disc_shapes()

    def body(ar, ai, ld, br, bi, o0, o1, o2, o3):
        outs = _discretise(ar[...], ai[...], ld[...], br[...], bi[...])
        for o, val in zip((o0, o1, o2, o3), outs):
            o[...] = val

    return pl.pallas_call(body, name="ssm_discretise", out_shape=[col, col, mat, mat],
                          compiler_params=_cparams())(a_re, a_im, log_dt, b_re, b_im)


def _disc_bwd(a_re, a_im, log_dt, b_re, b_im, g_ab_re, g_ab_im, g_bb_re, g_bb_im):
    col, mat = _disc_shapes()

    def body(ar, ai, ld, br, bi, g0, g1, g2, g3, o0, o1, o2, o3, o4):
        _, vjp = jax.vjp(_discretise, ar[...], ai[...], ld[...], br[...], bi[...])
        grads = vjp((g0[...], g1[...], g2[...], g3[...]))
        for o, val in zip((o0, o1, o2, o3, o4), grads):
            o[...] = val

    return pl.pallas_call(body, name="ssm_discretise_bwd", out_shape=[col, col, col, mat, mat],
                          compiler_params=_cparams())(a_re, a_im, log_dt, b_re, b_im, g_ab_re, g_ab_im, g_bb_re, g_bb_im)


def _cmul(ar, ai, br, bi):
    return ar * br - ai * bi, ar * bi + ai * br


def _scan_coefs(ab_re, ab_im, reverse):
    ar = ab_re.reshape(1, N_STATES)
    ai = ab_im.reshape(1, N_STATES)
    if reverse:
        ai = -ai
    a2 = _cmul(ar, ai, ar, ai)
    a4 = _cmul(*a2, *a2)
    rows = jnp.arange(SUBLANES)[:, None]
    out = []
    for (pr, pi), sh in (((ar, ai), 1), (a2, 2), (a4, 4)):
        keep = (rows <= SUBLANES - 1 - sh) if reverse else (rows >= sh)
        out += [jnp.where(keep, pr, 0.0), jnp.where(keep, pi, 0.0)]
    pows = [(ar, ai)]
    for _ in range(SUBLANES - 1):
        pows.append(_cmul(*pows[-1], ar, ai))
    order = pows[::-1] if reverse else pows
    out += [jnp.concatenate([p[0] for p in order], 0), jnp.concatenate([p[1] for p in order], 0)]
    t = jnp.stack(out, 0)
    return t.reshape(8, SUBLANES, SSM_CHUNKS, CH_S).transpose(2, 0, 1, 3)


def _block_diag_in(bb):
    t = bb.reshape(SSM_CHUNKS, 8, SSM_STATE, SSM_GROUP)
    eye = jnp.eye(8, dtype=bb.dtype)
    return jnp.einsum("jgph,gk->jghkp", t, eye).reshape(SSM_CHUNKS, CH_W, CH_S)


def _block_diag_in_t(d):
    t = d.reshape(SSM_CHUNKS, 8, SSM_GROUP, 8, SSM_STATE)
    return jnp.einsum("jghgp->jgph", t).reshape(N_STATES, SSM_GROUP)


def _block_diag_out(c):
    t = c.reshape(SSM_CHUNKS, 8, SSM_GROUP, SSM_STATE)
    eye = jnp.eye(8, dtype=c.dtype)
    return jnp.einsum("jghp,gk->jgpkh", t, eye).reshape(SSM_CHUNKS, CH_S, CH_W)


def _block_diag_out_t(d):
    t = d.reshape(SSM_CHUNKS, 8, SSM_STATE, 8, SSM_GROUP)
    return jnp.einsum("jgpgh->jghp", t).reshape(SSM_GROUPS, SSM_GROUP, SSM_STATE)


def _head_ones():
    r = jnp.arange(ATTN_WIDTH) // HEAD_DIM
    return jnp.where(r[:, None] == r[None, :], 1.0 / HEAD_DIM, 0.0).astype(BF16)


def _in_proj(h, g1, w_in_all, layer, qg, kg):
    s = h.shape[0]
    tm = _row_tile(s, 256)

    def body(h_ref, g_ref, w_ref, qg_ref, kg_ref, ones_ref, proj_ref, qkv_ref):
        x = h_ref[...]
        r = lax.rsqrt(jnp.mean(x * x, axis=-1, keepdims=True) + RMS_EPS)
        hn = (x * r * g_ref[...]).astype(BF16)
        for sh in range(N_CHIPS):
            proj_ref[:, IN_SHARD * sh:IN_SHARD * (sh + 1)] = _dot(hn, w_ref[sh])
        ones = ones_ref[...]
        q = proj_ref[:, 1024:1536]
        k = proj_ref[:, 1536:2048]
        rq = lax.rsqrt(_dot_hilo(q * q, ones) + RMS_EPS)
        rk = lax.rsqrt(_dot_hilo(k * k, ones) + RMS_EPS)
        qkv_ref[:, 0:512] = (q * rq * qg_ref[...] * ATTN_SCALE).astype(BF16)
        qkv_ref[:, 512:1024] = (k * rk * kg_ref[...]).astype(BF16)
        qkv_ref[:, 1024:1536] = proj_ref[:, 2048:2560].astype(BF16)

    full = lambda shape: pl.BlockSpec(shape, lambda i: (0,) * len(shape))
    return pl.pallas_call(
        body, name="in_proj", grid=(s // tm,),
        in_specs=[pl.BlockSpec((tm, D_MODEL), lambda i: (i, 0)), full((1, D_MODEL)),
                  pl.BlockSpec((N_CHIPS, None, D_MODEL, IN_SHARD), lambda i: (0, layer, 0, 0)),
                  full((1, ATTN_WIDTH)), full((1, ATTN_WIDTH)), full((ATTN_WIDTH, ATTN_WIDTH))],
        out_specs=[pl.BlockSpec((tm, IN_COLS), lambda i: (i, 0)), pl.BlockSpec((tm, 3 * ATTN_WIDTH), lambda i: (i, 0))],
        out_shape=[jax.ShapeDtypeStruct((s, IN_COLS), F32), jax.ShapeDtypeStruct((s, 3 * ATTN_WIDTH), BF16)],
        compiler_params=_cparams(1),
    )(h, g1, w_in_all, qg, kg, _head_ones())


def _scan_rows(x_ref, coef_ref, carry_ref, n_blocks, reverse, extra=None):
    c = [coef_ref[a] for a in range(8)]
    shifts = (7, 6, 4) if reverse else (1, 2, 4)
    edge = 0 if reverse else SUBLANES - 1

    def blk(b, carry):
        bb = (n_blocks - 1 - b) if reverse else b
        r0 = pl.multiple_of(bb * SUBLANES, SUBLANES)
        xr = x_ref[pl.ds(r0, SUBLANES), 0:CH_S]
        xi = x_ref[pl.ds(r0, SUBLANES), CH_S:2 * CH_S]
        for lvl, sh in enumerate(shifts):
            ar, ai = c[2 * lvl], c[2 * lvl + 1]
            sr = pltpu.roll(xr, sh, 0)
            si = pltpu.roll(xi, sh, 0)
            xr, xi = xr + (ar * sr - ai * si), xi + (ar * si + ai * sr)
        cr, ci = carry
        xr, xi = xr + (c[6] * cr - c[7] * ci), xi + (c[6] * ci + c[7] * cr)
        x_ref[pl.ds(r0, SUBLANES), 0:CH_S] = xr
        x_ref[pl.ds(r0, SUBLANES), CH_S:2 * CH_S] = xi
        if extra is not None:
            extra(r0, xr, xi, cr, ci)
        return (jnp.broadcast_to(xr[edge:edge + 1, :], (SUBLANES, CH_S)),
                jnp.broadcast_to(xi[edge:edge + 1, :], (SUBLANES, CH_S)))

    cr, ci = lax.fori_loop(0, n_blocks, blk, (carry_ref[:, 0:CH_S], carry_ref[:, CH_S:2 * CH_S]))
    carry_ref[:, 0:CH_S] = cr
    carry_ref[:, CH_S:2 * CH_S] = ci


def _ssm_scan_fwd(proj, wb, coef, wc):
    s = proj.shape[0]
    tm = _row_tile(s, 512)

    def body(u_ref, wb_ref, coef_ref, wc_ref, xs_ref, y_ref, carry_ref):
        @pl.when(pl.program_id(1) == 0)
        def _():
            carry_ref[...] = jnp.zeros_like(carry_ref)

        xs_ref[...] = _dot(u_ref[...].astype(BF16), wb_ref[...])
        _scan_rows(xs_ref, coef_ref, carry_ref, tm // SUBLANES, reverse=False)
        y_ref[...] = _dot(xs_ref[...].astype(BF16), wc_ref[...])

    return pl.pallas_call(
        body, name="ssm_scan", grid=(SSM_CHUNKS, s // tm),
        in_specs=[pl.BlockSpec((tm, CH_W), lambda j, i: (i, j)),
                  pl.BlockSpec((None, CH_W, 2 * CH_S), lambda j, i: (j, 0, 0)),
                  pl.BlockSpec((None, 8, SUBLANES, CH_S), lambda j, i: (j, 0, 0, 0)),
                  pl.BlockSpec((None, 2 * CH_S, CH_W), lambda j, i: (j, 0, 0))],
        out_specs=[pl.BlockSpec((None, tm, 2 * CH_S), lambda j, i: (j, i, 0)),
                   pl.BlockSpec((tm, CH_W), lambda j, i: (i, j))],
        out_shape=[jax.ShapeDtypeStruct((SSM_CHUNKS, s, 2 * CH_S), F32), jax.ShapeDtypeStruct((s, SSM_WIDTH), F32)],
        scratch_shapes=[pltpu.VMEM((SUBLANES, 2 * CH_S), F32)],
        compiler_params=_cparams(2),
    )(proj, wb, coef, wc)


def _glu_forward(y, u, d, wg_ref, bg):
    yf = y + d * u
    z = _gelu(yf)
    zb = z.astype(BF16)
    zz = jnp.concatenate([_dot(zb, wg_ref[sh]) for sh in range(N_CHIPS)], axis=-1) + bg
    return yf, z, zz[:, 0:SSM_WIDTH], zz[:, SSM_WIDTH:2 * SSM_WIDTH]


def _ssm_glu_fwd(y, proj, d, w_glu_all, layer, b_glu):
    s = y.shape[0]
    tm = _row_tile(s, 512)

    def body(y_ref, u_ref, gs_ref, d_ref, wg_ref, bg_ref, o_ref):
        _, _, val, gate = _glu_forward(y_ref[...], u_ref[...], d_ref[...], wg_ref, bg_ref[...])
        gs = gs_ref[...]
        o_ref[...] = val * _sigmoid(gate) * (gs * _sigmoid(gs))

    row = lambda i: (i, 0)
    return pl.pallas_call(
        body, name="ssm_glu", grid=(s // tm,),
        in_specs=[pl.BlockSpec((tm, SSM_WIDTH), row), pl.BlockSpec((tm, SSM_WIDTH), row),
                  pl.BlockSpec((tm, SSM_WIDTH), lambda i: (i, 1)), pl.BlockSpec((1, SSM_WIDTH), lambda i: (0, 0)),
                  pl.BlockSpec((N_CHIPS, None, SSM_WIDTH, ROW_SHARD), lambda i: (0, layer, 0, 0)),
                  pl.BlockSpec((1, 2 * SSM_WIDTH), lambda i: (0, 0))],
        out_specs=pl.BlockSpec((tm, SSM_WIDTH), row),
        out_shape=jax.ShapeDtypeStruct((s, SSM_WIDTH), F32),
        compiler_params=_cparams(1),
    )(y, proj, proj, d, w_glu_all, b_glu)


def _tri(kind):
    r = jnp.arange(ATTN_BLOCK)
    if kind == "suffix_incl":
        m = r[:, None] >= r[None, :]
    else:
        m = r[:, None] < r[None, :]
    return jnp.concatenate([m, jnp.ones_like(m)], axis=1).astype(BF16)


def _head_masks():
    lane = lax.broadcasted_iota(jnp.int32, (1, 2 * HEAD_DIM), 1)
    return [lane < HEAD_DIM, lane >= HEAD_DIM]


def _chain_step(t, base, n_sub, first, q_ref, k_ref, tri_ref, l_scr, per_chain):
    tb = ATTN_BLOCK
    row = lax.broadcasted_iota(jnp.int32, (tb, tb), 0)
    col = lax.broadcasted_iota(jnp.int32, (tb, tb), 1)
    masks = _head_masks()
    blks = [base + a - t for a in range(n_sub)]
    r0s = [pl.multiple_of(jnp.maximum(blk, 0) * tb, tb) for blk in blks]
    zs = []
    for a in range(n_sub):
        kb = k_ref[pl.ds(r0s[a], tb), :]
        qa = q_ref[a * tb:(a + 1) * tb, :]
        for mask in masks:
            zs.append(_dot_nt(jnp.where(mask, qa, jnp.zeros_like(qa)), kb))
    sps, lss, parts = [], [], []
    for z in zs:
        sp = jnp.maximum(z, 0.0) + jnp.log1p(jnp.exp(-jnp.abs(z)))
        ls = -sp
        if first:
            ls = jnp.where(col < row, ls, 0.0)
        sps.append(sp)
        lss.append(ls)
        parts.append(_split_hilo(ls))
    tri = tri_ref[...]
    sums = [_dot(hi, tri) + _dot(lo, tri) for hi, lo in parts]
    top = None
    ws = []
    for c, (z, sp, ls, sm) in enumerate(zip(zs, sps, lss, sums)):
        if first:
            lsum = jnp.zeros((tb, tb), F32)
        else:
            lsum = l_scr[c] + jnp.where(blks[c // 2] >= 0, 0.0, -1e30)
        w = jnp.exp((z - sp) + (sm[:, 0:tb] - ls) + lsum)
        if first:
            w = jnp.where(col < row, w, 0.0)
        ws.append(w)
        lsum = lsum + sm[:, tb:2 * tb]
        l_scr[c] = lsum
        top = lsum if top is None else jnp.maximum(top, lsum)
    for c, (z, w) in enumerate(zip(zs, ws)):
        per_chain(c // 2, c % 2, c, r0s[c // 2], z, w)
    return jnp.max(top)


def _chain_sweep(base, n_sub, q_ref, k_ref, tri_ref, l_scr, per_chain):
    top = _chain_step(0, base, n_sub, True, q_ref, k_ref, tri_ref, l_scr, functools.partial(per_chain, 0))

    def cond(carry):
        t, top = carry
        return jnp.logical_and(t <= base + n_sub - 1, top > EXP_ZERO)

    def step(carry):
        t, _ = carry
        return t + 1, _chain_step(t, base, n_sub, False, q_ref, k_ref, tri_ref, l_scr, functools.partial(per_chain, t))

    steps, _ = lax.while_loop(cond, step, (jnp.int32(1), top))
    return steps


ATTN_SUB_FWD = 4
ATTN_SUB_BWD = 4


def _attn_fwd(qkv, proj):
    s = qkv.shape[0]
    tb = ATTN_BLOCK
    n_sub = min(ATTN_SUB_FWD, s // tb)
    tq = n_sub * tb

    def body(q_ref, k_ref, v_ref, g_ref, tri_ref, o_ref, ya_ref, l_scr):
        i = pl.program_id(1)
        masks = _head_masks()
        o_ref[...] = jnp.zeros_like(o_ref)

        def per_chain(t, a, h, c, r0, z, w):
            vb = v_ref[pl.ds(r0, tb), :]
            vb = jnp.where(masks[h], vb, jnp.zeros_like(vb))
            o_ref[a * tb:(a + 1) * tb, :] += _dot(w.astype(BF16), vb)

        _chain_sweep(i * n_sub, n_sub, q_ref, k_ref, tri_ref, l_scr, per_chain)
        g = g_ref[...]
        ya_ref[...] = o_ref[...] * (g * _sigmoid(g))

    hp_blk = lambda off: pl.BlockSpec((tq, 2 * HEAD_DIM), lambda hp, i: (i, off + hp))
    res = lambda off: pl.BlockSpec((s, 2 * HEAD_DIM), lambda hp, i: (0, off + hp))
    return pl.pallas_call(
        body, name="attn_fwd", grid=(ATTN_WIDTH // (2 * HEAD_DIM), s // tq),
        in_specs=[hp_blk(0), res(4), res(8), hp_blk(20), pl.BlockSpec((tb, 2 * tb), lambda hp, i: (0, 0))],
        out_specs=[hp_blk(0), hp_blk(0)],
        out_shape=[jax.ShapeDtypeStruct((s, ATTN_WIDTH), F32)] * 2,
        scratch_shapes=[pltpu.VMEM((2 * n_sub, tb, tb), F32)],
        compiler_params=_cparams(2),
    )(qkv, qkv, qkv, proj, _tri("suffix_incl"))


def _rms_rows(x, g):
    r = lax.rsqrt(jnp.mean(x * x, axis=-1, keepdims=True) + RMS_EPS)
    return r, x * r * g


def _ple_forward(h1, p, g2, wpg_ref, wpp_ref):
    r2, hn2 = _rms_rows(h1, g2)
    hb = hn2.astype(BF16)
    gpre = _dot(hb[:, 0:ROW_SHARD], wpg_ref[0])
    for sh in range(1, N_CHIPS):
        gpre = gpre + _dot(hb[:, ROW_SHARD * sh:ROW_SHARD * (sh + 1)], wpg_ref[sh])
    gate = _sigmoid(gpre)
    pb = p.astype(BF16)
    pp = jnp.concatenate([_dot(pb, wpp_ref[sh]) for sh in range(N_CHIPS)], axis=-1)
    return r2, hb, gate, pp


def _out_ple(h, ys, ya, p, g2, w_out_all, w_pg_all, w_pp_all, layer):
    s = h.shape[0]
    tm = _row_tile(s, 256)

    def body(h_ref, ys_ref, ya_ref, p_ref, g_ref, wo_ref, wpg_ref, wpp_ref, h1_ref, h2_ref):
        ysb = ys_ref[...].astype(BF16)
        yab = ya_ref[...].astype(BF16)
        h1 = h_ref[...]
        for sh, src in enumerate((ysb[:, 0:ROW_SHARD], ysb[:, ROW_SHARD:], yab[:, 0:ROW_SHARD], yab[:, ROW_SHARD:])):
            h1 = h1 + _dot(src, wo_ref[sh])
        _, _, gate, pp = _ple_forward(h1, p_ref[...], g_ref[...], wpg_ref, wpp_ref)
        h1_ref[...] = h1
        h2_ref[...] = h1 + gate * pp

    row = lambda i: (i, 0)
    wspec = lambda r, cdim: pl.BlockSpec((N_CHIPS, None, r, cdim), lambda i: (0, layer, 0, 0))
    return pl.pallas_call(
        body, name="out_ple", grid=(s // tm,),
        in_specs=[pl.BlockSpec((tm, D_MODEL), row), pl.BlockSpec((tm, SSM_WIDTH), row), pl.BlockSpec((tm, ATTN_WIDTH), row),
                  pl.BlockSpec((tm, PLE_DIM), row), pl.BlockSpec((1, D_MODEL), lambda i: (0, 0)),
                  wspec(ROW_SHARD, D_MODEL), wspec(ROW_SHARD, D_MODEL), wspec(PLE_DIM, ROW_SHARD)],
        out_specs=[pl.BlockSpec((tm, D_MODEL), row)] * 2,
        out_shape=[jax.ShapeDtypeStruct((s, D_MODEL), F32)] * 2,
        compiler_params=_cparams(1),
    )(h, ys, ya, p, g2, w_out_all, w_pg_all, w_pp_all)


def _loss_grad(y, target):
    s = y.shape[0]
    tm = _row_tile(s, 512)

    def body(y_ref, t_ref, dy_ref, acc_ref):
        @pl.when(pl.program_id(0) == 0)
        def _():
            acc_ref[...] = jnp.zeros_like(acc_ref)

        e = y_ref[...] - t_ref[...]
        dy_ref[...] = e / D_MODEL
        sq = (e * e).reshape(tm // SUBLANES, SUBLANES, D_MODEL).sum(axis=0)
        part = sq[:, 0:128]
        for b in range(1, D_MODEL // 128):
            part = part + sq[:, 128 * b:128 * (b + 1)]
        acc_ref[...] += part

    row = lambda i: (i, 0)
    return pl.pallas_call(
        body, name="loss_grad", grid=(s // tm,),
        in_specs=[pl.BlockSpec((tm, D_MODEL), row)] * 2,
        out_specs=[pl.BlockSpec((tm, D_MODEL), row), pl.BlockSpec((SUBLANES, 128), lambda i: (0, 0))],
        out_shape=[jax.ShapeDtypeStruct((s, D_MODEL), F32), jax.ShapeDtypeStruct((SUBLANES, 128), F32)],
        compiler_params=_cparams(1),
    )(y, target)


def _rms_bwd(x, r, g, dy):
    gdy = g * dy
    dx = r * gdy - x * (r * r * r) * jnp.mean(x * gdy, axis=-1, keepdims=True)
    return dx, x * r * dy


def _colsum8(a):
    t = a.shape[0]
    return a.reshape(t // SUBLANES, SUBLANES, a.shape[1]).sum(axis=0)


def _out_ple_bwd(dh2, h1, p, g2, w_out_all, w_pg_all, w_pp_all, layer):
    s = h1.shape[0]
    tm = _row_tile(s, 256)

    def body(dh2_ref, h1_ref, p_ref, g_ref, wo_ref, wpg_ref, wpp_ref,
             dh1_ref, dmix_ref, hn_ref, dgp_ref, dpp_ref, dh1b_ref, dg_ref):
        @pl.when(pl.program_id(0) == 0)
        def _():
            dg_ref[...] = jnp.zeros_like(dg_ref)

        h1 = h1_ref[...]
        dh2 = dh2_ref[...]
        g2v = g_ref[...]
        r2, hb, gate, pp = _ple_forward(h1, p_ref[...], g2v, wpg_ref, wpp_ref)
        dgp = (dh2 * pp) * gate * (1.0 - gate)
        dgpb = dgp.astype(BF16)
        dhn = jnp.concatenate([_dot_nt(dgpb, wpg_ref[sh]) for sh in range(N_CHIPS)], axis=-1)
        dx, dgrow = _rms_bwd(h1, r2, g2v, dhn)
        dh1 = dh2 + dx
        dh1b = dh1.astype(BF16)
        dh1_ref[...] = dh1
        dh1b_ref[...] = dh1b
        hn_ref[...] = hb
        dgp_ref[...] = dgpb
        dpp_ref[...] = (dh2 * gate).astype(BF16)
        dg_ref[...] += _colsum8(dgrow)
        for sh in range(N_CHIPS):
            dmix_ref[:, ROW_SHARD * sh:ROW_SHARD * (sh + 1)] = _dot_nt(dh1b, wo_ref[sh])

    row = lambda i: (i, 0)
    wspec = lambda r, cdim: pl.BlockSpec((N_CHIPS, None, r, cdim), lambda i: (0, layer, 0, 0))
    big = pl.BlockSpec((tm, D_MODEL), row)
    return pl.pallas_call(
        body, name="out_ple_bwd", grid=(s // tm,),
        in_specs=[big, big, pl.BlockSpec((tm, PLE_DIM), row), pl.BlockSpec((1, D_MODEL), lambda i: (0, 0)),
                  wspec(ROW_SHARD, D_MODEL), wspec(ROW_SHARD, D_MODEL), wspec(PLE_DIM, ROW_SHARD)],
        out_specs=[big] * 6 + [pl.BlockSpec((SUBLANES, D_MODEL), lambda i: (0, 0))],
        out_shape=[jax.ShapeDtypeStruct((s, D_MODEL), F32)] * 2 + [jax.ShapeDtypeStruct((s, D_MODEL), BF16)] * 4
        + [jax.ShapeDtypeStruct((SUBLANES, D_MODEL), F32)],
        compiler_params=_cparams(1),
    )(dh2, h1, p, g2, w_out_all, w_pg_all, w_pp_all)


def _tn_matmul(a, b, n_blocks, block_a, name, layer, into=None, first_block=0, total_blocks=None):
    s = a.shape[0]
    tk = _row_tile(s, 512)
    total_blocks = n_blocks if total_blocks is None else total_blocks
    ka, nb = a.shape[1], b.shape[1]
    if block_a:
        ka //= n_blocks
    else:
        nb //= n_blocks

    def body(*refs):
        a_ref, b_ref, o_ref = refs[0], refs[1], refs[-1]

        @pl.when(pl.program_id(1) == 0)
        def _():
            o_ref[...] = jnp.zeros_like(o_ref)

        o_ref[...] += _dot_tn(a_ref[...].astype(BF16), b_ref[...].astype(BF16))

    a_map = (lambda sh, i: (i, sh)) if block_a else (lambda sh, i: (i, 0))
    b_map = (lambda sh, i: (i, 0)) if block_a else (lambda sh, i: (i, sh))
    in_specs = [pl.BlockSpec((tk, ka), a_map), pl.BlockSpec((tk, nb), b_map)]
    operands = [a, b]
    aliases = {}
    if into is not None:
        in_specs.append(ANY)
        operands.append(into)
        aliases = {2: 0}
    return pl.pallas_call(
        body, name=name, grid=(n_blocks, s // tk),
        in_specs=in_specs,
        out_specs=pl.BlockSpec((None, None, ka, nb), lambda sh, i: (layer, first_block + sh, 0, 0)),
        out_shape=jax.ShapeDtypeStruct((N_LAYERS, total_blocks, ka, nb), F32),
        input_output_aliases=aliases,
        compiler_params=_cparams(2),
    )(*operands)


def _attn_bwd(qkv, o, proj, dmix):
    s = qkv.shape[0]
    tb = ATTN_BLOCK
    nq = s // tb
    n_sub = min(ATTN_SUB_BWD, nq)
    tq = n_sub * tb
    n_chain = 2 * n_sub

    def body(q_ref, k_ref, v_ref, o_ref, g_ref, dya_ref, tri_s_ref, tri_p_ref,
             dq_ref, dk_ref, dv_ref, dg_ref, do_scr, l_scr, g_scr, s_scr, w_scr):
        i = pl.program_id(1)
        base = i * n_sub

        @pl.when(i == 0)
        def _():
            dk_ref[...] = jnp.zeros_like(dk_ref)
            dv_ref[...] = jnp.zeros_like(dv_ref)

        g = g_ref[...]
        sg = _sigmoid(g)
        dya = dya_ref[...]
        do_scr[...] = (dya * (g * sg)).astype(BF16)
        dg_ref[...] = dya * o_ref[...] * (sg * (1.0 + g * (1.0 - sg)))
        dq_ref[...] = jnp.zeros_like(dq_ref)
        g_scr[...] = jnp.zeros_like(g_scr)
        masks = _head_masks()

        def keep(t, a, h, c, r0, z, w):
            s_scr[c, t] = _sigmoid(z).astype(BF16)
            w_scr[c, t] = w.astype(BF16)

        steps = _chain_sweep(base, n_sub, q_ref, k_ref, tri_s_ref, l_scr, keep)
        row = lax.broadcasted_iota(jnp.int32, (tb, tb), 0)
        col = lax.broadcasted_iota(jnp.int32, (tb, tb), 1)

        def back(it, carry):
            t = steps - 1 - it
            r0s = [pl.multiple_of(jnp.maximum(base + a - t, 0) * tb, tb) for a in range(n_sub)]
            qhs, dohs, khs, gws = [], [], [], []
            for a in range(n_sub):
                kb = k_ref[pl.ds(r0s[a], tb), :]
                vb = v_ref[pl.ds(r0s[a], tb), :]
                qa = q_ref[a * tb:(a + 1) * tb, :]
                doa = do_scr[a * tb:(a + 1) * tb, :]
                for h, mask in enumerate(masks):
                    qhs.append(jnp.where(mask, qa, jnp.zeros_like(qa)))
                    khs.append(jnp.where(mask, kb, jnp.zeros_like(kb)))
                    dohs.append(jnp.where(mask, doa, jnp.zeros_like(doa)))
                    gws.append(w_scr[2 * a + h, t].astype(F32) * _dot_nt(dohs[-1], vb))
            parts = [_split_hilo(gw) for gw in gws]
            tri = tri_p_ref[...]
            sums = [_dot(hi, tri) + _dot(lo, tri) for hi, lo in parts]
            dzs = []
            for c, (gw, sm) in enumerate(zip(gws, sums)):
                gsum = g_scr[c]
                dz = gw - (gw + sm[:, 0:tb] + gsum) * s_scr[c, t].astype(F32)
                dz = jnp.where(col < row + t * tb, dz, 0.0)
                g_scr[c] = gsum + sm[:, tb:2 * tb]
                dzs.append(dz.astype(BF16))
            for c, dzb in enumerate(dzs):
                a = c // 2
                dk_ref[pl.ds(r0s[a], tb), :] += _dot_tn(dzb, qhs[c])
                dv_ref[pl.ds(r0s[a], tb), :] += _dot_tn(w_scr[c, t], dohs[c])
                dq_ref[a * tb:(a + 1) * tb, :] += _dot(dzb, khs[c])
            return carry

        lax.fori_loop(0, steps, back, 0)

    hp_blk = lambda off: pl.BlockSpec((tq, 2 * HEAD_DIM), lambda hp, i: (i, off + hp))
    res = lambda off: pl.BlockSpec((s, 2 * HEAD_DIM), lambda hp, i: (0, off + hp))
    tri = pl.BlockSpec((tb, 2 * tb), lambda hp, i: (0, 0))
    return pl.pallas_call(
        body, name="attn_bwd", grid=(ATTN_WIDTH // (2 * HEAD_DIM), s // tq),
        in_specs=[hp_blk(0), res(4), res(8), hp_blk(0), hp_blk(20), hp_blk(4), tri, tri],
        out_specs=[hp_blk(0), res(0), res(0), hp_blk(0)],
        out_shape=[jax.ShapeDtypeStruct((s, ATTN_WIDTH), F32)] * 4,
        scratch_shapes=[pltpu.VMEM((tq, 2 * HEAD_DIM), BF16), pltpu.VMEM((n_chain, tb, tb), F32),
                        pltpu.VMEM((n_chain, tb, tb), F32), pltpu.VMEM((n_chain, nq, tb, tb), BF16),
                        pltpu.VMEM((n_chain, nq, tb, tb), BF16)],
        compiler_params=_cparams(2),
    )(qkv, qkv, qkv, o, proj, dmix, _tri("suffix_incl"), _tri("prefix_strict"))


def _ssm_glu_bwd(dmix, y, proj, d, w_glu_all, layer, b_glu):
    s = y.shape[0]
    tm = _row_tile(s, 512)

    def body(dys_ref, y_ref, u_ref, gs_ref, d_ref, wg_ref, bg_ref,
             dyf_ref, du_ref, dgs_ref, z_ref, dzz_ref, dd_ref, db_ref):
        @pl.when(pl.program_id(0) == 0)
        def _():
            dd_ref[...] = jnp.zeros_like(dd_ref)
            db_ref[...] = jnp.zeros_like(db_ref)

        u = u_ref[...]
        dv = d_ref[...]
        yf, z, val, gate = _glu_forward(y_ref[...], u, dv, wg_ref, bg_ref[...])
        gs = gs_ref[...]
        sgs = _sigmoid(gs)
        sgate = _sigmoid(gate)
        dys = dys_ref[...]
        dgv = dys * (gs * sgs)
        dgs_ref[...] = dys * (val * sgate) * (sgs * (1.0 + gs * (1.0 - sgs)))
        dzz = jnp.concatenate([dgv * sgate, dgv * val * sgate * (1.0 - sgate)], axis=-1)
        dzzb = dzz.astype(BF16)
        dz = _dot_nt(dzzb[:, 0:ROW_SHARD], wg_ref[0])
        for sh in range(1, N_CHIPS):
            dz = dz + _dot_nt(dzzb[:, ROW_SHARD * sh:ROW_SHARD * (sh + 1)], wg_ref[sh])
        dyf = dz * _gelu_grad(yf)
        dyf_ref[...] = dyf
        du_ref[...] = dyf * dv
        z_ref[...] = z.astype(BF16)
        dzz_ref[...] = dzzb
        dd_ref[...] += _colsum8(dyf * u)
        db_ref[...] += _colsum8(dzz)

    row = lambda i: (i, 0)
    half = pl.BlockSpec((tm, SSM_WIDTH), row)
    return pl.pallas_call(
        body, name="ssm_glu_bwd", grid=(s // tm,),
        in_specs=[half, half, half, pl.BlockSpec((tm, SSM_WIDTH), lambda i: (i, 1)),
                  pl.BlockSpec((1, SSM_WIDTH), lambda i: (0, 0)),
                  pl.BlockSpec((N_CHIPS, None, SSM_WIDTH, ROW_SHARD), lambda i: (0, layer, 0, 0)),
                  pl.BlockSpec((1, 2 * SSM_WIDTH), lambda i: (0, 0))],
        out_specs=[half, half, half, half, pl.BlockSpec((tm, 2 * SSM_WIDTH), row),
                   pl.BlockSpec((SUBLANES, SSM_WIDTH), lambda i: (0, 0)),
                   pl.BlockSpec((SUBLANES, 2 * SSM_WIDTH), lambda i: (0, 0))],
        out_shape=[jax.ShapeDtypeStruct((s, SSM_WIDTH), F32)] * 3
        + [jax.ShapeDtypeStruct((s, SSM_WIDTH), BF16), jax.ShapeDtypeStruct((s, 2 * SSM_WIDTH), BF16),
           jax.ShapeDtypeStruct((SUBLANES, SSM_WIDTH), F32), jax.ShapeDtypeStruct((SUBLANES, 2 * SSM_WIDTH), F32)],
        compiler_params=_cparams(1),
    )(dmix, y, proj, proj, d, w_glu_all, b_glu)


def _ssm_scan_bwd(dyf, xs, proj, wct, coef_rev, wbt):
    s = dyf.shape[0]
    tm = _row_tile(s, 512)
    nt = s // tm

    def body(dy_ref, xs_ref, u_ref, wct_ref, coef_ref, wbt_ref, du_ref, dwc_ref, dwb_ref, da_ref, lam_ref, carry_ref):
        @pl.when(pl.program_id(1) == 0)
        def _():
            carry_ref[...] = jnp.zeros_like(carry_ref)
            dwc_ref[...] = jnp.zeros_like(dwc_ref)
            dwb_ref[...] = jnp.zeros_like(dwb_ref)
            da_ref[...] = jnp.zeros_like(da_ref)

        dyb = dy_ref[...].astype(BF16)
        lam_ref[...] = _dot(dyb, wct_ref[...])
        rows = lax.broadcasted_iota(jnp.int32, (SUBLANES, CH_S), 0)
        last = rows == SUBLANES - 1

        def extra(r0, lr, li, cr, ci):
            er = jnp.where(last, cr, pltpu.roll(lr, SUBLANES - 1, 0))
            ei = jnp.where(last, ci, pltpu.roll(li, SUBLANES - 1, 0))
            xr = xs_ref[pl.ds(r0, SUBLANES), 0:CH_S]
            xi = xs_ref[pl.ds(r0, SUBLANES), CH_S:2 * CH_S]
            da_ref[:, 0:CH_S] += xr * er + xi * ei
            da_ref[:, CH_S:2 * CH_S] += xr * ei - xi * er

        _scan_rows(lam_ref, coef_ref, carry_ref, tm // SUBLANES, reverse=True, extra=extra)
        lamb = lam_ref[...].astype(BF16)
        du_ref[...] = _dot(lamb, wbt_ref[...])
        dwc_ref[...] += _dot_tn(xs_ref[...].astype(BF16), dyb)
        dwb_ref[...] += _dot_tn(u_ref[...].astype(BF16), lamb)

    rev = lambda j, i: (nt - 1 - i, j)
    return pl.pallas_call(
        body, name="ssm_scan_bwd", grid=(SSM_CHUNKS, nt),
        in_specs=[pl.BlockSpec((tm, CH_W), rev),
                  pl.BlockSpec((None, tm, 2 * CH_S), lambda j, i: (j, nt - 1 - i, 0)),
                  pl.BlockSpec((tm, CH_W), rev),
                  pl.BlockSpec((None, CH_W, 2 * CH_S), lambda j, i: (j, 0, 0)),
                  pl.BlockSpec((None, 8, SUBLANES, CH_S), lambda j, i: (j, 0, 0, 0)),
                  pl.BlockSpec((None, 2 * CH_S, CH_W), lambda j, i: (j, 0, 0))],
        out_specs=[pl.BlockSpec((tm, CH_W), rev),
                   pl.BlockSpec((None, 2 * CH_S, CH_W), lambda j, i: (j, 0, 0)),
                   pl.BlockSpec((None, CH_W, 2 * CH_S), lambda j, i: (j, 0, 0)),
                   pl.BlockSpec((None, SUBLANES, 2 * CH_S), lambda j, i: (j, 0, 0))],
        out_shape=[jax.ShapeDtypeStruct((s, SSM_WIDTH), F32),
                   jax.ShapeDtypeStruct((SSM_CHUNKS, 2 * CH_S, CH_W), F32),
                   jax.ShapeDtypeStruct((SSM_CHUNKS, CH_W, 2 * CH_S), F32),
                   jax.ShapeDtypeStruct((SSM_CHUNKS, SUBLANES, 2 * CH_S), F32)],
        scratch_shapes=[pltpu.VMEM((tm, 2 * CH_S), F32), pltpu.VMEM((SUBLANES, 2 * CH_S), F32)],
        compiler_params=_cparams(2),
    )(dyf, xs, proj, wct, coef_rev, wbt)


def _in_proj_bwd(h, g1, w_in_all, layer, qg, kg, proj, du_a, du_b, dgs, dq, dk, dv, dga, dh1):
    s = h.shape[0]
    tm = _row_tile(s, 256)

    def body(h_ref, g_ref, w_ref, qg_ref, kg_ref, ones_ref, q_ref, k_ref, dua_ref, dub_ref, dgs_ref, dq_ref, dk_ref,
             dv_ref, dga_ref, dh1_ref, dh_ref, hn_ref, dp_ref, dg1_ref, dqg_ref, dkg_ref):
        @pl.when(pl.program_id(0) == 0)
        def _():
            dg1_ref[...] = jnp.zeros_like(dg1_ref)
            dqg_ref[...] = jnp.zeros_like(dqg_ref)
            dkg_ref[...] = jnp.zeros_like(dkg_ref)

        ones = ones_ref[...]

        def head_norm_bwd(x, gain, dy):
            r = lax.rsqrt(_dot_hilo(x * x, ones) + RMS_EPS)
            gdy = gain * dy
            dx = r * gdy - x * (r * r * r) * _dot_hilo(x * gdy, ones)
            return dx, x * r * dy

        dqr, dqg_rows = head_norm_bwd(q_ref[...], qg_ref[...], dq_ref[...] * ATTN_SCALE)
        dkr, dkg_rows = head_norm_bwd(k_ref[...], kg_ref[...], dk_ref[...])
        dqg_ref[...] += _colsum8(dqg_rows)
        dkg_ref[...] += _colsum8(dkg_rows)
        dp_ref[:, 0:512] = (dua_ref[...] + dub_ref[...]).astype(BF16)
        dp_ref[:, 512:1024] = dgs_ref[...].astype(BF16)
        dp_ref[:, 1024:1536] = dqr.astype(BF16)
        dp_ref[:, 1536:2048] = dkr.astype(BF16)
        dp_ref[:, 2048:2560] = dv_ref[...].astype(BF16)
        dp_ref[:, 2560:3072] = dga_ref[...].astype(BF16)
        dhn = _dot_nt(dp_ref[:, 0:IN_SHARD], w_ref[0])
        for sh in range(1, N_CHIPS):
            dhn = dhn + _dot_nt(dp_ref[:, IN_SHARD * sh:IN_SHARD * (sh + 1)], w_ref[sh])
        x = h_ref[...]
        gv = g_ref[...]
        r, hn = _rms_rows(x, gv)
        dx, dg_rows = _rms_bwd(x, r, gv, dhn)
        dh_ref[...] = dh1_ref[...] + dx
        hn_ref[...] = hn.astype(BF16)
        dg1_ref[...] += _colsum8(dg_rows)

    row = lambda i: (i, 0)
    full = lambda shape: pl.BlockSpec(shape, lambda i: (0,) * len(shape))
    big = pl.BlockSpec((tm, D_MODEL), row)
    half = pl.BlockSpec((tm, 512), row)
    return pl.pallas_call(
        body, name="in_proj_bwd", grid=(s // tm,),
        in_specs=[big, full((1, D_MODEL)), pl.BlockSpec((N_CHIPS, None, D_MODEL, IN_SHARD), lambda i: (0, layer, 0, 0)),
                  full((1, ATTN_WIDTH)), full((1, ATTN_WIDTH)), full((ATTN_WIDTH, ATTN_WIDTH)),
                  pl.BlockSpec((tm, 512), lambda i: (i, 2)), pl.BlockSpec((tm, 512), lambda i: (i, 3)),
                  half, half, half, half, half, half, half, big],
        out_specs=[big, big, pl.BlockSpec((tm, IN_COLS), row), pl.BlockSpec((SUBLANES, D_MODEL), lambda i: (0, 0)),
                   pl.BlockSpec((SUBLANES, ATTN_WIDTH), lambda i: (0, 0)), pl.BlockSpec((SUBLANES, ATTN_WIDTH), lambda i: (0, 0))],
        out_shape=[jax.ShapeDtypeStruct((s, D_MODEL), F32), jax.ShapeDtypeStruct((s, D_MODEL), BF16),
                   jax.ShapeDtypeStruct((s, IN_COLS), BF16), jax.ShapeDtypeStruct((SUBLANES, D_MODEL), F32),
                   jax.ShapeDtypeStruct((SUBLANES, ATTN_WIDTH), F32), jax.ShapeDtypeStruct((SUBLANES, ATTN_WIDTH), F32)],
        compiler_params=_cparams(1),
    )(h, g1, w_in_all, qg, kg, _head_ones(), proj, proj, du_a, du_b, dgs, dq, dk, dv, dga, dh1)


SMALL_NAMES = ("mix_norm_g", "ssm_a_re", "ssm_a_im", "ssm_log_dt", "ssm_b_re", "ssm_b_im", "ssm_c_re", "ssm_c_im",
               "ssm_d", "ssm_b_glu", "q_norm_g", "k_norm_g", "ple_norm_g")
SMALL_4D = ("ssm_b_re", "ssm_b_im", "ssm_c_re", "ssm_c_im")
BIG_NAMES = ("w_in", "ssm_w_glu", "w_out", "w_ple_gate", "w_ple_proj")


def _ssm_setup(sm, layer):
    col = lambda a: a[layer].reshape(N_STATES, 1)
    a_re, a_im = col(sm["ssm_a_re"]), col(sm["ssm_a_im"])
    log_dt = jnp.repeat(sm["ssm_log_dt"][layer], SSM_STATE).reshape(N_STATES, 1)
    b_re = sm["ssm_b_re"][layer].reshape(N_STATES, SSM_GROUP)
    b_im = sm["ssm_b_im"][layer].reshape(N_STATES, SSM_GROUP)
    disc_in = (a_re, a_im, log_dt, b_re, b_im)
    ab_re, ab_im, bb_re, bb_im = _disc_fwd(*disc_in)
    wb = jnp.concatenate([_block_diag_in(bb_re), _block_diag_in(bb_im)], axis=-1)
    wc = jnp.concatenate([_block_diag_out(sm["ssm_c_re"][layer]), -_block_diag_out(sm["ssm_c_im"][layer])], axis=1)
    return dict(disc_in=disc_in, wb=wb.astype(BF16), wbt=wb.transpose(0, 2, 1).astype(BF16),
                wc=wc.astype(BF16), wct=wc.transpose(0, 2, 1).astype(BF16),
                coef=_scan_coefs(ab_re, ab_im, False), coef_rev=_scan_coefs(ab_re, ab_im, True))


def _local_step(x, p, target, sm, wg):
    tile8 = lambda a: jnp.tile(a, ATTN_WIDTH // HEAD_DIM).reshape(1, ATTN_WIDTH)
    saved = []
    h = x
    for l in range(N_LAYERS):
        ssm = _ssm_setup(sm, l)
        g1 = sm["mix_norm_g"][l].reshape(1, D_MODEL)
        g2 = sm["ple_norm_g"][l].reshape(1, D_MODEL)
        qg, kg = tile8(sm["q_norm_g"][l]), tile8(sm["k_norm_g"][l])
        dsk = sm["ssm_d"][l].reshape(1, SSM_WIDTH)
        bgl = sm["ssm_b_glu"][l].reshape(1, 2 * SSM_WIDTH)
        proj, qkv = _in_proj(h, g1, wg["w_in"], l, qg, kg)
        xs, y = _ssm_scan_fwd(proj, ssm["wb"], ssm["coef"], ssm["wc"])
        ys = _ssm_glu_fwd(y, proj, dsk, wg["ssm_w_glu"], l, bgl)
        o, ya = _attn_fwd(qkv, proj)
        h1, h2 = _out_ple(h, ys, ya, p[l], g2, wg["w_out"], wg["w_ple_gate"], wg["w_ple_proj"], l)
        saved.append(dict(ssm=ssm, g1=g1, g2=g2, qg=qg, kg=kg, dsk=dsk, bgl=bgl, h=h, proj=proj, qkv=qkv, xs=xs, y=y,
                          ys=ys, o=o, ya=ya, h1=h1))
        h = h2
    dh, sq = _loss_grad(h, target)
    loss = 0.5 * jnp.sum(sq) / D_MODEL

    gbig = {n: None for n in BIG_NAMES}
    gsm = {n: [None] * N_LAYERS for n in SMALL_NAMES}
    for l in reversed(range(N_LAYERS)):
        sv = saved[l]
        ssm = sv["ssm"]
        dh1, dmix, hn2b, dgpb, dppb, dh1b, dg2 = _out_ple_bwd(dh, sv["h1"], p[l], sv["g2"], wg["w_out"],
                                                              wg["w_ple_gate"], wg["w_ple_proj"], l)
        gsm["ple_norm_g"][l] = dg2.sum(0)
        gbig["w_ple_proj"] = _tn_matmul(p[l], dppb, N_CHIPS, False, "dw_ple_proj", l, gbig["w_ple_proj"])
        gbig["w_ple_gate"] = _tn_matmul(hn2b, dgpb, N_CHIPS, True, "dw_ple_gate", l, gbig["w_ple_gate"])
        dwo = _tn_matmul(sv["ys"], dh1b, 2, True, "dw_out_ssm", l, gbig["w_out"], 0, N_CHIPS)
        gbig["w_out"] = _tn_matmul(sv["ya"], dh1b, 2, True, "dw_out_attn", l, dwo, 2, N_CHIPS)
        dqs, dkn, dv, dga = _attn_bwd(sv["qkv"], sv["o"], sv["proj"], dmix)
        dyf, du_a, dgs, zb, dzzb, dd, dbg = _ssm_glu_bwd(dmix, sv["y"], sv["proj"], sv["dsk"], wg["ssm_w_glu"], l, sv["bgl"])
        gsm["ssm_d"][l] = dd.sum(0).reshape(SSM_GROUPS, SSM_GROUP)
        gsm["ssm_b_glu"][l] = dbg.sum(0)
        gbig["ssm_w_glu"] = _tn_matmul(zb, dzzb, N_CHIPS, False, "dw_glu", l, gbig["ssm_w_glu"])
        du_b, dwc, dwb, da = _ssm_scan_bwd(dyf, sv["xs"], sv["proj"], ssm["wct"], ssm["coef_rev"], ssm["wbt"])
        gsm["ssm_c_re"][l] = _block_diag_out_t(dwc[:, 0:CH_S, :])
        gsm["ssm_c_im"][l] = -_block_diag_out_t(dwc[:, CH_S:, :])
        da = da.sum(1)
        g_ab_re = da[:, 0:CH_S].reshape(N_STATES, 1)
        g_ab_im = da[:, CH_S:].reshape(N_STATES, 1)
        g_bb_re = _block_diag_in_t(dwb[:, :, 0:CH_S])
        g_bb_im = _block_diag_in_t(dwb[:, :, CH_S:])
        d_are, d_aim, d_ldt, d_bre, d_bim = _disc_bwd(*ssm["disc_in"], g_ab_re, g_ab_im, g_bb_re, g_bb_im)
        gsm["ssm_a_re"][l] = d_are.reshape(SSM_GROUPS, SSM_STATE)
        gsm["ssm_a_im"][l] = d_aim.reshape(SSM_GROUPS, SSM_STATE)
        gsm["ssm_log_dt"][l] = d_ldt.reshape(SSM_GROUPS, SSM_STATE).sum(1)
        gsm["ssm_b_re"][l] = d_bre.reshape(SSM_GROUPS, SSM_STATE, SSM_GROUP)
        gsm["ssm_b_im"][l] = d_bim.reshape(SSM_GROUPS, SSM_STATE, SSM_GROUP)
        dh, hnb, dprojb, dg1, dqg, dkg = _in_proj_bwd(sv["h"], sv["g1"], wg["w_in"], l, sv["qg"], sv["kg"], sv["proj"],
                                                      du_a, du_b, dgs, dqs, dkn, dv, dga, dh1)
        gsm["mix_norm_g"][l] = dg1.sum(0)
        gsm["q_norm_g"][l] = dqg.sum(0).reshape(-1, HEAD_DIM).sum(0)
        gsm["k_norm_g"][l] = dkg.sum(0).reshape(-1, HEAD_DIM).sum(0)
        gbig["w_in"] = _tn_matmul(hnb, dprojb, N_CHIPS, False, "dw_in", l, gbig["w_in"])
    gsm = {n: jnp.stack(v, 0) for n, v in gsm.items()}
    return loss, dh, gbig, gsm


_SMALL_PAD = 8 * 8 * 128


def _pack_small(d):
    flat = jnp.concatenate([d[n].reshape(-1) for n in SMALL_NAMES])
    n = flat.shape[0]
    padded = -(-n // _SMALL_PAD) * _SMALL_PAD
    return jnp.pad(flat, (0, padded - n))


def _unpack_small(flat, like):
    out, off = {}, 0
    for n in SMALL_NAMES:
        size = like[n].size
        out[n] = flat[off:off + size].reshape(like[n].shape)
        off += size
    return out


def _reduce_grads(parts, wire_dtypes):
    n = len(parts)
    flat = [a.reshape(2, -1, a.shape[-1]) for a in parts]
    recv = _sibling_send_other_half(flat, "grad_sibling_send")
    chip = [_add_half(flat[k], recv[k], wire_dtypes[k], "grad_sibling_add").reshape(parts[k].shape[1:])
            for k in range(n)]
    got = _chip_scatter(chip, "grad_chip_scatter")
    tot = [_sum4(got[k], "grad_chip_sum") for k in range(n)]
    return _sibling_join_halves(tot, "grad_sibling_join")


def kernel(x, p, mix_norm_g, w_in, ssm_a_re, ssm_a_im, ssm_log_dt, ssm_b_re, ssm_b_im, ssm_c_re, ssm_c_im, ssm_d, ssm_w_glu, ssm_b_glu, q_norm_g, k_norm_g, w_out, ple_norm_g, w_ple_gate, w_ple_proj, loss_target, m_mix_norm_g, m_w_in, m_ssm_a_re, m_ssm_a_im, m_ssm_log_dt, m_ssm_b_re, m_ssm_b_im, m_ssm_c_re, m_ssm_c_im, m_ssm_d, m_ssm_w_glu, m_ssm_b_glu, m_q_norm_g, m_k_norm_g, m_w_out, m_ple_norm_g, m_w_ple_gate, m_w_ple_proj, v_mix_norm_g, v_w_in, v_ssm_a_re, v_ssm_a_im, v_ssm_log_dt, v_ssm_b_re, v_ssm_b_im, v_ssm_c_re, v_ssm_c_im, v_ssm_d, v_ssm_w_glu, v_ssm_b_glu, v_q_norm_g, v_k_norm_g, v_w_out, v_ple_norm_g, v_w_ple_gate, v_w_ple_proj):
    args = dict(locals())
    names = ("mix_norm_g", "w_in", "ssm_a_re", "ssm_a_im", "ssm_log_dt", "ssm_b_re", "ssm_b_im", "ssm_c_re", "ssm_c_im",
             "ssm_d", "ssm_w_glu", "ssm_b_glu", "q_norm_g", "k_norm_g", "w_out", "ple_norm_g", "w_ple_gate", "w_ple_proj")
    w = {n: args[n] for n in names}
    m = {n: args["m_" + n] for n in names}
    v = {n: args["v_" + n] for n in names}

    gathered = _chip_gather([w[n].astype(BF16) for n in BIG_NAMES], "weight_gather")
    wg = dict(zip(BIG_NAMES, gathered))
    sm = {n: w[n] for n in SMALL_NAMES}
    loss, dx, gbig, gsm = _local_step(x[0], p[:, 0], loss_target[0], sm, wg)
    loss = lax.psum(loss, ("x", "y", "c"))

    small = _pack_small(gsm)
    parts = [gbig[n] for n in BIG_NAMES] + [small.reshape(2, N_CHIPS, SUBLANES, -1)]
    red = _reduce_grads(parts, [BF16] * len(BIG_NAMES) + [F32])
    small_mine = red[-1]
    small_all = _chip_gather([small_mine], "small_grad_gather")[0]
    small_tot = small_all.transpose(1, 0, 2, 3).reshape(-1)
    g = dict(zip(BIG_NAMES, [r.reshape(w[n].shape) for r, n in zip(red[:-1], BIG_NAMES)]))
    g.update(_unpack_small(small_tot, sm))

    delta, new_m, new_v = {}, {}, {}
    for n in BIG_NAMES:
        lanes = w[n].shape[-1]
        outs = _adamw(_as_rows(w[n], lanes), _as_rows(g[n], lanes), _as_rows(m[n], lanes), _as_rows(v[n], lanes), "adamw_" + n)
        delta[n], new_m[n], new_v[n] = [o.reshape(w[n].shape) for o in outs]
    for group, per_layer in ((SMALL_4D, True), (tuple(n for n in SMALL_NAMES if n not in SMALL_4D), False)):
        outs = _adamw_many(*[[d[n] for n in group] for d in (w, g, m, v)], "adamw_small_4d" if per_layer else "adamw_small", per_layer)
        for d, o in zip((delta, new_m, new_v), outs):
            d.update(zip(group, o))

    return (loss, dx[None], *[g[n] for n in names], *[delta[n] for n in names],
            *[new_m[n] for n in names], *[new_v[n] for n in names])
```

```python
import functools
import math

import jax
import jax.numpy as jnp
from jax import lax
from jax.experimental import pallas as pl
from jax.experimental.pallas import tpu as pltpu

F32 = jnp.float32
BF16 = jnp.bfloat16

D_MODEL = 1024
N_LAYERS = 2
N_CHIPS = 4
IN_COLS = 3072
IN_SHARD = IN_COLS // N_CHIPS
SSM_WIDTH = 512
SSM_GROUP = 16
SSM_GROUPS = 32
SSM_STATE = 64
N_STATES = SSM_GROUPS * SSM_STATE
SSM_CHUNKS = 4
CH_W = SSM_WIDTH // SSM_CHUNKS
CH_S = N_STATES // SSM_CHUNKS
ATTN_WIDTH = 512
HEAD_DIM = 64
PLE_DIM = 256
ROW_SHARD = 256
RMS_EPS = 1e-6
ATTN_SCALE = HEAD_DIM ** -0.5
ATTN_BLOCK = 128
EXP_ZERO = -87.5
SUBLANES = 8
V7X_VMEM_LIMIT = 52 * 1024 * 1024

ADAM_LR = 0.001
ADAM_B1 = 0.9
ADAM_B2 = 0.999
ADAM_EPS = 1e-08
ADAM_WD = 0.01
ADAM_STEP = 10

MESH = pl.DeviceIdType.MESH
ANY = pl.BlockSpec(memory_space=pl.ANY)


def _cparams(n_grid=0, parallel=0):
    sem = tuple(["parallel"] * parallel + ["arbitrary"] * (n_grid - parallel))
    return pltpu.CompilerParams(dimension_semantics=sem, vmem_limit_bytes=V7X_VMEM_LIMIT)


def _dot(a, b):
    return jnp.dot(a, b, preferred_element_type=F32)


def _dot_nt(a, b):
    return lax.dot_general(a, b, (((1,), (1,)), ((), ())), preferred_element_type=F32)


def _dot_tn(a, b):
    return lax.dot_general(a, b, (((0,), (0,)), ((), ())), preferred_element_type=F32)


def _split_hilo(a):
    hi = a.astype(BF16)
    lo = (a - hi.astype(F32)).astype(BF16)
    return hi, lo


def _dot_hilo(a, b):
    hi, lo = _split_hilo(a)
    return _dot(hi, b) + _dot(lo, b)


def _sigmoid(x):
    return 0.5 * (jnp.tanh(0.5 * x) + 1.0)


_GELU_C = math.sqrt(2.0 / math.pi)


def _gelu(x):
    return 0.5 * x * (1.0 + jnp.tanh(_GELU_C * (x + 0.044715 * (x * x * x))))


def _gelu_grad(x):
    t = jnp.tanh(_GELU_C * (x + 0.044715 * (x * x * x)))
    return 0.5 * (1.0 + t) + 0.5 * x * (1.0 - t * t) * (_GELU_C * (1.0 + 3.0 * 0.044715 * (x * x)))


def _row_tile(s, want):
    for t in range(min(s, want), 7, -1):
        if s % t == 0 and t % SUBLANES == 0:
            return t
    return s


def _coords():
    return lax.axis_index("x"), lax.axis_index("y"), lax.axis_index("c")


def _other_chips(x, y):
    return [(1 - x, y), (x, 1 - y), (1 - x, 1 - y)]


def _remote(src, dst, send_sem, recv_sem, dev):
    return pltpu.make_async_remote_copy(src_ref=src, dst_ref=dst, send_sem=send_sem, recv_sem=recv_sem,
                                        device_id=dev, device_id_type=MESH)


def _set_block(buf, block, index):
    return lax.dynamic_update_index_in_dim(buf, block, index, 0)


def _chip_gather(arrs, name):
    n = len(arrs)

    def body(*refs):
        ins, outs = refs[:n], refs[n:2 * n]
        send_sems, recv_sems, fwd_send, fwd_recv = refs[2 * n:]
        x, y, c = _coords()
        me_chip = 2 * x + y
        chips = _other_chips(x, y)
        sibling = (x, y, 1 - c)
        first = []
        for k in range(n):
            for j, (cx, cy) in enumerate(chips):
                cp = _remote(ins[k].at[c], outs[k].at[me_chip, c], send_sems.at[3 * k + j], recv_sems.at[3 * k + j],
                             (cx, cy, c))
                cp.start()
                first.append(cp)
        passed = []
        for k in range(n):
            for j, (cx, cy) in enumerate(chips):
                blk = outs[k].at[2 * cx + cy, c]
                _remote(blk, blk, send_sems.at[3 * k + j], recv_sems.at[3 * k + j], (cx, cy, c)).wait_recv()
                cp = _remote(blk, blk, fwd_send.at[3 * k + j], fwd_recv.at[3 * k + j], sibling)
                cp.start()
                passed.append(cp)
        for k in range(n):
            for j, (cx, cy) in enumerate(chips):
                blk = outs[k].at[2 * cx + cy, 1 - c]
                _remote(blk, blk, fwd_send.at[3 * k + j], fwd_recv.at[3 * k + j], sibling).wait_recv()
        for cp in first + passed:
            cp.wait_send()

    outs = pl.pallas_call(
        body, name=name,
        out_shape=[jax.ShapeDtypeStruct((N_CHIPS,) + a.shape, a.dtype) for a in arrs],
        in_specs=[ANY] * n, out_specs=[ANY] * n,
        scratch_shapes=[pltpu.SemaphoreType.DMA((3 * n,)), pltpu.SemaphoreType.DMA((3 * n,)),
                        pltpu.SemaphoreType.DMA((3 * n,)), pltpu.SemaphoreType.DMA((3 * n,))],
    )(*arrs)
    me_chip = 2 * lax.axis_index("x") + lax.axis_index("y")
    return [_set_block(o, a, me_chip) for o, a in zip(outs, arrs)]


def _sibling_send_other_half(arrs, name):
    n = len(arrs)

    def body(*refs):
        ins, outs = refs[:n], refs[n:2 * n]
        send_sems, recv_sems = refs[2 * n:]
        x, y, c = _coords()
        cps = [_remote(ins[k].at[1 - c], outs[k], send_sems.at[k], recv_sems.at[k], (x, y, 1 - c)) for k in range(n)]
        for cp in cps:
            cp.start()
        for cp in cps:
            cp.wait_recv()
        for cp in cps:
            cp.wait_send()

    return pl.pallas_call(
        body, name=name,
        out_shape=[jax.ShapeDtypeStruct(a.shape[1:], a.dtype) for a in arrs],
        in_specs=[ANY] * n, out_specs=[ANY] * n,
        scratch_shapes=[pltpu.SemaphoreType.DMA((n,)), pltpu.SemaphoreType.DMA((n,))],
    )(*arrs)


def _sibling_join_halves(arrs, name):
    n = len(arrs)

    def body(*refs):
        ins, outs = refs[:n], refs[n:2 * n]
        send_sems, recv_sems = refs[2 * n:]
        x, y, c = _coords()
        cps = [_remote(ins[k], outs[k].at[c], send_sems.at[k], recv_sems.at[k], (x, y, 1 - c)) for k in range(n)]
        for cp in cps:
            cp.start()
        for k in range(n):
            blk = outs[k].at[1 - c]
            _remote(blk, blk, send_sems.at[k], recv_sems.at[k], (x, y, 1 - c)).wait_recv()
        for cp in cps:
            cp.wait_send()

    outs = pl.pallas_call(
        body, name=name,
        out_shape=[jax.ShapeDtypeStruct((2,) + a.shape, a.dtype) for a in arrs],
        in_specs=[ANY] * n, out_specs=[ANY] * n,
        scratch_shapes=[pltpu.SemaphoreType.DMA((n,)), pltpu.SemaphoreType.DMA((n,))],
    )(*arrs)
    c = lax.axis_index("c")
    return [_set_block(o, a, c) for o, a in zip(outs, arrs)]


def _chip_scatter(arrs, name):
    n = len(arrs)

    def body(*refs):
        ins, outs = refs[:n], refs[n:2 * n]
        send_sems, recv_sems = refs[2 * n:]
        x, y, c = _coords()
        me_chip = 2 * x + y
        chips = _other_chips(x, y)
        cps = []
        for k in range(n):
            for j, (cx, cy) in enumerate(chips):
                cp = _remote(ins[k].at[2 * cx + cy], outs[k].at[me_chip], send_sems.at[3 * k + j],
                             recv_sems.at[3 * k + j], (cx, cy, c))
                cp.start()
                cps.append(cp)
        for k in range(n):
            for j, (cx, cy) in enumerate(chips):
                blk = outs[k].at[2 * cx + cy]
                _remote(blk, blk, send_sems.at[3 * k + j], recv_sems.at[3 * k + j], (cx, cy, c)).wait_recv()
        for cp in cps:
            cp.wait_send()

    outs = pl.pallas_call(
        body, name=name,
        out_shape=[jax.ShapeDtypeStruct(a.shape, a.dtype) for a in arrs],
        in_specs=[ANY] * n, out_specs=[ANY] * n,
        scratch_shapes=[pltpu.SemaphoreType.DMA((3 * n,)), pltpu.SemaphoreType.DMA((3 * n,))],
    )(*arrs)
    me_chip = 2 * lax.axis_index("x") + lax.axis_index("y")
    return [_set_block(o, lax.dynamic_index_in_dim(a, me_chip, 0, keepdims=False), me_chip) for o, a in zip(outs, arrs)]


def _as_rows(a, lanes):
    return a.reshape(-1, lanes)


def _add_half(full, recv, out_dtype, name):
    _, r, cdim = full.shape
    tr = _row_tile(r, 512)

    def body(c_ref, a_ref, b_ref, o_ref):
        o_ref[...] = (a_ref[...] + b_ref[...]).astype(out_dtype)

    c = lax.axis_index("c").astype(jnp.int32).reshape(1)
    return pl.pallas_call(
        body, name=name,
        grid_spec=pltpu.PrefetchScalarGridSpec(
            num_scalar_prefetch=1, grid=(r // tr,),
            in_specs=[pl.BlockSpec((None, tr, cdim), lambda i, c_ref: (c_ref[0], i, 0)),
                      pl.BlockSpec((tr, cdim), lambda i, c_ref: (i, 0))],
            out_specs=pl.BlockSpec((tr, cdim), lambda i, c_ref: (i, 0))),
        out_shape=jax.ShapeDtypeStruct((r, cdim), out_dtype),
        compiler_params=_cparams(1),
    )(c, full, recv)


def _sum4(parts, name):
    _, r, cdim = parts.shape
    tr = _row_tile(r, 512)

    def body(p_ref, o_ref):
        acc = p_ref[0].astype(F32) + p_ref[1].astype(F32)
        acc = acc + p_ref[2].astype(F32)
        o_ref[...] = acc + p_ref[3].astype(F32)

    return pl.pallas_call(
        body, name=name, grid=(r // tr,),
        in_specs=[pl.BlockSpec((N_CHIPS, tr, cdim), lambda i: (0, i, 0))],
        out_specs=pl.BlockSpec((tr, cdim), lambda i: (i, 0)),
        out_shape=jax.ShapeDtypeStruct((r, cdim), F32),
        compiler_params=_cparams(1),
    )(parts)


def _adamw_math(w, g, m, v):
    c1 = 1.0 - ADAM_B1 ** ADAM_STEP
    c2 = 1.0 - ADAM_B2 ** ADAM_STEP
    nm = ADAM_B1 * m + (1.0 - ADAM_B1) * g
    nv = ADAM_B2 * v + (1.0 - ADAM_B2) * (g * g)
    delta = -ADAM_LR * ((nm / c1) / (jnp.sqrt(nv / c2) + ADAM_EPS) + ADAM_WD * w)
    return delta, nm, nv


def _adamw(w, g, m, v, name):
    r, cdim = w.shape
    tr = _row_tile(r, 256)

    def body(w_ref, g_ref, m_ref, v_ref, d_ref, nm_ref, nv_ref):
        d_ref[...], nm_ref[...], nv_ref[...] = _adamw_math(w_ref[...], g_ref[...], m_ref[...], v_ref[...])

    spec = pl.BlockSpec((tr, cdim), lambda i: (i, 0))
    return pl.pallas_call(
        body, name=name, grid=(r // tr,),
        in_specs=[spec] * 4, out_specs=[spec] * 3,
        out_shape=[jax.ShapeDtypeStruct((r, cdim), F32)] * 3,
        compiler_params=_cparams(1),
    )(w, g, m, v)


def _adamw_many(ws, gs, ms, vs, name, per_layer):
    n = len(ws)

    def body(*refs):
        for k in range(n):
            w, g, m, v = (refs[j * n + k][...] for j in range(4))
            outs = _adamw_math(w, g, m, v)
            for j in range(3):
                refs[(4 + j) * n + k][...] = outs[j]

    shapes = [jax.ShapeDtypeStruct(w.shape, F32) for w in ws]
    if per_layer:
        specs = [pl.BlockSpec((None,) + w.shape[1:], lambda l, nd=w.ndim: (l,) + (0,) * (nd - 1)) for w in ws]
        call = pl.pallas_call(body, name=name, grid=(N_LAYERS,), in_specs=specs * 4, out_specs=specs * 3,
                              out_shape=shapes * 3, compiler_params=_cparams(1))
    else:
        call = pl.pallas_call(body, name=name, out_shape=shapes * 3, compiler_params=_cparams())
    outs = call(*ws, *gs, *ms, *vs)
    return outs[0:n], outs[n:2 * n], outs[2 * n:3 * n]


def _discretise(a_re, a_im, log_dt, b_re, b_im):
    dt = jnp.exp(log_dt)
    mag = jnp.exp(a_re * dt)
    ab_re = mag * jnp.cos(a_im * dt)
    ab_im = mag * jnp.sin(a_im * dt)
    num_re = ab_re - 1.0
    num_im = ab_im
    den = a_re * a_re + a_im * a_im
    f_re = (num_re * a_re + num_im * a_im) / den
    f_im = (num_im * a_re - num_re * a_im) / den
    bb_re = f_re * b_re - f_im * b_im
    bb_im = f_re * b_im + f_im * b_re
    return ab_re, ab_im, bb_re, bb_im


def _disc_shapes():
    col = jax.ShapeDtypeStruct((1, N_STATES), F32)
    mat = jax.ShapeDtypeStruct((SSM_GROUP, N_STATES), F32)
    return col, mat


def _disc_fwd(a_re, a_im, log_dt, b_re, b_im):
    col, mat = _disc_shapes()

    def body(ar, ai, ld, br, bi, o0, o1, o2, o3):
        outs = _discretise(ar[...], ai[...], ld[...], br[...], bi[...])
        for o, val in zip((o0, o1, o2, o3), outs):
            o[...] = val

    return pl.pallas_call(body, name="ssm_discretise", out_shape=[col, col, mat, mat],
                          compiler_params=_cparams())(a_re, a_im, log_dt, b_re, b_im)


def _disc_bwd(a_re, a_im, log_dt, b_re, b_im, g_ab_re, g_ab_im, g_bb_re, g_bb_im):
    col, mat = _disc_shapes()

    def body(ar, ai, ld, br, bi, g0, g1, g2, g3, o0, o1, o2, o3, o4):
        _, vjp = jax.vjp(_discretise, ar[...], ai[...], ld[...], br[...], bi[...])
        grads = vjp((g0[...], g1[...], g2[...], g3[...]))
        for o, val in zip((o0, o1, o2, o3, o4), grads):
            o[...] = val

    return pl.pallas_call(body, name="ssm_discretise_bwd", out_shape=[col, col, col, mat, mat],
                          compiler_params=_cparams())(a_re, a_im, log_dt, b_re, b_im, g_ab_re, g_ab_im, g_bb_re, g_bb_im)


def _cmul(ar, ai, br, bi):
    return ar * br - ai * bi, ar * bi + ai * br


def _scan_coefs(ab_re, ab_im, reverse):
    ar = ab_re.reshape(1, N_STATES)
    ai = ab_im.reshape(1, N_STATES)
    if reverse:
        ai = -ai
    a2 = _cmul(ar, ai, ar, ai)
    a4 = _cmul(*a2, *a2)
    rows = jnp.arange(SUBLANES)[:, None]
    out = []
    for (pr, pi), sh in (((ar, ai), 1), (a2, 2), (a4, 4)):
        keep = (rows <= SUBLANES - 1 - sh) if reverse else (rows >= sh)
        out += [jnp.where(keep, pr, 0.0), jnp.where(keep, pi, 0.0)]
    pows = [(ar, ai)]
    for _ in range(SUBLANES - 1):
        pows.append(_cmul(*pows[-1], ar, ai))
    order = pows[::-1] if reverse else pows
    out += [jnp.concatenate([p[0] for p in order], 0), jnp.concatenate([p[1] for p in order], 0)]
    t = jnp.stack(out, 0)
    return t.reshape(8, SUBLANES, SSM_CHUNKS, CH_S).transpose(2, 0, 1, 3)


def _block_diag_in(bb):
    t = bb.reshape(SSM_GROUP, SSM_CHUNKS, 8, SSM_STATE)
    eye = jnp.eye(8, dtype=bb.dtype)
    return jnp.einsum("hjgp,gk->jghkp", t, eye).reshape(SSM_CHUNKS, CH_W, CH_S)


def _block_diag_in_t(d):
    t = d.reshape(SSM_CHUNKS, 8, SSM_GROUP, 8, SSM_STATE)
    return jnp.einsum("jghgp->hjgp", t).reshape(SSM_GROUP, N_STATES)


def _block_diag_out(c):
    t = c.reshape(SSM_CHUNKS, 8, SSM_GROUP, SSM_STATE)
    eye = jnp.eye(8, dtype=c.dtype)
    return jnp.einsum("jghp,gk->jgpkh", t, eye).reshape(SSM_CHUNKS, CH_S, CH_W)


def _block_diag_out_t(d):
    t = d.reshape(SSM_CHUNKS, 8, SSM_STATE, 8, SSM_GROUP)
    return jnp.einsum("jgpgh->jghp", t).reshape(SSM_GROUPS, SSM_GROUP, SSM_STATE)


def _head_ones():
    r = jnp.arange(ATTN_WIDTH) // HEAD_DIM
    return jnp.where(r[:, None] == r[None, :], 1.0 / HEAD_DIM, 0.0).astype(BF16)


def _in_proj(h, g1, w_in_all, layer, qg, kg):
    s = h.shape[0]
    tm = _row_tile(s, 256)

    def body(h_ref, g_ref, w_ref, qg_ref, kg_ref, ones_ref, proj_ref, qkv_ref):
        x = h_ref[...]
        r = lax.rsqrt(jnp.mean(x * x, axis=-1, keepdims=True) + RMS_EPS)
        hn = (x * r * g_ref[...]).astype(BF16)
        for sh in range(N_CHIPS):
            proj_ref[:, IN_SHARD * sh:IN_SHARD * (sh + 1)] = _dot(hn, w_ref[sh])
        ones = ones_ref[...]
        q = proj_ref[:, 1024:1536]
        k = proj_ref[:, 1536:2048]
        rq = lax.rsqrt(_dot_hilo(q * q, ones) + RMS_EPS)
        rk = lax.rsqrt(_dot_hilo(k * k, ones) + RMS_EPS)
        qkv_ref[:, 0:512] = (q * rq * qg_ref[...] * ATTN_SCALE).astype(BF16)
        qkv_ref[:, 512:1024] = (k * rk * kg_ref[...]).astype(BF16)
        qkv_ref[:, 1024:1536] = proj_ref[:, 2048:2560].astype(BF16)

    full = lambda shape: pl.BlockSpec(shape, lambda i: (0,) * len(shape))
    return pl.pallas_call(
        body, name="in_proj", grid=(s // tm,),
        in_specs=[pl.BlockSpec((tm, D_MODEL), lambda i: (i, 0)), full((1, D_MODEL)),
                  pl.BlockSpec((N_CHIPS, None, D_MODEL, IN_SHARD), lambda i: (0, layer, 0, 0)),
                  full((1, ATTN_WIDTH)), full((1, ATTN_WIDTH)), full((ATTN_WIDTH, ATTN_WIDTH))],
        out_specs=[pl.BlockSpec((tm, IN_COLS), lambda i: (i, 0)), pl.BlockSpec((tm, 3 * ATTN_WIDTH), lambda i: (i, 0))],
        out_shape=[jax.ShapeDtypeStruct((s, IN_COLS), F32), jax.ShapeDtypeStruct((s, 3 * ATTN_WIDTH), BF16)],
        compiler_params=_cparams(1),
    )(h, g1, w_in_all, qg, kg, _head_ones())


def _scan_rows(x_ref, coef_ref, carry_ref, n_blocks, reverse, extra=None):
    c = [coef_ref[a] for a in range(8)]
    shifts = (7, 6, 4) if reverse else (1, 2, 4)
    edge = 0 if reverse else SUBLANES - 1

    def blk(b, carry):
        bb = (n_blocks - 1 - b) if reverse else b
        r0 = pl.multiple_of(bb * SUBLANES, SUBLANES)
        xr = x_ref[pl.ds(r0, SUBLANES), 0:CH_S]
        xi = x_ref[pl.ds(r0, SUBLANES), CH_S:2 * CH_S]
        for lvl, sh in enumerate(shifts):
            ar, ai = c[2 * lvl], c[2 * lvl + 1]
            sr = pltpu.roll(xr, sh, 0)
            si = pltpu.roll(xi, sh, 0)
            xr, xi = xr + (ar * sr - ai * si), xi + (ar * si + ai * sr)
        cr, ci = carry
        xr, xi = xr + (c[6] * cr - c[7] * ci), xi + (c[6] * ci + c[7] * cr)
        x_ref[pl.ds(r0, SUBLANES), 0:CH_S] = xr
        x_ref[pl.ds(r0, SUBLANES), CH_S:2 * CH_S] = xi
        if extra is not None:
            extra(r0, xr, xi, cr, ci)
        return (jnp.broadcast_to(xr[edge:edge + 1, :], (SUBLANES, CH_S)),
                jnp.broadcast_to(xi[edge:edge + 1, :], (SUBLANES, CH_S)))

    cr, ci = lax.fori_loop(0, n_blocks, blk, (carry_ref[:, 0:CH_S], carry_ref[:, CH_S:2 * CH_S]))
    carry_ref[:, 0:CH_S] = cr
    carry_ref[:, CH_S:2 * CH_S] = ci


def _ssm_scan_fwd(proj, wb, coef, wc):
    s = proj.shape[0]
    tm = _row_tile(s, 512)

    def body(u_ref, wb_ref, coef_ref, wc_ref, xs_ref, y_ref, carry_ref):
        @pl.when(pl.program_id(1) == 0)
        def _():
            carry_ref[...] = jnp.zeros_like(carry_ref)

        xs_ref[...] = _dot(u_ref[...].astype(BF16), wb_ref[...])
        _scan_rows(xs_ref, coef_ref, carry_ref, tm // SUBLANES, reverse=False)
        y_ref[...] = _dot(xs_ref[...].astype(BF16), wc_ref[...])

    return pl.pallas_call(
        body, name="ssm_scan", grid=(SSM_CHUNKS, s // tm),
        in_specs=[pl.BlockSpec((tm, CH_W), lambda j, i: (i, j)),
                  pl.BlockSpec((None, CH_W, 2 * CH_S), lambda j, i: (j, 0, 0)),
                  pl.BlockSpec((None, 8, SUBLANES, CH_S), lambda j, i: (j, 0, 0, 0)),
                  pl.BlockSpec((None, 2 * CH_S, CH_W), lambda j, i: (j, 0, 0))],
        out_specs=[pl.BlockSpec((None, tm, 2 * CH_S), lambda j, i: (j, i, 0)),
                   pl.BlockSpec((tm, CH_W), lambda j, i: (i, j))],
        out_shape=[jax.ShapeDtypeStruct((SSM_CHUNKS, s, 2 * CH_S), F32), jax.ShapeDtypeStruct((s, SSM_WIDTH), F32)],
        scratch_shapes=[pltpu.VMEM((SUBLANES, 2 * CH_S), F32)],
        compiler_params=_cparams(2),
    )(proj, wb, coef, wc)


def _glu_forward(y, u, d, wg_ref, bg):
    yf = y + d * u
    z = _gelu(yf)
    zb = z.astype(BF16)
    zz = jnp.concatenate([_dot(zb, wg_ref[sh]) for sh in range(N_CHIPS)], axis=-1) + bg
    return yf, z, zz[:, 0:SSM_WIDTH], zz[:, SSM_WIDTH:2 * SSM_WIDTH]


def _ssm_glu_fwd(y, proj, d, w_glu_all, layer, b_glu):
    s = y.shape[0]
    tm = _row_tile(s, 512)

    def body(y_ref, u_ref, gs_ref, d_ref, wg_ref, bg_ref, o_ref):
        _, _, val, gate = _glu_forward(y_ref[...], u_ref[...], d_ref[...], wg_ref, bg_ref[...])
        gs = gs_ref[...]
        o_ref[...] = val * _sigmoid(gate) * (gs * _sigmoid(gs))

    row = lambda i: (i, 0)
    return pl.pallas_call(
        body, name="ssm_glu", grid=(s // tm,),
        in_specs=[pl.BlockSpec((tm, SSM_WIDTH), row), pl.BlockSpec((tm, SSM_WIDTH), row),
                  pl.BlockSpec((tm, SSM_WIDTH), lambda i: (i, 1)), pl.BlockSpec((1, SSM_WIDTH), lambda i: (0, 0)),
                  pl.BlockSpec((N_CHIPS, None, SSM_WIDTH, ROW_SHARD), lambda i: (0, layer, 0, 0)),
                  pl.BlockSpec((1, 2 * SSM_WIDTH), lambda i: (0, 0))],
        out_specs=pl.BlockSpec((tm, SSM_WIDTH), row),
        out_shape=jax.ShapeDtypeStruct((s, SSM_WIDTH), F32),
        compiler_params=_cparams(1),
    )(y, proj, proj, d, w_glu_all, b_glu)


def _tri(kind):
    r = jnp.arange(ATTN_BLOCK)
    if kind == "suffix_incl":
        m = r[:, None] >= r[None, :]
    else:
        m = r[:, None] < r[None, :]
    return jnp.concatenate([m, jnp.ones_like(m)], axis=1).astype(BF16)


def _head_masks():
    lane = lax.broadcasted_iota(jnp.int32, (1, 2 * HEAD_DIM), 1)
    return [lane < HEAD_DIM, lane >= HEAD_DIM]


def _chain_step(t, base, n_sub, first, q_ref, k_ref, tri_ref, l_scr, per_chain):
    tb = ATTN_BLOCK
    row = lax.broadcasted_iota(jnp.int32, (tb, tb), 0)
    col = lax.broadcasted_iota(jnp.int32, (tb, tb), 1)
    masks = _head_masks()
    blks = [base + a - t for a in range(n_sub)]
    r0s = [pl.multiple_of(jnp.maximum(blk, 0) * tb, tb) for blk in blks]
    zs = []
    for a in range(n_sub):
        kb = k_ref[pl.ds(r0s[a], tb), :]
        qa = q_ref[a * tb:(a + 1) * tb, :]
        for mask in masks:
            zs.append(_dot_nt(jnp.where(mask, qa, jnp.zeros_like(qa)), kb))
    parts = []
    for z in zs:
        ls = jnp.minimum(-z, 0.0) - jnp.log(1.0 + jnp.exp(-jnp.abs(z)))
        if first:
            ls = jnp.where(col < row, ls, 0.0)
        parts.append(_split_hilo(ls))
    tri = tri_ref[...]
    sums = [_dot(hi, tri) + _dot(lo, tri) for hi, lo in parts]
    top = None
    ws = []
    for c, (z, sm) in enumerate(zip(zs, sums)):
        if first:
            lsum = jnp.zeros((tb, tb), F32)
        else:
            lsum = l_scr[c] + jnp.where(blks[c // 2] >= 0, 0.0, -1e30)
        w = jnp.exp(z + sm[:, 0:tb] + lsum)
        if first:
            w = jnp.where(col < row, w, 0.0)
        ws.append(w)
        lsum = lsum + sm[:, tb:2 * tb]
        l_scr[c] = lsum
        top = lsum if top is None else jnp.maximum(top, lsum)
    for c, (z, w) in enumerate(zip(zs, ws)):
        per_chain(c // 2, c % 2, c, r0s[c // 2], z, w)
    return jnp.max(top)


def _chain_sweep(base, n_sub, q_ref, k_ref, tri_ref, l_scr, per_chain):
    top = _chain_step(0, base, n_sub, True, q_ref, k_ref, tri_ref, l_scr, functools.partial(per_chain, 0))

    def cond(carry):
        t, top = carry
        return jnp.logical_and(t <= base + n_sub - 1, top > EXP_ZERO)

    def step(carry):
        t, _ = carry
        return t + 1, _chain_step(t, base, n_sub, False, q_ref, k_ref, tri_ref, l_scr, functools.partial(per_chain, t))

    steps, _ = lax.while_loop(cond, step, (jnp.int32(1), top))
    return steps


ATTN_SUB_FWD = 4
ATTN_SUB_BWD = 4


def _attn_fwd(qkv, proj):
    s = qkv.shape[0]
    tb = ATTN_BLOCK
    n_sub = min(ATTN_SUB_FWD, s // tb)
    tq = n_sub * tb

    def body(q_ref, k_ref, v_ref, g_ref, tri_ref, o_ref, ya_ref, l_scr):
        i = pl.program_id(1)
        masks = _head_masks()
        o_ref[...] = jnp.zeros_like(o_ref)

        def per_chain(t, a, h, c, r0, z, w):
            vb = v_ref[pl.ds(r0, tb), :]
            vb = jnp.where(masks[h], vb, jnp.zeros_like(vb))
            o_ref[a * tb:(a + 1) * tb, :] += _dot(w.astype(BF16), vb)

        _chain_sweep(i * n_sub, n_sub, q_ref, k_ref, tri_ref, l_scr, per_chain)
        g = g_ref[...]
        ya_ref[...] = o_ref[...] * (g * _sigmoid(g))

    hp_blk = lambda off: pl.BlockSpec((tq, 2 * HEAD_DIM), lambda hp, i: (i, off + hp))
    res = lambda off: pl.BlockSpec((s, 2 * HEAD_DIM), lambda hp, i: (0, off + hp))
    return pl.pallas_call(
        body, name="attn_fwd", grid=(ATTN_WIDTH // (2 * HEAD_DIM), s // tq),
        in_specs=[hp_blk(0), res(4), res(8), hp_blk(20), pl.BlockSpec((tb, 2 * tb), lambda hp, i: (0, 0))],
        out_specs=[hp_blk(0), hp_blk(0)],
        out_shape=[jax.ShapeDtypeStruct((s, ATTN_WIDTH), F32)] * 2,
        scratch_shapes=[pltpu.VMEM((2 * n_sub, tb, tb), F32)],
        compiler_params=_cparams(2),
    )(qkv, qkv, qkv, proj, _tri("suffix_incl"))


def _rms_rows(x, g):
    r = lax.rsqrt(jnp.mean(x * x, axis=-1, keepdims=True) + RMS_EPS)
    return r, x * r * g


def _ple_forward(h1, p, g2, wpg_ref, wpp_ref):
    r2, hn2 = _rms_rows(h1, g2)
    hb = hn2.astype(BF16)
    gpre = _dot(hb[:, 0:ROW_SHARD], wpg_ref[0])
    for sh in range(1, N_CHIPS):
        gpre = gpre + _dot(hb[:, ROW_SHARD * sh:ROW_SHARD * (sh + 1)], wpg_ref[sh])
    gate = _sigmoid(gpre)
    pb = p.astype(BF16)
    pp = jnp.concatenate([_dot(pb, wpp_ref[sh]) for sh in range(N_CHIPS)], axis=-1)
    return r2, hb, gate, pp


def _out_ple(h, ys, ya, p, g2, w_out_all, w_pg_all, w_pp_all, layer):
    s = h.shape[0]
    tm = _row_tile(s, 256)

    def body(h_ref, ys_ref, ya_ref, p_ref, g_ref, wo_ref, wpg_ref, wpp_ref, h1_ref, h2_ref):
        ysb = ys_ref[...].astype(BF16)
        yab = ya_ref[...].astype(BF16)
        h1 = h_ref[...]
        for sh, src in enumerate((ysb[:, 0:ROW_SHARD], ysb[:, ROW_SHARD:], yab[:, 0:ROW_SHARD], yab[:, ROW_SHARD:])):
            h1 = h1 + _dot(src, wo_ref[sh])
        _, _, gate, pp = _ple_forward(h1, p_ref[...], g_ref[...], wpg_ref, wpp_ref)
        h1_ref[...] = h1
        h2_ref[...] = h1 + gate * pp

    row = lambda i: (i, 0)
    wspec = lambda r, cdim: pl.BlockSpec((N_CHIPS, None, r, cdim), lambda i: (0, layer, 0, 0))
    return pl.pallas_call(
        body, name="out_ple", grid=(s // tm,),
        in_specs=[pl.BlockSpec((tm, D_MODEL), row), pl.BlockSpec((tm, SSM_WIDTH), row), pl.BlockSpec((tm, ATTN_WIDTH), row),
                  pl.BlockSpec((tm, PLE_DIM), row), pl.BlockSpec((1, D_MODEL), lambda i: (0, 0)),
                  wspec(ROW_SHARD, D_MODEL), wspec(ROW_SHARD, D_MODEL), wspec(PLE_DIM, ROW_SHARD)],
        out_specs=[pl.BlockSpec((tm, D_MODEL), row)] * 2,
        out_shape=[jax.ShapeDtypeStruct((s, D_MODEL), F32)] * 2,
        compiler_params=_cparams(1),
    )(h, ys, ya, p, g2, w_out_all, w_pg_all, w_pp_all)


def _loss_grad(y, target):
    s = y.shape[0]
    tm = _row_tile(s, 512)

    def body(y_ref, t_ref, dy_ref, acc_ref):
        @pl.when(pl.program_id(0) == 0)
        def _():
            acc_ref[...] = jnp.zeros_like(acc_ref)

        e = y_ref[...] - t_ref[...]
        dy_ref[...] = e / D_MODEL
        sq = (e * e).reshape(tm // SUBLANES, SUBLANES, D_MODEL).sum(axis=0)
        part = sq[:, 0:128]
        for b in range(1, D_MODEL // 128):
            part = part + sq[:, 128 * b:128 * (b + 1)]
        acc_ref[...] += part

    row = lambda i: (i, 0)
    return pl.pallas_call(
        body, name="loss_grad", grid=(s // tm,),
        in_specs=[pl.BlockSpec((tm, D_MODEL), row)] * 2,
        out_specs=[pl.BlockSpec((tm, D_MODEL), row), pl.BlockSpec((SUBLANES, 128), lambda i: (0, 0))],
        out_shape=[jax.ShapeDtypeStruct((s, D_MODEL), F32), jax.ShapeDtypeStruct((SUBLANES, 128), F32)],
        compiler_params=_cparams(1),
    )(y, target)


def _rms_bwd(x, r, g, dy):
    gdy = g * dy
    dx = r * gdy - x * (r * r * r) * jnp.mean(x * gdy, axis=-1, keepdims=True)
    return dx, x * r * dy


def _colsum8(a):
    t = a.shape[0]
    return a.reshape(t // SUBLANES, SUBLANES, a.shape[1]).sum(axis=0)


def _out_ple_bwd(dh2, h1, p, g2, w_out_all, w_pg_all, w_pp_all, layer):
    s = h1.shape[0]
    tm = _row_tile(s, 256)

    def body(dh2_ref, h1_ref, p_ref, g_ref, wo_ref, wpg_ref, wpp_ref,
             dh1_ref, dmix_ref, hn_ref, dgp_ref, dpp_ref, dh1b_ref, dg_ref):
        @pl.when(pl.program_id(0) == 0)
        def _():
            dg_ref[...] = jnp.zeros_like(dg_ref)

        h1 = h1_ref[...]
        dh2 = dh2_ref[...]
        g2v = g_ref[...]
        r2, hb, gate, pp = _ple_forward(h1, p_ref[...], g2v, wpg_ref, wpp_ref)
        dgp = (dh2 * pp) * gate * (1.0 - gate)
        dgpb = dgp.astype(BF16)
        dhn = jnp.concatenate([_dot_nt(dgpb, wpg_ref[sh]) for sh in range(N_CHIPS)], axis=-1)
        dx, dgrow = _rms_bwd(h1, r2, g2v, dhn)
        dh1 = dh2 + dx
        dh1b = dh1.astype(BF16)
        dh1_ref[...] = dh1
        dh1b_ref[...] = dh1b
        hn_ref[...] = hb
        dgp_ref[...] = dgpb
        dpp_ref[...] = (dh2 * gate).astype(BF16)
        dg_ref[...] += _colsum8(dgrow)
        for sh in range(N_CHIPS):
            dmix_ref[:, ROW_SHARD * sh:ROW_SHARD * (sh + 1)] = _dot_nt(dh1b, wo_ref[sh])

    row = lambda i: (i, 0)
    wspec = lambda r, cdim: pl.BlockSpec((N_CHIPS, None, r, cdim), lambda i: (0, layer, 0, 0))
    big = pl.BlockSpec((tm, D_MODEL), row)
    return pl.pallas_call(
        body, name="out_ple_bwd", grid=(s // tm,),
        in_specs=[big, big, pl.BlockSpec((tm, PLE_DIM), row), pl.BlockSpec((1, D_MODEL), lambda i: (0, 0)),
                  wspec(ROW_SHARD, D_MODEL), wspec(ROW_SHARD, D_MODEL), wspec(PLE_DIM, ROW_SHARD)],
        out_specs=[big] * 6 + [pl.BlockSpec((SUBLANES, D_MODEL), lambda i: (0, 0))],
        out_shape=[jax.ShapeDtypeStruct((s, D_MODEL), F32)] * 2 + [jax.ShapeDtypeStruct((s, D_MODEL), BF16)] * 4
        + [jax.ShapeDtypeStruct((SUBLANES, D_MODEL), F32)],
        compiler_params=_cparams(1),
    )(dh2, h1, p, g2, w_out_all, w_pg_all, w_pp_all)


def _tn_matmul(a, b, n_blocks, block_a, name, layer, into=None, first_block=0, total_blocks=None):
    s = a.shape[0]
    tk = _row_tile(s, 512)
    total_blocks = n_blocks if total_blocks is None else total_blocks
    ka, nb = a.shape[1], b.shape[1]
    if block_a:
        ka //= n_blocks
    else:
        nb //= n_blocks

    def body(*refs):
        a_ref, b_ref, o_ref = refs[0], refs[1], refs[-1]

        @pl.when(pl.program_id(0) == 0)
        def _():
            o_ref[...] = jnp.zeros_like(o_ref)

        at = a_ref[...].astype(BF16).T
        bb = b_ref[...].astype(BF16)
        for sh in range(n_blocks):
            if block_a:
                o_ref[sh] += _dot(at[ka * sh:ka * (sh + 1), :], bb)
            else:
                o_ref[sh] += _dot(at, bb[:, nb * sh:nb * (sh + 1)])

    in_specs = [pl.BlockSpec((tk, a.shape[1]), lambda i: (i, 0)), pl.BlockSpec((tk, b.shape[1]), lambda i: (i, 0))]
    operands = [a, b]
    aliases = {}
    if into is not None:
        in_specs.append(ANY)
        operands.append(into)
        aliases = {2: 0}
    return pl.pallas_call(
        body, name=name, grid=(s // tk,),
        in_specs=in_specs,
        out_specs=pl.BlockSpec((None, n_blocks, ka, nb), lambda i: (layer, first_block // n_blocks, 0, 0)),
        out_shape=jax.ShapeDtypeStruct((N_LAYERS, total_blocks, ka, nb), F32),
        input_output_aliases=aliases,
        compiler_params=_cparams(1),
    )(*operands)


def _attn_bwd(qkv, o, proj, dmix):
    s = qkv.shape[0]
    tb = ATTN_BLOCK
    nq = s // tb
    n_sub = min(ATTN_SUB_BWD, nq)
    tq = n_sub * tb
    n_chain = 2 * n_sub

    def body(q_ref, k_ref, v_ref, o_ref, g_ref, dya_ref, tri_s_ref, tri_p_ref,
             dq_ref, dk_ref, dv_ref, dg_ref, do_scr, l_scr, g_scr, s_scr, w_scr):
        i = pl.program_id(1)
        base = i * n_sub

        @pl.when(i == 0)
        def _():
            dk_ref[...] = jnp.zeros_like(dk_ref)
            dv_ref[...] = jnp.zeros_like(dv_ref)

        g = g_ref[...]
        sg = _sigmoid(g)
        dya = dya_ref[...]
        do_scr[...] = (dya * (g * sg)).astype(BF16)
        dg_ref[...] = dya * o_ref[...] * (sg * (1.0 + g * (1.0 - sg)))
        dq_ref[...] = jnp.zeros_like(dq_ref)
        g_scr[...] = jnp.zeros_like(g_scr)
        masks = _head_masks()

        def keep(t, a, h, c, r0, z, w):
            s_scr[c, t] = _sigmoid(z).astype(BF16)
            w_scr[c, t] = w.astype(BF16)

        steps = _chain_sweep(base, n_sub, q_ref, k_ref, tri_s_ref, l_scr, keep)
        row = lax.broadcasted_iota(jnp.int32, (tb, tb), 0)
        col = lax.broadcasted_iota(jnp.int32, (tb, tb), 1)

        def back(it, carry):
            t = steps - 1 - it
            r0s = [pl.multiple_of(jnp.maximum(base + a - t, 0) * tb, tb) for a in range(n_sub)]
            qhs, dohs, khs, gws = [], [], [], []
            for a in range(n_sub):
                kb = k_ref[pl.ds(r0s[a], tb), :]
                vb = v_ref[pl.ds(r0s[a], tb), :]
                qa = q_ref[a * tb:(a + 1) * tb, :]
                doa = do_scr[a * tb:(a + 1) * tb, :]
                for h, mask in enumerate(masks):
                    qhs.append(jnp.where(mask, qa, jnp.zeros_like(qa)))
                    khs.append(jnp.where(mask, kb, jnp.zeros_like(kb)))
                    dohs.append(jnp.where(mask, doa, jnp.zeros_like(doa)))
                    gws.append(w_scr[2 * a + h, t].astype(F32) * _dot_nt(dohs[-1], vb))
            parts = [_split_hilo(gw) for gw in gws]
            tri = tri_p_ref[...]
            sums = [_dot(hi, tri) + _dot(lo, tri) for hi, lo in parts]
            dzs = []
            for c, (gw, sm) in enumerate(zip(gws, sums)):
                gsum = g_scr[c]
                dz = gw - (gw + sm[:, 0:tb] + gsum) * s_scr[c, t].astype(F32)
                dz = jnp.where(col < row + t * tb, dz, 0.0)
                g_scr[c] = gsum + sm[:, tb:2 * tb]
                dzs.append(dz.astype(BF16))
            for c, dzb in enumerate(dzs):
                a = c // 2
                dk_ref[pl.ds(r0s[a], tb), :] += _dot_tn(dzb, qhs[c])
                dv_ref[pl.ds(r0s[a], tb), :] += _dot_tn(w_scr[c, t], dohs[c])
                dq_ref[a * tb:(a + 1) * tb, :] += _dot(dzb, khs[c])
            return carry

        lax.fori_loop(0, steps, back, 0)

    hp_blk = lambda off: pl.BlockSpec((tq, 2 * HEAD_DIM), lambda hp, i: (i, off + hp))
    res = lambda off: pl.BlockSpec((s, 2 * HEAD_DIM), lambda hp, i: (0, off + hp))
    tri = pl.BlockSpec((tb, 2 * tb), lambda hp, i: (0, 0))
    return pl.pallas_call(
        body, name="attn_bwd", grid=(ATTN_WIDTH // (2 * HEAD_DIM), s // tq),
        in_specs=[hp_blk(0), res(4), res(8), hp_blk(0), hp_blk(20), hp_blk(4), tri, tri],
        out_specs=[hp_blk(0), res(0), res(0), hp_blk(0)],
        out_shape=[jax.ShapeDtypeStruct((s, ATTN_WIDTH), F32)] * 4,
        scratch_shapes=[pltpu.VMEM((tq, 2 * HEAD_DIM), BF16), pltpu.VMEM((n_chain, tb, tb), F32),
                        pltpu.VMEM((n_chain, tb, tb), F32), pltpu.VMEM((n_chain, nq, tb, tb), BF16),
                        pltpu.VMEM((n_chain, nq, tb, tb), BF16)],
        compiler_params=_cparams(2),
    )(qkv, qkv, qkv, o, proj, dmix, _tri("suffix_incl"), _tri("prefix_strict"))


def _ssm_glu_bwd(dmix, y, proj, d, w_glu_all, layer, b_glu):
    s = y.shape[0]
    tm = _row_tile(s, 512)

    def body(dys_ref, y_ref, u_ref, gs_ref, d_ref, wg_ref, bg_ref,
             dyf_ref, du_ref, dgs_ref, z_ref, dzz_ref, dd_ref, db_ref):
        @pl.when(pl.program_id(0) == 0)
        def _():
            dd_ref[...] = jnp.zeros_like(dd_ref)
            db_ref[...] = jnp.zeros_like(db_ref)

        u = u_ref[...]
        dv = d_ref[...]
        yf, z, val, gate = _glu_forward(y_ref[...], u, dv, wg_ref, bg_ref[...])
        gs = gs_ref[...]
        sgs = _sigmoid(gs)
        sgate = _sigmoid(gate)
        dys = dys_ref[...]
        dgv = dys * (gs * sgs)
        dgs_ref[...] = dys * (val * sgate) * (sgs * (1.0 + gs * (1.0 - sgs)))
        dzz = jnp.concatenate([dgv * sgate, dgv * val * sgate * (1.0 - sgate)], axis=-1)
        dzzb = dzz.astype(BF16)
        dz = _dot_nt(dzzb[:, 0:ROW_SHARD], wg_ref[0])
        for sh in range(1, N_CHIPS):
            dz = dz + _dot_nt(dzzb[:, ROW_SHARD * sh:ROW_SHARD * (sh + 1)], wg_ref[sh])
        dyf = dz * _gelu_grad(yf)
        dyf_ref[...] = dyf
        du_ref[...] = dyf * dv
        z_ref[...] = z.astype(BF16)
        dzz_ref[...] = dzzb
        dd_ref[...] += _colsum8(dyf * u)
        db_ref[...] += _colsum8(dzz)

    row = lambda i: (i, 0)
    half = pl.BlockSpec((tm, SSM_WIDTH), row)
    return pl.pallas_call(
        body, name="ssm_glu_bwd", grid=(s // tm,),
        in_specs=[half, half, half, pl.BlockSpec((tm, SSM_WIDTH), lambda i: (i, 1)),
                  pl.BlockSpec((1, SSM_WIDTH), lambda i: (0, 0)),
                  pl.BlockSpec((N_CHIPS, None, SSM_WIDTH, ROW_SHARD), lambda i: (0, layer, 0, 0)),
                  pl.BlockSpec((1, 2 * SSM_WIDTH), lambda i: (0, 0))],
        out_specs=[half, half, half, half, pl.BlockSpec((tm, 2 * SSM_WIDTH), row),
                   pl.BlockSpec((SUBLANES, SSM_WIDTH), lambda i: (0, 0)),
                   pl.BlockSpec((SUBLANES, 2 * SSM_WIDTH), lambda i: (0, 0))],
        out_shape=[jax.ShapeDtypeStruct((s, SSM_WIDTH), F32)] * 3
        + [jax.ShapeDtypeStruct((s, SSM_WIDTH), BF16), jax.ShapeDtypeStruct((s, 2 * SSM_WIDTH), BF16),
           jax.ShapeDtypeStruct((SUBLANES, SSM_WIDTH), F32), jax.ShapeDtypeStruct((SUBLANES, 2 * SSM_WIDTH), F32)],
        compiler_params=_cparams(1),
    )(dmix, y, proj, proj, d, w_glu_all, b_glu)


def _ssm_scan_bwd(dyf, xs, proj, wct, coef_rev, wbt):
    s = dyf.shape[0]
    tm = _row_tile(s, 512)
    nt = s // tm

    def body(dy_ref, xs_ref, u_ref, wct_ref, coef_ref, wbt_ref, du_ref, dwc_ref, dwb_ref, da_ref, lam_ref, carry_ref):
        @pl.when(pl.program_id(1) == 0)
        def _():
            carry_ref[...] = jnp.zeros_like(carry_ref)
            dwc_ref[...] = jnp.zeros_like(dwc_ref)
            dwb_ref[...] = jnp.zeros_like(dwb_ref)
            da_ref[...] = jnp.zeros_like(da_ref)

        dyb = dy_ref[...].astype(BF16)
        lam_ref[...] = _dot(dyb, wct_ref[...])
        rows = lax.broadcasted_iota(jnp.int32, (SUBLANES, CH_S), 0)
        last = rows == SUBLANES - 1

        def extra(r0, lr, li, cr, ci):
            er = jnp.where(last, cr, pltpu.roll(lr, SUBLANES - 1, 0))
            ei = jnp.where(last, ci, pltpu.roll(li, SUBLANES - 1, 0))
            xr = xs_ref[pl.ds(r0, SUBLANES), 0:CH_S]
            xi = xs_ref[pl.ds(r0, SUBLANES), CH_S:2 * CH_S]
            da_ref[:, 0:CH_S] += xr * er + xi * ei
            da_ref[:, CH_S:2 * CH_S] += xr * ei - xi * er

        _scan_rows(lam_ref, coef_ref, carry_ref, tm // SUBLANES, reverse=True, extra=extra)
        lamb = lam_ref[...].astype(BF16)
        du_ref[...] = _dot(lamb, wbt_ref[...])
        dwc_ref[...] += _dot_tn(xs_ref[...].astype(BF16), dyb)
        dwb_ref[...] += _dot_tn(u_ref[...].astype(BF16), lamb)

    rev = lambda j, i: (nt - 1 - i, j)
    return pl.pallas_call(
        body, name="ssm_scan_bwd", grid=(SSM_CHUNKS, nt),
        in_specs=[pl.BlockSpec((tm, CH_W), rev),
                  pl.BlockSpec((None, tm, 2 * CH_S), lambda j, i: (j, nt - 1 - i, 0)),
                  pl.BlockSpec((tm, CH_W), rev),
                  pl.BlockSpec((None, CH_W, 2 * CH_S), lambda j, i: (j, 0, 0)),
                  pl.BlockSpec((None, 8, SUBLANES, CH_S), lambda j, i: (j, 0, 0, 0)),
                  pl.BlockSpec((None, 2 * CH_S, CH_W), lambda j, i: (j, 0, 0))],
        out_specs=[pl.BlockSpec((tm, CH_W), rev),
                   pl.BlockSpec((None, 2 * CH_S, CH_W), lambda j, i: (j, 0, 0)),
                   pl.BlockSpec((None, CH_W, 2 * CH_S), lambda j, i: (j, 0, 0)),
                   pl.BlockSpec((None, SUBLANES, 2 * CH_S), lambda j, i: (j, 0, 0))],
        out_shape=[jax.ShapeDtypeStruct((s, SSM_WIDTH), F32),
                   jax.ShapeDtypeStruct((SSM_CHUNKS, 2 * CH_S, CH_W), F32),
                   jax.ShapeDtypeStruct((SSM_CHUNKS, CH_W, 2 * CH_S), F32),
                   jax.ShapeDtypeStruct((SSM_CHUNKS, SUBLANES, 2 * CH_S), F32)],
        scratch_shapes=[pltpu.VMEM((tm, 2 * CH_S), F32), pltpu.VMEM((SUBLANES, 2 * CH_S), F32)],
        compiler_params=_cparams(2),
    )(dyf, xs, proj, wct, coef_rev, wbt)


def _in_proj_bwd(h, g1, w_in_all, layer, qg, kg, proj, du_a, du_b, dgs, dq, dk, dv, dga, dh1):
    s = h.shape[0]
    tm = _row_tile(s, 256)

    def body(h_ref, g_ref, w_ref, qg_ref, kg_ref, ones_ref, q_ref, k_ref, dua_ref, dub_ref, dgs_ref, dq_ref, dk_ref,
             dv_ref, dga_ref, dh1_ref, dh_ref, hn_ref, dp_ref, dg1_ref, dqg_ref, dkg_ref):
        @pl.when(pl.program_id(0) == 0)
        def _():
            dg1_ref[...] = jnp.zeros_like(dg1_ref)
            dqg_ref[...] = jnp.zeros_like(dqg_ref)
            dkg_ref[...] = jnp.zeros_like(dkg_ref)

        ones = ones_ref[...]

        def head_norm_bwd(x, gain, dy):
            r = lax.rsqrt(_dot_hilo(x * x, ones) + RMS_EPS)
            gdy = gain * dy
            dx = r * gdy - x * (r * r * r) * _dot_hilo(x * gdy, ones)
            return dx, x * r * dy

        dqr, dqg_rows = head_norm_bwd(q_ref[...], qg_ref[...], dq_ref[...] * ATTN_SCALE)
        dkr, dkg_rows = head_norm_bwd(k_ref[...], kg_ref[...], dk_ref[...])
        dqg_ref[...] += _colsum8(dqg_rows)
        dkg_ref[...] += _colsum8(dkg_rows)
        dp_ref[:, 0:512] = (dua_ref[...] + dub_ref[...]).astype(BF16)
        dp_ref[:, 512:1024] = dgs_ref[...].astype(BF16)
        dp_ref[:, 1024:1536] = dqr.astype(BF16)
        dp_ref[:, 1536:2048] = dkr.astype(BF16)
        dp_ref[:, 2048:2560] = dv_ref[...].astype(BF16)
        dp_ref[:, 2560:3072] = dga_ref[...].astype(BF16)
        dhn = _dot_nt(dp_ref[:, 0:IN_SHARD], w_ref[0])
        for sh in range(1, N_CHIPS):
            dhn = dhn + _dot_nt(dp_ref[:, IN_SHARD * sh:IN_SHARD * (sh + 1)], w_ref[sh])
        x = h_ref[...]
        gv = g_ref[...]
        r, hn = _rms_rows(x, gv)
        dx, dg_rows = _rms_bwd(x, r, gv, dhn)
        dh_ref[...] = dh1_ref[...] + dx
        hn_ref[...] = hn.astype(BF16)
        dg1_ref[...] += _colsum8(dg_rows)

    row = lambda i: (i, 0)
    full = lambda shape: pl.BlockSpec(shape, lambda i: (0,) * len(shape))
    big = pl.BlockSpec((tm, D_MODEL), row)
    half = pl.BlockSpec((tm, 512), row)
    return pl.pallas_call(
        body, name="in_proj_bwd", grid=(s // tm,),
        in_specs=[big, full((1, D_MODEL)), pl.BlockSpec((N_CHIPS, None, D_MODEL, IN_SHARD), lambda i: (0, layer, 0, 0)),
                  full((1, ATTN_WIDTH)), full((1, ATTN_WIDTH)), full((ATTN_WIDTH, ATTN_WIDTH)),
                  pl.BlockSpec((tm, 512), lambda i: (i, 2)), pl.BlockSpec((tm, 512), lambda i: (i, 3)),
                  half, half, half, half, half, half, half, big],
        out_specs=[big, big, pl.BlockSpec((tm, IN_COLS), row), pl.BlockSpec((SUBLANES, D_MODEL), lambda i: (0, 0)),
                   pl.BlockSpec((SUBLANES, ATTN_WIDTH), lambda i: (0, 0)), pl.BlockSpec((SUBLANES, ATTN_WIDTH), lambda i: (0, 0))],
        out_shape=[jax.ShapeDtypeStruct((s, D_MODEL), F32), jax.ShapeDtypeStruct((s, D_MODEL), BF16),
                   jax.ShapeDtypeStruct((s, IN_COLS), BF16), jax.ShapeDtypeStruct((SUBLANES, D_MODEL), F32),
                   jax.ShapeDtypeStruct((SUBLANES, ATTN_WIDTH), F32), jax.ShapeDtypeStruct((SUBLANES, ATTN_WIDTH), F32)],
        compiler_params=_cparams(1),
    )(h, g1, w_in_all, qg, kg, _head_ones(), proj, proj, du_a, du_b, dgs, dq, dk, dv, dga, dh1)


SMALL_NAMES = ("mix_norm_g", "ssm_a_re", "ssm_a_im", "ssm_log_dt", "ssm_b_re", "ssm_b_im", "ssm_c_re", "ssm_c_im",
               "ssm_d", "ssm_b_glu", "q_norm_g", "k_norm_g", "ple_norm_g")
SMALL_4D = ("ssm_b_re", "ssm_b_im", "ssm_c_re", "ssm_c_im")
BIG_NAMES = ("w_in", "ssm_w_glu", "w_out", "w_ple_gate", "w_ple_proj")


def _ssm_setup(sm, layer):
    col = lambda a: a[layer].reshape(1, N_STATES)
    a_re, a_im = col(sm["ssm_a_re"]), col(sm["ssm_a_im"])
    log_dt = jnp.repeat(sm["ssm_log_dt"][layer], SSM_STATE).reshape(1, N_STATES)
    b_re = sm["ssm_b_re"][layer].reshape(N_STATES, SSM_GROUP).T
    b_im = sm["ssm_b_im"][layer].reshape(N_STATES, SSM_GROUP).T
    disc_in = (a_re, a_im, log_dt, b_re, b_im)
    ab_re, ab_im, bb_re, bb_im = _disc_fwd(*disc_in)
    wb = jnp.concatenate([_block_diag_in(bb_re), _block_diag_in(bb_im)], axis=-1)
    wc = jnp.concatenate([_block_diag_out(sm["ssm_c_re"][layer]), -_block_diag_out(sm["ssm_c_im"][layer])], axis=1)
    return dict(disc_in=disc_in, wb=wb.astype(BF16), wbt=wb.transpose(0, 2, 1).astype(BF16),
                wc=wc.astype(BF16), wct=wc.transpose(0, 2, 1).astype(BF16),
                coef=_scan_coefs(ab_re, ab_im, False), coef_rev=_scan_coefs(ab_re, ab_im, True))


def _local_step(x, p, target, sm, wg):
    tile8 = lambda a: jnp.tile(a, ATTN_WIDTH // HEAD_DIM).reshape(1, ATTN_WIDTH)
    saved = []
    h = x
    for l in range(N_LAYERS):
        ssm = _ssm_setup(sm, l)
        g1 = sm["mix_norm_g"][l].reshape(1, D_MODEL)
        g2 = sm["ple_norm_g"][l].reshape(1, D_MODEL)
        qg, kg = tile8(sm["q_norm_g"][l]), tile8(sm["k_norm_g"][l])
        dsk = sm["ssm_d"][l].reshape(1, SSM_WIDTH)
        bgl = sm["ssm_b_glu"][l].reshape(1, 2 * SSM_WIDTH)
        proj, qkv = _in_proj(h, g1, wg["w_in"], l, qg, kg)
        xs, y = _ssm_scan_fwd(proj, ssm["wb"], ssm["coef"], ssm["wc"])
        ys = _ssm_glu_fwd(y, proj, dsk, wg["ssm_w_glu"], l, bgl)
        o, ya = _attn_fwd(qkv, proj)
        h1, h2 = _out_ple(h, ys, ya, p[l], g2, wg["w_out"], wg["w_ple_gate"], wg["w_ple_proj"], l)
        saved.append(dict(ssm=ssm, g1=g1, g2=g2, qg=qg, kg=kg, dsk=dsk, bgl=bgl, h=h, proj=proj, qkv=qkv, xs=xs, y=y,
                          ys=ys, o=o, ya=ya, h1=h1))
        h = h2
    dh, sq = _loss_grad(h, target)
    loss = 0.5 * jnp.sum(sq) / D_MODEL

    gbig = {n: None for n in BIG_NAMES}
    gsm = {n: [None] * N_LAYERS for n in SMALL_NAMES}
    for l in reversed(range(N_LAYERS)):
        sv = saved[l]
        ssm = sv["ssm"]
        dh1, dmix, hn2b, dgpb, dppb, dh1b, dg2 = _out_ple_bwd(dh, sv["h1"], p[l], sv["g2"], wg["w_out"],
                                                              wg["w_ple_gate"], wg["w_ple_proj"], l)
        gsm["ple_norm_g"][l] = dg2.sum(0)
        gbig["w_ple_proj"] = _tn_matmul(p[l], dppb, N_CHIPS, False, "dw_ple_proj", l, gbig["w_ple_proj"])
        gbig["w_ple_gate"] = _tn_matmul(hn2b, dgpb, N_CHIPS, True, "dw_ple_gate", l, gbig["w_ple_gate"])
        dwo = _tn_matmul(sv["ys"], dh1b, 2, True, "dw_out_ssm", l, gbig["w_out"], 0, N_CHIPS)
        gbig["w_out"] = _tn_matmul(sv["ya"], dh1b, 2, True, "dw_out_attn", l, dwo, 2, N_CHIPS)
        dqs, dkn, dv, dga = _attn_bwd(sv["qkv"], sv["o"], sv["proj"], dmix)
        dyf, du_a, dgs, zb, dzzb, dd, dbg = _ssm_glu_bwd(dmix, sv["y"], sv["proj"], sv["dsk"], wg["ssm_w_glu"], l, sv["bgl"])
        gsm["ssm_d"][l] = dd.sum(0).reshape(SSM_GROUPS, SSM_GROUP)
        gsm["ssm_b_glu"][l] = dbg.sum(0)
        gbig["ssm_w_glu"] = _tn_matmul(zb, dzzb, N_CHIPS, False, "dw_glu", l, gbig["ssm_w_glu"])
        du_b, dwc, dwb, da = _ssm_scan_bwd(dyf, sv["xs"], sv["proj"], ssm["wct"], ssm["coef_rev"], ssm["wbt"])
        gsm["ssm_c_re"][l] = _block_diag_out_t(dwc[:, 0:CH_S, :])
        gsm["ssm_c_im"][l] = -_block_diag_out_t(dwc[:, CH_S:, :])
        da = da.sum(1)
        g_ab_re = da[:, 0:CH_S].reshape(1, N_STATES)
        g_ab_im = da[:, CH_S:].reshape(1, N_STATES)
        g_bb_re = _block_diag_in_t(dwb[:, :, 0:CH_S])
        g_bb_im = _block_diag_in_t(dwb[:, :, CH_S:])
        d_are, d_aim, d_ldt, d_bre, d_bim = _disc_bwd(*ssm["disc_in"], g_ab_re, g_ab_im, g_bb_re, g_bb_im)
        gsm["ssm_a_re"][l] = d_are.reshape(SSM_GROUPS, SSM_STATE)
        gsm["ssm_a_im"][l] = d_aim.reshape(SSM_GROUPS, SSM_STATE)
        gsm["ssm_log_dt"][l] = d_ldt.reshape(SSM_GROUPS, SSM_STATE).sum(1)
        gsm["ssm_b_re"][l] = d_bre.T.reshape(SSM_GROUPS, SSM_STATE, SSM_GROUP)
        gsm["ssm_b_im"][l] = d_bim.T.reshape(SSM_GROUPS, SSM_STATE, SSM_GROUP)
        dh, hnb, dprojb, dg1, dqg, dkg = _in_proj_bwd(sv["h"], sv["g1"], wg["w_in"], l, sv["qg"], sv["kg"], sv["proj"],
                                                      du_a, du_b, dgs, dqs, dkn, dv, dga, dh1)
        gsm["mix_norm_g"][l] = dg1.sum(0)
        gsm["q_norm_g"][l] = dqg.sum(0).reshape(-1, HEAD_DIM).sum(0)
        gsm["k_norm_g"][l] = dkg.sum(0).reshape(-1, HEAD_DIM).sum(0)
        gbig["w_in"] = _tn_matmul(hnb, dprojb, N_CHIPS, False, "dw_in", l, gbig["w_in"])
    gsm = {n: jnp.stack(v, 0) for n, v in gsm.items()}
    return loss, dh, gbig, gsm


_SMALL_PAD = 8 * 8 * 128


def _pack_small(d):
    flat = jnp.concatenate([d[n].reshape(-1) for n in SMALL_NAMES])
    n = flat.shape[0]
    padded = -(-n // _SMALL_PAD) * _SMALL_PAD
    return jnp.pad(flat, (0, padded - n))


def _unpack_small(flat, like):
    out, off = {}, 0
    for n in SMALL_NAMES:
        size = like[n].size
        out[n] = flat[off:off + size].reshape(like[n].shape)
        off += size
    return out


def _reduce_grads(parts, wire_dtypes):
    n = len(parts)
    flat = [a.reshape(2, -1, a.shape[-1]) for a in parts]
    recv = _sibling_send_other_half(flat, "grad_sibling_send")
    chip = [_add_half(flat[k], recv[k], wire_dtypes[k], "grad_sibling_add").reshape(parts[k].shape[1:])
            for k in range(n)]
    got = _chip_scatter(chip, "grad_chip_scatter")
    tot = [_sum4(got[k], "grad_chip_sum") for k in range(n)]
    return _sibling_join_halves(tot, "grad_sibling_join")


def kernel(x, p, mix_norm_g, w_in, ssm_a_re, ssm_a_im, ssm_log_dt, ssm_b_re, ssm_b_im, ssm_c_re, ssm_c_im, ssm_d, ssm_w_glu, ssm_b_glu, q_norm_g, k_norm_g, w_out, ple_norm_g, w_ple_gate, w_ple_proj, loss_target, m_mix_norm_g, m_w_in, m_ssm_a_re, m_ssm_a_im, m_ssm_log_dt, m_ssm_b_re, m_ssm_b_im, m_ssm_c_re, m_ssm_c_im, m_ssm_d, m_ssm_w_glu, m_ssm_b_glu, m_q_norm_g, m_k_norm_g, m_w_out, m_ple_norm_g, m_w_ple_gate, m_w_ple_proj, v_mix_norm_g, v_w_in, v_ssm_a_re, v_ssm_a_im, v_ssm_log_dt, v_ssm_b_re, v_ssm_b_im, v_ssm_c_re, v_ssm_c_im, v_ssm_d, v_ssm_w_glu, v_ssm_b_glu, v_q_norm_g, v_k_norm_g, v_w_out, v_ple_norm_g, v_w_ple_gate, v_w_ple_proj):
    args = dict(locals())
    names = ("mix_norm_g", "w_in", "ssm_a_re", "ssm_a_im", "ssm_log_dt", "ssm_b_re", "ssm_b_im", "ssm_c_re", "ssm_c_im",
             "ssm_d", "ssm_w_glu", "ssm_b_glu", "q_norm_g", "k_norm_g", "w_out", "ple_norm_g", "w_ple_gate", "w_ple_proj")
    w = {n: args[n] for n in names}
    m = {n: args["m_" + n] for n in names}
    v = {n: args["v_" + n] for n in names}

    gathered = _chip_gather([w[n].astype(BF16) for n in BIG_NAMES], "weight_gather")
    wg = dict(zip(BIG_NAMES, gathered))
    sm = {n: w[n] for n in SMALL_NAMES}
    loss, dx, gbig, gsm = _local_step(x[0], p[:, 0], loss_target[0], sm, wg)
    loss = lax.psum(loss, ("x", "y", "c"))

    small = _pack_small(gsm)
    parts = [gbig[n] for n in BIG_NAMES] + [small.reshape(2, N_CHIPS, SUBLANES, -1)]
    red = _reduce_grads(parts, [BF16] * len(BIG_NAMES) + [F32])
    small_mine = red[-1]
    small_all = _chip_gather([small_mine], "small_grad_gather")[0]
    small_tot = small_all.transpose(1, 0, 2, 3).reshape(-1)
    g = dict(zip(BIG_NAMES, [r.reshape(w[n].shape) for r, n in zip(red[:-1], BIG_NAMES)]))
    g.update(_unpack_small(small_tot, sm))

    delta, new_m, new_v = {}, {}, {}
    for n in BIG_NAMES:
        lanes = w[n].shape[-1]
        outs = _adamw(_as_rows(w[n], lanes), _as_rows(g[n], lanes), _as_rows(m[n], lanes), _as_rows(v[n], lanes), "adamw_" + n)
        delta[n], new_m[n], new_v[n] = [o.reshape(w[n].shape) for o in outs]
    for group, per_layer in ((SMALL_4D, True), (tuple(n for n in SMALL_NAMES if n not in SMALL_4D), False)):
        outs = _adamw_many(*[[d[n] for n in group] for d in (w, g, m, v)], "adamw_small_4d" if per_layer else "adamw_small", per_layer)
        for d, o in zip((delta, new_m, new_v), outs):
            d.update(zip(group, o))

    return (loss, dx[None], *[g[n] for n in names], *[delta[n] for n in names],
            *[new_m[n] for n in names], *[new_v[n] for n in names])
```

```python
import functools
import math

import jax
import jax.numpy as jnp
from jax import lax
from jax.experimental import pallas as pl
from jax.experimental.pallas import tpu as pltpu

F32 = jnp.float32
BF16 = jnp.bfloat16

D_MODEL = 1024
N_LAYERS = 2
N_CHIPS = 4
IN_COLS = 3072
IN_SHARD = IN_COLS // N_CHIPS
SSM_WIDTH = 512
SSM_GROUP = 16
SSM_GROUPS = 32
SSM_STATE = 64
N_STATES = SSM_GROUPS * SSM_STATE
SSM_CHUNKS = 4
CH_W = SSM_WIDTH // SSM_CHUNKS
CH_S = N_STATES // SSM_CHUNKS
ATTN_WIDTH = 512
HEAD_DIM = 64
PLE_DIM = 256
ROW_SHARD = 256
RMS_EPS = 1e-6
ATTN_SCALE = HEAD_DIM ** -0.5
ATTN_BLOCK = 128
EXP_ZERO = -87.5
SUBLANES = 8
V7X_VMEM_LIMIT = 52 * 1024 * 1024

ADAM_LR = 0.001
ADAM_B1 = 0.9
ADAM_B2 = 0.999
ADAM_EPS = 1e-08
ADAM_WD = 0.01
ADAM_STEP = 10

MESH = pl.DeviceIdType.MESH
ANY = pl.BlockSpec(memory_space=pl.ANY)


def _cparams(n_grid=0, parallel=0):
    sem = tuple(["parallel"] * parallel + ["arbitrary"] * (n_grid - parallel))
    return pltpu.CompilerParams(dimension_semantics=sem, vmem_limit_bytes=V7X_VMEM_LIMIT)


def _dot(a, b):
    return jnp.dot(a, b, preferred_element_type=F32)


def _dot_nt(a, b):
    return lax.dot_general(a, b, (((1,), (1,)), ((), ())), preferred_element_type=F32)


def _dot_tn(a, b):
    return lax.dot_general(a, b, (((0,), (0,)), ((), ())), preferred_element_type=F32)


def _split_hilo(a):
    hi = a.astype(BF16)
    lo = (a - hi.astype(F32)).astype(BF16)
    return hi, lo


def _dot_hilo(a, b):
    hi, lo = _split_hilo(a)
    return _dot(hi, b) + _dot(lo, b)


def _sigmoid(x):
    return 0.5 * (jnp.tanh(0.5 * x) + 1.0)


_GELU_C = math.sqrt(2.0 / math.pi)


def _gelu(x):
    return 0.5 * x * (1.0 + jnp.tanh(_GELU_C * (x + 0.044715 * (x * x * x))))


def _gelu_grad(x):
    t = jnp.tanh(_GELU_C * (x + 0.044715 * (x * x * x)))
    return 0.5 * (1.0 + t) + 0.5 * x * (1.0 - t * t) * (_GELU_C * (1.0 + 3.0 * 0.044715 * (x * x)))


def _row_tile(s, want):
    for t in range(min(s, want), 7, -1):
        if s % t == 0 and t % SUBLANES == 0:
            return t
    return s


def _coords():
    return lax.axis_index("x"), lax.axis_index("y"), lax.axis_index("c")


def _other_chips(x, y):
    return [(1 - x, y), (x, 1 - y), (1 - x, 1 - y)]


def _remote(src, dst, send_sem, recv_sem, dev):
    return pltpu.make_async_remote_copy(src_ref=src, dst_ref=dst, send_sem=send_sem, recv_sem=recv_sem,
                                        device_id=dev, device_id_type=MESH)


def _set_block(buf, block, index):
    return lax.dynamic_update_index_in_dim(buf, block, index, 0)


def _gather_sems(n):
    return [pltpu.SemaphoreType.DMA((3 * n,)) for _ in range(4)]


def _gather_copies(ins, bases, outs, sems):
    send_sems, recv_sems, fwd_send, fwd_recv = sems
    x, y, c = _coords()
    me_chip = 2 * x + y
    sibling = (x, y, 1 - c)
    first, landed, passed, from_sibling = [], [], [], []
    for k in range(len(ins)):
        for j, (cx, cy) in enumerate(_other_chips(x, y)):
            i = 3 * k + j
            first.append(_remote(ins[k].at[bases[k] + c], outs[k].at[me_chip, c], send_sems.at[i], recv_sems.at[i], (cx, cy, c)))
            blk = outs[k].at[2 * cx + cy, c]
            landed.append(_remote(blk, blk, send_sems.at[i], recv_sems.at[i], (cx, cy, c)))
            passed.append(_remote(blk, blk, fwd_send.at[i], fwd_recv.at[i], sibling))
            blk = outs[k].at[2 * cx + cy, 1 - c]
            from_sibling.append(_remote(blk, blk, fwd_send.at[i], fwd_recv.at[i], sibling))
    return first, landed, passed, from_sibling


def _gather_start(ins, bases, outs, sems):
    for cp in _gather_copies(ins, bases, outs, sems)[0]:
        cp.start()


def _gather_finish(ins, bases, outs, sems):
    first, landed, passed, from_sibling = _gather_copies(ins, bases, outs, sems)
    for arrived, forward in zip(landed, passed):
        arrived.wait_recv()
        forward.start()
    for cp in from_sibling:
        cp.wait_recv()
    for cp in first + passed:
        cp.wait_send()


def _gather_outputs(arrs):
    return [jax.ShapeDtypeStruct((N_CHIPS, 2) + a.shape[1:], a.dtype) for a in arrs]


def _gather_own(outs, arrs, bases):
    me_chip = 2 * lax.axis_index("x") + lax.axis_index("y")
    return [_set_block(o, lax.slice_in_dim(a, b, b + 2, axis=0), me_chip) for o, a, b in zip(outs, arrs, bases)]


def _chip_gather(arrs, name, bases=None):
    n = len(arrs)
    bases = [0] * n if bases is None else bases

    def body(*refs):
        ins, outs, sems = refs[:n], refs[n:2 * n], refs[2 * n:]
        _gather_start(ins, bases, outs, sems)
        _gather_finish(ins, bases, outs, sems)

    outs = pl.pallas_call(
        body, name=name, out_shape=_gather_outputs(arrs),
        in_specs=[ANY] * n, out_specs=[ANY] * n, scratch_shapes=_gather_sems(n),
    )(*arrs)
    return _gather_own(outs, arrs, bases)


def _sibling_send_other_half(arrs, name):
    n = len(arrs)

    def body(*refs):
        ins, outs = refs[:n], refs[n:2 * n]
        send_sems, recv_sems = refs[2 * n:]
        x, y, c = _coords()
        cps = [_remote(ins[k].at[1 - c], outs[k], send_sems.at[k], recv_sems.at[k], (x, y, 1 - c)) for k in range(n)]
        for cp in cps:
            cp.start()
        for cp in cps:
            cp.wait_recv()
        for cp in cps:
            cp.wait_send()

    return pl.pallas_call(
        body, name=name,
        out_shape=[jax.ShapeDtypeStruct(a.shape[1:], a.dtype) for a in arrs],
        in_specs=[ANY] * n, out_specs=[ANY] * n,
        scratch_shapes=[pltpu.SemaphoreType.DMA((n,)), pltpu.SemaphoreType.DMA((n,))],
    )(*arrs)


def _sibling_join_halves(arrs, name):
    n = len(arrs)

    def body(*refs):
        ins, outs = refs[:n], refs[n:2 * n]
        send_sems, recv_sems = refs[2 * n:]
        x, y, c = _coords()
        cps = [_remote(ins[k], outs[k].at[c], send_sems.at[k], recv_sems.at[k], (x, y, 1 - c)) for k in range(n)]
        for cp in cps:
            cp.start()
        for k in range(n):
            blk = outs[k].at[1 - c]
            _remote(blk, blk, send_sems.at[k], recv_sems.at[k], (x, y, 1 - c)).wait_recv()
        for cp in cps:
            cp.wait_send()

    outs = pl.pallas_call(
        body, name=name,
        out_shape=[jax.ShapeDtypeStruct((2,) + a.shape, a.dtype) for a in arrs],
        in_specs=[ANY] * n, out_specs=[ANY] * n,
        scratch_shapes=[pltpu.SemaphoreType.DMA((n,)), pltpu.SemaphoreType.DMA((n,))],
    )(*arrs)
    c = lax.axis_index("c")
    return [_set_block(o, a, c) for o, a in zip(outs, arrs)]


def _chip_scatter(arrs, name):
    n = len(arrs)

    def body(*refs):
        ins, outs = refs[:n], refs[n:2 * n]
        send_sems, recv_sems = refs[2 * n:]
        x, y, c = _coords()
        me_chip = 2 * x + y
        chips = _other_chips(x, y)
        cps = []
        for k in range(n):
            for j, (cx, cy) in enumerate(chips):
                cp = _remote(ins[k].at[2 * cx + cy], outs[k].at[me_chip], send_sems.at[3 * k + j],
                             recv_sems.at[3 * k + j], (cx, cy, c))
                cp.start()
                cps.append(cp)
        for k in range(n):
            for j, (cx, cy) in enumerate(chips):
                blk = outs[k].at[2 * cx + cy]
                _remote(blk, blk, send_sems.at[3 * k + j], recv_sems.at[3 * k + j], (cx, cy, c)).wait_recv()
        for cp in cps:
            cp.wait_send()

    outs = pl.pallas_call(
        body, name=name,
        out_shape=[jax.ShapeDtypeStruct(a.shape, a.dtype) for a in arrs],
        in_specs=[ANY] * n, out_specs=[ANY] * n,
        scratch_shapes=[pltpu.SemaphoreType.DMA((3 * n,)), pltpu.SemaphoreType.DMA((3 * n,))],
    )(*arrs)
    me_chip = 2 * lax.axis_index("x") + lax.axis_index("y")
    return [_set_block(o, lax.dynamic_index_in_dim(a, me_chip, 0, keepdims=False), me_chip) for o, a in zip(outs, arrs)]


def _as_rows(a, lanes):
    return a.reshape(-1, lanes)


def _add_half(full, recv, out_dtype, name):
    _, r, cdim = full.shape
    tr = _row_tile(r, 512)

    def body(c_ref, a_ref, b_ref, o_ref):
        o_ref[...] = (a_ref[...] + b_ref[...]).astype(out_dtype)

    c = lax.axis_index("c").astype(jnp.int32).reshape(1)
    return pl.pallas_call(
        body, name=name,
        grid_spec=pltpu.PrefetchScalarGridSpec(
            num_scalar_prefetch=1, grid=(r // tr,),
            in_specs=[pl.BlockSpec((None, tr, cdim), lambda i, c_ref: (c_ref[0], i, 0)),
                      pl.BlockSpec((tr, cdim), lambda i, c_ref: (i, 0))],
            out_specs=pl.BlockSpec((tr, cdim), lambda i, c_ref: (i, 0))),
        out_shape=jax.ShapeDtypeStruct((r, cdim), out_dtype),
        compiler_params=_cparams(1),
    )(c, full, recv)


def _sum4(parts, name):
    _, r, cdim = parts.shape
    tr = _row_tile(r, 512)

    def body(p_ref, o_ref):
        acc = p_ref[0].astype(F32) + p_ref[1].astype(F32)
        acc = acc + p_ref[2].astype(F32)
        o_ref[...] = acc + p_ref[3].astype(F32)

    return pl.pallas_call(
        body, name=name, grid=(r // tr,),
        in_specs=[pl.BlockSpec((N_CHIPS, tr, cdim), lambda i: (0, i, 0))],
        out_specs=pl.BlockSpec((tr, cdim), lambda i: (i, 0)),
        out_shape=jax.ShapeDtypeStruct((r, cdim), F32),
        compiler_params=_cparams(1),
    )(parts)


def _adamw_math(w, g, m, v):
    c1 = 1.0 - ADAM_B1 ** ADAM_STEP
    c2 = 1.0 - ADAM_B2 ** ADAM_STEP
    nm = ADAM_B1 * m + (1.0 - ADAM_B1) * g
    nv = ADAM_B2 * v + (1.0 - ADAM_B2) * (g * g)
    delta = -ADAM_LR * ((nm / c1) / (jnp.sqrt(nv / c2) + ADAM_EPS) + ADAM_WD * w)
    return delta, nm, nv


def _adamw(w, g, m, v, name):
    r, cdim = w.shape
    tr = _row_tile(r, 256)

    def body(w_ref, g_ref, m_ref, v_ref, d_ref, nm_ref, nv_ref):
        d_ref[...], nm_ref[...], nv_ref[...] = _adamw_math(w_ref[...], g_ref[...], m_ref[...], v_ref[...])

    spec = pl.BlockSpec((tr, cdim), lambda i: (i, 0))
    return pl.pallas_call(
        body, name=name, grid=(r // tr,),
        in_specs=[spec] * 4, out_specs=[spec] * 3,
        out_shape=[jax.ShapeDtypeStruct((r, cdim), F32)] * 3,
        compiler_params=_cparams(1),
    )(w, g, m, v)


def _adamw_many(ws, gs, ms, vs, name, per_layer):
    n = len(ws)

    def body(*refs):
        for k in range(n):
            w, g, m, v = (refs[j * n + k][...] for j in range(4))
            outs = _adamw_math(w, g, m, v)
            for j in range(3):
                refs[(4 + j) * n + k][...] = outs[j]

    shapes = [jax.ShapeDtypeStruct(w.shape, F32) for w in ws]
    if per_layer:
        specs = [pl.BlockSpec((None,) + w.shape[1:], lambda l, nd=w.ndim: (l,) + (0,) * (nd - 1)) for w in ws]
        call = pl.pallas_call(body, name=name, grid=(N_LAYERS,), in_specs=specs * 4, out_specs=specs * 3,
                              out_shape=shapes * 3, compiler_params=_cparams(1))
    else:
        call = pl.pallas_call(body, name=name, out_shape=shapes * 3, compiler_params=_cparams())
    outs = call(*ws, *gs, *ms, *vs)
    return outs[0:n], outs[n:2 * n], outs[2 * n:3 * n]


def _discretise(a_re, a_im, log_dt, b_re, b_im):
    dt = jnp.exp(log_dt)
    mag = jnp.exp(a_re * dt)
    ab_re = mag * jnp.cos(a_im * dt)
    ab_im = mag * jnp.sin(a_im * dt)
    num_re = ab_re - 1.0
    num_im = ab_im
    den = a_re * a_re + a_im * a_im
    f_re = (num_re * a_re + num_im * a_im) / den
    f_im = (num_im * a_re - num_re * a_im) / den
    bb_re = f_re * b_re - f_im * b_im
    bb_im = f_re * b_im + f_im * b_re
    return ab_re, ab_im, bb_re, bb_im


def _disc_shapes():
    col = jax.ShapeDtypeStruct((1, N_STATES), F32)
    mat = jax.ShapeDtypeStruct((SSM_GROUP, N_STATES), F32)
    return col, mat


def _disc_fwd(a_re, a_im, log_dt, b_re, b_im):
    col, mat = _disc_shapes()

    def body(ar, ai, ld, br, bi, o0, o1, o2, o3):
        outs = _discretise(ar[...], ai[...], ld[...], br[...], bi[...])
        for o, val in zip((o0, o1, o2, o3), outs):
            o[...] = val

    return pl.pallas_call(body, name="ssm_discretise", out_shape=[col, col, mat, mat],
                          compiler_params=_cparams())(a_re, a_im, log_dt, b_re, b_im)


def _disc_bwd(a_re, a_im, log_dt, b_re, b_im, g_ab_re, g_ab_im, g_bb_re, g_bb_im):
    col, mat = _disc_shapes()

    def body(ar, ai, ld, br, bi, g0, g1, g2, g3, o0, o1, o2, o3, o4):
        _, vjp = jax.vjp(_discretise, ar[...], ai[...], ld[...], br[...], bi[...])
        grads = vjp((g0[...], g1[...], g2[...], g3[...]))
        for o, val in zip((o0, o1, o2, o3, o4), grads):
            o[...] = val

    return pl.pallas_call(body, name="ssm_discretise_bwd", out_shape=[col, col, col, mat, mat],
                          compiler_params=_cparams())(a_re, a_im, log_dt, b_re, b_im, g_ab_re, g_ab_im, g_bb_re, g_bb_im)


def _cmul(ar, ai, br, bi):
    return ar * br - ai * bi, ar * bi + ai * br


def _scan_coefs(ab_re, ab_im, reverse):
    ar = ab_re.reshape(1, N_STATES)
    ai = ab_im.reshape(1, N_STATES)
    if reverse:
        ai = -ai
    a2 = _cmul(ar, ai, ar, ai)
    a4 = _cmul(*a2, *a2)
    rows = jnp.arange(SUBLANES)[:, None]
    out = []
    for (pr, pi), sh in (((ar, ai), 1), (a2, 2), (a4, 4)):
        keep = (rows <= SUBLANES - 1 - sh) if reverse else (rows >= sh)
        out += [jnp.where(keep, pr, 0.0), jnp.where(keep, pi, 0.0)]
    pows = [(ar, ai)]
    for _ in range(SUBLANES - 1):
        pows.append(_cmul(*pows[-1], ar, ai))
    order = pows[::-1] if reverse else pows
    out += [jnp.concatenate([p[0] for p in order], 0), jnp.concatenate([p[1] for p in order], 0)]
    t = jnp.stack(out, 0)
    return t.reshape(8, SUBLANES, SSM_CHUNKS, CH_S).transpose(2, 0, 1, 3)


def _block_diag_in(bb):
    t = bb.reshape(SSM_GROUP, SSM_CHUNKS, 8, SSM_STATE)
    eye = jnp.eye(8, dtype=bb.dtype)
    return jnp.einsum("hjgp,gk->jghkp", t, eye).reshape(SSM_CHUNKS, CH_W, CH_S)


def _block_diag_in_t(d):
    t = d.reshape(SSM_CHUNKS, 8, SSM_GROUP, 8, SSM_STATE)
    return jnp.einsum("jghgp->hjgp", t).reshape(SSM_GROUP, N_STATES)


def _block_diag_out(c):
    t = c.reshape(SSM_CHUNKS, 8, SSM_GROUP, SSM_STATE)
    eye = jnp.eye(8, dtype=c.dtype)
    return jnp.einsum("jghp,gk->jgpkh", t, eye).reshape(SSM_CHUNKS, CH_S, CH_W)


def _block_diag_out_t(d):
    t = d.reshape(SSM_CHUNKS, 8, SSM_STATE, 8, SSM_GROUP)
    return jnp.einsum("jgpgh->jghp", t).reshape(SSM_GROUPS, SSM_GROUP, SSM_STATE)


def _head_ones():
    r = jnp.arange(ATTN_WIDTH) // HEAD_DIM
    return jnp.where(r[:, None] == r[None, :], 1.0 / HEAD_DIM, 0.0).astype(BF16)


def _in_proj(h, g1, w_in_l, qg, kg):
    s = h.shape[0]
    tm = _row_tile(s, 256)

    def body(h_ref, g_ref, w_ref, qg_ref, kg_ref, ones_ref, proj_ref, qkv_ref):
        x = h_ref[...]
        r = lax.rsqrt(jnp.mean(x * x, axis=-1, keepdims=True) + RMS_EPS)
        hn = (x * r * g_ref[...]).astype(BF16)
        for sh in range(N_CHIPS):
            proj_ref[:, IN_SHARD * sh:IN_SHARD * (sh + 1)] = _dot(hn, w_ref[sh])
        ones = ones_ref[...]
        q = proj_ref[:, 1024:1536]
        k = proj_ref[:, 1536:2048]
        rq = lax.rsqrt(_dot_hilo(q * q, ones) + RMS_EPS)
        rk = lax.rsqrt(_dot_hilo(k * k, ones) + RMS_EPS)
        qkv_ref[:, 0:512] = (q * rq * qg_ref[...] * ATTN_SCALE).astype(BF16)
        qkv_ref[:, 512:1024] = (k * rk * kg_ref[...]).astype(BF16)
        qkv_ref[:, 1024:1536] = proj_ref[:, 2048:2560].astype(BF16)

    full = lambda shape: pl.BlockSpec(shape, lambda i: (0,) * len(shape))
    return pl.pallas_call(
        body, name="in_proj", grid=(s // tm,),
        in_specs=[pl.BlockSpec((tm, D_MODEL), lambda i: (i, 0)), full((1, D_MODEL)),
                  full((N_CHIPS, D_MODEL, IN_SHARD)),
                  full((1, ATTN_WIDTH)), full((1, ATTN_WIDTH)), full((ATTN_WIDTH, ATTN_WIDTH))],
        out_specs=[pl.BlockSpec((tm, IN_COLS), lambda i: (i, 0)), pl.BlockSpec((tm, 3 * ATTN_WIDTH), lambda i: (i, 0))],
        out_shape=[jax.ShapeDtypeStruct((s, IN_COLS), F32), jax.ShapeDtypeStruct((s, 3 * ATTN_WIDTH), BF16)],
        compiler_params=_cparams(1),
    )(h, g1, w_in_l, qg, kg, _head_ones())


def _scan_rows(x_ref, coef_ref, carry_ref, n_blocks, reverse, extra=None):
    c = [coef_ref[a] for a in range(8)]
    shifts = (7, 6, 4) if reverse else (1, 2, 4)
    edge = 0 if reverse else SUBLANES - 1

    def blk(b, carry):
        bb = (n_blocks - 1 - b) if reverse else b
        r0 = pl.multiple_of(bb * SUBLANES, SUBLANES)
        xr = x_ref[pl.ds(r0, SUBLANES), 0:CH_S]
        xi = x_ref[pl.ds(r0, SUBLANES), CH_S:2 * CH_S]
        for lvl, sh in enumerate(shifts):
            ar, ai = c[2 * lvl], c[2 * lvl + 1]
            sr = pltpu.roll(xr, sh, 0)
            si = pltpu.roll(xi, sh, 0)
            xr, xi = xr + (ar * sr - ai * si), xi + (ar * si + ai * sr)
        cr, ci = carry
        xr, xi = xr + (c[6] * cr - c[7] * ci), xi + (c[6] * ci + c[7] * cr)
        x_ref[pl.ds(r0, SUBLANES), 0:CH_S] = xr
        x_ref[pl.ds(r0, SUBLANES), CH_S:2 * CH_S] = xi
        if extra is not None:
            extra(r0, xr, xi, cr, ci)
        return (jnp.broadcast_to(xr[edge:edge + 1, :], (SUBLANES, CH_S)),
                jnp.broadcast_to(xi[edge:edge + 1, :], (SUBLANES, CH_S)))

    cr, ci = lax.fori_loop(0, n_blocks, blk, (carry_ref[:, 0:CH_S], carry_ref[:, CH_S:2 * CH_S]))
    carry_ref[:, 0:CH_S] = cr
    carry_ref[:, CH_S:2 * CH_S] = ci


def _ssm_scan_fwd(proj, wb, coef, wc, gather=None, gather_bases=None):
    s = proj.shape[0]
    tm = _row_tile(s, 512)
    nt = s // tm
    gather = [] if gather is None else gather
    ng = len(gather)

    def body(*refs):
        u_ref, wb_ref, coef_ref, wc_ref = refs[0:4]
        g_ins = refs[4:4 + ng]
        xs_ref, y_ref = refs[4 + ng:6 + ng]
        g_outs = refs[6 + ng:6 + 2 * ng]
        carry_ref = refs[6 + 2 * ng]
        sems = refs[7 + 2 * ng:]
        j, i = pl.program_id(0), pl.program_id(1)

        @pl.when(i == 0)
        def _():
            carry_ref[...] = jnp.zeros_like(carry_ref)

        if ng:
            @pl.when(jnp.logical_and(j == 0, i == 0))
            def _():
                _gather_start(g_ins, gather_bases, g_outs, sems)

        xs_ref[...] = _dot(u_ref[...].astype(BF16), wb_ref[...])
        _scan_rows(xs_ref, coef_ref, carry_ref, tm // SUBLANES, reverse=False)
        y_ref[...] = _dot(xs_ref[...].astype(BF16), wc_ref[...])

        if ng:
            @pl.when(jnp.logical_and(j == SSM_CHUNKS - 1, i == nt - 1))
            def _():
                _gather_finish(g_ins, gather_bases, g_outs, sems)

    outs = pl.pallas_call(
        body, name="ssm_scan_gather" if ng else "ssm_scan", grid=(SSM_CHUNKS, nt),
        in_specs=[pl.BlockSpec((tm, CH_W), lambda j, i: (i, j)),
                  pl.BlockSpec((None, CH_W, 2 * CH_S), lambda j, i: (j, 0, 0)),
                  pl.BlockSpec((None, 8, SUBLANES, CH_S), lambda j, i: (j, 0, 0, 0)),
                  pl.BlockSpec((None, 2 * CH_S, CH_W), lambda j, i: (j, 0, 0))] + [ANY] * ng,
        out_specs=[pl.BlockSpec((None, tm, 2 * CH_S), lambda j, i: (j, i, 0)),
                   pl.BlockSpec((tm, CH_W), lambda j, i: (i, j))] + [ANY] * ng,
        out_shape=[jax.ShapeDtypeStruct((SSM_CHUNKS, s, 2 * CH_S), F32), jax.ShapeDtypeStruct((s, SSM_WIDTH), F32)]
        + _gather_outputs(gather),
        scratch_shapes=[pltpu.VMEM((SUBLANES, 2 * CH_S), F32)] + (_gather_sems(ng) if ng else []),
        compiler_params=_cparams(2),
    )(proj, wb, coef, wc, *gather)
    return outs[0], outs[1], (_gather_own(outs[2:], gather, gather_bases) if ng else [])


def _glu_forward(y, u, d, wg_ref, bg):
    yf = y + d * u
    z = _gelu(yf)
    zb = z.astype(BF16)
    zz = jnp.concatenate([_dot(zb, wg_ref[sh]) for sh in range(N_CHIPS)], axis=-1) + bg
    return yf, z, zz[:, 0:SSM_WIDTH], zz[:, SSM_WIDTH:2 * SSM_WIDTH]


def _ssm_glu_fwd(y, proj, d, w_glu_all, layer, b_glu):
    s = y.shape[0]
    tm = _row_tile(s, 512)

    def body(y_ref, u_ref, gs_ref, d_ref, wg_ref, bg_ref, o_ref):
        _, _, val, gate = _glu_forward(y_ref[...], u_ref[...], d_ref[...], wg_ref, bg_ref[...])
        gs = gs_ref[...]
        o_ref[...] = val * _sigmoid(gate) * (gs * _sigmoid(gs))

    row = lambda i: (i, 0)
    return pl.pallas_call(
        body, name="ssm_glu", grid=(s // tm,),
        in_specs=[pl.BlockSpec((tm, SSM_WIDTH), row), pl.BlockSpec((tm, SSM_WIDTH), row),
                  pl.BlockSpec((tm, SSM_WIDTH), lambda i: (i, 1)), pl.BlockSpec((1, SSM_WIDTH), lambda i: (0, 0)),
                  pl.BlockSpec((N_CHIPS, None, SSM_WIDTH, ROW_SHARD), lambda i: (0, layer, 0, 0)),
                  pl.BlockSpec((1, 2 * SSM_WIDTH), lambda i: (0, 0))],
        out_specs=pl.BlockSpec((tm, SSM_WIDTH), row),
        out_shape=jax.ShapeDtypeStruct((s, SSM_WIDTH), F32),
        compiler_params=_cparams(1),
    )(y, proj, proj, d, w_glu_all, b_glu)


def _tri(kind):
    r = jnp.arange(ATTN_BLOCK)
    if kind == "suffix_incl":
        m = r[:, None] >= r[None, :]
    else:
        m = r[:, None] < r[None, :]
    return jnp.concatenate([m, jnp.ones_like(m)], axis=1).astype(BF16)


def _head_masks():
    lane = lax.broadcasted_iota(jnp.int32, (1, 2 * HEAD_DIM), 1)
    return [lane < HEAD_DIM, lane >= HEAD_DIM]


def _chain_step(t, base, n_sub, first, q_ref, k_ref, tri_ref, l_scr, per_chain):
    tb = ATTN_BLOCK
    row = lax.broadcasted_iota(jnp.int32, (tb, tb), 0)
    col = lax.broadcasted_iota(jnp.int32, (tb, tb), 1)
    masks = _head_masks()
    blks = [base + a - t for a in range(n_sub)]
    r0s = [pl.multiple_of(jnp.maximum(blk, 0) * tb, tb) for blk in blks]
    zs = []
    for a in range(n_sub):
        kb = k_ref[pl.ds(r0s[a], tb), :]
        qa = q_ref[a * tb:(a + 1) * tb, :]
        for mask in masks:
            zs.append(_dot_nt(jnp.where(mask, qa, jnp.zeros_like(qa)), kb))
    parts = []
    for z in zs:
        ls = jnp.minimum(-z, 0.0) - jnp.log(1.0 + jnp.exp(-jnp.abs(z)))
        if first:
            ls = jnp.where(col < row, ls, 0.0)
        parts.append(_split_hilo(ls))
    tri = tri_ref[...]
    sums = [_dot(hi, tri) + _dot(lo, tri) for hi, lo in parts]
    top = None
    ws = []
    for c, (z, sm) in enumerate(zip(zs, sums)):
        if first:
            lsum = jnp.zeros((tb, tb), F32)
        else:
            lsum = l_scr[c] + jnp.where(blks[c // 2] >= 0, 0.0, -1e30)
        w = jnp.exp(z + sm[:, 0:tb] + lsum)
        if first:
            w = jnp.where(col < row, w, 0.0)
        ws.append(w)
        lsum = lsum + sm[:, tb:2 * tb]
        l_scr[c] = lsum
        top = lsum if top is None else jnp.maximum(top, lsum)
    for c, (z, w) in enumerate(zip(zs, ws)):
        per_chain(c // 2, c % 2, c, r0s[c // 2], z, w)
    return jnp.max(top)


def _chain_sweep(base, n_sub, q_ref, k_ref, tri_ref, l_scr, per_chain):
    top = _chain_step(0, base, n_sub, True, q_ref, k_ref, tri_ref, l_scr, functools.partial(per_chain, 0))

    def cond(carry):
        t, top = carry
        return jnp.logical_and(t <= base + n_sub - 1, top > EXP_ZERO)

    def step(carry):
        t, _ = carry
        return t + 1, _chain_step(t, base, n_sub, False, q_ref, k_ref, tri_ref, l_scr, functools.partial(per_chain, t))

    steps, _ = lax.while_loop(cond, step, (jnp.int32(1), top))
    return steps


ATTN_SUB_FWD = 4
ATTN_SUB_BWD = 4


def _attn_fwd(qkv, proj):
    s = qkv.shape[0]
    tb = ATTN_BLOCK
    n_sub = min(ATTN_SUB_FWD, s // tb)
    tq = n_sub * tb

    def body(q_ref, k_ref, v_ref, g_ref, tri_ref, o_ref, ya_ref, l_scr):
        i = pl.program_id(1)
        masks = _head_masks()
        o_ref[...] = jnp.zeros_like(o_ref)

        def per_chain(t, a, h, c, r0, z, w):
            vb = v_ref[pl.ds(r0, tb), :]
            vb = jnp.where(masks[h], vb, jnp.zeros_like(vb))
            o_ref[a * tb:(a + 1) * tb, :] += _dot(w.astype(BF16), vb)

        _chain_sweep(i * n_sub, n_sub, q_ref, k_ref, tri_ref, l_scr, per_chain)
        g = g_ref[...]
        ya_ref[...] = o_ref[...] * (g * _sigmoid(g))

    hp_blk = lambda off: pl.BlockSpec((tq, 2 * HEAD_DIM), lambda hp, i: (i, off + hp))
    res = lambda off: pl.BlockSpec((s, 2 * HEAD_DIM), lambda hp, i: (0, off + hp))
    return pl.pallas_call(
        body, name="attn_fwd", grid=(ATTN_WIDTH // (2 * HEAD_DIM), s // tq),
        in_specs=[hp_blk(0), res(4), res(8), hp_blk(20), pl.BlockSpec((tb, 2 * tb), lambda hp, i: (0, 0))],
        out_specs=[hp_blk(0), hp_blk(0)],
        out_shape=[jax.ShapeDtypeStruct((s, ATTN_WIDTH), F32)] * 2,
        scratch_shapes=[pltpu.VMEM((2 * n_sub, tb, tb), F32)],
        compiler_params=_cparams(2),
    )(qkv, qkv, qkv, proj, _tri("suffix_incl"))


def _rms_rows(x, g):
    r = lax.rsqrt(jnp.mean(x * x, axis=-1, keepdims=True) + RMS_EPS)
    return r, x * r * g


def _ple_forward(h1, p, g2, wpg_ref, wpp_ref):
    r2, hn2 = _rms_rows(h1, g2)
    hb = hn2.astype(BF16)
    gpre = _dot(hb[:, 0:ROW_SHARD], wpg_ref[0])
    for sh in range(1, N_CHIPS):
        gpre = gpre + _dot(hb[:, ROW_SHARD * sh:ROW_SHARD * (sh + 1)], wpg_ref[sh])
    gate = _sigmoid(gpre)
    pb = p.astype(BF16)
    pp = jnp.concatenate([_dot(pb, wpp_ref[sh]) for sh in range(N_CHIPS)], axis=-1)
    return r2, hb, gate, pp


def _out_ple(h, ys, ya, p, g2, w_out_all, w_pg_all, w_pp_all, layer):
    s = h.shape[0]
    tm = _row_tile(s, 256)

    def body(h_ref, ys_ref, ya_ref, p_ref, g_ref, wo_ref, wpg_ref, wpp_ref, h1_ref, h2_ref):
        ysb = ys_ref[...].astype(BF16)
        yab = ya_ref[...].astype(BF16)
        h1 = h_ref[...]
        for sh, src in enumerate((ysb[:, 0:ROW_SHARD], ysb[:, ROW_SHARD:], yab[:, 0:ROW_SHARD], yab[:, ROW_SHARD:])):
            h1 = h1 + _dot(src, wo_ref[sh])
        _, _, gate, pp = _ple_forward(h1, p_ref[...], g_ref[...], wpg_ref, wpp_ref)
        h1_ref[...] = h1
        h2_ref[...] = h1 + gate * pp

    row = lambda i: (i, 0)
    wspec = lambda r, cdim: pl.BlockSpec((N_CHIPS, None, r, cdim), lambda i: (0, layer, 0, 0))
    return pl.pallas_call(
        body, name="out_ple", grid=(s // tm,),
        in_specs=[pl.BlockSpec((tm, D_MODEL), row), pl.BlockSpec((tm, SSM_WIDTH), row), pl.BlockSpec((tm, ATTN_WIDTH), row),
                  pl.BlockSpec((tm, PLE_DIM), row), pl.BlockSpec((1, D_MODEL), lambda i: (0, 0)),
                  wspec(ROW_SHARD, D_MODEL), wspec(ROW_SHARD, D_MODEL), wspec(PLE_DIM, ROW_SHARD)],
        out_specs=[pl.BlockSpec((tm, D_MODEL), row)] * 2,
        out_shape=[jax.ShapeDtypeStruct((s, D_MODEL), F32)] * 2,
        compiler_params=_cparams(1),
    )(h, ys, ya, p, g2, w_out_all, w_pg_all, w_pp_all)


def _loss_grad(y, target):
    s = y.shape[0]
    tm = _row_tile(s, 512)

    def body(y_ref, t_ref, dy_ref, acc_ref):
        @pl.when(pl.program_id(0) == 0)
        def _():
            acc_ref[...] = jnp.zeros_like(acc_ref)

        e = y_ref[...] - t_ref[...]
        dy_ref[...] = e / D_MODEL
        sq = (e * e).reshape(tm // SUBLANES, SUBLANES, D_MODEL).sum(axis=0)
        part = sq[:, 0:128]
        for b in range(1, D_MODEL // 128):
            part = part + sq[:, 128 * b:128 * (b + 1)]
        acc_ref[...] += part

    row = lambda i: (i, 0)
    return pl.pallas_call(
        body, name="loss_grad", grid=(s // tm,),
        in_specs=[pl.BlockSpec((tm, D_MODEL), row)] * 2,
        out_specs=[pl.BlockSpec((tm, D_MODEL), row), pl.BlockSpec((SUBLANES, 128), lambda i: (0, 0))],
        out_shape=[jax.ShapeDtypeStruct((s, D_MODEL), F32), jax.ShapeDtypeStruct((SUBLANES, 128), F32)],
        compiler_params=_cparams(1),
    )(y, target)


def _rms_bwd(x, r, g, dy):
    gdy = g * dy
    dx = r * gdy - x * (r * r * r) * jnp.mean(x * gdy, axis=-1, keepdims=True)
    return dx, x * r * dy


def _colsum8(a):
    t = a.shape[0]
    return a.reshape(t // SUBLANES, SUBLANES, a.shape[1]).sum(axis=0)


def _out_ple_bwd(dh2, h1, p, g2, w_out_all, w_pg_all, w_pp_all, layer):
    s = h1.shape[0]
    tm = _row_tile(s, 256)

    def body(dh2_ref, h1_ref, p_ref, g_ref, wo_ref, wpg_ref, wpp_ref,
             dh1_ref, dmix_ref, hn_ref, dgp_ref, dpp_ref, dh1b_ref, dg_ref):
        @pl.when(pl.program_id(0) == 0)
        def _():
            dg_ref[...] = jnp.zeros_like(dg_ref)

        h1 = h1_ref[...]
        dh2 = dh2_ref[...]
        g2v = g_ref[...]
        r2, hb, gate, pp = _ple_forward(h1, p_ref[...], g2v, wpg_ref, wpp_ref)
        dgp = (dh2 * pp) * gate * (1.0 - gate)
        dgpb = dgp.astype(BF16)
        dhn = jnp.concatenate([_dot_nt(dgpb, wpg_ref[sh]) for sh in range(N_CHIPS)], axis=-1)
        dx, dgrow = _rms_bwd(h1, r2, g2v, dhn)
        dh1 = dh2 + dx
        dh1b = dh1.astype(BF16)
        dh1_ref[...] = dh1
        dh1b_ref[...] = dh1b
        hn_ref[...] = hb
        dgp_ref[...] = dgpb
        dpp_ref[...] = (dh2 * gate).astype(BF16)
        dg_ref[...] += _colsum8(dgrow)
        for sh in range(N_CHIPS):
            dmix_ref[:, ROW_SHARD * sh:ROW_SHARD * (sh + 1)] = _dot_nt(dh1b, wo_ref[sh])

    row = lambda i: (i, 0)
    wspec = lambda r, cdim: pl.BlockSpec((N_CHIPS, None, r, cdim), lambda i: (0, layer, 0, 0))
    big = pl.BlockSpec((tm, D_MODEL), row)
    return pl.pallas_call(
        body, name="out_ple_bwd", grid=(s // tm,),
        in_specs=[big, big, pl.BlockSpec((tm, PLE_DIM), row), pl.BlockSpec((1, D_MODEL), lambda i: (0, 0)),
                  wspec(ROW_SHARD, D_MODEL), wspec(ROW_SHARD, D_MODEL), wspec(PLE_DIM, ROW_SHARD)],
        out_specs=[big] * 6 + [pl.BlockSpec((SUBLANES, D_MODEL), lambda i: (0, 0))],
        out_shape=[jax.ShapeDtypeStruct((s, D_MODEL), F32)] * 2 + [jax.ShapeDtypeStruct((s, D_MODEL), BF16)] * 4
        + [jax.ShapeDtypeStruct((SUBLANES, D_MODEL), F32)],
        compiler_params=_cparams(1),
    )(dh2, h1, p, g2, w_out_all, w_pg_all, w_pp_all)


def _tn_matmul(a, b, n_blocks, block_a, name, layer, into=None, first_block=0, total_blocks=None):
    s = a.shape[0]
    tk = _row_tile(s, 512)
    total_blocks = n_blocks if total_blocks is None else total_blocks
    ka, nb = a.shape[1], b.shape[1]
    if block_a:
        ka //= n_blocks
    else:
        nb //= n_blocks

    def body(*refs):
        a_ref, b_ref, o_ref = refs[0], refs[1], refs[-1]

        @pl.when(pl.program_id(0) == 0)
        def _():
            o_ref[...] = jnp.zeros_like(o_ref)

        at = a_ref[...].astype(BF16).T
        bb = b_ref[...].astype(BF16)
        for sh in range(n_blocks):
            if block_a:
                o_ref[sh] += _dot(at[ka * sh:ka * (sh + 1), :], bb)
            else:
                o_ref[sh] += _dot(at, bb[:, nb * sh:nb * (sh + 1)])

    in_specs = [pl.BlockSpec((tk, a.shape[1]), lambda i: (i, 0)), pl.BlockSpec((tk, b.shape[1]), lambda i: (i, 0))]
    operands = [a, b]
    aliases = {}
    if into is not None:
        in_specs.append(ANY)
        operands.append(into)
        aliases = {2: 0}
    return pl.pallas_call(
        body, name=name, grid=(s // tk,),
        in_specs=in_specs,
        out_specs=pl.BlockSpec((None, n_blocks, ka, nb), lambda i: (layer, first_block // n_blocks, 0, 0)),
        out_shape=jax.ShapeDtypeStruct((N_LAYERS, total_blocks, ka, nb), F32),
        input_output_aliases=aliases,
        compiler_params=_cparams(1),
    )(*operands)


def _attn_bwd(qkv, o, proj, dmix):
    s = qkv.shape[0]
    tb = ATTN_BLOCK
    nq = s // tb
    n_sub = min(ATTN_SUB_BWD, nq)
    tq = n_sub * tb
    n_chain = 2 * n_sub

    def body(q_ref, k_ref, v_ref, o_ref, g_ref, dya_ref, tri_s_ref, tri_p_ref,
             dq_ref, dk_ref, dv_ref, dg_ref, do_scr, l_scr, g_scr, s_scr, w_scr):
        i = pl.program_id(1)
        base = i * n_sub

        @pl.when(i == 0)
        def _():
            dk_ref[...] = jnp.zeros_like(dk_ref)
            dv_ref[...] = jnp.zeros_like(dv_ref)

        g = g_ref[...]
        sg = _sigmoid(g)
        dya = dya_ref[...]
        do_scr[...] = (dya * (g * sg)).astype(BF16)
        dg_ref[...] = dya * o_ref[...] * (sg * (1.0 + g * (1.0 - sg)))
        dq_ref[...] = jnp.zeros_like(dq_ref)
        g_scr[...] = jnp.zeros_like(g_scr)
        masks = _head_masks()

        def keep(t, a, h, c, r0, z, w):
            s_scr[c, t] = _sigmoid(z).astype(BF16)
            w_scr[c, t] = w.astype(BF16)

        steps = _chain_sweep(base, n_sub, q_ref, k_ref, tri_s_ref, l_scr, keep)
        row = lax.broadcasted_iota(jnp.int32, (tb, tb), 0)
        col = lax.broadcasted_iota(jnp.int32, (tb, tb), 1)

        def back(it, carry):
            t = steps - 1 - it
            r0s = [pl.multiple_of(jnp.maximum(base + a - t, 0) * tb, tb) for a in range(n_sub)]
            qhs, dohs, khs, gws = [], [], [], []
            for a in range(n_sub):
                kb = k_ref[pl.ds(r0s[a], tb), :]
                vb = v_ref[pl.ds(r0s[a], tb), :]
                qa = q_ref[a * tb:(a + 1) * tb, :]
                doa = do_scr[a * tb:(a + 1) * tb, :]
                for h, mask in enumerate(masks):
                    qhs.append(jnp.where(mask, qa, jnp.zeros_like(qa)))
                    khs.append(jnp.where(mask, kb, jnp.zeros_like(kb)))
                    dohs.append(jnp.where(mask, doa, jnp.zeros_like(doa)))
                    gws.append(w_scr[2 * a + h, t].astype(F32) * _dot_nt(dohs[-1], vb))
            parts = [_split_hilo(gw) for gw in gws]
            tri = tri_p_ref[...]
            sums = [_dot(hi, tri) + _dot(lo, tri) for hi, lo in parts]
            dzs = []
            for c, (gw, sm) in enumerate(zip(gws, sums)):
                gsum = g_scr[c]
                dz = gw - (gw + sm[:, 0:tb] + gsum) * s_scr[c, t].astype(F32)
                dz = jnp.where(col < row + t * tb, dz, 0.0)
                g_scr[c] = gsum + sm[:, tb:2 * tb]
                dzs.append(dz.astype(BF16))
            for c, dzb in enumerate(dzs):
                a = c // 2
                dk_ref[pl.ds(r0s[a], tb), :] += _dot_tn(dzb, qhs[c])
                dv_ref[pl.ds(r0s[a], tb), :] += _dot_tn(w_scr[c, t], dohs[c])
                dq_ref[a * tb:(a + 1) * tb, :] += _dot(dzb, khs[c])
            return carry

        lax.fori_loop(0, steps, back, 0)

    hp_blk = lambda off: pl.BlockSpec((tq, 2 * HEAD_DIM), lambda hp, i: (i, off + hp))
    res = lambda off: pl.BlockSpec((s, 2 * HEAD_DIM), lambda hp, i: (0, off + hp))
    tri = pl.BlockSpec((tb, 2 * tb), lambda hp, i: (0, 0))
    return pl.pallas_call(
        body, name="attn_bwd", grid=(ATTN_WIDTH // (2 * HEAD_DIM), s // tq),
        in_specs=[hp_blk(0), res(4), res(8), hp_blk(0), hp_blk(20), hp_blk(4), tri, tri],
        out_specs=[hp_blk(0), res(0), res(0), hp_blk(0)],
        out_shape=[jax.ShapeDtypeStruct((s, ATTN_WIDTH), F32)] * 4,
        scratch_shapes=[pltpu.VMEM((tq, 2 * HEAD_DIM), BF16), pltpu.VMEM((n_chain, tb, tb), F32),
                        pltpu.VMEM((n_chain, tb, tb), F32), pltpu.VMEM((n_chain, nq, tb, tb), BF16),
                        pltpu.VMEM((n_chain, nq, tb, tb), BF16)],
        compiler_params=_cparams(2),
    )(qkv, qkv, qkv, o, proj, dmix, _tri("suffix_incl"), _tri("prefix_strict"))


def _ssm_glu_bwd(dmix, y, proj, d, w_glu_all, layer, b_glu):
    s = y.shape[0]
    tm = _row_tile(s, 512)

    def body(dys_ref, y_ref, u_ref, gs_ref, d_ref, wg_ref, bg_ref,
             dyf_ref, du_ref, dgs_ref, z_ref, dzz_ref, dd_ref, db_ref):
        @pl.when(pl.program_id(0) == 0)
        def _():
            dd_ref[...] = jnp.zeros_like(dd_ref)
            db_ref[...] = jnp.zeros_like(db_ref)

        u = u_ref[...]
        dv = d_ref[...]
        yf, z, val, gate = _glu_forward(y_ref[...], u, dv, wg_ref, bg_ref[...])
        gs = gs_ref[...]
        sgs = _sigmoid(gs)
        sgate = _sigmoid(gate)
        dys = dys_ref[...]
        dgv = dys * (gs * sgs)
        dgs_ref[...] = dys * (val * sgate) * (sgs * (1.0 + gs * (1.0 - sgs)))
        dzz = jnp.concatenate([dgv * sgate, dgv * val * sgate * (1.0 - sgate)], axis=-1)
        dzzb = dzz.astype(BF16)
        dz = _dot_nt(dzzb[:, 0:ROW_SHARD], wg_ref[0])
        for sh in range(1, N_CHIPS):
            dz = dz + _dot_nt(dzzb[:, ROW_SHARD * sh:ROW_SHARD * (sh + 1)], wg_ref[sh])
        dyf = dz * _gelu_grad(yf)
        dyf_ref[...] = dyf
        du_ref[...] = dyf * dv
        z_ref[...] = z.astype(BF16)
        dzz_ref[...] = dzzb
        dd_ref[...] += _colsum8(dyf * u)
        db_ref[...] += _colsum8(dzz)

    row = lambda i: (i, 0)
    half = pl.BlockSpec((tm, SSM_WIDTH), row)
    return pl.pallas_call(
        body, name="ssm_glu_bwd", grid=(s // tm,),
        in_specs=[half, half, half, pl.BlockSpec((tm, SSM_WIDTH), lambda i: (i, 1)),
                  pl.BlockSpec((1, SSM_WIDTH), lambda i: (0, 0)),
                  pl.BlockSpec((N_CHIPS, None, SSM_WIDTH, ROW_SHARD), lambda i: (0, layer, 0, 0)),
                  pl.BlockSpec((1, 2 * SSM_WIDTH), lambda i: (0, 0))],
        out_specs=[half, half, half, half, pl.BlockSpec((tm, 2 * SSM_WIDTH), row),
                   pl.BlockSpec((SUBLANES, SSM_WIDTH), lambda i: (0, 0)),
                   pl.BlockSpec((SUBLANES, 2 * SSM_WIDTH), lambda i: (0, 0))],
        out_shape=[jax.ShapeDtypeStruct((s, SSM_WIDTH), F32)] * 3
        + [jax.ShapeDtypeStruct((s, SSM_WIDTH), BF16), jax.ShapeDtypeStruct((s, 2 * SSM_WIDTH), BF16),
           jax.ShapeDtypeStruct((SUBLANES, SSM_WIDTH), F32), jax.ShapeDtypeStruct((SUBLANES, 2 * SSM_WIDTH), F32)],
        compiler_params=_cparams(1),
    )(dmix, y, proj, proj, d, w_glu_all, b_glu)


def _ssm_scan_bwd(dyf, xs, proj, wct, coef_rev, wbt):
    s = dyf.shape[0]
    tm = _row_tile(s, 512)
    nt = s // tm

    def body(dy_ref, xs_ref, u_ref, wct_ref, coef_ref, wbt_ref, du_ref, dwc_ref, dwb_ref, da_ref, lam_ref, carry_ref):
        @pl.when(pl.program_id(1) == 0)
        def _():
            carry_ref[...] = jnp.zeros_like(carry_ref)
            dwc_ref[...] = jnp.zeros_like(dwc_ref)
            dwb_ref[...] = jnp.zeros_like(dwb_ref)
            da_ref[...] = jnp.zeros_like(da_ref)

        dyb = dy_ref[...].astype(BF16)
        lam_ref[...] = _dot(dyb, wct_ref[...])
        rows = lax.broadcasted_iota(jnp.int32, (SUBLANES, CH_S), 0)
        last = rows == SUBLANES - 1

        def extra(r0, lr, li, cr, ci):
            er = jnp.where(last, cr, pltpu.roll(lr, SUBLANES - 1, 0))
            ei = jnp.where(last, ci, pltpu.roll(li, SUBLANES - 1, 0))
            xr = xs_ref[pl.ds(r0, SUBLANES), 0:CH_S]
            xi = xs_ref[pl.ds(r0, SUBLANES), CH_S:2 * CH_S]
            da_ref[:, 0:CH_S] += xr * er + xi * ei
            da_ref[:, CH_S:2 * CH_S] += xr * ei - xi * er

        _scan_rows(lam_ref, coef_ref, carry_ref, tm // SUBLANES, reverse=True, extra=extra)
        lamb = lam_ref[...].astype(BF16)
        du_ref[...] = _dot(lamb, wbt_ref[...])
        dwc_ref[...] += _dot_tn(xs_ref[...].astype(BF16), dyb)
        dwb_ref[...] += _dot_tn(u_ref[...].astype(BF16), lamb)

    rev = lambda j, i: (nt - 1 - i, j)
    return pl.pallas_call(
        body, name="ssm_scan_bwd", grid=(SSM_CHUNKS, nt),
        in_specs=[pl.BlockSpec((tm, CH_W), rev),
                  pl.BlockSpec((None, tm, 2 * CH_S), lambda j, i: (j, nt - 1 - i, 0)),
                  pl.BlockSpec((tm, CH_W), rev),
                  pl.BlockSpec((None, CH_W, 2 * CH_S), lambda j, i: (j, 0, 0)),
                  pl.BlockSpec((None, 8, SUBLANES, CH_S), lambda j, i: (j, 0, 0, 0)),
                  pl.BlockSpec((None, 2 * CH_S, CH_W), lambda j, i: (j, 0, 0))],
        out_specs=[pl.BlockSpec((tm, CH_W), rev),
                   pl.BlockSpec((None, 2 * CH_S, CH_W), lambda j, i: (j, 0, 0)),
                   pl.BlockSpec((None, CH_W, 2 * CH_S), lambda j, i: (j, 0, 0)),
                   pl.BlockSpec((None, SUBLANES, 2 * CH_S), lambda j, i: (j, 0, 0))],
        out_shape=[jax.ShapeDtypeStruct((s, SSM_WIDTH), F32),
                   jax.ShapeDtypeStruct((SSM_CHUNKS, 2 * CH_S, CH_W), F32),
                   jax.ShapeDtypeStruct((SSM_CHUNKS, CH_W, 2 * CH_S), F32),
                   jax.ShapeDtypeStruct((SSM_CHUNKS, SUBLANES, 2 * CH_S), F32)],
        scratch_shapes=[pltpu.VMEM((tm, 2 * CH_S), F32), pltpu.VMEM((SUBLANES, 2 * CH_S), F32)],
        compiler_params=_cparams(2),
    )(dyf, xs, proj, wct, coef_rev, wbt)


def _in_proj_bwd(h, g1, w_in_l, qg, kg, proj, du_a, du_b, dgs, dq, dk, dv, dga, dh1):
    s = h.shape[0]
    tm = _row_tile(s, 256)

    def body(h_ref, g_ref, w_ref, qg_ref, kg_ref, ones_ref, q_ref, k_ref, dua_ref, dub_ref, dgs_ref, dq_ref, dk_ref,
             dv_ref, dga_ref, dh1_ref, dh_ref, hn_ref, dp_ref, dg1_ref, dqg_ref, dkg_ref):
        @pl.when(pl.program_id(0) == 0)
        def _():
            dg1_ref[...] = jnp.zeros_like(dg1_ref)
            dqg_ref[...] = jnp.zeros_like(dqg_ref)
            dkg_ref[...] = jnp.zeros_like(dkg_ref)

        ones = ones_ref[...]

        def head_norm_bwd(x, gain, dy):
            r = lax.rsqrt(_dot_hilo(x * x, ones) + RMS_EPS)
            gdy = gain * dy
            dx = r * gdy - x * (r * r * r) * _dot_hilo(x * gdy, ones)
            return dx, x * r * dy

        dqr, dqg_rows = head_norm_bwd(q_ref[...], qg_ref[...], dq_ref[...] * ATTN_SCALE)
        dkr, dkg_rows = head_norm_bwd(k_ref[...], kg_ref[...], dk_ref[...])
        dqg_ref[...] += _colsum8(dqg_rows)
        dkg_ref[...] += _colsum8(dkg_rows)
        dp_ref[:, 0:512] = (dua_ref[...] + dub_ref[...]).astype(BF16)
        dp_ref[:, 512:1024] = dgs_ref[...].astype(BF16)
        dp_ref[:, 1024:1536] = dqr.astype(BF16)
        dp_ref[:, 1536:2048] = dkr.astype(BF16)
        dp_ref[:, 2048:2560] = dv_ref[...].astype(BF16)
        dp_ref[:, 2560:3072] = dga_ref[...].astype(BF16)
        dhn = _dot_nt(dp_ref[:, 0:IN_SHARD], w_ref[0])
        for sh in range(1, N_CHIPS):
            dhn = dhn + _dot_nt(dp_ref[:, IN_SHARD * sh:IN_SHARD * (sh + 1)], w_ref[sh])
        x = h_ref[...]
        gv = g_ref[...]
        r, hn = _rms_rows(x, gv)
        dx, dg_rows = _rms_bwd(x, r, gv, dhn)
        dh_ref[...] = dh1_ref[...] + dx
        hn_ref[...] = hn.astype(BF16)
        dg1_ref[...] += _colsum8(dg_rows)

    row = lambda i: (i, 0)
    full = lambda shape: pl.BlockSpec(shape, lambda i: (0,) * len(shape))
    big = pl.BlockSpec((tm, D_MODEL), row)
    half = pl.BlockSpec((tm, 512), row)
    return pl.pallas_call(
        body, name="in_proj_bwd", grid=(s // tm,),
        in_specs=[big, full((1, D_MODEL)), full((N_CHIPS, D_MODEL, IN_SHARD)),
                  full((1, ATTN_WIDTH)), full((1, ATTN_WIDTH)), full((ATTN_WIDTH, ATTN_WIDTH)),
                  pl.BlockSpec((tm, 512), lambda i: (i, 2)), pl.BlockSpec((tm, 512), lambda i: (i, 3)),
                  half, half, half, half, half, half, half, big],
        out_specs=[big, big, pl.BlockSpec((tm, IN_COLS), row), pl.BlockSpec((SUBLANES, D_MODEL), lambda i: (0, 0)),
                   pl.BlockSpec((SUBLANES, ATTN_WIDTH), lambda i: (0, 0)), pl.BlockSpec((SUBLANES, ATTN_WIDTH), lambda i: (0, 0))],
        out_shape=[jax.ShapeDtypeStruct((s, D_MODEL), F32), jax.ShapeDtypeStruct((s, D_MODEL), BF16),
                   jax.ShapeDtypeStruct((s, IN_COLS), BF16), jax.ShapeDtypeStruct((SUBLANES, D_MODEL), F32),
                   jax.ShapeDtypeStruct((SUBLANES, ATTN_WIDTH), F32), jax.ShapeDtypeStruct((SUBLANES, ATTN_WIDTH), F32)],
        compiler_params=_cparams(1),
    )(h, g1, w_in_l, qg, kg, _head_ones(), proj, proj, du_a, du_b, dgs, dq, dk, dv, dga, dh1)


SMALL_NAMES = ("mix_norm_g", "ssm_a_re", "ssm_a_im", "ssm_log_dt", "ssm_b_re", "ssm_b_im", "ssm_c_re", "ssm_c_im",
               "ssm_d", "ssm_b_glu", "q_norm_g", "k_norm_g", "ple_norm_g")
SMALL_4D = ("ssm_b_re", "ssm_b_im", "ssm_c_re", "ssm_c_im")
BIG_NAMES = ("w_in", "ssm_w_glu", "w_out", "w_ple_gate", "w_ple_proj")


def _ssm_setup(sm, layer):
    col = lambda a: a[layer].reshape(1, N_STATES)
    a_re, a_im = col(sm["ssm_a_re"]), col(sm["ssm_a_im"])
    log_dt = jnp.repeat(sm["ssm_log_dt"][layer], SSM_STATE).reshape(1, N_STATES)
    b_re = sm["ssm_b_re"][layer].reshape(N_STATES, SSM_GROUP).T
    b_im = sm["ssm_b_im"][layer].reshape(N_STATES, SSM_GROUP).T
    disc_in = (a_re, a_im, log_dt, b_re, b_im)
    ab_re, ab_im, bb_re, bb_im = _disc_fwd(*disc_in)
    wb = jnp.concatenate([_block_diag_in(bb_re), _block_diag_in(bb_im)], axis=-1)
    wc = jnp.concatenate([_block_diag_out(sm["ssm_c_re"][layer]), -_block_diag_out(sm["ssm_c_im"][layer])], axis=1)
    return dict(disc_in=disc_in, wb=wb.astype(BF16), wbt=wb.transpose(0, 2, 1).astype(BF16),
                wc=wc.astype(BF16), wct=wc.transpose(0, 2, 1).astype(BF16),
                coef=_scan_coefs(ab_re, ab_im, False), coef_rev=_scan_coefs(ab_re, ab_im, True))


def _gathered_weights(w_in0, rest):
    wg = dict(zip(BIG_NAMES[1:], rest[1:]))
    wg["w_in"] = [w_in0, rest[0].reshape(N_CHIPS, D_MODEL, IN_SHARD)]
    return wg


def _local_step(x, p, target, sm, w_in0, rest_local=None, rest_bases=None, rest_gathered=None):
    wg = None if rest_gathered is None else _gathered_weights(w_in0, rest_gathered)
    tile8 = lambda a: jnp.tile(a, ATTN_WIDTH // HEAD_DIM).reshape(1, ATTN_WIDTH)
    saved = []
    h = x
    for l in range(N_LAYERS):
        ssm = _ssm_setup(sm, l)
        g1 = sm["mix_norm_g"][l].reshape(1, D_MODEL)
        g2 = sm["ple_norm_g"][l].reshape(1, D_MODEL)
        qg, kg = tile8(sm["q_norm_g"][l]), tile8(sm["k_norm_g"][l])
        dsk = sm["ssm_d"][l].reshape(1, SSM_WIDTH)
        bgl = sm["ssm_b_glu"][l].reshape(1, 2 * SSM_WIDTH)
        proj, qkv = _in_proj(h, g1, w_in0 if l == 0 else wg["w_in"][l], qg, kg)
        if wg is None:
            xs, y, rest = _ssm_scan_fwd(proj, ssm["wb"], ssm["coef"], ssm["wc"], rest_local, rest_bases)
            wg = _gathered_weights(w_in0, rest)
        else:
            xs, y, _ = _ssm_scan_fwd(proj, ssm["wb"], ssm["coef"], ssm["wc"])
        ys = _ssm_glu_fwd(y, proj, dsk, wg["ssm_w_glu"], l, bgl)
        o, ya = _attn_fwd(qkv, proj)
        h1, h2 = _out_ple(h, ys, ya, p[l], g2, wg["w_out"], wg["w_ple_gate"], wg["w_ple_proj"], l)
        saved.append(dict(ssm=ssm, g1=g1, g2=g2, qg=qg, kg=kg, dsk=dsk, bgl=bgl, h=h, proj=proj, qkv=qkv, xs=xs, y=y,
                          ys=ys, o=o, ya=ya, h1=h1))
        h = h2
    dh, sq = _loss_grad(h, target)
    loss = 0.5 * jnp.sum(sq) / D_MODEL

    gbig = {n: None for n in BIG_NAMES}
    gsm = {n: [None] * N_LAYERS for n in SMALL_NAMES}
    for l in reversed(range(N_LAYERS)):
        sv = saved[l]
        ssm = sv["ssm"]
        dh1, dmix, hn2b, dgpb, dppb, dh1b, dg2 = _out_ple_bwd(dh, sv["h1"], p[l], sv["g2"], wg["w_out"],
                                                              wg["w_ple_gate"], wg["w_ple_proj"], l)
        gsm["ple_norm_g"][l] = dg2.sum(0)
        gbig["w_ple_proj"] = _tn_matmul(p[l], dppb, N_CHIPS, False, "dw_ple_proj", l, gbig["w_ple_proj"])
        gbig["w_ple_gate"] = _tn_matmul(hn2b, dgpb, N_CHIPS, True, "dw_ple_gate", l, gbig["w_ple_gate"])
        dwo = _tn_matmul(sv["ys"], dh1b, 2, True, "dw_out_ssm", l, gbig["w_out"], 0, N_CHIPS)
        gbig["w_out"] = _tn_matmul(sv["ya"], dh1b, 2, True, "dw_out_attn", l, dwo, 2, N_CHIPS)
        dqs, dkn, dv, dga = _attn_bwd(sv["qkv"], sv["o"], sv["proj"], dmix)
        dyf, du_a, dgs, zb, dzzb, dd, dbg = _ssm_glu_bwd(dmix, sv["y"], sv["proj"], sv["dsk"], wg["ssm_w_glu"], l, sv["bgl"])
        gsm["ssm_d"][l] = dd.sum(0).reshape(SSM_GROUPS, SSM_GROUP)
        gsm["ssm_b_glu"][l] = dbg.sum(0)
        gbig["ssm_w_glu"] = _tn_matmul(zb, dzzb, N_CHIPS, False, "dw_glu", l, gbig["ssm_w_glu"])
        du_b, dwc, dwb, da = _ssm_scan_bwd(dyf, sv["xs"], sv["proj"], ssm["wct"], ssm["coef_rev"], ssm["wbt"])
        gsm["ssm_c_re"][l] = _block_diag_out_t(dwc[:, 0:CH_S, :])
        gsm["ssm_c_im"][l] = -_block_diag_out_t(dwc[:, CH_S:, :])
        da = da.sum(1)
        g_ab_re = da[:, 0:CH_S].reshape(1, N_STATES)
        g_ab_im = da[:, CH_S:].reshape(1, N_STATES)
        g_bb_re = _block_diag_in_t(dwb[:, :, 0:CH_S])
        g_bb_im = _block_diag_in_t(dwb[:, :, CH_S:])
        d_are, d_aim, d_ldt, d_bre, d_bim = _disc_bwd(*ssm["disc_in"], g_ab_re, g_ab_im, g_bb_re, g_bb_im)
        gsm["ssm_a_re"][l] = d_are.reshape(SSM_GROUPS, SSM_STATE)
        gsm["ssm_a_im"][l] = d_aim.reshape(SSM_GROUPS, SSM_STATE)
        gsm["ssm_log_dt"][l] = d_ldt.reshape(SSM_GROUPS, SSM_STATE).sum(1)
        gsm["ssm_b_re"][l] = d_bre.T.reshape(SSM_GROUPS, SSM_STATE, SSM_GROUP)
        gsm["ssm_b_im"][l] = d_bim.T.reshape(SSM_GROUPS, SSM_STATE, SSM_GROUP)
        dh, hnb, dprojb, dg1, dqg, dkg = _in_proj_bwd(sv["h"], sv["g1"], wg["w_in"][l], sv["qg"], sv["kg"], sv["proj"],
                                                      du_a, du_b, dgs, dqs, dkn, dv, dga, dh1)
        gsm["mix_norm_g"][l] = dg1.sum(0)
        gsm["q_norm_g"][l] = dqg.sum(0).reshape(-1, HEAD_DIM).sum(0)
        gsm["k_norm_g"][l] = dkg.sum(0).reshape(-1, HEAD_DIM).sum(0)
        gbig["w_in"] = _tn_matmul(hnb, dprojb, N_CHIPS, False, "dw_in", l, gbig["w_in"])
    gsm = {n: jnp.stack(v, 0) for n, v in gsm.items()}
    return loss, dh, gbig, gsm


_SMALL_PAD = 8 * 8 * 128


def _pack_small(d):
    flat = jnp.concatenate([d[n].reshape(-1) for n in SMALL_NAMES])
    n = flat.shape[0]
    padded = -(-n // _SMALL_PAD) * _SMALL_PAD
    return jnp.pad(flat, (0, padded - n))


def _unpack_small(flat, like):
    out, off = {}, 0
    for n in SMALL_NAMES:
        size = like[n].size
        out[n] = flat[off:off + size].reshape(like[n].shape)
        off += size
    return out


def _reduce_grads(parts, wire_dtypes):
    n = len(parts)
    flat = [a.reshape(2, -1, a.shape[-1]) for a in parts]
    recv = _sibling_send_other_half(flat, "grad_sibling_send")
    chip = [_add_half(flat[k], recv[k], wire_dtypes[k], "grad_sibling_add").reshape(parts[k].shape[1:])
            for k in range(n)]
    got = _chip_scatter(chip, "grad_chip_scatter")
    tot = [_sum4(got[k], "grad_chip_sum") for k in range(n)]
    return _sibling_join_halves(tot, "grad_sibling_join")


def kernel(x, p, mix_norm_g, w_in, ssm_a_re, ssm_a_im, ssm_log_dt, ssm_b_re, ssm_b_im, ssm_c_re, ssm_c_im, ssm_d, ssm_w_glu, ssm_b_glu, q_norm_g, k_norm_g, w_out, ple_norm_g, w_ple_gate, w_ple_proj, loss_target, m_mix_norm_g, m_w_in, m_ssm_a_re, m_ssm_a_im, m_ssm_log_dt, m_ssm_b_re, m_ssm_b_im, m_ssm_c_re, m_ssm_c_im, m_ssm_d, m_ssm_w_glu, m_ssm_b_glu, m_q_norm_g, m_k_norm_g, m_w_out, m_ple_norm_g, m_w_ple_gate, m_w_ple_proj, v_mix_norm_g, v_w_in, v_ssm_a_re, v_ssm_a_im, v_ssm_log_dt, v_ssm_b_re, v_ssm_b_im, v_ssm_c_re, v_ssm_c_im, v_ssm_d, v_ssm_w_glu, v_ssm_b_glu, v_q_norm_g, v_k_norm_g, v_w_out, v_ple_norm_g, v_w_ple_gate, v_w_ple_proj):
    args = dict(locals())
    names = ("mix_norm_g", "w_in", "ssm_a_re", "ssm_a_im", "ssm_log_dt", "ssm_b_re", "ssm_b_im", "ssm_c_re", "ssm_c_im",
             "ssm_d", "ssm_w_glu", "ssm_b_glu", "q_norm_g", "k_norm_g", "w_out", "ple_norm_g", "w_ple_gate", "w_ple_proj")
    w = {n: args[n] for n in names}
    m = {n: args["m_" + n] for n in names}
    v = {n: args["v_" + n] for n in names}

    w_in_halves = w["w_in"].astype(BF16).reshape(2 * N_LAYERS, D_MODEL // 2, IN_SHARD)
    w_in0 = _chip_gather([w_in_halves], "w_in_gather")[0].reshape(N_CHIPS, D_MODEL, IN_SHARD)
    rest_local = [w_in_halves] + [w[n].astype(BF16) for n in BIG_NAMES[1:]]
    sm = {n: w[n] for n in SMALL_NAMES}
    loss, dx, gbig, gsm = _local_step(x[0], p[:, 0], loss_target[0], sm, w_in0, rest_local, [2, 0, 0, 0, 0])
    loss = lax.psum(loss, ("x", "y", "c"))

    small = _pack_small(gsm)
    parts = [gbig[n] for n in BIG_NAMES] + [small.reshape(2, N_CHIPS, SUBLANES, -1)]
    red = _reduce_grads(parts, [BF16] * len(BIG_NAMES) + [F32])
    small_mine = red[-1]
    small_all = _chip_gather([small_mine], "small_grad_gather")[0]
    small_tot = small_all.transpose(1, 0, 2, 3).reshape(-1)
    g = dict(zip(BIG_NAMES, [r.reshape(w[n].shape) for r, n in zip(red[:-1], BIG_NAMES)]))
    g.update(_unpack_small(small_tot, sm))

    delta, new_m, new_v = {}, {}, {}
    for n in BIG_NAMES:
        lanes = w[n].shape[-1]
        outs = _adamw(_as_rows(w[n], lanes), _as_rows(g[n], lanes), _as_rows(m[n], lanes), _as_rows(v[n], lanes), "adamw_" + n)
        delta[n], new_m[n], new_v[n] = [o.reshape(w[n].shape) for o in outs]
    for group, per_layer in ((SMALL_4D, True), (tuple(n for n in SMALL_NAMES if n not in SMALL_4D), False)):
        outs = _adamw_many(*[[d[n] for n in group] for d in (w, g, m, v)], "adamw_small_4d" if per_layer else "adamw_small", per_layer)
        for d, o in zip((delta, new_m, new_v), outs):
            d.update(zip(group, o))

    return (loss, dx[None], *[g[n] for n in names], *[delta[n] for n in names],
            *[new_m[n] for n in names], *[new_v[n] for n in names])
```

```python
import functools
import math

import jax
import jax.numpy as jnp
from jax import lax
from jax.experimental import pallas as pl
from jax.experimental.pallas import tpu as pltpu

F32 = jnp.float32
BF16 = jnp.bfloat16

D_MODEL = 1024
N_LAYERS = 2
N_CHIPS = 4
IN_COLS = 3072
IN_SHARD = IN_COLS // N_CHIPS
SSM_WIDTH = 512
SSM_GROUP = 16
SSM_GROUPS = 32
SSM_STATE = 64
N_STATES = SSM_GROUPS * SSM_STATE
SSM_CHUNKS = 4
CH_W = SSM_WIDTH // SSM_CHUNKS
CH_S = N_STATES // SSM_CHUNKS
ATTN_WIDTH = 512
HEAD_DIM = 64
PLE_DIM = 256
ROW_SHARD = 256
RMS_EPS = 1e-6
ATTN_SCALE = HEAD_DIM ** -0.5
ATTN_BLOCK = 128
EXP_ZERO = -87.5
SUBLANES = 8
V7X_VMEM_LIMIT = 52 * 1024 * 1024

ADAM_LR = 0.001
ADAM_B1 = 0.9
ADAM_B2 = 0.999
ADAM_EPS = 1e-08
ADAM_WD = 0.01
ADAM_STEP = 10

MESH = pl.DeviceIdType.MESH
ANY = pl.BlockSpec(memory_space=pl.ANY)


def _cparams(n_grid=0, parallel=0):
    sem = tuple(["parallel"] * parallel + ["arbitrary"] * (n_grid - parallel))
    return pltpu.CompilerParams(dimension_semantics=sem, vmem_limit_bytes=V7X_VMEM_LIMIT)


def _dot(a, b):
    return jnp.dot(a, b, preferred_element_type=F32)


def _dot_nt(a, b):
    return lax.dot_general(a, b, (((1,), (1,)), ((), ())), preferred_element_type=F32)


def _dot_tn(a, b):
    return lax.dot_general(a, b, (((0,), (0,)), ((), ())), preferred_element_type=F32)


def _split_hilo(a):
    hi = a.astype(BF16)
    lo = (a - hi.astype(F32)).astype(BF16)
    return hi, lo


def _dot_hilo(a, b):
    hi, lo = _split_hilo(a)
    return _dot(hi, b) + _dot(lo, b)


def _sigmoid(x):
    return 0.5 * (jnp.tanh(0.5 * x) + 1.0)


_GELU_C = math.sqrt(2.0 / math.pi)


def _gelu(x):
    return 0.5 * x * (1.0 + jnp.tanh(_GELU_C * (x + 0.044715 * (x * x * x))))


def _gelu_grad(x):
    t = jnp.tanh(_GELU_C * (x + 0.044715 * (x * x * x)))
    return 0.5 * (1.0 + t) + 0.5 * x * (1.0 - t * t) * (_GELU_C * (1.0 + 3.0 * 0.044715 * (x * x)))


def _row_tile(s, want):
    for t in range(min(s, want), 7, -1):
        if s % t == 0 and t % SUBLANES == 0:
            return t
    return s


def _coords():
    return lax.axis_index("x"), lax.axis_index("y"), lax.axis_index("c")


def _other_chips(x, y):
    return [(1 - x, y), (x, 1 - y), (1 - x, 1 - y)]


def _remote(src, dst, send_sem, recv_sem, dev):
    return pltpu.make_async_remote_copy(src_ref=src, dst_ref=dst, send_sem=send_sem, recv_sem=recv_sem,
                                        device_id=dev, device_id_type=MESH)


def _set_block(buf, block, index):
    return lax.dynamic_update_index_in_dim(buf, block, index, 0)


def _gather_sems(n):
    return [pltpu.SemaphoreType.DMA((3 * n,)) for _ in range(4)]


def _gather_copies(ins, bases, outs, sems):
    send_sems, recv_sems, fwd_send, fwd_recv = sems
    x, y, c = _coords()
    me_chip = 2 * x + y
    sibling = (x, y, 1 - c)
    first, landed, passed, from_sibling = [], [], [], []
    for k in range(len(ins)):
        for j, (cx, cy) in enumerate(_other_chips(x, y)):
            i = 3 * k + j
            first.append(_remote(ins[k].at[bases[k] + c], outs[k].at[me_chip, c], send_sems.at[i], recv_sems.at[i], (cx, cy, c)))
            blk = outs[k].at[2 * cx + cy, c]
            landed.append(_remote(blk, blk, send_sems.at[i], recv_sems.at[i], (cx, cy, c)))
            passed.append(_remote(blk, blk, fwd_send.at[i], fwd_recv.at[i], sibling))
            blk = outs[k].at[2 * cx + cy, 1 - c]
            from_sibling.append(_remote(blk, blk, fwd_send.at[i], fwd_recv.at[i], sibling))
    return first, landed, passed, from_sibling


def _gather_start(ins, bases, outs, sems):
    for cp in _gather_copies(ins, bases, outs, sems)[0]:
        cp.start()


def _gather_finish(ins, bases, outs, sems):
    first, landed, passed, from_sibling = _gather_copies(ins, bases, outs, sems)
    for arrived, forward in zip(landed, passed):
        arrived.wait_recv()
        forward.start()
    for cp in from_sibling:
        cp.wait_recv()
    for cp in first + passed:
        cp.wait_send()


def _gather_outputs(arrs):
    return [jax.ShapeDtypeStruct((N_CHIPS, 2) + a.shape[1:], a.dtype) for a in arrs]


def _gather_own(outs, arrs, bases):
    me_chip = 2 * lax.axis_index("x") + lax.axis_index("y")
    return [_set_block(o, lax.slice_in_dim(a, b, b + 2, axis=0), me_chip) for o, a, b in zip(outs, arrs, bases)]


def _chip_gather(arrs, name, bases=None):
    n = len(arrs)
    bases = [0] * n if bases is None else bases

    def body(*refs):
        ins, outs, sems = refs[:n], refs[n:2 * n], refs[2 * n:]
        _gather_start(ins, bases, outs, sems)
        _gather_finish(ins, bases, outs, sems)

    outs = pl.pallas_call(
        body, name=name, out_shape=_gather_outputs(arrs),
        in_specs=[ANY] * n, out_specs=[ANY] * n, scratch_shapes=_gather_sems(n),
    )(*arrs)
    return _gather_own(outs, arrs, bases)


def _sibling_push(arrs, owners, name):
    n = len(arrs)

    def body(*refs):
        ins, outs = refs[:n], refs[n:2 * n]
        send_sems, recv_sems = refs[2 * n:]
        x, y, c = _coords()
        cps = [_remote(ins[k].at[1 - c] if owners[k] is None else ins[k], outs[k], send_sems.at[k], recv_sems.at[k],
                       (x, y, 1 - c)) for k in range(n)]
        for o in (None, 0, 1):
            mine = [cp for cp, ow in zip(cps, owners) if ow == o]
            if not mine:
                continue
            if o is None:
                for cp in mine:
                    cp.start()
                for cp in mine:
                    cp.wait_recv()
                for cp in mine:
                    cp.wait_send()
            else:
                @pl.when(c == 1 - o)
                def _():
                    for cp in mine:
                        cp.start()
                    for cp in mine:
                        cp.wait_send()

                @pl.when(c == o)
                def _():
                    for cp in mine:
                        cp.wait_recv()

    return pl.pallas_call(
        body, name=name,
        out_shape=[jax.ShapeDtypeStruct(a.shape[1:] if ow is None else a.shape, a.dtype) for a, ow in zip(arrs, owners)],
        in_specs=[ANY] * n, out_specs=[ANY] * n,
        scratch_shapes=[pltpu.SemaphoreType.DMA((n,)), pltpu.SemaphoreType.DMA((n,))],
    )(*arrs)


def _sibling_join(tot0, tot1, sym, name):
    nb, ns = len(tot0), len(sym)
    n = nb + ns

    def body(*refs):
        ins0, ins1, ins_s = refs[:nb], refs[nb:2 * nb], refs[2 * nb:2 * nb + ns]
        outs_b, outs_s = refs[2 * nb + ns:3 * nb + ns], refs[3 * nb + ns:3 * nb + 2 * ns]
        send_sems, recv_sems = refs[3 * nb + 2 * ns:]
        x, y, c = _coords()
        sibling = (x, y, 1 - c)

        def big(src):
            return [_remote(src[k], outs_b[k], send_sems.at[k], recv_sems.at[k], sibling) for k in range(nb)]

        @pl.when(c == 0)
        def _():
            for cp in big(ins0):
                cp.start()

        @pl.when(c == 1)
        def _():
            for cp in big(ins1):
                cp.start()

        halves = [_remote(ins_s[k], outs_s[k].at[c], send_sems.at[nb + k], recv_sems.at[nb + k], sibling) for k in range(ns)]
        for cp in halves:
            cp.start()
        for cp in big(ins0):
            cp.wait_recv()
        for k in range(ns):
            blk = outs_s[k].at[1 - c]
            _remote(blk, blk, send_sems.at[nb + k], recv_sems.at[nb + k], sibling).wait_recv()
        for cp in big(ins0) + halves:
            cp.wait_send()

    outs = pl.pallas_call(
        body, name=name,
        out_shape=[jax.ShapeDtypeStruct(a.shape, a.dtype) for a in tot0]
        + [jax.ShapeDtypeStruct((2,) + a.shape, a.dtype) for a in sym],
        in_specs=[ANY] * (2 * nb + ns), out_specs=[ANY] * n,
        scratch_shapes=[pltpu.SemaphoreType.DMA((n,)), pltpu.SemaphoreType.DMA((n,))],
    )(*tot0, *tot1, *sym)
    c = lax.axis_index("c")
    return outs[:nb], [_set_block(o, a, c) for o, a in zip(outs[nb:], sym)]


def _scatter_sems(n):
    return [pltpu.SemaphoreType.DMA((3 * n,)), pltpu.SemaphoreType.DMA((3 * n,))]


def _scatter_copies(ins, outs, sems):
    send_sems, recv_sems = sems
    x, y, c = _coords()
    me_chip = 2 * x + y
    sends, arrivals = [], []
    for k in range(len(ins)):
        for j, (cx, cy) in enumerate(_other_chips(x, y)):
            i = 3 * k + j
            sends.append(_remote(ins[k].at[2 * cx + cy], outs[k].at[me_chip], send_sems.at[i], recv_sems.at[i], (cx, cy, c)))
            blk = outs[k].at[2 * cx + cy]
            arrivals.append(_remote(blk, blk, send_sems.at[i], recv_sems.at[i], (cx, cy, c)))
    return sends, arrivals


def _by_owner(owners, fn):
    c = lax.axis_index("c")
    for o in (None, 0, 1):
        idx = [k for k, ow in enumerate(owners) if ow == o]
        if not idx:
            continue
        if o is None:
            fn(idx)
        else:
            pl.when(c == o)(functools.partial(fn, idx))


def _scatter_start(ins, outs, owners, sems):
    sends, _ = _scatter_copies(ins, outs, sems)

    def go(idx):
        for k in idx:
            for cp in sends[3 * k:3 * k + 3]:
                cp.start()

    _by_owner(owners, go)


def _scatter_finish(ins, outs, owners, sems):
    sends, arrivals = _scatter_copies(ins, outs, sems)

    def go(idx):
        for k in idx:
            for cp in arrivals[3 * k:3 * k + 3]:
                cp.wait_recv()
        for k in idx:
            for cp in sends[3 * k:3 * k + 3]:
                cp.wait_send()

    _by_owner(owners, go)


def _scatter_own(outs, arrs):
    me_chip = 2 * lax.axis_index("x") + lax.axis_index("y")
    return [_set_block(o, lax.dynamic_index_in_dim(a, me_chip, 0, keepdims=False), me_chip) for o, a in zip(outs, arrs)]


def _chip_scatter(arrs, owners, name):
    n = len(arrs)

    def body(*refs):
        ins, outs, sems = refs[:n], refs[n:2 * n], refs[2 * n:]
        _scatter_start(ins, outs, owners, sems)
        _scatter_finish(ins, outs, owners, sems)

    outs = pl.pallas_call(
        body, name=name,
        out_shape=[jax.ShapeDtypeStruct(a.shape, a.dtype) for a in arrs],
        in_specs=[ANY] * n, out_specs=[ANY] * n, scratch_shapes=_scatter_sems(n),
    )(*arrs)
    return _scatter_own(outs, arrs)


def _as_rows(a, lanes):
    return a.reshape(-1, lanes)


def _add_half(full, recv, out_dtype, name):
    _, r, cdim = full.shape
    tr = _row_tile(r, 512)

    def body(c_ref, a_ref, b_ref, o_ref):
        o_ref[...] = (a_ref[...].astype(F32) + b_ref[...].astype(F32)).astype(out_dtype)

    c = lax.axis_index("c").astype(jnp.int32).reshape(1)
    return pl.pallas_call(
        body, name=name,
        grid_spec=pltpu.PrefetchScalarGridSpec(
            num_scalar_prefetch=1, grid=(r // tr,),
            in_specs=[pl.BlockSpec((None, tr, cdim), lambda i, c_ref: (c_ref[0], i, 0)),
                      pl.BlockSpec((tr, cdim), lambda i, c_ref: (i, 0))],
            out_specs=pl.BlockSpec((tr, cdim), lambda i, c_ref: (i, 0))),
        out_shape=jax.ShapeDtypeStruct((r, cdim), out_dtype),
        compiler_params=_cparams(1),
    )(c, full, recv)


def _add_pair(a, b, name):
    r, cdim = a.shape
    tr = _row_tile(r, 512)

    def body(a_ref, b_ref, o_ref):
        o_ref[...] = (a_ref[...].astype(F32) + b_ref[...].astype(F32)).astype(o_ref.dtype)

    spec = pl.BlockSpec((tr, cdim), lambda i: (i, 0))
    return pl.pallas_call(body, name=name, grid=(r // tr,), in_specs=[spec] * 2, out_specs=spec,
                          out_shape=jax.ShapeDtypeStruct((r, cdim), a.dtype), compiler_params=_cparams(1))(a, b)


def _sum4(parts, name):
    _, r, cdim = parts.shape
    tr = _row_tile(r, 512)

    def body(p_ref, o_ref):
        acc = p_ref[0].astype(F32) + p_ref[1].astype(F32)
        acc = acc + p_ref[2].astype(F32)
        o_ref[...] = acc + p_ref[3].astype(F32)

    return pl.pallas_call(
        body, name=name, grid=(r // tr,),
        in_specs=[pl.BlockSpec((N_CHIPS, tr, cdim), lambda i: (0, i, 0))],
        out_specs=pl.BlockSpec((tr, cdim), lambda i: (i, 0)),
        out_shape=jax.ShapeDtypeStruct((r, cdim), F32),
        compiler_params=_cparams(1),
    )(parts)


def _adamw_math(w, g, m, v):
    c1 = 1.0 - ADAM_B1 ** ADAM_STEP
    c2 = 1.0 - ADAM_B2 ** ADAM_STEP
    nm = ADAM_B1 * m + (1.0 - ADAM_B1) * g
    nv = ADAM_B2 * v + (1.0 - ADAM_B2) * (g * g)
    delta = -ADAM_LR * ((nm / c1) / (jnp.sqrt(nv / c2) + ADAM_EPS) + ADAM_WD * w)
    return delta, nm, nv


def _adamw(w, g, m, v, name):
    r, cdim = w.shape
    tr = _row_tile(r, 256)

    def body(w_ref, g_ref, m_ref, v_ref, d_ref, nm_ref, nv_ref):
        d_ref[...], nm_ref[...], nv_ref[...] = _adamw_math(w_ref[...], g_ref[...], m_ref[...], v_ref[...])

    spec = pl.BlockSpec((tr, cdim), lambda i: (i, 0))
    return pl.pallas_call(
        body, name=name, grid=(r // tr,),
        in_specs=[spec] * 4, out_specs=[spec] * 3,
        out_shape=[jax.ShapeDtypeStruct((r, cdim), F32)] * 3,
        compiler_params=_cparams(1),
    )(w, g, m, v)


def _adamw_many(ws, gs, ms, vs, name, per_layer):
    n = len(ws)

    def body(*refs):
        for k in range(n):
            w, g, m, v = (refs[j * n + k][...] for j in range(4))
            outs = _adamw_math(w, g, m, v)
            for j in range(3):
                refs[(4 + j) * n + k][...] = outs[j]

    shapes = [jax.ShapeDtypeStruct(w.shape, F32) for w in ws]
    if per_layer:
        specs = [pl.BlockSpec((None,) + w.shape[1:], lambda l, nd=w.ndim: (l,) + (0,) * (nd - 1)) for w in ws]
        call = pl.pallas_call(body, name=name, grid=(N_LAYERS,), in_specs=specs * 4, out_specs=specs * 3,
                              out_shape=shapes * 3, compiler_params=_cparams(1))
    else:
        call = pl.pallas_call(body, name=name, out_shape=shapes * 3, compiler_params=_cparams())
    outs = call(*ws, *gs, *ms, *vs)
    return outs[0:n], outs[n:2 * n], outs[2 * n:3 * n]


def _discretise(a_re, a_im, log_dt, b_re, b_im):
    dt = jnp.exp(log_dt)
    mag = jnp.exp(a_re * dt)
    ab_re = mag * jnp.cos(a_im * dt)
    ab_im = mag * jnp.sin(a_im * dt)
    num_re = ab_re - 1.0
    num_im = ab_im
    den = a_re * a_re + a_im * a_im
    f_re = (num_re * a_re + num_im * a_im) / den
    f_im = (num_im * a_re - num_re * a_im) / den
    bb_re = f_re * b_re - f_im * b_im
    bb_im = f_re * b_im + f_im * b_re
    return ab_re, ab_im, bb_re, bb_im


def _disc_shapes():
    col = jax.ShapeDtypeStruct((1, N_STATES), F32)
    mat = jax.ShapeDtypeStruct((SSM_GROUP, N_STATES), F32)
    return col, mat


def _disc_fwd(a_re, a_im, log_dt, b_re, b_im):
    col, mat = _disc_shapes()

    def body(ar, ai, ld, br, bi, o0, o1, o2, o3):
        outs = _discretise(ar[...], ai[...], ld[...], br[...], bi[...])
        for o, val in zip((o0, o1, o2, o3), outs):
            o[...] = val

    return pl.pallas_call(body, name="ssm_discretise", out_shape=[col, col, mat, mat],
                          compiler_params=_cparams())(a_re, a_im, log_dt, b_re, b_im)


def _disc_bwd(a_re, a_im, log_dt, b_re, b_im, g_ab_re, g_ab_im, g_bb_re, g_bb_im):
    col, mat = _disc_shapes()

    def body(ar, ai, ld, br, bi, g0, g1, g2, g3, o0, o1, o2, o3, o4):
        _, vjp = jax.vjp(_discretise, ar[...], ai[...], ld[...], br[...], bi[...])
        grads = vjp((g0[...], g1[...], g2[...], g3[...]))
        for o, val in zip((o0, o1, o2, o3, o4), grads):
            o[...] = val

    return pl.pallas_call(body, name="ssm_discretise_bwd", out_shape=[col, col, col, mat, mat],
                          compiler_params=_cparams())(a_re, a_im, log_dt, b_re, b_im, g_ab_re, g_ab_im, g_bb_re, g_bb_im)


def _cmul(ar, ai, br, bi):
    return ar * br - ai * bi, ar * bi + ai * br


def _scan_coefs(ab_re, ab_im, reverse):
    ar = ab_re.reshape(1, N_STATES)
    ai = ab_im.reshape(1, N_STATES)
    if reverse:
        ai = -ai
    a2 = _cmul(ar, ai, ar, ai)
    a4 = _cmul(*a2, *a2)
    rows = jnp.arange(SUBLANES)[:, None]
    out = []
    for (pr, pi), sh in (((ar, ai), 1), (a2, 2), (a4, 4)):
        keep = (rows <= SUBLANES - 1 - sh) if reverse else (rows >= sh)
        out += [jnp.where(keep, pr, 0.0), jnp.where(keep, pi, 0.0)]
    pows = [(ar, ai)]
    for _ in range(SUBLANES - 1):
        pows.append(_cmul(*pows[-1], ar, ai))
    order = pows[::-1] if reverse else pows
    out += [jnp.concatenate([p[0] for p in order], 0), jnp.concatenate([p[1] for p in order], 0)]
    t = jnp.stack(out, 0)
    return t.reshape(8, SUBLANES, SSM_CHUNKS, CH_S).transpose(2, 0, 1, 3)


def _block_diag_in(bb):
    t = bb.reshape(SSM_GROUP, SSM_CHUNKS, 8, SSM_STATE)
    eye = jnp.eye(8, dtype=bb.dtype)
    return jnp.einsum("hjgp,gk->jghkp", t, eye).reshape(SSM_CHUNKS, CH_W, CH_S)


def _block_diag_in_t(d):
    t = d.reshape(SSM_CHUNKS, 8, SSM_GROUP, 8, SSM_STATE)
    return jnp.einsum("jghgp->hjgp", t).reshape(SSM_GROUP, N_STATES)


def _block_diag_out(c):
    t = c.reshape(SSM_CHUNKS, 8, SSM_GROUP, SSM_STATE)
    eye = jnp.eye(8, dtype=c.dtype)
    return jnp.einsum("jghp,gk->jgpkh", t, eye).reshape(SSM_CHUNKS, CH_S, CH_W)


def _block_diag_out_t(d):
    t = d.reshape(SSM_CHUNKS, 8, SSM_STATE, 8, SSM_GROUP)
    return jnp.einsum("jgpgh->jghp", t).reshape(SSM_GROUPS, SSM_GROUP, SSM_STATE)


def _head_ones():
    r = jnp.arange(ATTN_WIDTH) // HEAD_DIM
    return jnp.where(r[:, None] == r[None, :], 1.0 / HEAD_DIM, 0.0).astype(BF16)


def _in_proj(h, g1, w_in_l, qg, kg):
    s = h.shape[0]
    tm = _row_tile(s, 256)

    def body(h_ref, g_ref, w_ref, qg_ref, kg_ref, ones_ref, proj_ref, qkv_ref):
        x = h_ref[...]
        r = lax.rsqrt(jnp.mean(x * x, axis=-1, keepdims=True) + RMS_EPS)
        hn = (x * r * g_ref[...]).astype(BF16)
        for sh in range(N_CHIPS):
            proj_ref[:, IN_SHARD * sh:IN_SHARD * (sh + 1)] = _dot(hn, w_ref[sh])
        ones = ones_ref[...]
        q = proj_ref[:, 1024:1536]
        k = proj_ref[:, 1536:2048]
        rq = lax.rsqrt(_dot_hilo(q * q, ones) + RMS_EPS)
        rk = lax.rsqrt(_dot_hilo(k * k, ones) + RMS_EPS)
        qkv_ref[:, 0:512] = (q * rq * qg_ref[...] * ATTN_SCALE).astype(BF16)
        qkv_ref[:, 512:1024] = (k * rk * kg_ref[...]).astype(BF16)
        qkv_ref[:, 1024:1536] = proj_ref[:, 2048:2560].astype(BF16)

    full = lambda shape: pl.BlockSpec(shape, lambda i: (0,) * len(shape))
    return pl.pallas_call(
        body, name="in_proj", grid=(s // tm,),
        in_specs=[pl.BlockSpec((tm, D_MODEL), lambda i: (i, 0)), full((1, D_MODEL)),
                  full((N_CHIPS, D_MODEL, IN_SHARD)),
                  full((1, ATTN_WIDTH)), full((1, ATTN_WIDTH)), full((ATTN_WIDTH, ATTN_WIDTH))],
        out_specs=[pl.BlockSpec((tm, IN_COLS), lambda i: (i, 0)), pl.BlockSpec((tm, 3 * ATTN_WIDTH), lambda i: (i, 0))],
        out_shape=[jax.ShapeDtypeStruct((s, IN_COLS), F32), jax.ShapeDtypeStruct((s, 3 * ATTN_WIDTH), BF16)],
        compiler_params=_cparams(1),
    )(h, g1, w_in_l, qg, kg, _head_ones())


def _scan_rows(x_ref, coef_ref, carry_ref, n_blocks, reverse, extra=None):
    c = [coef_ref[a] for a in range(8)]
    shifts = (7, 6, 4) if reverse else (1, 2, 4)
    edge = 0 if reverse else SUBLANES - 1

    def blk(b, carry):
        bb = (n_blocks - 1 - b) if reverse else b
        r0 = pl.multiple_of(bb * SUBLANES, SUBLANES)
        xr = x_ref[pl.ds(r0, SUBLANES), 0:CH_S]
        xi = x_ref[pl.ds(r0, SUBLANES), CH_S:2 * CH_S]
        for lvl, sh in enumerate(shifts):
            ar, ai = c[2 * lvl], c[2 * lvl + 1]
            sr = pltpu.roll(xr, sh, 0)
            si = pltpu.roll(xi, sh, 0)
            xr, xi = xr + (ar * sr - ai * si), xi + (ar * si + ai * sr)
        cr, ci = carry
        xr, xi = xr + (c[6] * cr - c[7] * ci), xi + (c[6] * ci + c[7] * cr)
        x_ref[pl.ds(r0, SUBLANES), 0:CH_S] = xr
        x_ref[pl.ds(r0, SUBLANES), CH_S:2 * CH_S] = xi
        if extra is not None:
            extra(r0, xr, xi, cr, ci)
        return (jnp.broadcast_to(xr[edge:edge + 1, :], (SUBLANES, CH_S)),
                jnp.broadcast_to(xi[edge:edge + 1, :], (SUBLANES, CH_S)))

    cr, ci = lax.fori_loop(0, n_blocks, blk, (carry_ref[:, 0:CH_S], carry_ref[:, CH_S:2 * CH_S]))
    carry_ref[:, 0:CH_S] = cr
    carry_ref[:, CH_S:2 * CH_S] = ci


def _ssm_scan_fwd(proj, wb, coef, wc, gather=None, gather_bases=None):
    s = proj.shape[0]
    tm = _row_tile(s, 512)
    nt = s // tm
    gather = [] if gather is None else gather
    ng = len(gather)

    def body(*refs):
        u_ref, wb_ref, coef_ref, wc_ref = refs[0:4]
        g_ins = refs[4:4 + ng]
        xs_ref, y_ref = refs[4 + ng:6 + ng]
        g_outs = refs[6 + ng:6 + 2 * ng]
        carry_ref = refs[6 + 2 * ng]
        sems = refs[7 + 2 * ng:]
        j, i = pl.program_id(0), pl.program_id(1)

        @pl.when(i == 0)
        def _():
            carry_ref[...] = jnp.zeros_like(carry_ref)

        if ng:
            @pl.when(jnp.logical_and(j == 0, i == 0))
            def _():
                _gather_start(g_ins, gather_bases, g_outs, sems)

        xs_ref[...] = _dot(u_ref[...].astype(BF16), wb_ref[...])
        _scan_rows(xs_ref, coef_ref, carry_ref, tm // SUBLANES, reverse=False)
        y_ref[...] = _dot(xs_ref[...].astype(BF16), wc_ref[...])

        if ng:
            @pl.when(jnp.logical_and(j == SSM_CHUNKS - 1, i == nt - 1))
            def _():
                _gather_finish(g_ins, gather_bases, g_outs, sems)

    outs = pl.pallas_call(
        body, name="ssm_scan_gather" if ng else "ssm_scan", grid=(SSM_CHUNKS, nt),
        in_specs=[pl.BlockSpec((tm, CH_W), lambda j, i: (i, j)),
                  pl.BlockSpec((None, CH_W, 2 * CH_S), lambda j, i: (j, 0, 0)),
                  pl.BlockSpec((None, 8, SUBLANES, CH_S), lambda j, i: (j, 0, 0, 0)),
                  pl.BlockSpec((None, 2 * CH_S, CH_W), lambda j, i: (j, 0, 0))] + [ANY] * ng,
        out_specs=[pl.BlockSpec((None, tm, 2 * CH_S), lambda j, i: (j, i, 0)),
                   pl.BlockSpec((tm, CH_W), lambda j, i: (i, j))] + [ANY] * ng,
        out_shape=[jax.ShapeDtypeStruct((SSM_CHUNKS, s, 2 * CH_S), F32), jax.ShapeDtypeStruct((s, SSM_WIDTH), F32)]
        + _gather_outputs(gather),
        scratch_shapes=[pltpu.VMEM((SUBLANES, 2 * CH_S), F32)] + (_gather_sems(ng) if ng else []),
        compiler_params=_cparams(2),
    )(proj, wb, coef, wc, *gather)
    return outs[0], outs[1], (_gather_own(outs[2:], gather, gather_bases) if ng else [])


def _glu_forward(y, u, d, wg_ref, bg):
    yf = y + d * u
    z = _gelu(yf)
    zb = z.astype(BF16)
    zz = jnp.concatenate([_dot(zb, wg_ref[sh]) for sh in range(N_CHIPS)], axis=-1) + bg
    return yf, z, zz[:, 0:SSM_WIDTH], zz[:, SSM_WIDTH:2 * SSM_WIDTH]


def _ssm_glu_fwd(y, proj, d, w_glu_all, layer, b_glu):
    s = y.shape[0]
    tm = _row_tile(s, 512)

    def body(y_ref, u_ref, gs_ref, d_ref, wg_ref, bg_ref, o_ref):
        _, _, val, gate = _glu_forward(y_ref[...], u_ref[...], d_ref[...], wg_ref, bg_ref[...])
        gs = gs_ref[...]
        o_ref[...] = val * _sigmoid(gate) * (gs * _sigmoid(gs))

    row = lambda i: (i, 0)
    return pl.pallas_call(
        body, name="ssm_glu", grid=(s // tm,),
        in_specs=[pl.BlockSpec((tm, SSM_WIDTH), row), pl.BlockSpec((tm, SSM_WIDTH), row),
                  pl.BlockSpec((tm, SSM_WIDTH), lambda i: (i, 1)), pl.BlockSpec((1, SSM_WIDTH), lambda i: (0, 0)),
                  pl.BlockSpec((N_CHIPS, None, SSM_WIDTH, ROW_SHARD), lambda i: (0, layer, 0, 0)),
                  pl.BlockSpec((1, 2 * SSM_WIDTH), lambda i: (0, 0))],
        out_specs=pl.BlockSpec((tm, SSM_WIDTH), row),
        out_shape=jax.ShapeDtypeStruct((s, SSM_WIDTH), F32),
        compiler_params=_cparams(1),
    )(y, proj, proj, d, w_glu_all, b_glu)


def _tri(kind):
    r = jnp.arange(ATTN_BLOCK)
    if kind == "suffix_incl":
        m = r[:, None] >= r[None, :]
    else:
        m = r[:, None] < r[None, :]
    return jnp.concatenate([m, jnp.ones_like(m)], axis=1).astype(BF16)


def _head_masks():
    lane = lax.broadcasted_iota(jnp.int32, (1, 2 * HEAD_DIM), 1)
    return [lane < HEAD_DIM, lane >= HEAD_DIM]


def _chain_step(t, base, n_sub, first, q_ref, k_ref, tri_ref, l_scr, per_chain):
    tb = ATTN_BLOCK
    row = lax.broadcasted_iota(jnp.int32, (tb, tb), 0)
    col = lax.broadcasted_iota(jnp.int32, (tb, tb), 1)
    masks = _head_masks()
    blks = [base + a - t for a in range(n_sub)]
    r0s = [pl.multiple_of(jnp.maximum(blk, 0) * tb, tb) for blk in blks]
    zs = []
    for a in range(n_sub):
        kb = k_ref[pl.ds(r0s[a], tb), :]
        qa = q_ref[a * tb:(a + 1) * tb, :]
        for mask in masks:
            zs.append(_dot_nt(jnp.where(mask, qa, jnp.zeros_like(qa)), kb))
    parts = []
    for z in zs:
        ls = jnp.minimum(-z, 0.0) - jnp.log(1.0 + jnp.exp(-jnp.abs(z)))
        if first:
            ls = jnp.where(col < row, ls, 0.0)
        parts.append(_split_hilo(ls))
    tri = tri_ref[...]
    sums = [_dot(hi, tri) + _dot(lo, tri) for hi, lo in parts]
    top = None
    ws = []
    for c, (z, sm) in enumerate(zip(zs, sums)):
        if first:
            lsum = jnp.zeros((tb, tb), F32)
        else:
            lsum = l_scr[c] + jnp.where(blks[c // 2] >= 0, 0.0, -1e30)
        w = jnp.exp(z + sm[:, 0:tb] + lsum)
        if first:
            w = jnp.where(col < row, w, 0.0)
        ws.append(w)
        lsum = lsum + sm[:, tb:2 * tb]
        l_scr[c] = lsum
        top = lsum if top is None else jnp.maximum(top, lsum)
    for c, (z, w) in enumerate(zip(zs, ws)):
        per_chain(c // 2, c % 2, c, r0s[c // 2], z, w)
    return jnp.max(top)


def _chain_sweep(base, n_sub, q_ref, k_ref, tri_ref, l_scr, per_chain):
    top = _chain_step(0, base, n_sub, True, q_ref, k_ref, tri_ref, l_scr, functools.partial(per_chain, 0))

    def cond(carry):
        t, top = carry
        return jnp.logical_and(t <= base + n_sub - 1, top > EXP_ZERO)

    def step(carry):
        t, _ = carry
        return t + 1, _chain_step(t, base, n_sub, False, q_ref, k_ref, tri_ref, l_scr, functools.partial(per_chain, t))

    steps, _ = lax.while_loop(cond, step, (jnp.int32(1), top))
    return steps


ATTN_SUB_FWD = 4
ATTN_SUB_BWD = 4


def _attn_fwd(qkv, proj):
    s = qkv.shape[0]
    tb = ATTN_BLOCK
    n_sub = min(ATTN_SUB_FWD, s // tb)
    tq = n_sub * tb

    def body(q_ref, k_ref, v_ref, g_ref, tri_ref, o_ref, ya_ref, l_scr):
        i = pl.program_id(1)
        masks = _head_masks()
        o_ref[...] = jnp.zeros_like(o_ref)

        def per_chain(t, a, h, c, r0, z, w):
            vb = v_ref[pl.ds(r0, tb), :]
            vb = jnp.where(masks[h], vb, jnp.zeros_like(vb))
            o_ref[a * tb:(a + 1) * tb, :] += _dot(w.astype(BF16), vb)

        _chain_sweep(i * n_sub, n_sub, q_ref, k_ref, tri_ref, l_scr, per_chain)
        g = g_ref[...]
        ya_ref[...] = o_ref[...] * (g * _sigmoid(g))

    hp_blk = lambda off: pl.BlockSpec((tq, 2 * HEAD_DIM), lambda hp, i: (i, off + hp))
    res = lambda off: pl.BlockSpec((s, 2 * HEAD_DIM), lambda hp, i: (0, off + hp))
    return pl.pallas_call(
        body, name="attn_fwd", grid=(ATTN_WIDTH // (2 * HEAD_DIM), s // tq),
        in_specs=[hp_blk(0), res(4), res(8), hp_blk(20), pl.BlockSpec((tb, 2 * tb), lambda hp, i: (0, 0))],
        out_specs=[hp_blk(0), hp_blk(0)],
        out_shape=[jax.ShapeDtypeStruct((s, ATTN_WIDTH), F32)] * 2,
        scratch_shapes=[pltpu.VMEM((2 * n_sub, tb, tb), F32)],
        compiler_params=_cparams(2),
    )(qkv, qkv, qkv, proj, _tri("suffix_incl"))


def _rms_rows(x, g):
    r = lax.rsqrt(jnp.mean(x * x, axis=-1, keepdims=True) + RMS_EPS)
    return r, x * r * g


def _ple_forward(h1, p, g2, wpg_ref, wpp_ref):
    r2, hn2 = _rms_rows(h1, g2)
    hb = hn2.astype(BF16)
    gpre = _dot(hb[:, 0:ROW_SHARD], wpg_ref[0])
    for sh in range(1, N_CHIPS):
        gpre = gpre + _dot(hb[:, ROW_SHARD * sh:ROW_SHARD * (sh + 1)], wpg_ref[sh])
    gate = _sigmoid(gpre)
    pb = p.astype(BF16)
    pp = jnp.concatenate([_dot(pb, wpp_ref[sh]) for sh in range(N_CHIPS)], axis=-1)
    return r2, hb, gate, pp


def _out_ple(h, ys, ya, p, g2, w_out_all, w_pg_all, w_pp_all, layer):
    s = h.shape[0]
    tm = _row_tile(s, 256)

    def body(h_ref, ys_ref, ya_ref, p_ref, g_ref, wo_ref, wpg_ref, wpp_ref, h1_ref, h2_ref):
        ysb = ys_ref[...].astype(BF16)
        yab = ya_ref[...].astype(BF16)
        h1 = h_ref[...]
        for sh, src in enumerate((ysb[:, 0:ROW_SHARD], ysb[:, ROW_SHARD:], yab[:, 0:ROW_SHARD], yab[:, ROW_SHARD:])):
            h1 = h1 + _dot(src, wo_ref[sh])
        _, _, gate, pp = _ple_forward(h1, p_ref[...], g_ref[...], wpg_ref, wpp_ref)
        h1_ref[...] = h1
        h2_ref[...] = h1 + gate * pp

    row = lambda i: (i, 0)
    wspec = lambda r, cdim: pl.BlockSpec((N_CHIPS, None, r, cdim), lambda i: (0, layer, 0, 0))
    return pl.pallas_call(
        body, name="out_ple", grid=(s // tm,),
        in_specs=[pl.BlockSpec((tm, D_MODEL), row), pl.BlockSpec((tm, SSM_WIDTH), row), pl.BlockSpec((tm, ATTN_WIDTH), row),
                  pl.BlockSpec((tm, PLE_DIM), row), pl.BlockSpec((1, D_MODEL), lambda i: (0, 0)),
                  wspec(ROW_SHARD, D_MODEL), wspec(ROW_SHARD, D_MODEL), wspec(PLE_DIM, ROW_SHARD)],
        out_specs=[pl.BlockSpec((tm, D_MODEL), row)] * 2,
        out_shape=[jax.ShapeDtypeStruct((s, D_MODEL), F32)] * 2,
        compiler_params=_cparams(1),
    )(h, ys, ya, p, g2, w_out_all, w_pg_all, w_pp_all)


def _loss_grad(y, target):
    s = y.shape[0]
    tm = _row_tile(s, 512)

    def body(y_ref, t_ref, dy_ref, acc_ref):
        @pl.when(pl.program_id(0) == 0)
        def _():
            acc_ref[...] = jnp.zeros_like(acc_ref)

        e = y_ref[...] - t_ref[...]
        dy_ref[...] = e / D_MODEL
        sq = (e * e).reshape(tm // SUBLANES, SUBLANES, D_MODEL).sum(axis=0)
        part = sq[:, 0:128]
        for b in range(1, D_MODEL // 128):
            part = part + sq[:, 128 * b:128 * (b + 1)]
        acc_ref[...] += part

    row = lambda i: (i, 0)
    return pl.pallas_call(
        body, name="loss_grad", grid=(s // tm,),
        in_specs=[pl.BlockSpec((tm, D_MODEL), row)] * 2,
        out_specs=[pl.BlockSpec((tm, D_MODEL), row), pl.BlockSpec((SUBLANES, 128), lambda i: (0, 0))],
        out_shape=[jax.ShapeDtypeStruct((s, D_MODEL), F32), jax.ShapeDtypeStruct((SUBLANES, 128), F32)],
        compiler_params=_cparams(1),
    )(y, target)


def _rms_bwd(x, r, g, dy):
    gdy = g * dy
    dx = r * gdy - x * (r * r * r) * jnp.mean(x * gdy, axis=-1, keepdims=True)
    return dx, x * r * dy


def _colsum8(a):
    t = a.shape[0]
    return a.reshape(t // SUBLANES, SUBLANES, a.shape[1]).sum(axis=0)


def _out_ple_bwd(dh2, h1, p, g2, w_out_all, w_pg_all, w_pp_all, layer):
    s = h1.shape[0]
    tm = _row_tile(s, 256)

    def body(dh2_ref, h1_ref, p_ref, g_ref, wo_ref, wpg_ref, wpp_ref,
             dh1_ref, dmix_ref, hn_ref, dgp_ref, dpp_ref, dh1b_ref, dg_ref):
        @pl.when(pl.program_id(0) == 0)
        def _():
            dg_ref[...] = jnp.zeros_like(dg_ref)

        h1 = h1_ref[...]
        dh2 = dh2_ref[...]
        g2v = g_ref[...]
        r2, hb, gate, pp = _ple_forward(h1, p_ref[...], g2v, wpg_ref, wpp_ref)
        dgp = (dh2 * pp) * gate * (1.0 - gate)
        dgpb = dgp.astype(BF16)
        dhn = jnp.concatenate([_dot_nt(dgpb, wpg_ref[sh]) for sh in range(N_CHIPS)], axis=-1)
        dx, dgrow = _rms_bwd(h1, r2, g2v, dhn)
        dh1 = dh2 + dx
        dh1b = dh1.astype(BF16)
        dh1_ref[...] = dh1
        dh1b_ref[...] = dh1b
        hn_ref[...] = hb
        dgp_ref[...] = dgpb
        dpp_ref[...] = (dh2 * gate).astype(BF16)
        dg_ref[...] += _colsum8(dgrow)
        for sh in range(N_CHIPS):
            dmix_ref[:, ROW_SHARD * sh:ROW_SHARD * (sh + 1)] = _dot_nt(dh1b, wo_ref[sh])

    row = lambda i: (i, 0)
    wspec = lambda r, cdim: pl.BlockSpec((N_CHIPS, None, r, cdim), lambda i: (0, layer, 0, 0))
    big = pl.BlockSpec((tm, D_MODEL), row)
    return pl.pallas_call(
        body, name="out_ple_bwd", grid=(s // tm,),
        in_specs=[big, big, pl.BlockSpec((tm, PLE_DIM), row), pl.BlockSpec((1, D_MODEL), lambda i: (0, 0)),
                  wspec(ROW_SHARD, D_MODEL), wspec(ROW_SHARD, D_MODEL), wspec(PLE_DIM, ROW_SHARD)],
        out_specs=[big] * 6 + [pl.BlockSpec((SUBLANES, D_MODEL), lambda i: (0, 0))],
        out_shape=[jax.ShapeDtypeStruct((s, D_MODEL), F32)] * 2 + [jax.ShapeDtypeStruct((s, D_MODEL), BF16)] * 4
        + [jax.ShapeDtypeStruct((SUBLANES, D_MODEL), F32)],
        compiler_params=_cparams(1),
    )(dh2, h1, p, g2, w_out_all, w_pg_all, w_pp_all)


def _tn_matmul(a, b, n_blocks, block_a, name, into=None, first_block=0, total_blocks=None):
    s = a.shape[0]
    tk = _row_tile(s, 512)
    nk = s // tk
    total_blocks = n_blocks if total_blocks is None else total_blocks
    ka, nb = a.shape[1], b.shape[1]
    if block_a:
        ka //= n_blocks
    else:
        nb //= n_blocks

    def body(*refs):
        a_ref, b_ref, o_ref, acc_ref = refs[0], refs[1], refs[-2], refs[-1]

        @pl.when(pl.program_id(0) == 0)
        def _():
            acc_ref[...] = jnp.zeros_like(acc_ref)

        at = a_ref[...].astype(BF16).T
        bb = b_ref[...].astype(BF16)
        for sh in range(n_blocks):
            if block_a:
                acc_ref[sh] += _dot(at[ka * sh:ka * (sh + 1), :], bb)
            else:
                acc_ref[sh] += _dot(at, bb[:, nb * sh:nb * (sh + 1)])

        @pl.when(pl.program_id(0) == nk - 1)
        def _():
            o_ref[...] = acc_ref[...].astype(BF16)

    in_specs = [pl.BlockSpec((tk, a.shape[1]), lambda i: (i, 0)), pl.BlockSpec((tk, b.shape[1]), lambda i: (i, 0))]
    operands = [a, b]
    aliases = {}
    if into is not None:
        in_specs.append(ANY)
        operands.append(into)
        aliases = {2: 0}
    return pl.pallas_call(
        body, name=name, grid=(nk,),
        in_specs=in_specs,
        out_specs=pl.BlockSpec((n_blocks, ka, nb), lambda i: (first_block // n_blocks, 0, 0)),
        out_shape=jax.ShapeDtypeStruct((total_blocks, ka, nb), BF16),
        scratch_shapes=[pltpu.VMEM((n_blocks, ka, nb), F32)],
        input_output_aliases=aliases,
        compiler_params=_cparams(1),
    )(*operands)


def _attn_bwd(qkv, o, proj, dmix, scatter=None, scatter_owner=None):
    scatter = [] if scatter is None else scatter
    nsc = len(scatter)
    owners = [scatter_owner] * nsc
    s = qkv.shape[0]
    tb = ATTN_BLOCK
    nq = s // tb
    n_sub = min(ATTN_SUB_BWD, nq)
    tq = n_sub * tb
    n_chain = 2 * n_sub

    def body(*refs):
        q_ref, k_ref, v_ref, o_ref, g_ref, dya_ref, tri_s_ref, tri_p_ref = refs[0:8]
        sc_ins = refs[8:8 + nsc]
        dq_ref, dk_ref, dv_ref, dg_ref = refs[8 + nsc:12 + nsc]
        sc_outs = refs[12 + nsc:12 + 2 * nsc]
        do_scr, l_scr, g_scr, s_scr, w_scr = refs[12 + 2 * nsc:17 + 2 * nsc]
        sc_sems = refs[17 + 2 * nsc:]
        i = pl.program_id(1)
        base = i * n_sub

        if nsc:
            @pl.when(jnp.logical_and(pl.program_id(0) == 0, i == 0))
            def _():
                _scatter_start(sc_ins, sc_outs, owners, sc_sems)

        @pl.when(i == 0)
        def _():
            dk_ref[...] = jnp.zeros_like(dk_ref)
            dv_ref[...] = jnp.zeros_like(dv_ref)

        g = g_ref[...]
        sg = _sigmoid(g)
        dya = dya_ref[...]
        do_scr[...] = (dya * (g * sg)).astype(BF16)
        dg_ref[...] = dya * o_ref[...] * (sg * (1.0 + g * (1.0 - sg)))
        dq_ref[...] = jnp.zeros_like(dq_ref)
        g_scr[...] = jnp.zeros_like(g_scr)
        masks = _head_masks()

        def keep(t, a, h, c, r0, z, w):
            s_scr[c, t] = _sigmoid(z).astype(BF16)
            w_scr[c, t] = w.astype(BF16)

        steps = _chain_sweep(base, n_sub, q_ref, k_ref, tri_s_ref, l_scr, keep)
        row = lax.broadcasted_iota(jnp.int32, (tb, tb), 0)
        col = lax.broadcasted_iota(jnp.int32, (tb, tb), 1)

        def back(it, carry):
            t = steps - 1 - it
            r0s = [pl.multiple_of(jnp.maximum(base + a - t, 0) * tb, tb) for a in range(n_sub)]
            qhs, dohs, khs, gws = [], [], [], []
            for a in range(n_sub):
                kb = k_ref[pl.ds(r0s[a], tb), :]
                vb = v_ref[pl.ds(r0s[a], tb), :]
                qa = q_ref[a * tb:(a + 1) * tb, :]
                doa = do_scr[a * tb:(a + 1) * tb, :]
                for h, mask in enumerate(masks):
                    qhs.append(jnp.where(mask, qa, jnp.zeros_like(qa)))
                    khs.append(jnp.where(mask, kb, jnp.zeros_like(kb)))
                    dohs.append(jnp.where(mask, doa, jnp.zeros_like(doa)))
                    gws.append(w_scr[2 * a + h, t].astype(F32) * _dot_nt(dohs[-1], vb))
            parts = [_split_hilo(gw) for gw in gws]
            tri = tri_p_ref[...]
            sums = [_dot(hi, tri) + _dot(lo, tri) for hi, lo in parts]
            dzs = []
            for c, (gw, sm) in enumerate(zip(gws, sums)):
                gsum = g_scr[c]
                dz = gw - (gw + sm[:, 0:tb] + gsum) * s_scr[c, t].astype(F32)
                dz = jnp.where(col < row + t * tb, dz, 0.0)
                g_scr[c] = gsum + sm[:, tb:2 * tb]
                dzs.append(dz.astype(BF16))
            for c, dzb in enumerate(dzs):
                a = c // 2
                dk_ref[pl.ds(r0s[a], tb), :] += _dot_tn(dzb, qhs[c])
                dv_ref[pl.ds(r0s[a], tb), :] += _dot_tn(w_scr[c, t], dohs[c])
                dq_ref[a * tb:(a + 1) * tb, :] += _dot(dzb, khs[c])
            return carry

        lax.fori_loop(0, steps, back, 0)

        if nsc:
            @pl.when(jnp.logical_and(pl.program_id(0) == n_hp - 1, i == s // tq - 1))
            def _():
                _scatter_finish(sc_ins, sc_outs, owners, sc_sems)

    n_hp = ATTN_WIDTH // (2 * HEAD_DIM)
    hp_blk = lambda off: pl.BlockSpec((tq, 2 * HEAD_DIM), lambda hp, i: (i, off + hp))
    res = lambda off: pl.BlockSpec((s, 2 * HEAD_DIM), lambda hp, i: (0, off + hp))
    tri = pl.BlockSpec((tb, 2 * tb), lambda hp, i: (0, 0))
    outs = pl.pallas_call(
        body, name="attn_bwd_scatter" if nsc else "attn_bwd", grid=(n_hp, s // tq),
        in_specs=[hp_blk(0), res(4), res(8), hp_blk(0), hp_blk(20), hp_blk(4), tri, tri] + [ANY] * nsc,
        out_specs=[hp_blk(0), res(0), res(0), hp_blk(0)] + [ANY] * nsc,
        out_shape=[jax.ShapeDtypeStruct((s, ATTN_WIDTH), F32)] * 4 + [jax.ShapeDtypeStruct(a.shape, a.dtype) for a in scatter],
        scratch_shapes=[pltpu.VMEM((tq, 2 * HEAD_DIM), BF16), pltpu.VMEM((n_chain, tb, tb), F32),
                        pltpu.VMEM((n_chain, tb, tb), F32), pltpu.VMEM((n_chain, nq, tb, tb), BF16),
                        pltpu.VMEM((n_chain, nq, tb, tb), BF16)] + (_scatter_sems(nsc) if nsc else []),
        compiler_params=_cparams(2),
    )(qkv, qkv, qkv, o, proj, dmix, _tri("suffix_incl"), _tri("prefix_strict"), *scatter)
    return outs[0], outs[1], outs[2], outs[3], (_scatter_own(outs[4:], scatter) if nsc else [])


def _ssm_glu_bwd(dmix, y, proj, d, w_glu_all, layer, b_glu):
    s = y.shape[0]
    tm = _row_tile(s, 512)

    def body(dys_ref, y_ref, u_ref, gs_ref, d_ref, wg_ref, bg_ref,
             dyf_ref, du_ref, dgs_ref, z_ref, dzz_ref, dd_ref, db_ref):
        @pl.when(pl.program_id(0) == 0)
        def _():
            dd_ref[...] = jnp.zeros_like(dd_ref)
            db_ref[...] = jnp.zeros_like(db_ref)

        u = u_ref[...]
        dv = d_ref[...]
        yf, z, val, gate = _glu_forward(y_ref[...], u, dv, wg_ref, bg_ref[...])
        gs = gs_ref[...]
        sgs = _sigmoid(gs)
        sgate = _sigmoid(gate)
        dys = dys_ref[...]
        dgv = dys * (gs * sgs)
        dgs_ref[...] = dys * (val * sgate) * (sgs * (1.0 + gs * (1.0 - sgs)))
        dzz = jnp.concatenate([dgv * sgate, dgv * val * sgate * (1.0 - sgate)], axis=-1)
        dzzb = dzz.astype(BF16)
        dz = _dot_nt(dzzb[:, 0:ROW_SHARD], wg_ref[0])
        for sh in range(1, N_CHIPS):
            dz = dz + _dot_nt(dzzb[:, ROW_SHARD * sh:ROW_SHARD * (sh + 1)], wg_ref[sh])
        dyf = dz * _gelu_grad(yf)
        dyf_ref[...] = dyf
        du_ref[...] = dyf * dv
        z_ref[...] = z.astype(BF16)
        dzz_ref[...] = dzzb
        dd_ref[...] += _colsum8(dyf * u)
        db_ref[...] += _colsum8(dzz)

    row = lambda i: (i, 0)
    half = pl.BlockSpec((tm, SSM_WIDTH), row)
    return pl.pallas_call(
        body, name="ssm_glu_bwd", grid=(s // tm,),
        in_specs=[half, half, half, pl.BlockSpec((tm, SSM_WIDTH), lambda i: (i, 1)),
                  pl.BlockSpec((1, SSM_WIDTH), lambda i: (0, 0)),
                  pl.BlockSpec((N_CHIPS, None, SSM_WIDTH, ROW_SHARD), lambda i: (0, layer, 0, 0)),
                  pl.BlockSpec((1, 2 * SSM_WIDTH), lambda i: (0, 0))],
        out_specs=[half, half, half, half, pl.BlockSpec((tm, 2 * SSM_WIDTH), row),
                   pl.BlockSpec((SUBLANES, SSM_WIDTH), lambda i: (0, 0)),
                   pl.BlockSpec((SUBLANES, 2 * SSM_WIDTH), lambda i: (0, 0))],
        out_shape=[jax.ShapeDtypeStruct((s, SSM_WIDTH), F32)] * 3
        + [jax.ShapeDtypeStruct((s, SSM_WIDTH), BF16), jax.ShapeDtypeStruct((s, 2 * SSM_WIDTH), BF16),
           jax.ShapeDtypeStruct((SUBLANES, SSM_WIDTH), F32), jax.ShapeDtypeStruct((SUBLANES, 2 * SSM_WIDTH), F32)],
        compiler_params=_cparams(1),
    )(dmix, y, proj, proj, d, w_glu_all, b_glu)


def _ssm_scan_bwd(dyf, xs, proj, wct, coef_rev, wbt):
    s = dyf.shape[0]
    tm = _row_tile(s, 512)
    nt = s // tm

    def body(dy_ref, xs_ref, u_ref, wct_ref, coef_ref, wbt_ref, du_ref, dwc_ref, dwb_ref, da_ref, lam_ref, carry_ref):
        @pl.when(pl.program_id(1) == 0)
        def _():
            carry_ref[...] = jnp.zeros_like(carry_ref)
            dwc_ref[...] = jnp.zeros_like(dwc_ref)
            dwb_ref[...] = jnp.zeros_like(dwb_ref)
            da_ref[...] = jnp.zeros_like(da_ref)

        dyb = dy_ref[...].astype(BF16)
        lam_ref[...] = _dot(dyb, wct_ref[...])
        rows = lax.broadcasted_iota(jnp.int32, (SUBLANES, CH_S), 0)
        last = rows == SUBLANES - 1

        def extra(r0, lr, li, cr, ci):
            er = jnp.where(last, cr, pltpu.roll(lr, SUBLANES - 1, 0))
            ei = jnp.where(last, ci, pltpu.roll(li, SUBLANES - 1, 0))
            xr = xs_ref[pl.ds(r0, SUBLANES), 0:CH_S]
            xi = xs_ref[pl.ds(r0, SUBLANES), CH_S:2 * CH_S]
            da_ref[:, 0:CH_S] += xr * er + xi * ei
            da_ref[:, CH_S:2 * CH_S] += xr * ei - xi * er

        _scan_rows(lam_ref, coef_ref, carry_ref, tm // SUBLANES, reverse=True, extra=extra)
        lamb = lam_ref[...].astype(BF16)
        du_ref[...] = _dot(lamb, wbt_ref[...])
        dwc_ref[...] += _dot_tn(xs_ref[...].astype(BF16), dyb)
        dwb_ref[...] += _dot_tn(u_ref[...].astype(BF16), lamb)

    rev = lambda j, i: (nt - 1 - i, j)
    return pl.pallas_call(
        body, name="ssm_scan_bwd", grid=(SSM_CHUNKS, nt),
        in_specs=[pl.BlockSpec((tm, CH_W), rev),
                  pl.BlockSpec((None, tm, 2 * CH_S), lambda j, i: (j, nt - 1 - i, 0)),
                  pl.BlockSpec((tm, CH_W), rev),
                  pl.BlockSpec((None, CH_W, 2 * CH_S), lambda j, i: (j, 0, 0)),
                  pl.BlockSpec((None, 8, SUBLANES, CH_S), lambda j, i: (j, 0, 0, 0)),
                  pl.BlockSpec((None, 2 * CH_S, CH_W), lambda j, i: (j, 0, 0))],
        out_specs=[pl.BlockSpec((tm, CH_W), rev),
                   pl.BlockSpec((None, 2 * CH_S, CH_W), lambda j, i: (j, 0, 0)),
                   pl.BlockSpec((None, CH_W, 2 * CH_S), lambda j, i: (j, 0, 0)),
                   pl.BlockSpec((None, SUBLANES, 2 * CH_S), lambda j, i: (j, 0, 0))],
        out_shape=[jax.ShapeDtypeStruct((s, SSM_WIDTH), F32),
                   jax.ShapeDtypeStruct((SSM_CHUNKS, 2 * CH_S, CH_W), F32),
                   jax.ShapeDtypeStruct((SSM_CHUNKS, CH_W, 2 * CH_S), F32),
                   jax.ShapeDtypeStruct((SSM_CHUNKS, SUBLANES, 2 * CH_S), F32)],
        scratch_shapes=[pltpu.VMEM((tm, 2 * CH_S), F32), pltpu.VMEM((SUBLANES, 2 * CH_S), F32)],
        compiler_params=_cparams(2),
    )(dyf, xs, proj, wct, coef_rev, wbt)


def _in_proj_bwd(h, g1, w_in_l, qg, kg, proj, du_a, du_b, dgs, dq, dk, dv, dga, dh1):
    s = h.shape[0]
    tm = _row_tile(s, 256)

    def body(h_ref, g_ref, w_ref, qg_ref, kg_ref, ones_ref, q_ref, k_ref, dua_ref, dub_ref, dgs_ref, dq_ref, dk_ref,
             dv_ref, dga_ref, dh1_ref, dh_ref, hn_ref, dp_ref, dg1_ref, dqg_ref, dkg_ref):
        @pl.when(pl.program_id(0) == 0)
        def _():
            dg1_ref[...] = jnp.zeros_like(dg1_ref)
            dqg_ref[...] = jnp.zeros_like(dqg_ref)
            dkg_ref[...] = jnp.zeros_like(dkg_ref)

        ones = ones_ref[...]

        def head_norm_bwd(x, gain, dy):
            r = lax.rsqrt(_dot_hilo(x * x, ones) + RMS_EPS)
            gdy = gain * dy
            dx = r * gdy - x * (r * r * r) * _dot_hilo(x * gdy, ones)
            return dx, x * r * dy

        dqr, dqg_rows = head_norm_bwd(q_ref[...], qg_ref[...], dq_ref[...] * ATTN_SCALE)
        dkr, dkg_rows = head_norm_bwd(k_ref[...], kg_ref[...], dk_ref[...])
        dqg_ref[...] += _colsum8(dqg_rows)
        dkg_ref[...] += _colsum8(dkg_rows)
        dp_ref[:, 0:512] = (dua_ref[...] + dub_ref[...]).astype(BF16)
        dp_ref[:, 512:1024] = dgs_ref[...].astype(BF16)
        dp_ref[:, 1024:1536] = dqr.astype(BF16)
        dp_ref[:, 1536:2048] = dkr.astype(BF16)
        dp_ref[:, 2048:2560] = dv_ref[...].astype(BF16)
        dp_ref[:, 2560:3072] = dga_ref[...].astype(BF16)
        dhn = _dot_nt(dp_ref[:, 0:IN_SHARD], w_ref[0])
        for sh in range(1, N_CHIPS):
            dhn = dhn + _dot_nt(dp_ref[:, IN_SHARD * sh:IN_SHARD * (sh + 1)], w_ref[sh])
        x = h_ref[...]
        gv = g_ref[...]
        r, hn = _rms_rows(x, gv)
        dx, dg_rows = _rms_bwd(x, r, gv, dhn)
        dh_ref[...] = dh1_ref[...] + dx
        hn_ref[...] = hn.astype(BF16)
        dg1_ref[...] += _colsum8(dg_rows)

    row = lambda i: (i, 0)
    full = lambda shape: pl.BlockSpec(shape, lambda i: (0,) * len(shape))
    big = pl.BlockSpec((tm, D_MODEL), row)
    half = pl.BlockSpec((tm, 512), row)
    return pl.pallas_call(
        body, name="in_proj_bwd", grid=(s // tm,),
        in_specs=[big, full((1, D_MODEL)), full((N_CHIPS, D_MODEL, IN_SHARD)),
                  full((1, ATTN_WIDTH)), full((1, ATTN_WIDTH)), full((ATTN_WIDTH, ATTN_WIDTH)),
                  pl.BlockSpec((tm, 512), lambda i: (i, 2)), pl.BlockSpec((tm, 512), lambda i: (i, 3)),
                  half, half, half, half, half, half, half, big],
        out_specs=[big, big, pl.BlockSpec((tm, IN_COLS), row), pl.BlockSpec((SUBLANES, D_MODEL), lambda i: (0, 0)),
                   pl.BlockSpec((SUBLANES, ATTN_WIDTH), lambda i: (0, 0)), pl.BlockSpec((SUBLANES, ATTN_WIDTH), lambda i: (0, 0))],
        out_shape=[jax.ShapeDtypeStruct((s, D_MODEL), F32), jax.ShapeDtypeStruct((s, D_MODEL), BF16),
                   jax.ShapeDtypeStruct((s, IN_COLS), BF16), jax.ShapeDtypeStruct((SUBLANES, D_MODEL), F32),
                   jax.ShapeDtypeStruct((SUBLANES, ATTN_WIDTH), F32), jax.ShapeDtypeStruct((SUBLANES, ATTN_WIDTH), F32)],
        compiler_params=_cparams(1),
    )(h, g1, w_in_l, qg, kg, _head_ones(), proj, proj, du_a, du_b, dgs, dq, dk, dv, dga, dh1)


SMALL_NAMES = ("mix_norm_g", "ssm_a_re", "ssm_a_im", "ssm_log_dt", "ssm_b_re", "ssm_b_im", "ssm_c_re", "ssm_c_im",
               "ssm_d", "ssm_b_glu", "q_norm_g", "k_norm_g", "ple_norm_g")
SMALL_4D = ("ssm_b_re", "ssm_b_im", "ssm_c_re", "ssm_c_im")
BIG_NAMES = ("w_in", "ssm_w_glu", "w_out", "w_ple_gate", "w_ple_proj")


def _ssm_setup(sm, layer):
    col = lambda a: a[layer].reshape(1, N_STATES)
    a_re, a_im = col(sm["ssm_a_re"]), col(sm["ssm_a_im"])
    log_dt = jnp.repeat(sm["ssm_log_dt"][layer], SSM_STATE).reshape(1, N_STATES)
    b_re = sm["ssm_b_re"][layer].reshape(N_STATES, SSM_GROUP).T
    b_im = sm["ssm_b_im"][layer].reshape(N_STATES, SSM_GROUP).T
    disc_in = (a_re, a_im, log_dt, b_re, b_im)
    ab_re, ab_im, bb_re, bb_im = _disc_fwd(*disc_in)
    wb = jnp.concatenate([_block_diag_in(bb_re), _block_diag_in(bb_im)], axis=-1)
    wc = jnp.concatenate([_block_diag_out(sm["ssm_c_re"][layer]), -_block_diag_out(sm["ssm_c_im"][layer])], axis=1)
    return dict(disc_in=disc_in, wb=wb.astype(BF16), wbt=wb.transpose(0, 2, 1).astype(BF16),
                wc=wc.astype(BF16), wct=wc.transpose(0, 2, 1).astype(BF16),
                coef=_scan_coefs(ab_re, ab_im, False), coef_rev=_scan_coefs(ab_re, ab_im, True))


def _gathered_weights(w_in0, rest):
    wg = dict(zip(BIG_NAMES[1:], rest[1:]))
    wg["w_in"] = [w_in0, rest[0].reshape(N_CHIPS, D_MODEL, IN_SHARD)]
    return wg


def _local_step(x, p, target, sm, w_in0, rest_local=None, rest_bases=None, rest_gathered=None, layer1_hook=None):
    wg = None if rest_gathered is None else _gathered_weights(w_in0, rest_gathered)
    tile8 = lambda a: jnp.tile(a, ATTN_WIDTH // HEAD_DIM).reshape(1, ATTN_WIDTH)
    saved = []
    h = x
    for l in range(N_LAYERS):
        ssm = _ssm_setup(sm, l)
        g1 = sm["mix_norm_g"][l].reshape(1, D_MODEL)
        g2 = sm["ple_norm_g"][l].reshape(1, D_MODEL)
        qg, kg = tile8(sm["q_norm_g"][l]), tile8(sm["k_norm_g"][l])
        dsk = sm["ssm_d"][l].reshape(1, SSM_WIDTH)
        bgl = sm["ssm_b_glu"][l].reshape(1, 2 * SSM_WIDTH)
        proj, qkv = _in_proj(h, g1, w_in0 if l == 0 else wg["w_in"][l], qg, kg)
        if wg is None:
            xs, y, rest = _ssm_scan_fwd(proj, ssm["wb"], ssm["coef"], ssm["wc"], rest_local, rest_bases)
            wg = _gathered_weights(w_in0, rest)
        else:
            xs, y, _ = _ssm_scan_fwd(proj, ssm["wb"], ssm["coef"], ssm["wc"])
        ys = _ssm_glu_fwd(y, proj, dsk, wg["ssm_w_glu"], l, bgl)
        o, ya = _attn_fwd(qkv, proj)
        h1, h2 = _out_ple(h, ys, ya, p[l], g2, wg["w_out"], wg["w_ple_gate"], wg["w_ple_proj"], l)
        saved.append(dict(ssm=ssm, g1=g1, g2=g2, qg=qg, kg=kg, dsk=dsk, bgl=bgl, h=h, proj=proj, qkv=qkv, xs=xs, y=y,
                          ys=ys, o=o, ya=ya, h1=h1))
        h = h2
    dh, sq = _loss_grad(h, target)
    loss = 0.5 * jnp.sum(sq) / D_MODEL

    gbig = [{} for _ in range(N_LAYERS)]
    scattered = []
    gsm = {n: [None] * N_LAYERS for n in SMALL_NAMES}
    for l in reversed(range(N_LAYERS)):
        sv = saved[l]
        ssm = sv["ssm"]
        dh1, dmix, hn2b, dgpb, dppb, dh1b, dg2 = _out_ple_bwd(dh, sv["h1"], p[l], sv["g2"], wg["w_out"],
                                                              wg["w_ple_gate"], wg["w_ple_proj"], l)
        gsm["ple_norm_g"][l] = dg2.sum(0)
        gbig[l]["w_ple_proj"] = _tn_matmul(p[l], dppb, N_CHIPS, False, "dw_ple_proj")
        gbig[l]["w_ple_gate"] = _tn_matmul(hn2b, dgpb, N_CHIPS, True, "dw_ple_gate")
        dwo = _tn_matmul(sv["ys"], dh1b, 2, True, "dw_out_ssm", None, 0, N_CHIPS)
        gbig[l]["w_out"] = _tn_matmul(sv["ya"], dh1b, 2, True, "dw_out_attn", dwo, 2, N_CHIPS)
        if l == 0 and layer1_hook is not None:
            dqs, dkn, dv, dga, scattered = _attn_bwd(sv["qkv"], sv["o"], sv["proj"], dmix, *layer1_hook(gbig[1]))
        else:
            dqs, dkn, dv, dga, _ = _attn_bwd(sv["qkv"], sv["o"], sv["proj"], dmix)
        dyf, du_a, dgs, zb, dzzb, dd, dbg = _ssm_glu_bwd(dmix, sv["y"], sv["proj"], sv["dsk"], wg["ssm_w_glu"], l, sv["bgl"])
        gsm["ssm_d"][l] = dd.sum(0).reshape(SSM_GROUPS, SSM_GROUP)
        gsm["ssm_b_glu"][l] = dbg.sum(0)
        gbig[l]["ssm_w_glu"] = _tn_matmul(zb, dzzb, N_CHIPS, False, "dw_glu")
        du_b, dwc, dwb, da = _ssm_scan_bwd(dyf, sv["xs"], sv["proj"], ssm["wct"], ssm["coef_rev"], ssm["wbt"])
        gsm["ssm_c_re"][l] = _block_diag_out_t(dwc[:, 0:CH_S, :])
        gsm["ssm_c_im"][l] = -_block_diag_out_t(dwc[:, CH_S:, :])
        da = da.sum(1)
        g_ab_re = da[:, 0:CH_S].reshape(1, N_STATES)
        g_ab_im = da[:, CH_S:].reshape(1, N_STATES)
        g_bb_re = _block_diag_in_t(dwb[:, :, 0:CH_S])
        g_bb_im = _block_diag_in_t(dwb[:, :, CH_S:])
        d_are, d_aim, d_ldt, d_bre, d_bim = _disc_bwd(*ssm["disc_in"], g_ab_re, g_ab_im, g_bb_re, g_bb_im)
        gsm["ssm_a_re"][l] = d_are.reshape(SSM_GROUPS, SSM_STATE)
        gsm["ssm_a_im"][l] = d_aim.reshape(SSM_GROUPS, SSM_STATE)
        gsm["ssm_log_dt"][l] = d_ldt.reshape(SSM_GROUPS, SSM_STATE).sum(1)
        gsm["ssm_b_re"][l] = d_bre.T.reshape(SSM_GROUPS, SSM_STATE, SSM_GROUP)
        gsm["ssm_b_im"][l] = d_bim.T.reshape(SSM_GROUPS, SSM_STATE, SSM_GROUP)
        dh, hnb, dprojb, dg1, dqg, dkg = _in_proj_bwd(sv["h"], sv["g1"], wg["w_in"][l], sv["qg"], sv["kg"], sv["proj"],
                                                      du_a, du_b, dgs, dqs, dkn, dv, dga, dh1)
        gsm["mix_norm_g"][l] = dg1.sum(0)
        gsm["q_norm_g"][l] = dqg.sum(0).reshape(-1, HEAD_DIM).sum(0)
        gsm["k_norm_g"][l] = dkg.sum(0).reshape(-1, HEAD_DIM).sum(0)
        gbig[l]["w_in"] = _tn_matmul(hnb, dprojb, N_CHIPS, False, "dw_in")
    gsm = {n: jnp.stack(v, 0) for n, v in gsm.items()}
    return loss, dh, gbig, gsm, scattered


_SMALL_PAD = 8 * 8 * 128


def _pack_small(d):
    flat = jnp.concatenate([d[n].reshape(-1) for n in SMALL_NAMES])
    n = flat.shape[0]
    padded = -(-n // _SMALL_PAD) * _SMALL_PAD
    return jnp.pad(flat, (0, padded - n))


def _unpack_small(flat, like):
    out, off = {}, 0
    for n in SMALL_NAMES:
        size = like[n].size
        out[n] = flat[off:off + size].reshape(like[n].shape)
        off += size
    return out


def _flat_rows(a):
    return a.reshape(-1, a.shape[-1])


def _chip_sums(glayer, owner):
    parts = [glayer[n] for n in BIG_NAMES]
    flat = [_flat_rows(a) for a in parts]
    recv = _sibling_push(flat, [owner] * len(flat), "grad_push_layer%d" % owner)
    return [_add_pair(f, r, "grad_pair_add").reshape(a.shape) for f, r, a in zip(flat, recv, parts)]


def kernel(x, p, mix_norm_g, w_in, ssm_a_re, ssm_a_im, ssm_log_dt, ssm_b_re, ssm_b_im, ssm_c_re, ssm_c_im, ssm_d, ssm_w_glu, ssm_b_glu, q_norm_g, k_norm_g, w_out, ple_norm_g, w_ple_gate, w_ple_proj, loss_target, m_mix_norm_g, m_w_in, m_ssm_a_re, m_ssm_a_im, m_ssm_log_dt, m_ssm_b_re, m_ssm_b_im, m_ssm_c_re, m_ssm_c_im, m_ssm_d, m_ssm_w_glu, m_ssm_b_glu, m_q_norm_g, m_k_norm_g, m_w_out, m_ple_norm_g, m_w_ple_gate, m_w_ple_proj, v_mix_norm_g, v_w_in, v_ssm_a_re, v_ssm_a_im, v_ssm_log_dt, v_ssm_b_re, v_ssm_b_im, v_ssm_c_re, v_ssm_c_im, v_ssm_d, v_ssm_w_glu, v_ssm_b_glu, v_q_norm_g, v_k_norm_g, v_w_out, v_ple_norm_g, v_w_ple_gate, v_w_ple_proj):
    args = dict(locals())
    names = ("mix_norm_g", "w_in", "ssm_a_re", "ssm_a_im", "ssm_log_dt", "ssm_b_re", "ssm_b_im", "ssm_c_re", "ssm_c_im",
             "ssm_d", "ssm_w_glu", "ssm_b_glu", "q_norm_g", "k_norm_g", "w_out", "ple_norm_g", "w_ple_gate", "w_ple_proj")
    w = {n: args[n] for n in names}
    m = {n: args["m_" + n] for n in names}
    v = {n: args["v_" + n] for n in names}

    w_in_halves = w["w_in"].astype(BF16).reshape(2 * N_LAYERS, D_MODEL // 2, IN_SHARD)
    w_in0 = _chip_gather([w_in_halves], "w_in_gather")[0].reshape(N_CHIPS, D_MODEL, IN_SHARD)
    rest_local = [w_in_halves] + [w[n].astype(BF16) for n in BIG_NAMES[1:]]
    sm = {n: w[n] for n in SMALL_NAMES}
    loss, dx, gbig, gsm, got1 = _local_step(x[0], p[:, 0], loss_target[0], sm, w_in0, rest_local, [2, 0, 0, 0, 0],
                                            layer1_hook=lambda g1: (_chip_sums(g1, 1), 1))
    loss = lax.psum(loss, ("x", "y", "c"))

    nb = len(BIG_NAMES)
    small = _pack_small(gsm).reshape(2, N_CHIPS * SUBLANES, -1)
    flat0 = [_flat_rows(gbig[0][n]) for n in BIG_NAMES]
    recv = _sibling_push(flat0 + [small], [0] * nb + [None], "grad_push_layer0")
    chip0 = [_add_pair(f, r, "grad_pair_add").reshape(gbig[0][n].shape) for f, r, n in zip(flat0, recv, BIG_NAMES)]
    chip_small = _add_half(small, recv[-1], F32, "grad_half_add").reshape(N_CHIPS, SUBLANES, -1)
    got0 = _chip_scatter(chip0 + [chip_small], [0] * nb + [None], "grad_chip_scatter")
    tot1 = [_sum4(a, "grad_chip_sum") for a in got1]
    tot0 = [_sum4(a, "grad_chip_sum") for a in got0]
    other, (small_mine,) = _sibling_join(tot0[:nb], tot1, [tot0[nb]], "grad_sibling_join")
    small_all = _chip_gather([small_mine], "small_grad_gather")[0]
    small_tot = small_all.transpose(1, 0, 2, 3).reshape(-1)
    on_core0 = lax.axis_index("c") == 0
    g = {n: jnp.where(on_core0, jnp.stack([t0, ot]), jnp.stack([ot, t1])).reshape(w[n].shape)
         for n, t0, t1, ot in zip(BIG_NAMES, tot0, tot1, other)}
    g.update(_unpack_small(small_tot, sm))

    delta, new_m, new_v = {}, {}, {}
    for n in BIG_NAMES:
        lanes = w[n].shape[-1]
        outs = _adamw(_as_rows(w[n], lanes), _as_rows(g[n], lanes), _as_rows(m[n], lanes), _as_rows(v[n], lanes), "adamw_" + n)
        delta[n], new_m[n], new_v[n] = [o.reshape(w[n].shape) for o in outs]
    for group, per_layer in ((SMALL_4D, True), (tuple(n for n in SMALL_NAMES if n not in SMALL_4D), False)):
        outs = _adamw_many(*[[d[n] for n in group] for d in (w, g, m, v)], "adamw_small_4d" if per_layer else "adamw_small", per_layer)
        for d, o in zip((delta, new_m, new_v), outs):
            d.update(zip(group, o))

    return (loss, dx[None], *[g[n] for n in names], *[delta[n] for n in names],
            *[new_m[n] for n in names], *[new_v[n] for n in names])
```

```python
import functools
import math

import jax
import jax.numpy as jnp
from jax import lax
from jax.experimental import pallas as pl
from jax.experimental.pallas import tpu as pltpu

F32 = jnp.float32
BF16 = jnp.bfloat16

D_MODEL = 1024
N_LAYERS = 2
N_CHIPS = 4
IN_COLS = 3072
IN_SHARD = IN_COLS // N_CHIPS
SSM_WIDTH = 512
SSM_GROUP = 16
SSM_GROUPS = 32
SSM_STATE = 64
N_STATES = SSM_GROUPS * SSM_STATE
SSM_CHUNKS = 4
CH_W = SSM_WIDTH // SSM_CHUNKS
CH_S = N_STATES // SSM_CHUNKS
ATTN_WIDTH = 512
HEAD_DIM = 64
PLE_DIM = 256
ROW_SHARD = 256
RMS_EPS = 1e-6
ATTN_SCALE = HEAD_DIM ** -0.5
ATTN_BLOCK = 128
EXP_ZERO = -87.5
SUBLANES = 8
V7X_VMEM_LIMIT = 52 * 1024 * 1024

ADAM_LR = 0.001
ADAM_B1 = 0.9
ADAM_B2 = 0.999
ADAM_EPS = 1e-08
ADAM_WD = 0.01
ADAM_STEP = 10

MESH = pl.DeviceIdType.MESH
ANY = pl.BlockSpec(memory_space=pl.ANY)


def _cparams(n_grid=0, parallel=0):
    sem = tuple(["parallel"] * parallel + ["arbitrary"] * (n_grid - parallel))
    return pltpu.CompilerParams(dimension_semantics=sem, vmem_limit_bytes=V7X_VMEM_LIMIT)


def _dot(a, b):
    return jnp.dot(a, b, preferred_element_type=F32)


def _dot_nt(a, b):
    return lax.dot_general(a, b, (((1,), (1,)), ((), ())), preferred_element_type=F32)


def _dot_tn(a, b):
    return lax.dot_general(a, b, (((0,), (0,)), ((), ())), preferred_element_type=F32)


def _split_hilo(a):
    hi = a.astype(BF16)
    lo = (a - hi.astype(F32)).astype(BF16)
    return hi, lo


def _dot_hilo(a, b):
    hi, lo = _split_hilo(a)
    return _dot(hi, b) + _dot(lo, b)


def _sigmoid(x):
    return 0.5 * (jnp.tanh(0.5 * x) + 1.0)


_GELU_C = math.sqrt(2.0 / math.pi)


def _gelu(x):
    return 0.5 * x * (1.0 + jnp.tanh(_GELU_C * (x + 0.044715 * (x * x * x))))


def _gelu_grad(x):
    t = jnp.tanh(_GELU_C * (x + 0.044715 * (x * x * x)))
    return 0.5 * (1.0 + t) + 0.5 * x * (1.0 - t * t) * (_GELU_C * (1.0 + 3.0 * 0.044715 * (x * x)))


def _row_tile(s, want):
    for t in range(min(s, want), 7, -1):
        if s % t == 0 and t % SUBLANES == 0:
            return t
    return s


def _coords():
    return lax.axis_index("x"), lax.axis_index("y"), lax.axis_index("c")


def _other_chips(x, y):
    return [(1 - x, y), (x, 1 - y), (1 - x, 1 - y)]


def _remote(src, dst, send_sem, recv_sem, dev):
    return pltpu.make_async_remote_copy(src_ref=src, dst_ref=dst, send_sem=send_sem, recv_sem=recv_sem,
                                        device_id=dev, device_id_type=MESH)


def _set_block(buf, block, index):
    return lax.dynamic_update_index_in_dim(buf, block, index, 0)


def _gather_sems(n):
    return [pltpu.SemaphoreType.DMA((3 * n,)) for _ in range(4)]


def _gather_copies(ins, bases, outs, sems):
    send_sems, recv_sems, fwd_send, fwd_recv = sems
    x, y, c = _coords()
    me_chip = 2 * x + y
    sibling = (x, y, 1 - c)
    first, landed, passed, from_sibling = [], [], [], []
    for k in range(len(ins)):
        for j, (cx, cy) in enumerate(_other_chips(x, y)):
            i = 3 * k + j
            first.append(_remote(ins[k].at[bases[k] + c], outs[k].at[me_chip, c], send_sems.at[i], recv_sems.at[i], (cx, cy, c)))
            blk = outs[k].at[2 * cx + cy, c]
            landed.append(_remote(blk, blk, send_sems.at[i], recv_sems.at[i], (cx, cy, c)))
            passed.append(_remote(blk, blk, fwd_send.at[i], fwd_recv.at[i], sibling))
            blk = outs[k].at[2 * cx + cy, 1 - c]
            from_sibling.append(_remote(blk, blk, fwd_send.at[i], fwd_recv.at[i], sibling))
    return first, landed, passed, from_sibling


def _gather_start(ins, bases, outs, sems):
    for cp in _gather_copies(ins, bases, outs, sems)[0]:
        cp.start()


def _gather_finish(ins, bases, outs, sems):
    first, landed, passed, from_sibling = _gather_copies(ins, bases, outs, sems)
    for arrived, forward in zip(landed, passed):
        arrived.wait_recv()
        forward.start()
    for cp in from_sibling:
        cp.wait_recv()
    for cp in first + passed:
        cp.wait_send()


def _gather_outputs(arrs):
    return [jax.ShapeDtypeStruct((N_CHIPS, 2) + a.shape[1:], a.dtype) for a in arrs]


def _gather_own(outs, arrs, bases):
    me_chip = 2 * lax.axis_index("x") + lax.axis_index("y")
    return [_set_block(o, lax.slice_in_dim(a, b, b + 2, axis=0), me_chip) for o, a, b in zip(outs, arrs, bases)]


def _chip_gather(arrs, name, bases=None):
    n = len(arrs)
    bases = [0] * n if bases is None else bases

    def body(*refs):
        ins, outs, sems = refs[:n], refs[n:2 * n], refs[2 * n:]
        _gather_start(ins, bases, outs, sems)
        _gather_finish(ins, bases, outs, sems)

    outs = pl.pallas_call(
        body, name=name, out_shape=_gather_outputs(arrs),
        in_specs=[ANY] * n, out_specs=[ANY] * n, scratch_shapes=_gather_sems(n),
    )(*arrs)
    return _gather_own(outs, arrs, bases)


def _sibling_push(arrs, owners, name):
    n = len(arrs)

    def body(*refs):
        ins, outs = refs[:n], refs[n:2 * n]
        send_sems, recv_sems = refs[2 * n:]
        x, y, c = _coords()
        cps = [_remote(ins[k].at[1 - c] if owners[k] is None else ins[k], outs[k], send_sems.at[k], recv_sems.at[k],
                       (x, y, 1 - c)) for k in range(n)]
        for o in (None, 0, 1):
            mine = [cp for cp, ow in zip(cps, owners) if ow == o]
            if not mine:
                continue
            if o is None:
                for cp in mine:
                    cp.start()
                for cp in mine:
                    cp.wait_recv()
                for cp in mine:
                    cp.wait_send()
            else:
                @pl.when(c == 1 - o)
                def _():
                    for cp in mine:
                        cp.start()
                    for cp in mine:
                        cp.wait_send()

                @pl.when(c == o)
                def _():
                    for cp in mine:
                        cp.wait_recv()

    return pl.pallas_call(
        body, name=name,
        out_shape=[jax.ShapeDtypeStruct(a.shape[1:] if ow is None else a.shape, a.dtype) for a, ow in zip(arrs, owners)],
        in_specs=[ANY] * n, out_specs=[ANY] * n,
        scratch_shapes=[pltpu.SemaphoreType.DMA((n,)), pltpu.SemaphoreType.DMA((n,))],
    )(*arrs)


def _sibling_join(tot0, tot1, sym, name):
    nb, ns = len(tot0), len(sym)
    n = nb + ns

    def body(*refs):
        ins0, ins1, ins_s = refs[:nb], refs[nb:2 * nb], refs[2 * nb:2 * nb + ns]
        outs_b, outs_s = refs[2 * nb + ns:3 * nb + ns], refs[3 * nb + ns:3 * nb + 2 * ns]
        send_sems, recv_sems = refs[3 * nb + 2 * ns:]
        x, y, c = _coords()
        sibling = (x, y, 1 - c)

        def big(src):
            return [_remote(src[k], outs_b[k], send_sems.at[k], recv_sems.at[k], sibling) for k in range(nb)]

        @pl.when(c == 0)
        def _():
            for cp in big(ins0):
                cp.start()

        @pl.when(c == 1)
        def _():
            for cp in big(ins1):
                cp.start()

        halves = [_remote(ins_s[k], outs_s[k].at[c], send_sems.at[nb + k], recv_sems.at[nb + k], sibling) for k in range(ns)]
        for cp in halves:
            cp.start()
        for cp in big(ins0):
            cp.wait_recv()
        for k in range(ns):
            blk = outs_s[k].at[1 - c]
            _remote(blk, blk, send_sems.at[nb + k], recv_sems.at[nb + k], sibling).wait_recv()
        for cp in big(ins0) + halves:
            cp.wait_send()

    outs = pl.pallas_call(
        body, name=name,
        out_shape=[jax.ShapeDtypeStruct(a.shape, a.dtype) for a in tot0]
        + [jax.ShapeDtypeStruct((2,) + a.shape, a.dtype) for a in sym],
        in_specs=[ANY] * (2 * nb + ns), out_specs=[ANY] * n,
        scratch_shapes=[pltpu.SemaphoreType.DMA((n,)), pltpu.SemaphoreType.DMA((n,))],
    )(*tot0, *tot1, *sym)
    c = lax.axis_index("c")
    return outs[:nb], [_set_block(o, a, c) for o, a in zip(outs[nb:], sym)]


def _scatter_sems(n):
    return [pltpu.SemaphoreType.DMA((3 * n,)), pltpu.SemaphoreType.DMA((3 * n,))]


def _scatter_copies(ins, outs, sems):
    send_sems, recv_sems = sems
    x, y, c = _coords()
    me_chip = 2 * x + y
    sends, arrivals = [], []
    for k in range(len(ins)):
        for j, (cx, cy) in enumerate(_other_chips(x, y)):
            i = 3 * k + j
            sends.append(_remote(ins[k].at[2 * cx + cy], outs[k].at[me_chip], send_sems.at[i], recv_sems.at[i], (cx, cy, c)))
            blk = outs[k].at[2 * cx + cy]
            arrivals.append(_remote(blk, blk, send_sems.at[i], recv_sems.at[i], (cx, cy, c)))
    return sends, arrivals


def _by_owner(owners, fn):
    c = lax.axis_index("c")
    for o in (None, 0, 1):
        idx = [k for k, ow in enumerate(owners) if ow == o]
        if not idx:
            continue
        if o is None:
            fn(idx)
        else:
            pl.when(c == o)(functools.partial(fn, idx))


def _scatter_start(ins, outs, owners, sems):
    sends, _ = _scatter_copies(ins, outs, sems)

    def go(idx):
        for k in idx:
            for cp in sends[3 * k:3 * k + 3]:
                cp.start()

    _by_owner(owners, go)


def _scatter_finish(ins, outs, owners, sems):
    sends, arrivals = _scatter_copies(ins, outs, sems)

    def go(idx):
        for k in idx:
            for cp in arrivals[3 * k:3 * k + 3]:
                cp.wait_recv()
        for k in idx:
            for cp in sends[3 * k:3 * k + 3]:
                cp.wait_send()

    _by_owner(owners, go)


def _scatter_own(outs, arrs):
    me_chip = 2 * lax.axis_index("x") + lax.axis_index("y")
    return [_set_block(o, lax.dynamic_index_in_dim(a, me_chip, 0, keepdims=False), me_chip) for o, a in zip(outs, arrs)]


def _chip_scatter(arrs, owners, name):
    n = len(arrs)

    def body(*refs):
        ins, outs, sems = refs[:n], refs[n:2 * n], refs[2 * n:]
        _scatter_start(ins, outs, owners, sems)
        _scatter_finish(ins, outs, owners, sems)

    outs = pl.pallas_call(
        body, name=name,
        out_shape=[jax.ShapeDtypeStruct(a.shape, a.dtype) for a in arrs],
        in_specs=[ANY] * n, out_specs=[ANY] * n, scratch_shapes=_scatter_sems(n),
    )(*arrs)
    return _scatter_own(outs, arrs)


def _as_rows(a, lanes):
    return a.reshape(-1, lanes)


def _add_half(full, recv, out_dtype, name):
    _, r, cdim = full.shape
    tr = _row_tile(r, 512)

    def body(c_ref, a_ref, b_ref, o_ref):
        o_ref[...] = (a_ref[...].astype(F32) + b_ref[...].astype(F32)).astype(out_dtype)

    c = lax.axis_index("c").astype(jnp.int32).reshape(1)
    return pl.pallas_call(
        body, name=name,
        grid_spec=pltpu.PrefetchScalarGridSpec(
            num_scalar_prefetch=1, grid=(r // tr,),
            in_specs=[pl.BlockSpec((None, tr, cdim), lambda i, c_ref: (c_ref[0], i, 0)),
                      pl.BlockSpec((tr, cdim), lambda i, c_ref: (i, 0))],
            out_specs=pl.BlockSpec((tr, cdim), lambda i, c_ref: (i, 0))),
        out_shape=jax.ShapeDtypeStruct((r, cdim), out_dtype),
        compiler_params=_cparams(1),
    )(c, full, recv)


def _owner_flag(owner):
    if owner is None:
        return jnp.ones((1,), jnp.int32)
    return (lax.axis_index("c") == owner).astype(jnp.int32).reshape(1)


def _add_pair(a, b, owner, name):
    r, cdim = a.shape
    tr = _row_tile(r, 512)

    def body(f_ref, a_ref, b_ref, o_ref):
        o_ref[...] = (a_ref[...].astype(F32) + b_ref[...].astype(F32)).astype(o_ref.dtype)

    spec = pl.BlockSpec((tr, cdim), lambda i, f_ref: (i * f_ref[0], 0))
    return pl.pallas_call(
        body, name=name,
        grid_spec=pltpu.PrefetchScalarGridSpec(num_scalar_prefetch=1, grid=(r // tr,), in_specs=[spec] * 2, out_specs=spec),
        out_shape=jax.ShapeDtypeStruct((r, cdim), a.dtype), compiler_params=_cparams(1))(_owner_flag(owner), a, b)


def _sum4(parts, owner, name):
    _, r, cdim = parts.shape
    tr = _row_tile(r, 512)

    def body(f_ref, p_ref, o_ref):
        acc = p_ref[0].astype(F32) + p_ref[1].astype(F32)
        acc = acc + p_ref[2].astype(F32)
        o_ref[...] = acc + p_ref[3].astype(F32)

    return pl.pallas_call(
        body, name=name,
        grid_spec=pltpu.PrefetchScalarGridSpec(
            num_scalar_prefetch=1, grid=(r // tr,),
            in_specs=[pl.BlockSpec((N_CHIPS, tr, cdim), lambda i, f_ref: (0, i * f_ref[0], 0))],
            out_specs=pl.BlockSpec((tr, cdim), lambda i, f_ref: (i * f_ref[0], 0))),
        out_shape=jax.ShapeDtypeStruct((r, cdim), F32),
        compiler_params=_cparams(1),
    )(_owner_flag(owner), parts)


def _adamw_math(w, g, m, v):
    c1 = 1.0 - ADAM_B1 ** ADAM_STEP
    c2 = 1.0 - ADAM_B2 ** ADAM_STEP
    nm = ADAM_B1 * m + (1.0 - ADAM_B1) * g
    nv = ADAM_B2 * v + (1.0 - ADAM_B2) * (g * g)
    delta = -ADAM_LR * ((nm / c1) / (jnp.sqrt(nv / c2) + ADAM_EPS) + ADAM_WD * w)
    return delta, nm, nv


def _adamw(w, g, m, v, name):
    r, cdim = w.shape
    tr = _row_tile(r, 256)

    def body(w_ref, g_ref, m_ref, v_ref, d_ref, nm_ref, nv_ref):
        d_ref[...], nm_ref[...], nv_ref[...] = _adamw_math(w_ref[...], g_ref[...], m_ref[...], v_ref[...])

    spec = pl.BlockSpec((tr, cdim), lambda i: (i, 0))
    return pl.pallas_call(
        body, name=name, grid=(r // tr,),
        in_specs=[spec] * 4, out_specs=[spec] * 3,
        out_shape=[jax.ShapeDtypeStruct((r, cdim), F32)] * 3,
        compiler_params=_cparams(1),
    )(w, g, m, v)


def _adamw_many(ws, gs, ms, vs, name, per_layer):
    n = len(ws)

    def body(*refs):
        for k in range(n):
            w, g, m, v = (refs[j * n + k][...] for j in range(4))
            outs = _adamw_math(w, g, m, v)
            for j in range(3):
                refs[(4 + j) * n + k][...] = outs[j]

    shapes = [jax.ShapeDtypeStruct(w.shape, F32) for w in ws]
    if per_layer:
        specs = [pl.BlockSpec((None,) + w.shape[1:], lambda l, nd=w.ndim: (l,) + (0,) * (nd - 1)) for w in ws]
        call = pl.pallas_call(body, name=name, grid=(N_LAYERS,), in_specs=specs * 4, out_specs=specs * 3,
                              out_shape=shapes * 3, compiler_params=_cparams(1))
    else:
        call = pl.pallas_call(body, name=name, out_shape=shapes * 3, compiler_params=_cparams())
    outs = call(*ws, *gs, *ms, *vs)
    return outs[0:n], outs[n:2 * n], outs[2 * n:3 * n]


def _discretise(a_re, a_im, log_dt, b_re, b_im):
    dt = jnp.exp(log_dt)
    mag = jnp.exp(a_re * dt)
    ab_re = mag * jnp.cos(a_im * dt)
    ab_im = mag * jnp.sin(a_im * dt)
    num_re = ab_re - 1.0
    num_im = ab_im
    den = a_re * a_re + a_im * a_im
    f_re = (num_re * a_re + num_im * a_im) / den
    f_im = (num_im * a_re - num_re * a_im) / den
    bb_re = f_re * b_re - f_im * b_im
    bb_im = f_re * b_im + f_im * b_re
    return ab_re, ab_im, bb_re, bb_im


def _disc_shapes():
    col = jax.ShapeDtypeStruct((1, N_STATES), F32)
    mat = jax.ShapeDtypeStruct((SSM_GROUP, N_STATES), F32)
    return col, mat


def _disc_fwd(a_re, a_im, log_dt, b_re, b_im):
    col, mat = _disc_shapes()

    def body(ar, ai, ld, br, bi, o0, o1, o2, o3):
        outs = _discretise(ar[...], ai[...], ld[...], br[...], bi[...])
        for o, val in zip((o0, o1, o2, o3), outs):
            o[...] = val

    return pl.pallas_call(body, name="ssm_discretise", out_shape=[col, col, mat, mat],
                          compiler_params=_cparams())(a_re, a_im, log_dt, b_re, b_im)


def _disc_bwd(a_re, a_im, log_dt, b_re, b_im, g_ab_re, g_ab_im, g_bb_re, g_bb_im):
    col, mat = _disc_shapes()

    def body(ar, ai, ld, br, bi, g0, g1, g2, g3, o0, o1, o2, o3, o4):
        _, vjp = jax.vjp(_discretise, ar[...], ai[...], ld[...], br[...], bi[...])
        grads = vjp((g0[...], g1[...], g2[...], g3[...]))
        for o, val in zip((o0, o1, o2, o3, o4), grads):
            o[...] = val

    return pl.pallas_call(body, name="ssm_discretise_bwd", out_shape=[col, col, col, mat, mat],
                          compiler_params=_cparams())(a_re, a_im, log_dt, b_re, b_im, g_ab_re, g_ab_im, g_bb_re, g_bb_im)


def _cmul(ar, ai, br, bi):
    return ar * br - ai * bi, ar * bi + ai * br


def _scan_coefs(ab_re, ab_im, reverse):
    ar = ab_re.reshape(1, N_STATES)
    ai = ab_im.reshape(1, N_STATES)
    if reverse:
        ai = -ai
    a2 = _cmul(ar, ai, ar, ai)
    a4 = _cmul(*a2, *a2)
    rows = jnp.arange(SUBLANES)[:, None]
    out = []
    for (pr, pi), sh in (((ar, ai), 1), (a2, 2), (a4, 4)):
        keep = (rows <= SUBLANES - 1 - sh) if reverse else (rows >= sh)
        out += [jnp.where(keep, pr, 0.0), jnp.where(keep, pi, 0.0)]
    pows = [(ar, ai)]
    for _ in range(SUBLANES - 1):
        pows.append(_cmul(*pows[-1], ar, ai))
    order = pows[::-1] if reverse else pows
    out += [jnp.concatenate([p[0] for p in order], 0), jnp.concatenate([p[1] for p in order], 0)]
    t = jnp.stack(out, 0)
    return t.reshape(8, SUBLANES, SSM_CHUNKS, CH_S).transpose(2, 0, 1, 3)


def _block_diag_in(bb):
    t = bb.reshape(SSM_GROUP, SSM_CHUNKS, 8, SSM_STATE)
    eye = jnp.eye(8, dtype=bb.dtype)
    return jnp.einsum("hjgp,gk->jghkp", t, eye).reshape(SSM_CHUNKS, CH_W, CH_S)


def _block_diag_in_t(d):
    t = d.reshape(SSM_CHUNKS, 8, SSM_GROUP, 8, SSM_STATE)
    return jnp.einsum("jghgp->hjgp", t).reshape(SSM_GROUP, N_STATES)


def _block_diag_out(c):
    t = c.reshape(SSM_CHUNKS, 8, SSM_GROUP, SSM_STATE)
    eye = jnp.eye(8, dtype=c.dtype)
    return jnp.einsum("jghp,gk->jgpkh", t, eye).reshape(SSM_CHUNKS, CH_S, CH_W)


def _block_diag_out_t(d):
    t = d.reshape(SSM_CHUNKS, 8, SSM_STATE, 8, SSM_GROUP)
    return jnp.einsum("jgpgh->jghp", t).reshape(SSM_GROUPS, SSM_GROUP, SSM_STATE)


def _head_ones():
    r = jnp.arange(ATTN_WIDTH) // HEAD_DIM
    return jnp.where(r[:, None] == r[None, :], 1.0 / HEAD_DIM, 0.0).astype(BF16)


def _in_proj(h, g1, w_in_l, qg, kg):
    s = h.shape[0]
    tm = _row_tile(s, 256)

    def body(h_ref, g_ref, w_ref, qg_ref, kg_ref, ones_ref, proj_ref, qkv_ref):
        x = h_ref[...]
        r = lax.rsqrt(jnp.mean(x * x, axis=-1, keepdims=True) + RMS_EPS)
        hn = (x * r * g_ref[...]).astype(BF16)
        for sh in range(N_CHIPS):
            proj_ref[:, IN_SHARD * sh:IN_SHARD * (sh + 1)] = _dot(hn, w_ref[sh])
        ones = ones_ref[...]
        q = proj_ref[:, 1024:1536]
        k = proj_ref[:, 1536:2048]
        rq = lax.rsqrt(_dot_hilo(q * q, ones) + RMS_EPS)
        rk = lax.rsqrt(_dot_hilo(k * k, ones) + RMS_EPS)
        qkv_ref[:, 0:512] = (q * rq * qg_ref[...] * ATTN_SCALE).astype(BF16)
        qkv_ref[:, 512:1024] = (k * rk * kg_ref[...]).astype(BF16)
        qkv_ref[:, 1024:1536] = proj_ref[:, 2048:2560].astype(BF16)

    full = lambda shape: pl.BlockSpec(shape, lambda i: (0,) * len(shape))
    return pl.pallas_call(
        body, name="in_proj", grid=(s // tm,),
        in_specs=[pl.BlockSpec((tm, D_MODEL), lambda i: (i, 0)), full((1, D_MODEL)),
                  full((N_CHIPS, D_MODEL, IN_SHARD)),
                  full((1, ATTN_WIDTH)), full((1, ATTN_WIDTH)), full((ATTN_WIDTH, ATTN_WIDTH))],
        out_specs=[pl.BlockSpec((tm, IN_COLS), lambda i: (i, 0)), pl.BlockSpec((tm, 3 * ATTN_WIDTH), lambda i: (i, 0))],
        out_shape=[jax.ShapeDtypeStruct((s, IN_COLS), F32), jax.ShapeDtypeStruct((s, 3 * ATTN_WIDTH), BF16)],
        compiler_params=_cparams(1),
    )(h, g1, w_in_l, qg, kg, _head_ones())


def _scan_rows(x_ref, coef_ref, carry_ref, n_blocks, reverse, extra=None):
    c = [coef_ref[a] for a in range(8)]
    shifts = (7, 6, 4) if reverse else (1, 2, 4)
    edge = 0 if reverse else SUBLANES - 1

    def blk(b, carry):
        bb = (n_blocks - 1 - b) if reverse else b
        r0 = pl.multiple_of(bb * SUBLANES, SUBLANES)
        xr = x_ref[pl.ds(r0, SUBLANES), 0:CH_S]
        xi = x_ref[pl.ds(r0, SUBLANES), CH_S:2 * CH_S]
        for lvl, sh in enumerate(shifts):
            ar, ai = c[2 * lvl], c[2 * lvl + 1]
            sr = pltpu.roll(xr, sh, 0)
            si = pltpu.roll(xi, sh, 0)
            xr, xi = xr + (ar * sr - ai * si), xi + (ar * si + ai * sr)
        cr, ci = carry
        xr, xi = xr + (c[6] * cr - c[7] * ci), xi + (c[6] * ci + c[7] * cr)
        x_ref[pl.ds(r0, SUBLANES), 0:CH_S] = xr
        x_ref[pl.ds(r0, SUBLANES), CH_S:2 * CH_S] = xi
        if extra is not None:
            extra(r0, xr, xi, cr, ci)
        return (jnp.broadcast_to(xr[edge:edge + 1, :], (SUBLANES, CH_S)),
                jnp.broadcast_to(xi[edge:edge + 1, :], (SUBLANES, CH_S)))

    cr, ci = lax.fori_loop(0, n_blocks, blk, (carry_ref[:, 0:CH_S], carry_ref[:, CH_S:2 * CH_S]))
    carry_ref[:, 0:CH_S] = cr
    carry_ref[:, CH_S:2 * CH_S] = ci


def _ssm_scan_fwd(proj, wb, coef, wc, gather=None, gather_bases=None):
    s = proj.shape[0]
    tm = _row_tile(s, 512)
    nt = s // tm
    gather = [] if gather is None else gather
    ng = len(gather)

    def body(*refs):
        u_ref, wb_ref, coef_ref, wc_ref = refs[0:4]
        g_ins = refs[4:4 + ng]
        xs_ref, y_ref = refs[4 + ng:6 + ng]
        g_outs = refs[6 + ng:6 + 2 * ng]
        carry_ref = refs[6 + 2 * ng]
        sems = refs[7 + 2 * ng:]
        j, i = pl.program_id(0), pl.program_id(1)

        @pl.when(i == 0)
        def _():
            carry_ref[...] = jnp.zeros_like(carry_ref)

        if ng:
            @pl.when(jnp.logical_and(j == 0, i == 0))
            def _():
                _gather_start(g_ins, gather_bases, g_outs, sems)

        xs_ref[...] = _dot(u_ref[...].astype(BF16), wb_ref[...])
        _scan_rows(xs_ref, coef_ref, carry_ref, tm // SUBLANES, reverse=False)
        y_ref[...] = _dot(xs_ref[...].astype(BF16), wc_ref[...])

        if ng:
            @pl.when(jnp.logical_and(j == SSM_CHUNKS - 1, i == nt - 1))
            def _():
                _gather_finish(g_ins, gather_bases, g_outs, sems)

    outs = pl.pallas_call(
        body, name="ssm_scan_gather" if ng else "ssm_scan", grid=(SSM_CHUNKS, nt),
        in_specs=[pl.BlockSpec((tm, CH_W), lambda j, i: (i, j)),
                  pl.BlockSpec((None, CH_W, 2 * CH_S), lambda j, i: (j, 0, 0)),
                  pl.BlockSpec((None, 8, SUBLANES, CH_S), lambda j, i: (j, 0, 0, 0)),
                  pl.BlockSpec((None, 2 * CH_S, CH_W), lambda j, i: (j, 0, 0))] + [ANY] * ng,
        out_specs=[pl.BlockSpec((None, tm, 2 * CH_S), lambda j, i: (j, i, 0)),
                   pl.BlockSpec((tm, CH_W), lambda j, i: (i, j))] + [ANY] * ng,
        out_shape=[jax.ShapeDtypeStruct((SSM_CHUNKS, s, 2 * CH_S), F32), jax.ShapeDtypeStruct((s, SSM_WIDTH), F32)]
        + _gather_outputs(gather),
        scratch_shapes=[pltpu.VMEM((SUBLANES, 2 * CH_S), F32)] + (_gather_sems(ng) if ng else []),
        compiler_params=_cparams(2),
    )(proj, wb, coef, wc, *gather)
    return outs[0], outs[1], (_gather_own(outs[2:], gather, gather_bases) if ng else [])


def _glu_forward(y, u, d, wg_ref, bg):
    yf = y + d * u
    z = _gelu(yf)
    zb = z.astype(BF16)
    zz = jnp.concatenate([_dot(zb, wg_ref[sh]) for sh in range(N_CHIPS)], axis=-1) + bg
    return yf, z, zz[:, 0:SSM_WIDTH], zz[:, SSM_WIDTH:2 * SSM_WIDTH]


def _ssm_glu_fwd(y, proj, d, w_glu_all, layer, b_glu):
    s = y.shape[0]
    tm = _row_tile(s, 512)

    def body(y_ref, u_ref, gs_ref, d_ref, wg_ref, bg_ref, o_ref):
        _, _, val, gate = _glu_forward(y_ref[...], u_ref[...], d_ref[...], wg_ref, bg_ref[...])
        gs = gs_ref[...]
        o_ref[...] = val * _sigmoid(gate) * (gs * _sigmoid(gs))

    row = lambda i: (i, 0)
    return pl.pallas_call(
        body, name="ssm_glu", grid=(s // tm,),
        in_specs=[pl.BlockSpec((tm, SSM_WIDTH), row), pl.BlockSpec((tm, SSM_WIDTH), row),
                  pl.BlockSpec((tm, SSM_WIDTH), lambda i: (i, 1)), pl.BlockSpec((1, SSM_WIDTH), lambda i: (0, 0)),
                  pl.BlockSpec((N_CHIPS, None, SSM_WIDTH, ROW_SHARD), lambda i: (0, layer, 0, 0)),
                  pl.BlockSpec((1, 2 * SSM_WIDTH), lambda i: (0, 0))],
        out_specs=pl.BlockSpec((tm, SSM_WIDTH), row),
        out_shape=jax.ShapeDtypeStruct((s, SSM_WIDTH), F32),
        compiler_params=_cparams(1),
    )(y, proj, proj, d, w_glu_all, b_glu)


def _tri(kind):
    r = jnp.arange(ATTN_BLOCK)
    if kind == "suffix_incl":
        m = r[:, None] >= r[None, :]
    else:
        m = r[:, None] < r[None, :]
    return jnp.concatenate([m, jnp.ones_like(m)], axis=1).astype(BF16)


def _head_masks():
    lane = lax.broadcasted_iota(jnp.int32, (1, 2 * HEAD_DIM), 1)
    return [lane < HEAD_DIM, lane >= HEAD_DIM]


def _chain_step(t, base, n_sub, first, q_ref, k_ref, tri_ref, l_scr, per_chain):
    tb = ATTN_BLOCK
    row = lax.broadcasted_iota(jnp.int32, (tb, tb), 0)
    col = lax.broadcasted_iota(jnp.int32, (tb, tb), 1)
    masks = _head_masks()
    blks = [base + a - t for a in range(n_sub)]
    r0s = [pl.multiple_of(jnp.maximum(blk, 0) * tb, tb) for blk in blks]
    zs = []
    for a in range(n_sub):
        kb = k_ref[pl.ds(r0s[a], tb), :]
        qa = q_ref[a * tb:(a + 1) * tb, :]
        for mask in masks:
            zs.append(_dot_nt(jnp.where(mask, qa, jnp.zeros_like(qa)), kb))
    parts = []
    for z in zs:
        ls = jnp.minimum(-z, 0.0) - jnp.log(1.0 + jnp.exp(-jnp.abs(z)))
        if first:
            ls = jnp.where(col < row, ls, 0.0)
        parts.append(_split_hilo(ls))
    tri = tri_ref[...]
    sums = [_dot(hi, tri) + _dot(lo, tri) for hi, lo in parts]
    top = None
    ws = []
    for c, (z, sm) in enumerate(zip(zs, sums)):
        if first:
            lsum = jnp.zeros((tb, tb), F32)
        else:
            lsum = l_scr[c] + jnp.where(blks[c // 2] >= 0, 0.0, -1e30)
        w = jnp.exp(z + sm[:, 0:tb] + lsum)
        if first:
            w = jnp.where(col < row, w, 0.0)
        ws.append(w)
        lsum = lsum + sm[:, tb:2 * tb]
        l_scr[c] = lsum
        top = lsum if top is None else jnp.maximum(top, lsum)
    for c, (z, w) in enumerate(zip(zs, ws)):
        per_chain(c // 2, c % 2, c, r0s[c // 2], z, w)
    return jnp.max(top)


def _chain_sweep(base, n_sub, q_ref, k_ref, tri_ref, l_scr, per_chain):
    top = _chain_step(0, base, n_sub, True, q_ref, k_ref, tri_ref, l_scr, functools.partial(per_chain, 0))

    def cond(carry):
        t, top = carry
        return jnp.logical_and(t <= base + n_sub - 1, top > EXP_ZERO)

    def step(carry):
        t, _ = carry
        return t + 1, _chain_step(t, base, n_sub, False, q_ref, k_ref, tri_ref, l_scr, functools.partial(per_chain, t))

    steps, _ = lax.while_loop(cond, step, (jnp.int32(1), top))
    return steps


ATTN_SUB_FWD = 4
ATTN_SUB_BWD = 4


def _attn_fwd(qkv, proj):
    s = qkv.shape[0]
    tb = ATTN_BLOCK
    n_sub = min(ATTN_SUB_FWD, s // tb)
    tq = n_sub * tb

    def body(q_ref, k_ref, v_ref, g_ref, tri_ref, o_ref, ya_ref, l_scr):
        i = pl.program_id(1)
        masks = _head_masks()
        o_ref[...] = jnp.zeros_like(o_ref)

        def per_chain(t, a, h, c, r0, z, w):
            vb = v_ref[pl.ds(r0, tb), :]
            vb = jnp.where(masks[h], vb, jnp.zeros_like(vb))
            o_ref[a * tb:(a + 1) * tb, :] += _dot(w.astype(BF16), vb)

        _chain_sweep(i * n_sub, n_sub, q_ref, k_ref, tri_ref, l_scr, per_chain)
        g = g_ref[...]
        ya_ref[...] = o_ref[...] * (g * _sigmoid(g))

    hp_blk = lambda off: pl.BlockSpec((tq, 2 * HEAD_DIM), lambda hp, i: (i, off + hp))
    res = lambda off: pl.BlockSpec((s, 2 * HEAD_DIM), lambda hp, i: (0, off + hp))
    return pl.pallas_call(
        body, name="attn_fwd", grid=(ATTN_WIDTH // (2 * HEAD_DIM), s // tq),
        in_specs=[hp_blk(0), res(4), res(8), hp_blk(20), pl.BlockSpec((tb, 2 * tb), lambda hp, i: (0, 0))],
        out_specs=[hp_blk(0), hp_blk(0)],
        out_shape=[jax.ShapeDtypeStruct((s, ATTN_WIDTH), F32)] * 2,
        scratch_shapes=[pltpu.VMEM((2 * n_sub, tb, tb), F32)],
        compiler_params=_cparams(2),
    )(qkv, qkv, qkv, proj, _tri("suffix_incl"))


def _rms_rows(x, g):
    r = lax.rsqrt(jnp.mean(x * x, axis=-1, keepdims=True) + RMS_EPS)
    return r, x * r * g


def _ple_forward(h1, p, g2, wpg_ref, wpp_ref):
    r2, hn2 = _rms_rows(h1, g2)
    hb = hn2.astype(BF16)
    gpre = _dot(hb[:, 0:ROW_SHARD], wpg_ref[0])
    for sh in range(1, N_CHIPS):
        gpre = gpre + _dot(hb[:, ROW_SHARD * sh:ROW_SHARD * (sh + 1)], wpg_ref[sh])
    gate = _sigmoid(gpre)
    pb = p.astype(BF16)
    pp = jnp.concatenate([_dot(pb, wpp_ref[sh]) for sh in range(N_CHIPS)], axis=-1)
    return r2, hb, gate, pp


def _colsum8(a):
    t = a.shape[0]
    return a.reshape(t // SUBLANES, SUBLANES, a.shape[1]).sum(axis=0)


def _sq_err_grad(y, target):
    e = y - target
    sq = _colsum8(e * e)
    part = sq[:, 0:128]
    for b in range(1, D_MODEL // 128):
        part = part + sq[:, 128 * b:128 * (b + 1)]
    return e / D_MODEL, part


def _out_ple(h, ys, ya, p, g2, w_out_all, w_pg_all, w_pp_all, layer, target=None):
    s = h.shape[0]
    tm = _row_tile(s, 256)
    last = target is not None

    def body(*refs):
        h_ref, ys_ref, ya_ref, p_ref, g_ref, wo_ref, wpg_ref, wpp_ref = refs[0:8]
        h1_ref, h2_ref = refs[8 + last], refs[9 + last]
        ysb = ys_ref[...].astype(BF16)
        yab = ya_ref[...].astype(BF16)
        h1 = h_ref[...]
        for sh, src in enumerate((ysb[:, 0:ROW_SHARD], ysb[:, ROW_SHARD:], yab[:, 0:ROW_SHARD], yab[:, ROW_SHARD:])):
            h1 = h1 + _dot(src, wo_ref[sh])
        _, _, gate, pp = _ple_forward(h1, p_ref[...], g_ref[...], wpg_ref, wpp_ref)
        h1_ref[...] = h1
        h2 = h1 + gate * pp
        if last:
            acc_ref = refs[11]

            @pl.when(pl.program_id(0) == 0)
            def _():
                acc_ref[...] = jnp.zeros_like(acc_ref)

            h2_ref[...], part = _sq_err_grad(h2, refs[8][...])
            acc_ref[...] += part
        else:
            h2_ref[...] = h2

    row = lambda i: (i, 0)
    big = pl.BlockSpec((tm, D_MODEL), row)
    wspec = lambda r, cdim: pl.BlockSpec((N_CHIPS, None, r, cdim), lambda i: (0, layer, 0, 0))
    acc = pl.BlockSpec((SUBLANES, 128), lambda i: (0, 0))
    return pl.pallas_call(
        body, name="out_ple_loss" if last else "out_ple", grid=(s // tm,),
        in_specs=[big, pl.BlockSpec((tm, SSM_WIDTH), row), pl.BlockSpec((tm, ATTN_WIDTH), row),
                  pl.BlockSpec((tm, PLE_DIM), row), pl.BlockSpec((1, D_MODEL), lambda i: (0, 0)),
                  wspec(ROW_SHARD, D_MODEL), wspec(ROW_SHARD, D_MODEL), wspec(PLE_DIM, ROW_SHARD)] + [big] * last,
        out_specs=[big] * 2 + [acc] * last,
        out_shape=[jax.ShapeDtypeStruct((s, D_MODEL), F32)] * 2 + [jax.ShapeDtypeStruct((SUBLANES, 128), F32)] * last,
        compiler_params=_cparams(1),
    )(h, ys, ya, p, g2, w_out_all, w_pg_all, w_pp_all, *([target] if last else []))


def _rms_bwd(x, r, g, dy):
    gdy = g * dy
    dx = r * gdy - x * (r * r * r) * jnp.mean(x * gdy, axis=-1, keepdims=True)
    return dx, x * r * dy


def _out_ple_bwd(dh2, h1, p, g2, w_out_all, w_pg_all, w_pp_all, layer):
    s = h1.shape[0]
    tm = _row_tile(s, 256)

    def body(dh2_ref, h1_ref, p_ref, g_ref, wo_ref, wpg_ref, wpp_ref,
             dh1_ref, dmix_ref, hn_ref, dgp_ref, dpp_ref, dh1b_ref, dg_ref):
        @pl.when(pl.program_id(0) == 0)
        def _():
            dg_ref[...] = jnp.zeros_like(dg_ref)

        h1 = h1_ref[...]
        dh2 = dh2_ref[...]
        g2v = g_ref[...]
        r2, hb, gate, pp = _ple_forward(h1, p_ref[...], g2v, wpg_ref, wpp_ref)
        dgp = (dh2 * pp) * gate * (1.0 - gate)
        dgpb = dgp.astype(BF16)
        dhn = jnp.concatenate([_dot_nt(dgpb, wpg_ref[sh]) for sh in range(N_CHIPS)], axis=-1)
        dx, dgrow = _rms_bwd(h1, r2, g2v, dhn)
        dh1 = dh2 + dx
        dh1b = dh1.astype(BF16)
        dh1_ref[...] = dh1
        dh1b_ref[...] = dh1b
        hn_ref[...] = hb
        dgp_ref[...] = dgpb
        dpp_ref[...] = (dh2 * gate).astype(BF16)
        dg_ref[...] += _colsum8(dgrow)
        for sh in range(N_CHIPS):
            dmix_ref[:, ROW_SHARD * sh:ROW_SHARD * (sh + 1)] = _dot_nt(dh1b, wo_ref[sh])

    row = lambda i: (i, 0)
    wspec = lambda r, cdim: pl.BlockSpec((N_CHIPS, None, r, cdim), lambda i: (0, layer, 0, 0))
    big = pl.BlockSpec((tm, D_MODEL), row)
    return pl.pallas_call(
        body, name="out_ple_bwd", grid=(s // tm,),
        in_specs=[big, big, pl.BlockSpec((tm, PLE_DIM), row), pl.BlockSpec((1, D_MODEL), lambda i: (0, 0)),
                  wspec(ROW_SHARD, D_MODEL), wspec(ROW_SHARD, D_MODEL), wspec(PLE_DIM, ROW_SHARD)],
        out_specs=[big] * 6 + [pl.BlockSpec((SUBLANES, D_MODEL), lambda i: (0, 0))],
        out_shape=[jax.ShapeDtypeStruct((s, D_MODEL), F32)] * 2 + [jax.ShapeDtypeStruct((s, D_MODEL), BF16)] * 4
        + [jax.ShapeDtypeStruct((SUBLANES, D_MODEL), F32)],
        compiler_params=_cparams(1),
    )(dh2, h1, p, g2, w_out_all, w_pg_all, w_pp_all)


def _tn_matmul(a, b, n_blocks, block_a, name, into=None, first_block=0, total_blocks=None):
    s = a.shape[0]
    tk = _row_tile(s, 512)
    nk = s // tk
    total_blocks = n_blocks if total_blocks is None else total_blocks
    ka, nb = a.shape[1], b.shape[1]
    if block_a:
        ka //= n_blocks
    else:
        nb //= n_blocks

    def body(*refs):
        a_ref, b_ref, o_ref, acc_ref = refs[0], refs[1], refs[-2], refs[-1]

        @pl.when(pl.program_id(0) == 0)
        def _():
            acc_ref[...] = jnp.zeros_like(acc_ref)

        at = a_ref[...].astype(BF16).T
        bb = b_ref[...].astype(BF16)
        for sh in range(n_blocks):
            if block_a:
                acc_ref[sh] += _dot(at[ka * sh:ka * (sh + 1), :], bb)
            else:
                acc_ref[sh] += _dot(at, bb[:, nb * sh:nb * (sh + 1)])

        @pl.when(pl.program_id(0) == nk - 1)
        def _():
            o_ref[...] = acc_ref[...].astype(BF16)

    in_specs = [pl.BlockSpec((tk, a.shape[1]), lambda i: (i, 0)), pl.BlockSpec((tk, b.shape[1]), lambda i: (i, 0))]
    operands = [a, b]
    aliases = {}
    if into is not None:
        in_specs.append(ANY)
        operands.append(into)
        aliases = {2: 0}
    return pl.pallas_call(
        body, name=name, grid=(nk,),
        in_specs=in_specs,
        out_specs=pl.BlockSpec((n_blocks, ka, nb), lambda i: (first_block // n_blocks, 0, 0)),
        out_shape=jax.ShapeDtypeStruct((total_blocks, ka, nb), BF16),
        scratch_shapes=[pltpu.VMEM((n_blocks, ka, nb), F32)],
        input_output_aliases=aliases,
        compiler_params=_cparams(1),
    )(*operands)


def _attn_bwd(qkv, o, proj, dmix, scatter=None, scatter_owner=None):
    scatter = [] if scatter is None else scatter
    nsc = len(scatter)
    owners = [scatter_owner] * nsc
    s = qkv.shape[0]
    tb = ATTN_BLOCK
    nq = s // tb
    n_sub = min(ATTN_SUB_BWD, nq)
    tq = n_sub * tb
    n_chain = 2 * n_sub

    def body(*refs):
        q_ref, k_ref, v_ref, o_ref, g_ref, dya_ref, tri_s_ref, tri_p_ref = refs[0:8]
        sc_ins = refs[8:8 + nsc]
        dq_ref, dk_ref, dv_ref, dg_ref = refs[8 + nsc:12 + nsc]
        sc_outs = refs[12 + nsc:12 + 2 * nsc]
        do_scr, l_scr, g_scr, s_scr, w_scr = refs[12 + 2 * nsc:17 + 2 * nsc]
        sc_sems = refs[17 + 2 * nsc:]
        i = pl.program_id(1)
        base = i * n_sub

        if nsc:
            @pl.when(jnp.logical_and(pl.program_id(0) == 0, i == 0))
            def _():
                _scatter_start(sc_ins, sc_outs, owners, sc_sems)

        @pl.when(i == 0)
        def _():
            dk_ref[...] = jnp.zeros_like(dk_ref)
            dv_ref[...] = jnp.zeros_like(dv_ref)

        g = g_ref[...]
        sg = _sigmoid(g)
        dya = dya_ref[...]
        do_scr[...] = (dya * (g * sg)).astype(BF16)
        dg_ref[...] = dya * o_ref[...] * (sg * (1.0 + g * (1.0 - sg)))
        dq_ref[...] = jnp.zeros_like(dq_ref)
        g_scr[...] = jnp.zeros_like(g_scr)
        masks = _head_masks()

        def keep(t, a, h, c, r0, z, w):
            s_scr[c, t] = _sigmoid(z).astype(BF16)
            w_scr[c, t] = w.astype(BF16)

        steps = _chain_sweep(base, n_sub, q_ref, k_ref, tri_s_ref, l_scr, keep)
        row = lax.broadcasted_iota(jnp.int32, (tb, tb), 0)
        col = lax.broadcasted_iota(jnp.int32, (tb, tb), 1)

        def back(it, carry):
            t = steps - 1 - it
            r0s = [pl.multiple_of(jnp.maximum(base + a - t, 0) * tb, tb) for a in range(n_sub)]
            qhs, dohs, khs, gws = [], [], [], []
            for a in range(n_sub):
                kb = k_ref[pl.ds(r0s[a], tb), :]
                vb = v_ref[pl.ds(r0s[a], tb), :]
                qa = q_ref[a * tb:(a + 1) * tb, :]
                doa = do_scr[a * tb:(a + 1) * tb, :]
                for h, mask in enumerate(masks):
                    qhs.append(jnp.where(mask, qa, jnp.zeros_like(qa)))
                    khs.append(jnp.where(mask, kb, jnp.zeros_like(kb)))
                    dohs.append(jnp.where(mask, doa, jnp.zeros_like(doa)))
                    gws.append(w_scr[2 * a + h, t].astype(F32) * _dot_nt(dohs[-1], vb))
            parts = [_split_hilo(gw) for gw in gws]
            tri = tri_p_ref[...]
            sums = [_dot(hi, tri) + _dot(lo, tri) for hi, lo in parts]
            dzs = []
            for c, (gw, sm) in enumerate(zip(gws, sums)):
                gsum = g_scr[c]
                dz = gw - (gw + sm[:, 0:tb] + gsum) * s_scr[c, t].astype(F32)
                dz = jnp.where(col < row + t * tb, dz, 0.0)
                g_scr[c] = gsum + sm[:, tb:2 * tb]
                dzs.append(dz.astype(BF16))
            for c, dzb in enumerate(dzs):
                a = c // 2
                dk_ref[pl.ds(r0s[a], tb), :] += _dot_tn(dzb, qhs[c])
                dv_ref[pl.ds(r0s[a], tb), :] += _dot_tn(w_scr[c, t], dohs[c])
                dq_ref[a * tb:(a + 1) * tb, :] += _dot(dzb, khs[c])
            return carry

        lax.fori_loop(0, steps, back, 0)

        if nsc:
            @pl.when(jnp.logical_and(pl.program_id(0) == n_hp - 1, i == s // tq - 1))
            def _():
                _scatter_finish(sc_ins, sc_outs, owners, sc_sems)

    n_hp = ATTN_WIDTH // (2 * HEAD_DIM)
    hp_blk = lambda off: pl.BlockSpec((tq, 2 * HEAD_DIM), lambda hp, i: (i, off + hp))
    res = lambda off: pl.BlockSpec((s, 2 * HEAD_DIM), lambda hp, i: (0, off + hp))
    tri = pl.BlockSpec((tb, 2 * tb), lambda hp, i: (0, 0))
    outs = pl.pallas_call(
        body, name="attn_bwd_scatter" if nsc else "attn_bwd", grid=(n_hp, s // tq),
        in_specs=[hp_blk(0), res(4), res(8), hp_blk(0), hp_blk(20), hp_blk(4), tri, tri] + [ANY] * nsc,
        out_specs=[hp_blk(0), res(0), res(0), hp_blk(0)] + [ANY] * nsc,
        out_shape=[jax.ShapeDtypeStruct((s, ATTN_WIDTH), F32)] * 4 + [jax.ShapeDtypeStruct(a.shape, a.dtype) for a in scatter],
        scratch_shapes=[pltpu.VMEM((tq, 2 * HEAD_DIM), BF16), pltpu.VMEM((n_chain, tb, tb), F32),
                        pltpu.VMEM((n_chain, tb, tb), F32), pltpu.VMEM((n_chain, nq, tb, tb), BF16),
                        pltpu.VMEM((n_chain, nq, tb, tb), BF16)] + (_scatter_sems(nsc) if nsc else []),
        compiler_params=_cparams(2),
    )(qkv, qkv, qkv, o, proj, dmix, _tri("suffix_incl"), _tri("prefix_strict"), *scatter)
    return outs[0], outs[1], outs[2], outs[3], (_scatter_own(outs[4:], scatter) if nsc else [])


def _ssm_glu_bwd(dmix, y, proj, d, w_glu_all, layer, b_glu):
    s = y.shape[0]
    tm = _row_tile(s, 512)

    def body(dys_ref, y_ref, u_ref, gs_ref, d_ref, wg_ref, bg_ref,
             dyf_ref, du_ref, dgs_ref, z_ref, dzz_ref, dd_ref, db_ref):
        @pl.when(pl.program_id(0) == 0)
        def _():
            dd_ref[...] = jnp.zeros_like(dd_ref)
            db_ref[...] = jnp.zeros_like(db_ref)

        u = u_ref[...]
        dv = d_ref[...]
        yf, z, val, gate = _glu_forward(y_ref[...], u, dv, wg_ref, bg_ref[...])
        gs = gs_ref[...]
        sgs = _sigmoid(gs)
        sgate = _sigmoid(gate)
        dys = dys_ref[...]
        dgv = dys * (gs * sgs)
        dgs_ref[...] = dys * (val * sgate) * (sgs * (1.0 + gs * (1.0 - sgs)))
        dzz = jnp.concatenate([dgv * sgate, dgv * val * sgate * (1.0 - sgate)], axis=-1)
        dzzb = dzz.astype(BF16)
        dz = _dot_nt(dzzb[:, 0:ROW_SHARD], wg_ref[0])
        for sh in range(1, N_CHIPS):
            dz = dz + _dot_nt(dzzb[:, ROW_SHARD * sh:ROW_SHARD * (sh + 1)], wg_ref[sh])
        dyf = dz * _gelu_grad(yf)
        dyf_ref[...] = dyf
        du_ref[...] = dyf * dv
        z_ref[...] = z.astype(BF16)
        dzz_ref[...] = dzzb
        dd_ref[...] += _colsum8(dyf * u)
        db_ref[...] += _colsum8(dzz)

    row = lambda i: (i, 0)
    half = pl.BlockSpec((tm, SSM_WIDTH), row)
    return pl.pallas_call(
        body, name="ssm_glu_bwd", grid=(s // tm,),
        in_specs=[half, half, half, pl.BlockSpec((tm, SSM_WIDTH), lambda i: (i, 1)),
                  pl.BlockSpec((1, SSM_WIDTH), lambda i: (0, 0)),
                  pl.BlockSpec((N_CHIPS, None, SSM_WIDTH, ROW_SHARD), lambda i: (0, layer, 0, 0)),
                  pl.BlockSpec((1, 2 * SSM_WIDTH), lambda i: (0, 0))],
        out_specs=[half, half, half, half, pl.BlockSpec((tm, 2 * SSM_WIDTH), row),
                   pl.BlockSpec((SUBLANES, SSM_WIDTH), lambda i: (0, 0)),
                   pl.BlockSpec((SUBLANES, 2 * SSM_WIDTH), lambda i: (0, 0))],
        out_shape=[jax.ShapeDtypeStruct((s, SSM_WIDTH), F32)] * 3
        + [jax.ShapeDtypeStruct((s, SSM_WIDTH), BF16), jax.ShapeDtypeStruct((s, 2 * SSM_WIDTH), BF16),
           jax.ShapeDtypeStruct((SUBLANES, SSM_WIDTH), F32), jax.ShapeDtypeStruct((SUBLANES, 2 * SSM_WIDTH), F32)],
        compiler_params=_cparams(1),
    )(dmix, y, proj, proj, d, w_glu_all, b_glu)


def _ssm_scan_bwd(dyf, xs, proj, wct, coef_rev, wbt):
    s = dyf.shape[0]
    tm = _row_tile(s, 512)
    nt = s // tm

    def body(dy_ref, xs_ref, u_ref, wct_ref, coef_ref, wbt_ref, du_ref, dwc_ref, dwb_ref, da_ref, lam_ref, carry_ref):
        @pl.when(pl.program_id(1) == 0)
        def _():
            carry_ref[...] = jnp.zeros_like(carry_ref)
            dwc_ref[...] = jnp.zeros_like(dwc_ref)
            dwb_ref[...] = jnp.zeros_like(dwb_ref)
            da_ref[...] = jnp.zeros_like(da_ref)

        dyb = dy_ref[...].astype(BF16)
        lam_ref[...] = _dot(dyb, wct_ref[...])
        rows = lax.broadcasted_iota(jnp.int32, (SUBLANES, CH_S), 0)
        last = rows == SUBLANES - 1

        def extra(r0, lr, li, cr, ci):
            er = jnp.where(last, cr, pltpu.roll(lr, SUBLANES - 1, 0))
            ei = jnp.where(last, ci, pltpu.roll(li, SUBLANES - 1, 0))
            xr = xs_ref[pl.ds(r0, SUBLANES), 0:CH_S]
            xi = xs_ref[pl.ds(r0, SUBLANES), CH_S:2 * CH_S]
            da_ref[:, 0:CH_S] += xr * er + xi * ei
            da_ref[:, CH_S:2 * CH_S] += xr * ei - xi * er

        _scan_rows(lam_ref, coef_ref, carry_ref, tm // SUBLANES, reverse=True, extra=extra)
        lamb = lam_ref[...].astype(BF16)
        du_ref[...] = _dot(lamb, wbt_ref[...])
        dwc_ref[...] += _dot_tn(xs_ref[...].astype(BF16), dyb)
        dwb_ref[...] += _dot_tn(u_ref[...].astype(BF16), lamb)

    rev = lambda j, i: (nt - 1 - i, j)
    return pl.pallas_call(
        body, name="ssm_scan_bwd", grid=(SSM_CHUNKS, nt),
        in_specs=[pl.BlockSpec((tm, CH_W), rev),
                  pl.BlockSpec((None, tm, 2 * CH_S), lambda j, i: (j, nt - 1 - i, 0)),
                  pl.BlockSpec((tm, CH_W), rev),
                  pl.BlockSpec((None, CH_W, 2 * CH_S), lambda j, i: (j, 0, 0)),
                  pl.BlockSpec((None, 8, SUBLANES, CH_S), lambda j, i: (j, 0, 0, 0)),
                  pl.BlockSpec((None, 2 * CH_S, CH_W), lambda j, i: (j, 0, 0))],
        out_specs=[pl.BlockSpec((tm, CH_W), rev),
                   pl.BlockSpec((None, 2 * CH_S, CH_W), lambda j, i: (j, 0, 0)),
                   pl.BlockSpec((None, CH_W, 2 * CH_S), lambda j, i: (j, 0, 0)),
                   pl.BlockSpec((None, SUBLANES, 2 * CH_S), lambda j, i: (j, 0, 0))],
        out_shape=[jax.ShapeDtypeStruct((s, SSM_WIDTH), F32),
                   jax.ShapeDtypeStruct((SSM_CHUNKS, 2 * CH_S, CH_W), F32),
                   jax.ShapeDtypeStruct((SSM_CHUNKS, CH_W, 2 * CH_S), F32),
                   jax.ShapeDtypeStruct((SSM_CHUNKS, SUBLANES, 2 * CH_S), F32)],
        scratch_shapes=[pltpu.VMEM((tm, 2 * CH_S), F32), pltpu.VMEM((SUBLANES, 2 * CH_S), F32)],
        compiler_params=_cparams(2),
    )(dyf, xs, proj, wct, coef_rev, wbt)


def _in_proj_bwd(h, g1, w_in_l, qg, kg, proj, du_a, du_b, dgs, dq, dk, dv, dga, dh1):
    s = h.shape[0]
    tm = _row_tile(s, 256)

    def body(h_ref, g_ref, w_ref, qg_ref, kg_ref, ones_ref, q_ref, k_ref, dua_ref, dub_ref, dgs_ref, dq_ref, dk_ref,
             dv_ref, dga_ref, dh1_ref, dh_ref, hn_ref, dp_ref, dg1_ref, dqg_ref, dkg_ref):
        @pl.when(pl.program_id(0) == 0)
        def _():
            dg1_ref[...] = jnp.zeros_like(dg1_ref)
            dqg_ref[...] = jnp.zeros_like(dqg_ref)
            dkg_ref[...] = jnp.zeros_like(dkg_ref)

        ones = ones_ref[...]

        def head_norm_bwd(x, gain, dy):
            r = lax.rsqrt(_dot_hilo(x * x, ones) + RMS_EPS)
            gdy = gain * dy
            dx = r * gdy - x * (r * r * r) * _dot_hilo(x * gdy, ones)
            return dx, x * r * dy

        dqr, dqg_rows = head_norm_bwd(q_ref[...], qg_ref[...], dq_ref[...] * ATTN_SCALE)
        dkr, dkg_rows = head_norm_bwd(k_ref[...], kg_ref[...], dk_ref[...])
        dqg_ref[...] += _colsum8(dqg_rows)
        dkg_ref[...] += _colsum8(dkg_rows)
        dp_ref[:, 0:512] = (dua_ref[...] + dub_ref[...]).astype(BF16)
        dp_ref[:, 512:1024] = dgs_ref[...].astype(BF16)
        dp_ref[:, 1024:1536] = dqr.astype(BF16)
        dp_ref[:, 1536:2048] = dkr.astype(BF16)
        dp_ref[:, 2048:2560] = dv_ref[...].astype(BF16)
        dp_ref[:, 2560:3072] = dga_ref[...].astype(BF16)
        dhn = _dot_nt(dp_ref[:, 0:IN_SHARD], w_ref[0])
        for sh in range(1, N_CHIPS):
            dhn = dhn + _dot_nt(dp_ref[:, IN_SHARD * sh:IN_SHARD * (sh + 1)], w_ref[sh])
        x = h_ref[...]
        gv = g_ref[...]
        r, hn = _rms_rows(x, gv)
        dx, dg_rows = _rms_bwd(x, r, gv, dhn)
        dh_ref[...] = dh1_ref[...] + dx
        hn_ref[...] = hn.astype(BF16)
        dg1_ref[...] += _colsum8(dg_rows)

    row = lambda i: (i, 0)
    full = lambda shape: pl.BlockSpec(shape, lambda i: (0,) * len(shape))
    big = pl.BlockSpec((tm, D_MODEL), row)
    half = pl.BlockSpec((tm, 512), row)
    return pl.pallas_call(
        body, name="in_proj_bwd", grid=(s // tm,),
        in_specs=[big, full((1, D_MODEL)), full((N_CHIPS, D_MODEL, IN_SHARD)),
                  full((1, ATTN_WIDTH)), full((1, ATTN_WIDTH)), full((ATTN_WIDTH, ATTN_WIDTH)),
                  pl.BlockSpec((tm, 512), lambda i: (i, 2)), pl.BlockSpec((tm, 512), lambda i: (i, 3)),
                  half, half, half, half, half, half, half, big],
        out_specs=[big, big, pl.BlockSpec((tm, IN_COLS), row), pl.BlockSpec((SUBLANES, D_MODEL), lambda i: (0, 0)),
                   pl.BlockSpec((SUBLANES, ATTN_WIDTH), lambda i: (0, 0)), pl.BlockSpec((SUBLANES, ATTN_WIDTH), lambda i: (0, 0))],
        out_shape=[jax.ShapeDtypeStruct((s, D_MODEL), F32), jax.ShapeDtypeStruct((s, D_MODEL), BF16),
                   jax.ShapeDtypeStruct((s, IN_COLS), BF16), jax.ShapeDtypeStruct((SUBLANES, D_MODEL), F32),
                   jax.ShapeDtypeStruct((SUBLANES, ATTN_WIDTH), F32), jax.ShapeDtypeStruct((SUBLANES, ATTN_WIDTH), F32)],
        compiler_params=_cparams(1),
    )(h, g1, w_in_l, qg, kg, _head_ones(), proj, proj, du_a, du_b, dgs, dq, dk, dv, dga, dh1)


SMALL_NAMES = ("mix_norm_g", "ssm_a_re", "ssm_a_im", "ssm_log_dt", "ssm_b_re", "ssm_b_im", "ssm_c_re", "ssm_c_im",
               "ssm_d", "ssm_b_glu", "q_norm_g", "k_norm_g", "ple_norm_g")
SMALL_4D = ("ssm_b_re", "ssm_b_im", "ssm_c_re", "ssm_c_im")
BIG_NAMES = ("w_in", "ssm_w_glu", "w_out", "w_ple_gate", "w_ple_proj")


def _ssm_setup(sm, layer):
    col = lambda a: a[layer].reshape(1, N_STATES)
    a_re, a_im = col(sm["ssm_a_re"]), col(sm["ssm_a_im"])
    log_dt = jnp.repeat(sm["ssm_log_dt"][layer], SSM_STATE).reshape(1, N_STATES)
    b_re = sm["ssm_b_re"][layer].reshape(N_STATES, SSM_GROUP).T
    b_im = sm["ssm_b_im"][layer].reshape(N_STATES, SSM_GROUP).T
    disc_in = (a_re, a_im, log_dt, b_re, b_im)
    ab_re, ab_im, bb_re, bb_im = _disc_fwd(*disc_in)
    wb = jnp.concatenate([_block_diag_in(bb_re), _block_diag_in(bb_im)], axis=-1)
    wc = jnp.concatenate([_block_diag_out(sm["ssm_c_re"][layer]), -_block_diag_out(sm["ssm_c_im"][layer])], axis=1)
    return dict(disc_in=disc_in, wb=wb.astype(BF16), wbt=wb.transpose(0, 2, 1).astype(BF16),
                wc=wc.astype(BF16), wct=wc.transpose(0, 2, 1).astype(BF16),
                coef=_scan_coefs(ab_re, ab_im, False), coef_rev=_scan_coefs(ab_re, ab_im, True))


def _gathered_weights(w_in0, rest):
    wg = dict(zip(BIG_NAMES[1:], rest[1:]))
    wg["w_in"] = [w_in0, rest[0].reshape(N_CHIPS, D_MODEL, IN_SHARD)]
    return wg


def _local_step(x, p, target, sm, w_in0, rest_local=None, rest_bases=None, rest_gathered=None, layer1_hook=None):
    wg = None if rest_gathered is None else _gathered_weights(w_in0, rest_gathered)
    tile8 = lambda a: jnp.tile(a, ATTN_WIDTH // HEAD_DIM).reshape(1, ATTN_WIDTH)
    saved = []
    h = x
    for l in range(N_LAYERS):
        ssm = _ssm_setup(sm, l)
        g1 = sm["mix_norm_g"][l].reshape(1, D_MODEL)
        g2 = sm["ple_norm_g"][l].reshape(1, D_MODEL)
        qg, kg = tile8(sm["q_norm_g"][l]), tile8(sm["k_norm_g"][l])
        dsk = sm["ssm_d"][l].reshape(1, SSM_WIDTH)
        bgl = sm["ssm_b_glu"][l].reshape(1, 2 * SSM_WIDTH)
        proj, qkv = _in_proj(h, g1, w_in0 if l == 0 else wg["w_in"][l], qg, kg)
        if wg is None:
            xs, y, rest = _ssm_scan_fwd(proj, ssm["wb"], ssm["coef"], ssm["wc"], rest_local, rest_bases)
            wg = _gathered_weights(w_in0, rest)
        else:
            xs, y, _ = _ssm_scan_fwd(proj, ssm["wb"], ssm["coef"], ssm["wc"])
        ys = _ssm_glu_fwd(y, proj, dsk, wg["ssm_w_glu"], l, bgl)
        o, ya = _attn_fwd(qkv, proj)
        if l == N_LAYERS - 1:
            h1, h2, sq = _out_ple(h, ys, ya, p[l], g2, wg["w_out"], wg["w_ple_gate"], wg["w_ple_proj"], l, target)
        else:
            h1, h2 = _out_ple(h, ys, ya, p[l], g2, wg["w_out"], wg["w_ple_gate"], wg["w_ple_proj"], l)
        saved.append(dict(ssm=ssm, g1=g1, g2=g2, qg=qg, kg=kg, dsk=dsk, bgl=bgl, h=h, proj=proj, qkv=qkv, xs=xs, y=y,
                          ys=ys, o=o, ya=ya, h1=h1))
        h = h2
    dh = h
    loss = 0.5 * jnp.sum(sq) / D_MODEL

    gbig = [{} for _ in range(N_LAYERS)]
    scattered = []
    gsm = {n: [None] * N_LAYERS for n in SMALL_NAMES}
    for l in reversed(range(N_LAYERS)):
        sv = saved[l]
        ssm = sv["ssm"]
        dh1, dmix, hn2b, dgpb, dppb, dh1b, dg2 = _out_ple_bwd(dh, sv["h1"], p[l], sv["g2"], wg["w_out"],
                                                              wg["w_ple_gate"], wg["w_ple_proj"], l)
        gsm["ple_norm_g"][l] = dg2.sum(0)
        gbig[l]["w_ple_proj"] = _tn_matmul(p[l], dppb, N_CHIPS, False, "dw_ple_proj")
        gbig[l]["w_ple_gate"] = _tn_matmul(hn2b, dgpb, N_CHIPS, True, "dw_ple_gate")
        dwo = _tn_matmul(sv["ys"], dh1b, 2, True, "dw_out_ssm", None, 0, N_CHIPS)
        gbig[l]["w_out"] = _tn_matmul(sv["ya"], dh1b, 2, True, "dw_out_attn", dwo, 2, N_CHIPS)
        if l == 0 and layer1_hook is not None:
            dqs, dkn, dv, dga, scattered = _attn_bwd(sv["qkv"], sv["o"], sv["proj"], dmix, *layer1_hook(gbig[1]))
        else:
            dqs, dkn, dv, dga, _ = _attn_bwd(sv["qkv"], sv["o"], sv["proj"], dmix)
        dyf, du_a, dgs, zb, dzzb, dd, dbg = _ssm_glu_bwd(dmix, sv["y"], sv["proj"], sv["dsk"], wg["ssm_w_glu"], l, sv["bgl"])
        gsm["ssm_d"][l] = dd.sum(0).reshape(SSM_GROUPS, SSM_GROUP)
        gsm["ssm_b_glu"][l] = dbg.sum(0)
        gbig[l]["ssm_w_glu"] = _tn_matmul(zb, dzzb, N_CHIPS, False, "dw_glu")
        du_b, dwc, dwb, da = _ssm_scan_bwd(dyf, sv["xs"], sv["proj"], ssm["wct"], ssm["coef_rev"], ssm["wbt"])
        gsm["ssm_c_re"][l] = _block_diag_out_t(dwc[:, 0:CH_S, :])
        gsm["ssm_c_im"][l] = -_block_diag_out_t(dwc[:, CH_S:, :])
        da = da.sum(1)
        g_ab_re = da[:, 0:CH_S].reshape(1, N_STATES)
        g_ab_im = da[:, CH_S:].reshape(1, N_STATES)
        g_bb_re = _block_diag_in_t(dwb[:, :, 0:CH_S])
        g_bb_im = _block_diag_in_t(dwb[:, :, CH_S:])
        d_are, d_aim, d_ldt, d_bre, d_bim = _disc_bwd(*ssm["disc_in"], g_ab_re, g_ab_im, g_bb_re, g_bb_im)
        gsm["ssm_a_re"][l] = d_are.reshape(SSM_GROUPS, SSM_STATE)
        gsm["ssm_a_im"][l] = d_aim.reshape(SSM_GROUPS, SSM_STATE)
        gsm["ssm_log_dt"][l] = d_ldt.reshape(SSM_GROUPS, SSM_STATE).sum(1)
        gsm["ssm_b_re"][l] = d_bre.T.reshape(SSM_GROUPS, SSM_STATE, SSM_GROUP)
        gsm["ssm_b_im"][l] = d_bim.T.reshape(SSM_GROUPS, SSM_STATE, SSM_GROUP)
        dh, hnb, dprojb, dg1, dqg, dkg = _in_proj_bwd(sv["h"], sv["g1"], wg["w_in"][l], sv["qg"], sv["kg"], sv["proj"],
                                                      du_a, du_b, dgs, dqs, dkn, dv, dga, dh1)
        gsm["mix_norm_g"][l] = dg1.sum(0)
        gsm["q_norm_g"][l] = dqg.sum(0).reshape(-1, HEAD_DIM).sum(0)
        gsm["k_norm_g"][l] = dkg.sum(0).reshape(-1, HEAD_DIM).sum(0)
        gbig[l]["w_in"] = _tn_matmul(hnb, dprojb, N_CHIPS, False, "dw_in")
    gsm = {n: jnp.stack(v, 0) for n, v in gsm.items()}
    return loss, dh, gbig, gsm, scattered


_SMALL_PAD = 8 * 8 * 128


def _pack_small(d, extra):
    flat = jnp.concatenate([d[n].reshape(-1) for n in SMALL_NAMES] + [jnp.stack(extra)])
    n = flat.shape[0]
    padded = -(-n // _SMALL_PAD) * _SMALL_PAD
    return jnp.pad(flat, (0, padded - n))


def _unpack_small(flat, like):
    out, off = {}, 0
    for n in SMALL_NAMES:
        size = like[n].size
        out[n] = flat[off:off + size].reshape(like[n].shape)
        off += size
    return out, flat[off:]


def _flat_rows(a):
    return a.reshape(-1, a.shape[-1])


def _chip_sums(glayer, owner):
    parts = [glayer[n] for n in BIG_NAMES]
    flat = [_flat_rows(a) for a in parts]
    recv = _sibling_push(flat, [owner] * len(flat), "grad_push_layer%d" % owner)
    return [_add_pair(f, r, owner, "grad_pair_add").reshape(a.shape) for f, r, a in zip(flat, recv, parts)]


def kernel(x, p, mix_norm_g, w_in, ssm_a_re, ssm_a_im, ssm_log_dt, ssm_b_re, ssm_b_im, ssm_c_re, ssm_c_im, ssm_d, ssm_w_glu, ssm_b_glu, q_norm_g, k_norm_g, w_out, ple_norm_g, w_ple_gate, w_ple_proj, loss_target, m_mix_norm_g, m_w_in, m_ssm_a_re, m_ssm_a_im, m_ssm_log_dt, m_ssm_b_re, m_ssm_b_im, m_ssm_c_re, m_ssm_c_im, m_ssm_d, m_ssm_w_glu, m_ssm_b_glu, m_q_norm_g, m_k_norm_g, m_w_out, m_ple_norm_g, m_w_ple_gate, m_w_ple_proj, v_mix_norm_g, v_w_in, v_ssm_a_re, v_ssm_a_im, v_ssm_log_dt, v_ssm_b_re, v_ssm_b_im, v_ssm_c_re, v_ssm_c_im, v_ssm_d, v_ssm_w_glu, v_ssm_b_glu, v_q_norm_g, v_k_norm_g, v_w_out, v_ple_norm_g, v_w_ple_gate, v_w_ple_proj):
    args = dict(locals())
    names = ("mix_norm_g", "w_in", "ssm_a_re", "ssm_a_im", "ssm_log_dt", "ssm_b_re", "ssm_b_im", "ssm_c_re", "ssm_c_im",
             "ssm_d", "ssm_w_glu", "ssm_b_glu", "q_norm_g", "k_norm_g", "w_out", "ple_norm_g", "w_ple_gate", "w_ple_proj")
    w = {n: args[n] for n in names}
    m = {n: args["m_" + n] for n in names}
    v = {n: args["v_" + n] for n in names}

    w_in_halves = w["w_in"].astype(BF16).reshape(2 * N_LAYERS, D_MODEL // 2, IN_SHARD)
    w_in0 = _chip_gather([w_in_halves], "w_in_gather")[0].reshape(N_CHIPS, D_MODEL, IN_SHARD)
    rest_local = [w_in_halves] + [w[n].astype(BF16) for n in BIG_NAMES[1:]]
    sm = {n: w[n] for n in SMALL_NAMES}
    loss, dx, gbig, gsm, got1 = _local_step(x[0], p[:, 0], loss_target[0], sm, w_in0, rest_local, [2, 0, 0, 0, 0],
                                            layer1_hook=lambda g1: (_chip_sums(g1, 1), 1))

    nb = len(BIG_NAMES)
    small = _pack_small(gsm, [loss]).reshape(2, N_CHIPS * SUBLANES, -1)
    flat0 = [_flat_rows(gbig[0][n]) for n in BIG_NAMES]
    recv = _sibling_push(flat0 + [small], [0] * nb + [None], "grad_push_layer0")
    chip0 = [_add_pair(f, r, 0, "grad_pair_add").reshape(gbig[0][n].shape) for f, r, n in zip(flat0, recv, BIG_NAMES)]
    chip_small = _add_half(small, recv[-1], F32, "grad_half_add").reshape(N_CHIPS, SUBLANES, -1)
    got0 = _chip_scatter(chip0 + [chip_small], [0] * nb + [None], "grad_chip_scatter")
    tot1 = [_sum4(a, 1, "grad_chip_sum") for a in got1]
    tot0 = [_sum4(a, o, "grad_chip_sum") for a, o in zip(got0, [0] * nb + [None])]
    other, (small_mine,) = _sibling_join(tot0[:nb], tot1, [tot0[nb]], "grad_sibling_join")
    small_all = _chip_gather([small_mine], "small_grad_gather")[0]
    small_tot = small_all.transpose(1, 0, 2, 3).reshape(-1)
    on_core0 = lax.axis_index("c") == 0
    g = {n: jnp.where(on_core0, jnp.stack([t0, ot]), jnp.stack([ot, t1])).reshape(w[n].shape)
         for n, t0, t1, ot in zip(BIG_NAMES, tot0, tot1, other)}
    g_small, rest = _unpack_small(small_tot, sm)
    g.update(g_small)
    loss = rest[0]

    delta, new_m, new_v = {}, {}, {}
    for n in BIG_NAMES:
        lanes = w[n].shape[-1]
        outs = _adamw(_as_rows(w[n], lanes), _as_rows(g[n], lanes), _as_rows(m[n], lanes), _as_rows(v[n], lanes), "adamw_" + n)
        delta[n], new_m[n], new_v[n] = [o.reshape(w[n].shape) for o in outs]
    for group, per_layer in ((SMALL_4D, True), (tuple(n for n in SMALL_NAMES if n not in SMALL_4D), False)):
        outs = _adamw_many(*[[d[n] for n in group] for d in (w, g, m, v)], "adamw_small_4d" if per_layer else "adamw_small", per_layer)
        for d, o in zip((delta, new_m, new_v), outs):
            d.update(zip(group, o))

    return (loss, dx[None], *[g[n] for n in names], *[delta[n] for n in names],
            *[new_m[n] for n in names], *[new_v[n] for n in names])
```

```python
import functools
import math

import jax
import jax.numpy as jnp
from jax import lax
from jax.experimental import pallas as pl
from jax.experimental.pallas import tpu as pltpu

F32 = jnp.float32
BF16 = jnp.bfloat16

D_MODEL = 1024
N_LAYERS = 2
N_CHIPS = 4
IN_COLS = 3072
IN_SHARD = IN_COLS // N_CHIPS
SSM_WIDTH = 512
SSM_GROUP = 16
SSM_GROUPS = 32
SSM_STATE = 64
N_STATES = SSM_GROUPS * SSM_STATE
SSM_CHUNKS = 4
CH_W = SSM_WIDTH // SSM_CHUNKS
CH_S = N_STATES // SSM_CHUNKS
ATTN_WIDTH = 512
HEAD_DIM = 64
PLE_DIM = 256
ROW_SHARD = 256
RMS_EPS = 1e-6
ATTN_SCALE = HEAD_DIM ** -0.5
ATTN_BLOCK = 128
EXP_ZERO = -87.5
SUBLANES = 8
V7X_VMEM_LIMIT = 52 * 1024 * 1024

ADAM_LR = 0.001
ADAM_B1 = 0.9
ADAM_B2 = 0.999
ADAM_EPS = 1e-08
ADAM_WD = 0.01
ADAM_STEP = 10

MESH = pl.DeviceIdType.MESH
ANY = pl.BlockSpec(memory_space=pl.ANY)


def _cparams(n_grid=0, parallel=0):
    sem = tuple(["parallel"] * parallel + ["arbitrary"] * (n_grid - parallel))
    return pltpu.CompilerParams(dimension_semantics=sem, vmem_limit_bytes=V7X_VMEM_LIMIT)


def _dot(a, b):
    return jnp.dot(a, b, preferred_element_type=F32)


def _dot_nt(a, b):
    return lax.dot_general(a, b, (((1,), (1,)), ((), ())), preferred_element_type=F32)


def _dot_tn(a, b):
    return lax.dot_general(a, b, (((0,), (0,)), ((), ())), preferred_element_type=F32)


def _split_hilo(a):
    hi = a.astype(BF16)
    lo = (a - hi.astype(F32)).astype(BF16)
    return hi, lo


def _dot_hilo(a, b):
    hi, lo = _split_hilo(a)
    return _dot(hi, b) + _dot(lo, b)


def _sigmoid(x):
    return 0.5 * (jnp.tanh(0.5 * x) + 1.0)


_GELU_C = math.sqrt(2.0 / math.pi)


def _gelu(x):
    return 0.5 * x * (1.0 + jnp.tanh(_GELU_C * (x + 0.044715 * (x * x * x))))


def _gelu_grad(x):
    t = jnp.tanh(_GELU_C * (x + 0.044715 * (x * x * x)))
    return 0.5 * (1.0 + t) + 0.5 * x * (1.0 - t * t) * (_GELU_C * (1.0 + 3.0 * 0.044715 * (x * x)))


def _row_tile(s, want):
    for t in range(min(s, want), 7, -1):
        if s % t == 0 and t % SUBLANES == 0:
            return t
    return s


def _coords():
    return lax.axis_index("x"), lax.axis_index("y"), lax.axis_index("c")


def _other_chips(x, y):
    return [(1 - x, y), (x, 1 - y), (1 - x, 1 - y)]


def _remote(src, dst, send_sem, recv_sem, dev):
    return pltpu.make_async_remote_copy(src_ref=src, dst_ref=dst, send_sem=send_sem, recv_sem=recv_sem,
                                        device_id=dev, device_id_type=MESH)


def _set_block(buf, block, index):
    return lax.dynamic_update_index_in_dim(buf, block, index, 0)


def _gather_sems(n):
    return [pltpu.SemaphoreType.DMA((3 * n,)) for _ in range(4)]


def _gather_copies(ins, bases, outs, sems):
    send_sems, recv_sems, fwd_send, fwd_recv = sems
    x, y, c = _coords()
    me_chip = 2 * x + y
    sibling = (x, y, 1 - c)
    first, landed, passed, from_sibling = [], [], [], []
    for k in range(len(ins)):
        for j, (cx, cy) in enumerate(_other_chips(x, y)):
            i = 3 * k + j
            first.append(_remote(ins[k].at[bases[k] + c], outs[k].at[me_chip, c], send_sems.at[i], recv_sems.at[i], (cx, cy, c)))
            blk = outs[k].at[2 * cx + cy, c]
            landed.append(_remote(blk, blk, send_sems.at[i], recv_sems.at[i], (cx, cy, c)))
            passed.append(_remote(blk, blk, fwd_send.at[i], fwd_recv.at[i], sibling))
            blk = outs[k].at[2 * cx + cy, 1 - c]
            from_sibling.append(_remote(blk, blk, fwd_send.at[i], fwd_recv.at[i], sibling))
    return first, landed, passed, from_sibling


def _gather_start(ins, bases, outs, sems):
    for cp in _gather_copies(ins, bases, outs, sems)[0]:
        cp.start()


def _gather_finish(ins, bases, outs, sems):
    first, landed, passed, from_sibling = _gather_copies(ins, bases, outs, sems)
    for arrived, forward in zip(landed, passed):
        arrived.wait_recv()
        forward.start()
    for cp in from_sibling:
        cp.wait_recv()
    for cp in first + passed:
        cp.wait_send()


def _gather_outputs(arrs):
    return [jax.ShapeDtypeStruct((N_CHIPS, 2) + a.shape[1:], a.dtype) for a in arrs]


def _gather_own(outs, arrs, bases):
    me_chip = 2 * lax.axis_index("x") + lax.axis_index("y")
    return [_set_block(o, lax.slice_in_dim(a, b, b + 2, axis=0), me_chip) for o, a, b in zip(outs, arrs, bases)]


def _chip_gather(arrs, name, bases=None):
    n = len(arrs)
    bases = [0] * n if bases is None else bases

    def body(*refs):
        ins, outs, sems = refs[:n], refs[n:2 * n], refs[2 * n:]
        _gather_start(ins, bases, outs, sems)
        _gather_finish(ins, bases, outs, sems)

    outs = pl.pallas_call(
        body, name=name, out_shape=_gather_outputs(arrs),
        in_specs=[ANY] * n, out_specs=[ANY] * n, scratch_shapes=_gather_sems(n),
    )(*arrs)
    return _gather_own(outs, arrs, bases)


def _sibling_push(arrs, owners, name):
    n = len(arrs)

    def body(*refs):
        ins, outs = refs[:n], refs[n:2 * n]
        send_sems, recv_sems = refs[2 * n:]
        x, y, c = _coords()
        cps = [_remote(ins[k].at[1 - c] if owners[k] is None else ins[k], outs[k], send_sems.at[k], recv_sems.at[k],
                       (x, y, 1 - c)) for k in range(n)]
        for o in (None, 0, 1):
            mine = [cp for cp, ow in zip(cps, owners) if ow == o]
            if not mine:
                continue
            if o is None:
                for cp in mine:
                    cp.start()
                for cp in mine:
                    cp.wait_recv()
                for cp in mine:
                    cp.wait_send()
            else:
                @pl.when(c == 1 - o)
                def _():
                    for cp in mine:
                        cp.start()
                    for cp in mine:
                        cp.wait_send()

                @pl.when(c == o)
                def _():
                    for cp in mine:
                        cp.wait_recv()

    return pl.pallas_call(
        body, name=name,
        out_shape=[jax.ShapeDtypeStruct(a.shape[1:] if ow is None else a.shape, a.dtype) for a, ow in zip(arrs, owners)],
        in_specs=[ANY] * n, out_specs=[ANY] * n,
        scratch_shapes=[pltpu.SemaphoreType.DMA((n,)), pltpu.SemaphoreType.DMA((n,))],
    )(*arrs)


def _sibling_join(tot0, tot1, sym, name):
    nb, ns = len(tot0), len(sym)
    n = nb + ns

    def body(*refs):
        ins0, ins1, ins_s = refs[:nb], refs[nb:2 * nb], refs[2 * nb:2 * nb + ns]
        outs_b, outs_s = refs[2 * nb + ns:3 * nb + ns], refs[3 * nb + ns:3 * nb + 2 * ns]
        send_sems, recv_sems = refs[3 * nb + 2 * ns:]
        x, y, c = _coords()
        sibling = (x, y, 1 - c)

        def big(src):
            return [_remote(src[k], outs_b[k], send_sems.at[k], recv_sems.at[k], sibling) for k in range(nb)]

        @pl.when(c == 0)
        def _():
            for cp in big(ins0):
                cp.start()

        @pl.when(c == 1)
        def _():
            for cp in big(ins1):
                cp.start()

        halves = [_remote(ins_s[k], outs_s[k].at[c], send_sems.at[nb + k], recv_sems.at[nb + k], sibling) for k in range(ns)]
        for cp in halves:
            cp.start()
        for cp in big(ins0):
            cp.wait_recv()
        for k in range(ns):
            blk = outs_s[k].at[1 - c]
            _remote(blk, blk, send_sems.at[nb + k], recv_sems.at[nb + k], sibling).wait_recv()
        for cp in big(ins0) + halves:
            cp.wait_send()

    outs = pl.pallas_call(
        body, name=name,
        out_shape=[jax.ShapeDtypeStruct(a.shape, a.dtype) for a in tot0]
        + [jax.ShapeDtypeStruct((2,) + a.shape, a.dtype) for a in sym],
        in_specs=[ANY] * (2 * nb + ns), out_specs=[ANY] * n,
        scratch_shapes=[pltpu.SemaphoreType.DMA((n,)), pltpu.SemaphoreType.DMA((n,))],
    )(*tot0, *tot1, *sym)
    c = lax.axis_index("c")
    return outs[:nb], [_set_block(o, a, c) for o, a in zip(outs[nb:], sym)]


def _scatter_sems(n):
    return [pltpu.SemaphoreType.DMA((3 * n,)), pltpu.SemaphoreType.DMA((3 * n,))]


def _scatter_copies(ins, outs, sems):
    send_sems, recv_sems = sems
    x, y, c = _coords()
    me_chip = 2 * x + y
    sends, arrivals = [], []
    for k in range(len(ins)):
        for j, (cx, cy) in enumerate(_other_chips(x, y)):
            i = 3 * k + j
            sends.append(_remote(ins[k].at[2 * cx + cy], outs[k].at[me_chip], send_sems.at[i], recv_sems.at[i], (cx, cy, c)))
            blk = outs[k].at[2 * cx + cy]
            arrivals.append(_remote(blk, blk, send_sems.at[i], recv_sems.at[i], (cx, cy, c)))
    return sends, arrivals


def _by_owner(owners, fn):
    c = lax.axis_index("c")
    for o in (None, 0, 1):
        idx = [k for k, ow in enumerate(owners) if ow == o]
        if not idx:
            continue
        if o is None:
            fn(idx)
        else:
            pl.when(c == o)(functools.partial(fn, idx))


def _scatter_start(ins, outs, owners, sems):
    sends, _ = _scatter_copies(ins, outs, sems)

    def go(idx):
        for k in idx:
            for cp in sends[3 * k:3 * k + 3]:
                cp.start()

    _by_owner(owners, go)


def _scatter_finish(ins, outs, owners, sems):
    sends, arrivals = _scatter_copies(ins, outs, sems)

    def go(idx):
        for k in idx:
            for cp in arrivals[3 * k:3 * k + 3]:
                cp.wait_recv()
        for k in idx:
            for cp in sends[3 * k:3 * k + 3]:
                cp.wait_send()

    _by_owner(owners, go)


def _scatter_own(outs, arrs):
    me_chip = 2 * lax.axis_index("x") + lax.axis_index("y")
    return [_set_block(o, lax.dynamic_index_in_dim(a, me_chip, 0, keepdims=False), me_chip) for o, a in zip(outs, arrs)]


def _chip_scatter(arrs, owners, name):
    n = len(arrs)

    def body(*refs):
        ins, outs, sems = refs[:n], refs[n:2 * n], refs[2 * n:]
        _scatter_start(ins, outs, owners, sems)
        _scatter_finish(ins, outs, owners, sems)

    outs = pl.pallas_call(
        body, name=name,
        out_shape=[jax.ShapeDtypeStruct(a.shape, a.dtype) for a in arrs],
        in_specs=[ANY] * n, out_specs=[ANY] * n, scratch_shapes=_scatter_sems(n),
    )(*arrs)
    return _scatter_own(outs, arrs)


def _as_rows(a, lanes):
    return a.reshape(-1, lanes)


def _add_half(full, recv, out_dtype, name):
    _, r, cdim = full.shape
    tr = _row_tile(r, 512)

    def body(c_ref, a_ref, b_ref, o_ref):
        o_ref[...] = (a_ref[...].astype(F32) + b_ref[...].astype(F32)).astype(out_dtype)

    c = lax.axis_index("c").astype(jnp.int32).reshape(1)
    return pl.pallas_call(
        body, name=name,
        grid_spec=pltpu.PrefetchScalarGridSpec(
            num_scalar_prefetch=1, grid=(r // tr,),
            in_specs=[pl.BlockSpec((None, tr, cdim), lambda i, c_ref: (c_ref[0], i, 0)),
                      pl.BlockSpec((tr, cdim), lambda i, c_ref: (i, 0))],
            out_specs=pl.BlockSpec((tr, cdim), lambda i, c_ref: (i, 0))),
        out_shape=jax.ShapeDtypeStruct((r, cdim), out_dtype),
        compiler_params=_cparams(1),
    )(c, full, recv)


def _owner_flag(owner):
    if owner is None:
        return jnp.ones((1,), jnp.int32)
    return (lax.axis_index("c") == owner).astype(jnp.int32).reshape(1)


def _add_pair(a, b, owner, name):
    r, cdim = a.shape
    tr = _row_tile(r, 512)

    def body(f_ref, a_ref, b_ref, o_ref):
        o_ref[...] = (a_ref[...].astype(F32) + b_ref[...].astype(F32)).astype(o_ref.dtype)

    spec = pl.BlockSpec((tr, cdim), lambda i, f_ref: (i * f_ref[0], 0))
    return pl.pallas_call(
        body, name=name,
        grid_spec=pltpu.PrefetchScalarGridSpec(num_scalar_prefetch=1, grid=(r // tr,), in_specs=[spec] * 2, out_specs=spec),
        out_shape=jax.ShapeDtypeStruct((r, cdim), a.dtype), compiler_params=_cparams(1))(_owner_flag(owner), a, b)


def _sum4(parts, owner, name):
    _, r, cdim = parts.shape
    tr = _row_tile(r, 512)

    def body(f_ref, p_ref, o_ref):
        acc = p_ref[0].astype(F32) + p_ref[1].astype(F32)
        acc = acc + p_ref[2].astype(F32)
        o_ref[...] = acc + p_ref[3].astype(F32)

    return pl.pallas_call(
        body, name=name,
        grid_spec=pltpu.PrefetchScalarGridSpec(
            num_scalar_prefetch=1, grid=(r // tr,),
            in_specs=[pl.BlockSpec((N_CHIPS, tr, cdim), lambda i, f_ref: (0, i * f_ref[0], 0))],
            out_specs=pl.BlockSpec((tr, cdim), lambda i, f_ref: (i * f_ref[0], 0))),
        out_shape=jax.ShapeDtypeStruct((r, cdim), F32),
        compiler_params=_cparams(1),
    )(_owner_flag(owner), parts)


def _adamw_math(w, g, m, v):
    c1 = 1.0 - ADAM_B1 ** ADAM_STEP
    c2 = 1.0 - ADAM_B2 ** ADAM_STEP
    nm = ADAM_B1 * m + (1.0 - ADAM_B1) * g
    nv = ADAM_B2 * v + (1.0 - ADAM_B2) * (g * g)
    delta = -ADAM_LR * ((nm / c1) / (jnp.sqrt(nv / c2) + ADAM_EPS) + ADAM_WD * w)
    return delta, nm, nv


def _adamw(w, g, m, v, name):
    r, cdim = w.shape
    tr = _row_tile(r, 256)

    def body(w_ref, g_ref, m_ref, v_ref, d_ref, nm_ref, nv_ref):
        d_ref[...], nm_ref[...], nv_ref[...] = _adamw_math(w_ref[...], g_ref[...], m_ref[...], v_ref[...])

    spec = pl.BlockSpec((tr, cdim), lambda i: (i, 0))
    return pl.pallas_call(
        body, name=name, grid=(r // tr,),
        in_specs=[spec] * 4, out_specs=[spec] * 3,
        out_shape=[jax.ShapeDtypeStruct((r, cdim), F32)] * 3,
        compiler_params=_cparams(1),
    )(w, g, m, v)


def _adamw_many(ws, gs, ms, vs, name, per_layer):
    n = len(ws)

    def body(*refs):
        for k in range(n):
            w, g, m, v = (refs[j * n + k][...] for j in range(4))
            outs = _adamw_math(w, g, m, v)
            for j in range(3):
                refs[(4 + j) * n + k][...] = outs[j]

    shapes = [jax.ShapeDtypeStruct(w.shape, F32) for w in ws]
    if per_layer:
        specs = [pl.BlockSpec((None,) + w.shape[1:], lambda l, nd=w.ndim: (l,) + (0,) * (nd - 1)) for w in ws]
        call = pl.pallas_call(body, name=name, grid=(N_LAYERS,), in_specs=specs * 4, out_specs=specs * 3,
                              out_shape=shapes * 3, compiler_params=_cparams(1))
    else:
        call = pl.pallas_call(body, name=name, out_shape=shapes * 3, compiler_params=_cparams())
    outs = call(*ws, *gs, *ms, *vs)
    return outs[0:n], outs[n:2 * n], outs[2 * n:3 * n]


def _discretise(a_re, a_im, log_dt, b_re, b_im):
    dt = jnp.exp(log_dt)
    mag = jnp.exp(a_re * dt)
    ab_re = mag * jnp.cos(a_im * dt)
    ab_im = mag * jnp.sin(a_im * dt)
    num_re = ab_re - 1.0
    num_im = ab_im
    den = a_re * a_re + a_im * a_im
    f_re = (num_re * a_re + num_im * a_im) / den
    f_im = (num_im * a_re - num_re * a_im) / den
    bb_re = f_re * b_re - f_im * b_im
    bb_im = f_re * b_im + f_im * b_re
    return ab_re, ab_im, bb_re, bb_im


def _disc_shapes():
    col = jax.ShapeDtypeStruct((1, N_STATES), F32)
    mat = jax.ShapeDtypeStruct((SSM_GROUP, N_STATES), F32)
    return col, mat


def _disc_fwd(a_re, a_im, log_dt, b_re, b_im):
    col, mat = _disc_shapes()

    def body(ar, ai, ld, br, bi, o0, o1, o2, o3):
        outs = _discretise(ar[...], ai[...], ld[...], br[...], bi[...])
        for o, val in zip((o0, o1, o2, o3), outs):
            o[...] = val

    return pl.pallas_call(body, name="ssm_discretise", out_shape=[col, col, mat, mat],
                          compiler_params=_cparams())(a_re, a_im, log_dt, b_re, b_im)


def _disc_bwd(a_re, a_im, log_dt, b_re, b_im, g_ab_re, g_ab_im, g_bb_re, g_bb_im):
    col, mat = _disc_shapes()

    def body(ar, ai, ld, br, bi, g0, g1, g2, g3, o0, o1, o2, o3, o4):
        _, vjp = jax.vjp(_discretise, ar[...], ai[...], ld[...], br[...], bi[...])
        grads = vjp((g0[...], g1[...], g2[...], g3[...]))
        for o, val in zip((o0, o1, o2, o3, o4), grads):
            o[...] = val

    return pl.pallas_call(body, name="ssm_discretise_bwd", out_shape=[col, col, col, mat, mat],
                          compiler_params=_cparams())(a_re, a_im, log_dt, b_re, b_im, g_ab_re, g_ab_im, g_bb_re, g_bb_im)


def _cmul(ar, ai, br, bi):
    return ar * br - ai * bi, ar * bi + ai * br


def _scan_powers(ab_re, ab_im, length, reverse):
    br = ab_re.reshape(1, N_STATES)
    bi = -ab_im.reshape(1, N_STATES) if reverse else ab_im.reshape(1, N_STATES)
    ks = (length - jnp.arange(length)) if reverse else (jnp.arange(length) + 1)
    pr = jnp.ones((length, N_STATES), F32)
    pi = jnp.zeros((length, N_STATES), F32)
    for bit in range(length.bit_length()):
        take = ((ks >> bit) & 1)[:, None] == 1
        mr, mi = _cmul(pr, pi, br, bi)
        pr, pi = jnp.where(take, mr, pr), jnp.where(take, mi, pi)
        br, bi = _cmul(br, bi, br, bi)
    split = lambda t: t.reshape(length, SSM_CHUNKS, CH_S).transpose(1, 0, 2)
    return jnp.concatenate([split(pr), split(pi)], axis=-1)


def _interleave_chunks(v):
    rows, width = v.shape
    return pltpu.einshape("cjw->jcw", v.reshape(SUBLANES, rows // SUBLANES, width)).reshape(rows, width)


def _time_order(v):
    rows, width = v.shape
    return pltpu.einshape("jcw->cjw", v.reshape(rows // SUBLANES, SUBLANES, width)).reshape(rows, width)


def _block_diag_in(bb):
    t = bb.reshape(SSM_GROUP, SSM_CHUNKS, 8, SSM_STATE)
    eye = jnp.eye(8, dtype=bb.dtype)
    return jnp.einsum("hjgp,gk->jghkp", t, eye).reshape(SSM_CHUNKS, CH_W, CH_S)


def _block_diag_in_t(d):
    t = d.reshape(SSM_CHUNKS, 8, SSM_GROUP, 8, SSM_STATE)
    return jnp.einsum("jghgp->hjgp", t).reshape(SSM_GROUP, N_STATES)


def _block_diag_out(c):
    t = c.reshape(SSM_CHUNKS, 8, SSM_GROUP, SSM_STATE)
    eye = jnp.eye(8, dtype=c.dtype)
    return jnp.einsum("jghp,gk->jgpkh", t, eye).reshape(SSM_CHUNKS, CH_S, CH_W)


def _block_diag_out_t(d):
    t = d.reshape(SSM_CHUNKS, 8, SSM_STATE, 8, SSM_GROUP)
    return jnp.einsum("jgpgh->jghp", t).reshape(SSM_GROUPS, SSM_GROUP, SSM_STATE)


def _head_ones():
    r = jnp.arange(ATTN_WIDTH) // HEAD_DIM
    return jnp.where(r[:, None] == r[None, :], 1.0 / HEAD_DIM, 0.0).astype(BF16)


def _in_proj(h, g1, w_in_l, qg, kg):
    s = h.shape[0]
    tm = _row_tile(s, 256)

    def body(h_ref, g_ref, w_ref, qg_ref, kg_ref, ones_ref, proj_ref, qkv_ref):
        x = h_ref[...]
        r = lax.rsqrt(jnp.mean(x * x, axis=-1, keepdims=True) + RMS_EPS)
        hn = (x * r * g_ref[...]).astype(BF16)
        for sh in range(N_CHIPS):
            proj_ref[:, IN_SHARD * sh:IN_SHARD * (sh + 1)] = _dot(hn, w_ref[sh])
        ones = ones_ref[...]
        q = proj_ref[:, 1024:1536]
        k = proj_ref[:, 1536:2048]
        rq = lax.rsqrt(_dot_hilo(q * q, ones) + RMS_EPS)
        rk = lax.rsqrt(_dot_hilo(k * k, ones) + RMS_EPS)
        qkv_ref[:, 0:512] = (q * rq * qg_ref[...] * ATTN_SCALE).astype(BF16)
        qkv_ref[:, 512:1024] = (k * rk * kg_ref[...]).astype(BF16)
        qkv_ref[:, 1024:1536] = proj_ref[:, 2048:2560].astype(BF16)

    full = lambda shape: pl.BlockSpec(shape, lambda i: (0,) * len(shape))
    return pl.pallas_call(
        body, name="in_proj", grid=(s // tm,),
        in_specs=[pl.BlockSpec((tm, D_MODEL), lambda i: (i, 0)), full((1, D_MODEL)),
                  full((N_CHIPS, D_MODEL, IN_SHARD)),
                  full((1, ATTN_WIDTH)), full((1, ATTN_WIDTH)), full((ATTN_WIDTH, ATTN_WIDTH))],
        out_specs=[pl.BlockSpec((tm, IN_COLS), lambda i: (i, 0)), pl.BlockSpec((tm, 3 * ATTN_WIDTH), lambda i: (i, 0))],
        out_shape=[jax.ShapeDtypeStruct((s, IN_COLS), F32), jax.ShapeDtypeStruct((s, 3 * ATTN_WIDTH), BF16)],
        compiler_params=_cparams(1),
    )(h, g1, w_in_l, qg, kg, _head_ones())


def _row_bcast(ref, k, lo):
    return jnp.broadcast_to(ref[pl.ds(k, 1), lo:lo + CH_S], (SUBLANES, CH_S))


def _chunk_scan(x_ref, tab_ref, carry_ref, length, reverse, tail=None):
    row = lax.broadcasted_iota(jnp.int32, (SUBLANES, CH_S), 0)
    one, full = (length - 1, 0) if reverse else (0, length - 1)
    ar, ai = _row_bcast(tab_ref, one, 0), _row_bcast(tab_ref, one, CH_S)
    fr, fi = _row_bcast(tab_ref, full, 0), _row_bcast(tab_ref, full, CH_S)
    step = lambda jj: (length - 1 - jj) if reverse else jj

    def local(jj, carry):
        cr, ci = carry
        r0 = pl.multiple_of(step(jj) * SUBLANES, SUBLANES)
        xr = x_ref[pl.ds(r0, SUBLANES), 0:CH_S] + (ar * cr - ai * ci)
        xi = x_ref[pl.ds(r0, SUBLANES), CH_S:2 * CH_S] + (ar * ci + ai * cr)
        x_ref[pl.ds(r0, SUBLANES), 0:CH_S] = xr
        x_ref[pl.ds(r0, SUBLANES), CH_S:2 * CH_S] = xi
        return xr, xi

    zero = jnp.zeros((SUBLANES, CH_S), F32)
    er, ei = lax.fori_loop(0, length, local, (zero, zero))

    first, shift = (SUBLANES - 1, SUBLANES - 1) if reverse else (0, 1)
    hr = jnp.where(row == first, carry_ref[:, 0:CH_S], 0.0)
    hi = jnp.where(row == first, carry_ref[:, CH_S:2 * CH_S], 0.0)
    sr, si = pltpu.roll(er, shift, 0), pltpu.roll(ei, shift, 0)
    for k in range(1, SUBLANES):
        tr, ti = pltpu.roll(hr, shift, 0), pltpu.roll(hi, shift, 0)
        here = row == ((SUBLANES - 1 - k) if reverse else k)
        hr, hi = (jnp.where(here, fr * tr - fi * ti + sr, hr), jnp.where(here, fr * ti + fi * tr + si, hi))
    last = 0 if reverse else SUBLANES - 1
    outr, outi = fr * hr - fi * hi + er, fr * hi + fi * hr + ei
    carry_ref[:, 0:CH_S] = jnp.broadcast_to(outr[last:last + 1, :], (SUBLANES, CH_S))
    carry_ref[:, CH_S:2 * CH_S] = jnp.broadcast_to(outi[last:last + 1, :], (SUBLANES, CH_S))

    def fix(jj, carry):
        j = step(jj)
        r0 = pl.multiple_of(j * SUBLANES, SUBLANES)
        pr, pi = _row_bcast(tab_ref, j, 0), _row_bcast(tab_ref, j, CH_S)
        xr = x_ref[pl.ds(r0, SUBLANES), 0:CH_S] + (pr * hr - pi * hi)
        xi = x_ref[pl.ds(r0, SUBLANES), CH_S:2 * CH_S] + (pr * hi + pi * hr)
        x_ref[pl.ds(r0, SUBLANES), 0:CH_S] = xr
        x_ref[pl.ds(r0, SUBLANES), CH_S:2 * CH_S] = xi
        if tail is None:
            return carry
        return tail(r0, xr, xi, carry)

    return fix, (hr, hi)


def _ssm_scan_fwd(proj, wb, tab, wc, gather=None, gather_bases=None):
    s = proj.shape[0]
    tm = _row_tile(s, 512)
    nt = s // tm
    length = tm // SUBLANES
    gather = [] if gather is None else gather
    ng = len(gather)

    def body(*refs):
        u_ref, wb_ref, tab_ref, wc_ref = refs[0:4]
        g_ins = refs[4:4 + ng]
        xs_ref, y_ref = refs[4 + ng:6 + ng]
        g_outs = refs[6 + ng:6 + 2 * ng]
        carry_ref = refs[6 + 2 * ng]
        sems = refs[7 + 2 * ng:]
        j, i = pl.program_id(0), pl.program_id(1)

        @pl.when(i == 0)
        def _():
            carry_ref[...] = jnp.zeros_like(carry_ref)

        if ng:
            @pl.when(jnp.logical_and(j == 0, i == 0))
            def _():
                _gather_start(g_ins, gather_bases, g_outs, sems)

        xs_ref[...] = _dot(_interleave_chunks(u_ref[...]).astype(BF16), wb_ref[...])
        fix, start = _chunk_scan(xs_ref, tab_ref, carry_ref, length, reverse=False)
        lax.fori_loop(0, length, fix, start, unroll=2)
        y_ref[...] = _time_order(_dot(xs_ref[...].astype(BF16), wc_ref[...]))

        if ng:
            @pl.when(jnp.logical_and(j == SSM_CHUNKS - 1, i == nt - 1))
            def _():
                _gather_finish(g_ins, gather_bases, g_outs, sems)

    outs = pl.pallas_call(
        body, name="ssm_scan_gather" if ng else "ssm_scan", grid=(SSM_CHUNKS, nt),
        in_specs=[pl.BlockSpec((tm, CH_W), lambda j, i: (i, j)),
                  pl.BlockSpec((None, CH_W, 2 * CH_S), lambda j, i: (j, 0, 0)),
                  pl.BlockSpec((None, length, 2 * CH_S), lambda j, i: (j, 0, 0)),
                  pl.BlockSpec((None, 2 * CH_S, CH_W), lambda j, i: (j, 0, 0))] + [ANY] * ng,
        out_specs=[pl.BlockSpec((None, tm, 2 * CH_S), lambda j, i: (j, i, 0)),
                   pl.BlockSpec((tm, CH_W), lambda j, i: (i, j))] + [ANY] * ng,
        out_shape=[jax.ShapeDtypeStruct((SSM_CHUNKS, s, 2 * CH_S), F32), jax.ShapeDtypeStruct((s, SSM_WIDTH), F32)]
        + _gather_outputs(gather),
        scratch_shapes=[pltpu.VMEM((SUBLANES, 2 * CH_S), F32)] + (_gather_sems(ng) if ng else []),
        compiler_params=_cparams(2),
    )(proj, wb, tab, wc, *gather)
    return outs[0], outs[1], (_gather_own(outs[2:], gather, gather_bases) if ng else [])


def _glu_forward(y, u, d, wg_ref, bg):
    yf = y + d * u
    z = _gelu(yf)
    zb = z.astype(BF16)
    zz = jnp.concatenate([_dot(zb, wg_ref[sh]) for sh in range(N_CHIPS)], axis=-1) + bg
    return yf, z, zz[:, 0:SSM_WIDTH], zz[:, SSM_WIDTH:2 * SSM_WIDTH]


def _ssm_glu_fwd(y, proj, d, w_glu_all, layer, b_glu):
    s = y.shape[0]
    tm = _row_tile(s, 512)

    def body(y_ref, u_ref, gs_ref, d_ref, wg_ref, bg_ref, o_ref):
        _, _, val, gate = _glu_forward(y_ref[...], u_ref[...], d_ref[...], wg_ref, bg_ref[...])
        gs = gs_ref[...]
        o_ref[...] = val * _sigmoid(gate) * (gs * _sigmoid(gs))

    row = lambda i: (i, 0)
    return pl.pallas_call(
        body, name="ssm_glu", grid=(s // tm,),
        in_specs=[pl.BlockSpec((tm, SSM_WIDTH), row), pl.BlockSpec((tm, SSM_WIDTH), row),
                  pl.BlockSpec((tm, SSM_WIDTH), lambda i: (i, 1)), pl.BlockSpec((1, SSM_WIDTH), lambda i: (0, 0)),
                  pl.BlockSpec((N_CHIPS, None, SSM_WIDTH, ROW_SHARD), lambda i: (0, layer, 0, 0)),
                  pl.BlockSpec((1, 2 * SSM_WIDTH), lambda i: (0, 0))],
        out_specs=pl.BlockSpec((tm, SSM_WIDTH), row),
        out_shape=jax.ShapeDtypeStruct((s, SSM_WIDTH), F32),
        compiler_params=_cparams(1),
    )(y, proj, proj, d, w_glu_all, b_glu)


def _tri(kind):
    r = jnp.arange(ATTN_BLOCK)
    if kind == "suffix_incl":
        m = r[:, None] >= r[None, :]
    else:
        m = r[:, None] < r[None, :]
    return jnp.concatenate([m, jnp.ones_like(m)], axis=1).astype(BF16)


def _head_masks():
    lane = lax.broadcasted_iota(jnp.int32, (1, 2 * HEAD_DIM), 1)
    return [lane < HEAD_DIM, lane >= HEAD_DIM]


def _chain_step(t, base, n_sub, first, q_ref, k_ref, tri_ref, l_scr, per_chain):
    tb = ATTN_BLOCK
    row = lax.broadcasted_iota(jnp.int32, (tb, tb), 0)
    col = lax.broadcasted_iota(jnp.int32, (tb, tb), 1)
    masks = _head_masks()
    blks = [base + a - t for a in range(n_sub)]
    r0s = [pl.multiple_of(jnp.maximum(blk, 0) * tb, tb) for blk in blks]
    zs = []
    for a in range(n_sub):
        kb = k_ref[pl.ds(r0s[a], tb), :]
        qa = q_ref[a * tb:(a + 1) * tb, :]
        for mask in masks:
            zs.append(_dot_nt(jnp.where(mask, qa, jnp.zeros_like(qa)), kb))
    parts = []
    for z in zs:
        ls = jnp.minimum(-z, 0.0) - jnp.log(1.0 + jnp.exp(-jnp.abs(z)))
        if first:
            ls = jnp.where(col < row, ls, 0.0)
        parts.append(_split_hilo(ls))
    tri = tri_ref[...]
    sums = [_dot(hi, tri) + _dot(lo, tri) for hi, lo in parts]
    top = None
    ws = []
    for c, (z, sm) in enumerate(zip(zs, sums)):
        if first:
            lsum = jnp.zeros((tb, tb), F32)
        else:
            lsum = l_scr[c] + jnp.where(blks[c // 2] >= 0, 0.0, -1e30)
        w = jnp.exp(z + sm[:, 0:tb] + lsum)
        if first:
            w = jnp.where(col < row, w, 0.0)
        ws.append(w)
        lsum = lsum + sm[:, tb:2 * tb]
        l_scr[c] = lsum
        top = lsum if top is None else jnp.maximum(top, lsum)
    for c, (z, w) in enumerate(zip(zs, ws)):
        per_chain(c // 2, c % 2, c, r0s[c // 2], z, w)
    return jnp.max(top)


def _chain_sweep(base, n_sub, q_ref, k_ref, tri_ref, l_scr, per_chain):
    top = _chain_step(0, base, n_sub, True, q_ref, k_ref, tri_ref, l_scr, functools.partial(per_chain, 0))

    def cond(carry):
        t, top = carry
        return jnp.logical_and(t <= base + n_sub - 1, top > EXP_ZERO)

    def step(carry):
        t, _ = carry
        return t + 1, _chain_step(t, base, n_sub, False, q_ref, k_ref, tri_ref, l_scr, functools.partial(per_chain, t))

    steps, _ = lax.while_loop(cond, step, (jnp.int32(1), top))
    return steps


ATTN_SUB_FWD = 4
ATTN_SUB_BWD = 4


def _attn_fwd(qkv, proj):
    s = qkv.shape[0]
    tb = ATTN_BLOCK
    n_sub = min(ATTN_SUB_FWD, s // tb)
    tq = n_sub * tb

    def body(q_ref, k_ref, v_ref, g_ref, tri_ref, o_ref, ya_ref, l_scr):
        i = pl.program_id(1)
        masks = _head_masks()
        o_ref[...] = jnp.zeros_like(o_ref)

        def per_chain(t, a, h, c, r0, z, w):
            vb = v_ref[pl.ds(r0, tb), :]
            vb = jnp.where(masks[h], vb, jnp.zeros_like(vb))
            o_ref[a * tb:(a + 1) * tb, :] += _dot(w.astype(BF16), vb)

        _chain_sweep(i * n_sub, n_sub, q_ref, k_ref, tri_ref, l_scr, per_chain)
        g = g_ref[...]
        ya_ref[...] = o_ref[...] * (g * _sigmoid(g))

    hp_blk = lambda off: pl.BlockSpec((tq, 2 * HEAD_DIM), lambda hp, i: (i, off + hp))
    res = lambda off: pl.BlockSpec((s, 2 * HEAD_DIM), lambda hp, i: (0, off + hp))
    return pl.pallas_call(
        body, name="attn_fwd", grid=(ATTN_WIDTH // (2 * HEAD_DIM), s // tq),
        in_specs=[hp_blk(0), res(4), res(8), hp_blk(20), pl.BlockSpec((tb, 2 * tb), lambda hp, i: (0, 0))],
        out_specs=[hp_blk(0), hp_blk(0)],
        out_shape=[jax.ShapeDtypeStruct((s, ATTN_WIDTH), F32)] * 2,
        scratch_shapes=[pltpu.VMEM((2 * n_sub, tb, tb), F32)],
        compiler_params=_cparams(2),
    )(qkv, qkv, qkv, proj, _tri("suffix_incl"))


def _rms_rows(x, g):
    r = lax.rsqrt(jnp.mean(x * x, axis=-1, keepdims=True) + RMS_EPS)
    return r, x * r * g


def _ple_forward(h1, p, g2, wpg_ref, wpp_ref):
    r2, hn2 = _rms_rows(h1, g2)
    hb = hn2.astype(BF16)
    gpre = _dot(hb[:, 0:ROW_SHARD], wpg_ref[0])
    for sh in range(1, N_CHIPS):
        gpre = gpre + _dot(hb[:, ROW_SHARD * sh:ROW_SHARD * (sh + 1)], wpg_ref[sh])
    gate = _sigmoid(gpre)
    pb = p.astype(BF16)
    pp = jnp.concatenate([_dot(pb, wpp_ref[sh]) for sh in range(N_CHIPS)], axis=-1)
    return r2, hb, gate, pp


def _colsum8(a):
    t = a.shape[0]
    return a.reshape(t // SUBLANES, SUBLANES, a.shape[1]).sum(axis=0)


def _sq_err_grad(y, target):
    e = y - target
    sq = _colsum8(e * e)
    part = sq[:, 0:128]
    for b in range(1, D_MODEL // 128):
        part = part + sq[:, 128 * b:128 * (b + 1)]
    return e / D_MODEL, part


def _out_ple(h, ys, ya, p, g2, w_out_all, w_pg_all, w_pp_all, layer, target=None):
    s = h.shape[0]
    tm = _row_tile(s, 256)
    last = target is not None

    def body(*refs):
        h_ref, ys_ref, ya_ref, p_ref, g_ref, wo_ref, wpg_ref, wpp_ref = refs[0:8]
        h1_ref, h2_ref = refs[8 + last], refs[9 + last]
        ysb = ys_ref[...].astype(BF16)
        yab = ya_ref[...].astype(BF16)
        h1 = h_ref[...]
        for sh, src in enumerate((ysb[:, 0:ROW_SHARD], ysb[:, ROW_SHARD:], yab[:, 0:ROW_SHARD], yab[:, ROW_SHARD:])):
            h1 = h1 + _dot(src, wo_ref[sh])
        _, _, gate, pp = _ple_forward(h1, p_ref[...], g_ref[...], wpg_ref, wpp_ref)
        h1_ref[...] = h1
        h2 = h1 + gate * pp
        if last:
            acc_ref = refs[11]

            @pl.when(pl.program_id(0) == 0)
            def _():
                acc_ref[...] = jnp.zeros_like(acc_ref)

            h2_ref[...], part = _sq_err_grad(h2, refs[8][...])
            acc_ref[...] += part
        else:
            h2_ref[...] = h2

    row = lambda i: (i, 0)
    big = pl.BlockSpec((tm, D_MODEL), row)
    wspec = lambda r, cdim: pl.BlockSpec((N_CHIPS, None, r, cdim), lambda i: (0, layer, 0, 0))
    acc = pl.BlockSpec((SUBLANES, 128), lambda i: (0, 0))
    return pl.pallas_call(
        body, name="out_ple_loss" if last else "out_ple", grid=(s // tm,),
        in_specs=[big, pl.BlockSpec((tm, SSM_WIDTH), row), pl.BlockSpec((tm, ATTN_WIDTH), row),
                  pl.BlockSpec((tm, PLE_DIM), row), pl.BlockSpec((1, D_MODEL), lambda i: (0, 0)),
                  wspec(ROW_SHARD, D_MODEL), wspec(ROW_SHARD, D_MODEL), wspec(PLE_DIM, ROW_SHARD)] + [big] * last,
        out_specs=[big] * 2 + [acc] * last,
        out_shape=[jax.ShapeDtypeStruct((s, D_MODEL), F32)] * 2 + [jax.ShapeDtypeStruct((SUBLANES, 128), F32)] * last,
        compiler_params=_cparams(1),
    )(h, ys, ya, p, g2, w_out_all, w_pg_all, w_pp_all, *([target] if last else []))


def _rms_bwd(x, r, g, dy):
    gdy = g * dy
    dx = r * gdy - x * (r * r * r) * jnp.mean(x * gdy, axis=-1, keepdims=True)
    return dx, x * r * dy


def _out_ple_bwd(dh2, h1, p, g2, w_out_all, w_pg_all, w_pp_all, layer):
    s = h1.shape[0]
    tm = _row_tile(s, 256)

    def body(dh2_ref, h1_ref, p_ref, g_ref, wo_ref, wpg_ref, wpp_ref,
             dh1_ref, dmix_ref, hn_ref, dgp_ref, dpp_ref, dh1b_ref, dg_ref):
        @pl.when(pl.program_id(0) == 0)
        def _():
            dg_ref[...] = jnp.zeros_like(dg_ref)

        h1 = h1_ref[...]
        dh2 = dh2_ref[...]
        g2v = g_ref[...]
        r2, hb, gate, pp = _ple_forward(h1, p_ref[...], g2v, wpg_ref, wpp_ref)
        dgp = (dh2 * pp) * gate * (1.0 - gate)
        dgpb = dgp.astype(BF16)
        dhn = jnp.concatenate([_dot_nt(dgpb, wpg_ref[sh]) for sh in range(N_CHIPS)], axis=-1)
        dx, dgrow = _rms_bwd(h1, r2, g2v, dhn)
        dh1 = dh2 + dx
        dh1b = dh1.astype(BF16)
        dh1_ref[...] = dh1
        dh1b_ref[...] = dh1b
        hn_ref[...] = hb
        dgp_ref[...] = dgpb
        dpp_ref[...] = (dh2 * gate).astype(BF16)
        dg_ref[...] += _colsum8(dgrow)
        for sh in range(N_CHIPS):
            dmix_ref[:, ROW_SHARD * sh:ROW_SHARD * (sh + 1)] = _dot_nt(dh1b, wo_ref[sh])

    row = lambda i: (i, 0)
    wspec = lambda r, cdim: pl.BlockSpec((N_CHIPS, None, r, cdim), lambda i: (0, layer, 0, 0))
    big = pl.BlockSpec((tm, D_MODEL), row)
    return pl.pallas_call(
        body, name="out_ple_bwd", grid=(s // tm,),
        in_specs=[big, big, pl.BlockSpec((tm, PLE_DIM), row), pl.BlockSpec((1, D_MODEL), lambda i: (0, 0)),
                  wspec(ROW_SHARD, D_MODEL), wspec(ROW_SHARD, D_MODEL), wspec(PLE_DIM, ROW_SHARD)],
        out_specs=[big] * 6 + [pl.BlockSpec((SUBLANES, D_MODEL), lambda i: (0, 0))],
        out_shape=[jax.ShapeDtypeStruct((s, D_MODEL), F32)] * 2 + [jax.ShapeDtypeStruct((s, D_MODEL), BF16)] * 4
        + [jax.ShapeDtypeStruct((SUBLANES, D_MODEL), F32)],
        compiler_params=_cparams(1),
    )(dh2, h1, p, g2, w_out_all, w_pg_all, w_pp_all)


def _tn_matmul(a, b, n_blocks, block_a, name, into=None, first_block=0, total_blocks=None):
    s = a.shape[0]
    tk = _row_tile(s, 512)
    nk = s // tk
    total_blocks = n_blocks if total_blocks is None else total_blocks
    ka, nb = a.shape[1], b.shape[1]
    if block_a:
        ka //= n_blocks
    else:
        nb //= n_blocks

    def body(*refs):
        a_ref, b_ref, o_ref, acc_ref = refs[0], refs[1], refs[-2], refs[-1]

        @pl.when(pl.program_id(0) == 0)
        def _():
            acc_ref[...] = jnp.zeros_like(acc_ref)

        at = a_ref[...].astype(BF16).T
        bb = b_ref[...].astype(BF16)
        for sh in range(n_blocks):
            if block_a:
                acc_ref[sh] += _dot(at[ka * sh:ka * (sh + 1), :], bb)
            else:
                acc_ref[sh] += _dot(at, bb[:, nb * sh:nb * (sh + 1)])

        @pl.when(pl.program_id(0) == nk - 1)
        def _():
            o_ref[...] = acc_ref[...].astype(BF16)

    in_specs = [pl.BlockSpec((tk, a.shape[1]), lambda i: (i, 0)), pl.BlockSpec((tk, b.shape[1]), lambda i: (i, 0))]
    operands = [a, b]
    aliases = {}
    if into is not None:
        in_specs.append(ANY)
        operands.append(into)
        aliases = {2: 0}
    return pl.pallas_call(
        body, name=name, grid=(nk,),
        in_specs=in_specs,
        out_specs=pl.BlockSpec((n_blocks, ka, nb), lambda i: (first_block // n_blocks, 0, 0)),
        out_shape=jax.ShapeDtypeStruct((total_blocks, ka, nb), BF16),
        scratch_shapes=[pltpu.VMEM((n_blocks, ka, nb), F32)],
        input_output_aliases=aliases,
        compiler_params=_cparams(1),
    )(*operands)


def _attn_bwd(qkv, o, proj, dmix, scatter=None, scatter_owner=None):
    scatter = [] if scatter is None else scatter
    nsc = len(scatter)
    owners = [scatter_owner] * nsc
    s = qkv.shape[0]
    tb = ATTN_BLOCK
    nq = s // tb
    n_sub = min(ATTN_SUB_BWD, nq)
    tq = n_sub * tb
    n_chain = 2 * n_sub

    def body(*refs):
        q_ref, k_ref, v_ref, o_ref, g_ref, dya_ref, tri_s_ref, tri_p_ref = refs[0:8]
        sc_ins = refs[8:8 + nsc]
        dq_ref, dk_ref, dv_ref, dg_ref = refs[8 + nsc:12 + nsc]
        sc_outs = refs[12 + nsc:12 + 2 * nsc]
        do_scr, l_scr, g_scr, s_scr, w_scr = refs[12 + 2 * nsc:17 + 2 * nsc]
        sc_sems = refs[17 + 2 * nsc:]
        i = pl.program_id(1)
        base = i * n_sub

        if nsc:
            @pl.when(jnp.logical_and(pl.program_id(0) == 0, i == 0))
            def _():
                _scatter_start(sc_ins, sc_outs, owners, sc_sems)

        @pl.when(i == 0)
        def _():
            dk_ref[...] = jnp.zeros_like(dk_ref)
            dv_ref[...] = jnp.zeros_like(dv_ref)

        g = g_ref[...]
        sg = _sigmoid(g)
        dya = dya_ref[...]
        do_scr[...] = (dya * (g * sg)).astype(BF16)
        dg_ref[...] = dya * o_ref[...] * (sg * (1.0 + g * (1.0 - sg)))
        dq_ref[...] = jnp.zeros_like(dq_ref)
        g_scr[...] = jnp.zeros_like(g_scr)
        masks = _head_masks()

        def keep(t, a, h, c, r0, z, w):
            s_scr[c, t] = _sigmoid(z).astype(BF16)
            w_scr[c, t] = w.astype(BF16)

        steps = _chain_sweep(base, n_sub, q_ref, k_ref, tri_s_ref, l_scr, keep)
        row = lax.broadcasted_iota(jnp.int32, (tb, tb), 0)
        col = lax.broadcasted_iota(jnp.int32, (tb, tb), 1)

        def back(it, carry):
            t = steps - 1 - it
            r0s = [pl.multiple_of(jnp.maximum(base + a - t, 0) * tb, tb) for a in range(n_sub)]
            qhs, dohs, khs, gws = [], [], [], []
            for a in range(n_sub):
                kb = k_ref[pl.ds(r0s[a], tb), :]
                vb = v_ref[pl.ds(r0s[a], tb), :]
                qa = q_ref[a * tb:(a + 1) * tb, :]
                doa = do_scr[a * tb:(a + 1) * tb, :]
                for h, mask in enumerate(masks):
                    qhs.append(jnp.where(mask, qa, jnp.zeros_like(qa)))
                    khs.append(jnp.where(mask, kb, jnp.zeros_like(kb)))
                    dohs.append(jnp.where(mask, doa, jnp.zeros_like(doa)))
                    gws.append(w_scr[2 * a + h, t].astype(F32) * _dot_nt(dohs[-1], vb))
            parts = [_split_hilo(gw) for gw in gws]
            tri = tri_p_ref[...]
            sums = [_dot(hi, tri) + _dot(lo, tri) for hi, lo in parts]
            dzs = []
            for c, (gw, sm) in enumerate(zip(gws, sums)):
                gsum = g_scr[c]
                dz = gw - (gw + sm[:, 0:tb] + gsum) * s_scr[c, t].astype(F32)
                dz = jnp.where(col < row + t * tb, dz, 0.0)
                g_scr[c] = gsum + sm[:, tb:2 * tb]
                dzs.append(dz.astype(BF16))
            for c, dzb in enumerate(dzs):
                a = c // 2
                dk_ref[pl.ds(r0s[a], tb), :] += _dot_tn(dzb, qhs[c])
                dv_ref[pl.ds(r0s[a], tb), :] += _dot_tn(w_scr[c, t], dohs[c])
                dq_ref[a * tb:(a + 1) * tb, :] += _dot(dzb, khs[c])
            return carry

        lax.fori_loop(0, steps, back, 0)

        if nsc:
            @pl.when(jnp.logical_and(pl.program_id(0) == n_hp - 1, i == s // tq - 1))
            def _():
                _scatter_finish(sc_ins, sc_outs, owners, sc_sems)

    n_hp = ATTN_WIDTH // (2 * HEAD_DIM)
    hp_blk = lambda off: pl.BlockSpec((tq, 2 * HEAD_DIM), lambda hp, i: (i, off + hp))
    res = lambda off: pl.BlockSpec((s, 2 * HEAD_DIM), lambda hp, i: (0, off + hp))
    tri = pl.BlockSpec((tb, 2 * tb), lambda hp, i: (0, 0))
    outs = pl.pallas_call(
        body, name="attn_bwd_scatter" if nsc else "attn_bwd", grid=(n_hp, s // tq),
        in_specs=[hp_blk(0), res(4), res(8), hp_blk(0), hp_blk(20), hp_blk(4), tri, tri] + [ANY] * nsc,
        out_specs=[hp_blk(0), res(0), res(0), hp_blk(0)] + [ANY] * nsc,
        out_shape=[jax.ShapeDtypeStruct((s, ATTN_WIDTH), F32)] * 4 + [jax.ShapeDtypeStruct(a.shape, a.dtype) for a in scatter],
        scratch_shapes=[pltpu.VMEM((tq, 2 * HEAD_DIM), BF16), pltpu.VMEM((n_chain, tb, tb), F32),
                        pltpu.VMEM((n_chain, tb, tb), F32), pltpu.VMEM((n_chain, nq, tb, tb), BF16),
                        pltpu.VMEM((n_chain, nq, tb, tb), BF16)] + (_scatter_sems(nsc) if nsc else []),
        compiler_params=_cparams(2),
    )(qkv, qkv, qkv, o, proj, dmix, _tri("suffix_incl"), _tri("prefix_strict"), *scatter)
    return outs[0], outs[1], outs[2], outs[3], (_scatter_own(outs[4:], scatter) if nsc else [])


def _ssm_glu_bwd(dmix, y, proj, d, w_glu_all, layer, b_glu):
    s = y.shape[0]
    tm = _row_tile(s, 512)

    def body(dys_ref, y_ref, u_ref, gs_ref, d_ref, wg_ref, bg_ref,
             dyf_ref, du_ref, dgs_ref, z_ref, dzz_ref, dd_ref, db_ref):
        @pl.when(pl.program_id(0) == 0)
        def _():
            dd_ref[...] = jnp.zeros_like(dd_ref)
            db_ref[...] = jnp.zeros_like(db_ref)

        u = u_ref[...]
        dv = d_ref[...]
        yf, z, val, gate = _glu_forward(y_ref[...], u, dv, wg_ref, bg_ref[...])
        gs = gs_ref[...]
        sgs = _sigmoid(gs)
        sgate = _sigmoid(gate)
        dys = dys_ref[...]
        dgv = dys * (gs * sgs)
        dgs_ref[...] = dys * (val * sgate) * (sgs * (1.0 + gs * (1.0 - sgs)))
        dzz = jnp.concatenate([dgv * sgate, dgv * val * sgate * (1.0 - sgate)], axis=-1)
        dzzb = dzz.astype(BF16)
        dz = _dot_nt(dzzb[:, 0:ROW_SHARD], wg_ref[0])
        for sh in range(1, N_CHIPS):
            dz = dz + _dot_nt(dzzb[:, ROW_SHARD * sh:ROW_SHARD * (sh + 1)], wg_ref[sh])
        dyf = dz * _gelu_grad(yf)
        dyf_ref[...] = dyf
        du_ref[...] = dyf * dv
        z_ref[...] = z.astype(BF16)
        dzz_ref[...] = dzzb
        dd_ref[...] += _colsum8(dyf * u)
        db_ref[...] += _colsum8(dzz)

    row = lambda i: (i, 0)
    half = pl.BlockSpec((tm, SSM_WIDTH), row)
    return pl.pallas_call(
        body, name="ssm_glu_bwd", grid=(s // tm,),
        in_specs=[half, half, half, pl.BlockSpec((tm, SSM_WIDTH), lambda i: (i, 1)),
                  pl.BlockSpec((1, SSM_WIDTH), lambda i: (0, 0)),
                  pl.BlockSpec((N_CHIPS, None, SSM_WIDTH, ROW_SHARD), lambda i: (0, layer, 0, 0)),
                  pl.BlockSpec((1, 2 * SSM_WIDTH), lambda i: (0, 0))],
        out_specs=[half, half, half, half, pl.BlockSpec((tm, 2 * SSM_WIDTH), row),
                   pl.BlockSpec((SUBLANES, SSM_WIDTH), lambda i: (0, 0)),
                   pl.BlockSpec((SUBLANES, 2 * SSM_WIDTH), lambda i: (0, 0))],
        out_shape=[jax.ShapeDtypeStruct((s, SSM_WIDTH), F32)] * 3
        + [jax.ShapeDtypeStruct((s, SSM_WIDTH), BF16), jax.ShapeDtypeStruct((s, 2 * SSM_WIDTH), BF16),
           jax.ShapeDtypeStruct((SUBLANES, SSM_WIDTH), F32), jax.ShapeDtypeStruct((SUBLANES, 2 * SSM_WIDTH), F32)],
        compiler_params=_cparams(1),
    )(dmix, y, proj, proj, d, w_glu_all, b_glu)


def _ssm_scan_bwd(dyf, xs, proj, wct, tab_rev, wbt):
    s = dyf.shape[0]
    tm = _row_tile(s, 512)
    nt = s // tm
    length = tm // SUBLANES

    def body(dy_ref, xs_ref, u_ref, wct_ref, tab_ref, wbt_ref, du_ref, dwc_ref, dwb_ref, da_ref, lam_ref, carry_ref):
        @pl.when(pl.program_id(1) == 0)
        def _():
            carry_ref[...] = jnp.zeros_like(carry_ref)
            dwc_ref[...] = jnp.zeros_like(dwc_ref)
            dwb_ref[...] = jnp.zeros_like(dwb_ref)
            da_ref[...] = jnp.zeros_like(da_ref)

        dyp = _interleave_chunks(dy_ref[...]).astype(BF16)
        up = _interleave_chunks(u_ref[...]).astype(BF16)
        lam_ref[...] = _dot(dyp, wct_ref[...])

        def tail(r0, lr, li, carry):
            er, ei, dar, dai = carry
            xr = xs_ref[pl.ds(r0, SUBLANES), 0:CH_S]
            xi = xs_ref[pl.ds(r0, SUBLANES), CH_S:2 * CH_S]
            return lr, li, dar + (xr * er + xi * ei), dai + (xr * ei - xi * er)

        fix, (gr, gi) = _chunk_scan(lam_ref, tab_ref, carry_ref, length, reverse=True, tail=tail)
        zero = jnp.zeros((SUBLANES, CH_S), F32)
        _, _, dar, dai = lax.fori_loop(0, length, fix, (gr, gi, zero, zero), unroll=2)
        da_ref[:, 0:CH_S] += dar
        da_ref[:, CH_S:2 * CH_S] += dai
        lamb = lam_ref[...].astype(BF16)
        du_ref[...] = _time_order(_dot(lamb, wbt_ref[...]))
        dwc_ref[...] += _dot_tn(xs_ref[...].astype(BF16), dyp)
        dwb_ref[...] += _dot_tn(up, lamb)

    rev = lambda j, i: (nt - 1 - i, j)
    return pl.pallas_call(
        body, name="ssm_scan_bwd", grid=(SSM_CHUNKS, nt),
        in_specs=[pl.BlockSpec((tm, CH_W), rev),
                  pl.BlockSpec((None, tm, 2 * CH_S), lambda j, i: (j, nt - 1 - i, 0)),
                  pl.BlockSpec((tm, CH_W), rev),
                  pl.BlockSpec((None, CH_W, 2 * CH_S), lambda j, i: (j, 0, 0)),
                  pl.BlockSpec((None, length, 2 * CH_S), lambda j, i: (j, 0, 0)),
                  pl.BlockSpec((None, 2 * CH_S, CH_W), lambda j, i: (j, 0, 0))],
        out_specs=[pl.BlockSpec((tm, CH_W), rev),
                   pl.BlockSpec((None, 2 * CH_S, CH_W), lambda j, i: (j, 0, 0)),
                   pl.BlockSpec((None, CH_W, 2 * CH_S), lambda j, i: (j, 0, 0)),
                   pl.BlockSpec((None, SUBLANES, 2 * CH_S), lambda j, i: (j, 0, 0))],
        out_shape=[jax.ShapeDtypeStruct((s, SSM_WIDTH), F32),
                   jax.ShapeDtypeStruct((SSM_CHUNKS, 2 * CH_S, CH_W), F32),
                   jax.ShapeDtypeStruct((SSM_CHUNKS, CH_W, 2 * CH_S), F32),
                   jax.ShapeDtypeStruct((SSM_CHUNKS, SUBLANES, 2 * CH_S), F32)],
        scratch_shapes=[pltpu.VMEM((tm, 2 * CH_S), F32), pltpu.VMEM((SUBLANES, 2 * CH_S), F32)],
        compiler_params=_cparams(2),
    )(dyf, xs, proj, wct, tab_rev, wbt)


def _in_proj_bwd(h, g1, w_in_l, qg, kg, proj, du_a, du_b, dgs, dq, dk, dv, dga, dh1):
    s = h.shape[0]
    tm = _row_tile(s, 256)

    def body(h_ref, g_ref, w_ref, qg_ref, kg_ref, ones_ref, q_ref, k_ref, dua_ref, dub_ref, dgs_ref, dq_ref, dk_ref,
             dv_ref, dga_ref, dh1_ref, dh_ref, hn_ref, dp_ref, dg1_ref, dqg_ref, dkg_ref):
        @pl.when(pl.program_id(0) == 0)
        def _():
            dg1_ref[...] = jnp.zeros_like(dg1_ref)
            dqg_ref[...] = jnp.zeros_like(dqg_ref)
            dkg_ref[...] = jnp.zeros_like(dkg_ref)

        ones = ones_ref[...]

        def head_norm_bwd(x, gain, dy):
            r = lax.rsqrt(_dot_hilo(x * x, ones) + RMS_EPS)
            gdy = gain * dy
            dx = r * gdy - x * (r * r * r) * _dot_hilo(x * gdy, ones)
            return dx, x * r * dy

        dqr, dqg_rows = head_norm_bwd(q_ref[...], qg_ref[...], dq_ref[...] * ATTN_SCALE)
        dkr, dkg_rows = head_norm_bwd(k_ref[...], kg_ref[...], dk_ref[...])
        dqg_ref[...] += _colsum8(dqg_rows)
        dkg_ref[...] += _colsum8(dkg_rows)
        dp_ref[:, 0:512] = (dua_ref[...] + dub_ref[...]).astype(BF16)
        dp_ref[:, 512:1024] = dgs_ref[...].astype(BF16)
        dp_ref[:, 1024:1536] = dqr.astype(BF16)
        dp_ref[:, 1536:2048] = dkr.astype(BF16)
        dp_ref[:, 2048:2560] = dv_ref[...].astype(BF16)
        dp_ref[:, 2560:3072] = dga_ref[...].astype(BF16)
        dhn = _dot_nt(dp_ref[:, 0:IN_SHARD], w_ref[0])
        for sh in range(1, N_CHIPS):
            dhn = dhn + _dot_nt(dp_ref[:, IN_SHARD * sh:IN_SHARD * (sh + 1)], w_ref[sh])
        x = h_ref[...]
        gv = g_ref[...]
        r, hn = _rms_rows(x, gv)
        dx, dg_rows = _rms_bwd(x, r, gv, dhn)
        dh_ref[...] = dh1_ref[...] + dx
        hn_ref[...] = hn.astype(BF16)
        dg1_ref[...] += _colsum8(dg_rows)

    row = lambda i: (i, 0)
    full = lambda shape: pl.BlockSpec(shape, lambda i: (0,) * len(shape))
    big = pl.BlockSpec((tm, D_MODEL), row)
    half = pl.BlockSpec((tm, 512), row)
    return pl.pallas_call(
        body, name="in_proj_bwd", grid=(s // tm,),
        in_specs=[big, full((1, D_MODEL)), full((N_CHIPS, D_MODEL, IN_SHARD)),
                  full((1, ATTN_WIDTH)), full((1, ATTN_WIDTH)), full((ATTN_WIDTH, ATTN_WIDTH)),
                  pl.BlockSpec((tm, 512), lambda i: (i, 2)), pl.BlockSpec((tm, 512), lambda i: (i, 3)),
                  half, half, half, half, half, half, half, big],
        out_specs=[big, big, pl.BlockSpec((tm, IN_COLS), row), pl.BlockSpec((SUBLANES, D_MODEL), lambda i: (0, 0)),
                   pl.BlockSpec((SUBLANES, ATTN_WIDTH), lambda i: (0, 0)), pl.BlockSpec((SUBLANES, ATTN_WIDTH), lambda i: (0, 0))],
        out_shape=[jax.ShapeDtypeStruct((s, D_MODEL), F32), jax.ShapeDtypeStruct((s, D_MODEL), BF16),
                   jax.ShapeDtypeStruct((s, IN_COLS), BF16), jax.ShapeDtypeStruct((SUBLANES, D_MODEL), F32),
                   jax.ShapeDtypeStruct((SUBLANES, ATTN_WIDTH), F32), jax.ShapeDtypeStruct((SUBLANES, ATTN_WIDTH), F32)],
        compiler_params=_cparams(1),
    )(h, g1, w_in_l, qg, kg, _head_ones(), proj, proj, du_a, du_b, dgs, dq, dk, dv, dga, dh1)


SMALL_NAMES = ("mix_norm_g", "ssm_a_re", "ssm_a_im", "ssm_log_dt", "ssm_b_re", "ssm_b_im", "ssm_c_re", "ssm_c_im",
               "ssm_d", "ssm_b_glu", "q_norm_g", "k_norm_g", "ple_norm_g")
SMALL_4D = ("ssm_b_re", "ssm_b_im", "ssm_c_re", "ssm_c_im")
BIG_NAMES = ("w_in", "ssm_w_glu", "w_out", "w_ple_gate", "w_ple_proj")


def _ssm_setup(sm, layer, length):
    col = lambda a: a[layer].reshape(1, N_STATES)
    a_re, a_im = col(sm["ssm_a_re"]), col(sm["ssm_a_im"])
    log_dt = jnp.repeat(sm["ssm_log_dt"][layer], SSM_STATE).reshape(1, N_STATES)
    b_re = sm["ssm_b_re"][layer].reshape(N_STATES, SSM_GROUP).T
    b_im = sm["ssm_b_im"][layer].reshape(N_STATES, SSM_GROUP).T
    disc_in = (a_re, a_im, log_dt, b_re, b_im)
    ab_re, ab_im, bb_re, bb_im = _disc_fwd(*disc_in)
    wb = jnp.concatenate([_block_diag_in(bb_re), _block_diag_in(bb_im)], axis=-1)
    wc = jnp.concatenate([_block_diag_out(sm["ssm_c_re"][layer]), -_block_diag_out(sm["ssm_c_im"][layer])], axis=1)
    return dict(disc_in=disc_in, wb=wb.astype(BF16), wbt=wb.transpose(0, 2, 1).astype(BF16),
                wc=wc.astype(BF16), wct=wc.transpose(0, 2, 1).astype(BF16),
                tab=_scan_powers(ab_re, ab_im, length, False), tab_rev=_scan_powers(ab_re, ab_im, length, True))


def _gathered_weights(w_in0, rest):
    wg = dict(zip(BIG_NAMES[1:], rest[1:]))
    wg["w_in"] = [w_in0, rest[0].reshape(N_CHIPS, D_MODEL, IN_SHARD)]
    return wg


def _local_step(x, p, target, sm, w_in0, rest_local=None, rest_bases=None, rest_gathered=None, layer1_hook=None):
    wg = None if rest_gathered is None else _gathered_weights(w_in0, rest_gathered)
    tile8 = lambda a: jnp.tile(a, ATTN_WIDTH // HEAD_DIM).reshape(1, ATTN_WIDTH)
    saved = []
    h = x
    for l in range(N_LAYERS):
        ssm = _ssm_setup(sm, l, _row_tile(x.shape[0], 512) // SUBLANES)
        g1 = sm["mix_norm_g"][l].reshape(1, D_MODEL)
        g2 = sm["ple_norm_g"][l].reshape(1, D_MODEL)
        qg, kg = tile8(sm["q_norm_g"][l]), tile8(sm["k_norm_g"][l])
        dsk = sm["ssm_d"][l].reshape(1, SSM_WIDTH)
        bgl = sm["ssm_b_glu"][l].reshape(1, 2 * SSM_WIDTH)
        proj, qkv = _in_proj(h, g1, w_in0 if l == 0 else wg["w_in"][l], qg, kg)
        if wg is None:
            xs, y, rest = _ssm_scan_fwd(proj, ssm["wb"], ssm["tab"], ssm["wc"], rest_local, rest_bases)
            wg = _gathered_weights(w_in0, rest)
        else:
            xs, y, _ = _ssm_scan_fwd(proj, ssm["wb"], ssm["tab"], ssm["wc"])
        ys = _ssm_glu_fwd(y, proj, dsk, wg["ssm_w_glu"], l, bgl)
        o, ya = _attn_fwd(qkv, proj)
        if l == N_LAYERS - 1:
            h1, h2, sq = _out_ple(h, ys, ya, p[l], g2, wg["w_out"], wg["w_ple_gate"], wg["w_ple_proj"], l, target)
        else:
            h1, h2 = _out_ple(h, ys, ya, p[l], g2, wg["w_out"], wg["w_ple_gate"], wg["w_ple_proj"], l)
        saved.append(dict(ssm=ssm, g1=g1, g2=g2, qg=qg, kg=kg, dsk=dsk, bgl=bgl, h=h, proj=proj, qkv=qkv, xs=xs, y=y,
                          ys=ys, o=o, ya=ya, h1=h1))
        h = h2
    dh = h
    loss = 0.5 * jnp.sum(sq) / D_MODEL

    gbig = [{} for _ in range(N_LAYERS)]
    scattered = []
    gsm = {n: [None] * N_LAYERS for n in SMALL_NAMES}
    for l in reversed(range(N_LAYERS)):
        sv = saved[l]
        ssm = sv["ssm"]
        dh1, dmix, hn2b, dgpb, dppb, dh1b, dg2 = _out_ple_bwd(dh, sv["h1"], p[l], sv["g2"], wg["w_out"],
                                                              wg["w_ple_gate"], wg["w_ple_proj"], l)
        gsm["ple_norm_g"][l] = dg2.sum(0)
        gbig[l]["w_ple_proj"] = _tn_matmul(p[l], dppb, N_CHIPS, False, "dw_ple_proj")
        gbig[l]["w_ple_gate"] = _tn_matmul(hn2b, dgpb, N_CHIPS, True, "dw_ple_gate")
        dwo = _tn_matmul(sv["ys"], dh1b, 2, True, "dw_out_ssm", None, 0, N_CHIPS)
        gbig[l]["w_out"] = _tn_matmul(sv["ya"], dh1b, 2, True, "dw_out_attn", dwo, 2, N_CHIPS)
        if l == 0 and layer1_hook is not None:
            dqs, dkn, dv, dga, scattered = _attn_bwd(sv["qkv"], sv["o"], sv["proj"], dmix, *layer1_hook(gbig[1]))
        else:
            dqs, dkn, dv, dga, _ = _attn_bwd(sv["qkv"], sv["o"], sv["proj"], dmix)
        dyf, du_a, dgs, zb, dzzb, dd, dbg = _ssm_glu_bwd(dmix, sv["y"], sv["proj"], sv["dsk"], wg["ssm_w_glu"], l, sv["bgl"])
        gsm["ssm_d"][l] = dd.sum(0).reshape(SSM_GROUPS, SSM_GROUP)
        gsm["ssm_b_glu"][l] = dbg.sum(0)
        gbig[l]["ssm_w_glu"] = _tn_matmul(zb, dzzb, N_CHIPS, False, "dw_glu")
        du_b, dwc, dwb, da = _ssm_scan_bwd(dyf, sv["xs"], sv["proj"], ssm["wct"], ssm["tab_rev"], ssm["wbt"])
        gsm["ssm_c_re"][l] = _block_diag_out_t(dwc[:, 0:CH_S, :])
        gsm["ssm_c_im"][l] = -_block_diag_out_t(dwc[:, CH_S:, :])
        da = da.sum(1)
        g_ab_re = da[:, 0:CH_S].reshape(1, N_STATES)
        g_ab_im = da[:, CH_S:].reshape(1, N_STATES)
        g_bb_re = _block_diag_in_t(dwb[:, :, 0:CH_S])
        g_bb_im = _block_diag_in_t(dwb[:, :, CH_S:])
        d_are, d_aim, d_ldt, d_bre, d_bim = _disc_bwd(*ssm["disc_in"], g_ab_re, g_ab_im, g_bb_re, g_bb_im)
        gsm["ssm_a_re"][l] = d_are.reshape(SSM_GROUPS, SSM_STATE)
        gsm["ssm_a_im"][l] = d_aim.reshape(SSM_GROUPS, SSM_STATE)
        gsm["ssm_log_dt"][l] = d_ldt.reshape(SSM_GROUPS, SSM_STATE).sum(1)
        gsm["ssm_b_re"][l] = d_bre.T.reshape(SSM_GROUPS, SSM_STATE, SSM_GROUP)
        gsm["ssm_b_im"][l] = d_bim.T.reshape(SSM_GROUPS, SSM_STATE, SSM_GROUP)
        dh, hnb, dprojb, dg1, dqg, dkg = _in_proj_bwd(sv["h"], sv["g1"], wg["w_in"][l], sv["qg"], sv["kg"], sv["proj"],
                                                      du_a, du_b, dgs, dqs, dkn, dv, dga, dh1)
        gsm["mix_norm_g"][l] = dg1.sum(0)
        gsm["q_norm_g"][l] = dqg.sum(0).reshape(-1, HEAD_DIM).sum(0)
        gsm["k_norm_g"][l] = dkg.sum(0).reshape(-1, HEAD_DIM).sum(0)
        gbig[l]["w_in"] = _tn_matmul(hnb, dprojb, N_CHIPS, False, "dw_in")
    gsm = {n: jnp.stack(v, 0) for n, v in gsm.items()}
    return loss, dh, gbig, gsm, scattered


_SMALL_PAD = 8 * 8 * 128


def _pack_small(d, extra):
    flat = jnp.concatenate([d[n].reshape(-1) for n in SMALL_NAMES] + [jnp.stack(extra)])
    n = flat.shape[0]
    padded = -(-n // _SMALL_PAD) * _SMALL_PAD
    return jnp.pad(flat, (0, padded - n))


def _unpack_small(flat, like):
    out, off = {}, 0
    for n in SMALL_NAMES:
        size = like[n].size
        out[n] = flat[off:off + size].reshape(like[n].shape)
        off += size
    return out, flat[off:]


def _flat_rows(a):
    return a.reshape(-1, a.shape[-1])


def _chip_sums(glayer, owner):
    parts = [glayer[n] for n in BIG_NAMES]
    flat = [_flat_rows(a) for a in parts]
    recv = _sibling_push(flat, [owner] * len(flat), "grad_push_layer%d" % owner)
    return [_add_pair(f, r, owner, "grad_pair_add").reshape(a.shape) for f, r, a in zip(flat, recv, parts)]


def kernel(x, p, mix_norm_g, w_in, ssm_a_re, ssm_a_im, ssm_log_dt, ssm_b_re, ssm_b_im, ssm_c_re, ssm_c_im, ssm_d, ssm_w_glu, ssm_b_glu, q_norm_g, k_norm_g, w_out, ple_norm_g, w_ple_gate, w_ple_proj, loss_target, m_mix_norm_g, m_w_in, m_ssm_a_re, m_ssm_a_im, m_ssm_log_dt, m_ssm_b_re, m_ssm_b_im, m_ssm_c_re, m_ssm_c_im, m_ssm_d, m_ssm_w_glu, m_ssm_b_glu, m_q_norm_g, m_k_norm_g, m_w_out, m_ple_norm_g, m_w_ple_gate, m_w_ple_proj, v_mix_norm_g, v_w_in, v_ssm_a_re, v_ssm_a_im, v_ssm_log_dt, v_ssm_b_re, v_ssm_b_im, v_ssm_c_re, v_ssm_c_im, v_ssm_d, v_ssm_w_glu, v_ssm_b_glu, v_q_norm_g, v_k_norm_g, v_w_out, v_ple_norm_g, v_w_ple_gate, v_w_ple_proj):
    args = dict(locals())
    names = ("mix_norm_g", "w_in", "ssm_a_re", "ssm_a_im", "ssm_log_dt", "ssm_b_re", "ssm_b_im", "ssm_c_re", "ssm_c_im",
             "ssm_d", "ssm_w_glu", "ssm_b_glu", "q_norm_g", "k_norm_g", "w_out", "ple_norm_g", "w_ple_gate", "w_ple_proj")
    w = {n: args[n] for n in names}
    m = {n: args["m_" + n] for n in names}
    v = {n: args["v_" + n] for n in names}

    w_in_halves = w["w_in"].astype(BF16).reshape(2 * N_LAYERS, D_MODEL // 2, IN_SHARD)
    w_in0 = _chip_gather([w_in_halves], "w_in_gather")[0].reshape(N_CHIPS, D_MODEL, IN_SHARD)
    rest_local = [w_in_halves] + [w[n].astype(BF16) for n in BIG_NAMES[1:]]
    sm = {n: w[n] for n in SMALL_NAMES}
    loss, dx, gbig, gsm, got1 = _local_step(x[0], p[:, 0], loss_target[0], sm, w_in0, rest_local, [2, 0, 0, 0, 0],
                                            layer1_hook=lambda g1: (_chip_sums(g1, 1), 1))

    nb = len(BIG_NAMES)
    small = _pack_small(gsm, [loss]).reshape(2, N_CHIPS * SUBLANES, -1)
    flat0 = [_flat_rows(gbig[0][n]) for n in BIG_NAMES]
    recv = _sibling_push(flat0 + [small], [0] * nb + [None], "grad_push_layer0")
    chip0 = [_add_pair(f, r, 0, "grad_pair_add").reshape(gbig[0][n].shape) for f, r, n in zip(flat0, recv, BIG_NAMES)]
    chip_small = _add_half(small, recv[-1], F32, "grad_half_add").reshape(N_CHIPS, SUBLANES, -1)
    got0 = _chip_scatter(chip0 + [chip_small], [0] * nb + [None], "grad_chip_scatter")
    tot1 = [_sum4(a, 1, "grad_chip_sum") for a in got1]
    tot0 = [_sum4(a, o, "grad_chip_sum") for a, o in zip(got0, [0] * nb + [None])]
    other, (small_mine,) = _sibling_join(tot0[:nb], tot1, [tot0[nb]], "grad_sibling_join")
    small_all = _chip_gather([small_mine], "small_grad_gather")[0]
    small_tot = small_all.transpose(1, 0, 2, 3).reshape(-1)
    on_core0 = lax.axis_index("c") == 0
    g = {n: jnp.where(on_core0, jnp.stack([t0, ot]), jnp.stack([ot, t1])).reshape(w[n].shape)
         for n, t0, t1, ot in zip(BIG_NAMES, tot0, tot1, other)}
    g_small, rest = _unpack_small(small_tot, sm)
    g.update(g_small)
    loss = rest[0]

    delta, new_m, new_v = {}, {}, {}
    for n in BIG_NAMES:
        lanes = w[n].shape[-1]
        outs = _adamw(_as_rows(w[n], lanes), _as_rows(g[n], lanes), _as_rows(m[n], lanes), _as_rows(v[n], lanes), "adamw_" + n)
        delta[n], new_m[n], new_v[n] = [o.reshape(w[n].shape) for o in outs]
    for group, per_layer in ((SMALL_4D, True), (tuple(n for n in SMALL_NAMES if n not in SMALL_4D), False)):
        outs = _adamw_many(*[[d[n] for n in group] for d in (w, g, m, v)], "adamw_small_4d" if per_layer else "adamw_small", per_layer)
        for d, o in zip((delta, new_m, new_v), outs):
            d.update(zip(group, o))

    return (loss, dx[None], *[g[n] for n in names], *[delta[n] for n in names],
            *[new_m[n] for n in names], *[new_v[n] for n in names])
```

```python
import functools
import math

import jax
import jax.numpy as jnp
from jax import lax
from jax.experimental import pallas as pl
from jax.experimental.pallas import tpu as pltpu

F32 = jnp.float32
BF16 = jnp.bfloat16

D_MODEL = 1024
N_LAYERS = 2
N_CHIPS = 4
IN_COLS = 3072
IN_SHARD = IN_COLS // N_CHIPS
SSM_WIDTH = 512
SSM_GROUP = 16
SSM_GROUPS = 32
SSM_STATE = 64
N_STATES = SSM_GROUPS * SSM_STATE
SSM_CHUNKS = 4
CH_W = SSM_WIDTH // SSM_CHUNKS
CH_S = N_STATES // SSM_CHUNKS
ATTN_WIDTH = 512
HEAD_DIM = 64
PLE_DIM = 256
ROW_SHARD = 256
RMS_EPS = 1e-6
ATTN_SCALE = HEAD_DIM ** -0.5
ATTN_BLOCK = 128
EXP_ZERO = -87.5
SUBLANES = 8
V7X_VMEM_LIMIT = 52 * 1024 * 1024

ADAM_LR = 0.001
ADAM_B1 = 0.9
ADAM_B2 = 0.999
ADAM_EPS = 1e-08
ADAM_WD = 0.01
ADAM_STEP = 10

MESH = pl.DeviceIdType.MESH
ANY = pl.BlockSpec(memory_space=pl.ANY)


def _cparams(n_grid=0, parallel=0):
    sem = tuple(["parallel"] * parallel + ["arbitrary"] * (n_grid - parallel))
    return pltpu.CompilerParams(dimension_semantics=sem, vmem_limit_bytes=V7X_VMEM_LIMIT)


def _dot(a, b):
    return jnp.dot(a, b, preferred_element_type=F32)


def _dot_nt(a, b):
    return lax.dot_general(a, b, (((1,), (1,)), ((), ())), preferred_element_type=F32)


def _dot_tn(a, b):
    return lax.dot_general(a, b, (((0,), (0,)), ((), ())), preferred_element_type=F32)


def _split_hilo(a):
    hi = a.astype(BF16)
    lo = (a - hi.astype(F32)).astype(BF16)
    return hi, lo


def _dot_hilo(a, b):
    hi, lo = _split_hilo(a)
    return _dot(hi, b) + _dot(lo, b)


def _sigmoid(x):
    return 0.5 * (jnp.tanh(0.5 * x) + 1.0)


_GELU_C = math.sqrt(2.0 / math.pi)


def _gelu(x):
    return 0.5 * x * (1.0 + jnp.tanh(_GELU_C * (x + 0.044715 * (x * x * x))))


def _gelu_grad(x):
    t = jnp.tanh(_GELU_C * (x + 0.044715 * (x * x * x)))
    return 0.5 * (1.0 + t) + 0.5 * x * (1.0 - t * t) * (_GELU_C * (1.0 + 3.0 * 0.044715 * (x * x)))


def _row_tile(s, want):
    for t in range(min(s, want), 7, -1):
        if s % t == 0 and t % SUBLANES == 0:
            return t
    return s


def _coords():
    return lax.axis_index("x"), lax.axis_index("y"), lax.axis_index("c")


def _other_chips(x, y):
    return [(1 - x, y), (x, 1 - y), (1 - x, 1 - y)]


def _remote(src, dst, send_sem, recv_sem, dev):
    return pltpu.make_async_remote_copy(src_ref=src, dst_ref=dst, send_sem=send_sem, recv_sem=recv_sem,
                                        device_id=dev, device_id_type=MESH)


def _set_block(buf, block, index):
    return lax.dynamic_update_index_in_dim(buf, block, index, 0)


def _gather_sems(n):
    return [pltpu.SemaphoreType.DMA((3 * n,)) for _ in range(4)]


def _gather_copies(ins, bases, outs, sems):
    send_sems, recv_sems, fwd_send, fwd_recv = sems
    x, y, c = _coords()
    me_chip = 2 * x + y
    sibling = (x, y, 1 - c)
    first, landed, passed, from_sibling = [], [], [], []
    for k in range(len(ins)):
        for j, (cx, cy) in enumerate(_other_chips(x, y)):
            i = 3 * k + j
            first.append(_remote(ins[k].at[bases[k] + c], outs[k].at[me_chip, c], send_sems.at[i], recv_sems.at[i], (cx, cy, c)))
            blk = outs[k].at[2 * cx + cy, c]
            landed.append(_remote(blk, blk, send_sems.at[i], recv_sems.at[i], (cx, cy, c)))
            passed.append(_remote(blk, blk, fwd_send.at[i], fwd_recv.at[i], sibling))
            blk = outs[k].at[2 * cx + cy, 1 - c]
            from_sibling.append(_remote(blk, blk, fwd_send.at[i], fwd_recv.at[i], sibling))
    return first, landed, passed, from_sibling


def _gather_start(ins, bases, outs, sems):
    for cp in _gather_copies(ins, bases, outs, sems)[0]:
        cp.start()


def _gather_finish(ins, bases, outs, sems):
    first, landed, passed, from_sibling = _gather_copies(ins, bases, outs, sems)
    for arrived, forward in zip(landed, passed):
        arrived.wait_recv()
        forward.start()
    for cp in from_sibling:
        cp.wait_recv()
    for cp in first + passed:
        cp.wait_send()


def _gather_outputs(arrs):
    return [jax.ShapeDtypeStruct((N_CHIPS, 2) + a.shape[1:], a.dtype) for a in arrs]


def _gather_own(outs, arrs, bases):
    me_chip = 2 * lax.axis_index("x") + lax.axis_index("y")
    return [_set_block(o, lax.slice_in_dim(a, b, b + 2, axis=0), me_chip) for o, a, b in zip(outs, arrs, bases)]


def _chip_gather(arrs, name, bases=None):
    n = len(arrs)
    bases = [0] * n if bases is None else bases

    def body(*refs):
        ins, outs, sems = refs[:n], refs[n:2 * n], refs[2 * n:]
        _gather_start(ins, bases, outs, sems)
        _gather_finish(ins, bases, outs, sems)

    outs = pl.pallas_call(
        body, name=name, out_shape=_gather_outputs(arrs),
        in_specs=[ANY] * n, out_specs=[ANY] * n, scratch_shapes=_gather_sems(n),
    )(*arrs)
    return _gather_own(outs, arrs, bases)


def _sibling_push(arrs, owners, name):
    n = len(arrs)

    def body(*refs):
        ins, outs = refs[:n], refs[n:2 * n]
        send_sems, recv_sems = refs[2 * n:]
        x, y, c = _coords()
        cps = [_remote(ins[k].at[1 - c] if owners[k] is None else ins[k], outs[k], send_sems.at[k], recv_sems.at[k],
                       (x, y, 1 - c)) for k in range(n)]
        for o in (None, 0, 1):
            mine = [cp for cp, ow in zip(cps, owners) if ow == o]
            if not mine:
                continue
            if o is None:
                for cp in mine:
                    cp.start()
                for cp in mine:
                    cp.wait_recv()
                for cp in mine:
                    cp.wait_send()
            else:
                @pl.when(c == 1 - o)
                def _():
                    for cp in mine:
                        cp.start()
                    for cp in mine:
                        cp.wait_send()

                @pl.when(c == o)
                def _():
                    for cp in mine:
                        cp.wait_recv()

    return pl.pallas_call(
        body, name=name,
        out_shape=[jax.ShapeDtypeStruct(a.shape[1:] if ow is None else a.shape, a.dtype) for a, ow in zip(arrs, owners)],
        in_specs=[ANY] * n, out_specs=[ANY] * n,
        scratch_shapes=[pltpu.SemaphoreType.DMA((n,)), pltpu.SemaphoreType.DMA((n,))],
    )(*arrs)


def _sibling_join(tot0, tot1, sym, name):
    nb, ns = len(tot0), len(sym)
    n = nb + ns

    def body(*refs):
        ins0, ins1, ins_s = refs[:nb], refs[nb:2 * nb], refs[2 * nb:2 * nb + ns]
        outs_b, outs_s = refs[2 * nb + ns:3 * nb + ns], refs[3 * nb + ns:3 * nb + 2 * ns]
        send_sems, recv_sems = refs[3 * nb + 2 * ns:]
        x, y, c = _coords()
        sibling = (x, y, 1 - c)

        def big(src):
            return [_remote(src[k], outs_b[k], send_sems.at[k], recv_sems.at[k], sibling) for k in range(nb)]

        @pl.when(c == 0)
        def _():
            for cp in big(ins0):
                cp.start()

        @pl.when(c == 1)
        def _():
            for cp in big(ins1):
                cp.start()

        halves = [_remote(ins_s[k], outs_s[k].at[c], send_sems.at[nb + k], recv_sems.at[nb + k], sibling) for k in range(ns)]
        for cp in halves:
            cp.start()
        for cp in big(ins0):
            cp.wait_recv()
        for k in range(ns):
            blk = outs_s[k].at[1 - c]
            _remote(blk, blk, send_sems.at[nb + k], recv_sems.at[nb + k], sibling).wait_recv()
        for cp in big(ins0) + halves:
            cp.wait_send()

    outs = pl.pallas_call(
        body, name=name,
        out_shape=[jax.ShapeDtypeStruct(a.shape, a.dtype) for a in tot0]
        + [jax.ShapeDtypeStruct((2,) + a.shape, a.dtype) for a in sym],
        in_specs=[ANY] * (2 * nb + ns), out_specs=[ANY] * n,
        scratch_shapes=[pltpu.SemaphoreType.DMA((n,)), pltpu.SemaphoreType.DMA((n,))],
    )(*tot0, *tot1, *sym)
    c = lax.axis_index("c")
    return outs[:nb], [_set_block(o, a, c) for o, a in zip(outs[nb:], sym)]


def _scatter_sems(n):
    return [pltpu.SemaphoreType.DMA((3 * n,)), pltpu.SemaphoreType.DMA((3 * n,))]


def _scatter_copies(ins, outs, sems):
    send_sems, recv_sems = sems
    x, y, c = _coords()
    me_chip = 2 * x + y
    sends, arrivals = [], []
    for k in range(len(ins)):
        for j, (cx, cy) in enumerate(_other_chips(x, y)):
            i = 3 * k + j
            sends.append(_remote(ins[k].at[2 * cx + cy], outs[k].at[me_chip], send_sems.at[i], recv_sems.at[i], (cx, cy, c)))
            blk = outs[k].at[2 * cx + cy]
            arrivals.append(_remote(blk, blk, send_sems.at[i], recv_sems.at[i], (cx, cy, c)))
    return sends, arrivals


def _by_owner(owners, fn):
    c = lax.axis_index("c")
    for o in (None, 0, 1):
        idx = [k for k, ow in enumerate(owners) if ow == o]
        if not idx:
            continue
        if o is None:
            fn(idx)
        else:
            pl.when(c == o)(functools.partial(fn, idx))


def _scatter_start(ins, outs, owners, sems):
    sends, _ = _scatter_copies(ins, outs, sems)

    def go(idx):
        for k in idx:
            for cp in sends[3 * k:3 * k + 3]:
                cp.start()

    _by_owner(owners, go)


def _scatter_finish(ins, outs, owners, sems):
    sends, arrivals = _scatter_copies(ins, outs, sems)

    def go(idx):
        for k in idx:
            for cp in arrivals[3 * k:3 * k + 3]:
                cp.wait_recv()
        for k in idx:
            for cp in sends[3 * k:3 * k + 3]:
                cp.wait_send()

    _by_owner(owners, go)


def _scatter_own(outs, arrs):
    me_chip = 2 * lax.axis_index("x") + lax.axis_index("y")
    return [_set_block(o, lax.dynamic_index_in_dim(a, me_chip, 0, keepdims=False), me_chip) for o, a in zip(outs, arrs)]


def _chip_scatter(arrs, owners, name):
    n = len(arrs)

    def body(*refs):
        ins, outs, sems = refs[:n], refs[n:2 * n], refs[2 * n:]
        _scatter_start(ins, outs, owners, sems)
        _scatter_finish(ins, outs, owners, sems)

    outs = pl.pallas_call(
        body, name=name,
        out_shape=[jax.ShapeDtypeStruct(a.shape, a.dtype) for a in arrs],
        in_specs=[ANY] * n, out_specs=[ANY] * n, scratch_shapes=_scatter_sems(n),
    )(*arrs)
    return _scatter_own(outs, arrs)


def _as_rows(a, lanes):
    return a.reshape(-1, lanes)


def _add_half(full, recv, out_dtype, name):
    _, r, cdim = full.shape
    tr = _row_tile(r, 512)

    def body(c_ref, a_ref, b_ref, o_ref):
        o_ref[...] = (a_ref[...].astype(F32) + b_ref[...].astype(F32)).astype(out_dtype)

    c = lax.axis_index("c").astype(jnp.int32).reshape(1)
    return pl.pallas_call(
        body, name=name,
        grid_spec=pltpu.PrefetchScalarGridSpec(
            num_scalar_prefetch=1, grid=(r // tr,),
            in_specs=[pl.BlockSpec((None, tr, cdim), lambda i, c_ref: (c_ref[0], i, 0)),
                      pl.BlockSpec((tr, cdim), lambda i, c_ref: (i, 0))],
            out_specs=pl.BlockSpec((tr, cdim), lambda i, c_ref: (i, 0))),
        out_shape=jax.ShapeDtypeStruct((r, cdim), out_dtype),
        compiler_params=_cparams(1),
    )(c, full, recv)


def _owner_flag(owner):
    if owner is None:
        return jnp.ones((1,), jnp.int32)
    return (lax.axis_index("c") == owner).astype(jnp.int32).reshape(1)


def _add_pair(a, b, owner, name):
    r, cdim = a.shape
    tr = _row_tile(r, 512)

    def body(f_ref, a_ref, b_ref, o_ref):
        o_ref[...] = (a_ref[...].astype(F32) + b_ref[...].astype(F32)).astype(o_ref.dtype)

    spec = pl.BlockSpec((tr, cdim), lambda i, f_ref: (i * f_ref[0], 0))
    return pl.pallas_call(
        body, name=name,
        grid_spec=pltpu.PrefetchScalarGridSpec(num_scalar_prefetch=1, grid=(r // tr,), in_specs=[spec] * 2, out_specs=spec),
        out_shape=jax.ShapeDtypeStruct((r, cdim), a.dtype), compiler_params=_cparams(1))(_owner_flag(owner), a, b)


def _sum4(parts, owner, name):
    _, r, cdim = parts.shape
    tr = _row_tile(r, 512)

    def body(f_ref, p_ref, o_ref):
        acc = p_ref[0].astype(F32) + p_ref[1].astype(F32)
        acc = acc + p_ref[2].astype(F32)
        o_ref[...] = acc + p_ref[3].astype(F32)

    return pl.pallas_call(
        body, name=name,
        grid_spec=pltpu.PrefetchScalarGridSpec(
            num_scalar_prefetch=1, grid=(r // tr,),
            in_specs=[pl.BlockSpec((N_CHIPS, tr, cdim), lambda i, f_ref: (0, i * f_ref[0], 0))],
            out_specs=pl.BlockSpec((tr, cdim), lambda i, f_ref: (i * f_ref[0], 0))),
        out_shape=jax.ShapeDtypeStruct((r, cdim), F32),
        compiler_params=_cparams(1),
    )(_owner_flag(owner), parts)


def _adamw_math(w, g, m, v):
    c1 = 1.0 - ADAM_B1 ** ADAM_STEP
    c2 = 1.0 - ADAM_B2 ** ADAM_STEP
    nm = ADAM_B1 * m + (1.0 - ADAM_B1) * g
    nv = ADAM_B2 * v + (1.0 - ADAM_B2) * (g * g)
    delta = -ADAM_LR * ((nm / c1) / (jnp.sqrt(nv / c2) + ADAM_EPS) + ADAM_WD * w)
    return delta, nm, nv


def _adamw(w, g, m, v, name):
    r, cdim = w.shape
    tr = _row_tile(r, 256)

    def body(w_ref, g_ref, m_ref, v_ref, d_ref, nm_ref, nv_ref):
        d_ref[...], nm_ref[...], nv_ref[...] = _adamw_math(w_ref[...], g_ref[...], m_ref[...], v_ref[...])

    spec = pl.BlockSpec((tr, cdim), lambda i: (i, 0))
    return pl.pallas_call(
        body, name=name, grid=(r // tr,),
        in_specs=[spec] * 4, out_specs=[spec] * 3,
        out_shape=[jax.ShapeDtypeStruct((r, cdim), F32)] * 3,
        compiler_params=_cparams(1),
    )(w, g, m, v)


def _adamw_many(ws, gs, ms, vs, name, per_layer):
    n = len(ws)

    def body(*refs):
        for k in range(n):
            w, g, m, v = (refs[j * n + k][...] for j in range(4))
            outs = _adamw_math(w, g, m, v)
            for j in range(3):
                refs[(4 + j) * n + k][...] = outs[j]

    shapes = [jax.ShapeDtypeStruct(w.shape, F32) for w in ws]
    if per_layer:
        specs = [pl.BlockSpec((None,) + w.shape[1:], lambda l, nd=w.ndim: (l,) + (0,) * (nd - 1)) for w in ws]
        call = pl.pallas_call(body, name=name, grid=(N_LAYERS,), in_specs=specs * 4, out_specs=specs * 3,
                              out_shape=shapes * 3, compiler_params=_cparams(1))
    else:
        call = pl.pallas_call(body, name=name, out_shape=shapes * 3, compiler_params=_cparams())
    outs = call(*ws, *gs, *ms, *vs)
    return outs[0:n], outs[n:2 * n], outs[2 * n:3 * n]


def _discretise(a_re, a_im, log_dt, b_re, b_im):
    dt = jnp.exp(log_dt)
    mag = jnp.exp(a_re * dt)
    ab_re = mag * jnp.cos(a_im * dt)
    ab_im = mag * jnp.sin(a_im * dt)
    num_re = ab_re - 1.0
    num_im = ab_im
    den = a_re * a_re + a_im * a_im
    f_re = (num_re * a_re + num_im * a_im) / den
    f_im = (num_im * a_re - num_re * a_im) / den
    bb_re = f_re * b_re - f_im * b_im
    bb_im = f_re * b_im + f_im * b_re
    return ab_re, ab_im, bb_re, bb_im


def _disc_shapes():
    col = jax.ShapeDtypeStruct((1, N_STATES), F32)
    mat = jax.ShapeDtypeStruct((SSM_GROUP, N_STATES), F32)
    return col, mat


def _disc_fwd(a_re, a_im, log_dt, b_re, b_im):
    col, mat = _disc_shapes()

    def body(ar, ai, ld, br, bi, o0, o1, o2, o3):
        outs = _discretise(ar[...], ai[...], ld[...], br[...], bi[...])
        for o, val in zip((o0, o1, o2, o3), outs):
            o[...] = val

    return pl.pallas_call(body, name="ssm_discretise", out_shape=[col, col, mat, mat],
                          compiler_params=_cparams())(a_re, a_im, log_dt, b_re, b_im)


def _disc_bwd(a_re, a_im, log_dt, b_re, b_im, g_ab_re, g_ab_im, g_bb_re, g_bb_im):
    col, mat = _disc_shapes()

    def body(ar, ai, ld, br, bi, g0, g1, g2, g3, o0, o1, o2, o3, o4):
        _, vjp = jax.vjp(_discretise, ar[...], ai[...], ld[...], br[...], bi[...])
        grads = vjp((g0[...], g1[...], g2[...], g3[...]))
        for o, val in zip((o0, o1, o2, o3, o4), grads):
            o[...] = val

    return pl.pallas_call(body, name="ssm_discretise_bwd", out_shape=[col, col, col, mat, mat],
                          compiler_params=_cparams())(a_re, a_im, log_dt, b_re, b_im, g_ab_re, g_ab_im, g_bb_re, g_bb_im)


def _cmul(ar, ai, br, bi):
    return ar * br - ai * bi, ar * bi + ai * br


def _scan_powers(ab_re, ab_im, length, reverse):
    br = ab_re.reshape(1, N_STATES)
    bi = -ab_im.reshape(1, N_STATES) if reverse else ab_im.reshape(1, N_STATES)
    ks = (length - jnp.arange(length)) if reverse else (jnp.arange(length) + 1)
    pr = jnp.ones((length, N_STATES), F32)
    pi = jnp.zeros((length, N_STATES), F32)
    for bit in range(length.bit_length()):
        take = ((ks >> bit) & 1)[:, None] == 1
        mr, mi = _cmul(pr, pi, br, bi)
        pr, pi = jnp.where(take, mr, pr), jnp.where(take, mi, pi)
        br, bi = _cmul(br, bi, br, bi)
    split = lambda t: t.reshape(length, SSM_CHUNKS, CH_S).transpose(1, 0, 2)
    return jnp.concatenate([split(pr), split(pi)], axis=-1)


def _interleave_chunks(v):
    rows, width = v.shape
    return pltpu.einshape("cjw->jcw", v.reshape(SUBLANES, rows // SUBLANES, width)).reshape(rows, width)


def _time_order(v):
    rows, width = v.shape
    return pltpu.einshape("jcw->cjw", v.reshape(rows // SUBLANES, SUBLANES, width)).reshape(rows, width)


def _block_diag_in(bb):
    t = bb.reshape(SSM_GROUP, SSM_CHUNKS, 8, SSM_STATE)
    eye = jnp.eye(8, dtype=bb.dtype)
    return jnp.einsum("hjgp,gk->jghkp", t, eye).reshape(SSM_CHUNKS, CH_W, CH_S)


def _block_diag_in_t(d):
    t = d.reshape(SSM_CHUNKS, 8, SSM_GROUP, 8, SSM_STATE)
    return jnp.einsum("jghgp->hjgp", t).reshape(SSM_GROUP, N_STATES)


def _block_diag_out(c):
    t = c.reshape(SSM_CHUNKS, 8, SSM_GROUP, SSM_STATE)
    eye = jnp.eye(8, dtype=c.dtype)
    return jnp.einsum("jghp,gk->jgpkh", t, eye).reshape(SSM_CHUNKS, CH_S, CH_W)


def _block_diag_out_t(d):
    t = d.reshape(SSM_CHUNKS, 8, SSM_STATE, 8, SSM_GROUP)
    return jnp.einsum("jgpgh->jghp", t).reshape(SSM_GROUPS, SSM_GROUP, SSM_STATE)


def _head_ones():
    r = jnp.arange(ATTN_WIDTH) // HEAD_DIM
    return jnp.where(r[:, None] == r[None, :], 1.0 / HEAD_DIM, 0.0).astype(BF16)


def _in_proj(h, g1, w_in_l, qg, kg):
    s = h.shape[0]
    tm = _row_tile(s, 256)

    def body(h_ref, g_ref, w_ref, qg_ref, kg_ref, ones_ref, proj_ref, qkv_ref):
        x = h_ref[...]
        r = lax.rsqrt(jnp.mean(x * x, axis=-1, keepdims=True) + RMS_EPS)
        hn = (x * r * g_ref[...]).astype(BF16)
        for sh in range(N_CHIPS):
            proj_ref[:, IN_SHARD * sh:IN_SHARD * (sh + 1)] = _dot(hn, w_ref[sh])
        ones = ones_ref[...]
        q = proj_ref[:, 1024:1536]
        k = proj_ref[:, 1536:2048]
        rq = lax.rsqrt(_dot_hilo(q * q, ones) + RMS_EPS)
        rk = lax.rsqrt(_dot_hilo(k * k, ones) + RMS_EPS)
        qkv_ref[:, 0:512] = (q * rq * qg_ref[...] * ATTN_SCALE).astype(BF16)
        qkv_ref[:, 512:1024] = (k * rk * kg_ref[...]).astype(BF16)
        qkv_ref[:, 1024:1536] = proj_ref[:, 2048:2560].astype(BF16)

    full = lambda shape: pl.BlockSpec(shape, lambda i: (0,) * len(shape))
    return pl.pallas_call(
        body, name="in_proj", grid=(s // tm,),
        in_specs=[pl.BlockSpec((tm, D_MODEL), lambda i: (i, 0)), full((1, D_MODEL)),
                  full((N_CHIPS, D_MODEL, IN_SHARD)),
                  full((1, ATTN_WIDTH)), full((1, ATTN_WIDTH)), full((ATTN_WIDTH, ATTN_WIDTH))],
        out_specs=[pl.BlockSpec((tm, IN_COLS), lambda i: (i, 0)), pl.BlockSpec((tm, 3 * ATTN_WIDTH), lambda i: (i, 0))],
        out_shape=[jax.ShapeDtypeStruct((s, IN_COLS), F32), jax.ShapeDtypeStruct((s, 3 * ATTN_WIDTH), BF16)],
        compiler_params=_cparams(1),
    )(h, g1, w_in_l, qg, kg, _head_ones())


def _row_bcast(ref, k, lo):
    return jnp.broadcast_to(ref[pl.ds(k, 1), lo:lo + CH_S], (SUBLANES, CH_S))


def _chunk_scan(x_ref, tab_ref, carry_ref, length, reverse, tail=None):
    row = lax.broadcasted_iota(jnp.int32, (SUBLANES, CH_S), 0)
    one, full = (length - 1, 0) if reverse else (0, length - 1)
    ar, ai = _row_bcast(tab_ref, one, 0), _row_bcast(tab_ref, one, CH_S)
    fr, fi = _row_bcast(tab_ref, full, 0), _row_bcast(tab_ref, full, CH_S)
    step = lambda jj: (length - 1 - jj) if reverse else jj

    def local(jj, carry):
        cr, ci = carry
        r0 = pl.multiple_of(step(jj) * SUBLANES, SUBLANES)
        xr = x_ref[pl.ds(r0, SUBLANES), 0:CH_S] + (ar * cr - ai * ci)
        xi = x_ref[pl.ds(r0, SUBLANES), CH_S:2 * CH_S] + (ar * ci + ai * cr)
        x_ref[pl.ds(r0, SUBLANES), 0:CH_S] = xr
        x_ref[pl.ds(r0, SUBLANES), CH_S:2 * CH_S] = xi
        return xr, xi

    zero = jnp.zeros((SUBLANES, CH_S), F32)
    er, ei = lax.fori_loop(0, length, local, (zero, zero))

    first, shift = (SUBLANES - 1, SUBLANES - 1) if reverse else (0, 1)
    hr = jnp.where(row == first, carry_ref[:, 0:CH_S], 0.0)
    hi = jnp.where(row == first, carry_ref[:, CH_S:2 * CH_S], 0.0)
    sr, si = pltpu.roll(er, shift, 0), pltpu.roll(ei, shift, 0)
    for k in range(1, SUBLANES):
        tr, ti = pltpu.roll(hr, shift, 0), pltpu.roll(hi, shift, 0)
        here = row == ((SUBLANES - 1 - k) if reverse else k)
        hr, hi = (jnp.where(here, fr * tr - fi * ti + sr, hr), jnp.where(here, fr * ti + fi * tr + si, hi))
    last = 0 if reverse else SUBLANES - 1
    outr, outi = fr * hr - fi * hi + er, fr * hi + fi * hr + ei
    carry_ref[:, 0:CH_S] = jnp.broadcast_to(outr[last:last + 1, :], (SUBLANES, CH_S))
    carry_ref[:, CH_S:2 * CH_S] = jnp.broadcast_to(outi[last:last + 1, :], (SUBLANES, CH_S))

    def fix(jj, carry):
        j = step(jj)
        r0 = pl.multiple_of(j * SUBLANES, SUBLANES)
        pr, pi = _row_bcast(tab_ref, j, 0), _row_bcast(tab_ref, j, CH_S)
        xr = x_ref[pl.ds(r0, SUBLANES), 0:CH_S] + (pr * hr - pi * hi)
        xi = x_ref[pl.ds(r0, SUBLANES), CH_S:2 * CH_S] + (pr * hi + pi * hr)
        x_ref[pl.ds(r0, SUBLANES), 0:CH_S] = xr
        x_ref[pl.ds(r0, SUBLANES), CH_S:2 * CH_S] = xi
        if tail is None:
            return carry
        return tail(r0, xr, xi, carry)

    return fix, (hr, hi)


def _ssm_scan_fwd(proj, wb, tab, wc, gather=None, gather_bases=None):
    s = proj.shape[0]
    tm = _row_tile(s, 512)
    nt = s // tm
    length = tm // SUBLANES
    gather = [] if gather is None else gather
    ng = len(gather)

    def body(*refs):
        u_ref, wb_ref, tab_ref, wc_ref = refs[0:4]
        g_ins = refs[4:4 + ng]
        xs_ref, y_ref = refs[4 + ng:6 + ng]
        g_outs = refs[6 + ng:6 + 2 * ng]
        carry_ref = refs[6 + 2 * ng]
        sems = refs[7 + 2 * ng:]
        j, i = pl.program_id(0), pl.program_id(1)

        @pl.when(i == 0)
        def _():
            carry_ref[...] = jnp.zeros_like(carry_ref)

        if ng:
            @pl.when(jnp.logical_and(j == 0, i == 0))
            def _():
                _gather_start(g_ins, gather_bases, g_outs, sems)

        xs_ref[...] = _dot(_interleave_chunks(u_ref[...]).astype(BF16), wb_ref[...])
        fix, start = _chunk_scan(xs_ref, tab_ref, carry_ref, length, reverse=False)
        lax.fori_loop(0, length, fix, start, unroll=2)
        y_ref[...] = _time_order(_dot(xs_ref[...].astype(BF16), wc_ref[...]))

        if ng:
            @pl.when(jnp.logical_and(j == SSM_CHUNKS - 1, i == nt - 1))
            def _():
                _gather_finish(g_ins, gather_bases, g_outs, sems)

    outs = pl.pallas_call(
        body, name="ssm_scan_gather" if ng else "ssm_scan", grid=(SSM_CHUNKS, nt),
        in_specs=[pl.BlockSpec((tm, CH_W), lambda j, i: (i, j)),
                  pl.BlockSpec((None, CH_W, 2 * CH_S), lambda j, i: (j, 0, 0)),
                  pl.BlockSpec((None, length, 2 * CH_S), lambda j, i: (j, 0, 0)),
                  pl.BlockSpec((None, 2 * CH_S, CH_W), lambda j, i: (j, 0, 0))] + [ANY] * ng,
        out_specs=[pl.BlockSpec((None, tm, 2 * CH_S), lambda j, i: (j, i, 0)),
                   pl.BlockSpec((tm, CH_W), lambda j, i: (i, j))] + [ANY] * ng,
        out_shape=[jax.ShapeDtypeStruct((SSM_CHUNKS, s, 2 * CH_S), F32), jax.ShapeDtypeStruct((s, SSM_WIDTH), F32)]
        + _gather_outputs(gather),
        scratch_shapes=[pltpu.VMEM((SUBLANES, 2 * CH_S), F32)] + (_gather_sems(ng) if ng else []),
        compiler_params=_cparams(2),
    )(proj, wb, tab, wc, *gather)
    return outs[0], outs[1], (_gather_own(outs[2:], gather, gather_bases) if ng else [])


def _glu_forward(y, u, d, wg_ref, bg):
    yf = y + d * u
    z = _gelu(yf)
    zb = z.astype(BF16)
    zz = jnp.concatenate([_dot(zb, wg_ref[sh]) for sh in range(N_CHIPS)], axis=-1) + bg
    return yf, z, zz[:, 0:SSM_WIDTH], zz[:, SSM_WIDTH:2 * SSM_WIDTH]


def _ssm_glu_fwd(y, proj, d, w_glu_l, b_glu):
    s = y.shape[0]
    tm = _row_tile(s, 512)

    def body(y_ref, u_ref, gs_ref, d_ref, wg_ref, bg_ref, o_ref):
        _, _, val, gate = _glu_forward(y_ref[...], u_ref[...], d_ref[...], wg_ref, bg_ref[...])
        gs = gs_ref[...]
        o_ref[...] = val * _sigmoid(gate) * (gs * _sigmoid(gs))

    row = lambda i: (i, 0)
    return pl.pallas_call(
        body, name="ssm_glu", grid=(s // tm,),
        in_specs=[pl.BlockSpec((tm, SSM_WIDTH), row), pl.BlockSpec((tm, SSM_WIDTH), row),
                  pl.BlockSpec((tm, SSM_WIDTH), lambda i: (i, 1)), pl.BlockSpec((1, SSM_WIDTH), lambda i: (0, 0)),
                  pl.BlockSpec((N_CHIPS, SSM_WIDTH, ROW_SHARD), lambda i: (0, 0, 0)),
                  pl.BlockSpec((1, 2 * SSM_WIDTH), lambda i: (0, 0))],
        out_specs=pl.BlockSpec((tm, SSM_WIDTH), row),
        out_shape=jax.ShapeDtypeStruct((s, SSM_WIDTH), F32),
        compiler_params=_cparams(1),
    )(y, proj, proj, d, w_glu_l, b_glu)


def _tri(kind):
    r = jnp.arange(ATTN_BLOCK)
    if kind == "suffix_incl":
        m = r[:, None] >= r[None, :]
    else:
        m = r[:, None] < r[None, :]
    return jnp.concatenate([m, jnp.ones_like(m)], axis=1).astype(BF16)


def _head_masks():
    lane = lax.broadcasted_iota(jnp.int32, (1, 2 * HEAD_DIM), 1)
    return [lane < HEAD_DIM, lane >= HEAD_DIM]


def _chain_step(t, base, n_sub, first, q_ref, k_ref, tri_ref, l_scr, per_chain):
    tb = ATTN_BLOCK
    row = lax.broadcasted_iota(jnp.int32, (tb, tb), 0)
    col = lax.broadcasted_iota(jnp.int32, (tb, tb), 1)
    masks = _head_masks()
    blks = [base + a - t for a in range(n_sub)]
    r0s = [pl.multiple_of(jnp.maximum(blk, 0) * tb, tb) for blk in blks]
    zs = []
    for a in range(n_sub):
        kb = k_ref[pl.ds(r0s[a], tb), :]
        qa = q_ref[a * tb:(a + 1) * tb, :]
        for mask in masks:
            zs.append(_dot_nt(jnp.where(mask, qa, jnp.zeros_like(qa)), kb))
    parts = []
    for z in zs:
        ls = jnp.minimum(-z, 0.0) - jnp.log(1.0 + jnp.exp(-jnp.abs(z)))
        if first:
            ls = jnp.where(col < row, ls, 0.0)
        parts.append(_split_hilo(ls))
    tri = tri_ref[...]
    sums = [_dot(hi, tri) + _dot(lo, tri) for hi, lo in parts]
    top = None
    ws = []
    for c, (z, sm) in enumerate(zip(zs, sums)):
        if first:
            lsum = jnp.zeros((tb, tb), F32)
        else:
            lsum = l_scr[c] + jnp.where(blks[c // 2] >= 0, 0.0, -1e30)
        w = jnp.exp(z + sm[:, 0:tb] + lsum)
        if first:
            w = jnp.where(col < row, w, 0.0)
        ws.append(w)
        lsum = lsum + sm[:, tb:2 * tb]
        l_scr[c] = lsum
        top = lsum if top is None else jnp.maximum(top, lsum)
    for c, (z, w) in enumerate(zip(zs, ws)):
        per_chain(c // 2, c % 2, c, r0s[c // 2], z, w)
    return jnp.max(top)


def _chain_sweep(base, n_sub, q_ref, k_ref, tri_ref, l_scr, per_chain):
    top = _chain_step(0, base, n_sub, True, q_ref, k_ref, tri_ref, l_scr, functools.partial(per_chain, 0))

    def cond(carry):
        t, top = carry
        return jnp.logical_and(t <= base + n_sub - 1, top > EXP_ZERO)

    def step(carry):
        t, _ = carry
        return t + 1, _chain_step(t, base, n_sub, False, q_ref, k_ref, tri_ref, l_scr, functools.partial(per_chain, t))

    steps, _ = lax.while_loop(cond, step, (jnp.int32(1), top))
    return steps


ATTN_SUB_FWD = 4
ATTN_SUB_BWD = 4


def _attn_fwd(qkv, proj, gather=None, gather_bases=None):
    s = qkv.shape[0]
    tb = ATTN_BLOCK
    n_sub = min(ATTN_SUB_FWD, s // tb)
    tq = n_sub * tb
    n_hp = ATTN_WIDTH // (2 * HEAD_DIM)
    gather = [] if gather is None else gather
    ng = len(gather)

    def body(*refs):
        q_ref, k_ref, v_ref, g_ref, tri_ref = refs[0:5]
        g_ins = refs[5:5 + ng]
        o_ref, ya_ref = refs[5 + ng:7 + ng]
        g_outs = refs[7 + ng:7 + 2 * ng]
        l_scr = refs[7 + 2 * ng]
        sems = refs[8 + 2 * ng:]
        i = pl.program_id(1)
        masks = _head_masks()
        o_ref[...] = jnp.zeros_like(o_ref)

        if ng:
            @pl.when(jnp.logical_and(pl.program_id(0) == 0, i == 0))
            def _():
                _gather_start(g_ins, gather_bases, g_outs, sems)

        def per_chain(t, a, h, c, r0, z, w):
            vb = v_ref[pl.ds(r0, tb), :]
            vb = jnp.where(masks[h], vb, jnp.zeros_like(vb))
            o_ref[a * tb:(a + 1) * tb, :] += _dot(w.astype(BF16), vb)

        _chain_sweep(i * n_sub, n_sub, q_ref, k_ref, tri_ref, l_scr, per_chain)
        g = g_ref[...]
        ya_ref[...] = o_ref[...] * (g * _sigmoid(g))

        if ng:
            @pl.when(jnp.logical_and(pl.program_id(0) == n_hp - 1, i == s // tq - 1))
            def _():
                _gather_finish(g_ins, gather_bases, g_outs, sems)

    hp_blk = lambda off: pl.BlockSpec((tq, 2 * HEAD_DIM), lambda hp, i: (i, off + hp))
    res = lambda off: pl.BlockSpec((s, 2 * HEAD_DIM), lambda hp, i: (0, off + hp))
    outs = pl.pallas_call(
        body, name="attn_fwd_gather" if ng else "attn_fwd", grid=(n_hp, s // tq),
        in_specs=[hp_blk(0), res(4), res(8), hp_blk(20), pl.BlockSpec((tb, 2 * tb), lambda hp, i: (0, 0))] + [ANY] * ng,
        out_specs=[hp_blk(0), hp_blk(0)] + [ANY] * ng,
        out_shape=[jax.ShapeDtypeStruct((s, ATTN_WIDTH), F32)] * 2 + _gather_outputs(gather),
        scratch_shapes=[pltpu.VMEM((2 * n_sub, tb, tb), F32)] + (_gather_sems(ng) if ng else []),
        compiler_params=_cparams(2),
    )(qkv, qkv, qkv, proj, _tri("suffix_incl"), *gather)
    return outs[0], outs[1], (_gather_own(outs[2:], gather, gather_bases) if ng else [])


def _rms_rows(x, g):
    r = lax.rsqrt(jnp.mean(x * x, axis=-1, keepdims=True) + RMS_EPS)
    return r, x * r * g


def _ple_forward(h1, p, g2, wpg_ref, wpp_ref):
    r2, hn2 = _rms_rows(h1, g2)
    hb = hn2.astype(BF16)
    gpre = _dot(hb[:, 0:ROW_SHARD], wpg_ref[0])
    for sh in range(1, N_CHIPS):
        gpre = gpre + _dot(hb[:, ROW_SHARD * sh:ROW_SHARD * (sh + 1)], wpg_ref[sh])
    gate = _sigmoid(gpre)
    pb = p.astype(BF16)
    pp = jnp.concatenate([_dot(pb, wpp_ref[sh]) for sh in range(N_CHIPS)], axis=-1)
    return r2, hb, gate, pp


def _colsum8(a):
    t = a.shape[0]
    return a.reshape(t // SUBLANES, SUBLANES, a.shape[1]).sum(axis=0)


def _sq_err_grad(y, target):
    e = y - target
    sq = _colsum8(e * e)
    part = sq[:, 0:128]
    for b in range(1, D_MODEL // 128):
        part = part + sq[:, 128 * b:128 * (b + 1)]
    return e / D_MODEL, part


def _out_ple(h, ys, ya, p, g2, w_out_l, w_pg_l, w_pp_l, target=None):
    s = h.shape[0]
    tm = _row_tile(s, 256)
    last = target is not None

    def body(*refs):
        h_ref, ys_ref, ya_ref, p_ref, g_ref, wo_ref, wpg_ref, wpp_ref = refs[0:8]
        h1_ref, h2_ref = refs[8 + last], refs[9 + last]
        ysb = ys_ref[...].astype(BF16)
        yab = ya_ref[...].astype(BF16)
        h1 = h_ref[...]
        for sh, src in enumerate((ysb[:, 0:ROW_SHARD], ysb[:, ROW_SHARD:], yab[:, 0:ROW_SHARD], yab[:, ROW_SHARD:])):
            h1 = h1 + _dot(src, wo_ref[sh])
        _, _, gate, pp = _ple_forward(h1, p_ref[...], g_ref[...], wpg_ref, wpp_ref)
        h1_ref[...] = h1
        h2 = h1 + gate * pp
        if last:
            acc_ref = refs[11]

            @pl.when(pl.program_id(0) == 0)
            def _():
                acc_ref[...] = jnp.zeros_like(acc_ref)

            h2_ref[...], part = _sq_err_grad(h2, refs[8][...])
            acc_ref[...] += part
        else:
            h2_ref[...] = h2

    row = lambda i: (i, 0)
    big = pl.BlockSpec((tm, D_MODEL), row)
    wspec = lambda r, cdim: pl.BlockSpec((N_CHIPS, r, cdim), lambda i: (0, 0, 0))
    acc = pl.BlockSpec((SUBLANES, 128), lambda i: (0, 0))
    return pl.pallas_call(
        body, name="out_ple_loss" if last else "out_ple", grid=(s // tm,),
        in_specs=[big, pl.BlockSpec((tm, SSM_WIDTH), row), pl.BlockSpec((tm, ATTN_WIDTH), row),
                  pl.BlockSpec((tm, PLE_DIM), row), pl.BlockSpec((1, D_MODEL), lambda i: (0, 0)),
                  wspec(ROW_SHARD, D_MODEL), wspec(ROW_SHARD, D_MODEL), wspec(PLE_DIM, ROW_SHARD)] + [big] * last,
        out_specs=[big] * 2 + [acc] * last,
        out_shape=[jax.ShapeDtypeStruct((s, D_MODEL), F32)] * 2 + [jax.ShapeDtypeStruct((SUBLANES, 128), F32)] * last,
        compiler_params=_cparams(1),
    )(h, ys, ya, p, g2, w_out_l, w_pg_l, w_pp_l, *([target] if last else []))


def _rms_bwd(x, r, g, dy):
    gdy = g * dy
    dx = r * gdy - x * (r * r * r) * jnp.mean(x * gdy, axis=-1, keepdims=True)
    return dx, x * r * dy


def _out_ple_bwd(dh2, h1, p, g2, w_out_l, w_pg_l, w_pp_l):
    s = h1.shape[0]
    tm = _row_tile(s, 256)

    def body(dh2_ref, h1_ref, p_ref, g_ref, wo_ref, wpg_ref, wpp_ref,
             dh1_ref, dmix_ref, hn_ref, dgp_ref, dpp_ref, dh1b_ref, dg_ref):
        @pl.when(pl.program_id(0) == 0)
        def _():
            dg_ref[...] = jnp.zeros_like(dg_ref)

        h1 = h1_ref[...]
        dh2 = dh2_ref[...]
        g2v = g_ref[...]
        r2, hb, gate, pp = _ple_forward(h1, p_ref[...], g2v, wpg_ref, wpp_ref)
        dgp = (dh2 * pp) * gate * (1.0 - gate)
        dgpb = dgp.astype(BF16)
        dhn = jnp.concatenate([_dot_nt(dgpb, wpg_ref[sh]) for sh in range(N_CHIPS)], axis=-1)
        dx, dgrow = _rms_bwd(h1, r2, g2v, dhn)
        dh1 = dh2 + dx
        dh1b = dh1.astype(BF16)
        dh1_ref[...] = dh1
        dh1b_ref[...] = dh1b
        hn_ref[...] = hb
        dgp_ref[...] = dgpb
        dpp_ref[...] = (dh2 * gate).astype(BF16)
        dg_ref[...] += _colsum8(dgrow)
        for sh in range(N_CHIPS):
            dmix_ref[:, ROW_SHARD * sh:ROW_SHARD * (sh + 1)] = _dot_nt(dh1b, wo_ref[sh])

    row = lambda i: (i, 0)
    wspec = lambda r, cdim: pl.BlockSpec((N_CHIPS, r, cdim), lambda i: (0, 0, 0))
    big = pl.BlockSpec((tm, D_MODEL), row)
    return pl.pallas_call(
        body, name="out_ple_bwd", grid=(s // tm,),
        in_specs=[big, big, pl.BlockSpec((tm, PLE_DIM), row), pl.BlockSpec((1, D_MODEL), lambda i: (0, 0)),
                  wspec(ROW_SHARD, D_MODEL), wspec(ROW_SHARD, D_MODEL), wspec(PLE_DIM, ROW_SHARD)],
        out_specs=[big] * 6 + [pl.BlockSpec((SUBLANES, D_MODEL), lambda i: (0, 0))],
        out_shape=[jax.ShapeDtypeStruct((s, D_MODEL), F32)] * 2 + [jax.ShapeDtypeStruct((s, D_MODEL), BF16)] * 4
        + [jax.ShapeDtypeStruct((SUBLANES, D_MODEL), F32)],
        compiler_params=_cparams(1),
    )(dh2, h1, p, g2, w_out_l, w_pg_l, w_pp_l)


def _tn_matmul(a, b, n_blocks, block_a, name, into=None, first_block=0, total_blocks=None):
    s = a.shape[0]
    tk = _row_tile(s, 512)
    nk = s // tk
    total_blocks = n_blocks if total_blocks is None else total_blocks
    ka, nb = a.shape[1], b.shape[1]
    if block_a:
        ka //= n_blocks
    else:
        nb //= n_blocks

    def body(*refs):
        a_ref, b_ref, o_ref, acc_ref = refs[0], refs[1], refs[-2], refs[-1]

        @pl.when(pl.program_id(0) == 0)
        def _():
            acc_ref[...] = jnp.zeros_like(acc_ref)

        at = a_ref[...].astype(BF16).T
        bb = b_ref[...].astype(BF16)
        for sh in range(n_blocks):
            if block_a:
                acc_ref[sh] += _dot(at[ka * sh:ka * (sh + 1), :], bb)
            else:
                acc_ref[sh] += _dot(at, bb[:, nb * sh:nb * (sh + 1)])

        @pl.when(pl.program_id(0) == nk - 1)
        def _():
            o_ref[...] = acc_ref[...].astype(BF16)

    in_specs = [pl.BlockSpec((tk, a.shape[1]), lambda i: (i, 0)), pl.BlockSpec((tk, b.shape[1]), lambda i: (i, 0))]
    operands = [a, b]
    aliases = {}
    if into is not None:
        in_specs.append(ANY)
        operands.append(into)
        aliases = {2: 0}
    return pl.pallas_call(
        body, name=name, grid=(nk,),
        in_specs=in_specs,
        out_specs=pl.BlockSpec((n_blocks, ka, nb), lambda i: (first_block // n_blocks, 0, 0)),
        out_shape=jax.ShapeDtypeStruct((total_blocks, ka, nb), BF16),
        scratch_shapes=[pltpu.VMEM((n_blocks, ka, nb), F32)],
        input_output_aliases=aliases,
        compiler_params=_cparams(1),
    )(*operands)


def _attn_bwd(qkv, o, proj, dmix, scatter=None, scatter_owner=None):
    scatter = [] if scatter is None else scatter
    nsc = len(scatter)
    owners = [scatter_owner] * nsc
    s = qkv.shape[0]
    tb = ATTN_BLOCK
    nq = s // tb
    n_sub = min(ATTN_SUB_BWD, nq)
    tq = n_sub * tb
    n_chain = 2 * n_sub

    def body(*refs):
        q_ref, k_ref, v_ref, o_ref, g_ref, dya_ref, tri_s_ref, tri_p_ref = refs[0:8]
        sc_ins = refs[8:8 + nsc]
        dq_ref, dk_ref, dv_ref, dg_ref = refs[8 + nsc:12 + nsc]
        sc_outs = refs[12 + nsc:12 + 2 * nsc]
        do_scr, l_scr, g_scr, s_scr, w_scr = refs[12 + 2 * nsc:17 + 2 * nsc]
        sc_sems = refs[17 + 2 * nsc:]
        i = pl.program_id(1)
        base = i * n_sub

        if nsc:
            @pl.when(jnp.logical_and(pl.program_id(0) == 0, i == 0))
            def _():
                _scatter_start(sc_ins, sc_outs, owners, sc_sems)

        @pl.when(i == 0)
        def _():
            dk_ref[...] = jnp.zeros_like(dk_ref)
            dv_ref[...] = jnp.zeros_like(dv_ref)

        g = g_ref[...]
        sg = _sigmoid(g)
        dya = dya_ref[...]
        do_scr[...] = (dya * (g * sg)).astype(BF16)
        dg_ref[...] = dya * o_ref[...] * (sg * (1.0 + g * (1.0 - sg)))
        dq_ref[...] = jnp.zeros_like(dq_ref)
        g_scr[...] = jnp.zeros_like(g_scr)
        masks = _head_masks()

        def keep(t, a, h, c, r0, z, w):
            s_scr[c, t] = _sigmoid(z).astype(BF16)
            w_scr[c, t] = w.astype(BF16)

        steps = _chain_sweep(base, n_sub, q_ref, k_ref, tri_s_ref, l_scr, keep)
        row = lax.broadcasted_iota(jnp.int32, (tb, tb), 0)
        col = lax.broadcasted_iota(jnp.int32, (tb, tb), 1)

        def back(it, carry):
            t = steps - 1 - it
            r0s = [pl.multiple_of(jnp.maximum(base + a - t, 0) * tb, tb) for a in range(n_sub)]
            qhs, dohs, khs, gws = [], [], [], []
            for a in range(n_sub):
                kb = k_ref[pl.ds(r0s[a], tb), :]
                vb = v_ref[pl.ds(r0s[a], tb), :]
                qa = q_ref[a * tb:(a + 1) * tb, :]
                doa = do_scr[a * tb:(a + 1) * tb, :]
                for h, mask in enumerate(masks):
                    qhs.append(jnp.where(mask, qa, jnp.zeros_like(qa)))
                    khs.append(jnp.where(mask, kb, jnp.zeros_like(kb)))
                    dohs.append(jnp.where(mask, doa, jnp.zeros_like(doa)))
                    gws.append(w_scr[2 * a + h, t].astype(F32) * _dot_nt(dohs[-1], vb))
            parts = [_split_hilo(gw) for gw in gws]
            tri = tri_p_ref[...]
            sums = [_dot(hi, tri) + _dot(lo, tri) for hi, lo in parts]
            dzs = []
            for c, (gw, sm) in enumerate(zip(gws, sums)):
                gsum = g_scr[c]
                dz = gw - (gw + sm[:, 0:tb] + gsum) * s_scr[c, t].astype(F32)
                dz = jnp.where(col < row + t * tb, dz, 0.0)
                g_scr[c] = gsum + sm[:, tb:2 * tb]
                dzs.append(dz.astype(BF16))
            for c, dzb in enumerate(dzs):
                a = c // 2
                dk_ref[pl.ds(r0s[a], tb), :] += _dot_tn(dzb, qhs[c])
                dv_ref[pl.ds(r0s[a], tb), :] += _dot_tn(w_scr[c, t], dohs[c])
                dq_ref[a * tb:(a + 1) * tb, :] += _dot(dzb, khs[c])
            return carry

        lax.fori_loop(0, steps, back, 0)

        if nsc:
            @pl.when(jnp.logical_and(pl.program_id(0) == n_hp - 1, i == s // tq - 1))
            def _():
                _scatter_finish(sc_ins, sc_outs, owners, sc_sems)

    n_hp = ATTN_WIDTH // (2 * HEAD_DIM)
    hp_blk = lambda off: pl.BlockSpec((tq, 2 * HEAD_DIM), lambda hp, i: (i, off + hp))
    res = lambda off: pl.BlockSpec((s, 2 * HEAD_DIM), lambda hp, i: (0, off + hp))
    tri = pl.BlockSpec((tb, 2 * tb), lambda hp, i: (0, 0))
    outs = pl.pallas_call(
        body, name="attn_bwd_scatter" if nsc else "attn_bwd", grid=(n_hp, s // tq),
        in_specs=[hp_blk(0), res(4), res(8), hp_blk(0), hp_blk(20), hp_blk(4), tri, tri] + [ANY] * nsc,
        out_specs=[hp_blk(0), res(0), res(0), hp_blk(0)] + [ANY] * nsc,
        out_shape=[jax.ShapeDtypeStruct((s, ATTN_WIDTH), F32)] * 4 + [jax.ShapeDtypeStruct(a.shape, a.dtype) for a in scatter],
        scratch_shapes=[pltpu.VMEM((tq, 2 * HEAD_DIM), BF16), pltpu.VMEM((n_chain, tb, tb), F32),
                        pltpu.VMEM((n_chain, tb, tb), F32), pltpu.VMEM((n_chain, nq, tb, tb), BF16),
                        pltpu.VMEM((n_chain, nq, tb, tb), BF16)] + (_scatter_sems(nsc) if nsc else []),
        compiler_params=_cparams(2),
    )(qkv, qkv, qkv, o, proj, dmix, _tri("suffix_incl"), _tri("prefix_strict"), *scatter)
    return outs[0], outs[1], outs[2], outs[3], (_scatter_own(outs[4:], scatter) if nsc else [])


def _ssm_glu_bwd(dmix, y, proj, d, w_glu_l, b_glu):
    s = y.shape[0]
    tm = _row_tile(s, 512)

    def body(dys_ref, y_ref, u_ref, gs_ref, d_ref, wg_ref, bg_ref,
             dyf_ref, du_ref, dgs_ref, z_ref, dzz_ref, dd_ref, db_ref):
        @pl.when(pl.program_id(0) == 0)
        def _():
            dd_ref[...] = jnp.zeros_like(dd_ref)
            db_ref[...] = jnp.zeros_like(db_ref)

        u = u_ref[...]
        dv = d_ref[...]
        yf, z, val, gate = _glu_forward(y_ref[...], u, dv, wg_ref, bg_ref[...])
        gs = gs_ref[...]
        sgs = _sigmoid(gs)
        sgate = _sigmoid(gate)
        dys = dys_ref[...]
        dgv = dys * (gs * sgs)
        dgs_ref[...] = dys * (val * sgate) * (sgs * (1.0 + gs * (1.0 - sgs)))
        dzz = jnp.concatenate([dgv * sgate, dgv * val * sgate * (1.0 - sgate)], axis=-1)
        dzzb = dzz.astype(BF16)
        dz = _dot_nt(dzzb[:, 0:ROW_SHARD], wg_ref[0])
        for sh in range(1, N_CHIPS):
            dz = dz + _dot_nt(dzzb[:, ROW_SHARD * sh:ROW_SHARD * (sh + 1)], wg_ref[sh])
        dyf = dz * _gelu_grad(yf)
        dyf_ref[...] = dyf
        du_ref[...] = dyf * dv
        z_ref[...] = z.astype(BF16)
        dzz_ref[...] = dzzb
        dd_ref[...] += _colsum8(dyf * u)
        db_ref[...] += _colsum8(dzz)

    row = lambda i: (i, 0)
    half = pl.BlockSpec((tm, SSM_WIDTH), row)
    return pl.pallas_call(
        body, name="ssm_glu_bwd", grid=(s // tm,),
        in_specs=[half, half, half, pl.BlockSpec((tm, SSM_WIDTH), lambda i: (i, 1)),
                  pl.BlockSpec((1, SSM_WIDTH), lambda i: (0, 0)),
                  pl.BlockSpec((N_CHIPS, SSM_WIDTH, ROW_SHARD), lambda i: (0, 0, 0)),
                  pl.BlockSpec((1, 2 * SSM_WIDTH), lambda i: (0, 0))],
        out_specs=[half, half, half, half, pl.BlockSpec((tm, 2 * SSM_WIDTH), row),
                   pl.BlockSpec((SUBLANES, SSM_WIDTH), lambda i: (0, 0)),
                   pl.BlockSpec((SUBLANES, 2 * SSM_WIDTH), lambda i: (0, 0))],
        out_shape=[jax.ShapeDtypeStruct((s, SSM_WIDTH), F32)] * 3
        + [jax.ShapeDtypeStruct((s, SSM_WIDTH), BF16), jax.ShapeDtypeStruct((s, 2 * SSM_WIDTH), BF16),
           jax.ShapeDtypeStruct((SUBLANES, SSM_WIDTH), F32), jax.ShapeDtypeStruct((SUBLANES, 2 * SSM_WIDTH), F32)],
        compiler_params=_cparams(1),
    )(dmix, y, proj, proj, d, w_glu_l, b_glu)


def _ssm_scan_bwd(dyf, xs, proj, wct, tab_rev, wbt):
    s = dyf.shape[0]
    tm = _row_tile(s, 512)
    nt = s // tm
    length = tm // SUBLANES

    def body(dy_ref, xs_ref, u_ref, wct_ref, tab_ref, wbt_ref, du_ref, dwc_ref, dwb_ref, da_ref, lam_ref, carry_ref):
        @pl.when(pl.program_id(1) == 0)
        def _():
            carry_ref[...] = jnp.zeros_like(carry_ref)
            dwc_ref[...] = jnp.zeros_like(dwc_ref)
            dwb_ref[...] = jnp.zeros_like(dwb_ref)
            da_ref[...] = jnp.zeros_like(da_ref)

        dyp = _interleave_chunks(dy_ref[...]).astype(BF16)
        up = _interleave_chunks(u_ref[...]).astype(BF16)
        lam_ref[...] = _dot(dyp, wct_ref[...])

        def tail(r0, lr, li, carry):
            er, ei, dar, dai = carry
            xr = xs_ref[pl.ds(r0, SUBLANES), 0:CH_S]
            xi = xs_ref[pl.ds(r0, SUBLANES), CH_S:2 * CH_S]
            return lr, li, dar + (xr * er + xi * ei), dai + (xr * ei - xi * er)

        fix, (gr, gi) = _chunk_scan(lam_ref, tab_ref, carry_ref, length, reverse=True, tail=tail)
        zero = jnp.zeros((SUBLANES, CH_S), F32)
        _, _, dar, dai = lax.fori_loop(0, length, fix, (gr, gi, zero, zero), unroll=2)
        da_ref[:, 0:CH_S] += dar
        da_ref[:, CH_S:2 * CH_S] += dai
        lamb = lam_ref[...].astype(BF16)
        du_ref[...] = _time_order(_dot(lamb, wbt_ref[...]))
        dwc_ref[...] += _dot_tn(xs_ref[...].astype(BF16), dyp)
        dwb_ref[...] += _dot_tn(up, lamb)

    rev = lambda j, i: (nt - 1 - i, j)
    return pl.pallas_call(
        body, name="ssm_scan_bwd", grid=(SSM_CHUNKS, nt),
        in_specs=[pl.BlockSpec((tm, CH_W), rev),
                  pl.BlockSpec((None, tm, 2 * CH_S), lambda j, i: (j, nt - 1 - i, 0)),
                  pl.BlockSpec((tm, CH_W), rev),
                  pl.BlockSpec((None, CH_W, 2 * CH_S), lambda j, i: (j, 0, 0)),
                  pl.BlockSpec((None, length, 2 * CH_S), lambda j, i: (j, 0, 0)),
                  pl.BlockSpec((None, 2 * CH_S, CH_W), lambda j, i: (j, 0, 0))],
        out_specs=[pl.BlockSpec((tm, CH_W), rev),
                   pl.BlockSpec((None, 2 * CH_S, CH_W), lambda j, i: (j, 0, 0)),
                   pl.BlockSpec((None, CH_W, 2 * CH_S), lambda j, i: (j, 0, 0)),
                   pl.BlockSpec((None, SUBLANES, 2 * CH_S), lambda j, i: (j, 0, 0))],
        out_shape=[jax.ShapeDtypeStruct((s, SSM_WIDTH), F32),
                   jax.ShapeDtypeStruct((SSM_CHUNKS, 2 * CH_S, CH_W), F32),
                   jax.ShapeDtypeStruct((SSM_CHUNKS, CH_W, 2 * CH_S), F32),
                   jax.ShapeDtypeStruct((SSM_CHUNKS, SUBLANES, 2 * CH_S), F32)],
        scratch_shapes=[pltpu.VMEM((tm, 2 * CH_S), F32), pltpu.VMEM((SUBLANES, 2 * CH_S), F32)],
        compiler_params=_cparams(2),
    )(dyf, xs, proj, wct, tab_rev, wbt)


def _in_proj_bwd(h, g1, w_in_l, qg, kg, proj, du_a, du_b, dgs, dq, dk, dv, dga, dh1):
    s = h.shape[0]
    tm = _row_tile(s, 256)

    def body(h_ref, g_ref, w_ref, qg_ref, kg_ref, ones_ref, q_ref, k_ref, dua_ref, dub_ref, dgs_ref, dq_ref, dk_ref,
             dv_ref, dga_ref, dh1_ref, dh_ref, hn_ref, dp_ref, dg1_ref, dqg_ref, dkg_ref):
        @pl.when(pl.program_id(0) == 0)
        def _():
            dg1_ref[...] = jnp.zeros_like(dg1_ref)
            dqg_ref[...] = jnp.zeros_like(dqg_ref)
            dkg_ref[...] = jnp.zeros_like(dkg_ref)

        ones = ones_ref[...]

        def head_norm_bwd(x, gain, dy):
            r = lax.rsqrt(_dot_hilo(x * x, ones) + RMS_EPS)
            gdy = gain * dy
            dx = r * gdy - x * (r * r * r) * _dot_hilo(x * gdy, ones)
            return dx, x * r * dy

        dqr, dqg_rows = head_norm_bwd(q_ref[...], qg_ref[...], dq_ref[...] * ATTN_SCALE)
        dkr, dkg_rows = head_norm_bwd(k_ref[...], kg_ref[...], dk_ref[...])
        dqg_ref[...] += _colsum8(dqg_rows)
        dkg_ref[...] += _colsum8(dkg_rows)
        dp_ref[:, 0:512] = (dua_ref[...] + dub_ref[...]).astype(BF16)
        dp_ref[:, 512:1024] = dgs_ref[...].astype(BF16)
        dp_ref[:, 1024:1536] = dqr.astype(BF16)
        dp_ref[:, 1536:2048] = dkr.astype(BF16)
        dp_ref[:, 2048:2560] = dv_ref[...].astype(BF16)
        dp_ref[:, 2560:3072] = dga_ref[...].astype(BF16)
        dhn = _dot_nt(dp_ref[:, 0:IN_SHARD], w_ref[0])
        for sh in range(1, N_CHIPS):
            dhn = dhn + _dot_nt(dp_ref[:, IN_SHARD * sh:IN_SHARD * (sh + 1)], w_ref[sh])
        x = h_ref[...]
        gv = g_ref[...]
        r, hn = _rms_rows(x, gv)
        dx, dg_rows = _rms_bwd(x, r, gv, dhn)
        dh_ref[...] = dh1_ref[...] + dx
        hn_ref[...] = hn.astype(BF16)
        dg1_ref[...] += _colsum8(dg_rows)

    row = lambda i: (i, 0)
    full = lambda shape: pl.BlockSpec(shape, lambda i: (0,) * len(shape))
    big = pl.BlockSpec((tm, D_MODEL), row)
    half = pl.BlockSpec((tm, 512), row)
    return pl.pallas_call(
        body, name="in_proj_bwd", grid=(s // tm,),
        in_specs=[big, full((1, D_MODEL)), full((N_CHIPS, D_MODEL, IN_SHARD)),
                  full((1, ATTN_WIDTH)), full((1, ATTN_WIDTH)), full((ATTN_WIDTH, ATTN_WIDTH)),
                  pl.BlockSpec((tm, 512), lambda i: (i, 2)), pl.BlockSpec((tm, 512), lambda i: (i, 3)),
                  half, half, half, half, half, half, half, big],
        out_specs=[big, big, pl.BlockSpec((tm, IN_COLS), row), pl.BlockSpec((SUBLANES, D_MODEL), lambda i: (0, 0)),
                   pl.BlockSpec((SUBLANES, ATTN_WIDTH), lambda i: (0, 0)), pl.BlockSpec((SUBLANES, ATTN_WIDTH), lambda i: (0, 0))],
        out_shape=[jax.ShapeDtypeStruct((s, D_MODEL), F32), jax.ShapeDtypeStruct((s, D_MODEL), BF16),
                   jax.ShapeDtypeStruct((s, IN_COLS), BF16), jax.ShapeDtypeStruct((SUBLANES, D_MODEL), F32),
                   jax.ShapeDtypeStruct((SUBLANES, ATTN_WIDTH), F32), jax.ShapeDtypeStruct((SUBLANES, ATTN_WIDTH), F32)],
        compiler_params=_cparams(1),
    )(h, g1, w_in_l, qg, kg, _head_ones(), proj, proj, du_a, du_b, dgs, dq, dk, dv, dga, dh1)


SMALL_NAMES = ("mix_norm_g", "ssm_a_re", "ssm_a_im", "ssm_log_dt", "ssm_b_re", "ssm_b_im", "ssm_c_re", "ssm_c_im",
               "ssm_d", "ssm_b_glu", "q_norm_g", "k_norm_g", "ple_norm_g")
SMALL_4D = ("ssm_b_re", "ssm_b_im", "ssm_c_re", "ssm_c_im")
BIG_NAMES = ("w_in", "ssm_w_glu", "w_out", "w_ple_gate", "w_ple_proj")


def _ssm_setup(sm, layer, length):
    col = lambda a: a[layer].reshape(1, N_STATES)
    a_re, a_im = col(sm["ssm_a_re"]), col(sm["ssm_a_im"])
    log_dt = jnp.repeat(sm["ssm_log_dt"][layer], SSM_STATE).reshape(1, N_STATES)
    b_re = sm["ssm_b_re"][layer].reshape(N_STATES, SSM_GROUP).T
    b_im = sm["ssm_b_im"][layer].reshape(N_STATES, SSM_GROUP).T
    disc_in = (a_re, a_im, log_dt, b_re, b_im)
    ab_re, ab_im, bb_re, bb_im = _disc_fwd(*disc_in)
    wb = jnp.concatenate([_block_diag_in(bb_re), _block_diag_in(bb_im)], axis=-1)
    wc = jnp.concatenate([_block_diag_out(sm["ssm_c_re"][layer]), -_block_diag_out(sm["ssm_c_im"][layer])], axis=1)
    return dict(disc_in=disc_in, wb=wb.astype(BF16), wbt=wb.transpose(0, 2, 1).astype(BF16),
                wc=wc.astype(BF16), wct=wc.transpose(0, 2, 1).astype(BF16),
                tab=_scan_powers(ab_re, ab_im, length, False), tab_rev=_scan_powers(ab_re, ab_im, length, True))


def _whole_blocks(names, gathered):
    return {n: g.reshape(N_CHIPS, 2 * g.shape[2], g.shape[3]) for n, g in zip(names, gathered)}


def _local_step(x, p, target, sm, w_in0, local=None, gathered=None, layer1_hook=None):
    wg = [dict(w_in=w_in0), {}] if gathered is None else gathered
    tile8 = lambda a: jnp.tile(a, ATTN_WIDTH // HEAD_DIM).reshape(1, ATTN_WIDTH)
    saved = []
    h = x
    for l in range(N_LAYERS):
        ssm = _ssm_setup(sm, l, _row_tile(x.shape[0], 512) // SUBLANES)
        g1 = sm["mix_norm_g"][l].reshape(1, D_MODEL)
        g2 = sm["ple_norm_g"][l].reshape(1, D_MODEL)
        qg, kg = tile8(sm["q_norm_g"][l]), tile8(sm["k_norm_g"][l])
        dsk = sm["ssm_d"][l].reshape(1, SSM_WIDTH)
        bgl = sm["ssm_b_glu"][l].reshape(1, 2 * SSM_WIDTH)
        proj, qkv = _in_proj(h, g1, wg[l]["w_in"], qg, kg)
        if l == 0 and local is not None:
            rest = BIG_NAMES[1:]
            xs, y, got = _ssm_scan_fwd(proj, ssm["wb"], ssm["tab"], ssm["wc"], [local[n] for n in rest], [0] * len(rest))
            wg[0].update(_whole_blocks(rest, got))
            ys = _ssm_glu_fwd(y, proj, dsk, wg[0]["ssm_w_glu"], bgl)
            o, ya, got = _attn_fwd(qkv, proj, [local[n] for n in BIG_NAMES], [2] * len(BIG_NAMES))
            wg[1].update(_whole_blocks(BIG_NAMES, got))
        else:
            xs, y, _ = _ssm_scan_fwd(proj, ssm["wb"], ssm["tab"], ssm["wc"])
            ys = _ssm_glu_fwd(y, proj, dsk, wg[l]["ssm_w_glu"], bgl)
            o, ya, _ = _attn_fwd(qkv, proj)
        tail = (target,) if l == N_LAYERS - 1 else ()
        h1, h2, *sq = _out_ple(h, ys, ya, p[l], g2, wg[l]["w_out"], wg[l]["w_ple_gate"], wg[l]["w_ple_proj"], *tail)
        saved.append(dict(ssm=ssm, g1=g1, g2=g2, qg=qg, kg=kg, dsk=dsk, bgl=bgl, h=h, proj=proj, qkv=qkv, xs=xs, y=y,
                          ys=ys, o=o, ya=ya, h1=h1))
        h = h2
    dh = h
    loss = 0.5 * jnp.sum(sq[0]) / D_MODEL

    gbig = [{} for _ in range(N_LAYERS)]
    scattered = []
    gsm = {n: [None] * N_LAYERS for n in SMALL_NAMES}
    for l in reversed(range(N_LAYERS)):
        sv = saved[l]
        ssm = sv["ssm"]
        dh1, dmix, hn2b, dgpb, dppb, dh1b, dg2 = _out_ple_bwd(dh, sv["h1"], p[l], sv["g2"], wg[l]["w_out"],
                                                              wg[l]["w_ple_gate"], wg[l]["w_ple_proj"])
        gsm["ple_norm_g"][l] = dg2.sum(0)
        gbig[l]["w_ple_proj"] = _tn_matmul(p[l], dppb, N_CHIPS, False, "dw_ple_proj")
        gbig[l]["w_ple_gate"] = _tn_matmul(hn2b, dgpb, N_CHIPS, True, "dw_ple_gate")
        dwo = _tn_matmul(sv["ys"], dh1b, 2, True, "dw_out_ssm", None, 0, N_CHIPS)
        gbig[l]["w_out"] = _tn_matmul(sv["ya"], dh1b, 2, True, "dw_out_attn", dwo, 2, N_CHIPS)
        if l == 0 and layer1_hook is not None:
            dqs, dkn, dv, dga, scattered = _attn_bwd(sv["qkv"], sv["o"], sv["proj"], dmix, *layer1_hook(gbig[1]))
        else:
            dqs, dkn, dv, dga, _ = _attn_bwd(sv["qkv"], sv["o"], sv["proj"], dmix)
        dyf, du_a, dgs, zb, dzzb, dd, dbg = _ssm_glu_bwd(dmix, sv["y"], sv["proj"], sv["dsk"], wg[l]["ssm_w_glu"], sv["bgl"])
        gsm["ssm_d"][l] = dd.sum(0).reshape(SSM_GROUPS, SSM_GROUP)
        gsm["ssm_b_glu"][l] = dbg.sum(0)
        gbig[l]["ssm_w_glu"] = _tn_matmul(zb, dzzb, N_CHIPS, False, "dw_glu")
        du_b, dwc, dwb, da = _ssm_scan_bwd(dyf, sv["xs"], sv["proj"], ssm["wct"], ssm["tab_rev"], ssm["wbt"])
        gsm["ssm_c_re"][l] = _block_diag_out_t(dwc[:, 0:CH_S, :])
        gsm["ssm_c_im"][l] = -_block_diag_out_t(dwc[:, CH_S:, :])
        da = da.sum(1)
        g_ab_re = da[:, 0:CH_S].reshape(1, N_STATES)
        g_ab_im = da[:, CH_S:].reshape(1, N_STATES)
        g_bb_re = _block_diag_in_t(dwb[:, :, 0:CH_S])
        g_bb_im = _block_diag_in_t(dwb[:, :, CH_S:])
        d_are, d_aim, d_ldt, d_bre, d_bim = _disc_bwd(*ssm["disc_in"], g_ab_re, g_ab_im, g_bb_re, g_bb_im)
        gsm["ssm_a_re"][l] = d_are.reshape(SSM_GROUPS, SSM_STATE)
        gsm["ssm_a_im"][l] = d_aim.reshape(SSM_GROUPS, SSM_STATE)
        gsm["ssm_log_dt"][l] = d_ldt.reshape(SSM_GROUPS, SSM_STATE).sum(1)
        gsm["ssm_b_re"][l] = d_bre.T.reshape(SSM_GROUPS, SSM_STATE, SSM_GROUP)
        gsm["ssm_b_im"][l] = d_bim.T.reshape(SSM_GROUPS, SSM_STATE, SSM_GROUP)
        dh, hnb, dprojb, dg1, dqg, dkg = _in_proj_bwd(sv["h"], sv["g1"], wg[l]["w_in"], sv["qg"], sv["kg"], sv["proj"],
                                                      du_a, du_b, dgs, dqs, dkn, dv, dga, dh1)
        gsm["mix_norm_g"][l] = dg1.sum(0)
        gsm["q_norm_g"][l] = dqg.sum(0).reshape(-1, HEAD_DIM).sum(0)
        gsm["k_norm_g"][l] = dkg.sum(0).reshape(-1, HEAD_DIM).sum(0)
        gbig[l]["w_in"] = _tn_matmul(hnb, dprojb, N_CHIPS, False, "dw_in")
    gsm = {n: jnp.stack(v, 0) for n, v in gsm.items()}
    return loss, dh, gbig, gsm, scattered


_SMALL_PAD = 8 * 8 * 128


def _pack_small(d, extra):
    flat = jnp.concatenate([d[n].reshape(-1) for n in SMALL_NAMES] + [jnp.stack(extra)])
    n = flat.shape[0]
    padded = -(-n // _SMALL_PAD) * _SMALL_PAD
    return jnp.pad(flat, (0, padded - n))


def _unpack_small(flat, like):
    out, off = {}, 0
    for n in SMALL_NAMES:
        size = like[n].size
        out[n] = flat[off:off + size].reshape(like[n].shape)
        off += size
    return out, flat[off:]


def _flat_rows(a):
    return a.reshape(-1, a.shape[-1])


def _chip_sums(glayer, owner):
    parts = [glayer[n] for n in BIG_NAMES]
    flat = [_flat_rows(a) for a in parts]
    recv = _sibling_push(flat, [owner] * len(flat), "grad_push_layer%d" % owner)
    return [_add_pair(f, r, owner, "grad_pair_add").reshape(a.shape) for f, r, a in zip(flat, recv, parts)]


def kernel(x, p, mix_norm_g, w_in, ssm_a_re, ssm_a_im, ssm_log_dt, ssm_b_re, ssm_b_im, ssm_c_re, ssm_c_im, ssm_d, ssm_w_glu, ssm_b_glu, q_norm_g, k_norm_g, w_out, ple_norm_g, w_ple_gate, w_ple_proj, loss_target, m_mix_norm_g, m_w_in, m_ssm_a_re, m_ssm_a_im, m_ssm_log_dt, m_ssm_b_re, m_ssm_b_im, m_ssm_c_re, m_ssm_c_im, m_ssm_d, m_ssm_w_glu, m_ssm_b_glu, m_q_norm_g, m_k_norm_g, m_w_out, m_ple_norm_g, m_w_ple_gate, m_w_ple_proj, v_mix_norm_g, v_w_in, v_ssm_a_re, v_ssm_a_im, v_ssm_log_dt, v_ssm_b_re, v_ssm_b_im, v_ssm_c_re, v_ssm_c_im, v_ssm_d, v_ssm_w_glu, v_ssm_b_glu, v_q_norm_g, v_k_norm_g, v_w_out, v_ple_norm_g, v_w_ple_gate, v_w_ple_proj):
    args = dict(locals())
    names = ("mix_norm_g", "w_in", "ssm_a_re", "ssm_a_im", "ssm_log_dt", "ssm_b_re", "ssm_b_im", "ssm_c_re", "ssm_c_im",
             "ssm_d", "ssm_w_glu", "ssm_b_glu", "q_norm_g", "k_norm_g", "w_out", "ple_norm_g", "w_ple_gate", "w_ple_proj")
    w = {n: args[n] for n in names}
    m = {n: args["m_" + n] for n in names}
    v = {n: args["v_" + n] for n in names}

    local = {n: w[n].astype(BF16).reshape(2 * N_LAYERS, w[n].shape[1] // 2, w[n].shape[2]) for n in BIG_NAMES}
    w_in0 = _chip_gather([local["w_in"]], "w_in_gather")[0].reshape(N_CHIPS, D_MODEL, IN_SHARD)
    sm = {n: w[n] for n in SMALL_NAMES}
    loss, dx, gbig, gsm, got1 = _local_step(x[0], p[:, 0], loss_target[0], sm, w_in0, local,
                                            layer1_hook=lambda g1: (_chip_sums(g1, 1), 1))

    nb = len(BIG_NAMES)
    small = _pack_small(gsm, [loss]).reshape(2, N_CHIPS * SUBLANES, -1)
    flat0 = [_flat_rows(gbig[0][n]) for n in BIG_NAMES]
    recv = _sibling_push(flat0 + [small], [0] * nb + [None], "grad_push_layer0")
    chip0 = [_add_pair(f, r, 0, "grad_pair_add").reshape(gbig[0][n].shape) for f, r, n in zip(flat0, recv, BIG_NAMES)]
    chip_small = _add_half(small, recv[-1], F32, "grad_half_add").reshape(N_CHIPS, SUBLANES, -1)
    got0 = _chip_scatter(chip0 + [chip_small], [0] * nb + [None], "grad_chip_scatter")
    tot1 = [_sum4(a, 1, "grad_chip_sum") for a in got1]
    tot0 = [_sum4(a, o, "grad_chip_sum") for a, o in zip(got0, [0] * nb + [None])]
    other, (small_mine,) = _sibling_join(tot0[:nb], tot1, [tot0[nb]], "grad_sibling_join")
    small_all = _chip_gather([small_mine], "small_grad_gather")[0]
    small_tot = small_all.transpose(1, 0, 2, 3).reshape(-1)
    on_core0 = lax.axis_index("c") == 0
    g = {n: jnp.where(on_core0, jnp.stack([t0, ot]), jnp.stack([ot, t1])).reshape(w[n].shape)
         for n, t0, t1, ot in zip(BIG_NAMES, tot0, tot1, other)}
    g_small, rest = _unpack_small(small_tot, sm)
    g.update(g_small)
    loss = rest[0]

    delta, new_m, new_v = {}, {}, {}
    for n in BIG_NAMES:
        lanes = w[n].shape[-1]
        outs = _adamw(_as_rows(w[n], lanes), _as_rows(g[n], lanes), _as_rows(m[n], lanes), _as_rows(v[n], lanes), "adamw_" + n)
        delta[n], new_m[n], new_v[n] = [o.reshape(w[n].shape) for o in outs]
    for group, per_layer in ((SMALL_4D, True), (tuple(n for n in SMALL_NAMES if n not in SMALL_4D), False)):
        outs = _adamw_many(*[[d[n] for n in group] for d in (w, g, m, v)], "adamw_small_4d" if per_layer else "adamw_small", per_layer)
        for d, o in zip((delta, new_m, new_v), outs):
            d.update(zip(group, o))

    return (loss, dx[None], *[g[n] for n in names], *[delta[n] for n in names],
            *[new_m[n] for n in names], *[new_v[n] for n in names])
```

```python
import functools
import math

import jax
import jax.numpy as jnp
from jax import lax
from jax.experimental import pallas as pl
from jax.experimental.pallas import tpu as pltpu

F32 = jnp.float32
BF16 = jnp.bfloat16

D_MODEL = 1024
N_LAYERS = 2
N_CHIPS = 4
IN_COLS = 3072
IN_SHARD = IN_COLS // N_CHIPS
SSM_WIDTH = 512
SSM_GROUP = 16
SSM_GROUPS = 32
SSM_STATE = 64
N_STATES = SSM_GROUPS * SSM_STATE
SSM_CHUNKS = 4
CH_W = SSM_WIDTH // SSM_CHUNKS
CH_S = N_STATES // SSM_CHUNKS
ATTN_WIDTH = 512
HEAD_DIM = 64
PLE_DIM = 256
ROW_SHARD = 256
RMS_EPS = 1e-6
ATTN_SCALE = HEAD_DIM ** -0.5
ATTN_BLOCK = 128
EXP_ZERO = -87.5
SUBLANES = 8
V7X_VMEM_LIMIT = 52 * 1024 * 1024

ADAM_LR = 0.001
ADAM_B1 = 0.9
ADAM_B2 = 0.999
ADAM_EPS = 1e-08
ADAM_WD = 0.01
ADAM_STEP = 10

MESH = pl.DeviceIdType.MESH
ANY = pl.BlockSpec(memory_space=pl.ANY)


def _cparams(n_grid=0, parallel=0):
    sem = tuple(["parallel"] * parallel + ["arbitrary"] * (n_grid - parallel))
    return pltpu.CompilerParams(dimension_semantics=sem, vmem_limit_bytes=V7X_VMEM_LIMIT)


def _dot(a, b):
    return jnp.dot(a, b, preferred_element_type=F32)


def _dot_nt(a, b):
    return lax.dot_general(a, b, (((1,), (1,)), ((), ())), preferred_element_type=F32)


def _dot_tn(a, b):
    return lax.dot_general(a, b, (((0,), (0,)), ((), ())), preferred_element_type=F32)


def _split_hilo(a):
    hi = a.astype(BF16)
    lo = (a - hi.astype(F32)).astype(BF16)
    return hi, lo


def _dot_hilo(a, b):
    hi, lo = _split_hilo(a)
    return _dot(hi, b) + _dot(lo, b)


def _sigmoid(x):
    return 0.5 * (jnp.tanh(0.5 * x) + 1.0)


_GELU_C = math.sqrt(2.0 / math.pi)


def _gelu(x):
    return 0.5 * x * (1.0 + jnp.tanh(_GELU_C * (x + 0.044715 * (x * x * x))))


def _gelu_grad(x):
    t = jnp.tanh(_GELU_C * (x + 0.044715 * (x * x * x)))
    return 0.5 * (1.0 + t) + 0.5 * x * (1.0 - t * t) * (_GELU_C * (1.0 + 3.0 * 0.044715 * (x * x)))


def _row_tile(s, want):
    for t in range(min(s, want), 7, -1):
        if s % t == 0 and t % SUBLANES == 0:
            return t
    return s


def _coords():
    return lax.axis_index("x"), lax.axis_index("y"), lax.axis_index("c")


def _other_chips(x, y):
    return [(1 - x, y), (x, 1 - y), (1 - x, 1 - y)]


def _remote(src, dst, send_sem, recv_sem, dev):
    return pltpu.make_async_remote_copy(src_ref=src, dst_ref=dst, send_sem=send_sem, recv_sem=recv_sem,
                                        device_id=dev, device_id_type=MESH)


def _set_block(buf, block, index):
    return lax.dynamic_update_index_in_dim(buf, block, index, 0)


def _gather_sems(n):
    return [pltpu.SemaphoreType.DMA((3 * n,)) for _ in range(4)]


def _gather_copies(ins, bases, outs, sems):
    send_sems, recv_sems, fwd_send, fwd_recv = sems
    x, y, c = _coords()
    me_chip = 2 * x + y
    sibling = (x, y, 1 - c)
    first, landed, passed, from_sibling = [], [], [], []
    for k in range(len(ins)):
        for j, (cx, cy) in enumerate(_other_chips(x, y)):
            i = 3 * k + j
            first.append(_remote(ins[k].at[bases[k] + c], outs[k].at[me_chip, c], send_sems.at[i], recv_sems.at[i], (cx, cy, c)))
            blk = outs[k].at[2 * cx + cy, c]
            landed.append(_remote(blk, blk, send_sems.at[i], recv_sems.at[i], (cx, cy, c)))
            passed.append(_remote(blk, blk, fwd_send.at[i], fwd_recv.at[i], sibling))
            blk = outs[k].at[2 * cx + cy, 1 - c]
            from_sibling.append(_remote(blk, blk, fwd_send.at[i], fwd_recv.at[i], sibling))
    return first, landed, passed, from_sibling


def _gather_start(ins, bases, outs, sems):
    for cp in _gather_copies(ins, bases, outs, sems)[0]:
        cp.start()


def _gather_finish(ins, bases, outs, sems):
    first, landed, passed, from_sibling = _gather_copies(ins, bases, outs, sems)
    for arrived, forward in zip(landed, passed):
        arrived.wait_recv()
        forward.start()
    for cp in from_sibling:
        cp.wait_recv()
    for cp in first + passed:
        cp.wait_send()


def _gather_outputs(arrs):
    return [jax.ShapeDtypeStruct((N_CHIPS, 2) + a.shape[1:], a.dtype) for a in arrs]


def _gather_own(outs, arrs, bases):
    me_chip = 2 * lax.axis_index("x") + lax.axis_index("y")
    return [_set_block(o, lax.slice_in_dim(a, b, b + 2, axis=0), me_chip) for o, a, b in zip(outs, arrs, bases)]


def _chip_gather(arrs, name, bases=None):
    n = len(arrs)
    bases = [0] * n if bases is None else bases

    def body(*refs):
        ins, outs, sems = refs[:n], refs[n:2 * n], refs[2 * n:]
        _gather_start(ins, bases, outs, sems)
        _gather_finish(ins, bases, outs, sems)

    outs = pl.pallas_call(
        body, name=name, out_shape=_gather_outputs(arrs),
        in_specs=[ANY] * n, out_specs=[ANY] * n, scratch_shapes=_gather_sems(n),
    )(*arrs)
    return _gather_own(outs, arrs, bases)


def _sibling_push(arrs, name):
    n = len(arrs)

    def body(*refs):
        ins, outs = refs[:n], refs[n:2 * n]
        send_sems, recv_sems = refs[2 * n:]
        x, y, c = _coords()
        cps = [_remote(ins[k].at[pl.ds(0, N_CHIPS), 1 - c], outs[k], send_sems.at[k], recv_sems.at[k], (x, y, 1 - c))
               for k in range(n)]
        for cp in cps:
            cp.start()
        for cp in cps:
            cp.wait_recv()
        for cp in cps:
            cp.wait_send()

    return pl.pallas_call(
        body, name=name,
        out_shape=[jax.ShapeDtypeStruct((a.shape[0],) + a.shape[2:], a.dtype) for a in arrs],
        in_specs=[ANY] * n, out_specs=[ANY] * n,
        scratch_shapes=[pltpu.SemaphoreType.DMA((n,)), pltpu.SemaphoreType.DMA((n,))],
    )(*arrs)


def _sibling_join(pieces, out_shapes, name):
    n = len(pieces)
    no = len(out_shapes)

    def body(*refs):
        ins, outs = refs[:n], refs[n:n + no]
        send_sems, recv_sems = refs[n + no:]
        x, y, c = _coords()
        sibling = (x, y, 1 - c)
        cps = [_remote(ins[k], outs[o].at[lead + (c,)], send_sems.at[k], recv_sems.at[k], sibling)
               for k, (_, o, lead) in enumerate(pieces)]
        for cp in cps:
            cp.start()
        for k, (_, o, lead) in enumerate(pieces):
            blk = outs[o].at[lead + (1 - c,)]
            _remote(blk, blk, send_sems.at[k], recv_sems.at[k], sibling).wait_recv()
        for cp in cps:
            cp.wait_send()

    outs = pl.pallas_call(
        body, name=name,
        out_shape=[jax.ShapeDtypeStruct(sh, F32) for sh in out_shapes],
        in_specs=[ANY] * n, out_specs=[ANY] * no,
        scratch_shapes=[pltpu.SemaphoreType.DMA((n,)), pltpu.SemaphoreType.DMA((n,))],
    )(*[a for a, _, _ in pieces])
    outs = list(outs)
    c = lax.axis_index("c")
    for a, o, lead in pieces:
        block = a.reshape((1,) * (len(lead) + 1) + a.shape)
        outs[o] = lax.dynamic_update_slice(outs[o], block, lead + (c,) + (0,) * a.ndim)
    return outs


def _scatter_sems(n):
    return [pltpu.SemaphoreType.DMA((3 * n,)), pltpu.SemaphoreType.DMA((3 * n,))]


def _scatter_copies(ins, outs, sems):
    send_sems, recv_sems = sems
    x, y, c = _coords()
    me_chip = 2 * x + y
    sends, arrivals = [], []
    for k in range(len(ins)):
        for j, (cx, cy) in enumerate(_other_chips(x, y)):
            i = 3 * k + j
            sends.append(_remote(ins[k].at[2 * cx + cy], outs[k].at[me_chip], send_sems.at[i], recv_sems.at[i], (cx, cy, c)))
            blk = outs[k].at[2 * cx + cy]
            arrivals.append(_remote(blk, blk, send_sems.at[i], recv_sems.at[i], (cx, cy, c)))
    return sends, arrivals


def _scatter_start(ins, outs, sems):
    for cp in _scatter_copies(ins, outs, sems)[0]:
        cp.start()


def _scatter_finish(ins, outs, sems):
    sends, arrivals = _scatter_copies(ins, outs, sems)
    for cp in arrivals:
        cp.wait_recv()
    for cp in sends:
        cp.wait_send()


def _scatter_own(outs, arrs):
    me_chip = 2 * lax.axis_index("x") + lax.axis_index("y")
    return [_set_block(o, lax.dynamic_index_in_dim(a, me_chip, 0, keepdims=False), me_chip) for o, a in zip(outs, arrs)]


def _chip_scatter(arrs, name):
    n = len(arrs)

    def body(*refs):
        ins, outs, sems = refs[:n], refs[n:2 * n], refs[2 * n:]
        _scatter_start(ins, outs, sems)
        _scatter_finish(ins, outs, sems)

    outs = pl.pallas_call(
        body, name=name,
        out_shape=[jax.ShapeDtypeStruct(a.shape, a.dtype) for a in arrs],
        in_specs=[ANY] * n, out_specs=[ANY] * n, scratch_shapes=_scatter_sems(n),
    )(*arrs)
    return _scatter_own(outs, arrs)


def _as_rows(a, lanes):
    return a.reshape(-1, lanes)


def _add_my_half(v, recv, out_dtype, name):
    n_sh, _, h, cdim = v.shape
    tr = _row_tile(h, 512)

    def body(c_ref, a_ref, b_ref, o_ref):
        o_ref[...] = (a_ref[...].astype(F32) + b_ref[...].astype(F32)).astype(out_dtype)

    c = lax.axis_index("c").astype(jnp.int32).reshape(1)
    return pl.pallas_call(
        body, name=name,
        grid_spec=pltpu.PrefetchScalarGridSpec(
            num_scalar_prefetch=1, grid=(n_sh, h // tr),
            in_specs=[pl.BlockSpec((None, None, tr, cdim), lambda sh, i, c_ref: (sh, c_ref[0], i, 0)),
                      pl.BlockSpec((None, tr, cdim), lambda sh, i, c_ref: (sh, i, 0))],
            out_specs=pl.BlockSpec((None, tr, cdim), lambda sh, i, c_ref: (sh, i, 0))),
        out_shape=jax.ShapeDtypeStruct((n_sh, h, cdim), out_dtype),
        compiler_params=_cparams(2),
    )(c, v, recv)


def _sum4(parts, name):
    _, r, cdim = parts.shape
    tr = _row_tile(r, 512)

    def body(p_ref, o_ref):
        acc = p_ref[0].astype(F32) + p_ref[1].astype(F32)
        acc = acc + p_ref[2].astype(F32)
        o_ref[...] = acc + p_ref[3].astype(F32)

    return pl.pallas_call(
        body, name=name, grid=(r // tr,),
        in_specs=[pl.BlockSpec((N_CHIPS, tr, cdim), lambda i: (0, i, 0))],
        out_specs=pl.BlockSpec((tr, cdim), lambda i: (i, 0)),
        out_shape=jax.ShapeDtypeStruct((r, cdim), F32),
        compiler_params=_cparams(1),
    )(parts)


def _adamw_math(w, g, m, v):
    c1 = 1.0 - ADAM_B1 ** ADAM_STEP
    c2 = 1.0 - ADAM_B2 ** ADAM_STEP
    nm = ADAM_B1 * m + (1.0 - ADAM_B1) * g
    nv = ADAM_B2 * v + (1.0 - ADAM_B2) * (g * g)
    delta = -ADAM_LR * ((nm / c1) / (jnp.sqrt(nv / c2) + ADAM_EPS) + ADAM_WD * w)
    return delta, nm, nv


def _adamw(w, g, m, v, name):
    r, cdim = w.shape
    tr = _row_tile(r, 256)

    def body(w_ref, g_ref, m_ref, v_ref, d_ref, nm_ref, nv_ref):
        d_ref[...], nm_ref[...], nv_ref[...] = _adamw_math(w_ref[...], g_ref[...], m_ref[...], v_ref[...])

    spec = pl.BlockSpec((tr, cdim), lambda i: (i, 0))
    return pl.pallas_call(
        body, name=name, grid=(r // tr,),
        in_specs=[spec] * 4, out_specs=[spec] * 3,
        out_shape=[jax.ShapeDtypeStruct((r, cdim), F32)] * 3,
        compiler_params=_cparams(1),
    )(w, g, m, v)


def _adamw_many(ws, gs, ms, vs, name, per_layer):
    n = len(ws)

    def body(*refs):
        for k in range(n):
            w, g, m, v = (refs[j * n + k][...] for j in range(4))
            outs = _adamw_math(w, g, m, v)
            for j in range(3):
                refs[(4 + j) * n + k][...] = outs[j]

    shapes = [jax.ShapeDtypeStruct(w.shape, F32) for w in ws]
    if per_layer:
        specs = [pl.BlockSpec((None,) + w.shape[1:], lambda l, nd=w.ndim: (l,) + (0,) * (nd - 1)) for w in ws]
        call = pl.pallas_call(body, name=name, grid=(N_LAYERS,), in_specs=specs * 4, out_specs=specs * 3,
                              out_shape=shapes * 3, compiler_params=_cparams(1))
    else:
        call = pl.pallas_call(body, name=name, out_shape=shapes * 3, compiler_params=_cparams())
    outs = call(*ws, *gs, *ms, *vs)
    return outs[0:n], outs[n:2 * n], outs[2 * n:3 * n]


def _discretise(a_re, a_im, log_dt, b_re, b_im):
    dt = jnp.exp(log_dt)
    mag = jnp.exp(a_re * dt)
    ab_re = mag * jnp.cos(a_im * dt)
    ab_im = mag * jnp.sin(a_im * dt)
    num_re = ab_re - 1.0
    num_im = ab_im
    den = a_re * a_re + a_im * a_im
    f_re = (num_re * a_re + num_im * a_im) / den
    f_im = (num_im * a_re - num_re * a_im) / den
    bb_re = f_re * b_re - f_im * b_im
    bb_im = f_re * b_im + f_im * b_re
    return ab_re, ab_im, bb_re, bb_im


def _disc_shapes():
    col = jax.ShapeDtypeStruct((1, N_STATES), F32)
    mat = jax.ShapeDtypeStruct((SSM_GROUP, N_STATES), F32)
    return col, mat


def _disc_fwd(a_re, a_im, log_dt, b_re, b_im):
    col, mat = _disc_shapes()

    def body(ar, ai, ld, br, bi, o0, o1, o2, o3):
        outs = _discretise(ar[...], ai[...], ld[...], br[...], bi[...])
        for o, val in zip((o0, o1, o2, o3), outs):
            o[...] = val

    return pl.pallas_call(body, name="ssm_discretise", out_shape=[col, col, mat, mat],
                          compiler_params=_cparams())(a_re, a_im, log_dt, b_re, b_im)


def _disc_bwd(a_re, a_im, log_dt, b_re, b_im, g_ab_re, g_ab_im, g_bb_re, g_bb_im):
    col, mat = _disc_shapes()

    def body(ar, ai, ld, br, bi, g0, g1, g2, g3, o0, o1, o2, o3, o4):
        _, vjp = jax.vjp(_discretise, ar[...], ai[...], ld[...], br[...], bi[...])
        grads = vjp((g0[...], g1[...], g2[...], g3[...]))
        for o, val in zip((o0, o1, o2, o3, o4), grads):
            o[...] = val

    return pl.pallas_call(body, name="ssm_discretise_bwd", out_shape=[col, col, col, mat, mat],
                          compiler_params=_cparams())(a_re, a_im, log_dt, b_re, b_im, g_ab_re, g_ab_im, g_bb_re, g_bb_im)


def _cmul(ar, ai, br, bi):
    return ar * br - ai * bi, ar * bi + ai * br


def _scan_powers(ab_re, ab_im, length, reverse):
    br = ab_re.reshape(1, N_STATES)
    bi = -ab_im.reshape(1, N_STATES) if reverse else ab_im.reshape(1, N_STATES)
    ks = (length - jnp.arange(length)) if reverse else (jnp.arange(length) + 1)
    pr = jnp.ones((length, N_STATES), F32)
    pi = jnp.zeros((length, N_STATES), F32)
    for bit in range(length.bit_length()):
        take = ((ks >> bit) & 1)[:, None] == 1
        mr, mi = _cmul(pr, pi, br, bi)
        pr, pi = jnp.where(take, mr, pr), jnp.where(take, mi, pi)
        br, bi = _cmul(br, bi, br, bi)
    split = lambda t: t.reshape(length, SSM_CHUNKS, CH_S).transpose(1, 0, 2)
    return jnp.concatenate([split(pr), split(pi)], axis=-1)


def _interleave_chunks(v):
    rows, width = v.shape
    return pltpu.einshape("cjw->jcw", v.reshape(SUBLANES, rows // SUBLANES, width)).reshape(rows, width)


def _time_order(v):
    rows, width = v.shape
    return pltpu.einshape("jcw->cjw", v.reshape(rows // SUBLANES, SUBLANES, width)).reshape(rows, width)


def _block_diag_in(bb):
    t = bb.reshape(SSM_GROUP, SSM_CHUNKS, 8, SSM_STATE)
    eye = jnp.eye(8, dtype=bb.dtype)
    return jnp.einsum("hjgp,gk->jghkp", t, eye).reshape(SSM_CHUNKS, CH_W, CH_S)


def _block_diag_in_t(d):
    t = d.reshape(SSM_CHUNKS, 8, SSM_GROUP, 8, SSM_STATE)
    return jnp.einsum("jghgp->hjgp", t).reshape(SSM_GROUP, N_STATES)


def _block_diag_out(c):
    t = c.reshape(SSM_CHUNKS, 8, SSM_GROUP, SSM_STATE)
    eye = jnp.eye(8, dtype=c.dtype)
    return jnp.einsum("jghp,gk->jgpkh", t, eye).reshape(SSM_CHUNKS, CH_S, CH_W)


def _block_diag_out_t(d):
    t = d.reshape(SSM_CHUNKS, 8, SSM_STATE, 8, SSM_GROUP)
    return jnp.einsum("jgpgh->jghp", t).reshape(SSM_GROUPS, SSM_GROUP, SSM_STATE)


def _head_ones():
    r = jnp.arange(ATTN_WIDTH) // HEAD_DIM
    return jnp.where(r[:, None] == r[None, :], 1.0 / HEAD_DIM, 0.0).astype(BF16)


def _in_proj(h, g1, w_in_l, qg, kg):
    s = h.shape[0]
    tm = _row_tile(s, 256)

    def body(h_ref, g_ref, w_ref, qg_ref, kg_ref, ones_ref, proj_ref, qkv_ref):
        x = h_ref[...]
        r = lax.rsqrt(jnp.mean(x * x, axis=-1, keepdims=True) + RMS_EPS)
        hn = (x * r * g_ref[...]).astype(BF16)
        for sh in range(N_CHIPS):
            proj_ref[:, IN_SHARD * sh:IN_SHARD * (sh + 1)] = _dot(hn, w_ref[sh])
        ones = ones_ref[...]
        q = proj_ref[:, 1024:1536]
        k = proj_ref[:, 1536:2048]
        rq = lax.rsqrt(_dot_hilo(q * q, ones) + RMS_EPS)
        rk = lax.rsqrt(_dot_hilo(k * k, ones) + RMS_EPS)
        qkv_ref[:, 0:512] = (q * rq * qg_ref[...] * ATTN_SCALE).astype(BF16)
        qkv_ref[:, 512:1024] = (k * rk * kg_ref[...]).astype(BF16)
        qkv_ref[:, 1024:1536] = proj_ref[:, 2048:2560].astype(BF16)

    full = lambda shape: pl.BlockSpec(shape, lambda i: (0,) * len(shape))
    return pl.pallas_call(
        body, name="in_proj", grid=(s // tm,),
        in_specs=[pl.BlockSpec((tm, D_MODEL), lambda i: (i, 0)), full((1, D_MODEL)),
                  full((N_CHIPS, D_MODEL, IN_SHARD)),
                  full((1, ATTN_WIDTH)), full((1, ATTN_WIDTH)), full((ATTN_WIDTH, ATTN_WIDTH))],
        out_specs=[pl.BlockSpec((tm, IN_COLS), lambda i: (i, 0)), pl.BlockSpec((tm, 3 * ATTN_WIDTH), lambda i: (i, 0))],
        out_shape=[jax.ShapeDtypeStruct((s, IN_COLS), F32), jax.ShapeDtypeStruct((s, 3 * ATTN_WIDTH), BF16)],
        compiler_params=_cparams(1),
    )(h, g1, w_in_l, qg, kg, _head_ones())


def _row_bcast(ref, k, lo):
    return jnp.broadcast_to(ref[pl.ds(k, 1), lo:lo + CH_S], (SUBLANES, CH_S))


def _chunk_scan(x_ref, tab_ref, carry_ref, length, reverse, tail=None):
    row = lax.broadcasted_iota(jnp.int32, (SUBLANES, CH_S), 0)
    one, full = (length - 1, 0) if reverse else (0, length - 1)
    ar, ai = _row_bcast(tab_ref, one, 0), _row_bcast(tab_ref, one, CH_S)
    fr, fi = _row_bcast(tab_ref, full, 0), _row_bcast(tab_ref, full, CH_S)
    step = lambda jj: (length - 1 - jj) if reverse else jj

    def local(jj, carry):
        cr, ci = carry
        r0 = pl.multiple_of(step(jj) * SUBLANES, SUBLANES)
        xr = x_ref[pl.ds(r0, SUBLANES), 0:CH_S] + (ar * cr - ai * ci)
        xi = x_ref[pl.ds(r0, SUBLANES), CH_S:2 * CH_S] + (ar * ci + ai * cr)
        x_ref[pl.ds(r0, SUBLANES), 0:CH_S] = xr
        x_ref[pl.ds(r0, SUBLANES), CH_S:2 * CH_S] = xi
        return xr, xi

    zero = jnp.zeros((SUBLANES, CH_S), F32)
    er, ei = lax.fori_loop(0, length, local, (zero, zero))

    first, shift = (SUBLANES - 1, SUBLANES - 1) if reverse else (0, 1)
    hr = jnp.where(row == first, carry_ref[:, 0:CH_S], 0.0)
    hi = jnp.where(row == first, carry_ref[:, CH_S:2 * CH_S], 0.0)
    sr, si = pltpu.roll(er, shift, 0), pltpu.roll(ei, shift, 0)
    for k in range(1, SUBLANES):
        tr, ti = pltpu.roll(hr, shift, 0), pltpu.roll(hi, shift, 0)
        here = row == ((SUBLANES - 1 - k) if reverse else k)
        hr, hi = (jnp.where(here, fr * tr - fi * ti + sr, hr), jnp.where(here, fr * ti + fi * tr + si, hi))
    last = 0 if reverse else SUBLANES - 1
    outr, outi = fr * hr - fi * hi + er, fr * hi + fi * hr + ei
    carry_ref[:, 0:CH_S] = jnp.broadcast_to(outr[last:last + 1, :], (SUBLANES, CH_S))
    carry_ref[:, CH_S:2 * CH_S] = jnp.broadcast_to(outi[last:last + 1, :], (SUBLANES, CH_S))

    def fix(jj, carry):
        j = step(jj)
        r0 = pl.multiple_of(j * SUBLANES, SUBLANES)
        pr, pi = _row_bcast(tab_ref, j, 0), _row_bcast(tab_ref, j, CH_S)
        xr = x_ref[pl.ds(r0, SUBLANES), 0:CH_S] + (pr * hr - pi * hi)
        xi = x_ref[pl.ds(r0, SUBLANES), CH_S:2 * CH_S] + (pr * hi + pi * hr)
        x_ref[pl.ds(r0, SUBLANES), 0:CH_S] = xr
        x_ref[pl.ds(r0, SUBLANES), CH_S:2 * CH_S] = xi
        if tail is None:
            return carry
        return tail(r0, xr, xi, carry)

    return fix, (hr, hi)


def _ssm_scan_fwd(proj, wb, tab, wc, gather=None, gather_bases=None):
    s = proj.shape[0]
    tm = _row_tile(s, 512)
    nt = s // tm
    length = tm // SUBLANES
    gather = [] if gather is None else gather
    ng = len(gather)

    def body(*refs):
        u_ref, wb_ref, tab_ref, wc_ref = refs[0:4]
        g_ins = refs[4:4 + ng]
        xs_ref, y_ref = refs[4 + ng:6 + ng]
        g_outs = refs[6 + ng:6 + 2 * ng]
        carry_ref = refs[6 + 2 * ng]
        sems = refs[7 + 2 * ng:]
        j, i = pl.program_id(0), pl.program_id(1)

        @pl.when(i == 0)
        def _():
            carry_ref[...] = jnp.zeros_like(carry_ref)

        if ng:
            @pl.when(jnp.logical_and(j == 0, i == 0))
            def _():
                _gather_start(g_ins, gather_bases, g_outs, sems)

        xs_ref[...] = _dot(_interleave_chunks(u_ref[...]).astype(BF16), wb_ref[...])
        fix, start = _chunk_scan(xs_ref, tab_ref, carry_ref, length, reverse=False)
        lax.fori_loop(0, length, fix, start, unroll=2)
        y_ref[...] = _time_order(_dot(xs_ref[...].astype(BF16), wc_ref[...]))

        if ng:
            @pl.when(jnp.logical_and(j == SSM_CHUNKS - 1, i == nt - 1))
            def _():
                _gather_finish(g_ins, gather_bases, g_outs, sems)

    outs = pl.pallas_call(
        body, name="ssm_scan_gather" if ng else "ssm_scan", grid=(SSM_CHUNKS, nt),
        in_specs=[pl.BlockSpec((tm, CH_W), lambda j, i: (i, j)),
                  pl.BlockSpec((None, CH_W, 2 * CH_S), lambda j, i: (j, 0, 0)),
                  pl.BlockSpec((None, length, 2 * CH_S), lambda j, i: (j, 0, 0)),
                  pl.BlockSpec((None, 2 * CH_S, CH_W), lambda j, i: (j, 0, 0))] + [ANY] * ng,
        out_specs=[pl.BlockSpec((None, tm, 2 * CH_S), lambda j, i: (j, i, 0)),
                   pl.BlockSpec((tm, CH_W), lambda j, i: (i, j))] + [ANY] * ng,
        out_shape=[jax.ShapeDtypeStruct((SSM_CHUNKS, s, 2 * CH_S), F32), jax.ShapeDtypeStruct((s, SSM_WIDTH), F32)]
        + _gather_outputs(gather),
        scratch_shapes=[pltpu.VMEM((SUBLANES, 2 * CH_S), F32)] + (_gather_sems(ng) if ng else []),
        compiler_params=_cparams(2),
    )(proj, wb, tab, wc, *gather)
    return outs[0], outs[1], (_gather_own(outs[2:], gather, gather_bases) if ng else [])


def _glu_forward(y, u, d, wg_ref, bg):
    yf = y + d * u
    z = _gelu(yf)
    zb = z.astype(BF16)
    zz = jnp.concatenate([_dot(zb, wg_ref[sh]) for sh in range(N_CHIPS)], axis=-1) + bg
    return yf, z, zz[:, 0:SSM_WIDTH], zz[:, SSM_WIDTH:2 * SSM_WIDTH]


def _ssm_glu_fwd(y, proj, d, w_glu_l, b_glu):
    s = y.shape[0]
    tm = _row_tile(s, 512)

    def body(y_ref, u_ref, gs_ref, d_ref, wg_ref, bg_ref, o_ref):
        _, _, val, gate = _glu_forward(y_ref[...], u_ref[...], d_ref[...], wg_ref, bg_ref[...])
        gs = gs_ref[...]
        o_ref[...] = val * _sigmoid(gate) * (gs * _sigmoid(gs))

    row = lambda i: (i, 0)
    return pl.pallas_call(
        body, name="ssm_glu", grid=(s // tm,),
        in_specs=[pl.BlockSpec((tm, SSM_WIDTH), row), pl.BlockSpec((tm, SSM_WIDTH), row),
                  pl.BlockSpec((tm, SSM_WIDTH), lambda i: (i, 1)), pl.BlockSpec((1, SSM_WIDTH), lambda i: (0, 0)),
                  pl.BlockSpec((N_CHIPS, SSM_WIDTH, ROW_SHARD), lambda i: (0, 0, 0)),
                  pl.BlockSpec((1, 2 * SSM_WIDTH), lambda i: (0, 0))],
        out_specs=pl.BlockSpec((tm, SSM_WIDTH), row),
        out_shape=jax.ShapeDtypeStruct((s, SSM_WIDTH), F32),
        compiler_params=_cparams(1),
    )(y, proj, proj, d, w_glu_l, b_glu)


def _tri(kind):
    r = jnp.arange(ATTN_BLOCK)
    if kind == "suffix_incl":
        m = r[:, None] >= r[None, :]
    else:
        m = r[:, None] < r[None, :]
    return jnp.concatenate([m, jnp.ones_like(m)], axis=1).astype(BF16)


def _head_masks():
    lane = lax.broadcasted_iota(jnp.int32, (1, 2 * HEAD_DIM), 1)
    return [lane < HEAD_DIM, lane >= HEAD_DIM]


def _chain_step(t, base, n_sub, first, q_ref, k_ref, tri_ref, l_scr, per_chain):
    tb = ATTN_BLOCK
    row = lax.broadcasted_iota(jnp.int32, (tb, tb), 0)
    col = lax.broadcasted_iota(jnp.int32, (tb, tb), 1)
    masks = _head_masks()
    blks = [base + a - t for a in range(n_sub)]
    r0s = [pl.multiple_of(jnp.maximum(blk, 0) * tb, tb) for blk in blks]
    zs = []
    for a in range(n_sub):
        kb = k_ref[pl.ds(r0s[a], tb), :]
        qa = q_ref[a * tb:(a + 1) * tb, :]
        for mask in masks:
            zs.append(_dot_nt(jnp.where(mask, qa, jnp.zeros_like(qa)), kb))
    parts = []
    for z in zs:
        ls = jnp.minimum(-z, 0.0) - jnp.log(1.0 + jnp.exp(-jnp.abs(z)))
        if first:
            ls = jnp.where(col < row, ls, 0.0)
        parts.append(_split_hilo(ls))
    tri = tri_ref[...]
    sums = [_dot(hi, tri) + _dot(lo, tri) for hi, lo in parts]
    top = None
    ws = []
    for c, (z, sm) in enumerate(zip(zs, sums)):
        if first:
            lsum = jnp.zeros((tb, tb), F32)
        else:
            lsum = l_scr[c] + jnp.where(blks[c // 2] >= 0, 0.0, -1e30)
        w = jnp.exp(z + sm[:, 0:tb] + lsum)
        if first:
            w = jnp.where(col < row, w, 0.0)
        ws.append(w)
        lsum = lsum + sm[:, tb:2 * tb]
        l_scr[c] = lsum
        top = lsum if top is None else jnp.maximum(top, lsum)
    for c, (z, w) in enumerate(zip(zs, ws)):
        per_chain(c // 2, c % 2, c, r0s[c // 2], z, w)
    return jnp.max(top)


def _chain_sweep(base, n_sub, q_ref, k_ref, tri_ref, l_scr, per_chain):
    top = _chain_step(0, base, n_sub, True, q_ref, k_ref, tri_ref, l_scr, functools.partial(per_chain, 0))

    def cond(carry):
        t, top = carry
        return jnp.logical_and(t <= base + n_sub - 1, top > EXP_ZERO)

    def step(carry):
        t, _ = carry
        return t + 1, _chain_step(t, base, n_sub, False, q_ref, k_ref, tri_ref, l_scr, functools.partial(per_chain, t))

    steps, _ = lax.while_loop(cond, step, (jnp.int32(1), top))
    return steps


ATTN_SUB_FWD = 8
ATTN_SUB_BWD = 4


def _attn_fwd(qkv, proj, gather=None, gather_bases=None):
    s = qkv.shape[0]
    tb = ATTN_BLOCK
    n_sub = min(ATTN_SUB_FWD, s // tb)
    tq = n_sub * tb
    n_hp = ATTN_WIDTH // (2 * HEAD_DIM)
    gather = [] if gather is None else gather
    ng = len(gather)

    def body(*refs):
        q_ref, k_ref, v_ref, g_ref, tri_ref = refs[0:5]
        g_ins = refs[5:5 + ng]
        o_ref, ya_ref = refs[5 + ng:7 + ng]
        g_outs = refs[7 + ng:7 + 2 * ng]
        l_scr = refs[7 + 2 * ng]
        sems = refs[8 + 2 * ng:]
        i = pl.program_id(1)
        masks = _head_masks()
        o_ref[...] = jnp.zeros_like(o_ref)

        if ng:
            @pl.when(jnp.logical_and(pl.program_id(0) == 0, i == 0))
            def _():
                _gather_start(g_ins, gather_bases, g_outs, sems)

        def per_chain(t, a, h, c, r0, z, w):
            vb = v_ref[pl.ds(r0, tb), :]
            vb = jnp.where(masks[h], vb, jnp.zeros_like(vb))
            o_ref[a * tb:(a + 1) * tb, :] += _dot(w.astype(BF16), vb)

        _chain_sweep(i * n_sub, n_sub, q_ref, k_ref, tri_ref, l_scr, per_chain)
        g = g_ref[...]
        ya_ref[...] = o_ref[...] * (g * _sigmoid(g))

        if ng:
            @pl.when(jnp.logical_and(pl.program_id(0) == n_hp - 1, i == s // tq - 1))
            def _():
                _gather_finish(g_ins, gather_bases, g_outs, sems)

    hp_blk = lambda off: pl.BlockSpec((tq, 2 * HEAD_DIM), lambda hp, i: (i, off + hp))
    res = lambda off: pl.BlockSpec((s, 2 * HEAD_DIM), lambda hp, i: (0, off + hp))
    outs = pl.pallas_call(
        body, name="attn_fwd_gather" if ng else "attn_fwd", grid=(n_hp, s // tq),
        in_specs=[hp_blk(0), res(4), res(8), hp_blk(20), pl.BlockSpec((tb, 2 * tb), lambda hp, i: (0, 0))] + [ANY] * ng,
        out_specs=[hp_blk(0), hp_blk(0)] + [ANY] * ng,
        out_shape=[jax.ShapeDtypeStruct((s, ATTN_WIDTH), F32)] * 2 + _gather_outputs(gather),
        scratch_shapes=[pltpu.VMEM((2 * n_sub, tb, tb), F32)] + (_gather_sems(ng) if ng else []),
        compiler_params=_cparams(2),
    )(qkv, qkv, qkv, proj, _tri("suffix_incl"), *gather)
    return outs[0], outs[1], (_gather_own(outs[2:], gather, gather_bases) if ng else [])


def _rms_rows(x, g):
    r = lax.rsqrt(jnp.mean(x * x, axis=-1, keepdims=True) + RMS_EPS)
    return r, x * r * g


def _ple_forward(h1, p, g2, wpg_ref, wpp_ref):
    r2, hn2 = _rms_rows(h1, g2)
    hb = hn2.astype(BF16)
    gpre = _dot(hb[:, 0:ROW_SHARD], wpg_ref[0])
    for sh in range(1, N_CHIPS):
        gpre = gpre + _dot(hb[:, ROW_SHARD * sh:ROW_SHARD * (sh + 1)], wpg_ref[sh])
    gate = _sigmoid(gpre)
    pb = p.astype(BF16)
    pp = jnp.concatenate([_dot(pb, wpp_ref[sh]) for sh in range(N_CHIPS)], axis=-1)
    return r2, hb, gate, pp


def _colsum8(a):
    t = a.shape[0]
    return a.reshape(t // SUBLANES, SUBLANES, a.shape[1]).sum(axis=0)


def _sq_err_grad(y, target):
    e = y - target
    sq = _colsum8(e * e)
    part = sq[:, 0:128]
    for b in range(1, D_MODEL // 128):
        part = part + sq[:, 128 * b:128 * (b + 1)]
    return e / D_MODEL, part


def _out_ple(h, ys, ya, p, g2, w_out_l, w_pg_l, w_pp_l, target=None):
    s = h.shape[0]
    tm = _row_tile(s, 256)
    last = target is not None

    def body(*refs):
        h_ref, ys_ref, ya_ref, p_ref, g_ref, wo_ref, wpg_ref, wpp_ref = refs[0:8]
        h1_ref, h2_ref = refs[8 + last], refs[9 + last]
        ysb = ys_ref[...].astype(BF16)
        yab = ya_ref[...].astype(BF16)
        h1 = h_ref[...]
        for sh, src in enumerate((ysb[:, 0:ROW_SHARD], ysb[:, ROW_SHARD:], yab[:, 0:ROW_SHARD], yab[:, ROW_SHARD:])):
            h1 = h1 + _dot(src, wo_ref[sh])
        _, _, gate, pp = _ple_forward(h1, p_ref[...], g_ref[...], wpg_ref, wpp_ref)
        h1_ref[...] = h1
        h2 = h1 + gate * pp
        if last:
            acc_ref = refs[11]

            @pl.when(pl.program_id(0) == 0)
            def _():
                acc_ref[...] = jnp.zeros_like(acc_ref)

            h2_ref[...], part = _sq_err_grad(h2, refs[8][...])
            acc_ref[...] += part
        else:
            h2_ref[...] = h2

    row = lambda i: (i, 0)
    big = pl.BlockSpec((tm, D_MODEL), row)
    wspec = lambda r, cdim: pl.BlockSpec((N_CHIPS, r, cdim), lambda i: (0, 0, 0))
    acc = pl.BlockSpec((SUBLANES, 128), lambda i: (0, 0))
    return pl.pallas_call(
        body, name="out_ple_loss" if last else "out_ple", grid=(s // tm,),
        in_specs=[big, pl.BlockSpec((tm, SSM_WIDTH), row), pl.BlockSpec((tm, ATTN_WIDTH), row),
                  pl.BlockSpec((tm, PLE_DIM), row), pl.BlockSpec((1, D_MODEL), lambda i: (0, 0)),
                  wspec(ROW_SHARD, D_MODEL), wspec(ROW_SHARD, D_MODEL), wspec(PLE_DIM, ROW_SHARD)] + [big] * last,
        out_specs=[big] * 2 + [acc] * last,
        out_shape=[jax.ShapeDtypeStruct((s, D_MODEL), F32)] * 2 + [jax.ShapeDtypeStruct((SUBLANES, 128), F32)] * last,
        compiler_params=_cparams(1),
    )(h, ys, ya, p, g2, w_out_l, w_pg_l, w_pp_l, *([target] if last else []))


def _rms_bwd(x, r, g, dy):
    gdy = g * dy
    dx = r * gdy - x * (r * r * r) * jnp.mean(x * gdy, axis=-1, keepdims=True)
    return dx, x * r * dy


def _out_ple_bwd(dh2, h1, p, g2, w_out_l, w_pg_l, w_pp_l):
    s = h1.shape[0]
    tm = _row_tile(s, 256)

    def body(dh2_ref, h1_ref, p_ref, g_ref, wo_ref, wpg_ref, wpp_ref,
             dh1_ref, dmix_ref, hn_ref, dgp_ref, dpp_ref, dh1b_ref, dg_ref):
        @pl.when(pl.program_id(0) == 0)
        def _():
            dg_ref[...] = jnp.zeros_like(dg_ref)

        h1 = h1_ref[...]
        dh2 = dh2_ref[...]
        g2v = g_ref[...]
        r2, hb, gate, pp = _ple_forward(h1, p_ref[...], g2v, wpg_ref, wpp_ref)
        dgp = (dh2 * pp) * gate * (1.0 - gate)
        dgpb = dgp.astype(BF16)
        dhn = jnp.concatenate([_dot_nt(dgpb, wpg_ref[sh]) for sh in range(N_CHIPS)], axis=-1)
        dx, dgrow = _rms_bwd(h1, r2, g2v, dhn)
        dh1 = dh2 + dx
        dh1b = dh1.astype(BF16)
        dh1_ref[...] = dh1
        dh1b_ref[...] = dh1b
        hn_ref[...] = hb
        dgp_ref[...] = dgpb
        dpp_ref[...] = (dh2 * gate).astype(BF16)
        dg_ref[...] += _colsum8(dgrow)
        for sh in range(N_CHIPS):
            dmix_ref[:, ROW_SHARD * sh:ROW_SHARD * (sh + 1)] = _dot_nt(dh1b, wo_ref[sh])

    row = lambda i: (i, 0)
    wspec = lambda r, cdim: pl.BlockSpec((N_CHIPS, r, cdim), lambda i: (0, 0, 0))
    big = pl.BlockSpec((tm, D_MODEL), row)
    return pl.pallas_call(
        body, name="out_ple_bwd", grid=(s // tm,),
        in_specs=[big, big, pl.BlockSpec((tm, PLE_DIM), row), pl.BlockSpec((1, D_MODEL), lambda i: (0, 0)),
                  wspec(ROW_SHARD, D_MODEL), wspec(ROW_SHARD, D_MODEL), wspec(PLE_DIM, ROW_SHARD)],
        out_specs=[big] * 6 + [pl.BlockSpec((SUBLANES, D_MODEL), lambda i: (0, 0))],
        out_shape=[jax.ShapeDtypeStruct((s, D_MODEL), F32)] * 2 + [jax.ShapeDtypeStruct((s, D_MODEL), BF16)] * 4
        + [jax.ShapeDtypeStruct((SUBLANES, D_MODEL), F32)],
        compiler_params=_cparams(1),
    )(dh2, h1, p, g2, w_out_l, w_pg_l, w_pp_l)


def _tn_matmul(a, b, n_blocks, block_a, name, into=None, first_block=0, total_blocks=None):
    s = a.shape[0]
    tk = _row_tile(s, 512)
    nk = s // tk
    total_blocks = n_blocks if total_blocks is None else total_blocks
    ka, nb = a.shape[1], b.shape[1]
    if block_a:
        ka //= n_blocks
    else:
        nb //= n_blocks

    def body(*refs):
        a_ref, b_ref, o_ref, acc_ref = refs[0], refs[1], refs[-2], refs[-1]

        @pl.when(pl.program_id(0) == 0)
        def _():
            acc_ref[...] = jnp.zeros_like(acc_ref)

        at = a_ref[...].astype(BF16).T
        bb = b_ref[...].astype(BF16)
        for sh in range(n_blocks):
            if block_a:
                acc_ref[sh] += _dot(at[ka * sh:ka * (sh + 1), :], bb)
            else:
                acc_ref[sh] += _dot(at, bb[:, nb * sh:nb * (sh + 1)])

        @pl.when(pl.program_id(0) == nk - 1)
        def _():
            o_ref[...] = acc_ref[...].astype(BF16)

    in_specs = [pl.BlockSpec((tk, a.shape[1]), lambda i: (i, 0)), pl.BlockSpec((tk, b.shape[1]), lambda i: (i, 0))]
    operands = [a, b]
    aliases = {}
    if into is not None:
        in_specs.append(ANY)
        operands.append(into)
        aliases = {2: 0}
    return pl.pallas_call(
        body, name=name, grid=(nk,),
        in_specs=in_specs,
        out_specs=pl.BlockSpec((n_blocks, ka, nb), lambda i: (first_block // n_blocks, 0, 0)),
        out_shape=jax.ShapeDtypeStruct((total_blocks, ka, nb), BF16),
        scratch_shapes=[pltpu.VMEM((n_blocks, ka, nb), F32)],
        input_output_aliases=aliases,
        compiler_params=_cparams(1),
    )(*operands)


def _attn_bwd(qkv, o, proj, dmix, scatter=None):
    scatter = [] if scatter is None else scatter
    nsc = len(scatter)
    s = qkv.shape[0]
    tb = ATTN_BLOCK
    nq = s // tb
    n_sub = min(ATTN_SUB_BWD, nq)
    tq = n_sub * tb
    n_chain = 2 * n_sub

    def body(*refs):
        q_ref, k_ref, v_ref, o_ref, g_ref, dya_ref, tri_s_ref, tri_p_ref = refs[0:8]
        sc_ins = refs[8:8 + nsc]
        dq_ref, dk_ref, dv_ref, dg_ref = refs[8 + nsc:12 + nsc]
        sc_outs = refs[12 + nsc:12 + 2 * nsc]
        do_scr, l_scr, g_scr, s_scr, w_scr = refs[12 + 2 * nsc:17 + 2 * nsc]
        sc_sems = refs[17 + 2 * nsc:]
        i = pl.program_id(1)
        base = i * n_sub

        if nsc:
            @pl.when(jnp.logical_and(pl.program_id(0) == 0, i == 0))
            def _():
                _scatter_start(sc_ins, sc_outs, sc_sems)

        @pl.when(i == 0)
        def _():
            dk_ref[...] = jnp.zeros_like(dk_ref)
            dv_ref[...] = jnp.zeros_like(dv_ref)

        g = g_ref[...]
        sg = _sigmoid(g)
        dya = dya_ref[...]
        do_scr[...] = (dya * (g * sg)).astype(BF16)
        dg_ref[...] = dya * o_ref[...] * (sg * (1.0 + g * (1.0 - sg)))
        dq_ref[...] = jnp.zeros_like(dq_ref)
        g_scr[...] = jnp.zeros_like(g_scr)
        masks = _head_masks()

        def keep(t, a, h, c, r0, z, w):
            s_scr[c, t] = _sigmoid(z).astype(BF16)
            w_scr[c, t] = w.astype(BF16)

        steps = _chain_sweep(base, n_sub, q_ref, k_ref, tri_s_ref, l_scr, keep)
        row = lax.broadcasted_iota(jnp.int32, (tb, tb), 0)
        col = lax.broadcasted_iota(jnp.int32, (tb, tb), 1)

        def back(it, carry):
            t = steps - 1 - it
            r0s = [pl.multiple_of(jnp.maximum(base + a - t, 0) * tb, tb) for a in range(n_sub)]
            qhs, dohs, khs, gws = [], [], [], []
            for a in range(n_sub):
                kb = k_ref[pl.ds(r0s[a], tb), :]
                vb = v_ref[pl.ds(r0s[a], tb), :]
                qa = q_ref[a * tb:(a + 1) * tb, :]
                doa = do_scr[a * tb:(a + 1) * tb, :]
                for h, mask in enumerate(masks):
                    qhs.append(jnp.where(mask, qa, jnp.zeros_like(qa)))
                    khs.append(jnp.where(mask, kb, jnp.zeros_like(kb)))
                    dohs.append(jnp.where(mask, doa, jnp.zeros_like(doa)))
                    gws.append(w_scr[2 * a + h, t].astype(F32) * _dot_nt(dohs[-1], vb))
            parts = [_split_hilo(gw) for gw in gws]
            tri = tri_p_ref[...]
            sums = [_dot(hi, tri) + _dot(lo, tri) for hi, lo in parts]
            dzs = []
            for c, (gw, sm) in enumerate(zip(gws, sums)):
                gsum = g_scr[c]
                dz = gw - (gw + sm[:, 0:tb] + gsum) * s_scr[c, t].astype(F32)
                dz = jnp.where(col < row + t * tb, dz, 0.0)
                g_scr[c] = gsum + sm[:, tb:2 * tb]
                dzs.append(dz.astype(BF16))
            for c, dzb in enumerate(dzs):
                a = c // 2
                dk_ref[pl.ds(r0s[a], tb), :] += _dot_tn(dzb, qhs[c])
                dv_ref[pl.ds(r0s[a], tb), :] += _dot_tn(w_scr[c, t], dohs[c])
                dq_ref[a * tb:(a + 1) * tb, :] += _dot(dzb, khs[c])
            return carry

        lax.fori_loop(0, steps, back, 0)

        if nsc:
            @pl.when(jnp.logical_and(pl.program_id(0) == n_hp - 1, i == s // tq - 1))
            def _():
                _scatter_finish(sc_ins, sc_outs, sc_sems)

    n_hp = ATTN_WIDTH // (2 * HEAD_DIM)
    hp_blk = lambda off: pl.BlockSpec((tq, 2 * HEAD_DIM), lambda hp, i: (i, off + hp))
    res = lambda off: pl.BlockSpec((s, 2 * HEAD_DIM), lambda hp, i: (0, off + hp))
    tri = pl.BlockSpec((tb, 2 * tb), lambda hp, i: (0, 0))
    outs = pl.pallas_call(
        body, name="attn_bwd_scatter" if nsc else "attn_bwd", grid=(n_hp, s // tq),
        in_specs=[hp_blk(0), res(4), res(8), hp_blk(0), hp_blk(20), hp_blk(4), tri, tri] + [ANY] * nsc,
        out_specs=[hp_blk(0), res(0), res(0), hp_blk(0)] + [ANY] * nsc,
        out_shape=[jax.ShapeDtypeStruct((s, ATTN_WIDTH), F32)] * 4 + [jax.ShapeDtypeStruct(a.shape, a.dtype) for a in scatter],
        scratch_shapes=[pltpu.VMEM((tq, 2 * HEAD_DIM), BF16), pltpu.VMEM((n_chain, tb, tb), F32),
                        pltpu.VMEM((n_chain, tb, tb), F32), pltpu.VMEM((n_chain, nq, tb, tb), BF16),
                        pltpu.VMEM((n_chain, nq, tb, tb), BF16)] + (_scatter_sems(nsc) if nsc else []),
        compiler_params=_cparams(2),
    )(qkv, qkv, qkv, o, proj, dmix, _tri("suffix_incl"), _tri("prefix_strict"), *scatter)
    return outs[0], outs[1], outs[2], outs[3], (_scatter_own(outs[4:], scatter) if nsc else [])


def _ssm_glu_bwd(dmix, y, proj, d, w_glu_l, b_glu):
    s = y.shape[0]
    tm = _row_tile(s, 512)

    def body(dys_ref, y_ref, u_ref, gs_ref, d_ref, wg_ref, bg_ref,
             dyf_ref, du_ref, dgs_ref, z_ref, dzz_ref, dd_ref, db_ref):
        @pl.when(pl.program_id(0) == 0)
        def _():
            dd_ref[...] = jnp.zeros_like(dd_ref)
            db_ref[...] = jnp.zeros_like(db_ref)

        u = u_ref[...]
        dv = d_ref[...]
        yf, z, val, gate = _glu_forward(y_ref[...], u, dv, wg_ref, bg_ref[...])
        gs = gs_ref[...]
        sgs = _sigmoid(gs)
        sgate = _sigmoid(gate)
        dys = dys_ref[...]
        dgv = dys * (gs * sgs)
        dgs_ref[...] = dys * (val * sgate) * (sgs * (1.0 + gs * (1.0 - sgs)))
        dzz = jnp.concatenate([dgv * sgate, dgv * val * sgate * (1.0 - sgate)], axis=-1)
        dzzb = dzz.astype(BF16)
        dz = _dot_nt(dzzb[:, 0:ROW_SHARD], wg_ref[0])
        for sh in range(1, N_CHIPS):
            dz = dz + _dot_nt(dzzb[:, ROW_SHARD * sh:ROW_SHARD * (sh + 1)], wg_ref[sh])
        dyf = dz * _gelu_grad(yf)
        dyf_ref[...] = dyf
        du_ref[...] = dyf * dv
        z_ref[...] = z.astype(BF16)
        dzz_ref[...] = dzzb
        dd_ref[...] += _colsum8(dyf * u)
        db_ref[...] += _colsum8(dzz)

    row = lambda i: (i, 0)
    half = pl.BlockSpec((tm, SSM_WIDTH), row)
    return pl.pallas_call(
        body, name="ssm_glu_bwd", grid=(s // tm,),
        in_specs=[half, half, half, pl.BlockSpec((tm, SSM_WIDTH), lambda i: (i, 1)),
                  pl.BlockSpec((1, SSM_WIDTH), lambda i: (0, 0)),
                  pl.BlockSpec((N_CHIPS, SSM_WIDTH, ROW_SHARD), lambda i: (0, 0, 0)),
                  pl.BlockSpec((1, 2 * SSM_WIDTH), lambda i: (0, 0))],
        out_specs=[half, half, half, half, pl.BlockSpec((tm, 2 * SSM_WIDTH), row),
                   pl.BlockSpec((SUBLANES, SSM_WIDTH), lambda i: (0, 0)),
                   pl.BlockSpec((SUBLANES, 2 * SSM_WIDTH), lambda i: (0, 0))],
        out_shape=[jax.ShapeDtypeStruct((s, SSM_WIDTH), F32)] * 3
        + [jax.ShapeDtypeStruct((s, SSM_WIDTH), BF16), jax.ShapeDtypeStruct((s, 2 * SSM_WIDTH), BF16),
           jax.ShapeDtypeStruct((SUBLANES, SSM_WIDTH), F32), jax.ShapeDtypeStruct((SUBLANES, 2 * SSM_WIDTH), F32)],
        compiler_params=_cparams(1),
    )(dmix, y, proj, proj, d, w_glu_l, b_glu)


def _ssm_scan_bwd(dyf, xs, proj, wct, tab_rev, wbt):
    s = dyf.shape[0]
    tm = _row_tile(s, 512)
    nt = s // tm
    length = tm // SUBLANES

    def body(dy_ref, xs_ref, u_ref, wct_ref, tab_ref, wbt_ref, du_ref, dwc_ref, dwb_ref, da_ref, lam_ref, carry_ref):
        @pl.when(pl.program_id(1) == 0)
        def _():
            carry_ref[...] = jnp.zeros_like(carry_ref)
            dwc_ref[...] = jnp.zeros_like(dwc_ref)
            dwb_ref[...] = jnp.zeros_like(dwb_ref)
            da_ref[...] = jnp.zeros_like(da_ref)

        dyp = _interleave_chunks(dy_ref[...]).astype(BF16)
        up = _interleave_chunks(u_ref[...]).astype(BF16)
        lam_ref[...] = _dot(dyp, wct_ref[...])

        def tail(r0, lr, li, carry):
            er, ei, dar, dai = carry
            xr = xs_ref[pl.ds(r0, SUBLANES), 0:CH_S]
            xi = xs_ref[pl.ds(r0, SUBLANES), CH_S:2 * CH_S]
            return lr, li, dar + (xr * er + xi * ei), dai + (xr * ei - xi * er)

        fix, (gr, gi) = _chunk_scan(lam_ref, tab_ref, carry_ref, length, reverse=True, tail=tail)
        zero = jnp.zeros((SUBLANES, CH_S), F32)
        _, _, dar, dai = lax.fori_loop(0, length, fix, (gr, gi, zero, zero), unroll=2)
        da_ref[:, 0:CH_S] += dar
        da_ref[:, CH_S:2 * CH_S] += dai
        lamb = lam_ref[...].astype(BF16)
        du_ref[...] = _time_order(_dot(lamb, wbt_ref[...]))
        dwc_ref[...] += _dot_tn(xs_ref[...].astype(BF16), dyp)
        dwb_ref[...] += _dot_tn(up, lamb)

    rev = lambda j, i: (nt - 1 - i, j)
    return pl.pallas_call(
        body, name="ssm_scan_bwd", grid=(SSM_CHUNKS, nt),
        in_specs=[pl.BlockSpec((tm, CH_W), rev),
                  pl.BlockSpec((None, tm, 2 * CH_S), lambda j, i: (j, nt - 1 - i, 0)),
                  pl.BlockSpec((tm, CH_W), rev),
                  pl.BlockSpec((None, CH_W, 2 * CH_S), lambda j, i: (j, 0, 0)),
                  pl.BlockSpec((None, length, 2 * CH_S), lambda j, i: (j, 0, 0)),
                  pl.BlockSpec((None, 2 * CH_S, CH_W), lambda j, i: (j, 0, 0))],
        out_specs=[pl.BlockSpec((tm, CH_W), rev),
                   pl.BlockSpec((None, 2 * CH_S, CH_W), lambda j, i: (j, 0, 0)),
                   pl.BlockSpec((None, CH_W, 2 * CH_S), lambda j, i: (j, 0, 0)),
                   pl.BlockSpec((None, SUBLANES, 2 * CH_S), lambda j, i: (j, 0, 0))],
        out_shape=[jax.ShapeDtypeStruct((s, SSM_WIDTH), F32),
                   jax.ShapeDtypeStruct((SSM_CHUNKS, 2 * CH_S, CH_W), F32),
                   jax.ShapeDtypeStruct((SSM_CHUNKS, CH_W, 2 * CH_S), F32),
                   jax.ShapeDtypeStruct((SSM_CHUNKS, SUBLANES, 2 * CH_S), F32)],
        scratch_shapes=[pltpu.VMEM((tm, 2 * CH_S), F32), pltpu.VMEM((SUBLANES, 2 * CH_S), F32)],
        compiler_params=_cparams(2),
    )(dyf, xs, proj, wct, tab_rev, wbt)


def _in_proj_bwd(h, g1, w_in_l, qg, kg, proj, du_a, du_b, dgs, dq, dk, dv, dga, dh1):
    s = h.shape[0]
    tm = _row_tile(s, 256)

    def body(h_ref, g_ref, w_ref, qg_ref, kg_ref, ones_ref, q_ref, k_ref, dua_ref, dub_ref, dgs_ref, dq_ref, dk_ref,
             dv_ref, dga_ref, dh1_ref, dh_ref, hn_ref, dp_ref, dg1_ref, dqg_ref, dkg_ref):
        @pl.when(pl.program_id(0) == 0)
        def _():
            dg1_ref[...] = jnp.zeros_like(dg1_ref)
            dqg_ref[...] = jnp.zeros_like(dqg_ref)
            dkg_ref[...] = jnp.zeros_like(dkg_ref)

        ones = ones_ref[...]

        def head_norm_bwd(x, gain, dy):
            r = lax.rsqrt(_dot_hilo(x * x, ones) + RMS_EPS)
            gdy = gain * dy
            dx = r * gdy - x * (r * r * r) * _dot_hilo(x * gdy, ones)
            return dx, x * r * dy

        dqr, dqg_rows = head_norm_bwd(q_ref[...], qg_ref[...], dq_ref[...] * ATTN_SCALE)
        dkr, dkg_rows = head_norm_bwd(k_ref[...], kg_ref[...], dk_ref[...])
        dqg_ref[...] += _colsum8(dqg_rows)
        dkg_ref[...] += _colsum8(dkg_rows)
        dp_ref[:, 0:512] = (dua_ref[...] + dub_ref[...]).astype(BF16)
        dp_ref[:, 512:1024] = dgs_ref[...].astype(BF16)
        dp_ref[:, 1024:1536] = dqr.astype(BF16)
        dp_ref[:, 1536:2048] = dkr.astype(BF16)
        dp_ref[:, 2048:2560] = dv_ref[...].astype(BF16)
        dp_ref[:, 2560:3072] = dga_ref[...].astype(BF16)
        dhn = _dot_nt(dp_ref[:, 0:IN_SHARD], w_ref[0])
        for sh in range(1, N_CHIPS):
            dhn = dhn + _dot_nt(dp_ref[:, IN_SHARD * sh:IN_SHARD * (sh + 1)], w_ref[sh])
        x = h_ref[...]
        gv = g_ref[...]
        r, hn = _rms_rows(x, gv)
        dx, dg_rows = _rms_bwd(x, r, gv, dhn)
        dh_ref[...] = dh1_ref[...] + dx
        hn_ref[...] = hn.astype(BF16)
        dg1_ref[...] += _colsum8(dg_rows)

    row = lambda i: (i, 0)
    full = lambda shape: pl.BlockSpec(shape, lambda i: (0,) * len(shape))
    big = pl.BlockSpec((tm, D_MODEL), row)
    half = pl.BlockSpec((tm, 512), row)
    return pl.pallas_call(
        body, name="in_proj_bwd", grid=(s // tm,),
        in_specs=[big, full((1, D_MODEL)), full((N_CHIPS, D_MODEL, IN_SHARD)),
                  full((1, ATTN_WIDTH)), full((1, ATTN_WIDTH)), full((ATTN_WIDTH, ATTN_WIDTH)),
                  pl.BlockSpec((tm, 512), lambda i: (i, 2)), pl.BlockSpec((tm, 512), lambda i: (i, 3)),
                  half, half, half, half, half, half, half, big],
        out_specs=[big, big, pl.BlockSpec((tm, IN_COLS), row), pl.BlockSpec((SUBLANES, D_MODEL), lambda i: (0, 0)),
                   pl.BlockSpec((SUBLANES, ATTN_WIDTH), lambda i: (0, 0)), pl.BlockSpec((SUBLANES, ATTN_WIDTH), lambda i: (0, 0))],
        out_shape=[jax.ShapeDtypeStruct((s, D_MODEL), F32), jax.ShapeDtypeStruct((s, D_MODEL), BF16),
                   jax.ShapeDtypeStruct((s, IN_COLS), BF16), jax.ShapeDtypeStruct((SUBLANES, D_MODEL), F32),
                   jax.ShapeDtypeStruct((SUBLANES, ATTN_WIDTH), F32), jax.ShapeDtypeStruct((SUBLANES, ATTN_WIDTH), F32)],
        compiler_params=_cparams(1),
    )(h, g1, w_in_l, qg, kg, _head_ones(), proj, proj, du_a, du_b, dgs, dq, dk, dv, dga, dh1)


SMALL_NAMES = ("mix_norm_g", "ssm_a_re", "ssm_a_im", "ssm_log_dt", "ssm_b_re", "ssm_b_im", "ssm_c_re", "ssm_c_im",
               "ssm_d", "ssm_b_glu", "q_norm_g", "k_norm_g", "ple_norm_g")
SMALL_4D = ("ssm_b_re", "ssm_b_im", "ssm_c_re", "ssm_c_im")
BIG_NAMES = ("w_in", "ssm_w_glu", "w_out", "w_ple_gate", "w_ple_proj")


def _ssm_setup(sm, layer, length):
    col = lambda a: a[layer].reshape(1, N_STATES)
    a_re, a_im = col(sm["ssm_a_re"]), col(sm["ssm_a_im"])
    log_dt = jnp.repeat(sm["ssm_log_dt"][layer], SSM_STATE).reshape(1, N_STATES)
    b_re = sm["ssm_b_re"][layer].reshape(N_STATES, SSM_GROUP).T
    b_im = sm["ssm_b_im"][layer].reshape(N_STATES, SSM_GROUP).T
    disc_in = (a_re, a_im, log_dt, b_re, b_im)
    ab_re, ab_im, bb_re, bb_im = _disc_fwd(*disc_in)
    wb = jnp.concatenate([_block_diag_in(bb_re), _block_diag_in(bb_im)], axis=-1)
    wc = jnp.concatenate([_block_diag_out(sm["ssm_c_re"][layer]), -_block_diag_out(sm["ssm_c_im"][layer])], axis=1)
    return dict(disc_in=disc_in, wb=wb.astype(BF16), wbt=wb.transpose(0, 2, 1).astype(BF16),
                wc=wc.astype(BF16), wct=wc.transpose(0, 2, 1).astype(BF16),
                tab=_scan_powers(ab_re, ab_im, length, False), tab_rev=_scan_powers(ab_re, ab_im, length, True))


def _whole_blocks(names, gathered):
    return {n: g.reshape(N_CHIPS, 2 * g.shape[2], g.shape[3]) for n, g in zip(names, gathered)}


def _local_step(x, p, target, sm, w_in0, local=None, gathered=None, layer1_hook=None):
    wg = [dict(w_in=w_in0), {}] if gathered is None else gathered
    tile8 = lambda a: jnp.tile(a, ATTN_WIDTH // HEAD_DIM).reshape(1, ATTN_WIDTH)
    saved = []
    h = x
    for l in range(N_LAYERS):
        ssm = _ssm_setup(sm, l, _row_tile(x.shape[0], 512) // SUBLANES)
        g1 = sm["mix_norm_g"][l].reshape(1, D_MODEL)
        g2 = sm["ple_norm_g"][l].reshape(1, D_MODEL)
        qg, kg = tile8(sm["q_norm_g"][l]), tile8(sm["k_norm_g"][l])
        dsk = sm["ssm_d"][l].reshape(1, SSM_WIDTH)
        bgl = sm["ssm_b_glu"][l].reshape(1, 2 * SSM_WIDTH)
        proj, qkv = _in_proj(h, g1, wg[l]["w_in"], qg, kg)
        if l == 0 and local is not None:
            rest = BIG_NAMES[1:]
            xs, y, got = _ssm_scan_fwd(proj, ssm["wb"], ssm["tab"], ssm["wc"], [local[n] for n in rest], [0] * len(rest))
            wg[0].update(_whole_blocks(rest, got))
            ys = _ssm_glu_fwd(y, proj, dsk, wg[0]["ssm_w_glu"], bgl)
            o, ya, got = _attn_fwd(qkv, proj, [local[n] for n in BIG_NAMES], [2] * len(BIG_NAMES))
            wg[1].update(_whole_blocks(BIG_NAMES, got))
        else:
            xs, y, _ = _ssm_scan_fwd(proj, ssm["wb"], ssm["tab"], ssm["wc"])
            ys = _ssm_glu_fwd(y, proj, dsk, wg[l]["ssm_w_glu"], bgl)
            o, ya, _ = _attn_fwd(qkv, proj)
        tail = (target,) if l == N_LAYERS - 1 else ()
        h1, h2, *sq = _out_ple(h, ys, ya, p[l], g2, wg[l]["w_out"], wg[l]["w_ple_gate"], wg[l]["w_ple_proj"], *tail)
        saved.append(dict(ssm=ssm, g1=g1, g2=g2, qg=qg, kg=kg, dsk=dsk, bgl=bgl, h=h, proj=proj, qkv=qkv, xs=xs, y=y,
                          ys=ys, o=o, ya=ya, h1=h1))
        h = h2
    dh = h
    loss = 0.5 * jnp.sum(sq[0]) / D_MODEL

    gbig = [{} for _ in range(N_LAYERS)]
    scattered = []
    gsm = {n: [None] * N_LAYERS for n in SMALL_NAMES}
    for l in reversed(range(N_LAYERS)):
        sv = saved[l]
        ssm = sv["ssm"]
        dh1, dmix, hn2b, dgpb, dppb, dh1b, dg2 = _out_ple_bwd(dh, sv["h1"], p[l], sv["g2"], wg[l]["w_out"],
                                                              wg[l]["w_ple_gate"], wg[l]["w_ple_proj"])
        gsm["ple_norm_g"][l] = dg2.sum(0)
        gbig[l]["w_ple_proj"] = _tn_matmul(p[l], dppb, N_CHIPS, False, "dw_ple_proj")
        gbig[l]["w_ple_gate"] = _tn_matmul(hn2b, dgpb, N_CHIPS, True, "dw_ple_gate")
        dwo = _tn_matmul(sv["ys"], dh1b, 2, True, "dw_out_ssm", None, 0, N_CHIPS)
        gbig[l]["w_out"] = _tn_matmul(sv["ya"], dh1b, 2, True, "dw_out_attn", dwo, 2, N_CHIPS)
        if l == 0 and layer1_hook is not None:
            dqs, dkn, dv, dga, scattered = _attn_bwd(sv["qkv"], sv["o"], sv["proj"], dmix, layer1_hook(gbig[1]))
        else:
            dqs, dkn, dv, dga, _ = _attn_bwd(sv["qkv"], sv["o"], sv["proj"], dmix)
        dyf, du_a, dgs, zb, dzzb, dd, dbg = _ssm_glu_bwd(dmix, sv["y"], sv["proj"], sv["dsk"], wg[l]["ssm_w_glu"], sv["bgl"])
        gsm["ssm_d"][l] = dd.sum(0).reshape(SSM_GROUPS, SSM_GROUP)
        gsm["ssm_b_glu"][l] = dbg.sum(0)
        gbig[l]["ssm_w_glu"] = _tn_matmul(zb, dzzb, N_CHIPS, False, "dw_glu")
        du_b, dwc, dwb, da = _ssm_scan_bwd(dyf, sv["xs"], sv["proj"], ssm["wct"], ssm["tab_rev"], ssm["wbt"])
        gsm["ssm_c_re"][l] = _block_diag_out_t(dwc[:, 0:CH_S, :])
        gsm["ssm_c_im"][l] = -_block_diag_out_t(dwc[:, CH_S:, :])
        da = da.sum(1)
        g_ab_re = da[:, 0:CH_S].reshape(1, N_STATES)
        g_ab_im = da[:, CH_S:].reshape(1, N_STATES)
        g_bb_re = _block_diag_in_t(dwb[:, :, 0:CH_S])
        g_bb_im = _block_diag_in_t(dwb[:, :, CH_S:])
        d_are, d_aim, d_ldt, d_bre, d_bim = _disc_bwd(*ssm["disc_in"], g_ab_re, g_ab_im, g_bb_re, g_bb_im)
        gsm["ssm_a_re"][l] = d_are.reshape(SSM_GROUPS, SSM_STATE)
        gsm["ssm_a_im"][l] = d_aim.reshape(SSM_GROUPS, SSM_STATE)
        gsm["ssm_log_dt"][l] = d_ldt.reshape(SSM_GROUPS, SSM_STATE).sum(1)
        gsm["ssm_b_re"][l] = d_bre.T.reshape(SSM_GROUPS, SSM_STATE, SSM_GROUP)
        gsm["ssm_b_im"][l] = d_bim.T.reshape(SSM_GROUPS, SSM_STATE, SSM_GROUP)
        dh, hnb, dprojb, dg1, dqg, dkg = _in_proj_bwd(sv["h"], sv["g1"], wg[l]["w_in"], sv["qg"], sv["kg"], sv["proj"],
                                                      du_a, du_b, dgs, dqs, dkn, dv, dga, dh1)
        gsm["mix_norm_g"][l] = dg1.sum(0)
        gsm["q_norm_g"][l] = dqg.sum(0).reshape(-1, HEAD_DIM).sum(0)
        gsm["k_norm_g"][l] = dkg.sum(0).reshape(-1, HEAD_DIM).sum(0)
        gbig[l]["w_in"] = _tn_matmul(hnb, dprojb, N_CHIPS, False, "dw_in")
    gsm = {n: jnp.stack(v, 0) for n, v in gsm.items()}
    return loss, dh, gbig, gsm, scattered


_SMALL_PAD = 8 * 8 * 128


def _pack_small(d, extra):
    flat = jnp.concatenate([d[n].reshape(-1) for n in SMALL_NAMES] + [jnp.stack(extra)])
    n = flat.shape[0]
    padded = -(-n // _SMALL_PAD) * _SMALL_PAD
    return jnp.pad(flat, (0, padded - n))


def _unpack_small(flat, like):
    out, off = {}, 0
    for n in SMALL_NAMES:
        size = like[n].size
        out[n] = flat[off:off + size].reshape(like[n].shape)
        off += size
    return out, flat[off:]


def _half_views(arrs):
    return [a.reshape(a.shape[0], 2, a.shape[1] // 2, a.shape[2]) for a in arrs]


def _chip_sums(views, out_dtypes, tag):
    recv = _sibling_push(views, "grad_push_" + tag)
    return [_add_my_half(v, r, dt, "grad_half_add") for v, r, dt in zip(views, recv, out_dtypes)]


def kernel(x, p, mix_norm_g, w_in, ssm_a_re, ssm_a_im, ssm_log_dt, ssm_b_re, ssm_b_im, ssm_c_re, ssm_c_im, ssm_d, ssm_w_glu, ssm_b_glu, q_norm_g, k_norm_g, w_out, ple_norm_g, w_ple_gate, w_ple_proj, loss_target, m_mix_norm_g, m_w_in, m_ssm_a_re, m_ssm_a_im, m_ssm_log_dt, m_ssm_b_re, m_ssm_b_im, m_ssm_c_re, m_ssm_c_im, m_ssm_d, m_ssm_w_glu, m_ssm_b_glu, m_q_norm_g, m_k_norm_g, m_w_out, m_ple_norm_g, m_w_ple_gate, m_w_ple_proj, v_mix_norm_g, v_w_in, v_ssm_a_re, v_ssm_a_im, v_ssm_log_dt, v_ssm_b_re, v_ssm_b_im, v_ssm_c_re, v_ssm_c_im, v_ssm_d, v_ssm_w_glu, v_ssm_b_glu, v_q_norm_g, v_k_norm_g, v_w_out, v_ple_norm_g, v_w_ple_gate, v_w_ple_proj):
    args = dict(locals())
    names = ("mix_norm_g", "w_in", "ssm_a_re", "ssm_a_im", "ssm_log_dt", "ssm_b_re", "ssm_b_im", "ssm_c_re", "ssm_c_im",
             "ssm_d", "ssm_w_glu", "ssm_b_glu", "q_norm_g", "k_norm_g", "w_out", "ple_norm_g", "w_ple_gate", "w_ple_proj")
    w = {n: args[n] for n in names}
    m = {n: args["m_" + n] for n in names}
    v = {n: args["v_" + n] for n in names}

    local = {n: w[n].astype(BF16).reshape(2 * N_LAYERS, w[n].shape[1] // 2, w[n].shape[2]) for n in BIG_NAMES}
    w_in0 = _chip_gather([local["w_in"]], "w_in_gather")[0].reshape(N_CHIPS, D_MODEL, IN_SHARD)
    sm = {n: w[n] for n in SMALL_NAMES}
    nb = len(BIG_NAMES)
    loss, dx, gbig, gsm, got1 = _local_step(
        x[0], p[:, 0], loss_target[0], sm, w_in0, local,
        layer1_hook=lambda g1: _chip_sums(_half_views([g1[n] for n in BIG_NAMES]), [BF16] * nb, "layer1"))

    small = _pack_small(gsm, [loss]).reshape(N_CHIPS, 2, SUBLANES, -1)
    chip0 = _chip_sums(_half_views([gbig[0][n] for n in BIG_NAMES]) + [small], [BF16] * nb + [F32], "layer0")
    got0 = _chip_scatter(chip0, "grad_chip_scatter")
    tot1 = [_sum4(a, "grad_chip_sum") for a in got1]
    tot0 = [_sum4(a, "grad_chip_sum") for a in got0]
    pieces = [(t, k, (l,)) for l, tots in enumerate((tot0[:nb], tot1)) for k, t in enumerate(tots)] + [(tot0[nb], nb, ())]
    joined = _sibling_join(pieces, [(N_LAYERS, 2) + t.shape for t in tot1] + [(2,) + tot0[nb].shape], "grad_sibling_join")
    small_all = _chip_gather([joined[nb]], "small_grad_gather")[0]
    small_tot = small_all.reshape(-1)
    g = {n: j.reshape(w[n].shape) for n, j in zip(BIG_NAMES, joined)}
    g_small, rest = _unpack_small(small_tot, sm)
    g.update(g_small)
    loss = rest[0]

    delta, new_m, new_v = {}, {}, {}
    for n in BIG_NAMES:
        lanes = w[n].shape[-1]
        outs = _adamw(_as_rows(w[n], lanes), _as_rows(g[n], lanes), _as_rows(m[n], lanes), _as_rows(v[n], lanes), "adamw_" + n)
        delta[n], new_m[n], new_v[n] = [o.reshape(w[n].shape) for o in outs]
    for group, per_layer in ((SMALL_4D, True), (tuple(n for n in SMALL_NAMES if n not in SMALL_4D), False)):
        outs = _adamw_many(*[[d[n] for n in group] for d in (w, g, m, v)], "adamw_small_4d" if per_layer else "adamw_small", per_layer)
        for d, o in zip((delta, new_m, new_v), outs):
            d.update(zip(group, o))

    return (loss, dx[None], *[g[n] for n in names], *[delta[n] for n in names],
            *[new_m[n] for n in names], *[new_v[n] for n in names])
```

```python
import functools
import math

import jax
import jax.numpy as jnp
from jax import lax
from jax.experimental import pallas as pl
from jax.experimental.pallas import tpu as pltpu

F32 = jnp.float32
BF16 = jnp.bfloat16

D_MODEL = 1024
N_LAYERS = 2
N_CHIPS = 4
IN_COLS = 3072
IN_SHARD = IN_COLS // N_CHIPS
SSM_WIDTH = 512
SSM_GROUP = 16
SSM_GROUPS = 32
SSM_STATE = 64
N_STATES = SSM_GROUPS * SSM_STATE
SSM_CHUNKS = 4
CH_W = SSM_WIDTH // SSM_CHUNKS
CH_S = N_STATES // SSM_CHUNKS
ATTN_WIDTH = 512
HEAD_DIM = 64
PLE_DIM = 256
ROW_SHARD = 256
RMS_EPS = 1e-6
ATTN_SCALE = HEAD_DIM ** -0.5
ATTN_BLOCK = 128
EXP_ZERO = -87.5
SUBLANES = 8
SCAN_TILE = 1024
V7X_VMEM_LIMIT = 52 * 1024 * 1024

ADAM_LR = 0.001
ADAM_B1 = 0.9
ADAM_B2 = 0.999
ADAM_EPS = 1e-08
ADAM_WD = 0.01
ADAM_STEP = 10

MESH = pl.DeviceIdType.MESH
ANY = pl.BlockSpec(memory_space=pl.ANY)


def _cparams(n_grid=0, parallel=0):
    sem = tuple(["parallel"] * parallel + ["arbitrary"] * (n_grid - parallel))
    return pltpu.CompilerParams(dimension_semantics=sem, vmem_limit_bytes=V7X_VMEM_LIMIT)


def _dot(a, b):
    return jnp.dot(a, b, preferred_element_type=F32)


def _dot_nt(a, b):
    return lax.dot_general(a, b, (((1,), (1,)), ((), ())), preferred_element_type=F32)


def _dot_tn(a, b):
    return lax.dot_general(a, b, (((0,), (0,)), ((), ())), preferred_element_type=F32)


def _split_hilo(a):
    hi = a.astype(BF16)
    lo = (a - hi.astype(F32)).astype(BF16)
    return hi, lo


def _dot_hilo(a, b):
    hi, lo = _split_hilo(a)
    return _dot(hi, b) + _dot(lo, b)


def _sigmoid(x):
    return 0.5 * (jnp.tanh(0.5 * x) + 1.0)


_GELU_C = math.sqrt(2.0 / math.pi)


def _gelu(x):
    return 0.5 * x * (1.0 + jnp.tanh(_GELU_C * (x + 0.044715 * (x * x * x))))


def _gelu_grad(x):
    t = jnp.tanh(_GELU_C * (x + 0.044715 * (x * x * x)))
    return 0.5 * (1.0 + t) + 0.5 * x * (1.0 - t * t) * (_GELU_C * (1.0 + 3.0 * 0.044715 * (x * x)))


def _row_tile(s, want):
    for t in range(min(s, want), 7, -1):
        if s % t == 0 and t % SUBLANES == 0:
            return t
    return s


def _coords():
    return lax.axis_index("x"), lax.axis_index("y"), lax.axis_index("c")


def _other_chips(x, y):
    return [(1 - x, y), (x, 1 - y), (1 - x, 1 - y)]


def _remote(src, dst, send_sem, recv_sem, dev):
    return pltpu.make_async_remote_copy(src_ref=src, dst_ref=dst, send_sem=send_sem, recv_sem=recv_sem,
                                        device_id=dev, device_id_type=MESH)


def _set_block(buf, block, index):
    return lax.dynamic_update_index_in_dim(buf, block, index, 0)


def _gather_sems(n):
    return [pltpu.SemaphoreType.DMA((3 * n,)) for _ in range(4)]


def _gather_copies(ins, bases, outs, sems):
    send_sems, recv_sems, fwd_send, fwd_recv = sems
    x, y, c = _coords()
    me_chip = 2 * x + y
    sibling = (x, y, 1 - c)
    first, landed, passed, from_sibling = [], [], [], []
    for k in range(len(ins)):
        for j, (cx, cy) in enumerate(_other_chips(x, y)):
            i = 3 * k + j
            first.append(_remote(ins[k].at[bases[k] + c], outs[k].at[me_chip, c], send_sems.at[i], recv_sems.at[i], (cx, cy, c)))
            blk = outs[k].at[2 * cx + cy, c]
            landed.append(_remote(blk, blk, send_sems.at[i], recv_sems.at[i], (cx, cy, c)))
            passed.append(_remote(blk, blk, fwd_send.at[i], fwd_recv.at[i], sibling))
            blk = outs[k].at[2 * cx + cy, 1 - c]
            from_sibling.append(_remote(blk, blk, fwd_send.at[i], fwd_recv.at[i], sibling))
    return first, landed, passed, from_sibling


def _gather_start(ins, bases, outs, sems):
    for cp in _gather_copies(ins, bases, outs, sems)[0]:
        cp.start()


def _gather_finish(ins, bases, outs, sems):
    first, landed, passed, from_sibling = _gather_copies(ins, bases, outs, sems)
    for arrived, forward in zip(landed, passed):
        arrived.wait_recv()
        forward.start()
    for cp in from_sibling:
        cp.wait_recv()
    for cp in first + passed:
        cp.wait_send()


def _gather_outputs(arrs):
    return [jax.ShapeDtypeStruct((N_CHIPS, 2) + a.shape[1:], a.dtype) for a in arrs]


def _gather_own(outs, arrs, bases):
    me_chip = 2 * lax.axis_index("x") + lax.axis_index("y")
    return [_set_block(o, lax.slice_in_dim(a, b, b + 2, axis=0), me_chip) for o, a, b in zip(outs, arrs, bases)]


def _chip_gather(arrs, name, bases=None):
    n = len(arrs)
    bases = [0] * n if bases is None else bases

    def body(*refs):
        ins, outs, sems = refs[:n], refs[n:2 * n], refs[2 * n:]
        _gather_start(ins, bases, outs, sems)
        _gather_finish(ins, bases, outs, sems)

    outs = pl.pallas_call(
        body, name=name, out_shape=_gather_outputs(arrs),
        in_specs=[ANY] * n, out_specs=[ANY] * n, scratch_shapes=_gather_sems(n),
    )(*arrs)
    return _gather_own(outs, arrs, bases)


def _sibling_push(arrs, name):
    n = len(arrs)

    def body(*refs):
        ins, outs = refs[:n], refs[n:2 * n]
        send_sems, recv_sems = refs[2 * n:]
        x, y, c = _coords()
        cps = [_remote(ins[k].at[pl.ds(0, N_CHIPS), 1 - c], outs[k], send_sems.at[k], recv_sems.at[k], (x, y, 1 - c))
               for k in range(n)]
        for cp in cps:
            cp.start()
        for cp in cps:
            cp.wait_recv()
        for cp in cps:
            cp.wait_send()

    return pl.pallas_call(
        body, name=name,
        out_shape=[jax.ShapeDtypeStruct((a.shape[0],) + a.shape[2:], a.dtype) for a in arrs],
        in_specs=[ANY] * n, out_specs=[ANY] * n,
        scratch_shapes=[pltpu.SemaphoreType.DMA((n,)), pltpu.SemaphoreType.DMA((n,))],
    )(*arrs)


def _sibling_join(pieces, out_shapes, name):
    n = len(pieces)
    no = len(out_shapes)

    def body(*refs):
        ins, outs = refs[:n], refs[n:n + no]
        send_sems, recv_sems = refs[n + no:]
        x, y, c = _coords()
        sibling = (x, y, 1 - c)
        cps = [_remote(ins[k], outs[o].at[lead + (c,)], send_sems.at[k], recv_sems.at[k], sibling)
               for k, (_, o, lead) in enumerate(pieces)]
        for cp in cps:
            cp.start()
        for k, (_, o, lead) in enumerate(pieces):
            blk = outs[o].at[lead + (1 - c,)]
            _remote(blk, blk, send_sems.at[k], recv_sems.at[k], sibling).wait_recv()
        for cp in cps:
            cp.wait_send()

    outs = pl.pallas_call(
        body, name=name,
        out_shape=[jax.ShapeDtypeStruct(sh, F32) for sh in out_shapes],
        in_specs=[ANY] * n, out_specs=[ANY] * no,
        scratch_shapes=[pltpu.SemaphoreType.DMA((n,)), pltpu.SemaphoreType.DMA((n,))],
    )(*[a for a, _, _ in pieces])
    outs = list(outs)
    c = lax.axis_index("c")
    for a, o, lead in pieces:
        block = a.reshape((1,) * (len(lead) + 1) + a.shape)
        outs[o] = lax.dynamic_update_slice(outs[o], block, lead + (c,) + (0,) * a.ndim)
    return outs


def _scatter_sems(n):
    return [pltpu.SemaphoreType.DMA((3 * n,)), pltpu.SemaphoreType.DMA((3 * n,))]


def _scatter_copies(ins, outs, sems):
    send_sems, recv_sems = sems
    x, y, c = _coords()
    me_chip = 2 * x + y
    sends, arrivals = [], []
    for k in range(len(ins)):
        for j, (cx, cy) in enumerate(_other_chips(x, y)):
            i = 3 * k + j
            sends.append(_remote(ins[k].at[2 * cx + cy], outs[k].at[me_chip], send_sems.at[i], recv_sems.at[i], (cx, cy, c)))
            blk = outs[k].at[2 * cx + cy]
            arrivals.append(_remote(blk, blk, send_sems.at[i], recv_sems.at[i], (cx, cy, c)))
    return sends, arrivals


def _scatter_start(ins, outs, sems):
    for cp in _scatter_copies(ins, outs, sems)[0]:
        cp.start()


def _scatter_finish(ins, outs, sems):
    sends, arrivals = _scatter_copies(ins, outs, sems)
    for cp in arrivals:
        cp.wait_recv()
    for cp in sends:
        cp.wait_send()


def _scatter_own(outs, arrs):
    me_chip = 2 * lax.axis_index("x") + lax.axis_index("y")
    return [_set_block(o, lax.dynamic_index_in_dim(a, me_chip, 0, keepdims=False), me_chip) for o, a in zip(outs, arrs)]


def _chip_scatter(arrs, name):
    n = len(arrs)

    def body(*refs):
        ins, outs, sems = refs[:n], refs[n:2 * n], refs[2 * n:]
        _scatter_start(ins, outs, sems)
        _scatter_finish(ins, outs, sems)

    outs = pl.pallas_call(
        body, name=name,
        out_shape=[jax.ShapeDtypeStruct(a.shape, a.dtype) for a in arrs],
        in_specs=[ANY] * n, out_specs=[ANY] * n, scratch_shapes=_scatter_sems(n),
    )(*arrs)
    return _scatter_own(outs, arrs)


def _as_rows(a, lanes):
    return a.reshape(-1, lanes)


def _add_my_half(v, recv, out_dtype, name):
    n_sh, _, h, cdim = v.shape
    tr = _row_tile(h, 512)

    def body(c_ref, a_ref, b_ref, o_ref):
        o_ref[...] = (a_ref[...].astype(F32) + b_ref[...].astype(F32)).astype(out_dtype)

    c = lax.axis_index("c").astype(jnp.int32).reshape(1)
    return pl.pallas_call(
        body, name=name,
        grid_spec=pltpu.PrefetchScalarGridSpec(
            num_scalar_prefetch=1, grid=(n_sh, h // tr),
            in_specs=[pl.BlockSpec((None, None, tr, cdim), lambda sh, i, c_ref: (sh, c_ref[0], i, 0)),
                      pl.BlockSpec((None, tr, cdim), lambda sh, i, c_ref: (sh, i, 0))],
            out_specs=pl.BlockSpec((None, tr, cdim), lambda sh, i, c_ref: (sh, i, 0))),
        out_shape=jax.ShapeDtypeStruct((n_sh, h, cdim), out_dtype),
        compiler_params=_cparams(2),
    )(c, v, recv)


def _sum4(parts, name):
    _, r, cdim = parts.shape
    tr = _row_tile(r, 512)

    def body(p_ref, o_ref):
        acc = p_ref[0].astype(F32) + p_ref[1].astype(F32)
        acc = acc + p_ref[2].astype(F32)
        o_ref[...] = acc + p_ref[3].astype(F32)

    return pl.pallas_call(
        body, name=name, grid=(r // tr,),
        in_specs=[pl.BlockSpec((N_CHIPS, tr, cdim), lambda i: (0, i, 0))],
        out_specs=pl.BlockSpec((tr, cdim), lambda i: (i, 0)),
        out_shape=jax.ShapeDtypeStruct((r, cdim), F32),
        compiler_params=_cparams(1),
    )(parts)


def _adamw_math(w, g, m, v):
    c1 = 1.0 - ADAM_B1 ** ADAM_STEP
    c2 = 1.0 - ADAM_B2 ** ADAM_STEP
    nm = ADAM_B1 * m + (1.0 - ADAM_B1) * g
    nv = ADAM_B2 * v + (1.0 - ADAM_B2) * (g * g)
    delta = -ADAM_LR * ((nm / c1) / (jnp.sqrt(nv / c2) + ADAM_EPS) + ADAM_WD * w)
    return delta, nm, nv


def _adamw(w, g, m, v, name):
    r, cdim = w.shape
    tr = _row_tile(r, 256)

    def body(w_ref, g_ref, m_ref, v_ref, d_ref, nm_ref, nv_ref):
        d_ref[...], nm_ref[...], nv_ref[...] = _adamw_math(w_ref[...], g_ref[...], m_ref[...], v_ref[...])

    spec = pl.BlockSpec((tr, cdim), lambda i: (i, 0))
    return pl.pallas_call(
        body, name=name, grid=(r // tr,),
        in_specs=[spec] * 4, out_specs=[spec] * 3,
        out_shape=[jax.ShapeDtypeStruct((r, cdim), F32)] * 3,
        compiler_params=_cparams(1),
    )(w, g, m, v)


def _adamw_many(ws, gs, ms, vs, name, per_layer):
    n = len(ws)

    def body(*refs):
        for k in range(n):
            w, g, m, v = (refs[j * n + k][...] for j in range(4))
            outs = _adamw_math(w, g, m, v)
            for j in range(3):
                refs[(4 + j) * n + k][...] = outs[j]

    shapes = [jax.ShapeDtypeStruct(w.shape, F32) for w in ws]
    if per_layer:
        specs = [pl.BlockSpec((None,) + w.shape[1:], lambda l, nd=w.ndim: (l,) + (0,) * (nd - 1)) for w in ws]
        call = pl.pallas_call(body, name=name, grid=(N_LAYERS,), in_specs=specs * 4, out_specs=specs * 3,
                              out_shape=shapes * 3, compiler_params=_cparams(1))
    else:
        call = pl.pallas_call(body, name=name, out_shape=shapes * 3, compiler_params=_cparams())
    outs = call(*ws, *gs, *ms, *vs)
    return outs[0:n], outs[n:2 * n], outs[2 * n:3 * n]


def _discretise(a_re, a_im, log_dt, b_re, b_im):
    dt = jnp.exp(log_dt)
    mag = jnp.exp(a_re * dt)
    ab_re = mag * jnp.cos(a_im * dt)
    ab_im = mag * jnp.sin(a_im * dt)
    num_re = ab_re - 1.0
    num_im = ab_im
    den = a_re * a_re + a_im * a_im
    f_re = (num_re * a_re + num_im * a_im) / den
    f_im = (num_im * a_re - num_re * a_im) / den
    bb_re = f_re * b_re - f_im * b_im
    bb_im = f_re * b_im + f_im * b_re
    return ab_re, ab_im, bb_re, bb_im


def _disc_shapes():
    col = jax.ShapeDtypeStruct((1, N_STATES), F32)
    mat = jax.ShapeDtypeStruct((SSM_GROUP, N_STATES), F32)
    return col, mat


def _disc_fwd(a_re, a_im, log_dt, b_re, b_im):
    col, mat = _disc_shapes()

    def body(ar, ai, ld, br, bi, o0, o1, o2, o3):
        outs = _discretise(ar[...], ai[...], ld[...], br[...], bi[...])
        for o, val in zip((o0, o1, o2, o3), outs):
            o[...] = val

    return pl.pallas_call(body, name="ssm_discretise", out_shape=[col, col, mat, mat],
                          compiler_params=_cparams())(a_re, a_im, log_dt, b_re, b_im)


def _disc_bwd(a_re, a_im, log_dt, b_re, b_im, g_ab_re, g_ab_im, g_bb_re, g_bb_im):
    col, mat = _disc_shapes()

    def body(ar, ai, ld, br, bi, g0, g1, g2, g3, o0, o1, o2, o3, o4):
        _, vjp = jax.vjp(_discretise, ar[...], ai[...], ld[...], br[...], bi[...])
        grads = vjp((g0[...], g1[...], g2[...], g3[...]))
        for o, val in zip((o0, o1, o2, o3, o4), grads):
            o[...] = val

    return pl.pallas_call(body, name="ssm_discretise_bwd", out_shape=[col, col, col, mat, mat],
                          compiler_params=_cparams())(a_re, a_im, log_dt, b_re, b_im, g_ab_re, g_ab_im, g_bb_re, g_bb_im)


def _cmul(ar, ai, br, bi):
    return ar * br - ai * bi, ar * bi + ai * br


def _scan_powers(ab_re, ab_im, length, reverse):
    br = ab_re.reshape(1, N_STATES)
    bi = -ab_im.reshape(1, N_STATES) if reverse else ab_im.reshape(1, N_STATES)
    ks = (length - jnp.arange(length)) if reverse else (jnp.arange(length) + 1)
    pr = jnp.ones((length, N_STATES), F32)
    pi = jnp.zeros((length, N_STATES), F32)
    for bit in range(length.bit_length()):
        take = ((ks >> bit) & 1)[:, None] == 1
        mr, mi = _cmul(pr, pi, br, bi)
        pr, pi = jnp.where(take, mr, pr), jnp.where(take, mi, pi)
        br, bi = _cmul(br, bi, br, bi)
    split = lambda t: t.reshape(length, SSM_CHUNKS, CH_S).transpose(1, 0, 2)
    return jnp.concatenate([split(pr), split(pi)], axis=-1)


def _interleave_chunks(v):
    rows, width = v.shape
    return pltpu.einshape("cjw->jcw", v.reshape(SUBLANES, rows // SUBLANES, width)).reshape(rows, width)


def _time_order(v):
    rows, width = v.shape
    return pltpu.einshape("jcw->cjw", v.reshape(rows // SUBLANES, SUBLANES, width)).reshape(rows, width)


def _block_diag_in(bb):
    t = bb.reshape(SSM_GROUP, SSM_CHUNKS, 8, SSM_STATE)
    eye = jnp.eye(8, dtype=bb.dtype)
    return jnp.einsum("hjgp,gk->jghkp", t, eye).reshape(SSM_CHUNKS, CH_W, CH_S)


def _block_diag_in_t(d):
    t = d.reshape(SSM_CHUNKS, 8, SSM_GROUP, 8, SSM_STATE)
    return jnp.einsum("jghgp->hjgp", t).reshape(SSM_GROUP, N_STATES)


def _block_diag_out(c):
    t = c.reshape(SSM_CHUNKS, 8, SSM_GROUP, SSM_STATE)
    eye = jnp.eye(8, dtype=c.dtype)
    return jnp.einsum("jghp,gk->jgpkh", t, eye).reshape(SSM_CHUNKS, CH_S, CH_W)


def _block_diag_out_t(d):
    t = d.reshape(SSM_CHUNKS, 8, SSM_STATE, 8, SSM_GROUP)
    return jnp.einsum("jgpgh->jghp", t).reshape(SSM_GROUPS, SSM_GROUP, SSM_STATE)


def _head_ones():
    r = jnp.arange(ATTN_WIDTH) // HEAD_DIM
    return jnp.where(r[:, None] == r[None, :], 1.0 / HEAD_DIM, 0.0).astype(BF16)


def _in_proj(h, g1, w_in_l, qg, kg):
    s = h.shape[0]
    tm = _row_tile(s, 512)

    def body(h_ref, g_ref, w_ref, qg_ref, kg_ref, ones_ref, proj_ref, qkv_ref):
        x = h_ref[...]
        r = lax.rsqrt(jnp.mean(x * x, axis=-1, keepdims=True) + RMS_EPS)
        hn = (x * r * g_ref[...]).astype(BF16)
        for sh in range(N_CHIPS):
            proj_ref[:, IN_SHARD * sh:IN_SHARD * (sh + 1)] = _dot(hn, w_ref[sh])
        ones = ones_ref[...]
        q = proj_ref[:, 1024:1536]
        k = proj_ref[:, 1536:2048]
        rq = lax.rsqrt(_dot_hilo(q * q, ones) + RMS_EPS)
        rk = lax.rsqrt(_dot_hilo(k * k, ones) + RMS_EPS)
        qkv_ref[:, 0:512] = (q * rq * qg_ref[...] * ATTN_SCALE).astype(BF16)
        qkv_ref[:, 512:1024] = (k * rk * kg_ref[...]).astype(BF16)
        qkv_ref[:, 1024:1536] = proj_ref[:, 2048:2560].astype(BF16)

    full = lambda shape: pl.BlockSpec(shape, lambda i: (0,) * len(shape))
    return pl.pallas_call(
        body, name="in_proj", grid=(s // tm,),
        in_specs=[pl.BlockSpec((tm, D_MODEL), lambda i: (i, 0)), full((1, D_MODEL)),
                  full((N_CHIPS, D_MODEL, IN_SHARD)),
                  full((1, ATTN_WIDTH)), full((1, ATTN_WIDTH)), full((ATTN_WIDTH, ATTN_WIDTH))],
        out_specs=[pl.BlockSpec((tm, IN_COLS), lambda i: (i, 0)), pl.BlockSpec((tm, 3 * ATTN_WIDTH), lambda i: (i, 0))],
        out_shape=[jax.ShapeDtypeStruct((s, IN_COLS), F32), jax.ShapeDtypeStruct((s, 3 * ATTN_WIDTH), BF16)],
        compiler_params=_cparams(1),
    )(h, g1, w_in_l, qg, kg, _head_ones())


def _row_bcast(ref, k, lo):
    return jnp.broadcast_to(ref[pl.ds(k, 1), lo:lo + CH_S], (SUBLANES, CH_S))


def _chunk_scan(x_ref, tab_ref, carry_ref, length, reverse, tail=None):
    row = lax.broadcasted_iota(jnp.int32, (SUBLANES, CH_S), 0)
    one, full = (length - 1, 0) if reverse else (0, length - 1)
    ar, ai = _row_bcast(tab_ref, one, 0), _row_bcast(tab_ref, one, CH_S)
    fr, fi = _row_bcast(tab_ref, full, 0), _row_bcast(tab_ref, full, CH_S)
    step = lambda jj: (length - 1 - jj) if reverse else jj

    def local(jj, carry):
        cr, ci = carry
        r0 = pl.multiple_of(step(jj) * SUBLANES, SUBLANES)
        xr = x_ref[pl.ds(r0, SUBLANES), 0:CH_S] + (ar * cr - ai * ci)
        xi = x_ref[pl.ds(r0, SUBLANES), CH_S:2 * CH_S] + (ar * ci + ai * cr)
        x_ref[pl.ds(r0, SUBLANES), 0:CH_S] = xr
        x_ref[pl.ds(r0, SUBLANES), CH_S:2 * CH_S] = xi
        return xr, xi

    zero = jnp.zeros((SUBLANES, CH_S), F32)
    er, ei = lax.fori_loop(0, length, local, (zero, zero))

    first, shift = (SUBLANES - 1, SUBLANES - 1) if reverse else (0, 1)
    hr = jnp.where(row == first, carry_ref[:, 0:CH_S], 0.0)
    hi = jnp.where(row == first, carry_ref[:, CH_S:2 * CH_S], 0.0)
    sr, si = pltpu.roll(er, shift, 0), pltpu.roll(ei, shift, 0)
    for k in range(1, SUBLANES):
        tr, ti = pltpu.roll(hr, shift, 0), pltpu.roll(hi, shift, 0)
        here = row == ((SUBLANES - 1 - k) if reverse else k)
        hr, hi = (jnp.where(here, fr * tr - fi * ti + sr, hr), jnp.where(here, fr * ti + fi * tr + si, hi))
    last = 0 if reverse else SUBLANES - 1
    outr, outi = fr * hr - fi * hi + er, fr * hi + fi * hr + ei
    carry_ref[:, 0:CH_S] = jnp.broadcast_to(outr[last:last + 1, :], (SUBLANES, CH_S))
    carry_ref[:, CH_S:2 * CH_S] = jnp.broadcast_to(outi[last:last + 1, :], (SUBLANES, CH_S))

    def fix(jj, carry):
        j = step(jj)
        r0 = pl.multiple_of(j * SUBLANES, SUBLANES)
        pr, pi = _row_bcast(tab_ref, j, 0), _row_bcast(tab_ref, j, CH_S)
        xr = x_ref[pl.ds(r0, SUBLANES), 0:CH_S] + (pr * hr - pi * hi)
        xi = x_ref[pl.ds(r0, SUBLANES), CH_S:2 * CH_S] + (pr * hi + pi * hr)
        x_ref[pl.ds(r0, SUBLANES), 0:CH_S] = xr
        x_ref[pl.ds(r0, SUBLANES), CH_S:2 * CH_S] = xi
        if tail is None:
            return carry
        return tail(r0, xr, xi, carry)

    return fix, (hr, hi)


def _ssm_scan_fwd(proj, wb, tab, wc, gather=None, gather_bases=None):
    s = proj.shape[0]
    tm = _row_tile(s, SCAN_TILE)
    nt = s // tm
    length = tm // SUBLANES
    gather = [] if gather is None else gather
    ng = len(gather)

    def body(*refs):
        u_ref, wb_ref, tab_ref, wc_ref = refs[0:4]
        g_ins = refs[4:4 + ng]
        xs_ref, y_ref = refs[4 + ng:6 + ng]
        g_outs = refs[6 + ng:6 + 2 * ng]
        carry_ref = refs[6 + 2 * ng]
        sems = refs[7 + 2 * ng:]
        j, i = pl.program_id(0), pl.program_id(1)

        @pl.when(i == 0)
        def _():
            carry_ref[...] = jnp.zeros_like(carry_ref)

        if ng:
            @pl.when(jnp.logical_and(j == 0, i == 0))
            def _():
                _gather_start(g_ins, gather_bases, g_outs, sems)

        xs_ref[...] = _dot(_interleave_chunks(u_ref[...]).astype(BF16), wb_ref[...])
        fix, start = _chunk_scan(xs_ref, tab_ref, carry_ref, length, reverse=False)
        lax.fori_loop(0, length, fix, start, unroll=2)
        y_ref[...] = _time_order(_dot(xs_ref[...].astype(BF16), wc_ref[...]))

        if ng:
            @pl.when(jnp.logical_and(j == SSM_CHUNKS - 1, i == nt - 1))
            def _():
                _gather_finish(g_ins, gather_bases, g_outs, sems)

    outs = pl.pallas_call(
        body, name="ssm_scan_gather" if ng else "ssm_scan", grid=(SSM_CHUNKS, nt),
        in_specs=[pl.BlockSpec((tm, CH_W), lambda j, i: (i, j)),
                  pl.BlockSpec((None, CH_W, 2 * CH_S), lambda j, i: (j, 0, 0)),
                  pl.BlockSpec((None, length, 2 * CH_S), lambda j, i: (j, 0, 0)),
                  pl.BlockSpec((None, 2 * CH_S, CH_W), lambda j, i: (j, 0, 0))] + [ANY] * ng,
        out_specs=[pl.BlockSpec((None, tm, 2 * CH_S), lambda j, i: (j, i, 0)),
                   pl.BlockSpec((tm, CH_W), lambda j, i: (i, j))] + [ANY] * ng,
        out_shape=[jax.ShapeDtypeStruct((SSM_CHUNKS, s, 2 * CH_S), F32), jax.ShapeDtypeStruct((s, SSM_WIDTH), F32)]
        + _gather_outputs(gather),
        scratch_shapes=[pltpu.VMEM((SUBLANES, 2 * CH_S), F32)] + (_gather_sems(ng) if ng else []),
        compiler_params=_cparams(2),
    )(proj, wb, tab, wc, *gather)
    return outs[0], outs[1], (_gather_own(outs[2:], gather, gather_bases) if ng else [])


def _glu_forward(y, u, d, wg_ref, bg):
    yf = y + d * u
    z = _gelu(yf)
    zb = z.astype(BF16)
    zz = jnp.concatenate([_dot(zb, wg_ref[sh]) for sh in range(N_CHIPS)], axis=-1) + bg
    return yf, z, zz[:, 0:SSM_WIDTH], zz[:, SSM_WIDTH:2 * SSM_WIDTH]


def _ssm_glu_fwd(y, proj, d, w_glu_l, b_glu):
    s = y.shape[0]
    tm = _row_tile(s, 512)

    def body(y_ref, u_ref, gs_ref, d_ref, wg_ref, bg_ref, o_ref):
        _, _, val, gate = _glu_forward(y_ref[...], u_ref[...], d_ref[...], wg_ref, bg_ref[...])
        gs = gs_ref[...]
        o_ref[...] = val * _sigmoid(gate) * (gs * _sigmoid(gs))

    row = lambda i: (i, 0)
    return pl.pallas_call(
        body, name="ssm_glu", grid=(s // tm,),
        in_specs=[pl.BlockSpec((tm, SSM_WIDTH), row), pl.BlockSpec((tm, SSM_WIDTH), row),
                  pl.BlockSpec((tm, SSM_WIDTH), lambda i: (i, 1)), pl.BlockSpec((1, SSM_WIDTH), lambda i: (0, 0)),
                  pl.BlockSpec((N_CHIPS, SSM_WIDTH, ROW_SHARD), lambda i: (0, 0, 0)),
                  pl.BlockSpec((1, 2 * SSM_WIDTH), lambda i: (0, 0))],
        out_specs=pl.BlockSpec((tm, SSM_WIDTH), row),
        out_shape=jax.ShapeDtypeStruct((s, SSM_WIDTH), F32),
        compiler_params=_cparams(1),
    )(y, proj, proj, d, w_glu_l, b_glu)


def _tri(kind):
    r = jnp.arange(ATTN_BLOCK)
    if kind == "suffix_incl":
        m = r[:, None] >= r[None, :]
    else:
        m = r[:, None] < r[None, :]
    return jnp.concatenate([m, jnp.ones_like(m)], axis=1).astype(BF16)


def _head_masks():
    lane = lax.broadcasted_iota(jnp.int32, (1, 2 * HEAD_DIM), 1)
    return [lane < HEAD_DIM, lane >= HEAD_DIM]


def _chain_step(t, base, n_sub, first, q_ref, k_ref, tri_ref, l_scr, per_chain):
    tb = ATTN_BLOCK
    row = lax.broadcasted_iota(jnp.int32, (tb, tb), 0)
    col = lax.broadcasted_iota(jnp.int32, (tb, tb), 1)
    masks = _head_masks()
    blks = [base + a - t for a in range(n_sub)]
    r0s = [pl.multiple_of(jnp.maximum(blk, 0) * tb, tb) for blk in blks]
    zs = []
    for a in range(n_sub):
        kb = k_ref[pl.ds(r0s[a], tb), :]
        qa = q_ref[a * tb:(a + 1) * tb, :]
        for mask in masks:
            zs.append(_dot_nt(jnp.where(mask, qa, jnp.zeros_like(qa)), kb))
    parts = []
    for z in zs:
        ls = jnp.minimum(-z, 0.0) - jnp.log(1.0 + jnp.exp(-jnp.abs(z)))
        if first:
            ls = jnp.where(col < row, ls, 0.0)
        parts.append(_split_hilo(ls))
    tri = tri_ref[...]
    sums = [_dot(hi, tri) + _dot(lo, tri) for hi, lo in parts]
    top = None
    ws = []
    for c, (z, sm) in enumerate(zip(zs, sums)):
        if first:
            lsum = jnp.zeros((tb, tb), F32)
        else:
            lsum = l_scr[c] + jnp.where(blks[c // 2] >= 0, 0.0, -1e30)
        w = jnp.exp(z + sm[:, 0:tb] + lsum)
        if first:
            w = jnp.where(col < row, w, 0.0)
        ws.append(w)
        lsum = lsum + sm[:, tb:2 * tb]
        l_scr[c] = lsum
        top = lsum if top is None else jnp.maximum(top, lsum)
    for c, (z, w) in enumerate(zip(zs, ws)):
        per_chain(c // 2, c % 2, c, r0s[c // 2], z, w)
    return jnp.max(top)


def _chain_sweep(base, n_sub, q_ref, k_ref, tri_ref, l_scr, per_chain):
    top = _chain_step(0, base, n_sub, True, q_ref, k_ref, tri_ref, l_scr, functools.partial(per_chain, 0))

    def cond(carry):
        t, top = carry
        return jnp.logical_and(t <= base + n_sub - 1, top > EXP_ZERO)

    def step(carry):
        t, _ = carry
        return t + 1, _chain_step(t, base, n_sub, False, q_ref, k_ref, tri_ref, l_scr, functools.partial(per_chain, t))

    steps, _ = lax.while_loop(cond, step, (jnp.int32(1), top))
    return steps


ATTN_SUB_FWD = 8
ATTN_SUB_BWD = 4


def _attn_fwd(qkv, proj, gather=None, gather_bases=None):
    s = qkv.shape[0]
    tb = ATTN_BLOCK
    n_sub = min(ATTN_SUB_FWD, s // tb)
    tq = n_sub * tb
    n_hp = ATTN_WIDTH // (2 * HEAD_DIM)
    gather = [] if gather is None else gather
    ng = len(gather)

    def body(*refs):
        q_ref, k_ref, v_ref, g_ref, tri_ref = refs[0:5]
        g_ins = refs[5:5 + ng]
        o_ref, ya_ref = refs[5 + ng:7 + ng]
        g_outs = refs[7 + ng:7 + 2 * ng]
        l_scr = refs[7 + 2 * ng]
        sems = refs[8 + 2 * ng:]
        i = pl.program_id(1)
        masks = _head_masks()
        o_ref[...] = jnp.zeros_like(o_ref)

        if ng:
            @pl.when(jnp.logical_and(pl.program_id(0) == 0, i == 0))
            def _():
                _gather_start(g_ins, gather_bases, g_outs, sems)

        def per_chain(t, a, h, c, r0, z, w):
            vb = v_ref[pl.ds(r0, tb), :]
            vb = jnp.where(masks[h], vb, jnp.zeros_like(vb))
            o_ref[a * tb:(a + 1) * tb, :] += _dot(w.astype(BF16), vb)

        _chain_sweep(i * n_sub, n_sub, q_ref, k_ref, tri_ref, l_scr, per_chain)
        g = g_ref[...]
        ya_ref[...] = o_ref[...] * (g * _sigmoid(g))

        if ng:
            @pl.when(jnp.logical_and(pl.program_id(0) == n_hp - 1, i == s // tq - 1))
            def _():
                _gather_finish(g_ins, gather_bases, g_outs, sems)

    hp_blk = lambda off: pl.BlockSpec((tq, 2 * HEAD_DIM), lambda hp, i: (i, off + hp))
    res = lambda off: pl.BlockSpec((s, 2 * HEAD_DIM), lambda hp, i: (0, off + hp))
    outs = pl.pallas_call(
        body, name="attn_fwd_gather" if ng else "attn_fwd", grid=(n_hp, s // tq),
        in_specs=[hp_blk(0), res(4), res(8), hp_blk(20), pl.BlockSpec((tb, 2 * tb), lambda hp, i: (0, 0))] + [ANY] * ng,
        out_specs=[hp_blk(0), hp_blk(0)] + [ANY] * ng,
        out_shape=[jax.ShapeDtypeStruct((s, ATTN_WIDTH), F32)] * 2 + _gather_outputs(gather),
        scratch_shapes=[pltpu.VMEM((2 * n_sub, tb, tb), F32)] + (_gather_sems(ng) if ng else []),
        compiler_params=_cparams(2),
    )(qkv, qkv, qkv, proj, _tri("suffix_incl"), *gather)
    return outs[0], outs[1], (_gather_own(outs[2:], gather, gather_bases) if ng else [])


def _rms_rows(x, g):
    r = lax.rsqrt(jnp.mean(x * x, axis=-1, keepdims=True) + RMS_EPS)
    return r, x * r * g


def _ple_forward(h1, p, g2, wpg_ref, wpp_ref):
    r2, hn2 = _rms_rows(h1, g2)
    hb = hn2.astype(BF16)
    gpre = _dot(hb[:, 0:ROW_SHARD], wpg_ref[0])
    for sh in range(1, N_CHIPS):
        gpre = gpre + _dot(hb[:, ROW_SHARD * sh:ROW_SHARD * (sh + 1)], wpg_ref[sh])
    gate = _sigmoid(gpre)
    pb = p.astype(BF16)
    pp = jnp.concatenate([_dot(pb, wpp_ref[sh]) for sh in range(N_CHIPS)], axis=-1)
    return r2, hb, gate, pp


def _colsum8(a):
    t = a.shape[0]
    return a.reshape(t // SUBLANES, SUBLANES, a.shape[1]).sum(axis=0)


def _sq_err_grad(y, target):
    e = y - target
    sq = _colsum8(e * e)
    part = sq[:, 0:128]
    for b in range(1, D_MODEL // 128):
        part = part + sq[:, 128 * b:128 * (b + 1)]
    return e / D_MODEL, part


def _out_ple(h, ys, ya, p, g2, w_out_l, w_pg_l, w_pp_l, target=None):
    s = h.shape[0]
    tm = _row_tile(s, 512)
    last = target is not None

    def body(*refs):
        h_ref, ys_ref, ya_ref, p_ref, g_ref, wo_ref, wpg_ref, wpp_ref = refs[0:8]
        h1_ref, h2_ref = refs[8 + last], refs[9 + last]
        ysb = ys_ref[...].astype(BF16)
        yab = ya_ref[...].astype(BF16)
        h1 = h_ref[...]
        for sh, src in enumerate((ysb[:, 0:ROW_SHARD], ysb[:, ROW_SHARD:], yab[:, 0:ROW_SHARD], yab[:, ROW_SHARD:])):
            h1 = h1 + _dot(src, wo_ref[sh])
        _, _, gate, pp = _ple_forward(h1, p_ref[...], g_ref[...], wpg_ref, wpp_ref)
        h1_ref[...] = h1
        h2 = h1 + gate * pp
        if last:
            acc_ref = refs[11]

            @pl.when(pl.program_id(0) == 0)
            def _():
                acc_ref[...] = jnp.zeros_like(acc_ref)

            h2_ref[...], part = _sq_err_grad(h2, refs[8][...])
            acc_ref[...] += part
        else:
            h2_ref[...] = h2

    row = lambda i: (i, 0)
    big = pl.BlockSpec((tm, D_MODEL), row)
    wspec = lambda r, cdim: pl.BlockSpec((N_CHIPS, r, cdim), lambda i: (0, 0, 0))
    acc = pl.BlockSpec((SUBLANES, 128), lambda i: (0, 0))
    return pl.pallas_call(
        body, name="out_ple_loss" if last else "out_ple", grid=(s // tm,),
        in_specs=[big, pl.BlockSpec((tm, SSM_WIDTH), row), pl.BlockSpec((tm, ATTN_WIDTH), row),
                  pl.BlockSpec((tm, PLE_DIM), row), pl.BlockSpec((1, D_MODEL), lambda i: (0, 0)),
                  wspec(ROW_SHARD, D_MODEL), wspec(ROW_SHARD, D_MODEL), wspec(PLE_DIM, ROW_SHARD)] + [big] * last,
        out_specs=[big] * 2 + [acc] * last,
        out_shape=[jax.ShapeDtypeStruct((s, D_MODEL), F32)] * 2 + [jax.ShapeDtypeStruct((SUBLANES, 128), F32)] * last,
        compiler_params=_cparams(1),
    )(h, ys, ya, p, g2, w_out_l, w_pg_l, w_pp_l, *([target] if last else []))


def _rms_bwd(x, r, g, dy):
    gdy = g * dy
    dx = r * gdy - x * (r * r * r) * jnp.mean(x * gdy, axis=-1, keepdims=True)
    return dx, x * r * dy


def _out_ple_bwd(dh2, h1, p, g2, w_out_l, w_pg_l, w_pp_l):
    s = h1.shape[0]
    tm = _row_tile(s, 512)

    def body(dh2_ref, h1_ref, p_ref, g_ref, wo_ref, wpg_ref, wpp_ref,
             dh1_ref, dmix_ref, hn_ref, dgp_ref, dpp_ref, dh1b_ref, dg_ref):
        @pl.when(pl.program_id(0) == 0)
        def _():
            dg_ref[...] = jnp.zeros_like(dg_ref)

        h1 = h1_ref[...]
        dh2 = dh2_ref[...]
        g2v = g_ref[...]
        r2, hb, gate, pp = _ple_forward(h1, p_ref[...], g2v, wpg_ref, wpp_ref)
        dgp = (dh2 * pp) * gate * (1.0 - gate)
        dgpb = dgp.astype(BF16)
        dhn = jnp.concatenate([_dot_nt(dgpb, wpg_ref[sh]) for sh in range(N_CHIPS)], axis=-1)
        dx, dgrow = _rms_bwd(h1, r2, g2v, dhn)
        dh1 = dh2 + dx
        dh1b = dh1.astype(BF16)
        dh1_ref[...] = dh1
        dh1b_ref[...] = dh1b
        hn_ref[...] = hb
        dgp_ref[...] = dgpb
        dpp_ref[...] = (dh2 * gate).astype(BF16)
        dg_ref[...] += _colsum8(dgrow)
        for sh in range(N_CHIPS):
            dmix_ref[:, ROW_SHARD * sh:ROW_SHARD * (sh + 1)] = _dot_nt(dh1b, wo_ref[sh])

    row = lambda i: (i, 0)
    wspec = lambda r, cdim: pl.BlockSpec((N_CHIPS, r, cdim), lambda i: (0, 0, 0))
    big = pl.BlockSpec((tm, D_MODEL), row)
    return pl.pallas_call(
        body, name="out_ple_bwd", grid=(s // tm,),
        in_specs=[big, big, pl.BlockSpec((tm, PLE_DIM), row), pl.BlockSpec((1, D_MODEL), lambda i: (0, 0)),
                  wspec(ROW_SHARD, D_MODEL), wspec(ROW_SHARD, D_MODEL), wspec(PLE_DIM, ROW_SHARD)],
        out_specs=[big] * 6 + [pl.BlockSpec((SUBLANES, D_MODEL), lambda i: (0, 0))],
        out_shape=[jax.ShapeDtypeStruct((s, D_MODEL), F32)] * 2 + [jax.ShapeDtypeStruct((s, D_MODEL), BF16)] * 4
        + [jax.ShapeDtypeStruct((SUBLANES, D_MODEL), F32)],
        compiler_params=_cparams(1),
    )(dh2, h1, p, g2, w_out_l, w_pg_l, w_pp_l)


def _tn_matmul(a, b, n_blocks, block_a, name, into=None, first_block=0, total_blocks=None):
    s = a.shape[0]
    tk = _row_tile(s, 1024)
    nk = s // tk
    total_blocks = n_blocks if total_blocks is None else total_blocks
    ka, nb = a.shape[1], b.shape[1]
    if block_a:
        ka //= n_blocks
    else:
        nb //= n_blocks

    def body(*refs):
        a_ref, b_ref, o_ref, acc_ref = refs[0], refs[1], refs[-2], refs[-1]

        @pl.when(pl.program_id(0) == 0)
        def _():
            acc_ref[...] = jnp.zeros_like(acc_ref)

        at = a_ref[...].astype(BF16).T
        bb = b_ref[...].astype(BF16)
        for sh in range(n_blocks):
            if block_a:
                acc_ref[sh] += _dot(at[ka * sh:ka * (sh + 1), :], bb)
            else:
                acc_ref[sh] += _dot(at, bb[:, nb * sh:nb * (sh + 1)])

        @pl.when(pl.program_id(0) == nk - 1)
        def _():
            o_ref[...] = acc_ref[...].astype(BF16)

    in_specs = [pl.BlockSpec((tk, a.shape[1]), lambda i: (i, 0)), pl.BlockSpec((tk, b.shape[1]), lambda i: (i, 0))]
    operands = [a, b]
    aliases = {}
    if into is not None:
        in_specs.append(ANY)
        operands.append(into)
        aliases = {2: 0}
    return pl.pallas_call(
        body, name=name, grid=(nk,),
        in_specs=in_specs,
        out_specs=pl.BlockSpec((n_blocks, ka, nb), lambda i: (first_block // n_blocks, 0, 0)),
        out_shape=jax.ShapeDtypeStruct((total_blocks, ka, nb), BF16),
        scratch_shapes=[pltpu.VMEM((n_blocks, ka, nb), F32)],
        input_output_aliases=aliases,
        compiler_params=_cparams(1),
    )(*operands)


def _attn_bwd(qkv, o, proj, dmix, scatter=None):
    scatter = [] if scatter is None else scatter
    nsc = len(scatter)
    s = qkv.shape[0]
    tb = ATTN_BLOCK
    nq = s // tb
    n_sub = min(ATTN_SUB_BWD, nq)
    tq = n_sub * tb
    n_chain = 2 * n_sub

    def body(*refs):
        q_ref, k_ref, v_ref, o_ref, g_ref, dya_ref, tri_s_ref, tri_p_ref = refs[0:8]
        sc_ins = refs[8:8 + nsc]
        dq_ref, dk_ref, dv_ref, dg_ref = refs[8 + nsc:12 + nsc]
        sc_outs = refs[12 + nsc:12 + 2 * nsc]
        do_scr, l_scr, g_scr, s_scr, w_scr = refs[12 + 2 * nsc:17 + 2 * nsc]
        sc_sems = refs[17 + 2 * nsc:]
        i = pl.program_id(1)
        base = i * n_sub

        if nsc:
            @pl.when(jnp.logical_and(pl.program_id(0) == 0, i == 0))
            def _():
                _scatter_start(sc_ins, sc_outs, sc_sems)

        @pl.when(i == 0)
        def _():
            dk_ref[...] = jnp.zeros_like(dk_ref)
            dv_ref[...] = jnp.zeros_like(dv_ref)

        g = g_ref[...]
        sg = _sigmoid(g)
        dya = dya_ref[...]
        do_scr[...] = (dya * (g * sg)).astype(BF16)
        dg_ref[...] = dya * o_ref[...] * (sg * (1.0 + g * (1.0 - sg)))
        dq_ref[...] = jnp.zeros_like(dq_ref)
        g_scr[...] = jnp.zeros_like(g_scr)
        masks = _head_masks()

        def keep(t, a, h, c, r0, z, w):
            s_scr[c, t] = _sigmoid(z).astype(BF16)
            w_scr[c, t] = w.astype(BF16)

        steps = _chain_sweep(base, n_sub, q_ref, k_ref, tri_s_ref, l_scr, keep)
        row = lax.broadcasted_iota(jnp.int32, (tb, tb), 0)
        col = lax.broadcasted_iota(jnp.int32, (tb, tb), 1)

        def back(it, carry):
            t = steps - 1 - it
            r0s = [pl.multiple_of(jnp.maximum(base + a - t, 0) * tb, tb) for a in range(n_sub)]
            qhs, dohs, khs, gws = [], [], [], []
            for a in range(n_sub):
                kb = k_ref[pl.ds(r0s[a], tb), :]
                vb = v_ref[pl.ds(r0s[a], tb), :]
                qa = q_ref[a * tb:(a + 1) * tb, :]
                doa = do_scr[a * tb:(a + 1) * tb, :]
                for h, mask in enumerate(masks):
                    qhs.append(jnp.where(mask, qa, jnp.zeros_like(qa)))
                    khs.append(jnp.where(mask, kb, jnp.zeros_like(kb)))
                    dohs.append(jnp.where(mask, doa, jnp.zeros_like(doa)))
                    gws.append(w_scr[2 * a + h, t].astype(F32) * _dot_nt(dohs[-1], vb))
            parts = [_split_hilo(gw) for gw in gws]
            tri = tri_p_ref[...]
            sums = [_dot(hi, tri) + _dot(lo, tri) for hi, lo in parts]
            dzs = []
            for c, (gw, sm) in enumerate(zip(gws, sums)):
                gsum = g_scr[c]
                dz = gw - (gw + sm[:, 0:tb] + gsum) * s_scr[c, t].astype(F32)
                dz = jnp.where(col < row + t * tb, dz, 0.0)
                g_scr[c] = gsum + sm[:, tb:2 * tb]
                dzs.append(dz.astype(BF16))
            for c, dzb in enumerate(dzs):
                a = c // 2
                dk_ref[pl.ds(r0s[a], tb), :] += _dot_tn(dzb, qhs[c])
                dv_ref[pl.ds(r0s[a], tb), :] += _dot_tn(w_scr[c, t], dohs[c])
                dq_ref[a * tb:(a + 1) * tb, :] += _dot(dzb, khs[c])
            return carry

        lax.fori_loop(0, steps, back, 0)

        if nsc:
            @pl.when(jnp.logical_and(pl.program_id(0) == n_hp - 1, i == s // tq - 1))
            def _():
                _scatter_finish(sc_ins, sc_outs, sc_sems)

    n_hp = ATTN_WIDTH // (2 * HEAD_DIM)
    hp_blk = lambda off: pl.BlockSpec((tq, 2 * HEAD_DIM), lambda hp, i: (i, off + hp))
    res = lambda off: pl.BlockSpec((s, 2 * HEAD_DIM), lambda hp, i: (0, off + hp))
    tri = pl.BlockSpec((tb, 2 * tb), lambda hp, i: (0, 0))
    outs = pl.pallas_call(
        body, name="attn_bwd_scatter" if nsc else "attn_bwd", grid=(n_hp, s // tq),
        in_specs=[hp_blk(0), res(4), res(8), hp_blk(0), hp_blk(20), hp_blk(4), tri, tri] + [ANY] * nsc,
        out_specs=[hp_blk(0), res(0), res(0), hp_blk(0)] + [ANY] * nsc,
        out_shape=[jax.ShapeDtypeStruct((s, ATTN_WIDTH), F32)] * 4 + [jax.ShapeDtypeStruct(a.shape, a.dtype) for a in scatter],
        scratch_shapes=[pltpu.VMEM((tq, 2 * HEAD_DIM), BF16), pltpu.VMEM((n_chain, tb, tb), F32),
                        pltpu.VMEM((n_chain, tb, tb), F32), pltpu.VMEM((n_chain, nq, tb, tb), BF16),
                        pltpu.VMEM((n_chain, nq, tb, tb), BF16)] + (_scatter_sems(nsc) if nsc else []),
        compiler_params=_cparams(2),
    )(qkv, qkv, qkv, o, proj, dmix, _tri("suffix_incl"), _tri("prefix_strict"), *scatter)
    return outs[0], outs[1], outs[2], outs[3], (_scatter_own(outs[4:], scatter) if nsc else [])


def _ssm_glu_bwd(dmix, y, proj, d, w_glu_l, b_glu):
    s = y.shape[0]
    tm = _row_tile(s, 512)

    def body(dys_ref, y_ref, u_ref, gs_ref, d_ref, wg_ref, bg_ref,
             dyf_ref, du_ref, dgs_ref, z_ref, dzz_ref, dd_ref, db_ref):
        @pl.when(pl.program_id(0) == 0)
        def _():
            dd_ref[...] = jnp.zeros_like(dd_ref)
            db_ref[...] = jnp.zeros_like(db_ref)

        u = u_ref[...]
        dv = d_ref[...]
        yf, z, val, gate = _glu_forward(y_ref[...], u, dv, wg_ref, bg_ref[...])
        gs = gs_ref[...]
        sgs = _sigmoid(gs)
        sgate = _sigmoid(gate)
        dys = dys_ref[...]
        dgv = dys * (gs * sgs)
        dgs_ref[...] = dys * (val * sgate) * (sgs * (1.0 + gs * (1.0 - sgs)))
        dzz = jnp.concatenate([dgv * sgate, dgv * val * sgate * (1.0 - sgate)], axis=-1)
        dzzb = dzz.astype(BF16)
        dz = _dot_nt(dzzb[:, 0:ROW_SHARD], wg_ref[0])
        for sh in range(1, N_CHIPS):
            dz = dz + _dot_nt(dzzb[:, ROW_SHARD * sh:ROW_SHARD * (sh + 1)], wg_ref[sh])
        dyf = dz * _gelu_grad(yf)
        dyf_ref[...] = dyf
        du_ref[...] = dyf * dv
        z_ref[...] = z.astype(BF16)
        dzz_ref[...] = dzzb
        dd_ref[...] += _colsum8(dyf * u)
        db_ref[...] += _colsum8(dzz)

    row = lambda i: (i, 0)
    half = pl.BlockSpec((tm, SSM_WIDTH), row)
    return pl.pallas_call(
        body, name="ssm_glu_bwd", grid=(s // tm,),
        in_specs=[half, half, half, pl.BlockSpec((tm, SSM_WIDTH), lambda i: (i, 1)),
                  pl.BlockSpec((1, SSM_WIDTH), lambda i: (0, 0)),
                  pl.BlockSpec((N_CHIPS, SSM_WIDTH, ROW_SHARD), lambda i: (0, 0, 0)),
                  pl.BlockSpec((1, 2 * SSM_WIDTH), lambda i: (0, 0))],
        out_specs=[half, half, half, half, pl.BlockSpec((tm, 2 * SSM_WIDTH), row),
                   pl.BlockSpec((SUBLANES, SSM_WIDTH), lambda i: (0, 0)),
                   pl.BlockSpec((SUBLANES, 2 * SSM_WIDTH), lambda i: (0, 0))],
        out_shape=[jax.ShapeDtypeStruct((s, SSM_WIDTH), F32)] * 3
        + [jax.ShapeDtypeStruct((s, SSM_WIDTH), BF16), jax.ShapeDtypeStruct((s, 2 * SSM_WIDTH), BF16),
           jax.ShapeDtypeStruct((SUBLANES, SSM_WIDTH), F32), jax.ShapeDtypeStruct((SUBLANES, 2 * SSM_WIDTH), F32)],
        compiler_params=_cparams(1),
    )(dmix, y, proj, proj, d, w_glu_l, b_glu)


def _ssm_scan_bwd(dyf, xs, proj, wct, tab_rev, wbt):
    s = dyf.shape[0]
    tm = _row_tile(s, SCAN_TILE)
    nt = s // tm
    length = tm // SUBLANES

    def body(dy_ref, xs_ref, u_ref, wct_ref, tab_ref, wbt_ref, du_ref, dwc_ref, dwb_ref, da_ref, lam_ref, carry_ref):
        @pl.when(pl.program_id(1) == 0)
        def _():
            carry_ref[...] = jnp.zeros_like(carry_ref)
            dwc_ref[...] = jnp.zeros_like(dwc_ref)
            dwb_ref[...] = jnp.zeros_like(dwb_ref)
            da_ref[...] = jnp.zeros_like(da_ref)

        dyp = _interleave_chunks(dy_ref[...]).astype(BF16)
        up = _interleave_chunks(u_ref[...]).astype(BF16)
        lam_ref[...] = _dot(dyp, wct_ref[...])

        def tail(r0, lr, li, carry):
            er, ei, dar, dai = carry
            xr = xs_ref[pl.ds(r0, SUBLANES), 0:CH_S]
            xi = xs_ref[pl.ds(r0, SUBLANES), CH_S:2 * CH_S]
            return lr, li, dar + (xr * er + xi * ei), dai + (xr * ei - xi * er)

        fix, (gr, gi) = _chunk_scan(lam_ref, tab_ref, carry_ref, length, reverse=True, tail=tail)
        zero = jnp.zeros((SUBLANES, CH_S), F32)
        _, _, dar, dai = lax.fori_loop(0, length, fix, (gr, gi, zero, zero), unroll=2)
        da_ref[:, 0:CH_S] += dar
        da_ref[:, CH_S:2 * CH_S] += dai
        lamb = lam_ref[...].astype(BF16)
        du_ref[...] = _time_order(_dot(lamb, wbt_ref[...]))
        dwc_ref[...] += _dot_tn(xs_ref[...].astype(BF16), dyp)
        dwb_ref[...] += _dot_tn(up, lamb)

    rev = lambda j, i: (nt - 1 - i, j)
    return pl.pallas_call(
        body, name="ssm_scan_bwd", grid=(SSM_CHUNKS, nt),
        in_specs=[pl.BlockSpec((tm, CH_W), rev),
                  pl.BlockSpec((None, tm, 2 * CH_S), lambda j, i: (j, nt - 1 - i, 0)),
                  pl.BlockSpec((tm, CH_W), rev),
                  pl.BlockSpec((None, CH_W, 2 * CH_S), lambda j, i: (j, 0, 0)),
                  pl.BlockSpec((None, length, 2 * CH_S), lambda j, i: (j, 0, 0)),
                  pl.BlockSpec((None, 2 * CH_S, CH_W), lambda j, i: (j, 0, 0))],
        out_specs=[pl.BlockSpec((tm, CH_W), rev),
                   pl.BlockSpec((None, 2 * CH_S, CH_W), lambda j, i: (j, 0, 0)),
                   pl.BlockSpec((None, CH_W, 2 * CH_S), lambda j, i: (j, 0, 0)),
                   pl.BlockSpec((None, SUBLANES, 2 * CH_S), lambda j, i: (j, 0, 0))],
        out_shape=[jax.ShapeDtypeStruct((s, SSM_WIDTH), F32),
                   jax.ShapeDtypeStruct((SSM_CHUNKS, 2 * CH_S, CH_W), F32),
                   jax.ShapeDtypeStruct((SSM_CHUNKS, CH_W, 2 * CH_S), F32),
                   jax.ShapeDtypeStruct((SSM_CHUNKS, SUBLANES, 2 * CH_S), F32)],
        scratch_shapes=[pltpu.VMEM((tm, 2 * CH_S), F32), pltpu.VMEM((SUBLANES, 2 * CH_S), F32)],
        compiler_params=_cparams(2),
    )(dyf, xs, proj, wct, tab_rev, wbt)


def _in_proj_bwd(h, g1, w_in_l, qg, kg, proj, du_a, du_b, dgs, dq, dk, dv, dga, dh1):
    s = h.shape[0]
    tm = _row_tile(s, 256)

    def body(h_ref, g_ref, w_ref, qg_ref, kg_ref, ones_ref, q_ref, k_ref, dua_ref, dub_ref, dgs_ref, dq_ref, dk_ref,
             dv_ref, dga_ref, dh1_ref, dh_ref, hn_ref, dp_ref, dg1_ref, dqg_ref, dkg_ref):
        @pl.when(pl.program_id(0) == 0)
        def _():
            dg1_ref[...] = jnp.zeros_like(dg1_ref)
            dqg_ref[...] = jnp.zeros_like(dqg_ref)
            dkg_ref[...] = jnp.zeros_like(dkg_ref)

        ones = ones_ref[...]

        def head_norm_bwd(x, gain, dy):
            r = lax.rsqrt(_dot_hilo(x * x, ones) + RMS_EPS)
            gdy = gain * dy
            dx = r * gdy - x * (r * r * r) * _dot_hilo(x * gdy, ones)
            return dx, x * r * dy

        dqr, dqg_rows = head_norm_bwd(q_ref[...], qg_ref[...], dq_ref[...] * ATTN_SCALE)
        dkr, dkg_rows = head_norm_bwd(k_ref[...], kg_ref[...], dk_ref[...])
        dqg_ref[...] += _colsum8(dqg_rows)
        dkg_ref[...] += _colsum8(dkg_rows)
        dp_ref[:, 0:512] = (dua_ref[...] + dub_ref[...]).astype(BF16)
        dp_ref[:, 512:1024] = dgs_ref[...].astype(BF16)
        dp_ref[:, 1024:1536] = dqr.astype(BF16)
        dp_ref[:, 1536:2048] = dkr.astype(BF16)
        dp_ref[:, 2048:2560] = dv_ref[...].astype(BF16)
        dp_ref[:, 2560:3072] = dga_ref[...].astype(BF16)
        dhn = _dot_nt(dp_ref[:, 0:IN_SHARD], w_ref[0])
        for sh in range(1, N_CHIPS):
            dhn = dhn + _dot_nt(dp_ref[:, IN_SHARD * sh:IN_SHARD * (sh + 1)], w_ref[sh])
        x = h_ref[...]
        gv = g_ref[...]
        r, hn = _rms_rows(x, gv)
        dx, dg_rows = _rms_bwd(x, r, gv, dhn)
        dh_ref[...] = dh1_ref[...] + dx
        hn_ref[...] = hn.astype(BF16)
        dg1_ref[...] += _colsum8(dg_rows)

    row = lambda i: (i, 0)
    full = lambda shape: pl.BlockSpec(shape, lambda i: (0,) * len(shape))
    big = pl.BlockSpec((tm, D_MODEL), row)
    half = pl.BlockSpec((tm, 512), row)
    return pl.pallas_call(
        body, name="in_proj_bwd", grid=(s // tm,),
        in_specs=[big, full((1, D_MODEL)), full((N_CHIPS, D_MODEL, IN_SHARD)),
                  full((1, ATTN_WIDTH)), full((1, ATTN_WIDTH)), full((ATTN_WIDTH, ATTN_WIDTH)),
                  pl.BlockSpec((tm, 512), lambda i: (i, 2)), pl.BlockSpec((tm, 512), lambda i: (i, 3)),
                  half, half, half, half, half, half, half, big],
        out_specs=[big, big, pl.BlockSpec((tm, IN_COLS), row), pl.BlockSpec((SUBLANES, D_MODEL), lambda i: (0, 0)),
                   pl.BlockSpec((SUBLANES, ATTN_WIDTH), lambda i: (0, 0)), pl.BlockSpec((SUBLANES, ATTN_WIDTH), lambda i: (0, 0))],
        out_shape=[jax.ShapeDtypeStruct((s, D_MODEL), F32), jax.ShapeDtypeStruct((s, D_MODEL), BF16),
                   jax.ShapeDtypeStruct((s, IN_COLS), BF16), jax.ShapeDtypeStruct((SUBLANES, D_MODEL), F32),
                   jax.ShapeDtypeStruct((SUBLANES, ATTN_WIDTH), F32), jax.ShapeDtypeStruct((SUBLANES, ATTN_WIDTH), F32)],
        compiler_params=_cparams(1),
    )(h, g1, w_in_l, qg, kg, _head_ones(), proj, proj, du_a, du_b, dgs, dq, dk, dv, dga, dh1)


SMALL_NAMES = ("mix_norm_g", "ssm_a_re", "ssm_a_im", "ssm_log_dt", "ssm_b_re", "ssm_b_im", "ssm_c_re", "ssm_c_im",
               "ssm_d", "ssm_b_glu", "q_norm_g", "k_norm_g", "ple_norm_g")
SMALL_4D = ("ssm_b_re", "ssm_b_im", "ssm_c_re", "ssm_c_im")
BIG_NAMES = ("w_in", "ssm_w_glu", "w_out", "w_ple_gate", "w_ple_proj")


def _ssm_setup(sm, layer, length):
    col = lambda a: a[layer].reshape(1, N_STATES)
    a_re, a_im = col(sm["ssm_a_re"]), col(sm["ssm_a_im"])
    log_dt = jnp.repeat(sm["ssm_log_dt"][layer], SSM_STATE).reshape(1, N_STATES)
    b_re = sm["ssm_b_re"][layer].reshape(N_STATES, SSM_GROUP).T
    b_im = sm["ssm_b_im"][layer].reshape(N_STATES, SSM_GROUP).T
    disc_in = (a_re, a_im, log_dt, b_re, b_im)
    ab_re, ab_im, bb_re, bb_im = _disc_fwd(*disc_in)
    wb = jnp.concatenate([_block_diag_in(bb_re), _block_diag_in(bb_im)], axis=-1)
    wc = jnp.concatenate([_block_diag_out(sm["ssm_c_re"][layer]), -_block_diag_out(sm["ssm_c_im"][layer])], axis=1)
    return dict(disc_in=disc_in, wb=wb.astype(BF16), wbt=wb.transpose(0, 2, 1).astype(BF16),
                wc=wc.astype(BF16), wct=wc.transpose(0, 2, 1).astype(BF16),
                tab=_scan_powers(ab_re, ab_im, length, False), tab_rev=_scan_powers(ab_re, ab_im, length, True))


def _whole_blocks(names, gathered):
    return {n: g.reshape(N_CHIPS, 2 * g.shape[2], g.shape[3]) for n, g in zip(names, gathered)}


def _local_step(x, p, target, sm, w_in0, local=None, gathered=None, layer1_hook=None):
    wg = [dict(w_in=w_in0), {}] if gathered is None else gathered
    tile8 = lambda a: jnp.tile(a, ATTN_WIDTH // HEAD_DIM).reshape(1, ATTN_WIDTH)
    saved = []
    h = x
    for l in range(N_LAYERS):
        ssm = _ssm_setup(sm, l, _row_tile(x.shape[0], SCAN_TILE) // SUBLANES)
        g1 = sm["mix_norm_g"][l].reshape(1, D_MODEL)
        g2 = sm["ple_norm_g"][l].reshape(1, D_MODEL)
        qg, kg = tile8(sm["q_norm_g"][l]), tile8(sm["k_norm_g"][l])
        dsk = sm["ssm_d"][l].reshape(1, SSM_WIDTH)
        bgl = sm["ssm_b_glu"][l].reshape(1, 2 * SSM_WIDTH)
        proj, qkv = _in_proj(h, g1, wg[l]["w_in"], qg, kg)
        if l == 0 and local is not None:
            rest = BIG_NAMES[1:]
            xs, y, got = _ssm_scan_fwd(proj, ssm["wb"], ssm["tab"], ssm["wc"], [local[n] for n in rest], [0] * len(rest))
            wg[0].update(_whole_blocks(rest, got))
            ys = _ssm_glu_fwd(y, proj, dsk, wg[0]["ssm_w_glu"], bgl)
            o, ya, got = _attn_fwd(qkv, proj, [local[n] for n in BIG_NAMES], [2] * len(BIG_NAMES))
            wg[1].update(_whole_blocks(BIG_NAMES, got))
        else:
            xs, y, _ = _ssm_scan_fwd(proj, ssm["wb"], ssm["tab"], ssm["wc"])
            ys = _ssm_glu_fwd(y, proj, dsk, wg[l]["ssm_w_glu"], bgl)
            o, ya, _ = _attn_fwd(qkv, proj)
        tail = (target,) if l == N_LAYERS - 1 else ()
        h1, h2, *sq = _out_ple(h, ys, ya, p[l], g2, wg[l]["w_out"], wg[l]["w_ple_gate"], wg[l]["w_ple_proj"], *tail)
        saved.append(dict(ssm=ssm, g1=g1, g2=g2, qg=qg, kg=kg, dsk=dsk, bgl=bgl, h=h, proj=proj, qkv=qkv, xs=xs, y=y,
                          ys=ys, o=o, ya=ya, h1=h1))
        h = h2
    dh = h
    loss = 0.5 * jnp.sum(sq[0]) / D_MODEL

    gbig = [{} for _ in range(N_LAYERS)]
    scattered = []
    gsm = {n: [None] * N_LAYERS for n in SMALL_NAMES}
    for l in reversed(range(N_LAYERS)):
        sv = saved[l]
        ssm = sv["ssm"]
        dh1, dmix, hn2b, dgpb, dppb, dh1b, dg2 = _out_ple_bwd(dh, sv["h1"], p[l], sv["g2"], wg[l]["w_out"],
                                                              wg[l]["w_ple_gate"], wg[l]["w_ple_proj"])
        gsm["ple_norm_g"][l] = dg2.sum(0)
        gbig[l]["w_ple_proj"] = _tn_matmul(p[l], dppb, N_CHIPS, False, "dw_ple_proj")
        gbig[l]["w_ple_gate"] = _tn_matmul(hn2b, dgpb, N_CHIPS, True, "dw_ple_gate")
        dwo = _tn_matmul(sv["ys"], dh1b, 2, True, "dw_out_ssm", None, 0, N_CHIPS)
        gbig[l]["w_out"] = _tn_matmul(sv["ya"], dh1b, 2, True, "dw_out_attn", dwo, 2, N_CHIPS)
        if l == 0 and layer1_hook is not None:
            dqs, dkn, dv, dga, scattered = _attn_bwd(sv["qkv"], sv["o"], sv["proj"], dmix, layer1_hook(gbig[1]))
        else:
            dqs, dkn, dv, dga, _ = _attn_bwd(sv["qkv"], sv["o"], sv["proj"], dmix)
        dyf, du_a, dgs, zb, dzzb, dd, dbg = _ssm_glu_bwd(dmix, sv["y"], sv["proj"], sv["dsk"], wg[l]["ssm_w_glu"], sv["bgl"])
        gsm["ssm_d"][l] = dd.sum(0).reshape(SSM_GROUPS, SSM_GROUP)
        gsm["ssm_b_glu"][l] = dbg.sum(0)
        gbig[l]["ssm_w_glu"] = _tn_matmul(zb, dzzb, N_CHIPS, False, "dw_glu")
        du_b, dwc, dwb, da = _ssm_scan_bwd(dyf, sv["xs"], sv["proj"], ssm["wct"], ssm["tab_rev"], ssm["wbt"])
        gsm["ssm_c_re"][l] = _block_diag_out_t(dwc[:, 0:CH_S, :])
        gsm["ssm_c_im"][l] = -_block_diag_out_t(dwc[:, CH_S:, :])
        da = da.sum(1)
        g_ab_re = da[:, 0:CH_S].reshape(1, N_STATES)
        g_ab_im = da[:, CH_S:].reshape(1, N_STATES)
        g_bb_re = _block_diag_in_t(dwb[:, :, 0:CH_S])
        g_bb_im = _block_diag_in_t(dwb[:, :, CH_S:])
        d_are, d_aim, d_ldt, d_bre, d_bim = _disc_bwd(*ssm["disc_in"], g_ab_re, g_ab_im, g_bb_re, g_bb_im)
        gsm["ssm_a_re"][l] = d_are.reshape(SSM_GROUPS, SSM_STATE)
        gsm["ssm_a_im"][l] = d_aim.reshape(SSM_GROUPS, SSM_STATE)
        gsm["ssm_log_dt"][l] = d_ldt.reshape(SSM_GROUPS, SSM_STATE).sum(1)
        gsm["ssm_b_re"][l] = d_bre.T.reshape(SSM_GROUPS, SSM_STATE, SSM_GROUP)
        gsm["ssm_b_im"][l] = d_bim.T.reshape(SSM_GROUPS, SSM_STATE, SSM_GROUP)
        dh, hnb, dprojb, dg1, dqg, dkg = _in_proj_bwd(sv["h"], sv["g1"], wg[l]["w_in"], sv["qg"], sv["kg"], sv["proj"],
                                                      du_a, du_b, dgs, dqs, dkn, dv, dga, dh1)
        gsm["mix_norm_g"][l] = dg1.sum(0)
        gsm["q_norm_g"][l] = dqg.sum(0).reshape(-1, HEAD_DIM).sum(0)
        gsm["k_norm_g"][l] = dkg.sum(0).reshape(-1, HEAD_DIM).sum(0)
        gbig[l]["w_in"] = _tn_matmul(hnb, dprojb, N_CHIPS, False, "dw_in")
    gsm = {n: jnp.stack(v, 0) for n, v in gsm.items()}
    return loss, dh, gbig, gsm, scattered


_SMALL_PAD = 8 * 8 * 128


def _pack_small(d, extra):
    flat = jnp.concatenate([d[n].reshape(-1) for n in SMALL_NAMES] + [jnp.stack(extra)])
    n = flat.shape[0]
    padded = -(-n // _SMALL_PAD) * _SMALL_PAD
    return jnp.pad(flat, (0, padded - n))


def _unpack_small(flat, like):
    out, off = {}, 0
    for n in SMALL_NAMES:
        size = like[n].size
        out[n] = flat[off:off + size].reshape(like[n].shape)
        off += size
    return out, flat[off:]


def _half_views(arrs):
    return [a.reshape(a.shape[0], 2, a.shape[1] // 2, a.shape[2]) for a in arrs]


def _chip_sums(views, out_dtypes, tag):
    recv = _sibling_push(views, "grad_push_" + tag)
    return [_add_my_half(v, r, dt, "grad_half_add") for v, r, dt in zip(views, recv, out_dtypes)]


def kernel(x, p, mix_norm_g, w_in, ssm_a_re, ssm_a_im, ssm_log_dt, ssm_b_re, ssm_b_im, ssm_c_re, ssm_c_im, ssm_d, ssm_w_glu, ssm_b_glu, q_norm_g, k_norm_g, w_out, ple_norm_g, w_ple_gate, w_ple_proj, loss_target, m_mix_norm_g, m_w_in, m_ssm_a_re, m_ssm_a_im, m_ssm_log_dt, m_ssm_b_re, m_ssm_b_im, m_ssm_c_re, m_ssm_c_im, m_ssm_d, m_ssm_w_glu, m_ssm_b_glu, m_q_norm_g, m_k_norm_g, m_w_out, m_ple_norm_g, m_w_ple_gate, m_w_ple_proj, v_mix_norm_g, v_w_in, v_ssm_a_re, v_ssm_a_im, v_ssm_log_dt, v_ssm_b_re, v_ssm_b_im, v_ssm_c_re, v_ssm_c_im, v_ssm_d, v_ssm_w_glu, v_ssm_b_glu, v_q_norm_g, v_k_norm_g, v_w_out, v_ple_norm_g, v_w_ple_gate, v_w_ple_proj):
    args = dict(locals())
    names = ("mix_norm_g", "w_in", "ssm_a_re", "ssm_a_im", "ssm_log_dt", "ssm_b_re", "ssm_b_im", "ssm_c_re", "ssm_c_im",
             "ssm_d", "ssm_w_glu", "ssm_b_glu", "q_norm_g", "k_norm_g", "w_out", "ple_norm_g", "w_ple_gate", "w_ple_proj")
    w = {n: args[n] for n in names}
    m = {n: args["m_" + n] for n in names}
    v = {n: args["v_" + n] for n in names}

    local = {n: w[n].astype(BF16).reshape(2 * N_LAYERS, w[n].shape[1] // 2, w[n].shape[2]) for n in BIG_NAMES}
    w_in0 = _chip_gather([local["w_in"]], "w_in_gather")[0].reshape(N_CHIPS, D_MODEL, IN_SHARD)
    sm = {n: w[n] for n in SMALL_NAMES}
    nb = len(BIG_NAMES)
    loss, dx, gbig, gsm, got1 = _local_step(
        x[0], p[:, 0], loss_target[0], sm, w_in0, local,
        layer1_hook=lambda g1: _chip_sums(_half_views([g1[n] for n in BIG_NAMES]), [BF16] * nb, "layer1"))

    small = _pack_small(gsm, [loss]).reshape(N_CHIPS, 2, SUBLANES, -1)
    chip0 = _chip_sums(_half_views([gbig[0][n] for n in BIG_NAMES]) + [small], [BF16] * nb + [F32], "layer0")
    got0 = _chip_scatter(chip0, "grad_chip_scatter")
    tot1 = [_sum4(a, "grad_chip_sum") for a in got1]
    tot0 = [_sum4(a, "grad_chip_sum") for a in got0]
    pieces = [(t, k, (l,)) for l, tots in enumerate((tot0[:nb], tot1)) for k, t in enumerate(tots)] + [(tot0[nb], nb, ())]
    joined = _sibling_join(pieces, [(N_LAYERS, 2) + t.shape for t in tot1] + [(2,) + tot0[nb].shape], "grad_sibling_join")
    small_all = _chip_gather([joined[nb]], "small_grad_gather")[0]
    small_tot = small_all.reshape(-1)
    g = {n: j.reshape(w[n].shape) for n, j in zip(BIG_NAMES, joined)}
    g_small, rest = _unpack_small(small_tot, sm)
    g.update(g_small)
    loss = rest[0]

    delta, new_m, new_v = {}, {}, {}
    for n in BIG_NAMES:
        lanes = w[n].shape[-1]
        outs = _adamw(_as_rows(w[n], lanes), _as_rows(g[n], lanes), _as_rows(m[n], lanes), _as_rows(v[n], lanes), "adamw_" + n)
        delta[n], new_m[n], new_v[n] = [o.reshape(w[n].shape) for o in outs]
    for group, per_layer in ((SMALL_4D, True), (tuple(n for n in SMALL_NAMES if n not in SMALL_4D), False)):
        outs = _adamw_many(*[[d[n] for n in group] for d in (w, g, m, v)], "adamw_small_4d" if per_layer else "adamw_small", per_layer)
        for d, o in zip((delta, new_m, new_v), outs):
            d.update(zip(group, o))

    return (loss, dx[None], *[g[n] for n in names], *[delta[n] for n in names],
            *[new_m[n] for n in names], *[new_v[n] for n in names])
```

```python
import functools
import math

import jax
import jax.numpy as jnp
from jax import lax
from jax.experimental import pallas as pl
from jax.experimental.pallas import tpu as pltpu

F32 = jnp.float32
BF16 = jnp.bfloat16

D_MODEL = 1024
N_LAYERS = 2
N_CHIPS = 4
IN_COLS = 3072
IN_SHARD = IN_COLS // N_CHIPS
SSM_WIDTH = 512
SSM_GROUP = 16
SSM_GROUPS = 32
SSM_STATE = 64
N_STATES = SSM_GROUPS * SSM_STATE
SSM_CHUNKS = 4
CH_W = SSM_WIDTH // SSM_CHUNKS
CH_S = N_STATES // SSM_CHUNKS
ATTN_WIDTH = 512
HEAD_DIM = 64
PLE_DIM = 256
ROW_SHARD = 256
RMS_EPS = 1e-6
ATTN_SCALE = HEAD_DIM ** -0.5
ATTN_BLOCK = 128
EXP_ZERO = -87.5
SUBLANES = 8
SCAN_TILE = 1024
V7X_VMEM_LIMIT = 52 * 1024 * 1024

ADAM_LR = 0.001
ADAM_B1 = 0.9
ADAM_B2 = 0.999
ADAM_EPS = 1e-08
ADAM_WD = 0.01
ADAM_STEP = 10

MESH = pl.DeviceIdType.MESH
ANY = pl.BlockSpec(memory_space=pl.ANY)


def _cparams(n_grid=0, parallel=0):
    sem = tuple(["parallel"] * parallel + ["arbitrary"] * (n_grid - parallel))
    return pltpu.CompilerParams(dimension_semantics=sem, vmem_limit_bytes=V7X_VMEM_LIMIT)


def _dot(a, b):
    return jnp.dot(a, b, preferred_element_type=F32)


def _dot_nt(a, b):
    return lax.dot_general(a, b, (((1,), (1,)), ((), ())), preferred_element_type=F32)


def _dot_tn(a, b):
    return lax.dot_general(a, b, (((0,), (0,)), ((), ())), preferred_element_type=F32)


def _split_hilo(a):
    hi = a.astype(BF16)
    lo = (a - hi.astype(F32)).astype(BF16)
    return hi, lo


def _dot_hilo(a, b):
    hi, lo = _split_hilo(a)
    return _dot(hi, b) + _dot(lo, b)


def _sigmoid(x):
    return 0.5 * (jnp.tanh(0.5 * x) + 1.0)


_GELU_C = math.sqrt(2.0 / math.pi)


def _gelu(x):
    return 0.5 * x * (1.0 + jnp.tanh(_GELU_C * (x + 0.044715 * (x * x * x))))


def _gelu_grad(x):
    t = jnp.tanh(_GELU_C * (x + 0.044715 * (x * x * x)))
    return 0.5 * (1.0 + t) + 0.5 * x * (1.0 - t * t) * (_GELU_C * (1.0 + 3.0 * 0.044715 * (x * x)))


def _row_tile(s, want):
    for t in range(min(s, want), 7, -1):
        if s % t == 0 and t % SUBLANES == 0:
            return t
    return s


def _coords():
    return lax.axis_index("x"), lax.axis_index("y"), lax.axis_index("c")


def _other_chips(x, y):
    return [(1 - x, y), (x, 1 - y), (1 - x, 1 - y)]


def _remote(src, dst, send_sem, recv_sem, dev):
    return pltpu.make_async_remote_copy(src_ref=src, dst_ref=dst, send_sem=send_sem, recv_sem=recv_sem,
                                        device_id=dev, device_id_type=MESH)


def _set_block(buf, block, index):
    return lax.dynamic_update_index_in_dim(buf, block, index, 0)


def _gather_sems(n):
    return [pltpu.SemaphoreType.DMA((3 * n,)) for _ in range(4)]


def _gather_copies(ins, bases, outs, sems):
    send_sems, recv_sems, fwd_send, fwd_recv = sems
    x, y, c = _coords()
    me_chip = 2 * x + y
    sibling = (x, y, 1 - c)
    first, landed, passed, from_sibling = [], [], [], []
    for k in range(len(ins)):
        for j, (cx, cy) in enumerate(_other_chips(x, y)):
            i = 3 * k + j
            first.append(_remote(ins[k].at[bases[k] + c], outs[k].at[me_chip, c], send_sems.at[i], recv_sems.at[i], (cx, cy, c)))
            blk = outs[k].at[2 * cx + cy, c]
            landed.append(_remote(blk, blk, send_sems.at[i], recv_sems.at[i], (cx, cy, c)))
            passed.append(_remote(blk, blk, fwd_send.at[i], fwd_recv.at[i], sibling))
            blk = outs[k].at[2 * cx + cy, 1 - c]
            from_sibling.append(_remote(blk, blk, fwd_send.at[i], fwd_recv.at[i], sibling))
    return first, landed, passed, from_sibling


def _gather_start(ins, bases, outs, sems):
    for cp in _gather_copies(ins, bases, outs, sems)[0]:
        cp.start()


def _gather_finish(ins, bases, outs, sems):
    first, landed, passed, from_sibling = _gather_copies(ins, bases, outs, sems)
    for arrived, forward in zip(landed, passed):
        arrived.wait_recv()
        forward.start()
    for cp in from_sibling:
        cp.wait_recv()
    for cp in first + passed:
        cp.wait_send()


def _gather_outputs(arrs):
    return [jax.ShapeDtypeStruct((N_CHIPS, 2) + a.shape[1:], a.dtype) for a in arrs]


def _gather_own(outs, arrs, bases):
    me_chip = 2 * lax.axis_index("x") + lax.axis_index("y")
    return [_set_block(o, lax.slice_in_dim(a, b, b + 2, axis=0), me_chip) for o, a, b in zip(outs, arrs, bases)]


def _chip_gather(arrs, name, bases=None):
    n = len(arrs)
    bases = [0] * n if bases is None else bases

    def body(*refs):
        ins, outs, sems = refs[:n], refs[n:2 * n], refs[2 * n:]
        _gather_start(ins, bases, outs, sems)
        _gather_finish(ins, bases, outs, sems)

    outs = pl.pallas_call(
        body, name=name, out_shape=_gather_outputs(arrs),
        in_specs=[ANY] * n, out_specs=[ANY] * n, scratch_shapes=_gather_sems(n),
    )(*arrs)
    return _gather_own(outs, arrs, bases)


def _sibling_push(arrs, name):
    n = len(arrs)

    def body(*refs):
        ins, outs = refs[:n], refs[n:2 * n]
        send_sems, recv_sems = refs[2 * n:]
        x, y, c = _coords()
        cps = [_remote(ins[k].at[pl.ds(0, N_CHIPS), 1 - c], outs[k], send_sems.at[k], recv_sems.at[k], (x, y, 1 - c))
               for k in range(n)]
        for cp in cps:
            cp.start()
        for cp in cps:
            cp.wait_recv()
        for cp in cps:
            cp.wait_send()

    return pl.pallas_call(
        body, name=name,
        out_shape=[jax.ShapeDtypeStruct((a.shape[0],) + a.shape[2:], a.dtype) for a in arrs],
        in_specs=[ANY] * n, out_specs=[ANY] * n,
        scratch_shapes=[pltpu.SemaphoreType.DMA((n,)), pltpu.SemaphoreType.DMA((n,))],
    )(*arrs)


def _sibling_join(pieces, out_shapes, name):
    n = len(pieces)
    no = len(out_shapes)

    def body(*refs):
        ins, outs = refs[:n], refs[n:n + no]
        send_sems, recv_sems = refs[n + no:]
        x, y, c = _coords()
        sibling = (x, y, 1 - c)
        cps = [_remote(ins[k], outs[o].at[lead + (c,)], send_sems.at[k], recv_sems.at[k], sibling)
               for k, (_, o, lead) in enumerate(pieces)]
        for cp in cps:
            cp.start()
        for k, (_, o, lead) in enumerate(pieces):
            blk = outs[o].at[lead + (1 - c,)]
            _remote(blk, blk, send_sems.at[k], recv_sems.at[k], sibling).wait_recv()
        for cp in cps:
            cp.wait_send()

    outs = pl.pallas_call(
        body, name=name,
        out_shape=[jax.ShapeDtypeStruct(sh, F32) for sh in out_shapes],
        in_specs=[ANY] * n, out_specs=[ANY] * no,
        scratch_shapes=[pltpu.SemaphoreType.DMA((n,)), pltpu.SemaphoreType.DMA((n,))],
    )(*[a for a, _, _ in pieces])
    outs = list(outs)
    c = lax.axis_index("c")
    for a, o, lead in pieces:
        block = a.reshape((1,) * (len(lead) + 1) + a.shape)
        outs[o] = lax.dynamic_update_slice(outs[o], block, lead + (c,) + (0,) * a.ndim)
    return outs


def _scatter_sems(n):
    return [pltpu.SemaphoreType.DMA((3 * n,)), pltpu.SemaphoreType.DMA((3 * n,))]


def _scatter_copies(ins, outs, sems):
    send_sems, recv_sems = sems
    x, y, c = _coords()
    me_chip = 2 * x + y
    sends, arrivals = [], []
    for k in range(len(ins)):
        for j, (cx, cy) in enumerate(_other_chips(x, y)):
            i = 3 * k + j
            sends.append(_remote(ins[k].at[2 * cx + cy], outs[k].at[me_chip], send_sems.at[i], recv_sems.at[i], (cx, cy, c)))
            blk = outs[k].at[2 * cx + cy]
            arrivals.append(_remote(blk, blk, send_sems.at[i], recv_sems.at[i], (cx, cy, c)))
    return sends, arrivals


def _scatter_start(ins, outs, sems):
    for cp in _scatter_copies(ins, outs, sems)[0]:
        cp.start()


def _scatter_finish(ins, outs, sems):
    sends, arrivals = _scatter_copies(ins, outs, sems)
    for cp in arrivals:
        cp.wait_recv()
    for cp in sends:
        cp.wait_send()


def _scatter_own(outs, arrs):
    me_chip = 2 * lax.axis_index("x") + lax.axis_index("y")
    return [_set_block(o, lax.dynamic_index_in_dim(a, me_chip, 0, keepdims=False), me_chip) for o, a in zip(outs, arrs)]


def _chip_scatter(arrs, name):
    n = len(arrs)

    def body(*refs):
        ins, outs, sems = refs[:n], refs[n:2 * n], refs[2 * n:]
        _scatter_start(ins, outs, sems)
        _scatter_finish(ins, outs, sems)

    outs = pl.pallas_call(
        body, name=name,
        out_shape=[jax.ShapeDtypeStruct(a.shape, a.dtype) for a in arrs],
        in_specs=[ANY] * n, out_specs=[ANY] * n, scratch_shapes=_scatter_sems(n),
    )(*arrs)
    return _scatter_own(outs, arrs)


def _as_rows(a, lanes):
    return a.reshape(-1, lanes)


def _add_my_half(v, recv, out_dtype, name):
    n_sh, _, h, cdim = v.shape
    tr = _row_tile(h, 512)

    def body(c_ref, a_ref, b_ref, o_ref):
        o_ref[...] = (a_ref[...].astype(F32) + b_ref[...].astype(F32)).astype(out_dtype)

    c = lax.axis_index("c").astype(jnp.int32).reshape(1)
    return pl.pallas_call(
        body, name=name,
        grid_spec=pltpu.PrefetchScalarGridSpec(
            num_scalar_prefetch=1, grid=(n_sh, h // tr),
            in_specs=[pl.BlockSpec((None, None, tr, cdim), lambda sh, i, c_ref: (sh, c_ref[0], i, 0)),
                      pl.BlockSpec((None, tr, cdim), lambda sh, i, c_ref: (sh, i, 0))],
            out_specs=pl.BlockSpec((None, tr, cdim), lambda sh, i, c_ref: (sh, i, 0))),
        out_shape=jax.ShapeDtypeStruct((n_sh, h, cdim), out_dtype),
        compiler_params=_cparams(2),
    )(c, v, recv)


def _sum4(parts, name):
    _, r, cdim = parts.shape
    tr = _row_tile(r, 512)

    def body(p_ref, o_ref):
        acc = p_ref[0].astype(F32) + p_ref[1].astype(F32)
        acc = acc + p_ref[2].astype(F32)
        o_ref[...] = acc + p_ref[3].astype(F32)

    return pl.pallas_call(
        body, name=name, grid=(r // tr,),
        in_specs=[pl.BlockSpec((N_CHIPS, tr, cdim), lambda i: (0, i, 0))],
        out_specs=pl.BlockSpec((tr, cdim), lambda i: (i, 0)),
        out_shape=jax.ShapeDtypeStruct((r, cdim), F32),
        compiler_params=_cparams(1),
    )(parts)


def _adamw_math(w, g, m, v):
    c1 = 1.0 - ADAM_B1 ** ADAM_STEP
    c2 = 1.0 - ADAM_B2 ** ADAM_STEP
    nm = ADAM_B1 * m + (1.0 - ADAM_B1) * g
    nv = ADAM_B2 * v + (1.0 - ADAM_B2) * (g * g)
    delta = -ADAM_LR * ((nm / c1) / (jnp.sqrt(nv / c2) + ADAM_EPS) + ADAM_WD * w)
    return delta, nm, nv


def _adamw(w, g, m, v, name):
    r, cdim = w.shape
    tr = _row_tile(r, 256)

    def body(w_ref, g_ref, m_ref, v_ref, d_ref, nm_ref, nv_ref):
        d_ref[...], nm_ref[...], nv_ref[...] = _adamw_math(w_ref[...], g_ref[...], m_ref[...], v_ref[...])

    spec = pl.BlockSpec((tr, cdim), lambda i: (i, 0))
    return pl.pallas_call(
        body, name=name, grid=(r // tr,),
        in_specs=[spec] * 4, out_specs=[spec] * 3,
        out_shape=[jax.ShapeDtypeStruct((r, cdim), F32)] * 3,
        compiler_params=_cparams(1),
    )(w, g, m, v)


def _adamw_many(ws, gs, ms, vs, name, per_layer):
    n = len(ws)

    def body(*refs):
        for k in range(n):
            w, g, m, v = (refs[j * n + k][...] for j in range(4))
            outs = _adamw_math(w, g, m, v)
            for j in range(3):
                refs[(4 + j) * n + k][...] = outs[j]

    shapes = [jax.ShapeDtypeStruct(w.shape, F32) for w in ws]
    if per_layer:
        specs = [pl.BlockSpec((None,) + w.shape[1:], lambda l, nd=w.ndim: (l,) + (0,) * (nd - 1)) for w in ws]
        call = pl.pallas_call(body, name=name, grid=(N_LAYERS,), in_specs=specs * 4, out_specs=specs * 3,
                              out_shape=shapes * 3, compiler_params=_cparams(1))
    else:
        call = pl.pallas_call(body, name=name, out_shape=shapes * 3, compiler_params=_cparams())
    outs = call(*ws, *gs, *ms, *vs)
    return outs[0:n], outs[n:2 * n], outs[2 * n:3 * n]


def _cmul(ar, ai, br, bi):
    return ar * br - ai * bi, ar * bi + ai * br


def _discretise(a_re, a_im, log_dt, b_re, b_im):
    dt = jnp.exp(log_dt)
    mag = jnp.exp(a_re * dt)
    ab_re = mag * jnp.cos(a_im * dt)
    ab_im = mag * jnp.sin(a_im * dt)
    num_re = ab_re - 1.0
    num_im = ab_im
    den = a_re * a_re + a_im * a_im
    f_re = (num_re * a_re + num_im * a_im) / den
    f_im = (num_im * a_re - num_re * a_im) / den
    bb_re = f_re * b_re - f_im * b_im
    bb_im = f_re * b_im + f_im * b_re
    return ab_re, ab_im, bb_re, bb_im


def _disc_shapes():
    col = jax.ShapeDtypeStruct((1, N_STATES), F32)
    mat = jax.ShapeDtypeStruct((SSM_GROUP, N_STATES), F32)
    return col, mat


def _disc_fwd(a_re, a_im, log_dt, b_re, b_im, length):
    col, mat = _disc_shapes()
    tab = jax.ShapeDtypeStruct((SSM_CHUNKS, length, 2 * CH_S), F32)

    def body(ar, ai, ld, br, bi, o0, o1, o2, o3, tab_ref, rev_ref):
        outs = _discretise(ar[...], ai[...], ld[...], br[...], bi[...])
        for o, val in zip((o0, o1, o2, o3), outs):
            o[...] = val
        ab_re, ab_im = outs[0], outs[1]

        def step(j, carry):
            pr, pi = carry
            back = length - 1 - j
            for c in range(SSM_CHUNKS):
                lanes = slice(CH_S * c, CH_S * (c + 1))
                tab_ref[c, pl.ds(j, 1), 0:CH_S] = pr[:, lanes]
                tab_ref[c, pl.ds(j, 1), CH_S:2 * CH_S] = pi[:, lanes]
                rev_ref[c, pl.ds(back, 1), 0:CH_S] = pr[:, lanes]
                rev_ref[c, pl.ds(back, 1), CH_S:2 * CH_S] = -pi[:, lanes]
            return _cmul(pr, pi, ab_re, ab_im)

        lax.fori_loop(0, length, step, (ab_re, ab_im))

    return pl.pallas_call(body, name="ssm_discretise", out_shape=[col, col, mat, mat, tab, tab],
                          compiler_params=_cparams())(a_re, a_im, log_dt, b_re, b_im)


def _disc_bwd(a_re, a_im, log_dt, b_re, b_im, g_ab_re, g_ab_im, g_bb_re, g_bb_im):
    col, mat = _disc_shapes()

    def body(ar, ai, ld, br, bi, g0, g1, g2, g3, o0, o1, o2, o3, o4):
        _, vjp = jax.vjp(_discretise, ar[...], ai[...], ld[...], br[...], bi[...])
        grads = vjp((g0[...], g1[...], g2[...], g3[...]))
        for o, val in zip((o0, o1, o2, o3, o4), grads):
            o[...] = val

    return pl.pallas_call(body, name="ssm_discretise_bwd", out_shape=[col, col, col, mat, mat],
                          compiler_params=_cparams())(a_re, a_im, log_dt, b_re, b_im, g_ab_re, g_ab_im, g_bb_re, g_bb_im)


def _interleave_chunks(v):
    rows, width = v.shape
    return pltpu.einshape("cjw->jcw", v.reshape(SUBLANES, rows // SUBLANES, width)).reshape(rows, width)


def _time_order(v):
    rows, width = v.shape
    return pltpu.einshape("jcw->cjw", v.reshape(rows // SUBLANES, SUBLANES, width)).reshape(rows, width)


def _block_diag_in(bb):
    t = bb.reshape(SSM_GROUP, SSM_CHUNKS, 8, SSM_STATE)
    eye = jnp.eye(8, dtype=bb.dtype)
    return jnp.einsum("hjgp,gk->jghkp", t, eye).reshape(SSM_CHUNKS, CH_W, CH_S)


def _block_diag_in_t(d):
    t = d.reshape(SSM_CHUNKS, 8, SSM_GROUP, 8, SSM_STATE)
    return jnp.einsum("jghgp->hjgp", t).reshape(SSM_GROUP, N_STATES)


def _block_diag_out(c):
    t = c.reshape(SSM_CHUNKS, 8, SSM_GROUP, SSM_STATE)
    eye = jnp.eye(8, dtype=c.dtype)
    return jnp.einsum("jghp,gk->jgpkh", t, eye).reshape(SSM_CHUNKS, CH_S, CH_W)


def _block_diag_out_t(d):
    t = d.reshape(SSM_CHUNKS, 8, SSM_STATE, 8, SSM_GROUP)
    return jnp.einsum("jgpgh->jghp", t).reshape(SSM_GROUPS, SSM_GROUP, SSM_STATE)


def _head_ones():
    r = jnp.arange(ATTN_WIDTH) // HEAD_DIM
    return jnp.where(r[:, None] == r[None, :], 1.0 / HEAD_DIM, 0.0).astype(BF16)


def _in_proj(h, g1, w_in_l, qg, kg):
    s = h.shape[0]
    tm = _row_tile(s, 512)

    def body(h_ref, g_ref, w_ref, qg_ref, kg_ref, ones_ref, proj_ref, qkv_ref):
        x = h_ref[...]
        r = lax.rsqrt(jnp.mean(x * x, axis=-1, keepdims=True) + RMS_EPS)
        hn = (x * r * g_ref[...]).astype(BF16)
        for sh in range(N_CHIPS):
            proj_ref[:, IN_SHARD * sh:IN_SHARD * (sh + 1)] = _dot(hn, w_ref[sh])
        ones = ones_ref[...]
        q = proj_ref[:, 1024:1536]
        k = proj_ref[:, 1536:2048]
        rq = lax.rsqrt(_dot_hilo(q * q, ones) + RMS_EPS)
        rk = lax.rsqrt(_dot_hilo(k * k, ones) + RMS_EPS)
        qkv_ref[:, 0:512] = (q * rq * qg_ref[...] * ATTN_SCALE).astype(BF16)
        qkv_ref[:, 512:1024] = (k * rk * kg_ref[...]).astype(BF16)
        qkv_ref[:, 1024:1536] = proj_ref[:, 2048:2560].astype(BF16)

    full = lambda shape: pl.BlockSpec(shape, lambda i: (0,) * len(shape))
    return pl.pallas_call(
        body, name="in_proj", grid=(s // tm,),
        in_specs=[pl.BlockSpec((tm, D_MODEL), lambda i: (i, 0)), full((1, D_MODEL)),
                  full((N_CHIPS, D_MODEL, IN_SHARD)),
                  full((1, ATTN_WIDTH)), full((1, ATTN_WIDTH)), full((ATTN_WIDTH, ATTN_WIDTH))],
        out_specs=[pl.BlockSpec((tm, IN_COLS), lambda i: (i, 0)), pl.BlockSpec((tm, 3 * ATTN_WIDTH), lambda i: (i, 0))],
        out_shape=[jax.ShapeDtypeStruct((s, IN_COLS), F32), jax.ShapeDtypeStruct((s, 3 * ATTN_WIDTH), BF16)],
        compiler_params=_cparams(1),
    )(h, g1, w_in_l, qg, kg, _head_ones())


def _row_bcast(ref, k, lo):
    return jnp.broadcast_to(ref[pl.ds(k, 1), lo:lo + CH_S], (SUBLANES, CH_S))


def _chunk_scan(x_ref, tab_ref, carry_ref, length, reverse, tail=None):
    row = lax.broadcasted_iota(jnp.int32, (SUBLANES, CH_S), 0)
    one, full = (length - 1, 0) if reverse else (0, length - 1)
    ar, ai = _row_bcast(tab_ref, one, 0), _row_bcast(tab_ref, one, CH_S)
    fr, fi = _row_bcast(tab_ref, full, 0), _row_bcast(tab_ref, full, CH_S)
    step = lambda jj: (length - 1 - jj) if reverse else jj

    def local(jj, carry):
        cr, ci = carry
        r0 = pl.multiple_of(step(jj) * SUBLANES, SUBLANES)
        xr = x_ref[pl.ds(r0, SUBLANES), 0:CH_S] + (ar * cr - ai * ci)
        xi = x_ref[pl.ds(r0, SUBLANES), CH_S:2 * CH_S] + (ar * ci + ai * cr)
        x_ref[pl.ds(r0, SUBLANES), 0:CH_S] = xr
        x_ref[pl.ds(r0, SUBLANES), CH_S:2 * CH_S] = xi
        return xr, xi

    zero = jnp.zeros((SUBLANES, CH_S), F32)
    er, ei = lax.fori_loop(0, length, local, (zero, zero))

    first, shift = (SUBLANES - 1, SUBLANES - 1) if reverse else (0, 1)
    hr = jnp.where(row == first, carry_ref[:, 0:CH_S], 0.0)
    hi = jnp.where(row == first, carry_ref[:, CH_S:2 * CH_S], 0.0)
    sr, si = pltpu.roll(er, shift, 0), pltpu.roll(ei, shift, 0)
    for k in range(1, SUBLANES):
        tr, ti = pltpu.roll(hr, shift, 0), pltpu.roll(hi, shift, 0)
        here = row == ((SUBLANES - 1 - k) if reverse else k)
        hr, hi = (jnp.where(here, fr * tr - fi * ti + sr, hr), jnp.where(here, fr * ti + fi * tr + si, hi))
    last = 0 if reverse else SUBLANES - 1
    outr, outi = fr * hr - fi * hi + er, fr * hi + fi * hr + ei
    carry_ref[:, 0:CH_S] = jnp.broadcast_to(outr[last:last + 1, :], (SUBLANES, CH_S))
    carry_ref[:, CH_S:2 * CH_S] = jnp.broadcast_to(outi[last:last + 1, :], (SUBLANES, CH_S))

    def fix(jj, carry):
        j = step(jj)
        r0 = pl.multiple_of(j * SUBLANES, SUBLANES)
        pr, pi = _row_bcast(tab_ref, j, 0), _row_bcast(tab_ref, j, CH_S)
        xr = x_ref[pl.ds(r0, SUBLANES), 0:CH_S] + (pr * hr - pi * hi)
        xi = x_ref[pl.ds(r0, SUBLANES), CH_S:2 * CH_S] + (pr * hi + pi * hr)
        x_ref[pl.ds(r0, SUBLANES), 0:CH_S] = xr
        x_ref[pl.ds(r0, SUBLANES), CH_S:2 * CH_S] = xi
        if tail is None:
            return carry
        return tail(r0, xr, xi, carry)

    return fix, (hr, hi)


def _ssm_scan_fwd(proj, wb, tab, wc, gather=None, gather_bases=None):
    s = proj.shape[0]
    tm = _row_tile(s, SCAN_TILE)
    nt = s // tm
    length = tm // SUBLANES
    gather = [] if gather is None else gather
    ng = len(gather)

    def body(*refs):
        u_ref, wb_ref, tab_ref, wc_ref = refs[0:4]
        g_ins = refs[4:4 + ng]
        xs_ref, y_ref = refs[4 + ng:6 + ng]
        g_outs = refs[6 + ng:6 + 2 * ng]
        carry_ref = refs[6 + 2 * ng]
        sems = refs[7 + 2 * ng:]
        j, i = pl.program_id(0), pl.program_id(1)

        @pl.when(i == 0)
        def _():
            carry_ref[...] = jnp.zeros_like(carry_ref)

        if ng:
            @pl.when(jnp.logical_and(j == 0, i == 0))
            def _():
                _gather_start(g_ins, gather_bases, g_outs, sems)

        xs_ref[...] = _dot(_interleave_chunks(u_ref[...]).astype(BF16), wb_ref[...])
        fix, start = _chunk_scan(xs_ref, tab_ref, carry_ref, length, reverse=False)
        lax.fori_loop(0, length, fix, start, unroll=2)
        y_ref[...] = _time_order(_dot(xs_ref[...].astype(BF16), wc_ref[...]))

        if ng:
            @pl.when(jnp.logical_and(j == SSM_CHUNKS - 1, i == nt - 1))
            def _():
                _gather_finish(g_ins, gather_bases, g_outs, sems)

    outs = pl.pallas_call(
        body, name="ssm_scan_gather" if ng else "ssm_scan", grid=(SSM_CHUNKS, nt),
        in_specs=[pl.BlockSpec((tm, CH_W), lambda j, i: (i, j)),
                  pl.BlockSpec((None, CH_W, 2 * CH_S), lambda j, i: (j, 0, 0)),
                  pl.BlockSpec((None, length, 2 * CH_S), lambda j, i: (j, 0, 0)),
                  pl.BlockSpec((None, 2 * CH_S, CH_W), lambda j, i: (j, 0, 0))] + [ANY] * ng,
        out_specs=[pl.BlockSpec((None, tm, 2 * CH_S), lambda j, i: (j, i, 0)),
                   pl.BlockSpec((tm, CH_W), lambda j, i: (i, j))] + [ANY] * ng,
        out_shape=[jax.ShapeDtypeStruct((SSM_CHUNKS, s, 2 * CH_S), F32), jax.ShapeDtypeStruct((s, SSM_WIDTH), F32)]
        + _gather_outputs(gather),
        scratch_shapes=[pltpu.VMEM((SUBLANES, 2 * CH_S), F32)] + (_gather_sems(ng) if ng else []),
        compiler_params=_cparams(2),
    )(proj, wb, tab, wc, *gather)
    return outs[0], outs[1], (_gather_own(outs[2:], gather, gather_bases) if ng else [])


def _glu_forward(y, u, d, wg_ref, bg):
    yf = y + d * u
    z = _gelu(yf)
    zb = z.astype(BF16)
    zz = jnp.concatenate([_dot(zb, wg_ref[sh]) for sh in range(N_CHIPS)], axis=-1) + bg
    return yf, z, zz[:, 0:SSM_WIDTH], zz[:, SSM_WIDTH:2 * SSM_WIDTH]


def _ssm_glu_fwd(y, proj, d, w_glu_l, b_glu):
    s = y.shape[0]
    tm = _row_tile(s, 512)

    def body(y_ref, u_ref, gs_ref, d_ref, wg_ref, bg_ref, o_ref):
        _, _, val, gate = _glu_forward(y_ref[...], u_ref[...], d_ref[...], wg_ref, bg_ref[...])
        gs = gs_ref[...]
        o_ref[...] = val * _sigmoid(gate) * (gs * _sigmoid(gs))

    row = lambda i: (i, 0)
    return pl.pallas_call(
        body, name="ssm_glu", grid=(s // tm,),
        in_specs=[pl.BlockSpec((tm, SSM_WIDTH), row), pl.BlockSpec((tm, SSM_WIDTH), row),
                  pl.BlockSpec((tm, SSM_WIDTH), lambda i: (i, 1)), pl.BlockSpec((1, SSM_WIDTH), lambda i: (0, 0)),
                  pl.BlockSpec((N_CHIPS, SSM_WIDTH, ROW_SHARD), lambda i: (0, 0, 0)),
                  pl.BlockSpec((1, 2 * SSM_WIDTH), lambda i: (0, 0))],
        out_specs=pl.BlockSpec((tm, SSM_WIDTH), row),
        out_shape=jax.ShapeDtypeStruct((s, SSM_WIDTH), F32),
        compiler_params=_cparams(1),
    )(y, proj, proj, d, w_glu_l, b_glu)


def _tri(kind):
    r = jnp.arange(ATTN_BLOCK)
    if kind == "suffix_incl":
        m = r[:, None] >= r[None, :]
    else:
        m = r[:, None] < r[None, :]
    return jnp.concatenate([m, jnp.ones_like(m)], axis=1).astype(BF16)


def _head_masks():
    lane = lax.broadcasted_iota(jnp.int32, (1, 2 * HEAD_DIM), 1)
    return [lane < HEAD_DIM, lane >= HEAD_DIM]


def _chain_step(t, base, n_sub, first, q_ref, k_ref, tri_ref, l_scr, per_chain):
    tb = ATTN_BLOCK
    row = lax.broadcasted_iota(jnp.int32, (tb, tb), 0)
    col = lax.broadcasted_iota(jnp.int32, (tb, tb), 1)
    masks = _head_masks()
    blks = [base + a - t for a in range(n_sub)]
    r0s = [pl.multiple_of(jnp.maximum(blk, 0) * tb, tb) for blk in blks]
    zs = []
    for a in range(n_sub):
        kb = k_ref[pl.ds(r0s[a], tb), :]
        qa = q_ref[a * tb:(a + 1) * tb, :]
        for mask in masks:
            zs.append(_dot_nt(jnp.where(mask, qa, jnp.zeros_like(qa)), kb))
    parts = []
    for z in zs:
        ls = jnp.minimum(-z, 0.0) - jnp.log(1.0 + jnp.exp(-jnp.abs(z)))
        if first:
            ls = jnp.where(col < row, ls, 0.0)
        parts.append(_split_hilo(ls))
    tri = tri_ref[...]
    sums = [_dot(hi, tri) + _dot(lo, tri) for hi, lo in parts]
    top = None
    ws = []
    for c, (z, sm) in enumerate(zip(zs, sums)):
        if first:
            lsum = jnp.zeros((tb, tb), F32)
        else:
            lsum = l_scr[c] + jnp.where(blks[c // 2] >= 0, 0.0, -1e30)
        w = jnp.exp(z + sm[:, 0:tb] + lsum)
        if first:
            w = jnp.where(col < row, w, 0.0)
        ws.append(w)
        lsum = lsum + sm[:, tb:2 * tb]
        l_scr[c] = lsum
        top = lsum if top is None else jnp.maximum(top, lsum)
    for c, (z, w) in enumerate(zip(zs, ws)):
        per_chain(c // 2, c % 2, c, r0s[c // 2], z, w)
    return jnp.max(top)


def _chain_sweep(base, n_sub, q_ref, k_ref, tri_ref, l_scr, per_chain):
    top = _chain_step(0, base, n_sub, True, q_ref, k_ref, tri_ref, l_scr, functools.partial(per_chain, 0))

    def cond(carry):
        t, top = carry
        return jnp.logical_and(t <= base + n_sub - 1, top > EXP_ZERO)

    def step(carry):
        t, _ = carry
        return t + 1, _chain_step(t, base, n_sub, False, q_ref, k_ref, tri_ref, l_scr, functools.partial(per_chain, t))

    steps, _ = lax.while_loop(cond, step, (jnp.int32(1), top))
    return steps


ATTN_SUB_FWD = 8
ATTN_SUB_BWD = 4


def _attn_fwd(qkv, proj, gather=None, gather_bases=None):
    s = qkv.shape[0]
    tb = ATTN_BLOCK
    n_sub = min(ATTN_SUB_FWD, s // tb)
    tq = n_sub * tb
    n_hp = ATTN_WIDTH // (2 * HEAD_DIM)
    gather = [] if gather is None else gather
    ng = len(gather)

    def body(*refs):
        q_ref, k_ref, v_ref, g_ref, tri_ref = refs[0:5]
        g_ins = refs[5:5 + ng]
        o_ref, ya_ref = refs[5 + ng:7 + ng]
        g_outs = refs[7 + ng:7 + 2 * ng]
        l_scr = refs[7 + 2 * ng]
        sems = refs[8 + 2 * ng:]
        i = pl.program_id(1)
        masks = _head_masks()
        o_ref[...] = jnp.zeros_like(o_ref)

        if ng:
            @pl.when(jnp.logical_and(pl.program_id(0) == 0, i == 0))
            def _():
                _gather_start(g_ins, gather_bases, g_outs, sems)

        def per_chain(t, a, h, c, r0, z, w):
            vb = v_ref[pl.ds(r0, tb), :]
            vb = jnp.where(masks[h], vb, jnp.zeros_like(vb))
            o_ref[a * tb:(a + 1) * tb, :] += _dot(w.astype(BF16), vb)

        _chain_sweep(i * n_sub, n_sub, q_ref, k_ref, tri_ref, l_scr, per_chain)
        g = g_ref[...]
        ya_ref[...] = o_ref[...] * (g * _sigmoid(g))

        if ng:
            @pl.when(jnp.logical_and(pl.program_id(0) == n_hp - 1, i == s // tq - 1))
            def _():
                _gather_finish(g_ins, gather_bases, g_outs, sems)

    hp_blk = lambda off: pl.BlockSpec((tq, 2 * HEAD_DIM), lambda hp, i: (i, off + hp))
    res = lambda off: pl.BlockSpec((s, 2 * HEAD_DIM), lambda hp, i: (0, off + hp))
    outs = pl.pallas_call(
        body, name="attn_fwd_gather" if ng else "attn_fwd", grid=(n_hp, s // tq),
        in_specs=[hp_blk(0), res(4), res(8), hp_blk(20), pl.BlockSpec((tb, 2 * tb), lambda hp, i: (0, 0))] + [ANY] * ng,
        out_specs=[hp_blk(0), hp_blk(0)] + [ANY] * ng,
        out_shape=[jax.ShapeDtypeStruct((s, ATTN_WIDTH), F32)] * 2 + _gather_outputs(gather),
        scratch_shapes=[pltpu.VMEM((2 * n_sub, tb, tb), F32)] + (_gather_sems(ng) if ng else []),
        compiler_params=_cparams(2),
    )(qkv, qkv, qkv, proj, _tri("suffix_incl"), *gather)
    return outs[0], outs[1], (_gather_own(outs[2:], gather, gather_bases) if ng else [])


def _rms_rows(x, g):
    r = lax.rsqrt(jnp.mean(x * x, axis=-1, keepdims=True) + RMS_EPS)
    return r, x * r * g


def _ple_forward(h1, p, g2, wpg_ref, wpp_ref):
    r2, hn2 = _rms_rows(h1, g2)
    hb = hn2.astype(BF16)
    gpre = _dot(hb[:, 0:ROW_SHARD], wpg_ref[0])
    for sh in range(1, N_CHIPS):
        gpre = gpre + _dot(hb[:, ROW_SHARD * sh:ROW_SHARD * (sh + 1)], wpg_ref[sh])
    gate = _sigmoid(gpre)
    pb = p.astype(BF16)
    pp = jnp.concatenate([_dot(pb, wpp_ref[sh]) for sh in range(N_CHIPS)], axis=-1)
    return r2, hb, gate, pp


def _colsum8(a):
    t = a.shape[0]
    return a.reshape(t // SUBLANES, SUBLANES, a.shape[1]).sum(axis=0)


def _sq_err_grad(y, target):
    e = y - target
    sq = _colsum8(e * e)
    part = sq[:, 0:128]
    for b in range(1, D_MODEL // 128):
        part = part + sq[:, 128 * b:128 * (b + 1)]
    return e / D_MODEL, part


def _out_ple(h, ys, ya, p, g2, w_out_l, w_pg_l, w_pp_l, target=None):
    s = h.shape[0]
    tm = _row_tile(s, 512)
    last = target is not None

    def body(*refs):
        h_ref, ys_ref, ya_ref, p_ref, g_ref, wo_ref, wpg_ref, wpp_ref = refs[0:8]
        h1_ref, h2_ref = refs[8 + last], refs[9 + last]
        ysb = ys_ref[...].astype(BF16)
        yab = ya_ref[...].astype(BF16)
        h1 = h_ref[...]
        for sh, src in enumerate((ysb[:, 0:ROW_SHARD], ysb[:, ROW_SHARD:], yab[:, 0:ROW_SHARD], yab[:, ROW_SHARD:])):
            h1 = h1 + _dot(src, wo_ref[sh])
        _, _, gate, pp = _ple_forward(h1, p_ref[...], g_ref[...], wpg_ref, wpp_ref)
        h1_ref[...] = h1
        h2 = h1 + gate * pp
        if last:
            acc_ref = refs[11]

            @pl.when(pl.program_id(0) == 0)
            def _():
                acc_ref[...] = jnp.zeros_like(acc_ref)

            h2_ref[...], part = _sq_err_grad(h2, refs[8][...])
            acc_ref[...] += part
        else:
            h2_ref[...] = h2

    row = lambda i: (i, 0)
    big = pl.BlockSpec((tm, D_MODEL), row)
    wspec = lambda r, cdim: pl.BlockSpec((N_CHIPS, r, cdim), lambda i: (0, 0, 0))
    acc = pl.BlockSpec((SUBLANES, 128), lambda i: (0, 0))
    return pl.pallas_call(
        body, name="out_ple_loss" if last else "out_ple", grid=(s // tm,),
        in_specs=[big, pl.BlockSpec((tm, SSM_WIDTH), row), pl.BlockSpec((tm, ATTN_WIDTH), row),
                  pl.BlockSpec((tm, PLE_DIM), row), pl.BlockSpec((1, D_MODEL), lambda i: (0, 0)),
                  wspec(ROW_SHARD, D_MODEL), wspec(ROW_SHARD, D_MODEL), wspec(PLE_DIM, ROW_SHARD)] + [big] * last,
        out_specs=[big] * 2 + [acc] * last,
        out_shape=[jax.ShapeDtypeStruct((s, D_MODEL), F32)] * 2 + [jax.ShapeDtypeStruct((SUBLANES, 128), F32)] * last,
        compiler_params=_cparams(1),
    )(h, ys, ya, p, g2, w_out_l, w_pg_l, w_pp_l, *([target] if last else []))


def _rms_bwd(x, r, g, dy):
    gdy = g * dy
    dx = r * gdy - x * (r * r * r) * jnp.mean(x * gdy, axis=-1, keepdims=True)
    return dx, x * r * dy


def _out_ple_bwd(dh2, h1, p, g2, w_out_l, w_pg_l, w_pp_l):
    s = h1.shape[0]
    tm = _row_tile(s, 512)

    def body(dh2_ref, h1_ref, p_ref, g_ref, wo_ref, wpg_ref, wpp_ref,
             dh1_ref, dmix_ref, hn_ref, dgp_ref, dpp_ref, dh1b_ref, dg_ref):
        @pl.when(pl.program_id(0) == 0)
        def _():
            dg_ref[...] = jnp.zeros_like(dg_ref)

        h1 = h1_ref[...]
        dh2 = dh2_ref[...]
        g2v = g_ref[...]
        r2, hb, gate, pp = _ple_forward(h1, p_ref[...], g2v, wpg_ref, wpp_ref)
        dgp = (dh2 * pp) * gate * (1.0 - gate)
        dgpb = dgp.astype(BF16)
        dhn = jnp.concatenate([_dot_nt(dgpb, wpg_ref[sh]) for sh in range(N_CHIPS)], axis=-1)
        dx, dgrow = _rms_bwd(h1, r2, g2v, dhn)
        dh1 = dh2 + dx
        dh1b = dh1.astype(BF16)
        dh1_ref[...] = dh1
        dh1b_ref[...] = dh1b
        hn_ref[...] = hb
        dgp_ref[...] = dgpb
        dpp_ref[...] = (dh2 * gate).astype(BF16)
        dg_ref[...] += _colsum8(dgrow)
        for sh in range(N_CHIPS):
            dmix_ref[:, ROW_SHARD * sh:ROW_SHARD * (sh + 1)] = _dot_nt(dh1b, wo_ref[sh])

    row = lambda i: (i, 0)
    wspec = lambda r, cdim: pl.BlockSpec((N_CHIPS, r, cdim), lambda i: (0, 0, 0))
    big = pl.BlockSpec((tm, D_MODEL), row)
    return pl.pallas_call(
        body, name="out_ple_bwd", grid=(s // tm,),
        in_specs=[big, big, pl.BlockSpec((tm, PLE_DIM), row), pl.BlockSpec((1, D_MODEL), lambda i: (0, 0)),
                  wspec(ROW_SHARD, D_MODEL), wspec(ROW_SHARD, D_MODEL), wspec(PLE_DIM, ROW_SHARD)],
        out_specs=[big] * 6 + [pl.BlockSpec((SUBLANES, D_MODEL), lambda i: (0, 0))],
        out_shape=[jax.ShapeDtypeStruct((s, D_MODEL), F32)] * 2 + [jax.ShapeDtypeStruct((s, D_MODEL), BF16)] * 4
        + [jax.ShapeDtypeStruct((SUBLANES, D_MODEL), F32)],
        compiler_params=_cparams(1),
    )(dh2, h1, p, g2, w_out_l, w_pg_l, w_pp_l)


def _tn_matmul(a, b, n_blocks, block_a, name, into=None, first_block=0, total_blocks=None):
    s = a.shape[0]
    tk = _row_tile(s, 1024)
    nk = s // tk
    total_blocks = n_blocks if total_blocks is None else total_blocks
    ka, nb = a.shape[1], b.shape[1]
    if block_a:
        ka //= n_blocks
    else:
        nb //= n_blocks

    def body(*refs):
        a_ref, b_ref, o_ref, acc_ref = refs[0], refs[1], refs[-2], refs[-1]

        @pl.when(pl.program_id(0) == 0)
        def _():
            acc_ref[...] = jnp.zeros_like(acc_ref)

        at = a_ref[...].astype(BF16).T
        bb = b_ref[...].astype(BF16)
        for sh in range(n_blocks):
            if block_a:
                acc_ref[sh] += _dot(at[ka * sh:ka * (sh + 1), :], bb)
            else:
                acc_ref[sh] += _dot(at, bb[:, nb * sh:nb * (sh + 1)])

        @pl.when(pl.program_id(0) == nk - 1)
        def _():
            o_ref[...] = acc_ref[...].astype(BF16)

    in_specs = [pl.BlockSpec((tk, a.shape[1]), lambda i: (i, 0)), pl.BlockSpec((tk, b.shape[1]), lambda i: (i, 0))]
    operands = [a, b]
    aliases = {}
    if into is not None:
        in_specs.append(ANY)
        operands.append(into)
        aliases = {2: 0}
    return pl.pallas_call(
        body, name=name, grid=(nk,),
        in_specs=in_specs,
        out_specs=pl.BlockSpec((n_blocks, ka, nb), lambda i: (first_block // n_blocks, 0, 0)),
        out_shape=jax.ShapeDtypeStruct((total_blocks, ka, nb), BF16),
        scratch_shapes=[pltpu.VMEM((n_blocks, ka, nb), F32)],
        input_output_aliases=aliases,
        compiler_params=_cparams(1),
    )(*operands)


def _attn_bwd(qkv, o, proj, dmix, scatter=None):
    scatter = [] if scatter is None else scatter
    nsc = len(scatter)
    s = qkv.shape[0]
    tb = ATTN_BLOCK
    nq = s // tb
    n_sub = min(ATTN_SUB_BWD, nq)
    tq = n_sub * tb
    n_chain = 2 * n_sub

    def body(*refs):
        q_ref, k_ref, v_ref, o_ref, g_ref, dya_ref, tri_s_ref, tri_p_ref = refs[0:8]
        sc_ins = refs[8:8 + nsc]
        dq_ref, dk_ref, dv_ref, dg_ref = refs[8 + nsc:12 + nsc]
        sc_outs = refs[12 + nsc:12 + 2 * nsc]
        do_scr, l_scr, g_scr, s_scr, w_scr = refs[12 + 2 * nsc:17 + 2 * nsc]
        sc_sems = refs[17 + 2 * nsc:]
        i = pl.program_id(1)
        base = i * n_sub

        if nsc:
            @pl.when(jnp.logical_and(pl.program_id(0) == 0, i == 0))
            def _():
                _scatter_start(sc_ins, sc_outs, sc_sems)

        @pl.when(i == 0)
        def _():
            dk_ref[...] = jnp.zeros_like(dk_ref)
            dv_ref[...] = jnp.zeros_like(dv_ref)

        g = g_ref[...]
        sg = _sigmoid(g)
        dya = dya_ref[...]
        do_scr[...] = (dya * (g * sg)).astype(BF16)
        dg_ref[...] = dya * o_ref[...] * (sg * (1.0 + g * (1.0 - sg)))
        dq_ref[...] = jnp.zeros_like(dq_ref)
        g_scr[...] = jnp.zeros_like(g_scr)
        masks = _head_masks()

        def keep(t, a, h, c, r0, z, w):
            s_scr[c, t] = _sigmoid(z).astype(BF16)
            w_scr[c, t] = w.astype(BF16)

        steps = _chain_sweep(base, n_sub, q_ref, k_ref, tri_s_ref, l_scr, keep)
        row = lax.broadcasted_iota(jnp.int32, (tb, tb), 0)
        col = lax.broadcasted_iota(jnp.int32, (tb, tb), 1)

        def back(it, carry):
            t = steps - 1 - it
            r0s = [pl.multiple_of(jnp.maximum(base + a - t, 0) * tb, tb) for a in range(n_sub)]
            qhs, dohs, khs, gws = [], [], [], []
            for a in range(n_sub):
                kb = k_ref[pl.ds(r0s[a], tb), :]
                vb = v_ref[pl.ds(r0s[a], tb), :]
                qa = q_ref[a * tb:(a + 1) * tb, :]
                doa = do_scr[a * tb:(a + 1) * tb, :]
                for h, mask in enumerate(masks):
                    qhs.append(jnp.where(mask, qa, jnp.zeros_like(qa)))
                    khs.append(jnp.where(mask, kb, jnp.zeros_like(kb)))
                    dohs.append(jnp.where(mask, doa, jnp.zeros_like(doa)))
                    gws.append(w_scr[2 * a + h, t].astype(F32) * _dot_nt(dohs[-1], vb))
            parts = [_split_hilo(gw) for gw in gws]
            tri = tri_p_ref[...]
            sums = [_dot(hi, tri) + _dot(lo, tri) for hi, lo in parts]
            dzs = []
            for c, (gw, sm) in enumerate(zip(gws, sums)):
                gsum = g_scr[c]
                dz = gw - (gw + sm[:, 0:tb] + gsum) * s_scr[c, t].astype(F32)
                dz = jnp.where(col < row + t * tb, dz, 0.0)
                g_scr[c] = gsum + sm[:, tb:2 * tb]
                dzs.append(dz.astype(BF16))
            for c, dzb in enumerate(dzs):
                a = c // 2
                dk_ref[pl.ds(r0s[a], tb), :] += _dot_tn(dzb, qhs[c])
                dv_ref[pl.ds(r0s[a], tb), :] += _dot_tn(w_scr[c, t], dohs[c])
                dq_ref[a * tb:(a + 1) * tb, :] += _dot(dzb, khs[c])
            return carry

        lax.fori_loop(0, steps, back, 0)

        if nsc:
            @pl.when(jnp.logical_and(pl.program_id(0) == n_hp - 1, i == s // tq - 1))
            def _():
                _scatter_finish(sc_ins, sc_outs, sc_sems)

    n_hp = ATTN_WIDTH // (2 * HEAD_DIM)
    hp_blk = lambda off: pl.BlockSpec((tq, 2 * HEAD_DIM), lambda hp, i: (i, off + hp))
    res = lambda off: pl.BlockSpec((s, 2 * HEAD_DIM), lambda hp, i: (0, off + hp))
    tri = pl.BlockSpec((tb, 2 * tb), lambda hp, i: (0, 0))
    outs = pl.pallas_call(
        body, name="attn_bwd_scatter" if nsc else "attn_bwd", grid=(n_hp, s // tq),
        in_specs=[hp_blk(0), res(4), res(8), hp_blk(0), hp_blk(20), hp_blk(4), tri, tri] + [ANY] * nsc,
        out_specs=[hp_blk(0), res(0), res(0), hp_blk(0)] + [ANY] * nsc,
        out_shape=[jax.ShapeDtypeStruct((s, ATTN_WIDTH), F32)] * 4 + [jax.ShapeDtypeStruct(a.shape, a.dtype) for a in scatter],
        scratch_shapes=[pltpu.VMEM((tq, 2 * HEAD_DIM), BF16), pltpu.VMEM((n_chain, tb, tb), F32),
                        pltpu.VMEM((n_chain, tb, tb), F32), pltpu.VMEM((n_chain, nq, tb, tb), BF16),
                        pltpu.VMEM((n_chain, nq, tb, tb), BF16)] + (_scatter_sems(nsc) if nsc else []),
        compiler_params=_cparams(2),
    )(qkv, qkv, qkv, o, proj, dmix, _tri("suffix_incl"), _tri("prefix_strict"), *scatter)
    return outs[0], outs[1], outs[2], outs[3], (_scatter_own(outs[4:], scatter) if nsc else [])


def _ssm_glu_bwd(dmix, y, proj, d, w_glu_l, b_glu):
    s = y.shape[0]
    tm = _row_tile(s, 512)

    def body(dys_ref, y_ref, u_ref, gs_ref, d_ref, wg_ref, bg_ref,
             dyf_ref, du_ref, dgs_ref, z_ref, dzz_ref, dd_ref, db_ref):
        @pl.when(pl.program_id(0) == 0)
        def _():
            dd_ref[...] = jnp.zeros_like(dd_ref)
            db_ref[...] = jnp.zeros_like(db_ref)

        u = u_ref[...]
        dv = d_ref[...]
        yf, z, val, gate = _glu_forward(y_ref[...], u, dv, wg_ref, bg_ref[...])
        gs = gs_ref[...]
        sgs = _sigmoid(gs)
        sgate = _sigmoid(gate)
        dys = dys_ref[...]
        dgv = dys * (gs * sgs)
        dgs_ref[...] = dys * (val * sgate) * (sgs * (1.0 + gs * (1.0 - sgs)))
        dzz = jnp.concatenate([dgv * sgate, dgv * val * sgate * (1.0 - sgate)], axis=-1)
        dzzb = dzz.astype(BF16)
        dz = _dot_nt(dzzb[:, 0:ROW_SHARD], wg_ref[0])
        for sh in range(1, N_CHIPS):
            dz = dz + _dot_nt(dzzb[:, ROW_SHARD * sh:ROW_SHARD * (sh + 1)], wg_ref[sh])
        dyf = dz * _gelu_grad(yf)
        dyf_ref[...] = dyf
        du_ref[...] = dyf * dv
        z_ref[...] = z.astype(BF16)
        dzz_ref[...] = dzzb
        dd_ref[...] += _colsum8(dyf * u)
        db_ref[...] += _colsum8(dzz)

    row = lambda i: (i, 0)
    half = pl.BlockSpec((tm, SSM_WIDTH), row)
    return pl.pallas_call(
        body, name="ssm_glu_bwd", grid=(s // tm,),
        in_specs=[half, half, half, pl.BlockSpec((tm, SSM_WIDTH), lambda i: (i, 1)),
                  pl.BlockSpec((1, SSM_WIDTH), lambda i: (0, 0)),
                  pl.BlockSpec((N_CHIPS, SSM_WIDTH, ROW_SHARD), lambda i: (0, 0, 0)),
                  pl.BlockSpec((1, 2 * SSM_WIDTH), lambda i: (0, 0))],
        out_specs=[half, half, half, half, pl.BlockSpec((tm, 2 * SSM_WIDTH), row),
                   pl.BlockSpec((SUBLANES, SSM_WIDTH), lambda i: (0, 0)),
                   pl.BlockSpec((SUBLANES, 2 * SSM_WIDTH), lambda i: (0, 0))],
        out_shape=[jax.ShapeDtypeStruct((s, SSM_WIDTH), F32)] * 3
        + [jax.ShapeDtypeStruct((s, SSM_WIDTH), BF16), jax.ShapeDtypeStruct((s, 2 * SSM_WIDTH), BF16),
           jax.ShapeDtypeStruct((SUBLANES, SSM_WIDTH), F32), jax.ShapeDtypeStruct((SUBLANES, 2 * SSM_WIDTH), F32)],
        compiler_params=_cparams(1),
    )(dmix, y, proj, proj, d, w_glu_l, b_glu)


def _ssm_scan_bwd(dyf, xs, proj, wct, tab_rev, wbt):
    s = dyf.shape[0]
    tm = _row_tile(s, SCAN_TILE)
    nt = s // tm
    length = tm // SUBLANES

    def body(dy_ref, xs_ref, u_ref, wct_ref, tab_ref, wbt_ref, du_ref, dwc_ref, dwb_ref, da_ref, lam_ref, carry_ref):
        @pl.when(pl.program_id(1) == 0)
        def _():
            carry_ref[...] = jnp.zeros_like(carry_ref)
            dwc_ref[...] = jnp.zeros_like(dwc_ref)
            dwb_ref[...] = jnp.zeros_like(dwb_ref)
            da_ref[...] = jnp.zeros_like(da_ref)

        dyp = _interleave_chunks(dy_ref[...]).astype(BF16)
        up = _interleave_chunks(u_ref[...]).astype(BF16)
        lam_ref[...] = _dot(dyp, wct_ref[...])

        def tail(r0, lr, li, carry):
            er, ei, dar, dai = carry
            xr = xs_ref[pl.ds(r0, SUBLANES), 0:CH_S]
            xi = xs_ref[pl.ds(r0, SUBLANES), CH_S:2 * CH_S]
            return lr, li, dar + (xr * er + xi * ei), dai + (xr * ei - xi * er)

        fix, (gr, gi) = _chunk_scan(lam_ref, tab_ref, carry_ref, length, reverse=True, tail=tail)
        zero = jnp.zeros((SUBLANES, CH_S), F32)
        _, _, dar, dai = lax.fori_loop(0, length, fix, (gr, gi, zero, zero), unroll=2)
        da_ref[:, 0:CH_S] += dar
        da_ref[:, CH_S:2 * CH_S] += dai
        lamb = lam_ref[...].astype(BF16)
        du_ref[...] = _time_order(_dot(lamb, wbt_ref[...]))
        dwc_ref[...] += _dot_tn(xs_ref[...].astype(BF16), dyp)
        dwb_ref[...] += _dot_tn(up, lamb)

    rev = lambda j, i: (nt - 1 - i, j)
    return pl.pallas_call(
        body, name="ssm_scan_bwd", grid=(SSM_CHUNKS, nt),
        in_specs=[pl.BlockSpec((tm, CH_W), rev),
                  pl.BlockSpec((None, tm, 2 * CH_S), lambda j, i: (j, nt - 1 - i, 0)),
                  pl.BlockSpec((tm, CH_W), rev),
                  pl.BlockSpec((None, CH_W, 2 * CH_S), lambda j, i: (j, 0, 0)),
                  pl.BlockSpec((None, length, 2 * CH_S), lambda j, i: (j, 0, 0)),
                  pl.BlockSpec((None, 2 * CH_S, CH_W), lambda j, i: (j, 0, 0))],
        out_specs=[pl.BlockSpec((tm, CH_W), rev),
                   pl.BlockSpec((None, 2 * CH_S, CH_W), lambda j, i: (j, 0, 0)),
                   pl.BlockSpec((None, CH_W, 2 * CH_S), lambda j, i: (j, 0, 0)),
                   pl.BlockSpec((None, SUBLANES, 2 * CH_S), lambda j, i: (j, 0, 0))],
        out_shape=[jax.ShapeDtypeStruct((s, SSM_WIDTH), F32),
                   jax.ShapeDtypeStruct((SSM_CHUNKS, 2 * CH_S, CH_W), F32),
                   jax.ShapeDtypeStruct((SSM_CHUNKS, CH_W, 2 * CH_S), F32),
                   jax.ShapeDtypeStruct((SSM_CHUNKS, SUBLANES, 2 * CH_S), F32)],
        scratch_shapes=[pltpu.VMEM((tm, 2 * CH_S), F32), pltpu.VMEM((SUBLANES, 2 * CH_S), F32)],
        compiler_params=_cparams(2),
    )(dyf, xs, proj, wct, tab_rev, wbt)


def _in_proj_bwd(h, g1, w_in_l, qg, kg, proj, du_a, du_b, dgs, dq, dk, dv, dga, dh1):
    s = h.shape[0]
    tm = _row_tile(s, 256)

    def body(h_ref, g_ref, w_ref, qg_ref, kg_ref, ones_ref, q_ref, k_ref, dua_ref, dub_ref, dgs_ref, dq_ref, dk_ref,
             dv_ref, dga_ref, dh1_ref, dh_ref, hn_ref, dp_ref, dg1_ref, dqg_ref, dkg_ref):
        @pl.when(pl.program_id(0) == 0)
        def _():
            dg1_ref[...] = jnp.zeros_like(dg1_ref)
            dqg_ref[...] = jnp.zeros_like(dqg_ref)
            dkg_ref[...] = jnp.zeros_like(dkg_ref)

        ones = ones_ref[...]

        def head_norm_bwd(x, gain, dy):
            r = lax.rsqrt(_dot_hilo(x * x, ones) + RMS_EPS)
            gdy = gain * dy
            dx = r * gdy - x * (r * r * r) * _dot_hilo(x * gdy, ones)
            return dx, x * r * dy

        dqr, dqg_rows = head_norm_bwd(q_ref[...], qg_ref[...], dq_ref[...] * ATTN_SCALE)
        dkr, dkg_rows = head_norm_bwd(k_ref[...], kg_ref[...], dk_ref[...])
        dqg_ref[...] += _colsum8(dqg_rows)
        dkg_ref[...] += _colsum8(dkg_rows)
        dp_ref[:, 0:512] = (dua_ref[...] + dub_ref[...]).astype(BF16)
        dp_ref[:, 512:1024] = dgs_ref[...].astype(BF16)
        dp_ref[:, 1024:1536] = dqr.astype(BF16)
        dp_ref[:, 1536:2048] = dkr.astype(BF16)
        dp_ref[:, 2048:2560] = dv_ref[...].astype(BF16)
        dp_ref[:, 2560:3072] = dga_ref[...].astype(BF16)
        dhn = _dot_nt(dp_ref[:, 0:IN_SHARD], w_ref[0])
        for sh in range(1, N_CHIPS):
            dhn = dhn + _dot_nt(dp_ref[:, IN_SHARD * sh:IN_SHARD * (sh + 1)], w_ref[sh])
        x = h_ref[...]
        gv = g_ref[...]
        r, hn = _rms_rows(x, gv)
        dx, dg_rows = _rms_bwd(x, r, gv, dhn)
        dh_ref[...] = dh1_ref[...] + dx
        hn_ref[...] = hn.astype(BF16)
        dg1_ref[...] += _colsum8(dg_rows)

    row = lambda i: (i, 0)
    full = lambda shape: pl.BlockSpec(shape, lambda i: (0,) * len(shape))
    big = pl.BlockSpec((tm, D_MODEL), row)
    half = pl.BlockSpec((tm, 512), row)
    return pl.pallas_call(
        body, name="in_proj_bwd", grid=(s // tm,),
        in_specs=[big, full((1, D_MODEL)), full((N_CHIPS, D_MODEL, IN_SHARD)),
                  full((1, ATTN_WIDTH)), full((1, ATTN_WIDTH)), full((ATTN_WIDTH, ATTN_WIDTH)),
                  pl.BlockSpec((tm, 512), lambda i: (i, 2)), pl.BlockSpec((tm, 512), lambda i: (i, 3)),
                  half, half, half, half, half, half, half, big],
        out_specs=[big, big, pl.BlockSpec((tm, IN_COLS), row), pl.BlockSpec((SUBLANES, D_MODEL), lambda i: (0, 0)),
                   pl.BlockSpec((SUBLANES, ATTN_WIDTH), lambda i: (0, 0)), pl.BlockSpec((SUBLANES, ATTN_WIDTH), lambda i: (0, 0))],
        out_shape=[jax.ShapeDtypeStruct((s, D_MODEL), F32), jax.ShapeDtypeStruct((s, D_MODEL), BF16),
                   jax.ShapeDtypeStruct((s, IN_COLS), BF16), jax.ShapeDtypeStruct((SUBLANES, D_MODEL), F32),
                   jax.ShapeDtypeStruct((SUBLANES, ATTN_WIDTH), F32), jax.ShapeDtypeStruct((SUBLANES, ATTN_WIDTH), F32)],
        compiler_params=_cparams(1),
    )(h, g1, w_in_l, qg, kg, _head_ones(), proj, proj, du_a, du_b, dgs, dq, dk, dv, dga, dh1)


SMALL_NAMES = ("mix_norm_g", "ssm_a_re", "ssm_a_im", "ssm_log_dt", "ssm_b_re", "ssm_b_im", "ssm_c_re", "ssm_c_im",
               "ssm_d", "ssm_b_glu", "q_norm_g", "k_norm_g", "ple_norm_g")
SMALL_4D = ("ssm_b_re", "ssm_b_im", "ssm_c_re", "ssm_c_im")
BIG_NAMES = ("w_in", "ssm_w_glu", "w_out", "w_ple_gate", "w_ple_proj")


def _ssm_setup(sm, layer, length):
    col = lambda a: a[layer].reshape(1, N_STATES)
    a_re, a_im = col(sm["ssm_a_re"]), col(sm["ssm_a_im"])
    log_dt = jnp.repeat(sm["ssm_log_dt"][layer], SSM_STATE).reshape(1, N_STATES)
    b_re = sm["ssm_b_re"][layer].reshape(N_STATES, SSM_GROUP).T
    b_im = sm["ssm_b_im"][layer].reshape(N_STATES, SSM_GROUP).T
    disc_in = (a_re, a_im, log_dt, b_re, b_im)
    _, _, bb_re, bb_im, tab, tab_rev = _disc_fwd(*disc_in, length)
    wb = jnp.concatenate([_block_diag_in(bb_re), _block_diag_in(bb_im)], axis=-1)
    wc = jnp.concatenate([_block_diag_out(sm["ssm_c_re"][layer]), -_block_diag_out(sm["ssm_c_im"][layer])], axis=1)
    return dict(disc_in=disc_in, wb=wb.astype(BF16), wbt=wb.transpose(0, 2, 1).astype(BF16),
                wc=wc.astype(BF16), wct=wc.transpose(0, 2, 1).astype(BF16),
                tab=tab, tab_rev=tab_rev)


def _whole_blocks(names, gathered):
    return {n: g.reshape(N_CHIPS, 2 * g.shape[2], g.shape[3]) for n, g in zip(names, gathered)}


def _local_step(x, p, target, sm, w_in0, local=None, gathered=None, layer1_hook=None):
    wg = [dict(w_in=w_in0), {}] if gathered is None else gathered
    tile8 = lambda a: jnp.tile(a, ATTN_WIDTH // HEAD_DIM).reshape(1, ATTN_WIDTH)
    saved = []
    h = x
    for l in range(N_LAYERS):
        ssm = _ssm_setup(sm, l, _row_tile(x.shape[0], SCAN_TILE) // SUBLANES)
        g1 = sm["mix_norm_g"][l].reshape(1, D_MODEL)
        g2 = sm["ple_norm_g"][l].reshape(1, D_MODEL)
        qg, kg = tile8(sm["q_norm_g"][l]), tile8(sm["k_norm_g"][l])
        dsk = sm["ssm_d"][l].reshape(1, SSM_WIDTH)
        bgl = sm["ssm_b_glu"][l].reshape(1, 2 * SSM_WIDTH)
        proj, qkv = _in_proj(h, g1, wg[l]["w_in"], qg, kg)
        if l == 0 and local is not None:
            rest = BIG_NAMES[1:]
            xs, y, got = _ssm_scan_fwd(proj, ssm["wb"], ssm["tab"], ssm["wc"], [local[n] for n in rest], [0] * len(rest))
            wg[0].update(_whole_blocks(rest, got))
            ys = _ssm_glu_fwd(y, proj, dsk, wg[0]["ssm_w_glu"], bgl)
            o, ya, got = _attn_fwd(qkv, proj, [local[n] for n in BIG_NAMES], [2] * len(BIG_NAMES))
            wg[1].update(_whole_blocks(BIG_NAMES, got))
        else:
            xs, y, _ = _ssm_scan_fwd(proj, ssm["wb"], ssm["tab"], ssm["wc"])
            ys = _ssm_glu_fwd(y, proj, dsk, wg[l]["ssm_w_glu"], bgl)
            o, ya, _ = _attn_fwd(qkv, proj)
        tail = (target,) if l == N_LAYERS - 1 else ()
        h1, h2, *sq = _out_ple(h, ys, ya, p[l], g2, wg[l]["w_out"], wg[l]["w_ple_gate"], wg[l]["w_ple_proj"], *tail)
        saved.append(dict(ssm=ssm, g1=g1, g2=g2, qg=qg, kg=kg, dsk=dsk, bgl=bgl, h=h, proj=proj, qkv=qkv, xs=xs, y=y,
                          ys=ys, o=o, ya=ya, h1=h1))
        h = h2
    dh = h
    loss = 0.5 * jnp.sum(sq[0]) / D_MODEL

    gbig = [{} for _ in range(N_LAYERS)]
    scattered = []
    gsm = {n: [None] * N_LAYERS for n in SMALL_NAMES}
    for l in reversed(range(N_LAYERS)):
        sv = saved[l]
        ssm = sv["ssm"]
        dh1, dmix, hn2b, dgpb, dppb, dh1b, dg2 = _out_ple_bwd(dh, sv["h1"], p[l], sv["g2"], wg[l]["w_out"],
                                                              wg[l]["w_ple_gate"], wg[l]["w_ple_proj"])
        gsm["ple_norm_g"][l] = dg2.sum(0)
        gbig[l]["w_ple_proj"] = _tn_matmul(p[l], dppb, N_CHIPS, False, "dw_ple_proj")
        gbig[l]["w_ple_gate"] = _tn_matmul(hn2b, dgpb, N_CHIPS, True, "dw_ple_gate")
        dwo = _tn_matmul(sv["ys"], dh1b, 2, True, "dw_out_ssm", None, 0, N_CHIPS)
        gbig[l]["w_out"] = _tn_matmul(sv["ya"], dh1b, 2, True, "dw_out_attn", dwo, 2, N_CHIPS)
        if l == 0 and layer1_hook is not None:
            dqs, dkn, dv, dga, scattered = _attn_bwd(sv["qkv"], sv["o"], sv["proj"], dmix, layer1_hook(gbig[1]))
        else:
            dqs, dkn, dv, dga, _ = _attn_bwd(sv["qkv"], sv["o"], sv["proj"], dmix)
        dyf, du_a, dgs, zb, dzzb, dd, dbg = _ssm_glu_bwd(dmix, sv["y"], sv["proj"], sv["dsk"], wg[l]["ssm_w_glu"], sv["bgl"])
        gsm["ssm_d"][l] = dd.sum(0).reshape(SSM_GROUPS, SSM_GROUP)
        gsm["ssm_b_glu"][l] = dbg.sum(0)
        gbig[l]["ssm_w_glu"] = _tn_matmul(zb, dzzb, N_CHIPS, False, "dw_glu")
        du_b, dwc, dwb, da = _ssm_scan_bwd(dyf, sv["xs"], sv["proj"], ssm["wct"], ssm["tab_rev"], ssm["wbt"])
        gsm["ssm_c_re"][l] = _block_diag_out_t(dwc[:, 0:CH_S, :])
        gsm["ssm_c_im"][l] = -_block_diag_out_t(dwc[:, CH_S:, :])
        da = da.sum(1)
        g_ab_re = da[:, 0:CH_S].reshape(1, N_STATES)
        g_ab_im = da[:, CH_S:].reshape(1, N_STATES)
        g_bb_re = _block_diag_in_t(dwb[:, :, 0:CH_S])
        g_bb_im = _block_diag_in_t(dwb[:, :, CH_S:])
        d_are, d_aim, d_ldt, d_bre, d_bim = _disc_bwd(*ssm["disc_in"], g_ab_re, g_ab_im, g_bb_re, g_bb_im)
        gsm["ssm_a_re"][l] = d_are.reshape(SSM_GROUPS, SSM_STATE)
        gsm["ssm_a_im"][l] = d_aim.reshape(SSM_GROUPS, SSM_STATE)
        gsm["ssm_log_dt"][l] = d_ldt.reshape(SSM_GROUPS, SSM_STATE).sum(1)
        gsm["ssm_b_re"][l] = d_bre.T.reshape(SSM_GROUPS, SSM_STATE, SSM_GROUP)
        gsm["ssm_b_im"][l] = d_bim.T.reshape(SSM_GROUPS, SSM_STATE, SSM_GROUP)
        dh, hnb, dprojb, dg1, dqg, dkg = _in_proj_bwd(sv["h"], sv["g1"], wg[l]["w_in"], sv["qg"], sv["kg"], sv["proj"],
                                                      du_a, du_b, dgs, dqs, dkn, dv, dga, dh1)
        gsm["mix_norm_g"][l] = dg1.sum(0)
        gsm["q_norm_g"][l] = dqg.sum(0).reshape(-1, HEAD_DIM).sum(0)
        gsm["k_norm_g"][l] = dkg.sum(0).reshape(-1, HEAD_DIM).sum(0)
        gbig[l]["w_in"] = _tn_matmul(hnb, dprojb, N_CHIPS, False, "dw_in")
    gsm = {n: jnp.stack(v, 0) for n, v in gsm.items()}
    return loss, dh, gbig, gsm, scattered


_SMALL_PAD = 8 * 8 * 128


def _pack_small(d, extra):
    flat = jnp.concatenate([d[n].reshape(-1) for n in SMALL_NAMES] + [jnp.stack(extra)])
    n = flat.shape[0]
    padded = -(-n // _SMALL_PAD) * _SMALL_PAD
    return jnp.pad(flat, (0, padded - n))


def _unpack_small(flat, like):
    out, off = {}, 0
    for n in SMALL_NAMES:
        size = like[n].size
        out[n] = flat[off:off + size].reshape(like[n].shape)
        off += size
    return out, flat[off:]


def _half_views(arrs):
    return [a.reshape(a.shape[0], 2, a.shape[1] // 2, a.shape[2]) for a in arrs]


def _chip_sums(views, out_dtypes, tag):
    recv = _sibling_push(views, "grad_push_" + tag)
    return [_add_my_half(v, r, dt, "grad_half_add") for v, r, dt in zip(views, recv, out_dtypes)]


def kernel(x, p, mix_norm_g, w_in, ssm_a_re, ssm_a_im, ssm_log_dt, ssm_b_re, ssm_b_im, ssm_c_re, ssm_c_im, ssm_d, ssm_w_glu, ssm_b_glu, q_norm_g, k_norm_g, w_out, ple_norm_g, w_ple_gate, w_ple_proj, loss_target, m_mix_norm_g, m_w_in, m_ssm_a_re, m_ssm_a_im, m_ssm_log_dt, m_ssm_b_re, m_ssm_b_im, m_ssm_c_re, m_ssm_c_im, m_ssm_d, m_ssm_w_glu, m_ssm_b_glu, m_q_norm_g, m_k_norm_g, m_w_out, m_ple_norm_g, m_w_ple_gate, m_w_ple_proj, v_mix_norm_g, v_w_in, v_ssm_a_re, v_ssm_a_im, v_ssm_log_dt, v_ssm_b_re, v_ssm_b_im, v_ssm_c_re, v_ssm_c_im, v_ssm_d, v_ssm_w_glu, v_ssm_b_glu, v_q_norm_g, v_k_norm_g, v_w_out, v_ple_norm_g, v_w_ple_gate, v_w_ple_proj):
    args = dict(locals())
    names = ("mix_norm_g", "w_in", "ssm_a_re", "ssm_a_im", "ssm_log_dt", "ssm_b_re", "ssm_b_im", "ssm_c_re", "ssm_c_im",
             "ssm_d", "ssm_w_glu", "ssm_b_glu", "q_norm_g", "k_norm_g", "w_out", "ple_norm_g", "w_ple_gate", "w_ple_proj")
    w = {n: args[n] for n in names}
    m = {n: args["m_" + n] for n in names}
    v = {n: args["v_" + n] for n in names}

    local = {n: w[n].astype(BF16).reshape(2 * N_LAYERS, w[n].shape[1] // 2, w[n].shape[2]) for n in BIG_NAMES}
    w_in0 = _chip_gather([local["w_in"]], "w_in_gather")[0].reshape(N_CHIPS, D_MODEL, IN_SHARD)
    sm = {n: w[n] for n in SMALL_NAMES}
    nb = len(BIG_NAMES)
    loss, dx, gbig, gsm, got1 = _local_step(
        x[0], p[:, 0], loss_target[0], sm, w_in0, local,
        layer1_hook=lambda g1: _chip_sums(_half_views([g1[n] for n in BIG_NAMES]), [BF16] * nb, "layer1"))

    small = _pack_small(gsm, [loss]).reshape(N_CHIPS, 2, SUBLANES, -1)
    chip0 = _chip_sums(_half_views([gbig[0][n] for n in BIG_NAMES]) + [small], [BF16] * nb + [F32], "layer0")
    got0 = _chip_scatter(chip0, "grad_chip_scatter")
    tot1 = [_sum4(a, "grad_chip_sum") for a in got1]
    tot0 = [_sum4(a, "grad_chip_sum") for a in got0]
    pieces = [(t, k, (l,)) for l, tots in enumerate((tot0[:nb], tot1)) for k, t in enumerate(tots)] + [(tot0[nb], nb, ())]
    joined = _sibling_join(pieces, [(N_LAYERS, 2) + t.shape for t in tot1] + [(2,) + tot0[nb].shape], "grad_sibling_join")
    small_all = _chip_gather([joined[nb]], "small_grad_gather")[0]
    small_tot = small_all.reshape(-1)
    g = {n: j.reshape(w[n].shape) for n, j in zip(BIG_NAMES, joined)}
    g_small, rest = _unpack_small(small_tot, sm)
    g.update(g_small)
    loss = rest[0]

    delta, new_m, new_v = {}, {}, {}
    for n in BIG_NAMES:
        lanes = w[n].shape[-1]
        outs = _adamw(_as_rows(w[n], lanes), _as_rows(g[n], lanes), _as_rows(m[n], lanes), _as_rows(v[n], lanes), "adamw_" + n)
        delta[n], new_m[n], new_v[n] = [o.reshape(w[n].shape) for o in outs]
    for group, per_layer in ((SMALL_4D, True), (tuple(n for n in SMALL_NAMES if n not in SMALL_4D), False)):
        outs = _adamw_many(*[[d[n] for n in group] for d in (w, g, m, v)], "adamw_small_4d" if per_layer else "adamw_small", per_layer)
        for d, o in zip((delta, new_m, new_v), outs):
            d.update(zip(group, o))

    return (loss, dx[None], *[g[n] for n in names], *[delta[n] for n in names],
            *[new_m[n] for n in names], *[new_v[n] for n in names])
```

```python
import functools
import math

import jax
import jax.numpy as jnp
from jax import lax
from jax.experimental import pallas as pl
from jax.experimental.pallas import tpu as pltpu

F32 = jnp.float32
BF16 = jnp.bfloat16

D_MODEL = 1024
N_LAYERS = 2
N_CHIPS = 4
IN_COLS = 3072
IN_SHARD = IN_COLS // N_CHIPS
SSM_WIDTH = 512
SSM_GROUP = 16
SSM_GROUPS = 32
SSM_STATE = 64
N_STATES = SSM_GROUPS * SSM_STATE
SSM_CHUNKS = 4
CH_W = SSM_WIDTH // SSM_CHUNKS
CH_S = N_STATES // SSM_CHUNKS
ATTN_WIDTH = 512
HEAD_DIM = 64
PLE_DIM = 256
ROW_SHARD = 256
RMS_EPS = 1e-6
ATTN_SCALE = HEAD_DIM ** -0.5
ATTN_BLOCK = 128
EXP_ZERO = -87.5
SUBLANES = 8
SCAN_TILE = 1024
V7X_VMEM_LIMIT = 52 * 1024 * 1024

ADAM_LR = 0.001
ADAM_B1 = 0.9
ADAM_B2 = 0.999
ADAM_EPS = 1e-08
ADAM_WD = 0.01
ADAM_STEP = 10

MESH = pl.DeviceIdType.MESH
ANY = pl.BlockSpec(memory_space=pl.ANY)


def _cparams(n_grid=0, parallel=0):
    sem = tuple(["parallel"] * parallel + ["arbitrary"] * (n_grid - parallel))
    return pltpu.CompilerParams(dimension_semantics=sem, vmem_limit_bytes=V7X_VMEM_LIMIT)


def _dot(a, b):
    return jnp.dot(a, b, preferred_element_type=F32)


def _dot_nt(a, b):
    return lax.dot_general(a, b, (((1,), (1,)), ((), ())), preferred_element_type=F32)


def _dot_tn(a, b):
    return lax.dot_general(a, b, (((0,), (0,)), ((), ())), preferred_element_type=F32)


def _split_hilo(a):
    hi = a.astype(BF16)
    lo = (a - hi.astype(F32)).astype(BF16)
    return hi, lo


def _dot_hilo(a, b):
    hi, lo = _split_hilo(a)
    return _dot(hi, b) + _dot(lo, b)


def _sigmoid(x):
    return 0.5 * (jnp.tanh(0.5 * x) + 1.0)


_GELU_C = math.sqrt(2.0 / math.pi)


def _gelu(x):
    return 0.5 * x * (1.0 + jnp.tanh(_GELU_C * (x + 0.044715 * (x * x * x))))


def _gelu_grad(x):
    t = jnp.tanh(_GELU_C * (x + 0.044715 * (x * x * x)))
    return 0.5 * (1.0 + t) + 0.5 * x * (1.0 - t * t) * (_GELU_C * (1.0 + 3.0 * 0.044715 * (x * x)))


def _row_tile(s, want):
    for t in range(min(s, want), 7, -1):
        if s % t == 0 and t % SUBLANES == 0:
            return t
    return s


def _coords():
    return lax.axis_index("x"), lax.axis_index("y"), lax.axis_index("c")


def _other_chips(x, y):
    return [(1 - x, y), (x, 1 - y), (1 - x, 1 - y)]


def _remote(src, dst, send_sem, recv_sem, dev):
    return pltpu.make_async_remote_copy(src_ref=src, dst_ref=dst, send_sem=send_sem, recv_sem=recv_sem,
                                        device_id=dev, device_id_type=MESH)


def _set_block(buf, block, index):
    return lax.dynamic_update_index_in_dim(buf, block, index, 0)


def _gather_sems(n):
    return [pltpu.SemaphoreType.DMA((3 * n,)) for _ in range(4)]


def _gather_copies(ins, bases, outs, sems):
    send_sems, recv_sems, fwd_send, fwd_recv = sems
    x, y, c = _coords()
    me_chip = 2 * x + y
    sibling = (x, y, 1 - c)
    first, landed, passed, from_sibling = [], [], [], []
    for k in range(len(ins)):
        for j, (cx, cy) in enumerate(_other_chips(x, y)):
            i = 3 * k + j
            first.append(_remote(ins[k].at[bases[k] + c], outs[k].at[me_chip, c], send_sems.at[i], recv_sems.at[i], (cx, cy, c)))
            blk = outs[k].at[2 * cx + cy, c]
            landed.append(_remote(blk, blk, send_sems.at[i], recv_sems.at[i], (cx, cy, c)))
            passed.append(_remote(blk, blk, fwd_send.at[i], fwd_recv.at[i], sibling))
            blk = outs[k].at[2 * cx + cy, 1 - c]
            from_sibling.append(_remote(blk, blk, fwd_send.at[i], fwd_recv.at[i], sibling))
    return first, landed, passed, from_sibling


def _gather_start(ins, bases, outs, sems):
    for cp in _gather_copies(ins, bases, outs, sems)[0]:
        cp.start()


def _gather_finish(ins, bases, outs, sems):
    first, landed, passed, from_sibling = _gather_copies(ins, bases, outs, sems)
    for arrived, forward in zip(landed, passed):
        arrived.wait_recv()
        forward.start()
    for cp in from_sibling:
        cp.wait_recv()
    for cp in first + passed:
        cp.wait_send()


def _gather_outputs(arrs):
    return [jax.ShapeDtypeStruct((N_CHIPS, 2) + a.shape[1:], a.dtype) for a in arrs]


def _gather_own(outs, arrs, bases):
    me_chip = 2 * lax.axis_index("x") + lax.axis_index("y")
    return [_set_block(o, lax.slice_in_dim(a, b, b + 2, axis=0), me_chip) for o, a, b in zip(outs, arrs, bases)]


def _chip_gather(arrs, name, bases=None):
    n = len(arrs)
    bases = [0] * n if bases is None else bases

    def body(*refs):
        ins, outs, sems = refs[:n], refs[n:2 * n], refs[2 * n:]
        _gather_start(ins, bases, outs, sems)
        _gather_finish(ins, bases, outs, sems)

    outs = pl.pallas_call(
        body, name=name, out_shape=_gather_outputs(arrs),
        in_specs=[ANY] * n, out_specs=[ANY] * n, scratch_shapes=_gather_sems(n),
    )(*arrs)
    return _gather_own(outs, arrs, bases)


def _sibling_push(arrs, name):
    n = len(arrs)

    def body(*refs):
        ins, outs = refs[:n], refs[n:2 * n]
        send_sems, recv_sems = refs[2 * n:]
        x, y, c = _coords()
        cps = [_remote(ins[k].at[pl.ds(0, N_CHIPS), 1 - c], outs[k], send_sems.at[k], recv_sems.at[k], (x, y, 1 - c))
               for k in range(n)]
        for cp in cps:
            cp.start()
        for cp in cps:
            cp.wait_recv()
        for cp in cps:
            cp.wait_send()

    return pl.pallas_call(
        body, name=name,
        out_shape=[jax.ShapeDtypeStruct((a.shape[0],) + a.shape[2:], a.dtype) for a in arrs],
        in_specs=[ANY] * n, out_specs=[ANY] * n,
        scratch_shapes=[pltpu.SemaphoreType.DMA((n,)), pltpu.SemaphoreType.DMA((n,))],
    )(*arrs)


def _sibling_join(pieces, out_shapes, name):
    n = len(pieces)
    no = len(out_shapes)

    def body(*refs):
        ins, outs = refs[:n], refs[n:n + no]
        send_sems, recv_sems = refs[n + no:]
        x, y, c = _coords()
        sibling = (x, y, 1 - c)
        cps = [_remote(ins[k], outs[o].at[lead + (c,)], send_sems.at[k], recv_sems.at[k], sibling)
               for k, (_, o, lead) in enumerate(pieces)]
        for cp in cps:
            cp.start()
        for k, (_, o, lead) in enumerate(pieces):
            blk = outs[o].at[lead + (1 - c,)]
            _remote(blk, blk, send_sems.at[k], recv_sems.at[k], sibling).wait_recv()
        for cp in cps:
            cp.wait_send()

    outs = pl.pallas_call(
        body, name=name,
        out_shape=[jax.ShapeDtypeStruct(sh, F32) for sh in out_shapes],
        in_specs=[ANY] * n, out_specs=[ANY] * no,
        scratch_shapes=[pltpu.SemaphoreType.DMA((n,)), pltpu.SemaphoreType.DMA((n,))],
    )(*[a for a, _, _ in pieces])
    outs = list(outs)
    c = lax.axis_index("c")
    for a, o, lead in pieces:
        block = a.reshape((1,) * (len(lead) + 1) + a.shape)
        outs[o] = lax.dynamic_update_slice(outs[o], block, lead + (c,) + (0,) * a.ndim)
    return outs


def _scatter_sems(n):
    return [pltpu.SemaphoreType.DMA((3 * n,)), pltpu.SemaphoreType.DMA((3 * n,))]


def _scatter_copies(ins, outs, sems):
    send_sems, recv_sems = sems
    x, y, c = _coords()
    me_chip = 2 * x + y
    sends, arrivals = [], []
    for k in range(len(ins)):
        for j, (cx, cy) in enumerate(_other_chips(x, y)):
            i = 3 * k + j
            sends.append(_remote(ins[k].at[2 * cx + cy], outs[k].at[me_chip], send_sems.at[i], recv_sems.at[i], (cx, cy, c)))
            blk = outs[k].at[2 * cx + cy]
            arrivals.append(_remote(blk, blk, send_sems.at[i], recv_sems.at[i], (cx, cy, c)))
    return sends, arrivals


def _scatter_start(ins, outs, sems):
    for cp in _scatter_copies(ins, outs, sems)[0]:
        cp.start()


def _scatter_finish(ins, outs, sems):
    sends, arrivals = _scatter_copies(ins, outs, sems)
    for cp in arrivals:
        cp.wait_recv()
    for cp in sends:
        cp.wait_send()


def _scatter_own(outs, arrs):
    me_chip = 2 * lax.axis_index("x") + lax.axis_index("y")
    return [_set_block(o, lax.dynamic_index_in_dim(a, me_chip, 0, keepdims=False), me_chip) for o, a in zip(outs, arrs)]


def _chip_scatter(arrs, name):
    n = len(arrs)

    def body(*refs):
        ins, outs, sems = refs[:n], refs[n:2 * n], refs[2 * n:]
        _scatter_start(ins, outs, sems)
        _scatter_finish(ins, outs, sems)

    outs = pl.pallas_call(
        body, name=name,
        out_shape=[jax.ShapeDtypeStruct(a.shape, a.dtype) for a in arrs],
        in_specs=[ANY] * n, out_specs=[ANY] * n, scratch_shapes=_scatter_sems(n),
    )(*arrs)
    return _scatter_own(outs, arrs)


def _as_rows(a, lanes):
    return a.reshape(-1, lanes)


def _add_my_half(v, recv, out_dtype, name):
    n_sh, _, h, cdim = v.shape
    tr = _row_tile(h, 512)

    def body(c_ref, a_ref, b_ref, o_ref):
        o_ref[...] = (a_ref[...].astype(F32) + b_ref[...].astype(F32)).astype(out_dtype)

    c = lax.axis_index("c").astype(jnp.int32).reshape(1)
    return pl.pallas_call(
        body, name=name,
        grid_spec=pltpu.PrefetchScalarGridSpec(
            num_scalar_prefetch=1, grid=(n_sh, h // tr),
            in_specs=[pl.BlockSpec((None, None, tr, cdim), lambda sh, i, c_ref: (sh, c_ref[0], i, 0)),
                      pl.BlockSpec((None, tr, cdim), lambda sh, i, c_ref: (sh, i, 0))],
            out_specs=pl.BlockSpec((None, tr, cdim), lambda sh, i, c_ref: (sh, i, 0))),
        out_shape=jax.ShapeDtypeStruct((n_sh, h, cdim), out_dtype),
        compiler_params=_cparams(2),
    )(c, v, recv)


def _sum4(parts, name):
    _, r, cdim = parts.shape
    tr = _row_tile(r, 512)

    def body(p_ref, o_ref):
        acc = p_ref[0].astype(F32) + p_ref[1].astype(F32)
        acc = acc + p_ref[2].astype(F32)
        o_ref[...] = acc + p_ref[3].astype(F32)

    return pl.pallas_call(
        body, name=name, grid=(r // tr,),
        in_specs=[pl.BlockSpec((N_CHIPS, tr, cdim), lambda i: (0, i, 0))],
        out_specs=pl.BlockSpec((tr, cdim), lambda i: (i, 0)),
        out_shape=jax.ShapeDtypeStruct((r, cdim), F32),
        compiler_params=_cparams(1),
    )(parts)


def _adamw_math(w, g, m, v):
    c1 = 1.0 - ADAM_B1 ** ADAM_STEP
    c2 = 1.0 - ADAM_B2 ** ADAM_STEP
    nm = ADAM_B1 * m + (1.0 - ADAM_B1) * g
    nv = ADAM_B2 * v + (1.0 - ADAM_B2) * (g * g)
    delta = -ADAM_LR * ((nm / c1) / (jnp.sqrt(nv / c2) + ADAM_EPS) + ADAM_WD * w)
    return delta, nm, nv


def _adamw(w, g, m, v, name):
    r, cdim = w.shape
    tr = _row_tile(r, 256)

    def body(w_ref, g_ref, m_ref, v_ref, d_ref, nm_ref, nv_ref):
        d_ref[...], nm_ref[...], nv_ref[...] = _adamw_math(w_ref[...], g_ref[...], m_ref[...], v_ref[...])

    spec = pl.BlockSpec((tr, cdim), lambda i: (i, 0))
    return pl.pallas_call(
        body, name=name, grid=(r // tr,),
        in_specs=[spec] * 4, out_specs=[spec] * 3,
        out_shape=[jax.ShapeDtypeStruct((r, cdim), F32)] * 3,
        compiler_params=_cparams(1),
    )(w, g, m, v)


def _adamw_many(ws, gs, ms, vs, name, per_layer):
    n = len(ws)

    def body(*refs):
        for k in range(n):
            w, g, m, v = (refs[j * n + k][...] for j in range(4))
            outs = _adamw_math(w, g, m, v)
            for j in range(3):
                refs[(4 + j) * n + k][...] = outs[j]

    shapes = [jax.ShapeDtypeStruct(w.shape, F32) for w in ws]
    if per_layer:
        specs = [pl.BlockSpec((None,) + w.shape[1:], lambda l, nd=w.ndim: (l,) + (0,) * (nd - 1)) for w in ws]
        call = pl.pallas_call(body, name=name, grid=(N_LAYERS,), in_specs=specs * 4, out_specs=specs * 3,
                              out_shape=shapes * 3, compiler_params=_cparams(1))
    else:
        call = pl.pallas_call(body, name=name, out_shape=shapes * 3, compiler_params=_cparams())
    outs = call(*ws, *gs, *ms, *vs)
    return outs[0:n], outs[n:2 * n], outs[2 * n:3 * n]


def _cmul(ar, ai, br, bi):
    return ar * br - ai * bi, ar * bi + ai * br


def _discretise(a_re, a_im, log_dt, b_re, b_im):
    dt = jnp.exp(log_dt)
    mag = jnp.exp(a_re * dt)
    ab_re = mag * jnp.cos(a_im * dt)
    ab_im = mag * jnp.sin(a_im * dt)
    num_re = ab_re - 1.0
    num_im = ab_im
    den = a_re * a_re + a_im * a_im
    f_re = (num_re * a_re + num_im * a_im) / den
    f_im = (num_im * a_re - num_re * a_im) / den
    bb_re = f_re * b_re - f_im * b_im
    bb_im = f_re * b_im + f_im * b_re
    return ab_re, ab_im, bb_re, bb_im


def _disc_shapes():
    col = jax.ShapeDtypeStruct((1, N_STATES), F32)
    mat = jax.ShapeDtypeStruct((SSM_GROUP, N_STATES), F32)
    return col, mat


def _disc_fwd(a_re, a_im, log_dt, b_re, b_im, length):
    col, mat = _disc_shapes()
    tab = jax.ShapeDtypeStruct((SSM_CHUNKS, length, 2 * CH_S), F32)

    def body(ar, ai, ld, br, bi, o0, o1, o2, o3, tab_ref, rev_ref):
        outs = _discretise(ar[...], ai[...], ld[...], br[...], bi[...])
        for o, val in zip((o0, o1, o2, o3), outs):
            o[...] = val
        ab_re, ab_im = outs[0], outs[1]

        def step(j, carry):
            pr, pi = carry
            back = length - 1 - j
            for c in range(SSM_CHUNKS):
                lanes = slice(CH_S * c, CH_S * (c + 1))
                tab_ref[c, pl.ds(j, 1), 0:CH_S] = pr[:, lanes]
                tab_ref[c, pl.ds(j, 1), CH_S:2 * CH_S] = pi[:, lanes]
                rev_ref[c, pl.ds(back, 1), 0:CH_S] = pr[:, lanes]
                rev_ref[c, pl.ds(back, 1), CH_S:2 * CH_S] = -pi[:, lanes]
            return _cmul(pr, pi, ab_re, ab_im)

        lax.fori_loop(0, length, step, (ab_re, ab_im))

    return pl.pallas_call(body, name="ssm_discretise", out_shape=[col, col, mat, mat, tab, tab],
                          compiler_params=_cparams())(a_re, a_im, log_dt, b_re, b_im)


def _disc_bwd(a_re, a_im, log_dt, b_re, b_im, g_ab_re, g_ab_im, g_bb_re, g_bb_im):
    col, mat = _disc_shapes()

    def body(ar, ai, ld, br, bi, g0, g1, g2, g3, o0, o1, o2, o3, o4):
        _, vjp = jax.vjp(_discretise, ar[...], ai[...], ld[...], br[...], bi[...])
        grads = vjp((g0[...], g1[...], g2[...], g3[...]))
        for o, val in zip((o0, o1, o2, o3, o4), grads):
            o[...] = val

    return pl.pallas_call(body, name="ssm_discretise_bwd", out_shape=[col, col, col, mat, mat],
                          compiler_params=_cparams())(a_re, a_im, log_dt, b_re, b_im, g_ab_re, g_ab_im, g_bb_re, g_bb_im)


def _interleave_chunks(v):
    rows, width = v.shape
    return pltpu.einshape("cjw->jcw", v.reshape(SUBLANES, rows // SUBLANES, width)).reshape(rows, width)


def _time_order(v):
    rows, width = v.shape
    return pltpu.einshape("jcw->cjw", v.reshape(rows // SUBLANES, SUBLANES, width)).reshape(rows, width)


def _block_diag_in(bb):
    t = bb.reshape(SSM_GROUP, SSM_CHUNKS, 8, SSM_STATE)
    eye = jnp.eye(8, dtype=bb.dtype)
    return jnp.einsum("hjgp,gk->jghkp", t, eye).reshape(SSM_CHUNKS, CH_W, CH_S)


def _block_diag_in_t(d):
    t = d.reshape(SSM_CHUNKS, 8, SSM_GROUP, 8, SSM_STATE)
    return jnp.einsum("jghgp->hjgp", t).reshape(SSM_GROUP, N_STATES)


def _block_diag_out(c):
    t = c.reshape(SSM_CHUNKS, 8, SSM_GROUP, SSM_STATE)
    eye = jnp.eye(8, dtype=c.dtype)
    return jnp.einsum("jghp,gk->jgpkh", t, eye).reshape(SSM_CHUNKS, CH_S, CH_W)


def _block_diag_out_t(d):
    t = d.reshape(SSM_CHUNKS, 8, SSM_STATE, 8, SSM_GROUP)
    return jnp.einsum("jgpgh->jghp", t).reshape(SSM_GROUPS, SSM_GROUP, SSM_STATE)


def _head_ones():
    r = jnp.arange(ATTN_WIDTH) // HEAD_DIM
    return jnp.where(r[:, None] == r[None, :], 1.0 / HEAD_DIM, 0.0).astype(BF16)


def _in_proj(h, g1, w_in_l, qg, kg):
    s = h.shape[0]
    tm = _row_tile(s, 512)

    def body(h_ref, g_ref, w_ref, qg_ref, kg_ref, ones_ref, proj_ref, qkv_ref):
        x = h_ref[...]
        r = lax.rsqrt(jnp.mean(x * x, axis=-1, keepdims=True) + RMS_EPS)
        hn = (x * r * g_ref[...]).astype(BF16)
        for sh in range(N_CHIPS):
            proj_ref[:, IN_SHARD * sh:IN_SHARD * (sh + 1)] = _dot(hn, w_ref[sh])
        ones = ones_ref[...]
        q = proj_ref[:, 1024:1536]
        k = proj_ref[:, 1536:2048]
        rq = lax.rsqrt(_dot_hilo(q * q, ones) + RMS_EPS)
        rk = lax.rsqrt(_dot_hilo(k * k, ones) + RMS_EPS)
        qkv_ref[:, 0:512] = (q * rq * qg_ref[...] * ATTN_SCALE).astype(BF16)
        qkv_ref[:, 512:1024] = (k * rk * kg_ref[...]).astype(BF16)
        qkv_ref[:, 1024:1536] = proj_ref[:, 2048:2560].astype(BF16)

    full = lambda shape: pl.BlockSpec(shape, lambda i: (0,) * len(shape))
    return pl.pallas_call(
        body, name="in_proj", grid=(s // tm,),
        in_specs=[pl.BlockSpec((tm, D_MODEL), lambda i: (i, 0)), full((1, D_MODEL)),
                  full((N_CHIPS, D_MODEL, IN_SHARD)),
                  full((1, ATTN_WIDTH)), full((1, ATTN_WIDTH)), full((ATTN_WIDTH, ATTN_WIDTH))],
        out_specs=[pl.BlockSpec((tm, IN_COLS), lambda i: (i, 0)), pl.BlockSpec((tm, 3 * ATTN_WIDTH), lambda i: (i, 0))],
        out_shape=[jax.ShapeDtypeStruct((s, IN_COLS), F32), jax.ShapeDtypeStruct((s, 3 * ATTN_WIDTH), BF16)],
        compiler_params=_cparams(1),
    )(h, g1, w_in_l, qg, kg, _head_ones())


def _row_bcast(ref, k, lo):
    return jnp.broadcast_to(ref[pl.ds(k, 1), lo:lo + CH_S], (SUBLANES, CH_S))


def _chunk_scan(x_ref, tab_ref, carry_ref, length, reverse, tail=None):
    row = lax.broadcasted_iota(jnp.int32, (SUBLANES, CH_S), 0)
    one, full = (length - 1, 0) if reverse else (0, length - 1)
    ar, ai = _row_bcast(tab_ref, one, 0), _row_bcast(tab_ref, one, CH_S)
    fr, fi = _row_bcast(tab_ref, full, 0), _row_bcast(tab_ref, full, CH_S)
    step = lambda jj: (length - 1 - jj) if reverse else jj

    def local(jj, carry):
        cr, ci = carry
        r0 = pl.multiple_of(step(jj) * SUBLANES, SUBLANES)
        xr = x_ref[pl.ds(r0, SUBLANES), 0:CH_S] + (ar * cr - ai * ci)
        xi = x_ref[pl.ds(r0, SUBLANES), CH_S:2 * CH_S] + (ar * ci + ai * cr)
        x_ref[pl.ds(r0, SUBLANES), 0:CH_S] = xr
        x_ref[pl.ds(r0, SUBLANES), CH_S:2 * CH_S] = xi
        return xr, xi

    zero = jnp.zeros((SUBLANES, CH_S), F32)
    er, ei = lax.fori_loop(0, length, local, (zero, zero))

    first, shift = (SUBLANES - 1, SUBLANES - 1) if reverse else (0, 1)
    hr = jnp.where(row == first, carry_ref[:, 0:CH_S], 0.0)
    hi = jnp.where(row == first, carry_ref[:, CH_S:2 * CH_S], 0.0)
    sr, si = pltpu.roll(er, shift, 0), pltpu.roll(ei, shift, 0)
    for k in range(1, SUBLANES):
        tr, ti = pltpu.roll(hr, shift, 0), pltpu.roll(hi, shift, 0)
        here = row == ((SUBLANES - 1 - k) if reverse else k)
        hr, hi = (jnp.where(here, fr * tr - fi * ti + sr, hr), jnp.where(here, fr * ti + fi * tr + si, hi))
    last = 0 if reverse else SUBLANES - 1
    outr, outi = fr * hr - fi * hi + er, fr * hi + fi * hr + ei
    carry_ref[:, 0:CH_S] = jnp.broadcast_to(outr[last:last + 1, :], (SUBLANES, CH_S))
    carry_ref[:, CH_S:2 * CH_S] = jnp.broadcast_to(outi[last:last + 1, :], (SUBLANES, CH_S))

    def fix(jj, carry):
        j = step(jj)
        r0 = pl.multiple_of(j * SUBLANES, SUBLANES)
        pr, pi = _row_bcast(tab_ref, j, 0), _row_bcast(tab_ref, j, CH_S)
        xr = x_ref[pl.ds(r0, SUBLANES), 0:CH_S] + (pr * hr - pi * hi)
        xi = x_ref[pl.ds(r0, SUBLANES), CH_S:2 * CH_S] + (pr * hi + pi * hr)
        x_ref[pl.ds(r0, SUBLANES), 0:CH_S] = xr
        x_ref[pl.ds(r0, SUBLANES), CH_S:2 * CH_S] = xi
        if tail is None:
            return carry
        return tail(r0, xr, xi, carry)

    return fix, (hr, hi)


def _ssm_scan_fwd(proj, wb, tab, wc, gather=None, gather_bases=None):
    s = proj.shape[0]
    tm = _row_tile(s, SCAN_TILE)
    nt = s // tm
    length = tm // SUBLANES
    gather = [] if gather is None else gather
    ng = len(gather)

    def body(*refs):
        u_ref, wb_ref, tab_ref, wc_ref = refs[0:4]
        g_ins = refs[4:4 + ng]
        xs_ref, y_ref = refs[4 + ng:6 + ng]
        g_outs = refs[6 + ng:6 + 2 * ng]
        carry_ref = refs[6 + 2 * ng]
        sems = refs[7 + 2 * ng:]
        j, i = pl.program_id(0), pl.program_id(1)

        @pl.when(i == 0)
        def _():
            carry_ref[...] = jnp.zeros_like(carry_ref)

        if ng:
            @pl.when(jnp.logical_and(j == 0, i == 0))
            def _():
                _gather_start(g_ins, gather_bases, g_outs, sems)

        xs_ref[...] = _dot(_interleave_chunks(u_ref[...]).astype(BF16), wb_ref[...])
        fix, start = _chunk_scan(xs_ref, tab_ref, carry_ref, length, reverse=False)
        lax.fori_loop(0, length, fix, start, unroll=2)
        y_ref[...] = _time_order(_dot(xs_ref[...].astype(BF16), wc_ref[...]))

        if ng:
            @pl.when(jnp.logical_and(j == SSM_CHUNKS - 1, i == nt - 1))
            def _():
                _gather_finish(g_ins, gather_bases, g_outs, sems)

    outs = pl.pallas_call(
        body, name="ssm_scan_gather" if ng else "ssm_scan", grid=(SSM_CHUNKS, nt),
        in_specs=[pl.BlockSpec((tm, CH_W), lambda j, i: (i, j)),
                  pl.BlockSpec((None, CH_W, 2 * CH_S), lambda j, i: (j, 0, 0)),
                  pl.BlockSpec((None, length, 2 * CH_S), lambda j, i: (j, 0, 0)),
                  pl.BlockSpec((None, 2 * CH_S, CH_W), lambda j, i: (j, 0, 0))] + [ANY] * ng,
        out_specs=[pl.BlockSpec((None, tm, 2 * CH_S), lambda j, i: (j, i, 0)),
                   pl.BlockSpec((tm, CH_W), lambda j, i: (i, j))] + [ANY] * ng,
        out_shape=[jax.ShapeDtypeStruct((SSM_CHUNKS, s, 2 * CH_S), F32), jax.ShapeDtypeStruct((s, SSM_WIDTH), F32)]
        + _gather_outputs(gather),
        scratch_shapes=[pltpu.VMEM((SUBLANES, 2 * CH_S), F32)] + (_gather_sems(ng) if ng else []),
        compiler_params=_cparams(2),
    )(proj, wb, tab, wc, *gather)
    return outs[0], outs[1], (_gather_own(outs[2:], gather, gather_bases) if ng else [])


def _glu_forward(y, u, d, wg_ref, bg):
    yf = y + d * u
    z = _gelu(yf)
    zb = z.astype(BF16)
    zz = jnp.concatenate([_dot(zb, wg_ref[sh]) for sh in range(N_CHIPS)], axis=-1) + bg
    return yf, z, zz[:, 0:SSM_WIDTH], zz[:, SSM_WIDTH:2 * SSM_WIDTH]


def _ssm_glu_fwd(y, proj, d, w_glu_l, b_glu):
    s = y.shape[0]
    tm = _row_tile(s, 512)

    def body(y_ref, u_ref, gs_ref, d_ref, wg_ref, bg_ref, o_ref):
        _, _, val, gate = _glu_forward(y_ref[...], u_ref[...], d_ref[...], wg_ref, bg_ref[...])
        gs = gs_ref[...]
        o_ref[...] = val * _sigmoid(gate) * (gs * _sigmoid(gs))

    row = lambda i: (i, 0)
    return pl.pallas_call(
        body, name="ssm_glu", grid=(s // tm,),
        in_specs=[pl.BlockSpec((tm, SSM_WIDTH), row), pl.BlockSpec((tm, SSM_WIDTH), row),
                  pl.BlockSpec((tm, SSM_WIDTH), lambda i: (i, 1)), pl.BlockSpec((1, SSM_WIDTH), lambda i: (0, 0)),
                  pl.BlockSpec((N_CHIPS, SSM_WIDTH, ROW_SHARD), lambda i: (0, 0, 0)),
                  pl.BlockSpec((1, 2 * SSM_WIDTH), lambda i: (0, 0))],
        out_specs=pl.BlockSpec((tm, SSM_WIDTH), row),
        out_shape=jax.ShapeDtypeStruct((s, SSM_WIDTH), F32),
        compiler_params=_cparams(1),
    )(y, proj, proj, d, w_glu_l, b_glu)


def _tri(kind):
    r = jnp.arange(ATTN_BLOCK)
    if kind == "suffix_incl":
        m = r[:, None] >= r[None, :]
    else:
        m = r[:, None] < r[None, :]
    return jnp.concatenate([m, jnp.ones_like(m)], axis=1).astype(BF16)


def _head_masks():
    lane = lax.broadcasted_iota(jnp.int32, (1, 2 * HEAD_DIM), 1)
    return [lane < HEAD_DIM, lane >= HEAD_DIM]


def _chain_step(t, base, n_sub, first, q_ref, k_ref, tri_ref, l_scr, per_chain):
    tb = ATTN_BLOCK
    row = lax.broadcasted_iota(jnp.int32, (tb, tb), 0)
    col = lax.broadcasted_iota(jnp.int32, (tb, tb), 1)
    masks = _head_masks()
    blks = [base + a - t for a in range(n_sub)]
    r0s = [pl.multiple_of(jnp.maximum(blk, 0) * tb, tb) for blk in blks]
    zs = []
    for a in range(n_sub):
        kb = k_ref[pl.ds(r0s[a], tb), :]
        qa = q_ref[a * tb:(a + 1) * tb, :]
        for mask in masks:
            zs.append(_dot_nt(jnp.where(mask, qa, jnp.zeros_like(qa)), kb))
    parts = []
    for z in zs:
        ls = jnp.minimum(-z, 0.0) - jnp.log(1.0 + jnp.exp(-jnp.abs(z)))
        if first:
            ls = jnp.where(col < row, ls, 0.0)
        parts.append(_split_hilo(ls))
    tri = tri_ref[...]
    sums = [_dot(hi, tri) + _dot(lo, tri) for hi, lo in parts]
    top = None
    ws = []
    for c, (z, sm) in enumerate(zip(zs, sums)):
        if first:
            lsum = jnp.zeros((tb, tb), F32)
        else:
            lsum = l_scr[c] + jnp.where(blks[c // 2] >= 0, 0.0, -1e30)
        w = jnp.exp(z + sm[:, 0:tb] + lsum)
        if first:
            w = jnp.where(col < row, w, 0.0)
        ws.append(w)
        lsum = lsum + sm[:, tb:2 * tb]
        l_scr[c] = lsum
        top = lsum if top is None else jnp.maximum(top, lsum)
    for c, (z, w) in enumerate(zip(zs, ws)):
        per_chain(c // 2, c % 2, c, r0s[c // 2], z, w)
    return jnp.max(top)


def _chain_sweep(base, n_sub, q_ref, k_ref, tri_ref, l_scr, per_chain):
    top = _chain_step(0, base, n_sub, True, q_ref, k_ref, tri_ref, l_scr, functools.partial(per_chain, 0))

    def cond(carry):
        t, top = carry
        return jnp.logical_and(t <= base + n_sub - 1, top > EXP_ZERO)

    def step(carry):
        t, _ = carry
        return t + 1, _chain_step(t, base, n_sub, False, q_ref, k_ref, tri_ref, l_scr, functools.partial(per_chain, t))

    steps, _ = lax.while_loop(cond, step, (jnp.int32(1), top))
    return steps


ATTN_SUB_FWD = 8
ATTN_SUB_BWD = 4


def _attn_fwd(qkv, proj, gather=None, gather_bases=None):
    s = qkv.shape[0]
    tb = ATTN_BLOCK
    n_sub = min(ATTN_SUB_FWD, s // tb)
    tq = n_sub * tb
    n_hp = ATTN_WIDTH // (2 * HEAD_DIM)
    gather = [] if gather is None else gather
    ng = len(gather)

    def body(*refs):
        q_ref, k_ref, v_ref, g_ref, tri_ref = refs[0:5]
        g_ins = refs[5:5 + ng]
        o_ref, ya_ref = refs[5 + ng:7 + ng]
        g_outs = refs[7 + ng:7 + 2 * ng]
        l_scr = refs[7 + 2 * ng]
        sems = refs[8 + 2 * ng:]
        i = pl.program_id(1)
        masks = _head_masks()
        o_ref[...] = jnp.zeros_like(o_ref)

        if ng:
            @pl.when(jnp.logical_and(pl.program_id(0) == 0, i == 0))
            def _():
                _gather_start(g_ins, gather_bases, g_outs, sems)

        def per_chain(t, a, h, c, r0, z, w):
            vb = v_ref[pl.ds(r0, tb), :]
            vb = jnp.where(masks[h], vb, jnp.zeros_like(vb))
            o_ref[a * tb:(a + 1) * tb, :] += _dot(w.astype(BF16), vb)

        _chain_sweep(i * n_sub, n_sub, q_ref, k_ref, tri_ref, l_scr, per_chain)
        g = g_ref[...]
        ya_ref[...] = o_ref[...] * (g * _sigmoid(g))

        if ng:
            @pl.when(jnp.logical_and(pl.program_id(0) == n_hp - 1, i == s // tq - 1))
            def _():
                _gather_finish(g_ins, gather_bases, g_outs, sems)

    hp_blk = lambda off: pl.BlockSpec((tq, 2 * HEAD_DIM), lambda hp, i: (i, off + hp))
    res = lambda off: pl.BlockSpec((s, 2 * HEAD_DIM), lambda hp, i: (0, off + hp))
    outs = pl.pallas_call(
        body, name="attn_fwd_gather" if ng else "attn_fwd", grid=(n_hp, s // tq),
        in_specs=[hp_blk(0), res(4), res(8), hp_blk(20), pl.BlockSpec((tb, 2 * tb), lambda hp, i: (0, 0))] + [ANY] * ng,
        out_specs=[hp_blk(0), hp_blk(0)] + [ANY] * ng,
        out_shape=[jax.ShapeDtypeStruct((s, ATTN_WIDTH), F32)] * 2 + _gather_outputs(gather),
        scratch_shapes=[pltpu.VMEM((2 * n_sub, tb, tb), F32)] + (_gather_sems(ng) if ng else []),
        compiler_params=_cparams(2),
    )(qkv, qkv, qkv, proj, _tri("suffix_incl"), *gather)
    return outs[0], outs[1], (_gather_own(outs[2:], gather, gather_bases) if ng else [])


def _rms_rows(x, g):
    r = lax.rsqrt(jnp.mean(x * x, axis=-1, keepdims=True) + RMS_EPS)
    return r, x * r * g


def _ple_forward(h1, p, g2, wpg_ref, wpp_ref):
    r2, hn2 = _rms_rows(h1, g2)
    hb = hn2.astype(BF16)
    gpre = _dot(hb[:, 0:ROW_SHARD], wpg_ref[0])
    for sh in range(1, N_CHIPS):
        gpre = gpre + _dot(hb[:, ROW_SHARD * sh:ROW_SHARD * (sh + 1)], wpg_ref[sh])
    gate = _sigmoid(gpre)
    pb = p.astype(BF16)
    pp = jnp.concatenate([_dot(pb, wpp_ref[sh]) for sh in range(N_CHIPS)], axis=-1)
    return r2, hb, gate, pp


def _colsum8(a):
    t = a.shape[0]
    return a.reshape(t // SUBLANES, SUBLANES, a.shape[1]).sum(axis=0)


def _sq_err_grad(y, target):
    e = y - target
    sq = _colsum8(e * e)
    part = sq[:, 0:128]
    for b in range(1, D_MODEL // 128):
        part = part + sq[:, 128 * b:128 * (b + 1)]
    return e / D_MODEL, part


def _out_ple(h, ys, ya, p, g2, w_out_l, w_pg_l, w_pp_l, target=None):
    s = h.shape[0]
    tm = _row_tile(s, 512)
    last = target is not None

    def body(*refs):
        h_ref, ys_ref, ya_ref, p_ref, g_ref, wo_ref, wpg_ref, wpp_ref = refs[0:8]
        h1_ref, h2_ref = refs[8 + last], refs[9 + last]
        ysb = ys_ref[...].astype(BF16)
        yab = ya_ref[...].astype(BF16)
        h1 = h_ref[...]
        for sh, src in enumerate((ysb[:, 0:ROW_SHARD], ysb[:, ROW_SHARD:], yab[:, 0:ROW_SHARD], yab[:, ROW_SHARD:])):
            h1 = h1 + _dot(src, wo_ref[sh])
        _, _, gate, pp = _ple_forward(h1, p_ref[...], g_ref[...], wpg_ref, wpp_ref)
        h1_ref[...] = h1
        h2 = h1 + gate * pp
        if last:
            acc_ref = refs[11]

            @pl.when(pl.program_id(0) == 0)
            def _():
                acc_ref[...] = jnp.zeros_like(acc_ref)

            h2_ref[...], part = _sq_err_grad(h2, refs[8][...])
            acc_ref[...] += part
        else:
            h2_ref[...] = h2

    row = lambda i: (i, 0)
    big = pl.BlockSpec((tm, D_MODEL), row)
    wspec = lambda r, cdim: pl.BlockSpec((N_CHIPS, r, cdim), lambda i: (0, 0, 0))
    acc = pl.BlockSpec((SUBLANES, 128), lambda i: (0, 0))
    return pl.pallas_call(
        body, name="out_ple_loss" if last else "out_ple", grid=(s // tm,),
        in_specs=[big, pl.BlockSpec((tm, SSM_WIDTH), row), pl.BlockSpec((tm, ATTN_WIDTH), row),
                  pl.BlockSpec((tm, PLE_DIM), row), pl.BlockSpec((1, D_MODEL), lambda i: (0, 0)),
                  wspec(ROW_SHARD, D_MODEL), wspec(ROW_SHARD, D_MODEL), wspec(PLE_DIM, ROW_SHARD)] + [big] * last,
        out_specs=[big] * 2 + [acc] * last,
        out_shape=[jax.ShapeDtypeStruct((s, D_MODEL), F32)] * 2 + [jax.ShapeDtypeStruct((SUBLANES, 128), F32)] * last,
        compiler_params=_cparams(1),
    )(h, ys, ya, p, g2, w_out_l, w_pg_l, w_pp_l, *([target] if last else []))


def _rms_bwd(x, r, g, dy):
    gdy = g * dy
    dx = r * gdy - x * (r * r * r) * jnp.mean(x * gdy, axis=-1, keepdims=True)
    return dx, x * r * dy


def _out_ple_bwd(dh2, h1, p, g2, w_out_l, w_pg_l, w_pp_l):
    s = h1.shape[0]
    tm = _row_tile(s, 512)

    def body(dh2_ref, h1_ref, p_ref, g_ref, wo_ref, wpg_ref, wpp_ref,
             dh1_ref, dmix_ref, hn_ref, dgp_ref, dpp_ref, dh1b_ref, dg_ref):
        @pl.when(pl.program_id(0) == 0)
        def _():
            dg_ref[...] = jnp.zeros_like(dg_ref)

        h1 = h1_ref[...]
        dh2 = dh2_ref[...]
        g2v = g_ref[...]
        r2, hb, gate, pp = _ple_forward(h1, p_ref[...], g2v, wpg_ref, wpp_ref)
        dgp = (dh2 * pp) * gate * (1.0 - gate)
        dgpb = dgp.astype(BF16)
        dhn = jnp.concatenate([_dot_nt(dgpb, wpg_ref[sh]) for sh in range(N_CHIPS)], axis=-1)
        dx, dgrow = _rms_bwd(h1, r2, g2v, dhn)
        dh1 = dh2 + dx
        dh1b = dh1.astype(BF16)
        dh1_ref[...] = dh1
        dh1b_ref[...] = dh1b
        hn_ref[...] = hb
        dgp_ref[...] = dgpb
        dpp_ref[...] = (dh2 * gate).astype(BF16)
        dg_ref[...] += _colsum8(dgrow)
        for sh in range(N_CHIPS):
            dmix_ref[:, ROW_SHARD * sh:ROW_SHARD * (sh + 1)] = _dot_nt(dh1b, wo_ref[sh])

    row = lambda i: (i, 0)
    wspec = lambda r, cdim: pl.BlockSpec((N_CHIPS, r, cdim), lambda i: (0, 0, 0))
    big = pl.BlockSpec((tm, D_MODEL), row)
    return pl.pallas_call(
        body, name="out_ple_bwd", grid=(s // tm,),
        in_specs=[big, big, pl.BlockSpec((tm, PLE_DIM), row), pl.BlockSpec((1, D_MODEL), lambda i: (0, 0)),
                  wspec(ROW_SHARD, D_MODEL), wspec(ROW_SHARD, D_MODEL), wspec(PLE_DIM, ROW_SHARD)],
        out_specs=[big] * 6 + [pl.BlockSpec((SUBLANES, D_MODEL), lambda i: (0, 0))],
        out_shape=[jax.ShapeDtypeStruct((s, D_MODEL), F32)] * 2 + [jax.ShapeDtypeStruct((s, D_MODEL), BF16)] * 4
        + [jax.ShapeDtypeStruct((SUBLANES, D_MODEL), F32)],
        compiler_params=_cparams(1),
    )(dh2, h1, p, g2, w_out_l, w_pg_l, w_pp_l)


def _tn_matmul(a, b, n_blocks, block_a, name, into=None, first_block=0, total_blocks=None):
    s = a.shape[0]
    tk = _row_tile(s, 1024)
    nk = s // tk
    total_blocks = n_blocks if total_blocks is None else total_blocks
    ka, nb = a.shape[1], b.shape[1]
    if block_a:
        ka //= n_blocks
    else:
        nb //= n_blocks

    def body(*refs):
        a_ref, b_ref, o_ref, acc_ref = refs[0], refs[1], refs[-2], refs[-1]

        @pl.when(pl.program_id(0) == 0)
        def _():
            acc_ref[...] = jnp.zeros_like(acc_ref)

        at = a_ref[...].astype(BF16).T
        bb = b_ref[...].astype(BF16)
        for sh in range(n_blocks):
            if block_a:
                acc_ref[sh] += _dot(at[ka * sh:ka * (sh + 1), :], bb)
            else:
                acc_ref[sh] += _dot(at, bb[:, nb * sh:nb * (sh + 1)])

        @pl.when(pl.program_id(0) == nk - 1)
        def _():
            o_ref[...] = acc_ref[...].astype(BF16)

    in_specs = [pl.BlockSpec((tk, a.shape[1]), lambda i: (i, 0)), pl.BlockSpec((tk, b.shape[1]), lambda i: (i, 0))]
    operands = [a, b]
    aliases = {}
    if into is not None:
        in_specs.append(ANY)
        operands.append(into)
        aliases = {2: 0}
    return pl.pallas_call(
        body, name=name, grid=(nk,),
        in_specs=in_specs,
        out_specs=pl.BlockSpec((n_blocks, ka, nb), lambda i: (first_block // n_blocks, 0, 0)),
        out_shape=jax.ShapeDtypeStruct((total_blocks, ka, nb), BF16),
        scratch_shapes=[pltpu.VMEM((n_blocks, ka, nb), F32)],
        input_output_aliases=aliases,
        compiler_params=_cparams(1),
    )(*operands)


def _attn_bwd(qkv, o, proj, dmix, scatter=None):
    scatter = [] if scatter is None else scatter
    nsc = len(scatter)
    s = qkv.shape[0]
    tb = ATTN_BLOCK
    nq = s // tb
    n_sub = min(ATTN_SUB_BWD, nq)
    tq = n_sub * tb
    n_chain = 2 * n_sub

    def body(*refs):
        q_ref, k_ref, v_ref, o_ref, g_ref, dya_ref, tri_s_ref, tri_p_ref = refs[0:8]
        sc_ins = refs[8:8 + nsc]
        dq_ref, dk_ref, dv_ref, dg_ref = refs[8 + nsc:12 + nsc]
        sc_outs = refs[12 + nsc:12 + 2 * nsc]
        do_scr, l_scr, g_scr, s_scr, w_scr = refs[12 + 2 * nsc:17 + 2 * nsc]
        sc_sems = refs[17 + 2 * nsc:]
        i = pl.program_id(1)
        base = i * n_sub

        if nsc:
            @pl.when(jnp.logical_and(pl.program_id(0) == 0, i == 0))
            def _():
                _scatter_start(sc_ins, sc_outs, sc_sems)

        @pl.when(i == 0)
        def _():
            dk_ref[...] = jnp.zeros_like(dk_ref)
            dv_ref[...] = jnp.zeros_like(dv_ref)

        g = g_ref[...]
        sg = _sigmoid(g)
        dya = dya_ref[...]
        do_scr[...] = (dya * (g * sg)).astype(BF16)
        dg_ref[...] = dya * o_ref[...] * (sg * (1.0 + g * (1.0 - sg)))
        dq_ref[...] = jnp.zeros_like(dq_ref)
        g_scr[...] = jnp.zeros_like(g_scr)
        masks = _head_masks()

        def keep(t, a, h, c, r0, z, w):
            s_scr[c, t] = _sigmoid(z).astype(BF16)
            w_scr[c, t] = w.astype(BF16)

        steps = _chain_sweep(base, n_sub, q_ref, k_ref, tri_s_ref, l_scr, keep)
        row = lax.broadcasted_iota(jnp.int32, (tb, tb), 0)
        col = lax.broadcasted_iota(jnp.int32, (tb, tb), 1)

        def back(it, carry):
            t = steps - 1 - it
            r0s = [pl.multiple_of(jnp.maximum(base + a - t, 0) * tb, tb) for a in range(n_sub)]
            qhs, dohs, khs, gws = [], [], [], []
            for a in range(n_sub):
                kb = k_ref[pl.ds(r0s[a], tb), :]
                vb = v_ref[pl.ds(r0s[a], tb), :]
                qa = q_ref[a * tb:(a + 1) * tb, :]
                doa = do_scr[a * tb:(a + 1) * tb, :]
                for h, mask in enumerate(masks):
                    qhs.append(jnp.where(mask, qa, jnp.zeros_like(qa)))
                    khs.append(jnp.where(mask, kb, jnp.zeros_like(kb)))
                    dohs.append(jnp.where(mask, doa, jnp.zeros_like(doa)))
                    gws.append(w_scr[2 * a + h, t].astype(F32) * _dot_nt(dohs[-1], vb))
            parts = [_split_hilo(gw) for gw in gws]
            tri = tri_p_ref[...]
            sums = [_dot(hi, tri) + _dot(lo, tri) for hi, lo in parts]
            dzs = []
            for c, (gw, sm) in enumerate(zip(gws, sums)):
                gsum = g_scr[c]
                dz = gw - (gw + sm[:, 0:tb] + gsum) * s_scr[c, t].astype(F32)
                dz = jnp.where(col < row + t * tb, dz, 0.0)
                g_scr[c] = gsum + sm[:, tb:2 * tb]
                dzs.append(dz.astype(BF16))
            for c, dzb in enumerate(dzs):
                a = c // 2
                dk_ref[pl.ds(r0s[a], tb), :] += _dot_tn(dzb, qhs[c])
                dv_ref[pl.ds(r0s[a], tb), :] += _dot_tn(w_scr[c, t], dohs[c])
                dq_ref[a * tb:(a + 1) * tb, :] += _dot(dzb, khs[c])
            return carry

        lax.fori_loop(0, steps, back, 0)

        if nsc:
            @pl.when(jnp.logical_and(pl.program_id(0) == n_hp - 1, i == s // tq - 1))
            def _():
                _scatter_finish(sc_ins, sc_outs, sc_sems)

    n_hp = ATTN_WIDTH // (2 * HEAD_DIM)
    hp_blk = lambda off: pl.BlockSpec((tq, 2 * HEAD_DIM), lambda hp, i: (i, off + hp))
    res = lambda off: pl.BlockSpec((s, 2 * HEAD_DIM), lambda hp, i: (0, off + hp))
    tri = pl.BlockSpec((tb, 2 * tb), lambda hp, i: (0, 0))
    outs = pl.pallas_call(
        body, name="attn_bwd_scatter" if nsc else "attn_bwd", grid=(n_hp, s // tq),
        in_specs=[hp_blk(0), res(4), res(8), hp_blk(0), hp_blk(20), hp_blk(4), tri, tri] + [ANY] * nsc,
        out_specs=[hp_blk(0), res(0), res(0), hp_blk(0)] + [ANY] * nsc,
        out_shape=[jax.ShapeDtypeStruct((s, ATTN_WIDTH), F32)] * 4 + [jax.ShapeDtypeStruct(a.shape, a.dtype) for a in scatter],
        scratch_shapes=[pltpu.VMEM((tq, 2 * HEAD_DIM), BF16), pltpu.VMEM((n_chain, tb, tb), F32),
                        pltpu.VMEM((n_chain, tb, tb), F32), pltpu.VMEM((n_chain, nq, tb, tb), BF16),
                        pltpu.VMEM((n_chain, nq, tb, tb), BF16)] + (_scatter_sems(nsc) if nsc else []),
        compiler_params=_cparams(2),
    )(qkv, qkv, qkv, o, proj, dmix, _tri("suffix_incl"), _tri("prefix_strict"), *scatter)
    return outs[0], outs[1], outs[2], outs[3], (_scatter_own(outs[4:], scatter) if nsc else [])


def _ssm_glu_bwd(dmix, y, proj, d, w_glu_l, b_glu):
    s = y.shape[0]
    tm = _row_tile(s, 512)

    def body(dys_ref, y_ref, u_ref, gs_ref, d_ref, wg_ref, bg_ref,
             dyf_ref, du_ref, dgs_ref, z_ref, dzz_ref, dd_ref, db_ref):
        @pl.when(pl.program_id(0) == 0)
        def _():
            dd_ref[...] = jnp.zeros_like(dd_ref)
            db_ref[...] = jnp.zeros_like(db_ref)

        u = u_ref[...]
        dv = d_ref[...]
        yf, z, val, gate = _glu_forward(y_ref[...], u, dv, wg_ref, bg_ref[...])
        gs = gs_ref[...]
        sgs = _sigmoid(gs)
        sgate = _sigmoid(gate)
        dys = dys_ref[...]
        dgv = dys * (gs * sgs)
        dgs_ref[...] = dys * (val * sgate) * (sgs * (1.0 + gs * (1.0 - sgs)))
        dzz = jnp.concatenate([dgv * sgate, dgv * val * sgate * (1.0 - sgate)], axis=-1)
        dzzb = dzz.astype(BF16)
        dz = _dot_nt(dzzb[:, 0:ROW_SHARD], wg_ref[0])
        for sh in range(1, N_CHIPS):
            dz = dz + _dot_nt(dzzb[:, ROW_SHARD * sh:ROW_SHARD * (sh + 1)], wg_ref[sh])
        dyf = dz * _gelu_grad(yf)
        dyf_ref[...] = dyf
        du_ref[...] = dyf * dv
        z_ref[...] = z.astype(BF16)
        dzz_ref[...] = dzzb
        dd_ref[...] += _colsum8(dyf * u)
        db_ref[...] += _colsum8(dzz)

    row = lambda i: (i, 0)
    half = pl.BlockSpec((tm, SSM_WIDTH), row)
    return pl.pallas_call(
        body, name="ssm_glu_bwd", grid=(s // tm,),
        in_specs=[half, half, half, pl.BlockSpec((tm, SSM_WIDTH), lambda i: (i, 1)),
                  pl.BlockSpec((1, SSM_WIDTH), lambda i: (0, 0)),
                  pl.BlockSpec((N_CHIPS, SSM_WIDTH, ROW_SHARD), lambda i: (0, 0, 0)),
                  pl.BlockSpec((1, 2 * SSM_WIDTH), lambda i: (0, 0))],
        out_specs=[half, half, half, half, pl.BlockSpec((tm, 2 * SSM_WIDTH), row),
                   pl.BlockSpec((SUBLANES, SSM_WIDTH), lambda i: (0, 0)),
                   pl.BlockSpec((SUBLANES, 2 * SSM_WIDTH), lambda i: (0, 0))],
        out_shape=[jax.ShapeDtypeStruct((s, SSM_WIDTH), F32)] * 3
        + [jax.ShapeDtypeStruct((s, SSM_WIDTH), BF16), jax.ShapeDtypeStruct((s, 2 * SSM_WIDTH), BF16),
           jax.ShapeDtypeStruct((SUBLANES, SSM_WIDTH), F32), jax.ShapeDtypeStruct((SUBLANES, 2 * SSM_WIDTH), F32)],
        compiler_params=_cparams(1),
    )(dmix, y, proj, proj, d, w_glu_l, b_glu)


def _ssm_scan_bwd(dyf, xs, proj, wct, tab_rev, wbt):
    s = dyf.shape[0]
    tm = _row_tile(s, SCAN_TILE)
    nt = s // tm
    length = tm // SUBLANES

    def body(dy_ref, xs_ref, u_ref, wct_ref, tab_ref, wbt_ref, du_ref, dwc_ref, dwb_ref, da_ref, lam_ref, carry_ref):
        @pl.when(pl.program_id(1) == 0)
        def _():
            carry_ref[...] = jnp.zeros_like(carry_ref)
            dwc_ref[...] = jnp.zeros_like(dwc_ref)
            dwb_ref[...] = jnp.zeros_like(dwb_ref)
            da_ref[...] = jnp.zeros_like(da_ref)

        dyp = _interleave_chunks(dy_ref[...]).astype(BF16)
        up = _interleave_chunks(u_ref[...]).astype(BF16)
        lam_ref[...] = _dot(dyp, wct_ref[...])

        def tail(r0, lr, li, carry):
            er, ei, dar, dai = carry
            xr = xs_ref[pl.ds(r0, SUBLANES), 0:CH_S]
            xi = xs_ref[pl.ds(r0, SUBLANES), CH_S:2 * CH_S]
            return lr, li, dar + (xr * er + xi * ei), dai + (xr * ei - xi * er)

        fix, (gr, gi) = _chunk_scan(lam_ref, tab_ref, carry_ref, length, reverse=True, tail=tail)
        zero = jnp.zeros((SUBLANES, CH_S), F32)
        _, _, dar, dai = lax.fori_loop(0, length, fix, (gr, gi, zero, zero), unroll=2)
        da_ref[:, 0:CH_S] += dar
        da_ref[:, CH_S:2 * CH_S] += dai
        lamb = lam_ref[...].astype(BF16)
        du_ref[...] = _time_order(_dot(lamb, wbt_ref[...]))
        dwc_ref[...] += _dot_tn(xs_ref[...].astype(BF16), dyp)
        dwb_ref[...] += _dot_tn(up, lamb)

    rev = lambda j, i: (nt - 1 - i, j)
    return pl.pallas_call(
        body, name="ssm_scan_bwd", grid=(SSM_CHUNKS, nt),
        in_specs=[pl.BlockSpec((tm, CH_W), rev),
                  pl.BlockSpec((None, tm, 2 * CH_S), lambda j, i: (j, nt - 1 - i, 0)),
                  pl.BlockSpec((tm, CH_W), rev),
                  pl.BlockSpec((None, CH_W, 2 * CH_S), lambda j, i: (j, 0, 0)),
                  pl.BlockSpec((None, length, 2 * CH_S), lambda j, i: (j, 0, 0)),
                  pl.BlockSpec((None, 2 * CH_S, CH_W), lambda j, i: (j, 0, 0))],
        out_specs=[pl.BlockSpec((tm, CH_W), rev),
                   pl.BlockSpec((None, 2 * CH_S, CH_W), lambda j, i: (j, 0, 0)),
                   pl.BlockSpec((None, CH_W, 2 * CH_S), lambda j, i: (j, 0, 0)),
                   pl.BlockSpec((None, SUBLANES, 2 * CH_S), lambda j, i: (j, 0, 0))],
        out_shape=[jax.ShapeDtypeStruct((s, SSM_WIDTH), F32),
                   jax.ShapeDtypeStruct((SSM_CHUNKS, 2 * CH_S, CH_W), F32),
                   jax.ShapeDtypeStruct((SSM_CHUNKS, CH_W, 2 * CH_S), F32),
                   jax.ShapeDtypeStruct((SSM_CHUNKS, SUBLANES, 2 * CH_S), F32)],
        scratch_shapes=[pltpu.VMEM((tm, 2 * CH_S), F32), pltpu.VMEM((SUBLANES, 2 * CH_S), F32)],
        compiler_params=_cparams(2),
    )(dyf, xs, proj, wct, tab_rev, wbt)


def _in_proj_bwd(h, g1, w_in_l, qg, kg, proj, du_a, du_b, dgs, dq, dk, dv, dga, dh1):
    s = h.shape[0]
    tm = _row_tile(s, 256)

    def body(h_ref, g_ref, w_ref, qg_ref, kg_ref, ones_ref, q_ref, k_ref, dua_ref, dub_ref, dgs_ref, dq_ref, dk_ref,
             dv_ref, dga_ref, dh1_ref, dh_ref, hn_ref, dp_ref, dg1_ref, dqg_ref, dkg_ref):
        @pl.when(pl.program_id(0) == 0)
        def _():
            dg1_ref[...] = jnp.zeros_like(dg1_ref)
            dqg_ref[...] = jnp.zeros_like(dqg_ref)
            dkg_ref[...] = jnp.zeros_like(dkg_ref)

        ones = ones_ref[...]

        def head_norm_bwd(x, gain, dy):
            r = lax.rsqrt(_dot_hilo(x * x, ones) + RMS_EPS)
            gdy = gain * dy
            dx = r * gdy - x * (r * r * r) * _dot_hilo(x * gdy, ones)
            return dx, x * r * dy

        dqr, dqg_rows = head_norm_bwd(q_ref[...], qg_ref[...], dq_ref[...] * ATTN_SCALE)
        dkr, dkg_rows = head_norm_bwd(k_ref[...], kg_ref[...], dk_ref[...])
        dqg_ref[...] += _colsum8(dqg_rows)
        dkg_ref[...] += _colsum8(dkg_rows)
        dp_ref[:, 0:512] = (dua_ref[...] + dub_ref[...]).astype(BF16)
        dp_ref[:, 512:1024] = dgs_ref[...].astype(BF16)
        dp_ref[:, 1024:1536] = dqr.astype(BF16)
        dp_ref[:, 1536:2048] = dkr.astype(BF16)
        dp_ref[:, 2048:2560] = dv_ref[...].astype(BF16)
        dp_ref[:, 2560:3072] = dga_ref[...].astype(BF16)
        dhn = _dot_nt(dp_ref[:, 0:IN_SHARD], w_ref[0])
        for sh in range(1, N_CHIPS):
            dhn = dhn + _dot_nt(dp_ref[:, IN_SHARD * sh:IN_SHARD * (sh + 1)], w_ref[sh])
        x = h_ref[...]
        gv = g_ref[...]
        r, hn = _rms_rows(x, gv)
        dx, dg_rows = _rms_bwd(x, r, gv, dhn)
        dh_ref[...] = dh1_ref[...] + dx
        hn_ref[...] = hn.astype(BF16)
        dg1_ref[...] += _colsum8(dg_rows)

    row = lambda i: (i, 0)
    full = lambda shape: pl.BlockSpec(shape, lambda i: (0,) * len(shape))
    big = pl.BlockSpec((tm, D_MODEL), row)
    half = pl.BlockSpec((tm, 512), row)
    return pl.pallas_call(
        body, name="in_proj_bwd", grid=(s // tm,),
        in_specs=[big, full((1, D_MODEL)), full((N_CHIPS, D_MODEL, IN_SHARD)),
                  full((1, ATTN_WIDTH)), full((1, ATTN_WIDTH)), full((ATTN_WIDTH, ATTN_WIDTH)),
                  pl.BlockSpec((tm, 512), lambda i: (i, 2)), pl.BlockSpec((tm, 512), lambda i: (i, 3)),
                  half, half, half, half, half, half, half, big],
        out_specs=[big, big, pl.BlockSpec((tm, IN_COLS), row), pl.BlockSpec((SUBLANES, D_MODEL), lambda i: (0, 0)),
                   pl.BlockSpec((SUBLANES, ATTN_WIDTH), lambda i: (0, 0)), pl.BlockSpec((SUBLANES, ATTN_WIDTH), lambda i: (0, 0))],
        out_shape=[jax.ShapeDtypeStruct((s, D_MODEL), F32), jax.ShapeDtypeStruct((s, D_MODEL), BF16),
                   jax.ShapeDtypeStruct((s, IN_COLS), BF16), jax.ShapeDtypeStruct((SUBLANES, D_MODEL), F32),
                   jax.ShapeDtypeStruct((SUBLANES, ATTN_WIDTH), F32), jax.ShapeDtypeStruct((SUBLANES, ATTN_WIDTH), F32)],
        compiler_params=_cparams(1),
    )(h, g1, w_in_l, qg, kg, _head_ones(), proj, proj, du_a, du_b, dgs, dq, dk, dv, dga, dh1)


SMALL_NAMES = ("mix_norm_g", "ssm_a_re", "ssm_a_im", "ssm_log_dt", "ssm_b_re", "ssm_b_im", "ssm_c_re", "ssm_c_im",
               "ssm_d", "ssm_b_glu", "q_norm_g", "k_norm_g", "ple_norm_g")
SMALL_4D = ("ssm_b_re", "ssm_b_im", "ssm_c_re", "ssm_c_im")
BIG_NAMES = ("w_in", "ssm_w_glu", "w_out", "w_ple_gate", "w_ple_proj")


def _ssm_setup(sm, layer, length):
    col = lambda a: a[layer].reshape(1, N_STATES)
    a_re, a_im = col(sm["ssm_a_re"]), col(sm["ssm_a_im"])
    log_dt = jnp.repeat(sm["ssm_log_dt"][layer], SSM_STATE).reshape(1, N_STATES)
    b_re = sm["ssm_b_re"][layer].reshape(N_STATES, SSM_GROUP).T
    b_im = sm["ssm_b_im"][layer].reshape(N_STATES, SSM_GROUP).T
    disc_in = (a_re, a_im, log_dt, b_re, b_im)
    _, _, bb_re, bb_im, tab, tab_rev = _disc_fwd(*disc_in, length)
    wb = jnp.concatenate([_block_diag_in(bb_re), _block_diag_in(bb_im)], axis=-1)
    wc = jnp.concatenate([_block_diag_out(sm["ssm_c_re"][layer]), -_block_diag_out(sm["ssm_c_im"][layer])], axis=1)
    return dict(disc_in=disc_in, wb=wb.astype(BF16), wbt=wb.transpose(0, 2, 1).astype(BF16),
                wc=wc.astype(BF16), wct=wc.transpose(0, 2, 1).astype(BF16),
                tab=tab, tab_rev=tab_rev)


def _whole_blocks(names, gathered):
    return {n: g.reshape(N_CHIPS, 2 * g.shape[2], g.shape[3]) for n, g in zip(names, gathered)}


def _local_step(x, p, target, sm, w_in0, local=None, gathered=None, layer1_hook=None):
    wg = [dict(w_in=w_in0), {}] if gathered is None else gathered
    tile8 = lambda a: jnp.tile(a, ATTN_WIDTH // HEAD_DIM).reshape(1, ATTN_WIDTH)
    saved = []
    h = x
    for l in range(N_LAYERS):
        ssm = _ssm_setup(sm, l, _row_tile(x.shape[0], SCAN_TILE) // SUBLANES)
        g1 = sm["mix_norm_g"][l].reshape(1, D_MODEL)
        g2 = sm["ple_norm_g"][l].reshape(1, D_MODEL)
        qg, kg = tile8(sm["q_norm_g"][l]), tile8(sm["k_norm_g"][l])
        dsk = sm["ssm_d"][l].reshape(1, SSM_WIDTH)
        bgl = sm["ssm_b_glu"][l].reshape(1, 2 * SSM_WIDTH)
        proj, qkv = _in_proj(h, g1, wg[l]["w_in"], qg, kg)
        if l == 0 and local is not None:
            rest = BIG_NAMES[1:]
            xs, y, got = _ssm_scan_fwd(proj, ssm["wb"], ssm["tab"], ssm["wc"], [local[n] for n in rest], [0] * len(rest))
            wg[0].update(_whole_blocks(rest, got))
            ys = _ssm_glu_fwd(y, proj, dsk, wg[0]["ssm_w_glu"], bgl)
            o, ya, got = _attn_fwd(qkv, proj, [local[n] for n in BIG_NAMES], [2] * len(BIG_NAMES))
            wg[1].update(_whole_blocks(BIG_NAMES, got))
        else:
            xs, y, _ = _ssm_scan_fwd(proj, ssm["wb"], ssm["tab"], ssm["wc"])
            ys = _ssm_glu_fwd(y, proj, dsk, wg[l]["ssm_w_glu"], bgl)
            o, ya, _ = _attn_fwd(qkv, proj)
        tail = (target,) if l == N_LAYERS - 1 else ()
        h1, h2, *sq = _out_ple(h, ys, ya, p[l], g2, wg[l]["w_out"], wg[l]["w_ple_gate"], wg[l]["w_ple_proj"], *tail)
        saved.append(dict(ssm=ssm, g1=g1, g2=g2, qg=qg, kg=kg, dsk=dsk, bgl=bgl, h=h, proj=proj, qkv=qkv, xs=xs, y=y,
                          ys=ys, o=o, ya=ya, h1=h1))
        h = h2
    dh = h
    loss = 0.5 * jnp.sum(sq[0]) / D_MODEL

    gbig = [{} for _ in range(N_LAYERS)]
    scattered = []
    gsm = {n: [None] * N_LAYERS for n in SMALL_NAMES}
    for l in reversed(range(N_LAYERS)):
        sv = saved[l]
        ssm = sv["ssm"]
        dh1, dmix, hn2b, dgpb, dppb, dh1b, dg2 = _out_ple_bwd(dh, sv["h1"], p[l], sv["g2"], wg[l]["w_out"],
                                                              wg[l]["w_ple_gate"], wg[l]["w_ple_proj"])
        gsm["ple_norm_g"][l] = dg2.sum(0)
        gbig[l]["w_ple_proj"] = _tn_matmul(p[l], dppb, N_CHIPS, False, "dw_ple_proj")
        gbig[l]["w_ple_gate"] = _tn_matmul(hn2b, dgpb, N_CHIPS, True, "dw_ple_gate")
        dwo = _tn_matmul(sv["ys"], dh1b, 2, True, "dw_out_ssm", None, 0, N_CHIPS)
        gbig[l]["w_out"] = _tn_matmul(sv["ya"], dh1b, 2, True, "dw_out_attn", dwo, 2, N_CHIPS)
        if l == 0 and layer1_hook is not None:
            dqs, dkn, dv, dga, scattered = _attn_bwd(sv["qkv"], sv["o"], sv["proj"], dmix, layer1_hook(gbig[1]))
        else:
            dqs, dkn, dv, dga, _ = _attn_bwd(sv["qkv"], sv["o"], sv["proj"], dmix)
        dyf, du_a, dgs, zb, dzzb, dd, dbg = _ssm_glu_bwd(dmix, sv["y"], sv["proj"], sv["dsk"], wg[l]["ssm_w_glu"], sv["bgl"])
        gsm["ssm_d"][l] = dd.sum(0).reshape(SSM_GROUPS, SSM_GROUP)
        gsm["ssm_b_glu"][l] = dbg.sum(0)
        gbig[l]["ssm_w_glu"] = _tn_matmul(zb, dzzb, N_CHIPS, False, "dw_glu")
        du_b, dwc, dwb, da = _ssm_scan_bwd(dyf, sv["xs"], sv["proj"], ssm["wct"], ssm["tab_rev"], ssm["wbt"])
        gsm["ssm_c_re"][l] = _block_diag_out_t(dwc[:, 0:CH_S, :])
        gsm["ssm_c_im"][l] = -_block_diag_out_t(dwc[:, CH_S:, :])
        da = da.sum(1)
        g_ab_re = da[:, 0:CH_S].reshape(1, N_STATES)
        g_ab_im = da[:, CH_S:].reshape(1, N_STATES)
        g_bb_re = _block_diag_in_t(dwb[:, :, 0:CH_S])
        g_bb_im = _block_diag_in_t(dwb[:, :, CH_S:])
        d_are, d_aim, d_ldt, d_bre, d_bim = _disc_bwd(*ssm["disc_in"], g_ab_re, g_ab_im, g_bb_re, g_bb_im)
        gsm["ssm_a_re"][l] = d_are.reshape(SSM_GROUPS, SSM_STATE)
        gsm["ssm_a_im"][l] = d_aim.reshape(SSM_GROUPS, SSM_STATE)
        gsm["ssm_log_dt"][l] = d_ldt.reshape(SSM_GROUPS, SSM_STATE).sum(1)
        gsm["ssm_b_re"][l] = d_bre.T.reshape(SSM_GROUPS, SSM_STATE, SSM_GROUP)
        gsm["ssm_b_im"][l] = d_bim.T.reshape(SSM_GROUPS, SSM_STATE, SSM_GROUP)
        dh, hnb, dprojb, dg1, dqg, dkg = _in_proj_bwd(sv["h"], sv["g1"], wg[l]["w_in"], sv["qg"], sv["kg"], sv["proj"],
                                                      du_a, du_b, dgs, dqs, dkn, dv, dga, dh1)
        gsm["mix_norm_g"][l] = dg1.sum(0)
        gsm["q_norm_g"][l] = dqg.sum(0).reshape(-1, HEAD_DIM).sum(0)
        gsm["k_norm_g"][l] = dkg.sum(0).reshape(-1, HEAD_DIM).sum(0)
        gbig[l]["w_in"] = _tn_matmul(hnb, dprojb, N_CHIPS, False, "dw_in")
    gsm = {n: jnp.stack(v, 0) for n, v in gsm.items()}
    return loss, dh, gbig, gsm, scattered


_SMALL_PAD = 8 * 8 * 128


def _pack_small(d, extra):
    flat = jnp.concatenate([d[n].reshape(-1) for n in SMALL_NAMES] + [jnp.stack(extra)])
    n = flat.shape[0]
    padded = -(-n // _SMALL_PAD) * _SMALL_PAD
    return jnp.pad(flat, (0, padded - n))


def _unpack_small(flat, like):
    out, off = {}, 0
    for n in SMALL_NAMES:
        size = like[n].size
        out[n] = flat[off:off + size].reshape(like[n].shape)
        off += size
    return out, flat[off:]


def _half_views(arrs):
    return [a.reshape(a.shape[0], 2, a.shape[1] // 2, a.shape[2]) for a in arrs]


def _chip_sums(views, out_dtypes, tag):
    recv = _sibling_push(views, "grad_push_" + tag)
    return [_add_my_half(v, r, dt, "grad_half_add") for v, r, dt in zip(views, recv, out_dtypes)]


def kernel(x, p, mix_norm_g, w_in, ssm_a_re, ssm_a_im, ssm_log_dt, ssm_b_re, ssm_b_im, ssm_c_re, ssm_c_im, ssm_d, ssm_w_glu, ssm_b_glu, q_norm_g, k_norm_g, w_out, ple_norm_g, w_ple_gate, w_ple_proj, loss_target, m_mix_norm_g, m_w_in, m_ssm_a_re, m_ssm_a_im, m_ssm_log_dt, m_ssm_b_re, m_ssm_b_im, m_ssm_c_re, m_ssm_c_im, m_ssm_d, m_ssm_w_glu, m_ssm_b_glu, m_q_norm_g, m_k_norm_g, m_w_out, m_ple_norm_g, m_w_ple_gate, m_w_ple_proj, v_mix_norm_g, v_w_in, v_ssm_a_re, v_ssm_a_im, v_ssm_log_dt, v_ssm_b_re, v_ssm_b_im, v_ssm_c_re, v_ssm_c_im, v_ssm_d, v_ssm_w_glu, v_ssm_b_glu, v_q_norm_g, v_k_norm_g, v_w_out, v_ple_norm_g, v_w_ple_gate, v_w_ple_proj):
    args = dict(locals())
    names = ("mix_norm_g", "w_in", "ssm_a_re", "ssm_a_im", "ssm_log_dt", "ssm_b_re", "ssm_b_im", "ssm_c_re", "ssm_c_im",
             "ssm_d", "ssm_w_glu", "ssm_b_glu", "q_norm_g", "k_norm_g", "w_out", "ple_norm_g", "w_ple_gate", "w_ple_proj")
    w = {n: args[n] for n in names}
    m = {n: args["m_" + n] for n in names}
    v = {n: args["v_" + n] for n in names}

    local = {n: w[n].astype(BF16).reshape(2 * N_LAYERS, w[n].shape[1] // 2, w[n].shape[2]) for n in BIG_NAMES}
    w_in0 = _chip_gather([local["w_in"]], "w_in_gather")[0].reshape(N_CHIPS, D_MODEL, IN_SHARD)
    sm = {n: w[n] for n in SMALL_NAMES}
    nb = len(BIG_NAMES)
    loss, dx, gbig, gsm, got1 = _local_step(
        x[0], p[:, 0], loss_target[0], sm, w_in0, local,
        layer1_hook=lambda g1: _chip_sums(_half_views([g1[n] for n in BIG_NAMES]), [BF16] * nb, "layer1"))

    small = _pack_small(gsm, [loss]).reshape(N_CHIPS, 2, SUBLANES, -1)
    chip0 = _chip_sums(_half_views([gbig[0][n] for n in BIG_NAMES]) + [small], [BF16] * nb + [F32], "layer0")
    got0 = _chip_scatter(chip0, "grad_chip_scatter")
    tot1 = [_sum4(a, "grad_chip_sum") for a in got1]
    tot0 = [_sum4(a, "grad_chip_sum") for a in got0]
    pieces = [(t, k, (l,)) for l, tots in enumerate((tot0[:nb], tot1)) for k, t in enumerate(tots)] + [(tot0[nb], nb, ())]
    joined = _sibling_join(pieces, [(N_LAYERS, 2) + t.shape for t in tot1] + [(2,) + tot0[nb].shape], "grad_sibling_join")
    small_all = _chip_gather([joined[nb]], "small_grad_gather")[0]
    small_tot = small_all.reshape(-1)
    g = {n: j.reshape(w[n].shape) for n, j in zip(BIG_NAMES, joined)}
    g_small, rest = _unpack_small(small_tot, sm)
    g.update(g_small)
    loss = rest[0]

    delta, new_m, new_v = {}, {}, {}
    for n in BIG_NAMES:
        lanes = w[n].shape[-1]
        outs = _adamw(_as_rows(w[n], lanes), _as_rows(g[n], lanes), _as_rows(m[n], lanes), _as_rows(v[n], lanes), "adamw_" + n)
        delta[n], new_m[n], new_v[n] = [o.reshape(w[n].shape) for o in outs]
    swap = lambda n, a: jnp.swapaxes(a, -1, -2) if n in ("ssm_b_re", "ssm_b_im") else a
    for group, per_layer in ((SMALL_4D, True), (tuple(n for n in SMALL_NAMES if n not in SMALL_4D), False)):
        outs = _adamw_many(*[[swap(n, d[n]) for n in group] for d in (w, g, m, v)],
                           "adamw_small_4d" if per_layer else "adamw_small", per_layer)
        for d, o in zip((delta, new_m, new_v), outs):
            d.update({n: swap(n, a) for n, a in zip(group, o)})

    return (loss, dx[None], *[g[n] for n in names], *[delta[n] for n in names],
            *[new_m[n] for n in names], *[new_v[n] for n in names])
```

```python
import functools
import math

import jax
import jax.numpy as jnp
from jax import lax
from jax.experimental import pallas as pl
from jax.experimental.pallas import tpu as pltpu

F32 = jnp.float32
BF16 = jnp.bfloat16

D_MODEL = 1024
N_LAYERS = 2
N_CHIPS = 4
IN_COLS = 3072
IN_SHARD = IN_COLS // N_CHIPS
SSM_WIDTH = 512
SSM_GROUP = 16
SSM_GROUPS = 32
SSM_STATE = 64
N_STATES = SSM_GROUPS * SSM_STATE
SSM_CHUNKS = 4
CH_W = SSM_WIDTH // SSM_CHUNKS
CH_S = N_STATES // SSM_CHUNKS
ATTN_WIDTH = 512
HEAD_DIM = 64
PLE_DIM = 256
ROW_SHARD = 256
RMS_EPS = 1e-6
ATTN_SCALE = HEAD_DIM ** -0.5
ATTN_BLOCK = 128
EXP_ZERO = -87.5
SUBLANES = 8
SCAN_TILE = 1024
V7X_VMEM_LIMIT = 52 * 1024 * 1024

ADAM_LR = 0.001
ADAM_B1 = 0.9
ADAM_B2 = 0.999
ADAM_EPS = 1e-08
ADAM_WD = 0.01
ADAM_STEP = 10

MESH = pl.DeviceIdType.MESH
ANY = pl.BlockSpec(memory_space=pl.ANY)


def _cparams(n_grid=0, parallel=0):
    sem = tuple(["parallel"] * parallel + ["arbitrary"] * (n_grid - parallel))
    return pltpu.CompilerParams(dimension_semantics=sem, vmem_limit_bytes=V7X_VMEM_LIMIT)


def _dot(a, b):
    return jnp.dot(a, b, preferred_element_type=F32)


def _dot_nt(a, b):
    return lax.dot_general(a, b, (((1,), (1,)), ((), ())), preferred_element_type=F32)


def _dot_tn(a, b):
    return lax.dot_general(a, b, (((0,), (0,)), ((), ())), preferred_element_type=F32)


def _split_hilo(a):
    hi = a.astype(BF16)
    lo = (a - hi.astype(F32)).astype(BF16)
    return hi, lo


def _dot_hilo(a, b):
    hi, lo = _split_hilo(a)
    return _dot(hi, b) + _dot(lo, b)


def _sigmoid(x):
    return 0.5 * (jnp.tanh(0.5 * x) + 1.0)


_GELU_C = math.sqrt(2.0 / math.pi)


def _gelu(x):
    return 0.5 * x * (1.0 + jnp.tanh(_GELU_C * (x + 0.044715 * (x * x * x))))


def _gelu_grad(x):
    t = jnp.tanh(_GELU_C * (x + 0.044715 * (x * x * x)))
    return 0.5 * (1.0 + t) + 0.5 * x * (1.0 - t * t) * (_GELU_C * (1.0 + 3.0 * 0.044715 * (x * x)))


def _row_tile(s, want):
    for t in range(min(s, want), 7, -1):
        if s % t == 0 and t % SUBLANES == 0:
            return t
    return s


def _coords():
    return lax.axis_index("x"), lax.axis_index("y"), lax.axis_index("c")


def _other_chips(x, y):
    return [(1 - x, y), (x, 1 - y), (1 - x, 1 - y)]


def _remote(src, dst, send_sem, recv_sem, dev):
    return pltpu.make_async_remote_copy(src_ref=src, dst_ref=dst, send_sem=send_sem, recv_sem=recv_sem,
                                        device_id=dev, device_id_type=MESH)


def _set_block(buf, block, index):
    return lax.dynamic_update_index_in_dim(buf, block, index, 0)


def _gather_sems(n):
    return [pltpu.SemaphoreType.DMA((3 * n,)) for _ in range(4)]


def _gather_copies(ins, bases, outs, sems):
    send_sems, recv_sems, fwd_send, fwd_recv = sems
    x, y, c = _coords()
    me_chip = 2 * x + y
    sibling = (x, y, 1 - c)
    first, landed, passed, from_sibling = [], [], [], []
    for k in range(len(ins)):
        for j, (cx, cy) in enumerate(_other_chips(x, y)):
            i = 3 * k + j
            first.append(_remote(ins[k].at[bases[k] + c], outs[k].at[me_chip, c], send_sems.at[i], recv_sems.at[i], (cx, cy, c)))
            blk = outs[k].at[2 * cx + cy, c]
            landed.append(_remote(blk, blk, send_sems.at[i], recv_sems.at[i], (cx, cy, c)))
            passed.append(_remote(blk, blk, fwd_send.at[i], fwd_recv.at[i], sibling))
            blk = outs[k].at[2 * cx + cy, 1 - c]
            from_sibling.append(_remote(blk, blk, fwd_send.at[i], fwd_recv.at[i], sibling))
    return first, landed, passed, from_sibling


def _gather_start(ins, bases, outs, sems):
    for cp in _gather_copies(ins, bases, outs, sems)[0]:
        cp.start()


def _gather_finish(ins, bases, outs, sems):
    first, landed, passed, from_sibling = _gather_copies(ins, bases, outs, sems)
    for arrived, forward in zip(landed, passed):
        arrived.wait_recv()
        forward.start()
    for cp in from_sibling:
        cp.wait_recv()
    for cp in first + passed:
        cp.wait_send()


def _gather_outputs(arrs):
    return [jax.ShapeDtypeStruct((N_CHIPS, 2) + a.shape[1:], a.dtype) for a in arrs]


def _gather_own(outs, arrs, bases):
    me_chip = 2 * lax.axis_index("x") + lax.axis_index("y")
    return [_set_block(o, lax.slice_in_dim(a, b, b + 2, axis=0), me_chip) for o, a, b in zip(outs, arrs, bases)]


def _chip_gather(arrs, name, bases=None):
    n = len(arrs)
    bases = [0] * n if bases is None else bases

    def body(*refs):
        ins, outs, sems = refs[:n], refs[n:2 * n], refs[2 * n:]
        _gather_start(ins, bases, outs, sems)
        _gather_finish(ins, bases, outs, sems)

    outs = pl.pallas_call(
        body, name=name, out_shape=_gather_outputs(arrs),
        in_specs=[ANY] * n, out_specs=[ANY] * n, scratch_shapes=_gather_sems(n),
    )(*arrs)
    return _gather_own(outs, arrs, bases)


def _sibling_push(arrs, name):
    n = len(arrs)

    def body(*refs):
        ins, outs = refs[:n], refs[n:2 * n]
        send_sems, recv_sems = refs[2 * n:]
        x, y, c = _coords()
        cps = [_remote(ins[k].at[pl.ds(0, N_CHIPS), 1 - c], outs[k], send_sems.at[k], recv_sems.at[k], (x, y, 1 - c))
               for k in range(n)]
        for cp in cps:
            cp.start()
        for cp in cps:
            cp.wait_recv()
        for cp in cps:
            cp.wait_send()

    return pl.pallas_call(
        body, name=name,
        out_shape=[jax.ShapeDtypeStruct((a.shape[0],) + a.shape[2:], a.dtype) for a in arrs],
        in_specs=[ANY] * n, out_specs=[ANY] * n,
        scratch_shapes=[pltpu.SemaphoreType.DMA((n,)), pltpu.SemaphoreType.DMA((n,))],
    )(*arrs)


def _sibling_join(pieces, out_shapes, name):
    n = len(pieces)
    no = len(out_shapes)

    def body(*refs):
        ins, outs = refs[:n], refs[n:n + no]
        send_sems, recv_sems = refs[n + no:]
        x, y, c = _coords()
        sibling = (x, y, 1 - c)
        cps = [_remote(ins[k], outs[o].at[lead + (c,)], send_sems.at[k], recv_sems.at[k], sibling)
               for k, (_, o, lead) in enumerate(pieces)]
        for cp in cps:
            cp.start()
        for k, (_, o, lead) in enumerate(pieces):
            blk = outs[o].at[lead + (1 - c,)]
            _remote(blk, blk, send_sems.at[k], recv_sems.at[k], sibling).wait_recv()
        for cp in cps:
            cp.wait_send()

    outs = pl.pallas_call(
        body, name=name,
        out_shape=[jax.ShapeDtypeStruct(sh, F32) for sh in out_shapes],
        in_specs=[ANY] * n, out_specs=[ANY] * no,
        scratch_shapes=[pltpu.SemaphoreType.DMA((n,)), pltpu.SemaphoreType.DMA((n,))],
    )(*[a for a, _, _ in pieces])
    outs = list(outs)
    c = lax.axis_index("c")
    for a, o, lead in pieces:
        block = a.reshape((1,) * (len(lead) + 1) + a.shape)
        outs[o] = lax.dynamic_update_slice(outs[o], block, lead + (c,) + (0,) * a.ndim)
    return outs


def _scatter_sems(n):
    return [pltpu.SemaphoreType.DMA((3 * n,)), pltpu.SemaphoreType.DMA((3 * n,))]


def _scatter_copies(ins, outs, sems):
    send_sems, recv_sems = sems
    x, y, c = _coords()
    me_chip = 2 * x + y
    sends, arrivals = [], []
    for k in range(len(ins)):
        for j, (cx, cy) in enumerate(_other_chips(x, y)):
            i = 3 * k + j
            sends.append(_remote(ins[k].at[2 * cx + cy], outs[k].at[me_chip], send_sems.at[i], recv_sems.at[i], (cx, cy, c)))
            blk = outs[k].at[2 * cx + cy]
            arrivals.append(_remote(blk, blk, send_sems.at[i], recv_sems.at[i], (cx, cy, c)))
    return sends, arrivals


def _scatter_start(ins, outs, sems):
    for cp in _scatter_copies(ins, outs, sems)[0]:
        cp.start()


def _scatter_finish(ins, outs, sems):
    sends, arrivals = _scatter_copies(ins, outs, sems)
    for cp in arrivals:
        cp.wait_recv()
    for cp in sends:
        cp.wait_send()


def _scatter_own(outs, arrs):
    me_chip = 2 * lax.axis_index("x") + lax.axis_index("y")
    return [_set_block(o, lax.dynamic_index_in_dim(a, me_chip, 0, keepdims=False), me_chip) for o, a in zip(outs, arrs)]


def _chip_scatter(arrs, name):
    n = len(arrs)

    def body(*refs):
        ins, outs, sems = refs[:n], refs[n:2 * n], refs[2 * n:]
        _scatter_start(ins, outs, sems)
        _scatter_finish(ins, outs, sems)

    outs = pl.pallas_call(
        body, name=name,
        out_shape=[jax.ShapeDtypeStruct(a.shape, a.dtype) for a in arrs],
        in_specs=[ANY] * n, out_specs=[ANY] * n, scratch_shapes=_scatter_sems(n),
    )(*arrs)
    return _scatter_own(outs, arrs)


def _as_rows(a, lanes):
    return a.reshape(-1, lanes)


def _add_my_half(v, recv, out_dtype, name):
    n_sh, _, h, cdim = v.shape
    tr = _row_tile(h, 512)

    def body(c_ref, a_ref, b_ref, o_ref):
        o_ref[...] = (a_ref[...].astype(F32) + b_ref[...].astype(F32)).astype(out_dtype)

    c = lax.axis_index("c").astype(jnp.int32).reshape(1)
    return pl.pallas_call(
        body, name=name,
        grid_spec=pltpu.PrefetchScalarGridSpec(
            num_scalar_prefetch=1, grid=(n_sh, h // tr),
            in_specs=[pl.BlockSpec((None, None, tr, cdim), lambda sh, i, c_ref: (sh, c_ref[0], i, 0)),
                      pl.BlockSpec((None, tr, cdim), lambda sh, i, c_ref: (sh, i, 0))],
            out_specs=pl.BlockSpec((None, tr, cdim), lambda sh, i, c_ref: (sh, i, 0))),
        out_shape=jax.ShapeDtypeStruct((n_sh, h, cdim), out_dtype),
        compiler_params=_cparams(2),
    )(c, v, recv)


def _sum4(parts, name):
    _, r, cdim = parts.shape
    tr = _row_tile(r, 512)

    def body(p_ref, o_ref):
        acc = p_ref[0].astype(F32) + p_ref[1].astype(F32)
        acc = acc + p_ref[2].astype(F32)
        o_ref[...] = acc + p_ref[3].astype(F32)

    return pl.pallas_call(
        body, name=name, grid=(r // tr,),
        in_specs=[pl.BlockSpec((N_CHIPS, tr, cdim), lambda i: (0, i, 0))],
        out_specs=pl.BlockSpec((tr, cdim), lambda i: (i, 0)),
        out_shape=jax.ShapeDtypeStruct((r, cdim), F32),
        compiler_params=_cparams(1),
    )(parts)


def _adamw_math(w, g, m, v):
    c1 = 1.0 - ADAM_B1 ** ADAM_STEP
    c2 = 1.0 - ADAM_B2 ** ADAM_STEP
    nm = ADAM_B1 * m + (1.0 - ADAM_B1) * g
    nv = ADAM_B2 * v + (1.0 - ADAM_B2) * (g * g)
    delta = -ADAM_LR * ((nm / c1) / (jnp.sqrt(nv / c2) + ADAM_EPS) + ADAM_WD * w)
    return delta, nm, nv


def _adamw(w, g, m, v, name):
    r, cdim = w.shape
    tr = _row_tile(r, 256)

    def body(w_ref, g_ref, m_ref, v_ref, d_ref, nm_ref, nv_ref):
        d_ref[...], nm_ref[...], nv_ref[...] = _adamw_math(w_ref[...], g_ref[...], m_ref[...], v_ref[...])

    spec = pl.BlockSpec((tr, cdim), lambda i: (i, 0))
    return pl.pallas_call(
        body, name=name, grid=(r // tr,),
        in_specs=[spec] * 4, out_specs=[spec] * 3,
        out_shape=[jax.ShapeDtypeStruct((r, cdim), F32)] * 3,
        compiler_params=_cparams(1),
    )(w, g, m, v)


def _adamw_many(ws, gs, ms, vs, name, per_layer):
    n = len(ws)

    def body(*refs):
        for k in range(n):
            w, g, m, v = (refs[j * n + k][...] for j in range(4))
            outs = _adamw_math(w, g, m, v)
            for j in range(3):
                refs[(4 + j) * n + k][...] = outs[j]

    shapes = [jax.ShapeDtypeStruct(w.shape, F32) for w in ws]
    if per_layer:
        specs = [pl.BlockSpec((None,) + w.shape[1:], lambda l, nd=w.ndim: (l,) + (0,) * (nd - 1)) for w in ws]
        call = pl.pallas_call(body, name=name, grid=(N_LAYERS,), in_specs=specs * 4, out_specs=specs * 3,
                              out_shape=shapes * 3, compiler_params=_cparams(1))
    else:
        call = pl.pallas_call(body, name=name, out_shape=shapes * 3, compiler_params=_cparams())
    outs = call(*ws, *gs, *ms, *vs)
    return outs[0:n], outs[n:2 * n], outs[2 * n:3 * n]


def _cmul(ar, ai, br, bi):
    return ar * br - ai * bi, ar * bi + ai * br


def _discretise(a_re, a_im, log_dt, b_re, b_im):
    dt = jnp.exp(log_dt)
    mag = jnp.exp(a_re * dt)
    ab_re = mag * jnp.cos(a_im * dt)
    ab_im = mag * jnp.sin(a_im * dt)
    num_re = ab_re - 1.0
    num_im = ab_im
    den = a_re * a_re + a_im * a_im
    f_re = (num_re * a_re + num_im * a_im) / den
    f_im = (num_im * a_re - num_re * a_im) / den
    bb_re = f_re * b_re - f_im * b_im
    bb_im = f_re * b_im + f_im * b_re
    return ab_re, ab_im, bb_re, bb_im


def _disc_shapes():
    col = jax.ShapeDtypeStruct((1, N_STATES), F32)
    mat = jax.ShapeDtypeStruct((SSM_GROUP, N_STATES), F32)
    return col, mat


def _group_mask():
    row = lax.broadcasted_iota(jnp.int32, (CH_W, CH_S), 0)
    col = lax.broadcasted_iota(jnp.int32, (CH_W, CH_S), 1)
    return jnp.right_shift(row, SSM_GROUP.bit_length() - 1) == jnp.right_shift(col, SSM_STATE.bit_length() - 1)


def _block_diag(v, j):
    blk = v[:, CH_S * j:CH_S * (j + 1)]
    return jnp.where(_group_mask(), jnp.concatenate([blk] * (CH_W // SSM_GROUP), axis=0), 0.0)


def _block_diag_t(m):
    kept = jnp.where(_group_mask(), m, 0.0)
    return kept.reshape(CH_W // SSM_GROUP, SSM_GROUP, CH_S).sum(axis=0)


def _disc_fwd(a_re, a_im, log_dt, b_re, b_im, c_re, c_im, length):
    wide = jax.ShapeDtypeStruct((SSM_CHUNKS, CH_W, 2 * CH_S), BF16)
    tall = jax.ShapeDtypeStruct((SSM_CHUNKS, 2 * CH_S, CH_W), BF16)
    tab = jax.ShapeDtypeStruct((SSM_CHUNKS, length, 2 * CH_S), F32)

    def body(ar, ai, ld, br, bi, cr, ci, wb_ref, wbt_ref, wct_ref, wc_ref, tab_ref, rev_ref):
        ab_re, ab_im, bb_re, bb_im = _discretise(ar[...], ai[...], ld[...], br[...], bi[...])
        ccr, cci = cr[...], -ci[...]
        for j in range(SSM_CHUNKS):
            for lo, (vb, vc) in ((0, (bb_re, ccr)), (CH_S, (bb_im, cci))):
                mb, mc = _block_diag(vb, j), _block_diag(vc, j)
                wb_ref[j, :, lo:lo + CH_S] = mb.astype(BF16)
                wbt_ref[j, lo:lo + CH_S, :] = mb.T.astype(BF16)
                wct_ref[j, :, lo:lo + CH_S] = mc.astype(BF16)
                wc_ref[j, lo:lo + CH_S, :] = mc.T.astype(BF16)

        def step(j, carry):
            pr, pi = carry
            back = length - 1 - j
            for c in range(SSM_CHUNKS):
                lanes = slice(CH_S * c, CH_S * (c + 1))
                tab_ref[c, pl.ds(j, 1), 0:CH_S] = pr[:, lanes]
                tab_ref[c, pl.ds(j, 1), CH_S:2 * CH_S] = pi[:, lanes]
                rev_ref[c, pl.ds(back, 1), 0:CH_S] = pr[:, lanes]
                rev_ref[c, pl.ds(back, 1), CH_S:2 * CH_S] = -pi[:, lanes]
            return _cmul(pr, pi, ab_re, ab_im)

        lax.fori_loop(0, length, step, (ab_re, ab_im))

    return pl.pallas_call(body, name="ssm_discretise", out_shape=[wide, tall, wide, tall, tab, tab],
                          compiler_params=_cparams())(a_re, a_im, log_dt, b_re, b_im, c_re, c_im)


def _disc_bwd(a_re, a_im, log_dt, b_re, b_im, da, dwb, dwc):
    col, mat = _disc_shapes()

    def body(ar, ai, ld, br, bi, da_ref, dwb_ref, dwc_ref, o0, o1, o2, o3, o4, dcr_ref, dci_ref):
        g_ab = [jnp.concatenate([jnp.sum(da_ref[j, :, lo:lo + CH_S], axis=0, keepdims=True) for j in range(SSM_CHUNKS)],
                                axis=-1) for lo in (0, CH_S)]
        g_bb = [jnp.concatenate([_block_diag_t(dwb_ref[j, :, lo:lo + CH_S]) for j in range(SSM_CHUNKS)], axis=-1)
                for lo in (0, CH_S)]
        for ref, lo, sign in ((dcr_ref, 0, 1.0), (dci_ref, CH_S, -1.0)):
            ref[...] = sign * jnp.concatenate([_block_diag_t(dwc_ref[j, lo:lo + CH_S, :].T) for j in range(SSM_CHUNKS)],
                                              axis=-1)
        _, vjp = jax.vjp(_discretise, ar[...], ai[...], ld[...], br[...], bi[...])
        grads = vjp((g_ab[0], g_ab[1], g_bb[0], g_bb[1]))
        for o, val in zip((o0, o1, o2, o3, o4), grads):
            o[...] = val

    return pl.pallas_call(body, name="ssm_discretise_bwd", out_shape=[col, col, col, mat, mat, mat, mat],
                          compiler_params=_cparams())(a_re, a_im, log_dt, b_re, b_im, da, dwb, dwc)


def _interleave_chunks(v):
    rows, width = v.shape
    return pltpu.einshape("cjw->jcw", v.reshape(SUBLANES, rows // SUBLANES, width)).reshape(rows, width)


def _time_order(v):
    rows, width = v.shape
    return pltpu.einshape("jcw->cjw", v.reshape(rows // SUBLANES, SUBLANES, width)).reshape(rows, width)


def _head_ones():
    r = jnp.arange(ATTN_WIDTH) // HEAD_DIM
    return jnp.where(r[:, None] == r[None, :], 1.0 / HEAD_DIM, 0.0).astype(BF16)


def _in_proj(h, g1, w_in_l, qg, kg):
    s = h.shape[0]
    tm = _row_tile(s, 512)

    def body(h_ref, g_ref, w_ref, qg_ref, kg_ref, ones_ref, proj_ref, qkv_ref):
        x = h_ref[...]
        r = lax.rsqrt(jnp.mean(x * x, axis=-1, keepdims=True) + RMS_EPS)
        hn = (x * r * g_ref[...]).astype(BF16)
        for sh in range(N_CHIPS):
            proj_ref[:, IN_SHARD * sh:IN_SHARD * (sh + 1)] = _dot(hn, w_ref[sh])
        ones = ones_ref[...]
        q = proj_ref[:, 1024:1536]
        k = proj_ref[:, 1536:2048]
        rq = lax.rsqrt(_dot_hilo(q * q, ones) + RMS_EPS)
        rk = lax.rsqrt(_dot_hilo(k * k, ones) + RMS_EPS)
        qkv_ref[:, 0:512] = (q * rq * qg_ref[...] * ATTN_SCALE).astype(BF16)
        qkv_ref[:, 512:1024] = (k * rk * kg_ref[...]).astype(BF16)
        qkv_ref[:, 1024:1536] = proj_ref[:, 2048:2560].astype(BF16)

    full = lambda shape: pl.BlockSpec(shape, lambda i: (0,) * len(shape))
    return pl.pallas_call(
        body, name="in_proj", grid=(s // tm,),
        in_specs=[pl.BlockSpec((tm, D_MODEL), lambda i: (i, 0)), full((1, D_MODEL)),
                  full((N_CHIPS, D_MODEL, IN_SHARD)),
                  full((1, ATTN_WIDTH)), full((1, ATTN_WIDTH)), full((ATTN_WIDTH, ATTN_WIDTH))],
        out_specs=[pl.BlockSpec((tm, IN_COLS), lambda i: (i, 0)), pl.BlockSpec((tm, 3 * ATTN_WIDTH), lambda i: (i, 0))],
        out_shape=[jax.ShapeDtypeStruct((s, IN_COLS), F32), jax.ShapeDtypeStruct((s, 3 * ATTN_WIDTH), BF16)],
        compiler_params=_cparams(1),
    )(h, g1, w_in_l, qg, kg, _head_ones())


def _row_bcast(ref, k, lo):
    return jnp.broadcast_to(ref[pl.ds(k, 1), lo:lo + CH_S], (SUBLANES, CH_S))


def _chunk_scan(x_ref, tab_ref, carry_ref, length, reverse, tail=None):
    row = lax.broadcasted_iota(jnp.int32, (SUBLANES, CH_S), 0)
    one, full = (length - 1, 0) if reverse else (0, length - 1)
    ar, ai = _row_bcast(tab_ref, one, 0), _row_bcast(tab_ref, one, CH_S)
    fr, fi = _row_bcast(tab_ref, full, 0), _row_bcast(tab_ref, full, CH_S)
    step = lambda jj: (length - 1 - jj) if reverse else jj

    def local(jj, carry):
        cr, ci = carry
        r0 = pl.multiple_of(step(jj) * SUBLANES, SUBLANES)
        xr = x_ref[pl.ds(r0, SUBLANES), 0:CH_S] + (ar * cr - ai * ci)
        xi = x_ref[pl.ds(r0, SUBLANES), CH_S:2 * CH_S] + (ar * ci + ai * cr)
        x_ref[pl.ds(r0, SUBLANES), 0:CH_S] = xr
        x_ref[pl.ds(r0, SUBLANES), CH_S:2 * CH_S] = xi
        return xr, xi

    zero = jnp.zeros((SUBLANES, CH_S), F32)
    er, ei = lax.fori_loop(0, length, local, (zero, zero))

    first, shift = (SUBLANES - 1, SUBLANES - 1) if reverse else (0, 1)
    hr = jnp.where(row == first, carry_ref[:, 0:CH_S], 0.0)
    hi = jnp.where(row == first, carry_ref[:, CH_S:2 * CH_S], 0.0)
    sr, si = pltpu.roll(er, shift, 0), pltpu.roll(ei, shift, 0)
    for k in range(1, SUBLANES):
        tr, ti = pltpu.roll(hr, shift, 0), pltpu.roll(hi, shift, 0)
        here = row == ((SUBLANES - 1 - k) if reverse else k)
        hr, hi = (jnp.where(here, fr * tr - fi * ti + sr, hr), jnp.where(here, fr * ti + fi * tr + si, hi))
    last = 0 if reverse else SUBLANES - 1
    outr, outi = fr * hr - fi * hi + er, fr * hi + fi * hr + ei
    carry_ref[:, 0:CH_S] = jnp.broadcast_to(outr[last:last + 1, :], (SUBLANES, CH_S))
    carry_ref[:, CH_S:2 * CH_S] = jnp.broadcast_to(outi[last:last + 1, :], (SUBLANES, CH_S))

    def fix(jj, carry):
        j = step(jj)
        r0 = pl.multiple_of(j * SUBLANES, SUBLANES)
        pr, pi = _row_bcast(tab_ref, j, 0), _row_bcast(tab_ref, j, CH_S)
        xr = x_ref[pl.ds(r0, SUBLANES), 0:CH_S] + (pr * hr - pi * hi)
        xi = x_ref[pl.ds(r0, SUBLANES), CH_S:2 * CH_S] + (pr * hi + pi * hr)
        x_ref[pl.ds(r0, SUBLANES), 0:CH_S] = xr
        x_ref[pl.ds(r0, SUBLANES), CH_S:2 * CH_S] = xi
        if tail is None:
            return carry
        return tail(r0, xr, xi, carry)

    return fix, (hr, hi)


def _ssm_scan_fwd(proj, wb, tab, wc, gather=None, gather_bases=None):
    s = proj.shape[0]
    tm = _row_tile(s, SCAN_TILE)
    nt = s // tm
    length = tm // SUBLANES
    gather = [] if gather is None else gather
    ng = len(gather)

    def body(*refs):
        u_ref, wb_ref, tab_ref, wc_ref = refs[0:4]
        g_ins = refs[4:4 + ng]
        xs_ref, y_ref = refs[4 + ng:6 + ng]
        g_outs = refs[6 + ng:6 + 2 * ng]
        carry_ref = refs[6 + 2 * ng]
        sems = refs[7 + 2 * ng:]
        j, i = pl.program_id(0), pl.program_id(1)

        @pl.when(i == 0)
        def _():
            carry_ref[...] = jnp.zeros_like(carry_ref)

        if ng:
            @pl.when(jnp.logical_and(j == 0, i == 0))
            def _():
                _gather_start(g_ins, gather_bases, g_outs, sems)

        xs_ref[...] = _dot(_interleave_chunks(u_ref[...]).astype(BF16), wb_ref[...])
        fix, start = _chunk_scan(xs_ref, tab_ref, carry_ref, length, reverse=False)
        lax.fori_loop(0, length, fix, start, unroll=2)
        y_ref[...] = _time_order(_dot(xs_ref[...].astype(BF16), wc_ref[...]))

        if ng:
            @pl.when(jnp.logical_and(j == SSM_CHUNKS - 1, i == nt - 1))
            def _():
                _gather_finish(g_ins, gather_bases, g_outs, sems)

    outs = pl.pallas_call(
        body, name="ssm_scan_gather" if ng else "ssm_scan", grid=(SSM_CHUNKS, nt),
        in_specs=[pl.BlockSpec((tm, CH_W), lambda j, i: (i, j)),
                  pl.BlockSpec((None, CH_W, 2 * CH_S), lambda j, i: (j, 0, 0)),
                  pl.BlockSpec((None, length, 2 * CH_S), lambda j, i: (j, 0, 0)),
                  pl.BlockSpec((None, 2 * CH_S, CH_W), lambda j, i: (j, 0, 0))] + [ANY] * ng,
        out_specs=[pl.BlockSpec((None, tm, 2 * CH_S), lambda j, i: (j, i, 0)),
                   pl.BlockSpec((tm, CH_W), lambda j, i: (i, j))] + [ANY] * ng,
        out_shape=[jax.ShapeDtypeStruct((SSM_CHUNKS, s, 2 * CH_S), F32), jax.ShapeDtypeStruct((s, SSM_WIDTH), F32)]
        + _gather_outputs(gather),
        scratch_shapes=[pltpu.VMEM((SUBLANES, 2 * CH_S), F32)] + (_gather_sems(ng) if ng else []),
        compiler_params=_cparams(2),
    )(proj, wb, tab, wc, *gather)
    return outs[0], outs[1], (_gather_own(outs[2:], gather, gather_bases) if ng else [])


def _glu_forward(y, u, d, wg_ref, bg):
    yf = y + d * u
    z = _gelu(yf)
    zb = z.astype(BF16)
    zz = jnp.concatenate([_dot(zb, wg_ref[sh]) for sh in range(N_CHIPS)], axis=-1) + bg
    return yf, z, zz[:, 0:SSM_WIDTH], zz[:, SSM_WIDTH:2 * SSM_WIDTH]


def _ssm_glu_fwd(y, proj, d, w_glu_l, b_glu):
    s = y.shape[0]
    tm = _row_tile(s, 512)

    def body(y_ref, u_ref, gs_ref, d_ref, wg_ref, bg_ref, o_ref):
        _, _, val, gate = _glu_forward(y_ref[...], u_ref[...], d_ref[...], wg_ref, bg_ref[...])
        gs = gs_ref[...]
        o_ref[...] = val * _sigmoid(gate) * (gs * _sigmoid(gs))

    row = lambda i: (i, 0)
    return pl.pallas_call(
        body, name="ssm_glu", grid=(s // tm,),
        in_specs=[pl.BlockSpec((tm, SSM_WIDTH), row), pl.BlockSpec((tm, SSM_WIDTH), row),
                  pl.BlockSpec((tm, SSM_WIDTH), lambda i: (i, 1)), pl.BlockSpec((1, SSM_WIDTH), lambda i: (0, 0)),
                  pl.BlockSpec((N_CHIPS, SSM_WIDTH, ROW_SHARD), lambda i: (0, 0, 0)),
                  pl.BlockSpec((1, 2 * SSM_WIDTH), lambda i: (0, 0))],
        out_specs=pl.BlockSpec((tm, SSM_WIDTH), row),
        out_shape=jax.ShapeDtypeStruct((s, SSM_WIDTH), F32),
        compiler_params=_cparams(1),
    )(y, proj, proj, d, w_glu_l, b_glu)


def _tri(kind):
    r = jnp.arange(ATTN_BLOCK)
    if kind == "suffix_incl":
        m = r[:, None] >= r[None, :]
    else:
        m = r[:, None] < r[None, :]
    return jnp.concatenate([m, jnp.ones_like(m)], axis=1).astype(BF16)


def _head_masks():
    lane = lax.broadcasted_iota(jnp.int32, (1, 2 * HEAD_DIM), 1)
    return [lane < HEAD_DIM, lane >= HEAD_DIM]


def _chain_step(t, base, n_sub, first, q_ref, k_ref, tri_ref, l_scr, per_chain):
    tb = ATTN_BLOCK
    row = lax.broadcasted_iota(jnp.int32, (tb, tb), 0)
    col = lax.broadcasted_iota(jnp.int32, (tb, tb), 1)
    masks = _head_masks()
    blks = [base + a - t for a in range(n_sub)]
    r0s = [pl.multiple_of(jnp.maximum(blk, 0) * tb, tb) for blk in blks]
    zs = []
    for a in range(n_sub):
        kb = k_ref[pl.ds(r0s[a], tb), :]
        qa = q_ref[a * tb:(a + 1) * tb, :]
        for mask in masks:
            zs.append(_dot_nt(jnp.where(mask, qa, jnp.zeros_like(qa)), kb))
    parts = []
    for z in zs:
        ls = jnp.minimum(-z, 0.0) - jnp.log(1.0 + jnp.exp(-jnp.abs(z)))
        if first:
            ls = jnp.where(col < row, ls, 0.0)
        parts.append(_split_hilo(ls))
    tri = tri_ref[...]
    sums = [_dot(hi, tri) + _dot(lo, tri) for hi, lo in parts]
    top = None
    ws = []
    for c, (z, sm) in enumerate(zip(zs, sums)):
        if first:
            lsum = jnp.zeros((tb, tb), F32)
        else:
            lsum = l_scr[c] + jnp.where(blks[c // 2] >= 0, 0.0, -1e30)
        w = jnp.exp(z + sm[:, 0:tb] + lsum)
        if first:
            w = jnp.where(col < row, w, 0.0)
        ws.append(w)
        lsum = lsum + sm[:, tb:2 * tb]
        l_scr[c] = lsum
        top = lsum if top is None else jnp.maximum(top, lsum)
    for c, (z, w) in enumerate(zip(zs, ws)):
        per_chain(c // 2, c % 2, c, r0s[c // 2], z, w)
    return jnp.max(top)


def _chain_sweep(base, n_sub, q_ref, k_ref, tri_ref, l_scr, per_chain):
    top = _chain_step(0, base, n_sub, True, q_ref, k_ref, tri_ref, l_scr, functools.partial(per_chain, 0))

    def cond(carry):
        t, top = carry
        return jnp.logical_and(t <= base + n_sub - 1, top > EXP_ZERO)

    def step(carry):
        t, _ = carry
        return t + 1, _chain_step(t, base, n_sub, False, q_ref, k_ref, tri_ref, l_scr, functools.partial(per_chain, t))

    steps, _ = lax.while_loop(cond, step, (jnp.int32(1), top))
    return steps


ATTN_SUB_FWD = 8
ATTN_SUB_BWD = 4


def _attn_fwd(qkv, proj, gather=None, gather_bases=None):
    s = qkv.shape[0]
    tb = ATTN_BLOCK
    n_sub = min(ATTN_SUB_FWD, s // tb)
    tq = n_sub * tb
    n_hp = ATTN_WIDTH // (2 * HEAD_DIM)
    gather = [] if gather is None else gather
    ng = len(gather)

    def body(*refs):
        q_ref, k_ref, v_ref, g_ref, tri_ref = refs[0:5]
        g_ins = refs[5:5 + ng]
        o_ref, ya_ref = refs[5 + ng:7 + ng]
        g_outs = refs[7 + ng:7 + 2 * ng]
        l_scr = refs[7 + 2 * ng]
        sems = refs[8 + 2 * ng:]
        i = pl.program_id(1)
        masks = _head_masks()
        o_ref[...] = jnp.zeros_like(o_ref)

        if ng:
            @pl.when(jnp.logical_and(pl.program_id(0) == 0, i == 0))
            def _():
                _gather_start(g_ins, gather_bases, g_outs, sems)

        def per_chain(t, a, h, c, r0, z, w):
            vb = v_ref[pl.ds(r0, tb), :]
            vb = jnp.where(masks[h], vb, jnp.zeros_like(vb))
            o_ref[a * tb:(a + 1) * tb, :] += _dot(w.astype(BF16), vb)

        _chain_sweep(i * n_sub, n_sub, q_ref, k_ref, tri_ref, l_scr, per_chain)
        g = g_ref[...]
        ya_ref[...] = o_ref[...] * (g * _sigmoid(g))

        if ng:
            @pl.when(jnp.logical_and(pl.program_id(0) == n_hp - 1, i == s // tq - 1))
            def _():
                _gather_finish(g_ins, gather_bases, g_outs, sems)

    hp_blk = lambda off: pl.BlockSpec((tq, 2 * HEAD_DIM), lambda hp, i: (i, off + hp))
    res = lambda off: pl.BlockSpec((s, 2 * HEAD_DIM), lambda hp, i: (0, off + hp))
    outs = pl.pallas_call(
        body, name="attn_fwd_gather" if ng else "attn_fwd", grid=(n_hp, s // tq),
        in_specs=[hp_blk(0), res(4), res(8), hp_blk(20), pl.BlockSpec((tb, 2 * tb), lambda hp, i: (0, 0))] + [ANY] * ng,
        out_specs=[hp_blk(0), hp_blk(0)] + [ANY] * ng,
        out_shape=[jax.ShapeDtypeStruct((s, ATTN_WIDTH), F32)] * 2 + _gather_outputs(gather),
        scratch_shapes=[pltpu.VMEM((2 * n_sub, tb, tb), F32)] + (_gather_sems(ng) if ng else []),
        compiler_params=_cparams(2),
    )(qkv, qkv, qkv, proj, _tri("suffix_incl"), *gather)
    return outs[0], outs[1], (_gather_own(outs[2:], gather, gather_bases) if ng else [])


def _rms_rows(x, g):
    r = lax.rsqrt(jnp.mean(x * x, axis=-1, keepdims=True) + RMS_EPS)
    return r, x * r * g


def _ple_forward(h1, p, g2, wpg_ref, wpp_ref):
    r2, hn2 = _rms_rows(h1, g2)
    hb = hn2.astype(BF16)
    gpre = _dot(hb[:, 0:ROW_SHARD], wpg_ref[0])
    for sh in range(1, N_CHIPS):
        gpre = gpre + _dot(hb[:, ROW_SHARD * sh:ROW_SHARD * (sh + 1)], wpg_ref[sh])
    gate = _sigmoid(gpre)
    pb = p.astype(BF16)
    pp = jnp.concatenate([_dot(pb, wpp_ref[sh]) for sh in range(N_CHIPS)], axis=-1)
    return r2, hb, gate, pp


def _colsum8(a):
    t = a.shape[0]
    return a.reshape(t // SUBLANES, SUBLANES, a.shape[1]).sum(axis=0)


def _sq_err_grad(y, target):
    e = y - target
    sq = _colsum8(e * e)
    part = sq[:, 0:128]
    for b in range(1, D_MODEL // 128):
        part = part + sq[:, 128 * b:128 * (b + 1)]
    return e / D_MODEL, part


def _out_ple(h, ys, ya, p, g2, w_out_l, w_pg_l, w_pp_l, target=None):
    s = h.shape[0]
    tm = _row_tile(s, 512)
    last = target is not None

    def body(*refs):
        h_ref, ys_ref, ya_ref, p_ref, g_ref, wo_ref, wpg_ref, wpp_ref = refs[0:8]
        h1_ref, h2_ref = refs[8 + last], refs[9 + last]
        ysb = ys_ref[...].astype(BF16)
        yab = ya_ref[...].astype(BF16)
        h1 = h_ref[...]
        for sh, src in enumerate((ysb[:, 0:ROW_SHARD], ysb[:, ROW_SHARD:], yab[:, 0:ROW_SHARD], yab[:, ROW_SHARD:])):
            h1 = h1 + _dot(src, wo_ref[sh])
        _, _, gate, pp = _ple_forward(h1, p_ref[...], g_ref[...], wpg_ref, wpp_ref)
        h1_ref[...] = h1
        h2 = h1 + gate * pp
        if last:
            acc_ref = refs[11]

            @pl.when(pl.program_id(0) == 0)
            def _():
                acc_ref[...] = jnp.zeros_like(acc_ref)

            h2_ref[...], part = _sq_err_grad(h2, refs[8][...])
            acc_ref[...] += part
        else:
            h2_ref[...] = h2

    row = lambda i: (i, 0)
    big = pl.BlockSpec((tm, D_MODEL), row)
    wspec = lambda r, cdim: pl.BlockSpec((N_CHIPS, r, cdim), lambda i: (0, 0, 0))
    acc = pl.BlockSpec((SUBLANES, 128), lambda i: (0, 0))
    return pl.pallas_call(
        body, name="out_ple_loss" if last else "out_ple", grid=(s // tm,),
        in_specs=[big, pl.BlockSpec((tm, SSM_WIDTH), row), pl.BlockSpec((tm, ATTN_WIDTH), row),
                  pl.BlockSpec((tm, PLE_DIM), row), pl.BlockSpec((1, D_MODEL), lambda i: (0, 0)),
                  wspec(ROW_SHARD, D_MODEL), wspec(ROW_SHARD, D_MODEL), wspec(PLE_DIM, ROW_SHARD)] + [big] * last,
        out_specs=[big] * 2 + [acc] * last,
        out_shape=[jax.ShapeDtypeStruct((s, D_MODEL), F32)] * 2 + [jax.ShapeDtypeStruct((SUBLANES, 128), F32)] * last,
        compiler_params=_cparams(1),
    )(h, ys, ya, p, g2, w_out_l, w_pg_l, w_pp_l, *([target] if last else []))


def _rms_bwd(x, r, g, dy):
    gdy = g * dy
    dx = r * gdy - x * (r * r * r) * jnp.mean(x * gdy, axis=-1, keepdims=True)
    return dx, x * r * dy


def _out_ple_bwd(dh2, h1, p, g2, w_out_l, w_pg_l, w_pp_l):
    s = h1.shape[0]
    tm = _row_tile(s, 512)

    def body(dh2_ref, h1_ref, p_ref, g_ref, wo_ref, wpg_ref, wpp_ref,
             dh1_ref, dmix_ref, hn_ref, dgp_ref, dpp_ref, dh1b_ref, dg_ref):
        @pl.when(pl.program_id(0) == 0)
        def _():
            dg_ref[...] = jnp.zeros_like(dg_ref)

        h1 = h1_ref[...]
        dh2 = dh2_ref[...]
        g2v = g_ref[...]
        r2, hb, gate, pp = _ple_forward(h1, p_ref[...], g2v, wpg_ref, wpp_ref)
        dgp = (dh2 * pp) * gate * (1.0 - gate)
        dgpb = dgp.astype(BF16)
        dhn = jnp.concatenate([_dot_nt(dgpb, wpg_ref[sh]) for sh in range(N_CHIPS)], axis=-1)
        dx, dgrow = _rms_bwd(h1, r2, g2v, dhn)
        dh1 = dh2 + dx
        dh1b = dh1.astype(BF16)
        dh1_ref[...] = dh1
        dh1b_ref[...] = dh1b
        hn_ref[...] = hb
        dgp_ref[...] = dgpb
        dpp_ref[...] = (dh2 * gate).astype(BF16)
        dg_ref[...] += _colsum8(dgrow)
        for sh in range(N_CHIPS):
            dmix_ref[:, ROW_SHARD * sh:ROW_SHARD * (sh + 1)] = _dot_nt(dh1b, wo_ref[sh])

    row = lambda i: (i, 0)
    wspec = lambda r, cdim: pl.BlockSpec((N_CHIPS, r, cdim), lambda i: (0, 0, 0))
    big = pl.BlockSpec((tm, D_MODEL), row)
    return pl.pallas_call(
        body, name="out_ple_bwd", grid=(s // tm,),
        in_specs=[big, big, pl.BlockSpec((tm, PLE_DIM), row), pl.BlockSpec((1, D_MODEL), lambda i: (0, 0)),
                  wspec(ROW_SHARD, D_MODEL), wspec(ROW_SHARD, D_MODEL), wspec(PLE_DIM, ROW_SHARD)],
        out_specs=[big] * 6 + [pl.BlockSpec((SUBLANES, D_MODEL), lambda i: (0, 0))],
        out_shape=[jax.ShapeDtypeStruct((s, D_MODEL), F32)] * 2 + [jax.ShapeDtypeStruct((s, D_MODEL), BF16)] * 4
        + [jax.ShapeDtypeStruct((SUBLANES, D_MODEL), F32)],
        compiler_params=_cparams(1),
    )(dh2, h1, p, g2, w_out_l, w_pg_l, w_pp_l)


def _tn_matmul(a, b, n_blocks, block_a, name, into=None, first_block=0, total_blocks=None):
    s = a.shape[0]
    tk = _row_tile(s, 1024)
    nk = s // tk
    total_blocks = n_blocks if total_blocks is None else total_blocks
    ka, nb = a.shape[1], b.shape[1]
    if block_a:
        ka //= n_blocks
    else:
        nb //= n_blocks

    def body(*refs):
        a_ref, b_ref, o_ref, acc_ref = refs[0], refs[1], refs[-2], refs[-1]

        @pl.when(pl.program_id(0) == 0)
        def _():
            acc_ref[...] = jnp.zeros_like(acc_ref)

        at = a_ref[...].astype(BF16).T
        bb = b_ref[...].astype(BF16)
        for sh in range(n_blocks):
            if block_a:
                acc_ref[sh] += _dot(at[ka * sh:ka * (sh + 1), :], bb)
            else:
                acc_ref[sh] += _dot(at, bb[:, nb * sh:nb * (sh + 1)])

        @pl.when(pl.program_id(0) == nk - 1)
        def _():
            o_ref[...] = acc_ref[...].astype(BF16)

    in_specs = [pl.BlockSpec((tk, a.shape[1]), lambda i: (i, 0)), pl.BlockSpec((tk, b.shape[1]), lambda i: (i, 0))]
    operands = [a, b]
    aliases = {}
    if into is not None:
        in_specs.append(ANY)
        operands.append(into)
        aliases = {2: 0}
    return pl.pallas_call(
        body, name=name, grid=(nk,),
        in_specs=in_specs,
        out_specs=pl.BlockSpec((n_blocks, ka, nb), lambda i: (first_block // n_blocks, 0, 0)),
        out_shape=jax.ShapeDtypeStruct((total_blocks, ka, nb), BF16),
        scratch_shapes=[pltpu.VMEM((n_blocks, ka, nb), F32)],
        input_output_aliases=aliases,
        compiler_params=_cparams(1),
    )(*operands)


def _attn_bwd(qkv, o, proj, dmix, scatter=None):
    scatter = [] if scatter is None else scatter
    nsc = len(scatter)
    s = qkv.shape[0]
    tb = ATTN_BLOCK
    nq = s // tb
    n_sub = min(ATTN_SUB_BWD, nq)
    tq = n_sub * tb
    n_chain = 2 * n_sub

    def body(*refs):
        q_ref, k_ref, v_ref, o_ref, g_ref, dya_ref, tri_s_ref, tri_p_ref = refs[0:8]
        sc_ins = refs[8:8 + nsc]
        dq_ref, dk_ref, dv_ref, dg_ref = refs[8 + nsc:12 + nsc]
        sc_outs = refs[12 + nsc:12 + 2 * nsc]
        do_scr, l_scr, g_scr, s_scr, w_scr = refs[12 + 2 * nsc:17 + 2 * nsc]
        sc_sems = refs[17 + 2 * nsc:]
        i = pl.program_id(1)
        base = i * n_sub

        if nsc:
            @pl.when(jnp.logical_and(pl.program_id(0) == 0, i == 0))
            def _():
                _scatter_start(sc_ins, sc_outs, sc_sems)

        @pl.when(i == 0)
        def _():
            dk_ref[...] = jnp.zeros_like(dk_ref)
            dv_ref[...] = jnp.zeros_like(dv_ref)

        g = g_ref[...]
        sg = _sigmoid(g)
        dya = dya_ref[...]
        do_scr[...] = (dya * (g * sg)).astype(BF16)
        dg_ref[...] = dya * o_ref[...] * (sg * (1.0 + g * (1.0 - sg)))
        dq_ref[...] = jnp.zeros_like(dq_ref)
        g_scr[...] = jnp.zeros_like(g_scr)
        masks = _head_masks()

        def keep(t, a, h, c, r0, z, w):
            s_scr[c, t] = _sigmoid(z).astype(BF16)
            w_scr[c, t] = w.astype(BF16)

        steps = _chain_sweep(base, n_sub, q_ref, k_ref, tri_s_ref, l_scr, keep)
        row = lax.broadcasted_iota(jnp.int32, (tb, tb), 0)
        col = lax.broadcasted_iota(jnp.int32, (tb, tb), 1)

        def back(it, carry):
            t = steps - 1 - it
            r0s = [pl.multiple_of(jnp.maximum(base + a - t, 0) * tb, tb) for a in range(n_sub)]
            qhs, dohs, khs, gws = [], [], [], []
            for a in range(n_sub):
                kb = k_ref[pl.ds(r0s[a], tb), :]
                vb = v_ref[pl.ds(r0s[a], tb), :]
                qa = q_ref[a * tb:(a + 1) * tb, :]
                doa = do_scr[a * tb:(a + 1) * tb, :]
                for h, mask in enumerate(masks):
                    qhs.append(jnp.where(mask, qa, jnp.zeros_like(qa)))
                    khs.append(jnp.where(mask, kb, jnp.zeros_like(kb)))
                    dohs.append(jnp.where(mask, doa, jnp.zeros_like(doa)))
                    gws.append(w_scr[2 * a + h, t].astype(F32) * _dot_nt(dohs[-1], vb))
            parts = [_split_hilo(gw) for gw in gws]
            tri = tri_p_ref[...]
            sums = [_dot(hi, tri) + _dot(lo, tri) for hi, lo in parts]
            dzs = []
            for c, (gw, sm) in enumerate(zip(gws, sums)):
                gsum = g_scr[c]
                dz = gw - (gw + sm[:, 0:tb] + gsum) * s_scr[c, t].astype(F32)
                dz = jnp.where(col < row + t * tb, dz, 0.0)
                g_scr[c] = gsum + sm[:, tb:2 * tb]
                dzs.append(dz.astype(BF16))
            for c, dzb in enumerate(dzs):
                a = c // 2
                dk_ref[pl.ds(r0s[a], tb), :] += _dot_tn(dzb, qhs[c])
                dv_ref[pl.ds(r0s[a], tb), :] += _dot_tn(w_scr[c, t], dohs[c])
                dq_ref[a * tb:(a + 1) * tb, :] += _dot(dzb, khs[c])
            return carry

        lax.fori_loop(0, steps, back, 0)

        if nsc:
            @pl.when(jnp.logical_and(pl.program_id(0) == n_hp - 1, i == s // tq - 1))
            def _():
                _scatter_finish(sc_ins, sc_outs, sc_sems)

    n_hp = ATTN_WIDTH // (2 * HEAD_DIM)
    hp_blk = lambda off: pl.BlockSpec((tq, 2 * HEAD_DIM), lambda hp, i: (i, off + hp))
    res = lambda off: pl.BlockSpec((s, 2 * HEAD_DIM), lambda hp, i: (0, off + hp))
    tri = pl.BlockSpec((tb, 2 * tb), lambda hp, i: (0, 0))
    outs = pl.pallas_call(
        body, name="attn_bwd_scatter" if nsc else "attn_bwd", grid=(n_hp, s // tq),
        in_specs=[hp_blk(0), res(4), res(8), hp_blk(0), hp_blk(20), hp_blk(4), tri, tri] + [ANY] * nsc,
        out_specs=[hp_blk(0), res(0), res(0), hp_blk(0)] + [ANY] * nsc,
        out_shape=[jax.ShapeDtypeStruct((s, ATTN_WIDTH), F32)] * 4 + [jax.ShapeDtypeStruct(a.shape, a.dtype) for a in scatter],
        scratch_shapes=[pltpu.VMEM((tq, 2 * HEAD_DIM), BF16), pltpu.VMEM((n_chain, tb, tb), F32),
                        pltpu.VMEM((n_chain, tb, tb), F32), pltpu.VMEM((n_chain, nq, tb, tb), BF16),
                        pltpu.VMEM((n_chain, nq, tb, tb), BF16)] + (_scatter_sems(nsc) if nsc else []),
        compiler_params=_cparams(2),
    )(qkv, qkv, qkv, o, proj, dmix, _tri("suffix_incl"), _tri("prefix_strict"), *scatter)
    return outs[0], outs[1], outs[2], outs[3], (_scatter_own(outs[4:], scatter) if nsc else [])


def _ssm_glu_bwd(dmix, y, proj, d, w_glu_l, b_glu):
    s = y.shape[0]
    tm = _row_tile(s, 512)

    def body(dys_ref, y_ref, u_ref, gs_ref, d_ref, wg_ref, bg_ref,
             dyf_ref, du_ref, dgs_ref, z_ref, dzz_ref, dd_ref, db_ref):
        @pl.when(pl.program_id(0) == 0)
        def _():
            dd_ref[...] = jnp.zeros_like(dd_ref)
            db_ref[...] = jnp.zeros_like(db_ref)

        u = u_ref[...]
        dv = d_ref[...]
        yf, z, val, gate = _glu_forward(y_ref[...], u, dv, wg_ref, bg_ref[...])
        gs = gs_ref[...]
        sgs = _sigmoid(gs)
        sgate = _sigmoid(gate)
        dys = dys_ref[...]
        dgv = dys * (gs * sgs)
        dgs_ref[...] = dys * (val * sgate) * (sgs * (1.0 + gs * (1.0 - sgs)))
        dzz = jnp.concatenate([dgv * sgate, dgv * val * sgate * (1.0 - sgate)], axis=-1)
        dzzb = dzz.astype(BF16)
        dz = _dot_nt(dzzb[:, 0:ROW_SHARD], wg_ref[0])
        for sh in range(1, N_CHIPS):
            dz = dz + _dot_nt(dzzb[:, ROW_SHARD * sh:ROW_SHARD * (sh + 1)], wg_ref[sh])
        dyf = dz * _gelu_grad(yf)
        dyf_ref[...] = dyf
        du_ref[...] = dyf * dv
        z_ref[...] = z.astype(BF16)
        dzz_ref[...] = dzzb
        dd_ref[...] += _colsum8(dyf * u)
        db_ref[...] += _colsum8(dzz)

    row = lambda i: (i, 0)
    half = pl.BlockSpec((tm, SSM_WIDTH), row)
    return pl.pallas_call(
        body, name="ssm_glu_bwd", grid=(s // tm,),
        in_specs=[half, half, half, pl.BlockSpec((tm, SSM_WIDTH), lambda i: (i, 1)),
                  pl.BlockSpec((1, SSM_WIDTH), lambda i: (0, 0)),
                  pl.BlockSpec((N_CHIPS, SSM_WIDTH, ROW_SHARD), lambda i: (0, 0, 0)),
                  pl.BlockSpec((1, 2 * SSM_WIDTH), lambda i: (0, 0))],
        out_specs=[half, half, half, half, pl.BlockSpec((tm, 2 * SSM_WIDTH), row),
                   pl.BlockSpec((SUBLANES, SSM_WIDTH), lambda i: (0, 0)),
                   pl.BlockSpec((SUBLANES, 2 * SSM_WIDTH), lambda i: (0, 0))],
        out_shape=[jax.ShapeDtypeStruct((s, SSM_WIDTH), F32)] * 3
        + [jax.ShapeDtypeStruct((s, SSM_WIDTH), BF16), jax.ShapeDtypeStruct((s, 2 * SSM_WIDTH), BF16),
           jax.ShapeDtypeStruct((SUBLANES, SSM_WIDTH), F32), jax.ShapeDtypeStruct((SUBLANES, 2 * SSM_WIDTH), F32)],
        compiler_params=_cparams(1),
    )(dmix, y, proj, proj, d, w_glu_l, b_glu)


def _ssm_scan_bwd(dyf, xs, proj, wct, tab_rev, wbt):
    s = dyf.shape[0]
    tm = _row_tile(s, SCAN_TILE)
    nt = s // tm
    length = tm // SUBLANES

    def body(dy_ref, xs_ref, u_ref, wct_ref, tab_ref, wbt_ref, du_ref, dwc_ref, dwb_ref, da_ref, lam_ref, carry_ref):
        @pl.when(pl.program_id(1) == 0)
        def _():
            carry_ref[...] = jnp.zeros_like(carry_ref)
            dwc_ref[...] = jnp.zeros_like(dwc_ref)
            dwb_ref[...] = jnp.zeros_like(dwb_ref)
            da_ref[...] = jnp.zeros_like(da_ref)

        dyp = _interleave_chunks(dy_ref[...]).astype(BF16)
        up = _interleave_chunks(u_ref[...]).astype(BF16)
        lam_ref[...] = _dot(dyp, wct_ref[...])

        def tail(r0, lr, li, carry):
            er, ei, dar, dai = carry
            xr = xs_ref[pl.ds(r0, SUBLANES), 0:CH_S]
            xi = xs_ref[pl.ds(r0, SUBLANES), CH_S:2 * CH_S]
            return lr, li, dar + (xr * er + xi * ei), dai + (xr * ei - xi * er)

        fix, (gr, gi) = _chunk_scan(lam_ref, tab_ref, carry_ref, length, reverse=True, tail=tail)
        zero = jnp.zeros((SUBLANES, CH_S), F32)
        _, _, dar, dai = lax.fori_loop(0, length, fix, (gr, gi, zero, zero), unroll=2)
        da_ref[:, 0:CH_S] += dar
        da_ref[:, CH_S:2 * CH_S] += dai
        lamb = lam_ref[...].astype(BF16)
        du_ref[...] = _time_order(_dot(lamb, wbt_ref[...]))
        dwc_ref[...] += _dot_tn(xs_ref[...].astype(BF16), dyp)
        dwb_ref[...] += _dot_tn(up, lamb)

    rev = lambda j, i: (nt - 1 - i, j)
    return pl.pallas_call(
        body, name="ssm_scan_bwd", grid=(SSM_CHUNKS, nt),
        in_specs=[pl.BlockSpec((tm, CH_W), rev),
                  pl.BlockSpec((None, tm, 2 * CH_S), lambda j, i: (j, nt - 1 - i, 0)),
                  pl.BlockSpec((tm, CH_W), rev),
                  pl.BlockSpec((None, CH_W, 2 * CH_S), lambda j, i: (j, 0, 0)),
                  pl.BlockSpec((None, length, 2 * CH_S), lambda j, i: (j, 0, 0)),
                  pl.BlockSpec((None, 2 * CH_S, CH_W), lambda j, i: (j, 0, 0))],
        out_specs=[pl.BlockSpec((tm, CH_W), rev),
                   pl.BlockSpec((None, 2 * CH_S, CH_W), lambda j, i: (j, 0, 0)),
                   pl.BlockSpec((None, CH_W, 2 * CH_S), lambda j, i: (j, 0, 0)),
                   pl.BlockSpec((None, SUBLANES, 2 * CH_S), lambda j, i: (j, 0, 0))],
        out_shape=[jax.ShapeDtypeStruct((s, SSM_WIDTH), F32),
                   jax.ShapeDtypeStruct((SSM_CHUNKS, 2 * CH_S, CH_W), F32),
                   jax.ShapeDtypeStruct((SSM_CHUNKS, CH_W, 2 * CH_S), F32),
                   jax.ShapeDtypeStruct((SSM_CHUNKS, SUBLANES, 2 * CH_S), F32)],
        scratch_shapes=[pltpu.VMEM((tm, 2 * CH_S), F32), pltpu.VMEM((SUBLANES, 2 * CH_S), F32)],
        compiler_params=_cparams(2),
    )(dyf, xs, proj, wct, tab_rev, wbt)


def _in_proj_bwd(h, g1, w_in_l, qg, kg, proj, du_a, du_b, dgs, dq, dk, dv, dga, dh1):
    s = h.shape[0]
    tm = _row_tile(s, 256)

    def body(h_ref, g_ref, w_ref, qg_ref, kg_ref, ones_ref, q_ref, k_ref, dua_ref, dub_ref, dgs_ref, dq_ref, dk_ref,
             dv_ref, dga_ref, dh1_ref, dh_ref, hn_ref, dp_ref, dg1_ref, dqg_ref, dkg_ref):
        @pl.when(pl.program_id(0) == 0)
        def _():
            dg1_ref[...] = jnp.zeros_like(dg1_ref)
            dqg_ref[...] = jnp.zeros_like(dqg_ref)
            dkg_ref[...] = jnp.zeros_like(dkg_ref)

        ones = ones_ref[...]

        def head_norm_bwd(x, gain, dy):
            r = lax.rsqrt(_dot_hilo(x * x, ones) + RMS_EPS)
            gdy = gain * dy
            dx = r * gdy - x * (r * r * r) * _dot_hilo(x * gdy, ones)
            return dx, x * r * dy

        dqr, dqg_rows = head_norm_bwd(q_ref[...], qg_ref[...], dq_ref[...] * ATTN_SCALE)
        dkr, dkg_rows = head_norm_bwd(k_ref[...], kg_ref[...], dk_ref[...])
        dqg_ref[...] += _colsum8(dqg_rows)
        dkg_ref[...] += _colsum8(dkg_rows)
        dp_ref[:, 0:512] = (dua_ref[...] + dub_ref[...]).astype(BF16)
        dp_ref[:, 512:1024] = dgs_ref[...].astype(BF16)
        dp_ref[:, 1024:1536] = dqr.astype(BF16)
        dp_ref[:, 1536:2048] = dkr.astype(BF16)
        dp_ref[:, 2048:2560] = dv_ref[...].astype(BF16)
        dp_ref[:, 2560:3072] = dga_ref[...].astype(BF16)
        dhn = _dot_nt(dp_ref[:, 0:IN_SHARD], w_ref[0])
        for sh in range(1, N_CHIPS):
            dhn = dhn + _dot_nt(dp_ref[:, IN_SHARD * sh:IN_SHARD * (sh + 1)], w_ref[sh])
        x = h_ref[...]
        gv = g_ref[...]
        r, hn = _rms_rows(x, gv)
        dx, dg_rows = _rms_bwd(x, r, gv, dhn)
        dh_ref[...] = dh1_ref[...] + dx
        hn_ref[...] = hn.astype(BF16)
        dg1_ref[...] += _colsum8(dg_rows)

    row = lambda i: (i, 0)
    full = lambda shape: pl.BlockSpec(shape, lambda i: (0,) * len(shape))
    big = pl.BlockSpec((tm, D_MODEL), row)
    half = pl.BlockSpec((tm, 512), row)
    return pl.pallas_call(
        body, name="in_proj_bwd", grid=(s // tm,),
        in_specs=[big, full((1, D_MODEL)), full((N_CHIPS, D_MODEL, IN_SHARD)),
                  full((1, ATTN_WIDTH)), full((1, ATTN_WIDTH)), full((ATTN_WIDTH, ATTN_WIDTH)),
                  pl.BlockSpec((tm, 512), lambda i: (i, 2)), pl.BlockSpec((tm, 512), lambda i: (i, 3)),
                  half, half, half, half, half, half, half, big],
        out_specs=[big, big, pl.BlockSpec((tm, IN_COLS), row), pl.BlockSpec((SUBLANES, D_MODEL), lambda i: (0, 0)),
                   pl.BlockSpec((SUBLANES, ATTN_WIDTH), lambda i: (0, 0)), pl.BlockSpec((SUBLANES, ATTN_WIDTH), lambda i: (0, 0))],
        out_shape=[jax.ShapeDtypeStruct((s, D_MODEL), F32), jax.ShapeDtypeStruct((s, D_MODEL), BF16),
                   jax.ShapeDtypeStruct((s, IN_COLS), BF16), jax.ShapeDtypeStruct((SUBLANES, D_MODEL), F32),
                   jax.ShapeDtypeStruct((SUBLANES, ATTN_WIDTH), F32), jax.ShapeDtypeStruct((SUBLANES, ATTN_WIDTH), F32)],
        compiler_params=_cparams(1),
    )(h, g1, w_in_l, qg, kg, _head_ones(), proj, proj, du_a, du_b, dgs, dq, dk, dv, dga, dh1)


SMALL_NAMES = ("mix_norm_g", "ssm_a_re", "ssm_a_im", "ssm_log_dt", "ssm_b_re", "ssm_b_im", "ssm_c_re", "ssm_c_im",
               "ssm_d", "ssm_b_glu", "q_norm_g", "k_norm_g", "ple_norm_g")
SMALL_4D = ("ssm_b_re", "ssm_b_im", "ssm_c_re", "ssm_c_im")
BIG_NAMES = ("w_in", "ssm_w_glu", "w_out", "w_ple_gate", "w_ple_proj")


def _ssm_setup(sm, layer, length):
    col = lambda a: a[layer].reshape(1, N_STATES)
    a_re, a_im = col(sm["ssm_a_re"]), col(sm["ssm_a_im"])
    log_dt = jnp.repeat(sm["ssm_log_dt"][layer], SSM_STATE).reshape(1, N_STATES)
    b_re = sm["ssm_b_re"][layer].reshape(N_STATES, SSM_GROUP).T
    b_im = sm["ssm_b_im"][layer].reshape(N_STATES, SSM_GROUP).T
    by_channel = lambda c: c[layer].transpose(1, 0, 2).reshape(SSM_GROUP, N_STATES)
    disc_in = (a_re, a_im, log_dt, b_re, b_im)
    wb, wbt, wct, wc, tab, tab_rev = _disc_fwd(*disc_in, by_channel(sm["ssm_c_re"]), by_channel(sm["ssm_c_im"]), length)
    return dict(disc_in=disc_in, wb=wb, wbt=wbt, wc=wc, wct=wct, tab=tab, tab_rev=tab_rev)


def _whole_blocks(names, gathered):
    return {n: g.reshape(N_CHIPS, 2 * g.shape[2], g.shape[3]) for n, g in zip(names, gathered)}


def _local_step(x, p, target, sm, w_in0, local=None, gathered=None, layer1_hook=None):
    wg = [dict(w_in=w_in0), {}] if gathered is None else gathered
    tile8 = lambda a: jnp.tile(a, ATTN_WIDTH // HEAD_DIM).reshape(1, ATTN_WIDTH)
    saved = []
    h = x
    for l in range(N_LAYERS):
        ssm = _ssm_setup(sm, l, _row_tile(x.shape[0], SCAN_TILE) // SUBLANES)
        g1 = sm["mix_norm_g"][l].reshape(1, D_MODEL)
        g2 = sm["ple_norm_g"][l].reshape(1, D_MODEL)
        qg, kg = tile8(sm["q_norm_g"][l]), tile8(sm["k_norm_g"][l])
        dsk = sm["ssm_d"][l].reshape(1, SSM_WIDTH)
        bgl = sm["ssm_b_glu"][l].reshape(1, 2 * SSM_WIDTH)
        proj, qkv = _in_proj(h, g1, wg[l]["w_in"], qg, kg)
        if l == 0 and local is not None:
            rest = BIG_NAMES[1:]
            xs, y, got = _ssm_scan_fwd(proj, ssm["wb"], ssm["tab"], ssm["wc"], [local[n] for n in rest], [0] * len(rest))
            wg[0].update(_whole_blocks(rest, got))
            ys = _ssm_glu_fwd(y, proj, dsk, wg[0]["ssm_w_glu"], bgl)
            o, ya, got = _attn_fwd(qkv, proj, [local[n] for n in BIG_NAMES], [2] * len(BIG_NAMES))
            wg[1].update(_whole_blocks(BIG_NAMES, got))
        else:
            xs, y, _ = _ssm_scan_fwd(proj, ssm["wb"], ssm["tab"], ssm["wc"])
            ys = _ssm_glu_fwd(y, proj, dsk, wg[l]["ssm_w_glu"], bgl)
            o, ya, _ = _attn_fwd(qkv, proj)
        tail = (target,) if l == N_LAYERS - 1 else ()
        h1, h2, *sq = _out_ple(h, ys, ya, p[l], g2, wg[l]["w_out"], wg[l]["w_ple_gate"], wg[l]["w_ple_proj"], *tail)
        saved.append(dict(ssm=ssm, g1=g1, g2=g2, qg=qg, kg=kg, dsk=dsk, bgl=bgl, h=h, proj=proj, qkv=qkv, xs=xs, y=y,
                          ys=ys, o=o, ya=ya, h1=h1))
        h = h2
    dh = h
    loss = 0.5 * jnp.sum(sq[0]) / D_MODEL

    gbig = [{} for _ in range(N_LAYERS)]
    scattered = []
    gsm = {n: [None] * N_LAYERS for n in SMALL_NAMES}
    for l in reversed(range(N_LAYERS)):
        sv = saved[l]
        ssm = sv["ssm"]
        dh1, dmix, hn2b, dgpb, dppb, dh1b, dg2 = _out_ple_bwd(dh, sv["h1"], p[l], sv["g2"], wg[l]["w_out"],
                                                              wg[l]["w_ple_gate"], wg[l]["w_ple_proj"])
        gsm["ple_norm_g"][l] = dg2.sum(0)
        gbig[l]["w_ple_proj"] = _tn_matmul(p[l], dppb, N_CHIPS, False, "dw_ple_proj")
        gbig[l]["w_ple_gate"] = _tn_matmul(hn2b, dgpb, N_CHIPS, True, "dw_ple_gate")
        dwo = _tn_matmul(sv["ys"], dh1b, 2, True, "dw_out_ssm", None, 0, N_CHIPS)
        gbig[l]["w_out"] = _tn_matmul(sv["ya"], dh1b, 2, True, "dw_out_attn", dwo, 2, N_CHIPS)
        if l == 0 and layer1_hook is not None:
            dqs, dkn, dv, dga, scattered = _attn_bwd(sv["qkv"], sv["o"], sv["proj"], dmix, layer1_hook(gbig[1]))
        else:
            dqs, dkn, dv, dga, _ = _attn_bwd(sv["qkv"], sv["o"], sv["proj"], dmix)
        dyf, du_a, dgs, zb, dzzb, dd, dbg = _ssm_glu_bwd(dmix, sv["y"], sv["proj"], sv["dsk"], wg[l]["ssm_w_glu"], sv["bgl"])
        gsm["ssm_d"][l] = dd.sum(0).reshape(SSM_GROUPS, SSM_GROUP)
        gsm["ssm_b_glu"][l] = dbg.sum(0)
        gbig[l]["ssm_w_glu"] = _tn_matmul(zb, dzzb, N_CHIPS, False, "dw_glu")
        du_b, dwc, dwb, da = _ssm_scan_bwd(dyf, sv["xs"], sv["proj"], ssm["wct"], ssm["tab_rev"], ssm["wbt"])
        d_are, d_aim, d_ldt, d_bre, d_bim, d_cre, d_cim = _disc_bwd(*ssm["disc_in"], da, dwb, dwc)
        by_group = lambda t: t.reshape(SSM_GROUP, SSM_GROUPS, SSM_STATE).transpose(1, 0, 2)
        gsm["ssm_c_re"][l] = by_group(d_cre)
        gsm["ssm_c_im"][l] = by_group(d_cim)
        gsm["ssm_a_re"][l] = d_are.reshape(SSM_GROUPS, SSM_STATE)
        gsm["ssm_a_im"][l] = d_aim.reshape(SSM_GROUPS, SSM_STATE)
        gsm["ssm_log_dt"][l] = d_ldt.reshape(SSM_GROUPS, SSM_STATE).sum(1)
        gsm["ssm_b_re"][l] = d_bre.T.reshape(SSM_GROUPS, SSM_STATE, SSM_GROUP)
        gsm["ssm_b_im"][l] = d_bim.T.reshape(SSM_GROUPS, SSM_STATE, SSM_GROUP)
        dh, hnb, dprojb, dg1, dqg, dkg = _in_proj_bwd(sv["h"], sv["g1"], wg[l]["w_in"], sv["qg"], sv["kg"], sv["proj"],
                                                      du_a, du_b, dgs, dqs, dkn, dv, dga, dh1)
        gsm["mix_norm_g"][l] = dg1.sum(0)
        gsm["q_norm_g"][l] = dqg.sum(0).reshape(-1, HEAD_DIM).sum(0)
        gsm["k_norm_g"][l] = dkg.sum(0).reshape(-1, HEAD_DIM).sum(0)
        gbig[l]["w_in"] = _tn_matmul(hnb, dprojb, N_CHIPS, False, "dw_in")
    gsm = {n: jnp.stack(v, 0) for n, v in gsm.items()}
    return loss, dh, gbig, gsm, scattered


_SMALL_PAD = 8 * 8 * 128


def _pack_small(d, extra):
    flat = jnp.concatenate([d[n].reshape(-1) for n in SMALL_NAMES] + [jnp.stack(extra)])
    n = flat.shape[0]
    padded = -(-n // _SMALL_PAD) * _SMALL_PAD
    return jnp.pad(flat, (0, padded - n))


def _unpack_small(flat, like):
    out, off = {}, 0
    for n in SMALL_NAMES:
        size = like[n].size
        out[n] = flat[off:off + size].reshape(like[n].shape)
        off += size
    return out, flat[off:]


def _half_views(arrs):
    return [a.reshape(a.shape[0], 2, a.shape[1] // 2, a.shape[2]) for a in arrs]


def _chip_sums(views, out_dtypes, tag):
    recv = _sibling_push(views, "grad_push_" + tag)
    return [_add_my_half(v, r, dt, "grad_half_add") for v, r, dt in zip(views, recv, out_dtypes)]


def kernel(x, p, mix_norm_g, w_in, ssm_a_re, ssm_a_im, ssm_log_dt, ssm_b_re, ssm_b_im, ssm_c_re, ssm_c_im, ssm_d, ssm_w_glu, ssm_b_glu, q_norm_g, k_norm_g, w_out, ple_norm_g, w_ple_gate, w_ple_proj, loss_target, m_mix_norm_g, m_w_in, m_ssm_a_re, m_ssm_a_im, m_ssm_log_dt, m_ssm_b_re, m_ssm_b_im, m_ssm_c_re, m_ssm_c_im, m_ssm_d, m_ssm_w_glu, m_ssm_b_glu, m_q_norm_g, m_k_norm_g, m_w_out, m_ple_norm_g, m_w_ple_gate, m_w_ple_proj, v_mix_norm_g, v_w_in, v_ssm_a_re, v_ssm_a_im, v_ssm_log_dt, v_ssm_b_re, v_ssm_b_im, v_ssm_c_re, v_ssm_c_im, v_ssm_d, v_ssm_w_glu, v_ssm_b_glu, v_q_norm_g, v_k_norm_g, v_w_out, v_ple_norm_g, v_w_ple_gate, v_w_ple_proj):
    args = dict(locals())
    names = ("mix_norm_g", "w_in", "ssm_a_re", "ssm_a_im", "ssm_log_dt", "ssm_b_re", "ssm_b_im", "ssm_c_re", "ssm_c_im",
             "ssm_d", "ssm_w_glu", "ssm_b_glu", "q_norm_g", "k_norm_g", "w_out", "ple_norm_g", "w_ple_gate", "w_ple_proj")
    w = {n: args[n] for n in names}
    m = {n: args["m_" + n] for n in names}
    v = {n: args["v_" + n] for n in names}

    local = {n: w[n].astype(BF16).reshape(2 * N_LAYERS, w[n].shape[1] // 2, w[n].shape[2]) for n in BIG_NAMES}
    w_in0 = _chip_gather([local["w_in"]], "w_in_gather")[0].reshape(N_CHIPS, D_MODEL, IN_SHARD)
    sm = {n: w[n] for n in SMALL_NAMES}
    nb = len(BIG_NAMES)
    loss, dx, gbig, gsm, got1 = _local_step(
        x[0], p[:, 0], loss_target[0], sm, w_in0, local,
        layer1_hook=lambda g1: _chip_sums(_half_views([g1[n] for n in BIG_NAMES]), [BF16] * nb, "layer1"))

    small = _pack_small(gsm, [loss]).reshape(N_CHIPS, 2, SUBLANES, -1)
    chip0 = _chip_sums(_half_views([gbig[0][n] for n in BIG_NAMES]) + [small], [BF16] * nb + [F32], "layer0")
    got0 = _chip_scatter(chip0, "grad_chip_scatter")
    tot1 = [_sum4(a, "grad_chip_sum") for a in got1]
    tot0 = [_sum4(a, "grad_chip_sum") for a in got0]
    pieces = [(t, k, (l,)) for l, tots in enumerate((tot0[:nb], tot1)) for k, t in enumerate(tots)] + [(tot0[nb], nb, ())]
    joined = _sibling_join(pieces, [(N_LAYERS, 2) + t.shape for t in tot1] + [(2,) + tot0[nb].shape], "grad_sibling_join")
    small_all = _chip_gather([joined[nb]], "small_grad_gather")[0]
    small_tot = small_all.reshape(-1)
    g = {n: j.reshape(w[n].shape) for n, j in zip(BIG_NAMES, joined)}
    g_small, rest = _unpack_small(small_tot, sm)
    g.update(g_small)
    loss = rest[0]

    delta, new_m, new_v = {}, {}, {}
    for n in BIG_NAMES:
        lanes = w[n].shape[-1]
        outs = _adamw(_as_rows(w[n], lanes), _as_rows(g[n], lanes), _as_rows(m[n], lanes), _as_rows(v[n], lanes), "adamw_" + n)
        delta[n], new_m[n], new_v[n] = [o.reshape(w[n].shape) for o in outs]
    swap = lambda n, a: jnp.swapaxes(a, -1, -2) if n in ("ssm_b_re", "ssm_b_im") else a
    for group, per_layer in ((SMALL_4D, True), (tuple(n for n in SMALL_NAMES if n not in SMALL_4D), False)):
        outs = _adamw_many(*[[swap(n, d[n]) for n in group] for d in (w, g, m, v)],
                           "adamw_small_4d" if per_layer else "adamw_small", per_layer)
        for d, o in zip((delta, new_m, new_v), outs):
            d.update({n: swap(n, a) for n, a in zip(group, o)})

    return (loss, dx[None], *[g[n] for n in names], *[delta[n] for n in names],
            *[new_m[n] for n in names], *[new_v[n] for n in names])
```

```python
import functools
import math

import jax
import jax.numpy as jnp
from jax import lax
from jax.experimental import pallas as pl
from jax.experimental.pallas import tpu as pltpu

F32 = jnp.float32
BF16 = jnp.bfloat16

D_MODEL = 1024
N_LAYERS = 2
N_CHIPS = 4
IN_COLS = 3072
IN_SHARD = IN_COLS // N_CHIPS
SSM_WIDTH = 512
SSM_GROUP = 16
SSM_GROUPS = 32
SSM_STATE = 64
N_STATES = SSM_GROUPS * SSM_STATE
SSM_CHUNKS = 4
CH_W = SSM_WIDTH // SSM_CHUNKS
CH_S = N_STATES // SSM_CHUNKS
ATTN_WIDTH = 512
HEAD_DIM = 64
PLE_DIM = 256
ROW_SHARD = 256
RMS_EPS = 1e-6
ATTN_SCALE = HEAD_DIM ** -0.5
ATTN_BLOCK = 128
EXP_ZERO = -87.5
SUBLANES = 8
SCAN_TILE = 1024
V7X_VMEM_LIMIT = 52 * 1024 * 1024

ADAM_LR = 0.001
ADAM_B1 = 0.9
ADAM_B2 = 0.999
ADAM_EPS = 1e-08
ADAM_WD = 0.01
ADAM_STEP = 10

MESH = pl.DeviceIdType.MESH
ANY = pl.BlockSpec(memory_space=pl.ANY)


def _cparams(n_grid=0, parallel=0):
    sem = tuple(["parallel"] * parallel + ["arbitrary"] * (n_grid - parallel))
    return pltpu.CompilerParams(dimension_semantics=sem, vmem_limit_bytes=V7X_VMEM_LIMIT)


def _dot(a, b):
    return jnp.dot(a, b, preferred_element_type=F32)


def _dot_nt(a, b):
    return lax.dot_general(a, b, (((1,), (1,)), ((), ())), preferred_element_type=F32)


def _dot_tn(a, b):
    return lax.dot_general(a, b, (((0,), (0,)), ((), ())), preferred_element_type=F32)


def _split_hilo(a):
    hi = a.astype(BF16)
    lo = (a - hi.astype(F32)).astype(BF16)
    return hi, lo


def _dot_hilo(a, b):
    hi, lo = _split_hilo(a)
    return _dot(hi, b) + _dot(lo, b)


def _sigmoid(x):
    return 0.5 * (jnp.tanh(0.5 * x) + 1.0)


_GELU_C = math.sqrt(2.0 / math.pi)


def _gelu(x):
    return 0.5 * x * (1.0 + jnp.tanh(_GELU_C * (x + 0.044715 * (x * x * x))))


def _gelu_grad(x):
    t = jnp.tanh(_GELU_C * (x + 0.044715 * (x * x * x)))
    return 0.5 * (1.0 + t) + 0.5 * x * (1.0 - t * t) * (_GELU_C * (1.0 + 3.0 * 0.044715 * (x * x)))


def _row_tile(s, want):
    for t in range(min(s, want), 7, -1):
        if s % t == 0 and t % SUBLANES == 0:
            return t
    return s


def _coords():
    return lax.axis_index("x"), lax.axis_index("y"), lax.axis_index("c")


def _other_chips(x, y):
    return [(1 - x, y), (x, 1 - y), (1 - x, 1 - y)]


def _remote(src, dst, send_sem, recv_sem, dev):
    return pltpu.make_async_remote_copy(src_ref=src, dst_ref=dst, send_sem=send_sem, recv_sem=recv_sem,
                                        device_id=dev, device_id_type=MESH)


def _set_block(buf, block, index):
    return lax.dynamic_update_index_in_dim(buf, block, index, 0)


def _gather_sems(n):
    return [pltpu.SemaphoreType.DMA((3 * n,)) for _ in range(4)]


def _gather_copies(ins, bases, outs, sems):
    send_sems, recv_sems, fwd_send, fwd_recv = sems
    x, y, c = _coords()
    me_chip = 2 * x + y
    sibling = (x, y, 1 - c)
    first, landed, passed, from_sibling = [], [], [], []
    for k in range(len(ins)):
        for j, (cx, cy) in enumerate(_other_chips(x, y)):
            i = 3 * k + j
            first.append(_remote(ins[k].at[bases[k] + c], outs[k].at[me_chip, c], send_sems.at[i], recv_sems.at[i], (cx, cy, c)))
            blk = outs[k].at[2 * cx + cy, c]
            landed.append(_remote(blk, blk, send_sems.at[i], recv_sems.at[i], (cx, cy, c)))
            passed.append(_remote(blk, blk, fwd_send.at[i], fwd_recv.at[i], sibling))
            blk = outs[k].at[2 * cx + cy, 1 - c]
            from_sibling.append(_remote(blk, blk, fwd_send.at[i], fwd_recv.at[i], sibling))
    return first, landed, passed, from_sibling


def _gather_start(ins, bases, outs, sems):
    for cp in _gather_copies(ins, bases, outs, sems)[0]:
        cp.start()


def _gather_finish(ins, bases, outs, sems):
    first, landed, passed, from_sibling = _gather_copies(ins, bases, outs, sems)
    for arrived, forward in zip(landed, passed):
        arrived.wait_recv()
        forward.start()
    for cp in from_sibling:
        cp.wait_recv()
    for cp in first + passed:
        cp.wait_send()


def _gather_outputs(arrs):
    return [jax.ShapeDtypeStruct((N_CHIPS, 2) + a.shape[1:], a.dtype) for a in arrs]


def _gather_own(outs, arrs, bases):
    me_chip = 2 * lax.axis_index("x") + lax.axis_index("y")
    return [_set_block(o, lax.slice_in_dim(a, b, b + 2, axis=0), me_chip) for o, a, b in zip(outs, arrs, bases)]


def _chip_gather(arrs, name, bases=None):
    n = len(arrs)
    bases = [0] * n if bases is None else bases

    def body(*refs):
        ins, outs, sems = refs[:n], refs[n:2 * n], refs[2 * n:]
        _gather_start(ins, bases, outs, sems)
        _gather_finish(ins, bases, outs, sems)

    outs = pl.pallas_call(
        body, name=name, out_shape=_gather_outputs(arrs),
        in_specs=[ANY] * n, out_specs=[ANY] * n, scratch_shapes=_gather_sems(n),
    )(*arrs)
    return _gather_own(outs, arrs, bases)


def _sibling_push(arrs, name):
    n = len(arrs)

    def body(*refs):
        ins, outs = refs[:n], refs[n:2 * n]
        send_sems, recv_sems = refs[2 * n:]
        x, y, c = _coords()
        cps = [_remote(ins[k].at[pl.ds(0, N_CHIPS), 1 - c], outs[k], send_sems.at[k], recv_sems.at[k], (x, y, 1 - c))
               for k in range(n)]
        for cp in cps:
            cp.start()
        for cp in cps:
            cp.wait_recv()
        for cp in cps:
            cp.wait_send()

    return pl.pallas_call(
        body, name=name,
        out_shape=[jax.ShapeDtypeStruct((a.shape[0],) + a.shape[2:], a.dtype) for a in arrs],
        in_specs=[ANY] * n, out_specs=[ANY] * n,
        scratch_shapes=[pltpu.SemaphoreType.DMA((n,)), pltpu.SemaphoreType.DMA((n,))],
    )(*arrs)


def _sibling_join(pieces, out_shapes, name):
    n = len(pieces)
    no = len(out_shapes)

    def body(*refs):
        ins, outs = refs[:n], refs[n:n + no]
        send_sems, recv_sems = refs[n + no:]
        x, y, c = _coords()
        sibling = (x, y, 1 - c)
        cps = [_remote(ins[k], outs[o].at[lead + (c,)], send_sems.at[k], recv_sems.at[k], sibling)
               for k, (_, o, lead) in enumerate(pieces)]
        for cp in cps:
            cp.start()
        for k, (_, o, lead) in enumerate(pieces):
            blk = outs[o].at[lead + (1 - c,)]
            _remote(blk, blk, send_sems.at[k], recv_sems.at[k], sibling).wait_recv()
        for cp in cps:
            cp.wait_send()

    outs = pl.pallas_call(
        body, name=name,
        out_shape=[jax.ShapeDtypeStruct(sh, F32) for sh in out_shapes],
        in_specs=[ANY] * n, out_specs=[ANY] * no,
        scratch_shapes=[pltpu.SemaphoreType.DMA((n,)), pltpu.SemaphoreType.DMA((n,))],
    )(*[a for a, _, _ in pieces])
    outs = list(outs)
    c = lax.axis_index("c")
    for a, o, lead in pieces:
        block = a.reshape((1,) * (len(lead) + 1) + a.shape)
        outs[o] = lax.dynamic_update_slice(outs[o], block, lead + (c,) + (0,) * a.ndim)
    return outs


def _scatter_sems(n):
    return [pltpu.SemaphoreType.DMA((3 * n,)), pltpu.SemaphoreType.DMA((3 * n,))]


def _scatter_copies(ins, outs, sems):
    send_sems, recv_sems = sems
    x, y, c = _coords()
    me_chip = 2 * x + y
    sends, arrivals = [], []
    for k in range(len(ins)):
        for j, (cx, cy) in enumerate(_other_chips(x, y)):
            i = 3 * k + j
            sends.append(_remote(ins[k].at[2 * cx + cy], outs[k].at[me_chip], send_sems.at[i], recv_sems.at[i], (cx, cy, c)))
            blk = outs[k].at[2 * cx + cy]
            arrivals.append(_remote(blk, blk, send_sems.at[i], recv_sems.at[i], (cx, cy, c)))
    return sends, arrivals


def _scatter_start(ins, outs, sems):
    for cp in _scatter_copies(ins, outs, sems)[0]:
        cp.start()


def _scatter_finish(ins, outs, sems):
    sends, arrivals = _scatter_copies(ins, outs, sems)
    for cp in arrivals:
        cp.wait_recv()
    for cp in sends:
        cp.wait_send()


def _chip_scatter(arrs, name):
    n = len(arrs)

    def body(*refs):
        ins, outs, sems = refs[:n], refs[n:2 * n], refs[2 * n:]
        _scatter_start(ins, outs, sems)
        _scatter_finish(ins, outs, sems)

    outs = pl.pallas_call(
        body, name=name,
        out_shape=[jax.ShapeDtypeStruct(a.shape, a.dtype) for a in arrs],
        in_specs=[ANY] * n, out_specs=[ANY] * n, scratch_shapes=_scatter_sems(n),
    )(*arrs)
    return outs


def _as_rows(a, lanes):
    return a.reshape(-1, lanes)


def _add_my_half(v, recv, out_dtype, name):
    n_sh, _, h, cdim = v.shape
    tr = _row_tile(h, 512)

    def body(c_ref, a_ref, b_ref, o_ref):
        o_ref[...] = (a_ref[...].astype(F32) + b_ref[...].astype(F32)).astype(out_dtype)

    c = lax.axis_index("c").astype(jnp.int32).reshape(1)
    return pl.pallas_call(
        body, name=name,
        grid_spec=pltpu.PrefetchScalarGridSpec(
            num_scalar_prefetch=1, grid=(n_sh, h // tr),
            in_specs=[pl.BlockSpec((None, None, tr, cdim), lambda sh, i, c_ref: (sh, c_ref[0], i, 0)),
                      pl.BlockSpec((None, tr, cdim), lambda sh, i, c_ref: (sh, i, 0))],
            out_specs=pl.BlockSpec((None, tr, cdim), lambda sh, i, c_ref: (sh, i, 0))),
        out_shape=jax.ShapeDtypeStruct((n_sh, h, cdim), out_dtype),
        compiler_params=_cparams(2),
    )(c, v, recv)


def _sum4(got, own, name):
    _, r, cdim = got.shape
    tr = _row_tile(r, 512)

    def body(me_ref, p_ref, own_ref, o_ref):
        mine = own_ref[...].astype(F32)
        acc = None
        for j in range(N_CHIPS):
            term = jnp.where(me_ref[0] == j, mine, p_ref[j].astype(F32))
            acc = term if acc is None else acc + term
        o_ref[...] = acc

    me = (2 * lax.axis_index("x") + lax.axis_index("y")).astype(jnp.int32).reshape(1)
    return pl.pallas_call(
        body, name=name,
        grid_spec=pltpu.PrefetchScalarGridSpec(
            num_scalar_prefetch=1, grid=(r // tr,),
            in_specs=[pl.BlockSpec((N_CHIPS, tr, cdim), lambda i, me_ref: (0, i, 0)),
                      pl.BlockSpec((None, tr, cdim), lambda i, me_ref: (me_ref[0], i, 0))],
            out_specs=pl.BlockSpec((tr, cdim), lambda i, me_ref: (i, 0))),
        out_shape=jax.ShapeDtypeStruct((r, cdim), F32),
        compiler_params=_cparams(1),
    )(me, got, own)


def _adamw_math(w, g, m, v):
    c1 = 1.0 - ADAM_B1 ** ADAM_STEP
    c2 = 1.0 - ADAM_B2 ** ADAM_STEP
    nm = ADAM_B1 * m + (1.0 - ADAM_B1) * g
    nv = ADAM_B2 * v + (1.0 - ADAM_B2) * (g * g)
    delta = -ADAM_LR * ((nm / c1) / (jnp.sqrt(nv / c2) + ADAM_EPS) + ADAM_WD * w)
    return delta, nm, nv


def _adamw(w, g, m, v, name):
    r, cdim = w.shape
    tr = _row_tile(r, 256)

    def body(w_ref, g_ref, m_ref, v_ref, d_ref, nm_ref, nv_ref):
        d_ref[...], nm_ref[...], nv_ref[...] = _adamw_math(w_ref[...], g_ref[...], m_ref[...], v_ref[...])

    spec = pl.BlockSpec((tr, cdim), lambda i: (i, 0))
    return pl.pallas_call(
        body, name=name, grid=(r // tr,),
        in_specs=[spec] * 4, out_specs=[spec] * 3,
        out_shape=[jax.ShapeDtypeStruct((r, cdim), F32)] * 3,
        compiler_params=_cparams(1),
    )(w, g, m, v)


def _adamw_many(ws, gs, ms, vs, name, per_layer):
    n = len(ws)

    def body(*refs):
        for k in range(n):
            w, g, m, v = (refs[j * n + k][...] for j in range(4))
            outs = _adamw_math(w, g, m, v)
            for j in range(3):
                refs[(4 + j) * n + k][...] = outs[j]

    shapes = [jax.ShapeDtypeStruct(w.shape, F32) for w in ws]
    if per_layer:
        specs = [pl.BlockSpec((None,) + w.shape[1:], lambda l, nd=w.ndim: (l,) + (0,) * (nd - 1)) for w in ws]
        call = pl.pallas_call(body, name=name, grid=(N_LAYERS,), in_specs=specs * 4, out_specs=specs * 3,
                              out_shape=shapes * 3, compiler_params=_cparams(1))
    else:
        call = pl.pallas_call(body, name=name, out_shape=shapes * 3, compiler_params=_cparams())
    outs = call(*ws, *gs, *ms, *vs)
    return outs[0:n], outs[n:2 * n], outs[2 * n:3 * n]


def _cmul(ar, ai, br, bi):
    return ar * br - ai * bi, ar * bi + ai * br


def _discretise(a_re, a_im, log_dt, b_re, b_im):
    dt = jnp.exp(log_dt)
    mag = jnp.exp(a_re * dt)
    ab_re = mag * jnp.cos(a_im * dt)
    ab_im = mag * jnp.sin(a_im * dt)
    num_re = ab_re - 1.0
    num_im = ab_im
    den = a_re * a_re + a_im * a_im
    f_re = (num_re * a_re + num_im * a_im) / den
    f_im = (num_im * a_re - num_re * a_im) / den
    bb_re = f_re * b_re - f_im * b_im
    bb_im = f_re * b_im + f_im * b_re
    return ab_re, ab_im, bb_re, bb_im


def _disc_shapes():
    col = jax.ShapeDtypeStruct((1, N_STATES), F32)
    mat = jax.ShapeDtypeStruct((SSM_GROUP, N_STATES), F32)
    return col, mat


def _group_mask():
    row = lax.broadcasted_iota(jnp.int32, (CH_W, CH_S), 0)
    col = lax.broadcasted_iota(jnp.int32, (CH_W, CH_S), 1)
    return jnp.right_shift(row, SSM_GROUP.bit_length() - 1) == jnp.right_shift(col, SSM_STATE.bit_length() - 1)


def _block_diag(v, j):
    blk = v[:, CH_S * j:CH_S * (j + 1)]
    return jnp.where(_group_mask(), jnp.concatenate([blk] * (CH_W // SSM_GROUP), axis=0), 0.0)


def _block_diag_t(m):
    kept = jnp.where(_group_mask(), m, 0.0)
    return kept.reshape(CH_W // SSM_GROUP, SSM_GROUP, CH_S).sum(axis=0)


def _disc_fwd(a_re, a_im, log_dt, b_re, b_im, c_re, c_im, length):
    wide = jax.ShapeDtypeStruct((SSM_CHUNKS, CH_W, 2 * CH_S), BF16)
    tall = jax.ShapeDtypeStruct((SSM_CHUNKS, 2 * CH_S, CH_W), BF16)
    tab = jax.ShapeDtypeStruct((SSM_CHUNKS, length, 2 * CH_S), F32)

    def body(ar, ai, ld, br, bi, cr, ci, wb_ref, wbt_ref, wct_ref, wc_ref, tab_ref, rev_ref):
        ab_re, ab_im, bb_re, bb_im = _discretise(ar[...], ai[...], ld[...], br[...], bi[...])
        ccr, cci = cr[...], -ci[...]
        for j in range(SSM_CHUNKS):
            for lo, (vb, vc) in ((0, (bb_re, ccr)), (CH_S, (bb_im, cci))):
                mb, mc = _block_diag(vb, j), _block_diag(vc, j)
                wb_ref[j, :, lo:lo + CH_S] = mb.astype(BF16)
                wbt_ref[j, lo:lo + CH_S, :] = mb.T.astype(BF16)
                wct_ref[j, :, lo:lo + CH_S] = mc.astype(BF16)
                wc_ref[j, lo:lo + CH_S, :] = mc.T.astype(BF16)

        def step(j, carry):
            pr, pi = carry
            back = length - 1 - j
            for c in range(SSM_CHUNKS):
                lanes = slice(CH_S * c, CH_S * (c + 1))
                tab_ref[c, pl.ds(j, 1), 0:CH_S] = pr[:, lanes]
                tab_ref[c, pl.ds(j, 1), CH_S:2 * CH_S] = pi[:, lanes]
                rev_ref[c, pl.ds(back, 1), 0:CH_S] = pr[:, lanes]
                rev_ref[c, pl.ds(back, 1), CH_S:2 * CH_S] = -pi[:, lanes]
            return _cmul(pr, pi, ab_re, ab_im)

        lax.fori_loop(0, length, step, (ab_re, ab_im))

    return pl.pallas_call(body, name="ssm_discretise", out_shape=[wide, tall, wide, tall, tab, tab],
                          compiler_params=_cparams())(a_re, a_im, log_dt, b_re, b_im, c_re, c_im)


def _disc_bwd(a_re, a_im, log_dt, b_re, b_im, da, dwb, dwc):
    col, mat = _disc_shapes()

    def body(ar, ai, ld, br, bi, da_ref, dwb_ref, dwc_ref, o0, o1, o2, o3, o4, dcr_ref, dci_ref):
        g_ab = [jnp.concatenate([jnp.sum(da_ref[j, :, lo:lo + CH_S], axis=0, keepdims=True) for j in range(SSM_CHUNKS)],
                                axis=-1) for lo in (0, CH_S)]
        g_bb = [jnp.concatenate([_block_diag_t(dwb_ref[j, :, lo:lo + CH_S]) for j in range(SSM_CHUNKS)], axis=-1)
                for lo in (0, CH_S)]
        for ref, lo, sign in ((dcr_ref, 0, 1.0), (dci_ref, CH_S, -1.0)):
            ref[...] = sign * jnp.concatenate([_block_diag_t(dwc_ref[j, lo:lo + CH_S, :].T) for j in range(SSM_CHUNKS)],
                                              axis=-1)
        _, vjp = jax.vjp(_discretise, ar[...], ai[...], ld[...], br[...], bi[...])
        grads = vjp((g_ab[0], g_ab[1], g_bb[0], g_bb[1]))
        for o, val in zip((o0, o1, o2, o3, o4), grads):
            o[...] = val

    return pl.pallas_call(body, name="ssm_discretise_bwd", out_shape=[col, col, col, mat, mat, mat, mat],
                          compiler_params=_cparams())(a_re, a_im, log_dt, b_re, b_im, da, dwb, dwc)


def _interleave_chunks(v):
    rows, width = v.shape
    return pltpu.einshape("cjw->jcw", v.reshape(SUBLANES, rows // SUBLANES, width)).reshape(rows, width)


def _time_order(v):
    rows, width = v.shape
    return pltpu.einshape("jcw->cjw", v.reshape(rows // SUBLANES, SUBLANES, width)).reshape(rows, width)


def _head_ones():
    r = jnp.arange(ATTN_WIDTH) // HEAD_DIM
    return jnp.where(r[:, None] == r[None, :], 1.0 / HEAD_DIM, 0.0).astype(BF16)


def _in_proj(h, g1, w_in_l, qg, kg):
    s = h.shape[0]
    tm = _row_tile(s, 512)

    def body(h_ref, g_ref, w_ref, qg_ref, kg_ref, ones_ref, proj_ref, qkv_ref):
        x = h_ref[...]
        r = lax.rsqrt(jnp.mean(x * x, axis=-1, keepdims=True) + RMS_EPS)
        hn = (x * r * g_ref[...]).astype(BF16)
        for sh in range(N_CHIPS):
            proj_ref[:, IN_SHARD * sh:IN_SHARD * (sh + 1)] = _dot(hn, w_ref[sh])
        ones = ones_ref[...]
        q = proj_ref[:, 1024:1536]
        k = proj_ref[:, 1536:2048]
        rq = lax.rsqrt(_dot_hilo(q * q, ones) + RMS_EPS)
        rk = lax.rsqrt(_dot_hilo(k * k, ones) + RMS_EPS)
        qkv_ref[:, 0:512] = (q * rq * qg_ref[...] * ATTN_SCALE).astype(BF16)
        qkv_ref[:, 512:1024] = (k * rk * kg_ref[...]).astype(BF16)
        qkv_ref[:, 1024:1536] = proj_ref[:, 2048:2560].astype(BF16)

    full = lambda shape: pl.BlockSpec(shape, lambda i: (0,) * len(shape))
    return pl.pallas_call(
        body, name="in_proj", grid=(s // tm,),
        in_specs=[pl.BlockSpec((tm, D_MODEL), lambda i: (i, 0)), full((1, D_MODEL)),
                  full((N_CHIPS, D_MODEL, IN_SHARD)),
                  full((1, ATTN_WIDTH)), full((1, ATTN_WIDTH)), full((ATTN_WIDTH, ATTN_WIDTH))],
        out_specs=[pl.BlockSpec((tm, IN_COLS), lambda i: (i, 0)), pl.BlockSpec((tm, 3 * ATTN_WIDTH), lambda i: (i, 0))],
        out_shape=[jax.ShapeDtypeStruct((s, IN_COLS), F32), jax.ShapeDtypeStruct((s, 3 * ATTN_WIDTH), BF16)],
        compiler_params=_cparams(1),
    )(h, g1, w_in_l, qg, kg, _head_ones())


def _row_bcast(ref, k, lo):
    return jnp.broadcast_to(ref[pl.ds(k, 1), lo:lo + CH_S], (SUBLANES, CH_S))


def _chunk_scan(x_ref, tab_ref, carry_ref, length, reverse, tail=None):
    row = lax.broadcasted_iota(jnp.int32, (SUBLANES, CH_S), 0)
    one, full = (length - 1, 0) if reverse else (0, length - 1)
    ar, ai = _row_bcast(tab_ref, one, 0), _row_bcast(tab_ref, one, CH_S)
    fr, fi = _row_bcast(tab_ref, full, 0), _row_bcast(tab_ref, full, CH_S)
    step = lambda jj: (length - 1 - jj) if reverse else jj

    def local(jj, carry):
        cr, ci = carry
        r0 = pl.multiple_of(step(jj) * SUBLANES, SUBLANES)
        xr = x_ref[pl.ds(r0, SUBLANES), 0:CH_S] + (ar * cr - ai * ci)
        xi = x_ref[pl.ds(r0, SUBLANES), CH_S:2 * CH_S] + (ar * ci + ai * cr)
        x_ref[pl.ds(r0, SUBLANES), 0:CH_S] = xr
        x_ref[pl.ds(r0, SUBLANES), CH_S:2 * CH_S] = xi
        return xr, xi

    zero = jnp.zeros((SUBLANES, CH_S), F32)
    er, ei = lax.fori_loop(0, length, local, (zero, zero))

    first, shift = (SUBLANES - 1, SUBLANES - 1) if reverse else (0, 1)
    hr = jnp.where(row == first, carry_ref[:, 0:CH_S], 0.0)
    hi = jnp.where(row == first, carry_ref[:, CH_S:2 * CH_S], 0.0)
    sr, si = pltpu.roll(er, shift, 0), pltpu.roll(ei, shift, 0)
    for k in range(1, SUBLANES):
        tr, ti = pltpu.roll(hr, shift, 0), pltpu.roll(hi, shift, 0)
        here = row == ((SUBLANES - 1 - k) if reverse else k)
        hr, hi = (jnp.where(here, fr * tr - fi * ti + sr, hr), jnp.where(here, fr * ti + fi * tr + si, hi))
    last = 0 if reverse else SUBLANES - 1
    outr, outi = fr * hr - fi * hi + er, fr * hi + fi * hr + ei
    carry_ref[:, 0:CH_S] = jnp.broadcast_to(outr[last:last + 1, :], (SUBLANES, CH_S))
    carry_ref[:, CH_S:2 * CH_S] = jnp.broadcast_to(outi[last:last + 1, :], (SUBLANES, CH_S))

    def fix(jj, carry):
        j = step(jj)
        r0 = pl.multiple_of(j * SUBLANES, SUBLANES)
        pr, pi = _row_bcast(tab_ref, j, 0), _row_bcast(tab_ref, j, CH_S)
        xr = x_ref[pl.ds(r0, SUBLANES), 0:CH_S] + (pr * hr - pi * hi)
        xi = x_ref[pl.ds(r0, SUBLANES), CH_S:2 * CH_S] + (pr * hi + pi * hr)
        x_ref[pl.ds(r0, SUBLANES), 0:CH_S] = xr
        x_ref[pl.ds(r0, SUBLANES), CH_S:2 * CH_S] = xi
        if tail is None:
            return carry
        return tail(r0, xr, xi, carry)

    return fix, (hr, hi)


def _ssm_scan_fwd(proj, wb, tab, wc, gather=None, gather_bases=None):
    s = proj.shape[0]
    tm = _row_tile(s, SCAN_TILE)
    nt = s // tm
    length = tm // SUBLANES
    gather = [] if gather is None else gather
    ng = len(gather)

    def body(*refs):
        u_ref, wb_ref, tab_ref, wc_ref = refs[0:4]
        g_ins = refs[4:4 + ng]
        xs_ref, y_ref = refs[4 + ng:6 + ng]
        g_outs = refs[6 + ng:6 + 2 * ng]
        carry_ref = refs[6 + 2 * ng]
        sems = refs[7 + 2 * ng:]
        j, i = pl.program_id(0), pl.program_id(1)

        @pl.when(i == 0)
        def _():
            carry_ref[...] = jnp.zeros_like(carry_ref)

        if ng:
            @pl.when(jnp.logical_and(j == 0, i == 0))
            def _():
                _gather_start(g_ins, gather_bases, g_outs, sems)

        xs_ref[...] = _dot(_interleave_chunks(u_ref[...]).astype(BF16), wb_ref[...])
        fix, start = _chunk_scan(xs_ref, tab_ref, carry_ref, length, reverse=False)
        lax.fori_loop(0, length, fix, start, unroll=2)
        y_ref[...] = _time_order(_dot(xs_ref[...].astype(BF16), wc_ref[...]))

        if ng:
            @pl.when(jnp.logical_and(j == SSM_CHUNKS - 1, i == nt - 1))
            def _():
                _gather_finish(g_ins, gather_bases, g_outs, sems)

    outs = pl.pallas_call(
        body, name="ssm_scan_gather" if ng else "ssm_scan", grid=(SSM_CHUNKS, nt),
        in_specs=[pl.BlockSpec((tm, CH_W), lambda j, i: (i, j)),
                  pl.BlockSpec((None, CH_W, 2 * CH_S), lambda j, i: (j, 0, 0)),
                  pl.BlockSpec((None, length, 2 * CH_S), lambda j, i: (j, 0, 0)),
                  pl.BlockSpec((None, 2 * CH_S, CH_W), lambda j, i: (j, 0, 0))] + [ANY] * ng,
        out_specs=[pl.BlockSpec((None, tm, 2 * CH_S), lambda j, i: (j, i, 0)),
                   pl.BlockSpec((tm, CH_W), lambda j, i: (i, j))] + [ANY] * ng,
        out_shape=[jax.ShapeDtypeStruct((SSM_CHUNKS, s, 2 * CH_S), F32), jax.ShapeDtypeStruct((s, SSM_WIDTH), F32)]
        + _gather_outputs(gather),
        scratch_shapes=[pltpu.VMEM((SUBLANES, 2 * CH_S), F32)] + (_gather_sems(ng) if ng else []),
        compiler_params=_cparams(2),
    )(proj, wb, tab, wc, *gather)
    return outs[0], outs[1], (_gather_own(outs[2:], gather, gather_bases) if ng else [])


def _glu_forward(y, u, d, wg_ref, bg):
    yf = y + d * u
    z = _gelu(yf)
    zb = z.astype(BF16)
    zz = jnp.concatenate([_dot(zb, wg_ref[sh]) for sh in range(N_CHIPS)], axis=-1) + bg
    return yf, z, zz[:, 0:SSM_WIDTH], zz[:, SSM_WIDTH:2 * SSM_WIDTH]


def _ssm_glu_fwd(y, proj, d, w_glu_l, b_glu):
    s = y.shape[0]
    tm = _row_tile(s, 1024)

    def body(y_ref, u_ref, gs_ref, d_ref, wg_ref, bg_ref, o_ref):
        _, _, val, gate = _glu_forward(y_ref[...], u_ref[...], d_ref[...], wg_ref, bg_ref[...])
        gs = gs_ref[...]
        o_ref[...] = val * _sigmoid(gate) * (gs * _sigmoid(gs))

    row = lambda i: (i, 0)
    return pl.pallas_call(
        body, name="ssm_glu", grid=(s // tm,),
        in_specs=[pl.BlockSpec((tm, SSM_WIDTH), row), pl.BlockSpec((tm, SSM_WIDTH), row),
                  pl.BlockSpec((tm, SSM_WIDTH), lambda i: (i, 1)), pl.BlockSpec((1, SSM_WIDTH), lambda i: (0, 0)),
                  pl.BlockSpec((N_CHIPS, SSM_WIDTH, ROW_SHARD), lambda i: (0, 0, 0)),
                  pl.BlockSpec((1, 2 * SSM_WIDTH), lambda i: (0, 0))],
        out_specs=pl.BlockSpec((tm, SSM_WIDTH), row),
        out_shape=jax.ShapeDtypeStruct((s, SSM_WIDTH), F32),
        compiler_params=_cparams(1),
    )(y, proj, proj, d, w_glu_l, b_glu)


def _tri(kind):
    r = jnp.arange(ATTN_BLOCK)
    if kind == "suffix_incl":
        m = r[:, None] >= r[None, :]
    else:
        m = r[:, None] < r[None, :]
    return jnp.concatenate([m, jnp.ones_like(m)], axis=1).astype(BF16)


def _head_masks():
    lane = lax.broadcasted_iota(jnp.int32, (1, 2 * HEAD_DIM), 1)
    return [lane < HEAD_DIM, lane >= HEAD_DIM]


def _chain_step(t, base, n_sub, first, q_ref, k_ref, tri_ref, l_scr, per_chain):
    tb = ATTN_BLOCK
    row = lax.broadcasted_iota(jnp.int32, (tb, tb), 0)
    col = lax.broadcasted_iota(jnp.int32, (tb, tb), 1)
    masks = _head_masks()
    blks = [base + a - t for a in range(n_sub)]
    r0s = [pl.multiple_of(jnp.maximum(blk, 0) * tb, tb) for blk in blks]
    zs = []
    for a in range(n_sub):
        kb = k_ref[pl.ds(r0s[a], tb), :]
        qa = q_ref[a * tb:(a + 1) * tb, :]
        for mask in masks:
            zs.append(_dot_nt(jnp.where(mask, qa, jnp.zeros_like(qa)), kb))
    parts = []
    for z in zs:
        ls = jnp.minimum(-z, 0.0) - jnp.log(1.0 + jnp.exp(-jnp.abs(z)))
        if first:
            ls = jnp.where(col < row, ls, 0.0)
        parts.append(_split_hilo(ls))
    tri = tri_ref[...]
    sums = [_dot(hi, tri) + _dot(lo, tri) for hi, lo in parts]
    top = None
    ws = []
    for c, (z, sm) in enumerate(zip(zs, sums)):
        if first:
            lsum = jnp.zeros((tb, tb), F32)
        else:
            lsum = l_scr[c] + jnp.where(blks[c // 2] >= 0, 0.0, -1e30)
        w = jnp.exp(z + sm[:, 0:tb] + lsum)
        if first:
            w = jnp.where(col < row, w, 0.0)
        ws.append(w)
        lsum = lsum + sm[:, tb:2 * tb]
        l_scr[c] = lsum
        top = lsum if top is None else jnp.maximum(top, lsum)
    for c, (z, w) in enumerate(zip(zs, ws)):
        per_chain(c // 2, c % 2, c, r0s[c // 2], z, w)
    return jnp.max(top)


def _chain_sweep(base, n_sub, q_ref, k_ref, tri_ref, l_scr, per_chain):
    top = _chain_step(0, base, n_sub, True, q_ref, k_ref, tri_ref, l_scr, functools.partial(per_chain, 0))

    def cond(carry):
        t, top = carry
        return jnp.logical_and(t <= base + n_sub - 1, top > EXP_ZERO)

    def step(carry):
        t, _ = carry
        return t + 1, _chain_step(t, base, n_sub, False, q_ref, k_ref, tri_ref, l_scr, functools.partial(per_chain, t))

    steps, _ = lax.while_loop(cond, step, (jnp.int32(1), top))
    return steps


ATTN_SUB_FWD = 8
ATTN_SUB_BWD = 4


def _attn_fwd(qkv, proj, gather=None, gather_bases=None):
    s = qkv.shape[0]
    tb = ATTN_BLOCK
    n_sub = min(ATTN_SUB_FWD, s // tb)
    tq = n_sub * tb
    n_hp = ATTN_WIDTH // (2 * HEAD_DIM)
    gather = [] if gather is None else gather
    ng = len(gather)

    def body(*refs):
        q_ref, k_ref, v_ref, g_ref, tri_ref = refs[0:5]
        g_ins = refs[5:5 + ng]
        o_ref, ya_ref = refs[5 + ng:7 + ng]
        g_outs = refs[7 + ng:7 + 2 * ng]
        l_scr = refs[7 + 2 * ng]
        sems = refs[8 + 2 * ng:]
        i = pl.program_id(1)
        masks = _head_masks()
        o_ref[...] = jnp.zeros_like(o_ref)

        if ng:
            @pl.when(jnp.logical_and(pl.program_id(0) == 0, i == 0))
            def _():
                _gather_start(g_ins, gather_bases, g_outs, sems)

        def per_chain(t, a, h, c, r0, z, w):
            vb = v_ref[pl.ds(r0, tb), :]
            vb = jnp.where(masks[h], vb, jnp.zeros_like(vb))
            o_ref[a * tb:(a + 1) * tb, :] += _dot(w.astype(BF16), vb)

        _chain_sweep(i * n_sub, n_sub, q_ref, k_ref, tri_ref, l_scr, per_chain)
        g = g_ref[...]
        ya_ref[...] = o_ref[...] * (g * _sigmoid(g))

        if ng:
            @pl.when(jnp.logical_and(pl.program_id(0) == n_hp - 1, i == s // tq - 1))
            def _():
                _gather_finish(g_ins, gather_bases, g_outs, sems)

    hp_blk = lambda off: pl.BlockSpec((tq, 2 * HEAD_DIM), lambda hp, i: (i, off + hp))
    res = lambda off: pl.BlockSpec((s, 2 * HEAD_DIM), lambda hp, i: (0, off + hp))
    outs = pl.pallas_call(
        body, name="attn_fwd_gather" if ng else "attn_fwd", grid=(n_hp, s // tq),
        in_specs=[hp_blk(0), res(4), res(8), hp_blk(20), pl.BlockSpec((tb, 2 * tb), lambda hp, i: (0, 0))] + [ANY] * ng,
        out_specs=[hp_blk(0), hp_blk(0)] + [ANY] * ng,
        out_shape=[jax.ShapeDtypeStruct((s, ATTN_WIDTH), F32)] * 2 + _gather_outputs(gather),
        scratch_shapes=[pltpu.VMEM((2 * n_sub, tb, tb), F32)] + (_gather_sems(ng) if ng else []),
        compiler_params=_cparams(2),
    )(qkv, qkv, qkv, proj, _tri("suffix_incl"), *gather)
    return outs[0], outs[1], (_gather_own(outs[2:], gather, gather_bases) if ng else [])


def _rms_rows(x, g):
    r = lax.rsqrt(jnp.mean(x * x, axis=-1, keepdims=True) + RMS_EPS)
    return r, x * r * g


def _ple_forward(h1, p, g2, wpg_ref, wpp_ref):
    r2, hn2 = _rms_rows(h1, g2)
    hb = hn2.astype(BF16)
    gpre = _dot(hb[:, 0:ROW_SHARD], wpg_ref[0])
    for sh in range(1, N_CHIPS):
        gpre = gpre + _dot(hb[:, ROW_SHARD * sh:ROW_SHARD * (sh + 1)], wpg_ref[sh])
    gate = _sigmoid(gpre)
    pb = p.astype(BF16)
    pp = jnp.concatenate([_dot(pb, wpp_ref[sh]) for sh in range(N_CHIPS)], axis=-1)
    return r2, hb, gate, pp


def _colsum8(a):
    t = a.shape[0]
    return a.reshape(t // SUBLANES, SUBLANES, a.shape[1]).sum(axis=0)


def _sq_err_grad(y, target):
    e = y - target
    sq = _colsum8(e * e)
    part = sq[:, 0:128]
    for b in range(1, D_MODEL // 128):
        part = part + sq[:, 128 * b:128 * (b + 1)]
    return e / D_MODEL, part


def _out_ple(h, ys, ya, p, g2, w_out_l, w_pg_l, w_pp_l, target=None):
    s = h.shape[0]
    tm = _row_tile(s, 512)
    last = target is not None

    def body(*refs):
        h_ref, ys_ref, ya_ref, p_ref, g_ref, wo_ref, wpg_ref, wpp_ref = refs[0:8]
        h1_ref, h2_ref = refs[8 + last], refs[9 + last]
        ysb = ys_ref[...].astype(BF16)
        yab = ya_ref[...].astype(BF16)
        h1 = h_ref[...]
        for sh, src in enumerate((ysb[:, 0:ROW_SHARD], ysb[:, ROW_SHARD:], yab[:, 0:ROW_SHARD], yab[:, ROW_SHARD:])):
            h1 = h1 + _dot(src, wo_ref[sh])
        _, _, gate, pp = _ple_forward(h1, p_ref[...], g_ref[...], wpg_ref, wpp_ref)
        h1_ref[...] = h1
        h2 = h1 + gate * pp
        if last:
            acc_ref = refs[11]

            @pl.when(pl.program_id(0) == 0)
            def _():
                acc_ref[...] = jnp.zeros_like(acc_ref)

            h2_ref[...], part = _sq_err_grad(h2, refs[8][...])
            acc_ref[...] += part
        else:
            h2_ref[...] = h2

    row = lambda i: (i, 0)
    big = pl.BlockSpec((tm, D_MODEL), row)
    wspec = lambda r, cdim: pl.BlockSpec((N_CHIPS, r, cdim), lambda i: (0, 0, 0))
    acc = pl.BlockSpec((SUBLANES, 128), lambda i: (0, 0))
    return pl.pallas_call(
        body, name="out_ple_loss" if last else "out_ple", grid=(s // tm,),
        in_specs=[big, pl.BlockSpec((tm, SSM_WIDTH), row), pl.BlockSpec((tm, ATTN_WIDTH), row),
                  pl.BlockSpec((tm, PLE_DIM), row), pl.BlockSpec((1, D_MODEL), lambda i: (0, 0)),
                  wspec(ROW_SHARD, D_MODEL), wspec(ROW_SHARD, D_MODEL), wspec(PLE_DIM, ROW_SHARD)] + [big] * last,
        out_specs=[big] * 2 + [acc] * last,
        out_shape=[jax.ShapeDtypeStruct((s, D_MODEL), F32)] * 2 + [jax.ShapeDtypeStruct((SUBLANES, 128), F32)] * last,
        compiler_params=_cparams(1),
    )(h, ys, ya, p, g2, w_out_l, w_pg_l, w_pp_l, *([target] if last else []))


def _rms_bwd(x, r, g, dy):
    gdy = g * dy
    dx = r * gdy - x * (r * r * r) * jnp.mean(x * gdy, axis=-1, keepdims=True)
    return dx, x * r * dy


def _out_ple_bwd(dh2, h1, p, g2, w_out_l, w_pg_l, w_pp_l):
    s = h1.shape[0]
    tm = _row_tile(s, 512)

    def body(dh2_ref, h1_ref, p_ref, g_ref, wo_ref, wpg_ref, wpp_ref,
             dh1_ref, dmix_ref, hn_ref, dgp_ref, dpp_ref, dh1b_ref, dg_ref):
        @pl.when(pl.program_id(0) == 0)
        def _():
            dg_ref[...] = jnp.zeros_like(dg_ref)

        h1 = h1_ref[...]
        dh2 = dh2_ref[...]
        g2v = g_ref[...]
        r2, hb, gate, pp = _ple_forward(h1, p_ref[...], g2v, wpg_ref, wpp_ref)
        dgp = (dh2 * pp) * gate * (1.0 - gate)
        dgpb = dgp.astype(BF16)
        dhn = jnp.concatenate([_dot_nt(dgpb, wpg_ref[sh]) for sh in range(N_CHIPS)], axis=-1)
        dx, dgrow = _rms_bwd(h1, r2, g2v, dhn)
        dh1 = dh2 + dx
        dh1b = dh1.astype(BF16)
        dh1_ref[...] = dh1
        dh1b_ref[...] = dh1b
        hn_ref[...] = hb
        dgp_ref[...] = dgpb
        dpp_ref[...] = (dh2 * gate).astype(BF16)
        dg_ref[...] += _colsum8(dgrow)
        for sh in range(N_CHIPS):
            dmix_ref[:, ROW_SHARD * sh:ROW_SHARD * (sh + 1)] = _dot_nt(dh1b, wo_ref[sh])

    row = lambda i: (i, 0)
    wspec = lambda r, cdim: pl.BlockSpec((N_CHIPS, r, cdim), lambda i: (0, 0, 0))
    big = pl.BlockSpec((tm, D_MODEL), row)
    return pl.pallas_call(
        body, name="out_ple_bwd", grid=(s // tm,),
        in_specs=[big, big, pl.BlockSpec((tm, PLE_DIM), row), pl.BlockSpec((1, D_MODEL), lambda i: (0, 0)),
                  wspec(ROW_SHARD, D_MODEL), wspec(ROW_SHARD, D_MODEL), wspec(PLE_DIM, ROW_SHARD)],
        out_specs=[big] * 6 + [pl.BlockSpec((SUBLANES, D_MODEL), lambda i: (0, 0))],
        out_shape=[jax.ShapeDtypeStruct((s, D_MODEL), F32)] * 2 + [jax.ShapeDtypeStruct((s, D_MODEL), BF16)] * 4
        + [jax.ShapeDtypeStruct((SUBLANES, D_MODEL), F32)],
        compiler_params=_cparams(1),
    )(dh2, h1, p, g2, w_out_l, w_pg_l, w_pp_l)


def _tn_matmul(a, b, n_blocks, block_a, name, into=None, first_block=0, total_blocks=None):
    s = a.shape[0]
    tk = _row_tile(s, 1024)
    nk = s // tk
    total_blocks = n_blocks if total_blocks is None else total_blocks
    ka, nb = a.shape[1], b.shape[1]
    if block_a:
        ka //= n_blocks
    else:
        nb //= n_blocks

    def body(*refs):
        a_ref, b_ref, o_ref, acc_ref = refs[0], refs[1], refs[-2], refs[-1]

        @pl.when(pl.program_id(0) == 0)
        def _():
            acc_ref[...] = jnp.zeros_like(acc_ref)

        at = a_ref[...].astype(BF16).T
        bb = b_ref[...].astype(BF16)
        for sh in range(n_blocks):
            if block_a:
                acc_ref[sh] += _dot(at[ka * sh:ka * (sh + 1), :], bb)
            else:
                acc_ref[sh] += _dot(at, bb[:, nb * sh:nb * (sh + 1)])

        @pl.when(pl.program_id(0) == nk - 1)
        def _():
            o_ref[...] = acc_ref[...].astype(BF16)

    in_specs = [pl.BlockSpec((tk, a.shape[1]), lambda i: (i, 0)), pl.BlockSpec((tk, b.shape[1]), lambda i: (i, 0))]
    operands = [a, b]
    aliases = {}
    if into is not None:
        in_specs.append(ANY)
        operands.append(into)
        aliases = {2: 0}
    return pl.pallas_call(
        body, name=name, grid=(nk,),
        in_specs=in_specs,
        out_specs=pl.BlockSpec((n_blocks, ka, nb), lambda i: (first_block // n_blocks, 0, 0)),
        out_shape=jax.ShapeDtypeStruct((total_blocks, ka, nb), BF16),
        scratch_shapes=[pltpu.VMEM((n_blocks, ka, nb), F32)],
        input_output_aliases=aliases,
        compiler_params=_cparams(1),
    )(*operands)


def _attn_bwd(qkv, o, proj, dmix, scatter=None):
    scatter = [] if scatter is None else scatter
    nsc = len(scatter)
    s = qkv.shape[0]
    tb = ATTN_BLOCK
    nq = s // tb
    n_sub = min(ATTN_SUB_BWD, nq)
    tq = n_sub * tb
    n_chain = 2 * n_sub

    def body(*refs):
        q_ref, k_ref, v_ref, o_ref, g_ref, dya_ref, tri_s_ref, tri_p_ref = refs[0:8]
        sc_ins = refs[8:8 + nsc]
        dq_ref, dk_ref, dv_ref, dg_ref = refs[8 + nsc:12 + nsc]
        sc_outs = refs[12 + nsc:12 + 2 * nsc]
        do_scr, l_scr, g_scr, s_scr, w_scr = refs[12 + 2 * nsc:17 + 2 * nsc]
        sc_sems = refs[17 + 2 * nsc:]
        i = pl.program_id(1)
        base = i * n_sub

        if nsc:
            @pl.when(jnp.logical_and(pl.program_id(0) == 0, i == 0))
            def _():
                _scatter_start(sc_ins, sc_outs, sc_sems)

        @pl.when(i == 0)
        def _():
            dk_ref[...] = jnp.zeros_like(dk_ref)
            dv_ref[...] = jnp.zeros_like(dv_ref)

        g = g_ref[...]
        sg = _sigmoid(g)
        dya = dya_ref[...]
        do_scr[...] = (dya * (g * sg)).astype(BF16)
        dg_ref[...] = dya * o_ref[...] * (sg * (1.0 + g * (1.0 - sg)))
        dq_ref[...] = jnp.zeros_like(dq_ref)
        g_scr[...] = jnp.zeros_like(g_scr)
        masks = _head_masks()

        def keep(t, a, h, c, r0, z, w):
            s_scr[c, t] = _sigmoid(z).astype(BF16)
            w_scr[c, t] = w.astype(BF16)

        steps = _chain_sweep(base, n_sub, q_ref, k_ref, tri_s_ref, l_scr, keep)
        row = lax.broadcasted_iota(jnp.int32, (tb, tb), 0)
        col = lax.broadcasted_iota(jnp.int32, (tb, tb), 1)

        def back(it, carry):
            t = steps - 1 - it
            r0s = [pl.multiple_of(jnp.maximum(base + a - t, 0) * tb, tb) for a in range(n_sub)]
            qhs, dohs, khs, gws = [], [], [], []
            for a in range(n_sub):
                kb = k_ref[pl.ds(r0s[a], tb), :]
                vb = v_ref[pl.ds(r0s[a], tb), :]
                qa = q_ref[a * tb:(a + 1) * tb, :]
                doa = do_scr[a * tb:(a + 1) * tb, :]
                for h, mask in enumerate(masks):
                    qhs.append(jnp.where(mask, qa, jnp.zeros_like(qa)))
                    khs.append(jnp.where(mask, kb, jnp.zeros_like(kb)))
                    dohs.append(jnp.where(mask, doa, jnp.zeros_like(doa)))
                    gws.append(w_scr[2 * a + h, t].astype(F32) * _dot_nt(dohs[-1], vb))
            parts = [_split_hilo(gw) for gw in gws]
            tri = tri_p_ref[...]
            sums = [_dot(hi, tri) + _dot(lo, tri) for hi, lo in parts]
            dzs = []
            for c, (gw, sm) in enumerate(zip(gws, sums)):
                gsum = g_scr[c]
                dz = gw - (gw + sm[:, 0:tb] + gsum) * s_scr[c, t].astype(F32)
                dz = jnp.where(col < row + t * tb, dz, 0.0)
                g_scr[c] = gsum + sm[:, tb:2 * tb]
                dzs.append(dz.astype(BF16))
            for c, dzb in enumerate(dzs):
                a = c // 2
                dk_ref[pl.ds(r0s[a], tb), :] += _dot_tn(dzb, qhs[c])
                dv_ref[pl.ds(r0s[a], tb), :] += _dot_tn(w_scr[c, t], dohs[c])
                dq_ref[a * tb:(a + 1) * tb, :] += _dot(dzb, khs[c])
            return carry

        lax.fori_loop(0, steps, back, 0)

        if nsc:
            @pl.when(jnp.logical_and(pl.program_id(0) == n_hp - 1, i == s // tq - 1))
            def _():
                _scatter_finish(sc_ins, sc_outs, sc_sems)

    n_hp = ATTN_WIDTH // (2 * HEAD_DIM)
    hp_blk = lambda off: pl.BlockSpec((tq, 2 * HEAD_DIM), lambda hp, i: (i, off + hp))
    res = lambda off: pl.BlockSpec((s, 2 * HEAD_DIM), lambda hp, i: (0, off + hp))
    tri = pl.BlockSpec((tb, 2 * tb), lambda hp, i: (0, 0))
    outs = pl.pallas_call(
        body, name="attn_bwd_scatter" if nsc else "attn_bwd", grid=(n_hp, s // tq),
        in_specs=[hp_blk(0), res(4), res(8), hp_blk(0), hp_blk(20), hp_blk(4), tri, tri] + [ANY] * nsc,
        out_specs=[hp_blk(0), res(0), res(0), hp_blk(0)] + [ANY] * nsc,
        out_shape=[jax.ShapeDtypeStruct((s, ATTN_WIDTH), F32)] * 4 + [jax.ShapeDtypeStruct(a.shape, a.dtype) for a in scatter],
        scratch_shapes=[pltpu.VMEM((tq, 2 * HEAD_DIM), BF16), pltpu.VMEM((n_chain, tb, tb), F32),
                        pltpu.VMEM((n_chain, tb, tb), F32), pltpu.VMEM((n_chain, nq, tb, tb), BF16),
                        pltpu.VMEM((n_chain, nq, tb, tb), BF16)] + (_scatter_sems(nsc) if nsc else []),
        compiler_params=_cparams(2),
    )(qkv, qkv, qkv, o, proj, dmix, _tri("suffix_incl"), _tri("prefix_strict"), *scatter)
    return outs[0], outs[1], outs[2], outs[3], outs[4:]


def _ssm_glu_bwd(dmix, y, proj, d, w_glu_l, b_glu):
    s = y.shape[0]
    tm = _row_tile(s, 1024)

    def body(dys_ref, y_ref, u_ref, gs_ref, d_ref, wg_ref, bg_ref,
             dyf_ref, du_ref, dgs_ref, z_ref, dzz_ref, dd_ref, db_ref):
        @pl.when(pl.program_id(0) == 0)
        def _():
            dd_ref[...] = jnp.zeros_like(dd_ref)
            db_ref[...] = jnp.zeros_like(db_ref)

        u = u_ref[...]
        dv = d_ref[...]
        yf, z, val, gate = _glu_forward(y_ref[...], u, dv, wg_ref, bg_ref[...])
        gs = gs_ref[...]
        sgs = _sigmoid(gs)
        sgate = _sigmoid(gate)
        dys = dys_ref[...]
        dgv = dys * (gs * sgs)
        dgs_ref[...] = dys * (val * sgate) * (sgs * (1.0 + gs * (1.0 - sgs)))
        dzz = jnp.concatenate([dgv * sgate, dgv * val * sgate * (1.0 - sgate)], axis=-1)
        dzzb = dzz.astype(BF16)
        dz = _dot_nt(dzzb[:, 0:ROW_SHARD], wg_ref[0])
        for sh in range(1, N_CHIPS):
            dz = dz + _dot_nt(dzzb[:, ROW_SHARD * sh:ROW_SHARD * (sh + 1)], wg_ref[sh])
        dyf = dz * _gelu_grad(yf)
        dyf_ref[...] = dyf
        du_ref[...] = dyf * dv
        z_ref[...] = z.astype(BF16)
        dzz_ref[...] = dzzb
        dd_ref[...] += _colsum8(dyf * u)
        db_ref[...] += _colsum8(dzz)

    row = lambda i: (i, 0)
    half = pl.BlockSpec((tm, SSM_WIDTH), row)
    return pl.pallas_call(
        body, name="ssm_glu_bwd", grid=(s // tm,),
        in_specs=[half, half, half, pl.BlockSpec((tm, SSM_WIDTH), lambda i: (i, 1)),
                  pl.BlockSpec((1, SSM_WIDTH), lambda i: (0, 0)),
                  pl.BlockSpec((N_CHIPS, SSM_WIDTH, ROW_SHARD), lambda i: (0, 0, 0)),
                  pl.BlockSpec((1, 2 * SSM_WIDTH), lambda i: (0, 0))],
        out_specs=[half, half, half, half, pl.BlockSpec((tm, 2 * SSM_WIDTH), row),
                   pl.BlockSpec((SUBLANES, SSM_WIDTH), lambda i: (0, 0)),
                   pl.BlockSpec((SUBLANES, 2 * SSM_WIDTH), lambda i: (0, 0))],
        out_shape=[jax.ShapeDtypeStruct((s, SSM_WIDTH), F32)] * 3
        + [jax.ShapeDtypeStruct((s, SSM_WIDTH), BF16), jax.ShapeDtypeStruct((s, 2 * SSM_WIDTH), BF16),
           jax.ShapeDtypeStruct((SUBLANES, SSM_WIDTH), F32), jax.ShapeDtypeStruct((SUBLANES, 2 * SSM_WIDTH), F32)],
        compiler_params=_cparams(1),
    )(dmix, y, proj, proj, d, w_glu_l, b_glu)


def _ssm_scan_bwd(dyf, xs, proj, wct, tab_rev, wbt):
    s = dyf.shape[0]
    tm = _row_tile(s, SCAN_TILE)
    nt = s // tm
    length = tm // SUBLANES

    def body(dy_ref, xs_ref, u_ref, wct_ref, tab_ref, wbt_ref, du_ref, dwc_ref, dwb_ref, da_ref, lam_ref, carry_ref):
        @pl.when(pl.program_id(1) == 0)
        def _():
            carry_ref[...] = jnp.zeros_like(carry_ref)
            dwc_ref[...] = jnp.zeros_like(dwc_ref)
            dwb_ref[...] = jnp.zeros_like(dwb_ref)
            da_ref[...] = jnp.zeros_like(da_ref)

        dyp = _interleave_chunks(dy_ref[...]).astype(BF16)
        up = _interleave_chunks(u_ref[...]).astype(BF16)
        lam_ref[...] = _dot(dyp, wct_ref[...])

        def tail(r0, lr, li, carry):
            er, ei, dar, dai = carry
            xr = xs_ref[pl.ds(r0, SUBLANES), 0:CH_S]
            xi = xs_ref[pl.ds(r0, SUBLANES), CH_S:2 * CH_S]
            return lr, li, dar + (xr * er + xi * ei), dai + (xr * ei - xi * er)

        fix, (gr, gi) = _chunk_scan(lam_ref, tab_ref, carry_ref, length, reverse=True, tail=tail)
        zero = jnp.zeros((SUBLANES, CH_S), F32)
        _, _, dar, dai = lax.fori_loop(0, length, fix, (gr, gi, zero, zero), unroll=2)
        da_ref[:, 0:CH_S] += dar
        da_ref[:, CH_S:2 * CH_S] += dai
        lamb = lam_ref[...].astype(BF16)
        du_ref[...] = _time_order(_dot(lamb, wbt_ref[...]))
        dwc_ref[...] += _dot_tn(xs_ref[...].astype(BF16), dyp)
        dwb_ref[...] += _dot_tn(up, lamb)

    rev = lambda j, i: (nt - 1 - i, j)
    return pl.pallas_call(
        body, name="ssm_scan_bwd", grid=(SSM_CHUNKS, nt),
        in_specs=[pl.BlockSpec((tm, CH_W), rev),
                  pl.BlockSpec((None, tm, 2 * CH_S), lambda j, i: (j, nt - 1 - i, 0)),
                  pl.BlockSpec((tm, CH_W), rev),
                  pl.BlockSpec((None, CH_W, 2 * CH_S), lambda j, i: (j, 0, 0)),
                  pl.BlockSpec((None, length, 2 * CH_S), lambda j, i: (j, 0, 0)),
                  pl.BlockSpec((None, 2 * CH_S, CH_W), lambda j, i: (j, 0, 0))],
        out_specs=[pl.BlockSpec((tm, CH_W), rev),
                   pl.BlockSpec((None, 2 * CH_S, CH_W), lambda j, i: (j, 0, 0)),
                   pl.BlockSpec((None, CH_W, 2 * CH_S), lambda j, i: (j, 0, 0)),
                   pl.BlockSpec((None, SUBLANES, 2 * CH_S), lambda j, i: (j, 0, 0))],
        out_shape=[jax.ShapeDtypeStruct((s, SSM_WIDTH), F32),
                   jax.ShapeDtypeStruct((SSM_CHUNKS, 2 * CH_S, CH_W), F32),
                   jax.ShapeDtypeStruct((SSM_CHUNKS, CH_W, 2 * CH_S), F32),
                   jax.ShapeDtypeStruct((SSM_CHUNKS, SUBLANES, 2 * CH_S), F32)],
        scratch_shapes=[pltpu.VMEM((tm, 2 * CH_S), F32), pltpu.VMEM((SUBLANES, 2 * CH_S), F32)],
        compiler_params=_cparams(2),
    )(dyf, xs, proj, wct, tab_rev, wbt)


def _in_proj_bwd(h, g1, w_in_l, qg, kg, proj, du_a, du_b, dgs, dq, dk, dv, dga, dh1):
    s = h.shape[0]
    tm = _row_tile(s, 256)

    def body(h_ref, g_ref, w_ref, qg_ref, kg_ref, ones_ref, q_ref, k_ref, dua_ref, dub_ref, dgs_ref, dq_ref, dk_ref,
             dv_ref, dga_ref, dh1_ref, dh_ref, hn_ref, dp_ref, dg1_ref, dqg_ref, dkg_ref):
        @pl.when(pl.program_id(0) == 0)
        def _():
            dg1_ref[...] = jnp.zeros_like(dg1_ref)
            dqg_ref[...] = jnp.zeros_like(dqg_ref)
            dkg_ref[...] = jnp.zeros_like(dkg_ref)

        ones = ones_ref[...]

        def head_norm_bwd(x, gain, dy):
            r = lax.rsqrt(_dot_hilo(x * x, ones) + RMS_EPS)
            gdy = gain * dy
            dx = r * gdy - x * (r * r * r) * _dot_hilo(x * gdy, ones)
            return dx, x * r * dy

        dqr, dqg_rows = head_norm_bwd(q_ref[...], qg_ref[...], dq_ref[...] * ATTN_SCALE)
        dkr, dkg_rows = head_norm_bwd(k_ref[...], kg_ref[...], dk_ref[...])
        dqg_ref[...] += _colsum8(dqg_rows)
        dkg_ref[...] += _colsum8(dkg_rows)
        dp_ref[:, 0:512] = (dua_ref[...] + dub_ref[...]).astype(BF16)
        dp_ref[:, 512:1024] = dgs_ref[...].astype(BF16)
        dp_ref[:, 1024:1536] = dqr.astype(BF16)
        dp_ref[:, 1536:2048] = dkr.astype(BF16)
        dp_ref[:, 2048:2560] = dv_ref[...].astype(BF16)
        dp_ref[:, 2560:3072] = dga_ref[...].astype(BF16)
        dhn = _dot_nt(dp_ref[:, 0:IN_SHARD], w_ref[0])
        for sh in range(1, N_CHIPS):
            dhn = dhn + _dot_nt(dp_ref[:, IN_SHARD * sh:IN_SHARD * (sh + 1)], w_ref[sh])
        x = h_ref[...]
        gv = g_ref[...]
        r, hn = _rms_rows(x, gv)
        dx, dg_rows = _rms_bwd(x, r, gv, dhn)
        dh_ref[...] = dh1_ref[...] + dx
        hn_ref[...] = hn.astype(BF16)
        dg1_ref[...] += _colsum8(dg_rows)

    row = lambda i: (i, 0)
    full = lambda shape: pl.BlockSpec(shape, lambda i: (0,) * len(shape))
    big = pl.BlockSpec((tm, D_MODEL), row)
    half = pl.BlockSpec((tm, 512), row)
    return pl.pallas_call(
        body, name="in_proj_bwd", grid=(s // tm,),
        in_specs=[big, full((1, D_MODEL)), full((N_CHIPS, D_MODEL, IN_SHARD)),
                  full((1, ATTN_WIDTH)), full((1, ATTN_WIDTH)), full((ATTN_WIDTH, ATTN_WIDTH)),
                  pl.BlockSpec((tm, 512), lambda i: (i, 2)), pl.BlockSpec((tm, 512), lambda i: (i, 3)),
                  half, half, half, half, half, half, half, big],
        out_specs=[big, big, pl.BlockSpec((tm, IN_COLS), row), pl.BlockSpec((SUBLANES, D_MODEL), lambda i: (0, 0)),
                   pl.BlockSpec((SUBLANES, ATTN_WIDTH), lambda i: (0, 0)), pl.BlockSpec((SUBLANES, ATTN_WIDTH), lambda i: (0, 0))],
        out_shape=[jax.ShapeDtypeStruct((s, D_MODEL), F32), jax.ShapeDtypeStruct((s, D_MODEL), BF16),
                   jax.ShapeDtypeStruct((s, IN_COLS), BF16), jax.ShapeDtypeStruct((SUBLANES, D_MODEL), F32),
                   jax.ShapeDtypeStruct((SUBLANES, ATTN_WIDTH), F32), jax.ShapeDtypeStruct((SUBLANES, ATTN_WIDTH), F32)],
        compiler_params=_cparams(1),
    )(h, g1, w_in_l, qg, kg, _head_ones(), proj, proj, du_a, du_b, dgs, dq, dk, dv, dga, dh1)


SMALL_NAMES = ("mix_norm_g", "ssm_a_re", "ssm_a_im", "ssm_log_dt", "ssm_b_re", "ssm_b_im", "ssm_c_re", "ssm_c_im",
               "ssm_d", "ssm_b_glu", "q_norm_g", "k_norm_g", "ple_norm_g")
SMALL_4D = ("ssm_b_re", "ssm_b_im", "ssm_c_re", "ssm_c_im")
BIG_NAMES = ("w_in", "ssm_w_glu", "w_out", "w_ple_gate", "w_ple_proj")


def _ssm_setup(sm, layer, length):
    col = lambda a: a[layer].reshape(1, N_STATES)
    a_re, a_im = col(sm["ssm_a_re"]), col(sm["ssm_a_im"])
    log_dt = jnp.repeat(sm["ssm_log_dt"][layer], SSM_STATE).reshape(1, N_STATES)
    b_re = sm["ssm_b_re"][layer].reshape(N_STATES, SSM_GROUP).T
    b_im = sm["ssm_b_im"][layer].reshape(N_STATES, SSM_GROUP).T
    by_channel = lambda c: c[layer].transpose(1, 0, 2).reshape(SSM_GROUP, N_STATES)
    disc_in = (a_re, a_im, log_dt, b_re, b_im)
    wb, wbt, wct, wc, tab, tab_rev = _disc_fwd(*disc_in, by_channel(sm["ssm_c_re"]), by_channel(sm["ssm_c_im"]), length)
    return dict(disc_in=disc_in, wb=wb, wbt=wbt, wc=wc, wct=wct, tab=tab, tab_rev=tab_rev)


def _whole_blocks(names, gathered):
    return {n: g.reshape(N_CHIPS, 2 * g.shape[2], g.shape[3]) for n, g in zip(names, gathered)}


def _local_step(x, p, target, sm, w_in0, local=None, gathered=None, layer1_hook=None):
    wg = [dict(w_in=w_in0), {}] if gathered is None else gathered
    tile8 = lambda a: jnp.tile(a, ATTN_WIDTH // HEAD_DIM).reshape(1, ATTN_WIDTH)
    saved = []
    h = x
    for l in range(N_LAYERS):
        ssm = _ssm_setup(sm, l, _row_tile(x.shape[0], SCAN_TILE) // SUBLANES)
        g1 = sm["mix_norm_g"][l].reshape(1, D_MODEL)
        g2 = sm["ple_norm_g"][l].reshape(1, D_MODEL)
        qg, kg = tile8(sm["q_norm_g"][l]), tile8(sm["k_norm_g"][l])
        dsk = sm["ssm_d"][l].reshape(1, SSM_WIDTH)
        bgl = sm["ssm_b_glu"][l].reshape(1, 2 * SSM_WIDTH)
        proj, qkv = _in_proj(h, g1, wg[l]["w_in"], qg, kg)
        if l == 0 and local is not None:
            rest = BIG_NAMES[1:]
            xs, y, got = _ssm_scan_fwd(proj, ssm["wb"], ssm["tab"], ssm["wc"], [local[n] for n in rest], [0] * len(rest))
            wg[0].update(_whole_blocks(rest, got))
            ys = _ssm_glu_fwd(y, proj, dsk, wg[0]["ssm_w_glu"], bgl)
            o, ya, got = _attn_fwd(qkv, proj, [local[n] for n in BIG_NAMES], [2] * len(BIG_NAMES))
            wg[1].update(_whole_blocks(BIG_NAMES, got))
        else:
            xs, y, _ = _ssm_scan_fwd(proj, ssm["wb"], ssm["tab"], ssm["wc"])
            ys = _ssm_glu_fwd(y, proj, dsk, wg[l]["ssm_w_glu"], bgl)
            o, ya, _ = _attn_fwd(qkv, proj)
        tail = (target,) if l == N_LAYERS - 1 else ()
        h1, h2, *sq = _out_ple(h, ys, ya, p[l], g2, wg[l]["w_out"], wg[l]["w_ple_gate"], wg[l]["w_ple_proj"], *tail)
        saved.append(dict(ssm=ssm, g1=g1, g2=g2, qg=qg, kg=kg, dsk=dsk, bgl=bgl, h=h, proj=proj, qkv=qkv, xs=xs, y=y,
                          ys=ys, o=o, ya=ya, h1=h1))
        h = h2
    dh = h
    loss = 0.5 * jnp.sum(sq[0]) / D_MODEL

    gbig = [{} for _ in range(N_LAYERS)]
    scattered = ([], [])
    gsm = {n: [None] * N_LAYERS for n in SMALL_NAMES}
    for l in reversed(range(N_LAYERS)):
        sv = saved[l]
        ssm = sv["ssm"]
        dh1, dmix, hn2b, dgpb, dppb, dh1b, dg2 = _out_ple_bwd(dh, sv["h1"], p[l], sv["g2"], wg[l]["w_out"],
                                                              wg[l]["w_ple_gate"], wg[l]["w_ple_proj"])
        gsm["ple_norm_g"][l] = dg2.sum(0)
        gbig[l]["w_ple_proj"] = _tn_matmul(p[l], dppb, N_CHIPS, False, "dw_ple_proj")
        gbig[l]["w_ple_gate"] = _tn_matmul(hn2b, dgpb, N_CHIPS, True, "dw_ple_gate")
        dwo = _tn_matmul(sv["ys"], dh1b, 2, True, "dw_out_ssm", None, 0, N_CHIPS)
        gbig[l]["w_out"] = _tn_matmul(sv["ya"], dh1b, 2, True, "dw_out_attn", dwo, 2, N_CHIPS)
        if l == 0 and layer1_hook is not None:
            chip1 = layer1_hook(gbig[1])
            dqs, dkn, dv, dga, got = _attn_bwd(sv["qkv"], sv["o"], sv["proj"], dmix, chip1)
            scattered = (chip1, got)
        else:
            dqs, dkn, dv, dga, _ = _attn_bwd(sv["qkv"], sv["o"], sv["proj"], dmix)
        dyf, du_a, dgs, zb, dzzb, dd, dbg = _ssm_glu_bwd(dmix, sv["y"], sv["proj"], sv["dsk"], wg[l]["ssm_w_glu"], sv["bgl"])
        gsm["ssm_d"][l] = dd.sum(0).reshape(SSM_GROUPS, SSM_GROUP)
        gsm["ssm_b_glu"][l] = dbg.sum(0)
        gbig[l]["ssm_w_glu"] = _tn_matmul(zb, dzzb, N_CHIPS, False, "dw_glu")
        du_b, dwc, dwb, da = _ssm_scan_bwd(dyf, sv["xs"], sv["proj"], ssm["wct"], ssm["tab_rev"], ssm["wbt"])
        d_are, d_aim, d_ldt, d_bre, d_bim, d_cre, d_cim = _disc_bwd(*ssm["disc_in"], da, dwb, dwc)
        by_group = lambda t: t.reshape(SSM_GROUP, SSM_GROUPS, SSM_STATE).transpose(1, 0, 2)
        gsm["ssm_c_re"][l] = by_group(d_cre)
        gsm["ssm_c_im"][l] = by_group(d_cim)
        gsm["ssm_a_re"][l] = d_are.reshape(SSM_GROUPS, SSM_STATE)
        gsm["ssm_a_im"][l] = d_aim.reshape(SSM_GROUPS, SSM_STATE)
        gsm["ssm_log_dt"][l] = d_ldt.reshape(SSM_GROUPS, SSM_STATE).sum(1)
        gsm["ssm_b_re"][l] = d_bre.T.reshape(SSM_GROUPS, SSM_STATE, SSM_GROUP)
        gsm["ssm_b_im"][l] = d_bim.T.reshape(SSM_GROUPS, SSM_STATE, SSM_GROUP)
        dh, hnb, dprojb, dg1, dqg, dkg = _in_proj_bwd(sv["h"], sv["g1"], wg[l]["w_in"], sv["qg"], sv["kg"], sv["proj"],
                                                      du_a, du_b, dgs, dqs, dkn, dv, dga, dh1)
        gsm["mix_norm_g"][l] = dg1.sum(0)
        gsm["q_norm_g"][l] = dqg.sum(0).reshape(-1, HEAD_DIM).sum(0)
        gsm["k_norm_g"][l] = dkg.sum(0).reshape(-1, HEAD_DIM).sum(0)
        gbig[l]["w_in"] = _tn_matmul(hnb, dprojb, N_CHIPS, False, "dw_in")
    gsm = {n: jnp.stack(v, 0) for n, v in gsm.items()}
    return loss, dh, gbig, gsm, scattered


_SMALL_PAD = 8 * 8 * 128


def _pack_small(d, extra):
    flat = jnp.concatenate([d[n].reshape(-1) for n in SMALL_NAMES] + [jnp.stack(extra)])
    n = flat.shape[0]
    padded = -(-n // _SMALL_PAD) * _SMALL_PAD
    return jnp.pad(flat, (0, padded - n))


def _unpack_small(flat, like):
    out, off = {}, 0
    for n in SMALL_NAMES:
        size = like[n].size
        out[n] = flat[off:off + size].reshape(like[n].shape)
        off += size
    return out, flat[off:]


def _half_views(arrs):
    return [a.reshape(a.shape[0], 2, a.shape[1] // 2, a.shape[2]) for a in arrs]


def _chip_sums(views, out_dtypes, tag):
    recv = _sibling_push(views, "grad_push_" + tag)
    return [_add_my_half(v, r, dt, "grad_half_add") for v, r, dt in zip(views, recv, out_dtypes)]


def kernel(x, p, mix_norm_g, w_in, ssm_a_re, ssm_a_im, ssm_log_dt, ssm_b_re, ssm_b_im, ssm_c_re, ssm_c_im, ssm_d, ssm_w_glu, ssm_b_glu, q_norm_g, k_norm_g, w_out, ple_norm_g, w_ple_gate, w_ple_proj, loss_target, m_mix_norm_g, m_w_in, m_ssm_a_re, m_ssm_a_im, m_ssm_log_dt, m_ssm_b_re, m_ssm_b_im, m_ssm_c_re, m_ssm_c_im, m_ssm_d, m_ssm_w_glu, m_ssm_b_glu, m_q_norm_g, m_k_norm_g, m_w_out, m_ple_norm_g, m_w_ple_gate, m_w_ple_proj, v_mix_norm_g, v_w_in, v_ssm_a_re, v_ssm_a_im, v_ssm_log_dt, v_ssm_b_re, v_ssm_b_im, v_ssm_c_re, v_ssm_c_im, v_ssm_d, v_ssm_w_glu, v_ssm_b_glu, v_q_norm_g, v_k_norm_g, v_w_out, v_ple_norm_g, v_w_ple_gate, v_w_ple_proj):
    args = dict(locals())
    names = ("mix_norm_g", "w_in", "ssm_a_re", "ssm_a_im", "ssm_log_dt", "ssm_b_re", "ssm_b_im", "ssm_c_re", "ssm_c_im",
             "ssm_d", "ssm_w_glu", "ssm_b_glu", "q_norm_g", "k_norm_g", "w_out", "ple_norm_g", "w_ple_gate", "w_ple_proj")
    w = {n: args[n] for n in names}
    m = {n: args["m_" + n] for n in names}
    v = {n: args["v_" + n] for n in names}

    local = {n: w[n].astype(BF16).reshape(2 * N_LAYERS, w[n].shape[1] // 2, w[n].shape[2]) for n in BIG_NAMES}
    w_in0 = _chip_gather([local["w_in"]], "w_in_gather")[0].reshape(N_CHIPS, D_MODEL, IN_SHARD)
    sm = {n: w[n] for n in SMALL_NAMES}
    nb = len(BIG_NAMES)
    loss, dx, gbig, gsm, (chip1, got1) = _local_step(
        x[0], p[:, 0], loss_target[0], sm, w_in0, local,
        layer1_hook=lambda g1: _chip_sums(_half_views([g1[n] for n in BIG_NAMES]), [BF16] * nb, "layer1"))

    small = _pack_small(gsm, [loss]).reshape(N_CHIPS, 2, SUBLANES, -1)
    chip0 = _chip_sums(_half_views([gbig[0][n] for n in BIG_NAMES]) + [small], [BF16] * nb + [F32], "layer0")
    got0 = _chip_scatter(chip0, "grad_chip_scatter")
    tot1 = [_sum4(a, own, "grad_chip_sum") for a, own in zip(got1, chip1)]
    tot0 = [_sum4(a, own, "grad_chip_sum") for a, own in zip(got0, chip0)]
    pieces = [(t, k, (l,)) for l, tots in enumerate((tot0[:nb], tot1)) for k, t in enumerate(tots)] + [(tot0[nb], nb, ())]
    joined = _sibling_join(pieces, [(N_LAYERS, 2) + t.shape for t in tot1] + [(2,) + tot0[nb].shape], "grad_sibling_join")
    small_all = _chip_gather([joined[nb]], "small_grad_gather")[0]
    small_tot = small_all.reshape(-1)
    g = {n: j.reshape(w[n].shape) for n, j in zip(BIG_NAMES, joined)}
    g_small, rest = _unpack_small(small_tot, sm)
    g.update(g_small)
    loss = rest[0]

    delta, new_m, new_v = {}, {}, {}
    for n in BIG_NAMES:
        lanes = w[n].shape[-1]
        outs = _adamw(_as_rows(w[n], lanes), _as_rows(g[n], lanes), _as_rows(m[n], lanes), _as_rows(v[n], lanes), "adamw_" + n)
        delta[n], new_m[n], new_v[n] = [o.reshape(w[n].shape) for o in outs]
    swap = lambda n, a: jnp.swapaxes(a, -1, -2) if n in ("ssm_b_re", "ssm_b_im") else a
    for group, per_layer in ((SMALL_4D, True), (tuple(n for n in SMALL_NAMES if n not in SMALL_4D), False)):
        outs = _adamw_many(*[[swap(n, d[n]) for n in group] for d in (w, g, m, v)],
                           "adamw_small_4d" if per_layer else "adamw_small", per_layer)
        for d, o in zip((delta, new_m, new_v), outs):
            d.update({n: swap(n, a) for n, a in zip(group, o)})

    return (loss, dx[None], *[g[n] for n in names], *[delta[n] for n in names],
            *[new_m[n] for n in names], *[new_v[n] for n in names])
```

```python
import functools
import math

import jax
import jax.numpy as jnp
from jax import lax
from jax.experimental import pallas as pl
from jax.experimental.pallas import tpu as pltpu

F32 = jnp.float32
BF16 = jnp.bfloat16

D_MODEL = 1024
N_LAYERS = 2
N_CHIPS = 4
IN_COLS = 3072
IN_SHARD = IN_COLS // N_CHIPS
SSM_WIDTH = 512
SSM_GROUP = 16
SSM_GROUPS = 32
SSM_STATE = 64
N_STATES = SSM_GROUPS * SSM_STATE
SSM_CHUNKS = 4
CH_W = SSM_WIDTH // SSM_CHUNKS
CH_S = N_STATES // SSM_CHUNKS
ATTN_WIDTH = 512
HEAD_DIM = 64
PLE_DIM = 256
ROW_SHARD = 256
RMS_EPS = 1e-6
ATTN_SCALE = HEAD_DIM ** -0.5
ATTN_BLOCK = 128
EXP_ZERO = -87.5
SUBLANES = 8
SCAN_TILE = 1024
V7X_VMEM_LIMIT = 60 * 1024 * 1024

ADAM_LR = 0.001
ADAM_B1 = 0.9
ADAM_B2 = 0.999
ADAM_EPS = 1e-08
ADAM_WD = 0.01
ADAM_STEP = 10

MESH = pl.DeviceIdType.MESH
ANY = pl.BlockSpec(memory_space=pl.ANY)


def _cparams(n_grid=0, parallel=0):
    sem = tuple(["parallel"] * parallel + ["arbitrary"] * (n_grid - parallel))
    return pltpu.CompilerParams(dimension_semantics=sem, vmem_limit_bytes=V7X_VMEM_LIMIT)


def _dot(a, b):
    return jnp.dot(a, b, preferred_element_type=F32)


def _dot_nt(a, b):
    return lax.dot_general(a, b, (((1,), (1,)), ((), ())), preferred_element_type=F32)


def _dot_tn(a, b):
    return lax.dot_general(a, b, (((0,), (0,)), ((), ())), preferred_element_type=F32)


def _split_hilo(a):
    hi = a.astype(BF16)
    lo = (a - hi.astype(F32)).astype(BF16)
    return hi, lo


def _dot_hilo(a, b):
    hi, lo = _split_hilo(a)
    return _dot(hi, b) + _dot(lo, b)


def _sigmoid(x):
    return 0.5 * (jnp.tanh(0.5 * x) + 1.0)


_GELU_C = math.sqrt(2.0 / math.pi)


def _gelu(x):
    return 0.5 * x * (1.0 + jnp.tanh(_GELU_C * (x + 0.044715 * (x * x * x))))


def _gelu_grad(x):
    t = jnp.tanh(_GELU_C * (x + 0.044715 * (x * x * x)))
    return 0.5 * (1.0 + t) + 0.5 * x * (1.0 - t * t) * (_GELU_C * (1.0 + 3.0 * 0.044715 * (x * x)))


def _row_tile(s, want):
    for t in range(min(s, want), 7, -1):
        if s % t == 0 and t % SUBLANES == 0:
            return t
    return s


def _coords():
    return lax.axis_index("x"), lax.axis_index("y"), lax.axis_index("c")


def _other_chips(x, y):
    return [(1 - x, y), (x, 1 - y), (1 - x, 1 - y)]


def _remote(src, dst, send_sem, recv_sem, dev):
    return pltpu.make_async_remote_copy(src_ref=src, dst_ref=dst, send_sem=send_sem, recv_sem=recv_sem,
                                        device_id=dev, device_id_type=MESH)


def _set_block(buf, block, index):
    return lax.dynamic_update_index_in_dim(buf, block, index, 0)


def _gather_sems(n):
    return [pltpu.SemaphoreType.DMA((3 * n,)) for _ in range(4)]


def _gather_copies(ins, bases, outs, sems):
    send_sems, recv_sems, fwd_send, fwd_recv = sems
    x, y, c = _coords()
    me_chip = 2 * x + y
    sibling = (x, y, 1 - c)
    first, landed, passed, from_sibling = [], [], [], []
    for k in range(len(ins)):
        for j, (cx, cy) in enumerate(_other_chips(x, y)):
            i = 3 * k + j
            first.append(_remote(ins[k].at[bases[k] + c], outs[k].at[me_chip, c], send_sems.at[i], recv_sems.at[i], (cx, cy, c)))
            blk = outs[k].at[2 * cx + cy, c]
            landed.append(_remote(blk, blk, send_sems.at[i], recv_sems.at[i], (cx, cy, c)))
            passed.append(_remote(blk, blk, fwd_send.at[i], fwd_recv.at[i], sibling))
            blk = outs[k].at[2 * cx + cy, 1 - c]
            from_sibling.append(_remote(blk, blk, fwd_send.at[i], fwd_recv.at[i], sibling))
    return first, landed, passed, from_sibling


def _gather_start(ins, bases, outs, sems):
    for cp in _gather_copies(ins, bases, outs, sems)[0]:
        cp.start()


def _gather_finish(ins, bases, outs, sems):
    first, landed, passed, from_sibling = _gather_copies(ins, bases, outs, sems)
    for arrived, forward in zip(landed, passed):
        arrived.wait_recv()
        forward.start()
    for cp in from_sibling:
        cp.wait_recv()
    for cp in first + passed:
        cp.wait_send()


def _gather_outputs(arrs):
    return [jax.ShapeDtypeStruct((N_CHIPS, 2) + a.shape[1:], a.dtype) for a in arrs]


def _gather_own(outs, arrs, bases):
    me_chip = 2 * lax.axis_index("x") + lax.axis_index("y")
    return [_set_block(o, lax.slice_in_dim(a, b, b + 2, axis=0), me_chip) for o, a, b in zip(outs, arrs, bases)]


def _chip_gather(arrs, name, bases=None):
    n = len(arrs)
    bases = [0] * n if bases is None else bases

    def body(*refs):
        ins, outs, sems = refs[:n], refs[n:2 * n], refs[2 * n:]
        _gather_start(ins, bases, outs, sems)
        _gather_finish(ins, bases, outs, sems)

    outs = pl.pallas_call(
        body, name=name, out_shape=_gather_outputs(arrs),
        in_specs=[ANY] * n, out_specs=[ANY] * n, scratch_shapes=_gather_sems(n),
    )(*arrs)
    return _gather_own(outs, arrs, bases)


def _sibling_push(arrs, name):
    n = len(arrs)

    def body(*refs):
        ins, outs = refs[:n], refs[n:2 * n]
        send_sems, recv_sems = refs[2 * n:]
        x, y, c = _coords()
        cps = [_remote(ins[k].at[pl.ds(0, N_CHIPS), 1 - c], outs[k], send_sems.at[k], recv_sems.at[k], (x, y, 1 - c))
               for k in range(n)]
        for cp in cps:
            cp.start()
        for cp in cps:
            cp.wait_recv()
        for cp in cps:
            cp.wait_send()

    return pl.pallas_call(
        body, name=name,
        out_shape=[jax.ShapeDtypeStruct((a.shape[0],) + a.shape[2:], a.dtype) for a in arrs],
        in_specs=[ANY] * n, out_specs=[ANY] * n,
        scratch_shapes=[pltpu.SemaphoreType.DMA((n,)), pltpu.SemaphoreType.DMA((n,))],
    )(*arrs)


def _sibling_join(pieces, out_shapes, name):
    n = len(pieces)
    no = len(out_shapes)

    def body(*refs):
        ins, outs = refs[:n], refs[n:n + no]
        send_sems, recv_sems = refs[n + no:]
        x, y, c = _coords()
        sibling = (x, y, 1 - c)
        cps = [_remote(ins[k], outs[o].at[lead + (c,)], send_sems.at[k], recv_sems.at[k], sibling)
               for k, (_, o, lead) in enumerate(pieces)]
        for cp in cps:
            cp.start()
        for k, (_, o, lead) in enumerate(pieces):
            blk = outs[o].at[lead + (1 - c,)]
            _remote(blk, blk, send_sems.at[k], recv_sems.at[k], sibling).wait_recv()
        for cp in cps:
            cp.wait_send()

    outs = pl.pallas_call(
        body, name=name,
        out_shape=[jax.ShapeDtypeStruct(sh, F32) for sh in out_shapes],
        in_specs=[ANY] * n, out_specs=[ANY] * no,
        scratch_shapes=[pltpu.SemaphoreType.DMA((n,)), pltpu.SemaphoreType.DMA((n,))],
    )(*[a for a, _, _ in pieces])
    outs = list(outs)
    c = lax.axis_index("c")
    for a, o, lead in pieces:
        block = a.reshape((1,) * (len(lead) + 1) + a.shape)
        outs[o] = lax.dynamic_update_slice(outs[o], block, lead + (c,) + (0,) * a.ndim)
    return outs


def _scatter_sems(n):
    return [pltpu.SemaphoreType.DMA((3 * n,)), pltpu.SemaphoreType.DMA((3 * n,))]


def _scatter_copies(ins, outs, sems):
    send_sems, recv_sems = sems
    x, y, c = _coords()
    me_chip = 2 * x + y
    sends, arrivals = [], []
    for k in range(len(ins)):
        for j, (cx, cy) in enumerate(_other_chips(x, y)):
            i = 3 * k + j
            sends.append(_remote(ins[k].at[2 * cx + cy], outs[k].at[me_chip], send_sems.at[i], recv_sems.at[i], (cx, cy, c)))
            blk = outs[k].at[2 * cx + cy]
            arrivals.append(_remote(blk, blk, send_sems.at[i], recv_sems.at[i], (cx, cy, c)))
    return sends, arrivals


def _scatter_start(ins, outs, sems):
    for cp in _scatter_copies(ins, outs, sems)[0]:
        cp.start()


def _scatter_finish(ins, outs, sems):
    sends, arrivals = _scatter_copies(ins, outs, sems)
    for cp in arrivals:
        cp.wait_recv()
    for cp in sends:
        cp.wait_send()


def _chip_scatter(arrs, name):
    n = len(arrs)

    def body(*refs):
        ins, outs, sems = refs[:n], refs[n:2 * n], refs[2 * n:]
        _scatter_start(ins, outs, sems)
        _scatter_finish(ins, outs, sems)

    outs = pl.pallas_call(
        body, name=name,
        out_shape=[jax.ShapeDtypeStruct(a.shape, a.dtype) for a in arrs],
        in_specs=[ANY] * n, out_specs=[ANY] * n, scratch_shapes=_scatter_sems(n),
    )(*arrs)
    return outs


def _as_rows(a, lanes):
    return a.reshape(-1, lanes)


def _add_my_half(v, recv, out_dtype, name):
    n_sh, _, h, cdim = v.shape
    tr = _row_tile(h, 512)

    def body(c_ref, a_ref, b_ref, o_ref):
        o_ref[...] = (a_ref[...].astype(F32) + b_ref[...].astype(F32)).astype(out_dtype)

    c = lax.axis_index("c").astype(jnp.int32).reshape(1)
    return pl.pallas_call(
        body, name=name,
        grid_spec=pltpu.PrefetchScalarGridSpec(
            num_scalar_prefetch=1, grid=(n_sh, h // tr),
            in_specs=[pl.BlockSpec((None, None, tr, cdim), lambda sh, i, c_ref: (sh, c_ref[0], i, 0)),
                      pl.BlockSpec((None, tr, cdim), lambda sh, i, c_ref: (sh, i, 0))],
            out_specs=pl.BlockSpec((None, tr, cdim), lambda sh, i, c_ref: (sh, i, 0))),
        out_shape=jax.ShapeDtypeStruct((n_sh, h, cdim), out_dtype),
        compiler_params=_cparams(2),
    )(c, v, recv)


def _sum4(got, own, name):
    _, r, cdim = got.shape
    tr = _row_tile(r, 512)

    def body(me_ref, p_ref, own_ref, o_ref):
        mine = own_ref[...].astype(F32)
        acc = None
        for j in range(N_CHIPS):
            term = jnp.where(me_ref[0] == j, mine, p_ref[j].astype(F32))
            acc = term if acc is None else acc + term
        o_ref[...] = acc

    me = (2 * lax.axis_index("x") + lax.axis_index("y")).astype(jnp.int32).reshape(1)
    return pl.pallas_call(
        body, name=name,
        grid_spec=pltpu.PrefetchScalarGridSpec(
            num_scalar_prefetch=1, grid=(r // tr,),
            in_specs=[pl.BlockSpec((N_CHIPS, tr, cdim), lambda i, me_ref: (0, i, 0)),
                      pl.BlockSpec((None, tr, cdim), lambda i, me_ref: (me_ref[0], i, 0))],
            out_specs=pl.BlockSpec((tr, cdim), lambda i, me_ref: (i, 0))),
        out_shape=jax.ShapeDtypeStruct((r, cdim), F32),
        compiler_params=_cparams(1),
    )(me, got, own)


def _adamw_math(w, g, m, v):
    c1 = 1.0 - ADAM_B1 ** ADAM_STEP
    c2 = 1.0 - ADAM_B2 ** ADAM_STEP
    nm = ADAM_B1 * m + (1.0 - ADAM_B1) * g
    nv = ADAM_B2 * v + (1.0 - ADAM_B2) * (g * g)
    delta = -ADAM_LR * ((nm / c1) / (jnp.sqrt(nv / c2) + ADAM_EPS) + ADAM_WD * w)
    return delta, nm, nv


def _adamw(w, g, m, v, name):
    r, cdim = w.shape
    tr = _row_tile(r, 256)

    def body(w_ref, g_ref, m_ref, v_ref, d_ref, nm_ref, nv_ref):
        d_ref[...], nm_ref[...], nv_ref[...] = _adamw_math(w_ref[...], g_ref[...], m_ref[...], v_ref[...])

    spec = pl.BlockSpec((tr, cdim), lambda i: (i, 0))
    return pl.pallas_call(
        body, name=name, grid=(r // tr,),
        in_specs=[spec] * 4, out_specs=[spec] * 3,
        out_shape=[jax.ShapeDtypeStruct((r, cdim), F32)] * 3,
        compiler_params=_cparams(1),
    )(w, g, m, v)


def _adamw_many(ws, gs, ms, vs, name, per_layer):
    n = len(ws)

    def body(*refs):
        for k in range(n):
            w, g, m, v = (refs[j * n + k][...] for j in range(4))
            outs = _adamw_math(w, g, m, v)
            for j in range(3):
                refs[(4 + j) * n + k][...] = outs[j]

    shapes = [jax.ShapeDtypeStruct(w.shape, F32) for w in ws]
    if per_layer:
        specs = [pl.BlockSpec((None,) + w.shape[1:], lambda l, nd=w.ndim: (l,) + (0,) * (nd - 1)) for w in ws]
        call = pl.pallas_call(body, name=name, grid=(N_LAYERS,), in_specs=specs * 4, out_specs=specs * 3,
                              out_shape=shapes * 3, compiler_params=_cparams(1))
    else:
        call = pl.pallas_call(body, name=name, out_shape=shapes * 3, compiler_params=_cparams())
    outs = call(*ws, *gs, *ms, *vs)
    return outs[0:n], outs[n:2 * n], outs[2 * n:3 * n]


def _cmul(ar, ai, br, bi):
    return ar * br - ai * bi, ar * bi + ai * br


def _discretise(a_re, a_im, log_dt, b_re, b_im):
    dt = jnp.exp(log_dt)
    mag = jnp.exp(a_re * dt)
    ab_re = mag * jnp.cos(a_im * dt)
    ab_im = mag * jnp.sin(a_im * dt)
    num_re = ab_re - 1.0
    num_im = ab_im
    den = a_re * a_re + a_im * a_im
    f_re = (num_re * a_re + num_im * a_im) / den
    f_im = (num_im * a_re - num_re * a_im) / den
    bb_re = f_re * b_re - f_im * b_im
    bb_im = f_re * b_im + f_im * b_re
    return ab_re, ab_im, bb_re, bb_im


def _disc_shapes():
    col = jax.ShapeDtypeStruct((1, N_STATES), F32)
    mat = jax.ShapeDtypeStruct((SSM_GROUP, N_STATES), F32)
    return col, mat


def _group_mask():
    row = lax.broadcasted_iota(jnp.int32, (CH_W, CH_S), 0)
    col = lax.broadcasted_iota(jnp.int32, (CH_W, CH_S), 1)
    return jnp.right_shift(row, SSM_GROUP.bit_length() - 1) == jnp.right_shift(col, SSM_STATE.bit_length() - 1)


def _block_diag(v, j):
    blk = v[:, CH_S * j:CH_S * (j + 1)]
    return jnp.where(_group_mask(), jnp.concatenate([blk] * (CH_W // SSM_GROUP), axis=0), 0.0)


def _block_diag_t(m):
    kept = jnp.where(_group_mask(), m, 0.0)
    return kept.reshape(CH_W // SSM_GROUP, SSM_GROUP, CH_S).sum(axis=0)


def _disc_fwd(a_re, a_im, log_dt, b_re, b_im, c_re, c_im, length):
    wide = jax.ShapeDtypeStruct((SSM_CHUNKS, CH_W, 2 * CH_S), BF16)
    tall = jax.ShapeDtypeStruct((SSM_CHUNKS, 2 * CH_S, CH_W), BF16)
    tab = jax.ShapeDtypeStruct((SSM_CHUNKS, length, 2 * CH_S), F32)

    def body(ar, ai, ld, br, bi, cr, ci, wb_ref, wbt_ref, wct_ref, wc_ref, tab_ref, rev_ref):
        ab_re, ab_im, bb_re, bb_im = _discretise(ar[...], ai[...], ld[...], br[...], bi[...])
        ccr, cci = cr[...], -ci[...]
        for j in range(SSM_CHUNKS):
            for lo, (vb, vc) in ((0, (bb_re, ccr)), (CH_S, (bb_im, cci))):
                mb, mc = _block_diag(vb, j), _block_diag(vc, j)
                wb_ref[j, :, lo:lo + CH_S] = mb.astype(BF16)
                wbt_ref[j, lo:lo + CH_S, :] = mb.T.astype(BF16)
                wct_ref[j, :, lo:lo + CH_S] = mc.astype(BF16)
                wc_ref[j, lo:lo + CH_S, :] = mc.T.astype(BF16)

        def step(j, carry):
            pr, pi = carry
            back = length - 1 - j
            for c in range(SSM_CHUNKS):
                lanes = slice(CH_S * c, CH_S * (c + 1))
                tab_ref[c, pl.ds(j, 1), 0:CH_S] = pr[:, lanes]
                tab_ref[c, pl.ds(j, 1), CH_S:2 * CH_S] = pi[:, lanes]
                rev_ref[c, pl.ds(back, 1), 0:CH_S] = pr[:, lanes]
                rev_ref[c, pl.ds(back, 1), CH_S:2 * CH_S] = -pi[:, lanes]
            return _cmul(pr, pi, ab_re, ab_im)

        lax.fori_loop(0, length, step, (ab_re, ab_im))

    return pl.pallas_call(body, name="ssm_discretise", out_shape=[wide, tall, wide, tall, tab, tab],
                          compiler_params=_cparams())(a_re, a_im, log_dt, b_re, b_im, c_re, c_im)


def _disc_bwd(a_re, a_im, log_dt, b_re, b_im, da, dwb, dwc):
    col, mat = _disc_shapes()

    def body(ar, ai, ld, br, bi, da_ref, dwb_ref, dwc_ref, o0, o1, o2, o3, o4, dcr_ref, dci_ref):
        g_ab = [jnp.concatenate([jnp.sum(da_ref[j, :, lo:lo + CH_S], axis=0, keepdims=True) for j in range(SSM_CHUNKS)],
                                axis=-1) for lo in (0, CH_S)]
        g_bb = [jnp.concatenate([_block_diag_t(dwb_ref[j, :, lo:lo + CH_S]) for j in range(SSM_CHUNKS)], axis=-1)
                for lo in (0, CH_S)]
        for ref, lo, sign in ((dcr_ref, 0, 1.0), (dci_ref, CH_S, -1.0)):
            ref[...] = sign * jnp.concatenate([_block_diag_t(dwc_ref[j, lo:lo + CH_S, :].T) for j in range(SSM_CHUNKS)],
                                              axis=-1)
        _, vjp = jax.vjp(_discretise, ar[...], ai[...], ld[...], br[...], bi[...])
        grads = vjp((g_ab[0], g_ab[1], g_bb[0], g_bb[1]))
        for o, val in zip((o0, o1, o2, o3, o4), grads):
            o[...] = val

    return pl.pallas_call(body, name="ssm_discretise_bwd", out_shape=[col, col, col, mat, mat, mat, mat],
                          compiler_params=_cparams())(a_re, a_im, log_dt, b_re, b_im, da, dwb, dwc)


def _interleave_chunks(v):
    rows, width = v.shape
    return pltpu.einshape("cjw->jcw", v.reshape(SUBLANES, rows // SUBLANES, width)).reshape(rows, width)


def _time_order(v):
    rows, width = v.shape
    return pltpu.einshape("jcw->cjw", v.reshape(rows // SUBLANES, SUBLANES, width)).reshape(rows, width)


def _head_ones():
    r = jnp.arange(ATTN_WIDTH) // HEAD_DIM
    return jnp.where(r[:, None] == r[None, :], 1.0 / HEAD_DIM, 0.0).astype(BF16)


def _in_proj(h, g1, w_in_l, qg, kg):
    s = h.shape[0]
    tm = _row_tile(s, 512)

    def body(h_ref, g_ref, w_ref, qg_ref, kg_ref, ones_ref, proj_ref, qkv_ref):
        x = h_ref[...]
        r = lax.rsqrt(jnp.mean(x * x, axis=-1, keepdims=True) + RMS_EPS)
        hn = (x * r * g_ref[...]).astype(BF16)
        for sh in range(N_CHIPS):
            proj_ref[:, IN_SHARD * sh:IN_SHARD * (sh + 1)] = _dot(hn, w_ref[sh])
        ones = ones_ref[...]
        q = proj_ref[:, 1024:1536]
        k = proj_ref[:, 1536:2048]
        rq = lax.rsqrt(_dot_hilo(q * q, ones) + RMS_EPS)
        rk = lax.rsqrt(_dot_hilo(k * k, ones) + RMS_EPS)
        qkv_ref[:, 0:512] = (q * rq * qg_ref[...] * ATTN_SCALE).astype(BF16)
        qkv_ref[:, 512:1024] = (k * rk * kg_ref[...]).astype(BF16)
        qkv_ref[:, 1024:1536] = proj_ref[:, 2048:2560].astype(BF16)

    full = lambda shape: pl.BlockSpec(shape, lambda i: (0,) * len(shape))
    return pl.pallas_call(
        body, name="in_proj", grid=(s // tm,),
        in_specs=[pl.BlockSpec((tm, D_MODEL), lambda i: (i, 0)), full((1, D_MODEL)),
                  full((N_CHIPS, D_MODEL, IN_SHARD)),
                  full((1, ATTN_WIDTH)), full((1, ATTN_WIDTH)), full((ATTN_WIDTH, ATTN_WIDTH))],
        out_specs=[pl.BlockSpec((tm, IN_COLS), lambda i: (i, 0)), pl.BlockSpec((tm, 3 * ATTN_WIDTH), lambda i: (i, 0))],
        out_shape=[jax.ShapeDtypeStruct((s, IN_COLS), F32), jax.ShapeDtypeStruct((s, 3 * ATTN_WIDTH), BF16)],
        compiler_params=_cparams(1),
    )(h, g1, w_in_l, qg, kg, _head_ones())


def _row_bcast(ref, k, lo):
    return jnp.broadcast_to(ref[pl.ds(k, 1), lo:lo + CH_S], (SUBLANES, CH_S))


def _chunk_scan(x_ref, tab_ref, carry_ref, length, reverse, tail=None):
    row = lax.broadcasted_iota(jnp.int32, (SUBLANES, CH_S), 0)
    one, full = (length - 1, 0) if reverse else (0, length - 1)
    ar, ai = _row_bcast(tab_ref, one, 0), _row_bcast(tab_ref, one, CH_S)
    fr, fi = _row_bcast(tab_ref, full, 0), _row_bcast(tab_ref, full, CH_S)
    step = lambda jj: (length - 1 - jj) if reverse else jj

    def local(jj, carry):
        cr, ci = carry
        r0 = pl.multiple_of(step(jj) * SUBLANES, SUBLANES)
        xr = x_ref[pl.ds(r0, SUBLANES), 0:CH_S] + (ar * cr - ai * ci)
        xi = x_ref[pl.ds(r0, SUBLANES), CH_S:2 * CH_S] + (ar * ci + ai * cr)
        x_ref[pl.ds(r0, SUBLANES), 0:CH_S] = xr
        x_ref[pl.ds(r0, SUBLANES), CH_S:2 * CH_S] = xi
        return xr, xi

    zero = jnp.zeros((SUBLANES, CH_S), F32)
    er, ei = lax.fori_loop(0, length, local, (zero, zero))

    first, shift = (SUBLANES - 1, SUBLANES - 1) if reverse else (0, 1)
    hr = jnp.where(row == first, carry_ref[:, 0:CH_S], 0.0)
    hi = jnp.where(row == first, carry_ref[:, CH_S:2 * CH_S], 0.0)
    sr, si = pltpu.roll(er, shift, 0), pltpu.roll(ei, shift, 0)
    for k in range(1, SUBLANES):
        tr, ti = pltpu.roll(hr, shift, 0), pltpu.roll(hi, shift, 0)
        here = row == ((SUBLANES - 1 - k) if reverse else k)
        hr, hi = (jnp.where(here, fr * tr - fi * ti + sr, hr), jnp.where(here, fr * ti + fi * tr + si, hi))
    last = 0 if reverse else SUBLANES - 1
    outr, outi = fr * hr - fi * hi + er, fr * hi + fi * hr + ei
    carry_ref[:, 0:CH_S] = jnp.broadcast_to(outr[last:last + 1, :], (SUBLANES, CH_S))
    carry_ref[:, CH_S:2 * CH_S] = jnp.broadcast_to(outi[last:last + 1, :], (SUBLANES, CH_S))

    def fix(jj, carry):
        j = step(jj)
        r0 = pl.multiple_of(j * SUBLANES, SUBLANES)
        pr, pi = _row_bcast(tab_ref, j, 0), _row_bcast(tab_ref, j, CH_S)
        xr = x_ref[pl.ds(r0, SUBLANES), 0:CH_S] + (pr * hr - pi * hi)
        xi = x_ref[pl.ds(r0, SUBLANES), CH_S:2 * CH_S] + (pr * hi + pi * hr)
        x_ref[pl.ds(r0, SUBLANES), 0:CH_S] = xr
        x_ref[pl.ds(r0, SUBLANES), CH_S:2 * CH_S] = xi
        if tail is None:
            return carry
        return tail(r0, xr, xi, carry)

    return fix, (hr, hi)


def _ssm_scan_fwd(proj, wb, tab, wc, gather=None, gather_bases=None):
    s = proj.shape[0]
    tm = _row_tile(s, SCAN_TILE)
    nt = s // tm
    length = tm // SUBLANES
    gather = [] if gather is None else gather
    ng = len(gather)

    def body(*refs):
        u_ref, wb_ref, tab_ref, wc_ref = refs[0:4]
        g_ins = refs[4:4 + ng]
        xs_ref, y_ref = refs[4 + ng:6 + ng]
        g_outs = refs[6 + ng:6 + 2 * ng]
        carry_ref = refs[6 + 2 * ng]
        sems = refs[7 + 2 * ng:]
        j, i = pl.program_id(0), pl.program_id(1)

        @pl.when(i == 0)
        def _():
            carry_ref[...] = jnp.zeros_like(carry_ref)

        if ng:
            @pl.when(jnp.logical_and(j == 0, i == 0))
            def _():
                _gather_start(g_ins, gather_bases, g_outs, sems)

        xs_ref[...] = _dot(_interleave_chunks(u_ref[...]).astype(BF16), wb_ref[...])
        fix, start = _chunk_scan(xs_ref, tab_ref, carry_ref, length, reverse=False)
        lax.fori_loop(0, length, fix, start, unroll=2)
        y_ref[...] = _time_order(_dot(xs_ref[...].astype(BF16), wc_ref[...]))

        if ng:
            @pl.when(jnp.logical_and(j == SSM_CHUNKS - 1, i == nt - 1))
            def _():
                _gather_finish(g_ins, gather_bases, g_outs, sems)

    outs = pl.pallas_call(
        body, name="ssm_scan_gather" if ng else "ssm_scan", grid=(SSM_CHUNKS, nt),
        in_specs=[pl.BlockSpec((tm, CH_W), lambda j, i: (i, j)),
                  pl.BlockSpec((None, CH_W, 2 * CH_S), lambda j, i: (j, 0, 0)),
                  pl.BlockSpec((None, length, 2 * CH_S), lambda j, i: (j, 0, 0)),
                  pl.BlockSpec((None, 2 * CH_S, CH_W), lambda j, i: (j, 0, 0))] + [ANY] * ng,
        out_specs=[pl.BlockSpec((None, tm, 2 * CH_S), lambda j, i: (j, i, 0)),
                   pl.BlockSpec((tm, CH_W), lambda j, i: (i, j))] + [ANY] * ng,
        out_shape=[jax.ShapeDtypeStruct((SSM_CHUNKS, s, 2 * CH_S), F32), jax.ShapeDtypeStruct((s, SSM_WIDTH), F32)]
        + _gather_outputs(gather),
        scratch_shapes=[pltpu.VMEM((SUBLANES, 2 * CH_S), F32)] + (_gather_sems(ng) if ng else []),
        compiler_params=_cparams(2),
    )(proj, wb, tab, wc, *gather)
    return outs[0], outs[1], (_gather_own(outs[2:], gather, gather_bases) if ng else [])


def _glu_forward(y, u, d, wg_ref, bg):
    yf = y + d * u
    z = _gelu(yf)
    zb = z.astype(BF16)
    zz = jnp.concatenate([_dot(zb, wg_ref[sh]) for sh in range(N_CHIPS)], axis=-1) + bg
    return yf, z, zz[:, 0:SSM_WIDTH], zz[:, SSM_WIDTH:2 * SSM_WIDTH]


def _ssm_glu_fwd(y, proj, d, w_glu_l, b_glu):
    s = y.shape[0]
    tm = _row_tile(s, 1024)

    def body(y_ref, u_ref, gs_ref, d_ref, wg_ref, bg_ref, o_ref):
        _, _, val, gate = _glu_forward(y_ref[...], u_ref[...], d_ref[...], wg_ref, bg_ref[...])
        gs = gs_ref[...]
        o_ref[...] = val * _sigmoid(gate) * (gs * _sigmoid(gs))

    row = lambda i: (i, 0)
    return pl.pallas_call(
        body, name="ssm_glu", grid=(s // tm,),
        in_specs=[pl.BlockSpec((tm, SSM_WIDTH), row), pl.BlockSpec((tm, SSM_WIDTH), row),
                  pl.BlockSpec((tm, SSM_WIDTH), lambda i: (i, 1)), pl.BlockSpec((1, SSM_WIDTH), lambda i: (0, 0)),
                  pl.BlockSpec((N_CHIPS, SSM_WIDTH, ROW_SHARD), lambda i: (0, 0, 0)),
                  pl.BlockSpec((1, 2 * SSM_WIDTH), lambda i: (0, 0))],
        out_specs=pl.BlockSpec((tm, SSM_WIDTH), row),
        out_shape=jax.ShapeDtypeStruct((s, SSM_WIDTH), F32),
        compiler_params=_cparams(1),
    )(y, proj, proj, d, w_glu_l, b_glu)


def _tri(kind):
    r = jnp.arange(ATTN_BLOCK)
    if kind == "suffix_incl":
        m = r[:, None] >= r[None, :]
    else:
        m = r[:, None] < r[None, :]
    return jnp.concatenate([m, jnp.ones_like(m)], axis=1).astype(BF16)


def _head_masks():
    lane = lax.broadcasted_iota(jnp.int32, (1, 2 * HEAD_DIM), 1)
    return [lane < HEAD_DIM, lane >= HEAD_DIM]


def _chain_step(t, base, n_sub, first, q_ref, k_ref, tri_ref, l_scr, per_chain):
    tb = ATTN_BLOCK
    row = lax.broadcasted_iota(jnp.int32, (tb, tb), 0)
    col = lax.broadcasted_iota(jnp.int32, (tb, tb), 1)
    masks = _head_masks()
    blks = [base + a - t for a in range(n_sub)]
    r0s = [pl.multiple_of(jnp.maximum(blk, 0) * tb, tb) for blk in blks]
    zs = []
    for a in range(n_sub):
        kb = k_ref[pl.ds(r0s[a], tb), :]
        qa = q_ref[a * tb:(a + 1) * tb, :]
        for mask in masks:
            zs.append(_dot_nt(jnp.where(mask, qa, jnp.zeros_like(qa)), kb))
    parts = []
    for z in zs:
        ls = jnp.minimum(-z, 0.0) - jnp.log(1.0 + jnp.exp(-jnp.abs(z)))
        if first:
            ls = jnp.where(col < row, ls, 0.0)
        parts.append(_split_hilo(ls))
    tri = tri_ref[...]
    sums = [_dot(hi, tri) + _dot(lo, tri) for hi, lo in parts]
    top = None
    ws = []
    for c, (z, sm) in enumerate(zip(zs, sums)):
        if first:
            lsum = jnp.zeros((tb, tb), F32)
        else:
            lsum = l_scr[c] + jnp.where(blks[c // 2] >= 0, 0.0, -1e30)
        w = jnp.exp(z + sm[:, 0:tb] + lsum)
        if first:
            w = jnp.where(col < row, w, 0.0)
        ws.append(w)
        lsum = lsum + sm[:, tb:2 * tb]
        l_scr[c] = lsum
        top = lsum if top is None else jnp.maximum(top, lsum)
    for c, (z, w) in enumerate(zip(zs, ws)):
        per_chain(c // 2, c % 2, c, r0s[c // 2], z, w)
    return jnp.max(top)


def _chain_sweep(base, n_sub, q_ref, k_ref, tri_ref, l_scr, per_chain):
    top = _chain_step(0, base, n_sub, True, q_ref, k_ref, tri_ref, l_scr, functools.partial(per_chain, 0))

    def cond(carry):
        t, top = carry
        return jnp.logical_and(t <= base + n_sub - 1, top > EXP_ZERO)

    def step(carry):
        t, _ = carry
        return t + 1, _chain_step(t, base, n_sub, False, q_ref, k_ref, tri_ref, l_scr, functools.partial(per_chain, t))

    steps, _ = lax.while_loop(cond, step, (jnp.int32(1), top))
    return steps


ATTN_SUB_FWD = 8
ATTN_SUB_BWD = 8


def _attn_fwd(qkv, proj, gather=None, gather_bases=None):
    s = qkv.shape[0]
    tb = ATTN_BLOCK
    n_sub = min(ATTN_SUB_FWD, s // tb)
    tq = n_sub * tb
    n_hp = ATTN_WIDTH // (2 * HEAD_DIM)
    gather = [] if gather is None else gather
    ng = len(gather)

    def body(*refs):
        q_ref, k_ref, v_ref, g_ref, tri_ref = refs[0:5]
        g_ins = refs[5:5 + ng]
        o_ref, ya_ref = refs[5 + ng:7 + ng]
        g_outs = refs[7 + ng:7 + 2 * ng]
        l_scr = refs[7 + 2 * ng]
        sems = refs[8 + 2 * ng:]
        i = pl.program_id(1)
        masks = _head_masks()
        o_ref[...] = jnp.zeros_like(o_ref)

        if ng:
            @pl.when(jnp.logical_and(pl.program_id(0) == 0, i == 0))
            def _():
                _gather_start(g_ins, gather_bases, g_outs, sems)

        def per_chain(t, a, h, c, r0, z, w):
            vb = v_ref[pl.ds(r0, tb), :]
            vb = jnp.where(masks[h], vb, jnp.zeros_like(vb))
            o_ref[a * tb:(a + 1) * tb, :] += _dot(w.astype(BF16), vb)

        _chain_sweep(i * n_sub, n_sub, q_ref, k_ref, tri_ref, l_scr, per_chain)
        g = g_ref[...]
        ya_ref[...] = o_ref[...] * (g * _sigmoid(g))

        if ng:
            @pl.when(jnp.logical_and(pl.program_id(0) == n_hp - 1, i == s // tq - 1))
            def _():
                _gather_finish(g_ins, gather_bases, g_outs, sems)

    hp_blk = lambda off: pl.BlockSpec((tq, 2 * HEAD_DIM), lambda hp, i: (i, off + hp))
    res = lambda off: pl.BlockSpec((s, 2 * HEAD_DIM), lambda hp, i: (0, off + hp))
    outs = pl.pallas_call(
        body, name="attn_fwd_gather" if ng else "attn_fwd", grid=(n_hp, s // tq),
        in_specs=[hp_blk(0), res(4), res(8), hp_blk(20), pl.BlockSpec((tb, 2 * tb), lambda hp, i: (0, 0))] + [ANY] * ng,
        out_specs=[hp_blk(0), hp_blk(0)] + [ANY] * ng,
        out_shape=[jax.ShapeDtypeStruct((s, ATTN_WIDTH), F32)] * 2 + _gather_outputs(gather),
        scratch_shapes=[pltpu.VMEM((2 * n_sub, tb, tb), F32)] + (_gather_sems(ng) if ng else []),
        compiler_params=_cparams(2),
    )(qkv, qkv, qkv, proj, _tri("suffix_incl"), *gather)
    return outs[0], outs[1], (_gather_own(outs[2:], gather, gather_bases) if ng else [])


def _rms_rows(x, g):
    r = lax.rsqrt(jnp.mean(x * x, axis=-1, keepdims=True) + RMS_EPS)
    return r, x * r * g


def _ple_forward(h1, p, g2, wpg_ref, wpp_ref):
    r2, hn2 = _rms_rows(h1, g2)
    hb = hn2.astype(BF16)
    gpre = _dot(hb[:, 0:ROW_SHARD], wpg_ref[0])
    for sh in range(1, N_CHIPS):
        gpre = gpre + _dot(hb[:, ROW_SHARD * sh:ROW_SHARD * (sh + 1)], wpg_ref[sh])
    gate = _sigmoid(gpre)
    pb = p.astype(BF16)
    pp = jnp.concatenate([_dot(pb, wpp_ref[sh]) for sh in range(N_CHIPS)], axis=-1)
    return r2, hb, gate, pp


def _colsum8(a):
    t = a.shape[0]
    return a.reshape(t // SUBLANES, SUBLANES, a.shape[1]).sum(axis=0)


def _sq_err_grad(y, target):
    e = y - target
    sq = _colsum8(e * e)
    part = sq[:, 0:128]
    for b in range(1, D_MODEL // 128):
        part = part + sq[:, 128 * b:128 * (b + 1)]
    return e / D_MODEL, part


def _out_ple(h, ys, ya, p, g2, w_out_l, w_pg_l, w_pp_l, target=None):
    s = h.shape[0]
    tm = _row_tile(s, 512)
    last = target is not None

    def body(*refs):
        h_ref, ys_ref, ya_ref, p_ref, g_ref, wo_ref, wpg_ref, wpp_ref = refs[0:8]
        h1_ref, h2_ref = refs[8 + last], refs[9 + last]
        ysb = ys_ref[...].astype(BF16)
        yab = ya_ref[...].astype(BF16)
        h1 = h_ref[...]
        for sh, src in enumerate((ysb[:, 0:ROW_SHARD], ysb[:, ROW_SHARD:], yab[:, 0:ROW_SHARD], yab[:, ROW_SHARD:])):
            h1 = h1 + _dot(src, wo_ref[sh])
        _, _, gate, pp = _ple_forward(h1, p_ref[...], g_ref[...], wpg_ref, wpp_ref)
        h1_ref[...] = h1
        h2 = h1 + gate * pp
        if last:
            acc_ref = refs[11]

            @pl.when(pl.program_id(0) == 0)
            def _():
                acc_ref[...] = jnp.zeros_like(acc_ref)

            h2_ref[...], part = _sq_err_grad(h2, refs[8][...])
            acc_ref[...] += part
        else:
            h2_ref[...] = h2

    row = lambda i: (i, 0)
    big = pl.BlockSpec((tm, D_MODEL), row)
    wspec = lambda r, cdim: pl.BlockSpec((N_CHIPS, r, cdim), lambda i: (0, 0, 0))
    acc = pl.BlockSpec((SUBLANES, 128), lambda i: (0, 0))
    return pl.pallas_call(
        body, name="out_ple_loss" if last else "out_ple", grid=(s // tm,),
        in_specs=[big, pl.BlockSpec((tm, SSM_WIDTH), row), pl.BlockSpec((tm, ATTN_WIDTH), row),
                  pl.BlockSpec((tm, PLE_DIM), row), pl.BlockSpec((1, D_MODEL), lambda i: (0, 0)),
                  wspec(ROW_SHARD, D_MODEL), wspec(ROW_SHARD, D_MODEL), wspec(PLE_DIM, ROW_SHARD)] + [big] * last,
        out_specs=[big] * 2 + [acc] * last,
        out_shape=[jax.ShapeDtypeStruct((s, D_MODEL), F32)] * 2 + [jax.ShapeDtypeStruct((SUBLANES, 128), F32)] * last,
        compiler_params=_cparams(1),
    )(h, ys, ya, p, g2, w_out_l, w_pg_l, w_pp_l, *([target] if last else []))


def _rms_bwd(x, r, g, dy):
    gdy = g * dy
    dx = r * gdy - x * (r * r * r) * jnp.mean(x * gdy, axis=-1, keepdims=True)
    return dx, x * r * dy


def _out_ple_bwd(dh2, h1, p, g2, w_out_l, w_pg_l, w_pp_l):
    s = h1.shape[0]
    tm = _row_tile(s, 512)

    def body(dh2_ref, h1_ref, p_ref, g_ref, wo_ref, wpg_ref, wpp_ref,
             dh1_ref, dmix_ref, hn_ref, dgp_ref, dpp_ref, dh1b_ref, dg_ref):
        @pl.when(pl.program_id(0) == 0)
        def _():
            dg_ref[...] = jnp.zeros_like(dg_ref)

        h1 = h1_ref[...]
        dh2 = dh2_ref[...]
        g2v = g_ref[...]
        r2, hb, gate, pp = _ple_forward(h1, p_ref[...], g2v, wpg_ref, wpp_ref)
        dgp = (dh2 * pp) * gate * (1.0 - gate)
        dgpb = dgp.astype(BF16)
        dhn = jnp.concatenate([_dot_nt(dgpb, wpg_ref[sh]) for sh in range(N_CHIPS)], axis=-1)
        dx, dgrow = _rms_bwd(h1, r2, g2v, dhn)
        dh1 = dh2 + dx
        dh1b = dh1.astype(BF16)
        dh1_ref[...] = dh1
        dh1b_ref[...] = dh1b
        hn_ref[...] = hb
        dgp_ref[...] = dgpb
        dpp_ref[...] = (dh2 * gate).astype(BF16)
        dg_ref[...] += _colsum8(dgrow)
        for sh in range(N_CHIPS):
            dmix_ref[:, ROW_SHARD * sh:ROW_SHARD * (sh + 1)] = _dot_nt(dh1b, wo_ref[sh])

    row = lambda i: (i, 0)
    wspec = lambda r, cdim: pl.BlockSpec((N_CHIPS, r, cdim), lambda i: (0, 0, 0))
    big = pl.BlockSpec((tm, D_MODEL), row)
    return pl.pallas_call(
        body, name="out_ple_bwd", grid=(s // tm,),
        in_specs=[big, big, pl.BlockSpec((tm, PLE_DIM), row), pl.BlockSpec((1, D_MODEL), lambda i: (0, 0)),
                  wspec(ROW_SHARD, D_MODEL), wspec(ROW_SHARD, D_MODEL), wspec(PLE_DIM, ROW_SHARD)],
        out_specs=[big] * 6 + [pl.BlockSpec((SUBLANES, D_MODEL), lambda i: (0, 0))],
        out_shape=[jax.ShapeDtypeStruct((s, D_MODEL), F32)] * 2 + [jax.ShapeDtypeStruct((s, D_MODEL), BF16)] * 4
        + [jax.ShapeDtypeStruct((SUBLANES, D_MODEL), F32)],
        compiler_params=_cparams(1),
    )(dh2, h1, p, g2, w_out_l, w_pg_l, w_pp_l)


def _tn_matmul(a, b, n_blocks, block_a, name, into=None, first_block=0, total_blocks=None):
    s = a.shape[0]
    tk = _row_tile(s, 1024)
    nk = s // tk
    total_blocks = n_blocks if total_blocks is None else total_blocks
    ka, nb = a.shape[1], b.shape[1]
    if block_a:
        ka //= n_blocks
    else:
        nb //= n_blocks

    def body(*refs):
        a_ref, b_ref, o_ref, acc_ref = refs[0], refs[1], refs[-2], refs[-1]

        @pl.when(pl.program_id(0) == 0)
        def _():
            acc_ref[...] = jnp.zeros_like(acc_ref)

        at = a_ref[...].astype(BF16).T
        bb = b_ref[...].astype(BF16)
        for sh in range(n_blocks):
            if block_a:
                acc_ref[sh] += _dot(at[ka * sh:ka * (sh + 1), :], bb)
            else:
                acc_ref[sh] += _dot(at, bb[:, nb * sh:nb * (sh + 1)])

        @pl.when(pl.program_id(0) == nk - 1)
        def _():
            o_ref[...] = acc_ref[...].astype(BF16)

    in_specs = [pl.BlockSpec((tk, a.shape[1]), lambda i: (i, 0)), pl.BlockSpec((tk, b.shape[1]), lambda i: (i, 0))]
    operands = [a, b]
    aliases = {}
    if into is not None:
        in_specs.append(ANY)
        operands.append(into)
        aliases = {2: 0}
    return pl.pallas_call(
        body, name=name, grid=(nk,),
        in_specs=in_specs,
        out_specs=pl.BlockSpec((n_blocks, ka, nb), lambda i: (first_block // n_blocks, 0, 0)),
        out_shape=jax.ShapeDtypeStruct((total_blocks, ka, nb), BF16),
        scratch_shapes=[pltpu.VMEM((n_blocks, ka, nb), F32)],
        input_output_aliases=aliases,
        compiler_params=_cparams(1),
    )(*operands)


def _attn_bwd(qkv, o, proj, dmix, scatter=None):
    scatter = [] if scatter is None else scatter
    nsc = len(scatter)
    s = qkv.shape[0]
    tb = ATTN_BLOCK
    nq = s // tb
    n_sub = min(ATTN_SUB_BWD, nq)
    tq = n_sub * tb
    n_chain = 2 * n_sub

    def body(*refs):
        q_ref, k_ref, v_ref, o_ref, g_ref, dya_ref, tri_s_ref, tri_p_ref = refs[0:8]
        sc_ins = refs[8:8 + nsc]
        dq_ref, dk_ref, dv_ref, dg_ref = refs[8 + nsc:12 + nsc]
        sc_outs = refs[12 + nsc:12 + 2 * nsc]
        do_scr, l_scr, g_scr, s_scr, w_scr = refs[12 + 2 * nsc:17 + 2 * nsc]
        sc_sems = refs[17 + 2 * nsc:]
        i = pl.program_id(1)
        base = i * n_sub

        if nsc:
            @pl.when(jnp.logical_and(pl.program_id(0) == 0, i == 0))
            def _():
                _scatter_start(sc_ins, sc_outs, sc_sems)

        @pl.when(i == 0)
        def _():
            dk_ref[...] = jnp.zeros_like(dk_ref)
            dv_ref[...] = jnp.zeros_like(dv_ref)

        g = g_ref[...]
        sg = _sigmoid(g)
        dya = dya_ref[...]
        do_scr[...] = (dya * (g * sg)).astype(BF16)
        dg_ref[...] = dya * o_ref[...] * (sg * (1.0 + g * (1.0 - sg)))
        dq_ref[...] = jnp.zeros_like(dq_ref)
        g_scr[...] = jnp.zeros_like(g_scr)
        masks = _head_masks()

        def keep(t, a, h, c, r0, z, w):
            s_scr[c, t] = _sigmoid(z).astype(BF16)
            w_scr[c, t] = w.astype(BF16)

        steps = _chain_sweep(base, n_sub, q_ref, k_ref, tri_s_ref, l_scr, keep)
        row = lax.broadcasted_iota(jnp.int32, (tb, tb), 0)
        col = lax.broadcasted_iota(jnp.int32, (tb, tb), 1)

        def back(it, carry):
            t = steps - 1 - it
            r0s = [pl.multiple_of(jnp.maximum(base + a - t, 0) * tb, tb) for a in range(n_sub)]
            qhs, dohs, khs, gws = [], [], [], []
            for a in range(n_sub):
                kb = k_ref[pl.ds(r0s[a], tb), :]
                vb = v_ref[pl.ds(r0s[a], tb), :]
                qa = q_ref[a * tb:(a + 1) * tb, :]
                doa = do_scr[a * tb:(a + 1) * tb, :]
                for h, mask in enumerate(masks):
                    qhs.append(jnp.where(mask, qa, jnp.zeros_like(qa)))
                    khs.append(jnp.where(mask, kb, jnp.zeros_like(kb)))
                    dohs.append(jnp.where(mask, doa, jnp.zeros_like(doa)))
                    gws.append(w_scr[2 * a + h, t].astype(F32) * _dot_nt(dohs[-1], vb))
            parts = [_split_hilo(gw) for gw in gws]
            tri = tri_p_ref[...]
            sums = [_dot(hi, tri) + _dot(lo, tri) for hi, lo in parts]
            dzs = []
            for c, (gw, sm) in enumerate(zip(gws, sums)):
                gsum = g_scr[c]
                dz = gw - (gw + sm[:, 0:tb] + gsum) * s_scr[c, t].astype(F32)
                dz = jnp.where(col < row + t * tb, dz, 0.0)
                g_scr[c] = gsum + sm[:, tb:2 * tb]
                dzs.append(dz.astype(BF16))
            for c, dzb in enumerate(dzs):
                a = c // 2
                dk_ref[pl.ds(r0s[a], tb), :] += _dot_tn(dzb, qhs[c])
                dv_ref[pl.ds(r0s[a], tb), :] += _dot_tn(w_scr[c, t], dohs[c])
                dq_ref[a * tb:(a + 1) * tb, :] += _dot(dzb, khs[c])
            return carry

        lax.fori_loop(0, steps, back, 0)

        if nsc:
            @pl.when(jnp.logical_and(pl.program_id(0) == n_hp - 1, i == s // tq - 1))
            def _():
                _scatter_finish(sc_ins, sc_outs, sc_sems)

    n_hp = ATTN_WIDTH // (2 * HEAD_DIM)
    hp_blk = lambda off: pl.BlockSpec((tq, 2 * HEAD_DIM), lambda hp, i: (i, off + hp))
    res = lambda off: pl.BlockSpec((s, 2 * HEAD_DIM), lambda hp, i: (0, off + hp))
    tri = pl.BlockSpec((tb, 2 * tb), lambda hp, i: (0, 0))
    outs = pl.pallas_call(
        body, name="attn_bwd_scatter" if nsc else "attn_bwd", grid=(n_hp, s // tq),
        in_specs=[hp_blk(0), res(4), res(8), hp_blk(0), hp_blk(20), hp_blk(4), tri, tri] + [ANY] * nsc,
        out_specs=[hp_blk(0), res(0), res(0), hp_blk(0)] + [ANY] * nsc,
        out_shape=[jax.ShapeDtypeStruct((s, ATTN_WIDTH), F32)] * 4 + [jax.ShapeDtypeStruct(a.shape, a.dtype) for a in scatter],
        scratch_shapes=[pltpu.VMEM((tq, 2 * HEAD_DIM), BF16), pltpu.VMEM((n_chain, tb, tb), F32),
                        pltpu.VMEM((n_chain, tb, tb), F32), pltpu.VMEM((n_chain, nq, tb, tb), BF16),
                        pltpu.VMEM((n_chain, nq, tb, tb), BF16)] + (_scatter_sems(nsc) if nsc else []),
        compiler_params=_cparams(2),
    )(qkv, qkv, qkv, o, proj, dmix, _tri("suffix_incl"), _tri("prefix_strict"), *scatter)
    return outs[0], outs[1], outs[2], outs[3], outs[4:]


def _ssm_glu_bwd(dmix, y, proj, d, w_glu_l, b_glu):
    s = y.shape[0]
    tm = _row_tile(s, 1024)

    def body(dys_ref, y_ref, u_ref, gs_ref, d_ref, wg_ref, bg_ref,
             dyf_ref, du_ref, dgs_ref, z_ref, dzz_ref, dd_ref, db_ref):
        @pl.when(pl.program_id(0) == 0)
        def _():
            dd_ref[...] = jnp.zeros_like(dd_ref)
            db_ref[...] = jnp.zeros_like(db_ref)

        u = u_ref[...]
        dv = d_ref[...]
        yf, z, val, gate = _glu_forward(y_ref[...], u, dv, wg_ref, bg_ref[...])
        gs = gs_ref[...]
        sgs = _sigmoid(gs)
        sgate = _sigmoid(gate)
        dys = dys_ref[...]
        dgv = dys * (gs * sgs)
        dgs_ref[...] = dys * (val * sgate) * (sgs * (1.0 + gs * (1.0 - sgs)))
        dzz = jnp.concatenate([dgv * sgate, dgv * val * sgate * (1.0 - sgate)], axis=-1)
        dzzb = dzz.astype(BF16)
        dz = _dot_nt(dzzb[:, 0:ROW_SHARD], wg_ref[0])
        for sh in range(1, N_CHIPS):
            dz = dz + _dot_nt(dzzb[:, ROW_SHARD * sh:ROW_SHARD * (sh + 1)], wg_ref[sh])
        dyf = dz * _gelu_grad(yf)
        dyf_ref[...] = dyf
        du_ref[...] = dyf * dv
        z_ref[...] = z.astype(BF16)
        dzz_ref[...] = dzzb
        dd_ref[...] += _colsum8(dyf * u)
        db_ref[...] += _colsum8(dzz)

    row = lambda i: (i, 0)
    half = pl.BlockSpec((tm, SSM_WIDTH), row)
    return pl.pallas_call(
        body, name="ssm_glu_bwd", grid=(s // tm,),
        in_specs=[half, half, half, pl.BlockSpec((tm, SSM_WIDTH), lambda i: (i, 1)),
                  pl.BlockSpec((1, SSM_WIDTH), lambda i: (0, 0)),
                  pl.BlockSpec((N_CHIPS, SSM_WIDTH, ROW_SHARD), lambda i: (0, 0, 0)),
                  pl.BlockSpec((1, 2 * SSM_WIDTH), lambda i: (0, 0))],
        out_specs=[half, half, half, half, pl.BlockSpec((tm, 2 * SSM_WIDTH), row),
                   pl.BlockSpec((SUBLANES, SSM_WIDTH), lambda i: (0, 0)),
                   pl.BlockSpec((SUBLANES, 2 * SSM_WIDTH), lambda i: (0, 0))],
        out_shape=[jax.ShapeDtypeStruct((s, SSM_WIDTH), F32)] * 3
        + [jax.ShapeDtypeStruct((s, SSM_WIDTH), BF16), jax.ShapeDtypeStruct((s, 2 * SSM_WIDTH), BF16),
           jax.ShapeDtypeStruct((SUBLANES, SSM_WIDTH), F32), jax.ShapeDtypeStruct((SUBLANES, 2 * SSM_WIDTH), F32)],
        compiler_params=_cparams(1),
    )(dmix, y, proj, proj, d, w_glu_l, b_glu)


def _ssm_scan_bwd(dyf, xs, proj, wct, tab_rev, wbt):
    s = dyf.shape[0]
    tm = _row_tile(s, SCAN_TILE)
    nt = s // tm
    length = tm // SUBLANES

    def body(dy_ref, xs_ref, u_ref, wct_ref, tab_ref, wbt_ref, du_ref, dwc_ref, dwb_ref, da_ref, lam_ref, carry_ref):
        @pl.when(pl.program_id(1) == 0)
        def _():
            carry_ref[...] = jnp.zeros_like(carry_ref)
            dwc_ref[...] = jnp.zeros_like(dwc_ref)
            dwb_ref[...] = jnp.zeros_like(dwb_ref)
            da_ref[...] = jnp.zeros_like(da_ref)

        dyp = _interleave_chunks(dy_ref[...]).astype(BF16)
        up = _interleave_chunks(u_ref[...]).astype(BF16)
        lam_ref[...] = _dot(dyp, wct_ref[...])

        def tail(r0, lr, li, carry):
            er, ei, dar, dai = carry
            xr = xs_ref[pl.ds(r0, SUBLANES), 0:CH_S]
            xi = xs_ref[pl.ds(r0, SUBLANES), CH_S:2 * CH_S]
            return lr, li, dar + (xr * er + xi * ei), dai + (xr * ei - xi * er)

        fix, (gr, gi) = _chunk_scan(lam_ref, tab_ref, carry_ref, length, reverse=True, tail=tail)
        zero = jnp.zeros((SUBLANES, CH_S), F32)
        _, _, dar, dai = lax.fori_loop(0, length, fix, (gr, gi, zero, zero), unroll=2)
        da_ref[:, 0:CH_S] += dar
        da_ref[:, CH_S:2 * CH_S] += dai
        lamb = lam_ref[...].astype(BF16)
        du_ref[...] = _time_order(_dot(lamb, wbt_ref[...]))
        dwc_ref[...] += _dot_tn(xs_ref[...].astype(BF16), dyp)
        dwb_ref[...] += _dot_tn(up, lamb)

    rev = lambda j, i: (nt - 1 - i, j)
    return pl.pallas_call(
        body, name="ssm_scan_bwd", grid=(SSM_CHUNKS, nt),
        in_specs=[pl.BlockSpec((tm, CH_W), rev),
                  pl.BlockSpec((None, tm, 2 * CH_S), lambda j, i: (j, nt - 1 - i, 0)),
                  pl.BlockSpec((tm, CH_W), rev),
                  pl.BlockSpec((None, CH_W, 2 * CH_S), lambda j, i: (j, 0, 0)),
                  pl.BlockSpec((None, length, 2 * CH_S), lambda j, i: (j, 0, 0)),
                  pl.BlockSpec((None, 2 * CH_S, CH_W), lambda j, i: (j, 0, 0))],
        out_specs=[pl.BlockSpec((tm, CH_W), rev),
                   pl.BlockSpec((None, 2 * CH_S, CH_W), lambda j, i: (j, 0, 0)),
                   pl.BlockSpec((None, CH_W, 2 * CH_S), lambda j, i: (j, 0, 0)),
                   pl.BlockSpec((None, SUBLANES, 2 * CH_S), lambda j, i: (j, 0, 0))],
        out_shape=[jax.ShapeDtypeStruct((s, SSM_WIDTH), F32),
                   jax.ShapeDtypeStruct((SSM_CHUNKS, 2 * CH_S, CH_W), F32),
                   jax.ShapeDtypeStruct((SSM_CHUNKS, CH_W, 2 * CH_S), F32),
                   jax.ShapeDtypeStruct((SSM_CHUNKS, SUBLANES, 2 * CH_S), F32)],
        scratch_shapes=[pltpu.VMEM((tm, 2 * CH_S), F32), pltpu.VMEM((SUBLANES, 2 * CH_S), F32)],
        compiler_params=_cparams(2),
    )(dyf, xs, proj, wct, tab_rev, wbt)


def _in_proj_bwd(h, g1, w_in_l, qg, kg, proj, du_a, du_b, dgs, dq, dk, dv, dga, dh1):
    s = h.shape[0]
    tm = _row_tile(s, 256)

    def body(h_ref, g_ref, w_ref, qg_ref, kg_ref, ones_ref, q_ref, k_ref, dua_ref, dub_ref, dgs_ref, dq_ref, dk_ref,
             dv_ref, dga_ref, dh1_ref, dh_ref, hn_ref, dp_ref, dg1_ref, dqg_ref, dkg_ref):
        @pl.when(pl.program_id(0) == 0)
        def _():
            dg1_ref[...] = jnp.zeros_like(dg1_ref)
            dqg_ref[...] = jnp.zeros_like(dqg_ref)
            dkg_ref[...] = jnp.zeros_like(dkg_ref)

        ones = ones_ref[...]

        def head_norm_bwd(x, gain, dy):
            r = lax.rsqrt(_dot_hilo(x * x, ones) + RMS_EPS)
            gdy = gain * dy
            dx = r * gdy - x * (r * r * r) * _dot_hilo(x * gdy, ones)
            return dx, x * r * dy

        dqr, dqg_rows = head_norm_bwd(q_ref[...], qg_ref[...], dq_ref[...] * ATTN_SCALE)
        dkr, dkg_rows = head_norm_bwd(k_ref[...], kg_ref[...], dk_ref[...])
        dqg_ref[...] += _colsum8(dqg_rows)
        dkg_ref[...] += _colsum8(dkg_rows)
        dp_ref[:, 0:512] = (dua_ref[...] + dub_ref[...]).astype(BF16)
        dp_ref[:, 512:1024] = dgs_ref[...].astype(BF16)
        dp_ref[:, 1024:1536] = dqr.astype(BF16)
        dp_ref[:, 1536:2048] = dkr.astype(BF16)
        dp_ref[:, 2048:2560] = dv_ref[...].astype(BF16)
        dp_ref[:, 2560:3072] = dga_ref[...].astype(BF16)
        dhn = _dot_nt(dp_ref[:, 0:IN_SHARD], w_ref[0])
        for sh in range(1, N_CHIPS):
            dhn = dhn + _dot_nt(dp_ref[:, IN_SHARD * sh:IN_SHARD * (sh + 1)], w_ref[sh])
        x = h_ref[...]
        gv = g_ref[...]
        r, hn = _rms_rows(x, gv)
        dx, dg_rows = _rms_bwd(x, r, gv, dhn)
        dh_ref[...] = dh1_ref[...] + dx
        hn_ref[...] = hn.astype(BF16)
        dg1_ref[...] += _colsum8(dg_rows)

    row = lambda i: (i, 0)
    full = lambda shape: pl.BlockSpec(shape, lambda i: (0,) * len(shape))
    big = pl.BlockSpec((tm, D_MODEL), row)
    half = pl.BlockSpec((tm, 512), row)
    return pl.pallas_call(
        body, name="in_proj_bwd", grid=(s // tm,),
        in_specs=[big, full((1, D_MODEL)), full((N_CHIPS, D_MODEL, IN_SHARD)),
                  full((1, ATTN_WIDTH)), full((1, ATTN_WIDTH)), full((ATTN_WIDTH, ATTN_WIDTH)),
                  pl.BlockSpec((tm, 512), lambda i: (i, 2)), pl.BlockSpec((tm, 512), lambda i: (i, 3)),
                  half, half, half, half, half, half, half, big],
        out_specs=[big, big, pl.BlockSpec((tm, IN_COLS), row), pl.BlockSpec((SUBLANES, D_MODEL), lambda i: (0, 0)),
                   pl.BlockSpec((SUBLANES, ATTN_WIDTH), lambda i: (0, 0)), pl.BlockSpec((SUBLANES, ATTN_WIDTH), lambda i: (0, 0))],
        out_shape=[jax.ShapeDtypeStruct((s, D_MODEL), F32), jax.ShapeDtypeStruct((s, D_MODEL), BF16),
                   jax.ShapeDtypeStruct((s, IN_COLS), BF16), jax.ShapeDtypeStruct((SUBLANES, D_MODEL), F32),
                   jax.ShapeDtypeStruct((SUBLANES, ATTN_WIDTH), F32), jax.ShapeDtypeStruct((SUBLANES, ATTN_WIDTH), F32)],
        compiler_params=_cparams(1),
    )(h, g1, w_in_l, qg, kg, _head_ones(), proj, proj, du_a, du_b, dgs, dq, dk, dv, dga, dh1)


SMALL_NAMES = ("mix_norm_g", "ssm_a_re", "ssm_a_im", "ssm_log_dt", "ssm_b_re", "ssm_b_im", "ssm_c_re", "ssm_c_im",
               "ssm_d", "ssm_b_glu", "q_norm_g", "k_norm_g", "ple_norm_g")
SMALL_4D = ("ssm_b_re", "ssm_b_im", "ssm_c_re", "ssm_c_im")
BIG_NAMES = ("w_in", "ssm_w_glu", "w_out", "w_ple_gate", "w_ple_proj")


def _ssm_setup(sm, layer, length):
    col = lambda a: a[layer].reshape(1, N_STATES)
    a_re, a_im = col(sm["ssm_a_re"]), col(sm["ssm_a_im"])
    log_dt = jnp.repeat(sm["ssm_log_dt"][layer], SSM_STATE).reshape(1, N_STATES)
    b_re = sm["ssm_b_re"][layer].reshape(N_STATES, SSM_GROUP).T
    b_im = sm["ssm_b_im"][layer].reshape(N_STATES, SSM_GROUP).T
    by_channel = lambda c: c[layer].transpose(1, 0, 2).reshape(SSM_GROUP, N_STATES)
    disc_in = (a_re, a_im, log_dt, b_re, b_im)
    wb, wbt, wct, wc, tab, tab_rev = _disc_fwd(*disc_in, by_channel(sm["ssm_c_re"]), by_channel(sm["ssm_c_im"]), length)
    return dict(disc_in=disc_in, wb=wb, wbt=wbt, wc=wc, wct=wct, tab=tab, tab_rev=tab_rev)


def _whole_blocks(names, gathered):
    return {n: g.reshape(N_CHIPS, 2 * g.shape[2], g.shape[3]) for n, g in zip(names, gathered)}


def _local_step(x, p, target, sm, w_in0, local=None, gathered=None, layer1_hook=None):
    wg = [dict(w_in=w_in0), {}] if gathered is None else gathered
    tile8 = lambda a: jnp.tile(a, ATTN_WIDTH // HEAD_DIM).reshape(1, ATTN_WIDTH)
    saved = []
    h = x
    for l in range(N_LAYERS):
        ssm = _ssm_setup(sm, l, _row_tile(x.shape[0], SCAN_TILE) // SUBLANES)
        g1 = sm["mix_norm_g"][l].reshape(1, D_MODEL)
        g2 = sm["ple_norm_g"][l].reshape(1, D_MODEL)
        qg, kg = tile8(sm["q_norm_g"][l]), tile8(sm["k_norm_g"][l])
        dsk = sm["ssm_d"][l].reshape(1, SSM_WIDTH)
        bgl = sm["ssm_b_glu"][l].reshape(1, 2 * SSM_WIDTH)
        proj, qkv = _in_proj(h, g1, wg[l]["w_in"], qg, kg)
        if l == 0 and local is not None:
            rest = BIG_NAMES[1:]
            xs, y, got = _ssm_scan_fwd(proj, ssm["wb"], ssm["tab"], ssm["wc"], [local[n] for n in rest], [0] * len(rest))
            wg[0].update(_whole_blocks(rest, got))
            ys = _ssm_glu_fwd(y, proj, dsk, wg[0]["ssm_w_glu"], bgl)
            o, ya, got = _attn_fwd(qkv, proj, [local[n] for n in BIG_NAMES], [2] * len(BIG_NAMES))
            wg[1].update(_whole_blocks(BIG_NAMES, got))
        else:
            xs, y, _ = _ssm_scan_fwd(proj, ssm["wb"], ssm["tab"], ssm["wc"])
            ys = _ssm_glu_fwd(y, proj, dsk, wg[l]["ssm_w_glu"], bgl)
            o, ya, _ = _attn_fwd(qkv, proj)
        tail = (target,) if l == N_LAYERS - 1 else ()
        h1, h2, *sq = _out_ple(h, ys, ya, p[l], g2, wg[l]["w_out"], wg[l]["w_ple_gate"], wg[l]["w_ple_proj"], *tail)
        saved.append(dict(ssm=ssm, g1=g1, g2=g2, qg=qg, kg=kg, dsk=dsk, bgl=bgl, h=h, proj=proj, qkv=qkv, xs=xs, y=y,
                          ys=ys, o=o, ya=ya, h1=h1))
        h = h2
    dh = h
    loss = 0.5 * jnp.sum(sq[0]) / D_MODEL

    gbig = [{} for _ in range(N_LAYERS)]
    scattered = ([], [])
    gsm = {n: [None] * N_LAYERS for n in SMALL_NAMES}
    for l in reversed(range(N_LAYERS)):
        sv = saved[l]
        ssm = sv["ssm"]
        dh1, dmix, hn2b, dgpb, dppb, dh1b, dg2 = _out_ple_bwd(dh, sv["h1"], p[l], sv["g2"], wg[l]["w_out"],
                                                              wg[l]["w_ple_gate"], wg[l]["w_ple_proj"])
        gsm["ple_norm_g"][l] = dg2.sum(0)
        gbig[l]["w_ple_proj"] = _tn_matmul(p[l], dppb, N_CHIPS, False, "dw_ple_proj")
        gbig[l]["w_ple_gate"] = _tn_matmul(hn2b, dgpb, N_CHIPS, True, "dw_ple_gate")
        dwo = _tn_matmul(sv["ys"], dh1b, 2, True, "dw_out_ssm", None, 0, N_CHIPS)
        gbig[l]["w_out"] = _tn_matmul(sv["ya"], dh1b, 2, True, "dw_out_attn", dwo, 2, N_CHIPS)
        if l == 0 and layer1_hook is not None:
            chip1 = layer1_hook(gbig[1])
            dqs, dkn, dv, dga, got = _attn_bwd(sv["qkv"], sv["o"], sv["proj"], dmix, chip1)
            scattered = (chip1, got)
        else:
            dqs, dkn, dv, dga, _ = _attn_bwd(sv["qkv"], sv["o"], sv["proj"], dmix)
        dyf, du_a, dgs, zb, dzzb, dd, dbg = _ssm_glu_bwd(dmix, sv["y"], sv["proj"], sv["dsk"], wg[l]["ssm_w_glu"], sv["bgl"])
        gsm["ssm_d"][l] = dd.sum(0).reshape(SSM_GROUPS, SSM_GROUP)
        gsm["ssm_b_glu"][l] = dbg.sum(0)
        gbig[l]["ssm_w_glu"] = _tn_matmul(zb, dzzb, N_CHIPS, False, "dw_glu")
        du_b, dwc, dwb, da = _ssm_scan_bwd(dyf, sv["xs"], sv["proj"], ssm["wct"], ssm["tab_rev"], ssm["wbt"])
        d_are, d_aim, d_ldt, d_bre, d_bim, d_cre, d_cim = _disc_bwd(*ssm["disc_in"], da, dwb, dwc)
        by_group = lambda t: t.reshape(SSM_GROUP, SSM_GROUPS, SSM_STATE).transpose(1, 0, 2)
        gsm["ssm_c_re"][l] = by_group(d_cre)
        gsm["ssm_c_im"][l] = by_group(d_cim)
        gsm["ssm_a_re"][l] = d_are.reshape(SSM_GROUPS, SSM_STATE)
        gsm["ssm_a_im"][l] = d_aim.reshape(SSM_GROUPS, SSM_STATE)
        gsm["ssm_log_dt"][l] = d_ldt.reshape(SSM_GROUPS, SSM_STATE).sum(1)
        gsm["ssm_b_re"][l] = d_bre.T.reshape(SSM_GROUPS, SSM_STATE, SSM_GROUP)
        gsm["ssm_b_im"][l] = d_bim.T.reshape(SSM_GROUPS, SSM_STATE, SSM_GROUP)
        dh, hnb, dprojb, dg1, dqg, dkg = _in_proj_bwd(sv["h"], sv["g1"], wg[l]["w_in"], sv["qg"], sv["kg"], sv["proj"],
                                                      du_a, du_b, dgs, dqs, dkn, dv, dga, dh1)
        gsm["mix_norm_g"][l] = dg1.sum(0)
        gsm["q_norm_g"][l] = dqg.sum(0).reshape(-1, HEAD_DIM).sum(0)
        gsm["k_norm_g"][l] = dkg.sum(0).reshape(-1, HEAD_DIM).sum(0)
        gbig[l]["w_in"] = _tn_matmul(hnb, dprojb, N_CHIPS, False, "dw_in")
    gsm = {n: jnp.stack(v, 0) for n, v in gsm.items()}
    return loss, dh, gbig, gsm, scattered


_SMALL_PAD = 8 * 8 * 128


def _pack_small(d, extra):
    flat = jnp.concatenate([d[n].reshape(-1) for n in SMALL_NAMES] + [jnp.stack(extra)])
    n = flat.shape[0]
    padded = -(-n // _SMALL_PAD) * _SMALL_PAD
    return jnp.pad(flat, (0, padded - n))


def _unpack_small(flat, like):
    out, off = {}, 0
    for n in SMALL_NAMES:
        size = like[n].size
        out[n] = flat[off:off + size].reshape(like[n].shape)
        off += size
    return out, flat[off:]


def _half_views(arrs):
    return [a.reshape(a.shape[0], 2, a.shape[1] // 2, a.shape[2]) for a in arrs]


def _chip_sums(views, out_dtypes, tag):
    recv = _sibling_push(views, "grad_push_" + tag)
    return [_add_my_half(v, r, dt, "grad_half_add") for v, r, dt in zip(views, recv, out_dtypes)]


def kernel(x, p, mix_norm_g, w_in, ssm_a_re, ssm_a_im, ssm_log_dt, ssm_b_re, ssm_b_im, ssm_c_re, ssm_c_im, ssm_d, ssm_w_glu, ssm_b_glu, q_norm_g, k_norm_g, w_out, ple_norm_g, w_ple_gate, w_ple_proj, loss_target, m_mix_norm_g, m_w_in, m_ssm_a_re, m_ssm_a_im, m_ssm_log_dt, m_ssm_b_re, m_ssm_b_im, m_ssm_c_re, m_ssm_c_im, m_ssm_d, m_ssm_w_glu, m_ssm_b_glu, m_q_norm_g, m_k_norm_g, m_w_out, m_ple_norm_g, m_w_ple_gate, m_w_ple_proj, v_mix_norm_g, v_w_in, v_ssm_a_re, v_ssm_a_im, v_ssm_log_dt, v_ssm_b_re, v_ssm_b_im, v_ssm_c_re, v_ssm_c_im, v_ssm_d, v_ssm_w_glu, v_ssm_b_glu, v_q_norm_g, v_k_norm_g, v_w_out, v_ple_norm_g, v_w_ple_gate, v_w_ple_proj):
    args = dict(locals())
    names = ("mix_norm_g", "w_in", "ssm_a_re", "ssm_a_im", "ssm_log_dt", "ssm_b_re", "ssm_b_im", "ssm_c_re", "ssm_c_im",
             "ssm_d", "ssm_w_glu", "ssm_b_glu", "q_norm_g", "k_norm_g", "w_out", "ple_norm_g", "w_ple_gate", "w_ple_proj")
    w = {n: args[n] for n in names}
    m = {n: args["m_" + n] for n in names}
    v = {n: args["v_" + n] for n in names}

    local = {n: w[n].astype(BF16).reshape(2 * N_LAYERS, w[n].shape[1] // 2, w[n].shape[2]) for n in BIG_NAMES}
    w_in0 = _chip_gather([local["w_in"]], "w_in_gather")[0].reshape(N_CHIPS, D_MODEL, IN_SHARD)
    sm = {n: w[n] for n in SMALL_NAMES}
    nb = len(BIG_NAMES)
    loss, dx, gbig, gsm, (chip1, got1) = _local_step(
        x[0], p[:, 0], loss_target[0], sm, w_in0, local,
        layer1_hook=lambda g1: _chip_sums(_half_views([g1[n] for n in BIG_NAMES]), [BF16] * nb, "layer1"))

    small = _pack_small(gsm, [loss]).reshape(N_CHIPS, 2, SUBLANES, -1)
    chip0 = _chip_sums(_half_views([gbig[0][n] for n in BIG_NAMES]) + [small], [BF16] * nb + [F32], "layer0")
    got0 = _chip_scatter(chip0, "grad_chip_scatter")
    tot1 = [_sum4(a, own, "grad_chip_sum") for a, own in zip(got1, chip1)]
    tot0 = [_sum4(a, own, "grad_chip_sum") for a, own in zip(got0, chip0)]
    pieces = [(t, k, (l,)) for l, tots in enumerate((tot0[:nb], tot1)) for k, t in enumerate(tots)] + [(tot0[nb], nb, ())]
    joined = _sibling_join(pieces, [(N_LAYERS, 2) + t.shape for t in tot1] + [(2,) + tot0[nb].shape], "grad_sibling_join")
    small_all = _chip_gather([joined[nb]], "small_grad_gather")[0]
    small_tot = small_all.reshape(-1)
    g = {n: j.reshape(w[n].shape) for n, j in zip(BIG_NAMES, joined)}
    g_small, rest = _unpack_small(small_tot, sm)
    g.update(g_small)
    loss = rest[0]

    delta, new_m, new_v = {}, {}, {}
    for n in BIG_NAMES:
        lanes = w[n].shape[-1]
        outs = _adamw(_as_rows(w[n], lanes), _as_rows(g[n], lanes), _as_rows(m[n], lanes), _as_rows(v[n], lanes), "adamw_" + n)
        delta[n], new_m[n], new_v[n] = [o.reshape(w[n].shape) for o in outs]
    swap = lambda n, a: jnp.swapaxes(a, -1, -2) if n in ("ssm_b_re", "ssm_b_im") else a
    for group, per_layer in ((SMALL_4D, True), (tuple(n for n in SMALL_NAMES if n not in SMALL_4D), False)):
        outs = _adamw_many(*[[swap(n, d[n]) for n in group] for d in (w, g, m, v)],
                           "adamw_small_4d" if per_layer else "adamw_small", per_layer)
        for d, o in zip((delta, new_m, new_v), outs):
            d.update({n: swap(n, a) for n, a in zip(group, o)})

    return (loss, dx[None], *[g[n] for n in names], *[delta[n] for n in names],
            *[new_m[n] for n in names], *[new_v[n] for n in names])
```

```python
import functools
import math

import jax
import jax.numpy as jnp
from jax import lax
from jax.experimental import pallas as pl
from jax.experimental.pallas import tpu as pltpu

F32 = jnp.float32
BF16 = jnp.bfloat16

D_MODEL = 1024
N_LAYERS = 2
N_CHIPS = 4
IN_COLS = 3072
IN_SHARD = IN_COLS // N_CHIPS
SSM_WIDTH = 512
SSM_GROUP = 16
SSM_GROUPS = 32
SSM_STATE = 64
N_STATES = SSM_GROUPS * SSM_STATE
SSM_CHUNKS = 4
CH_W = SSM_WIDTH // SSM_CHUNKS
CH_S = N_STATES // SSM_CHUNKS
ATTN_WIDTH = 512
HEAD_DIM = 64
PLE_DIM = 256
ROW_SHARD = 256
RMS_EPS = 1e-6
ATTN_SCALE = HEAD_DIM ** -0.5
ATTN_BLOCK = 128
EXP_ZERO = -87.5
SUBLANES = 8
SCAN_TILE = 1024
V7X_VMEM_LIMIT = 52 * 1024 * 1024
V7X_VMEM_LIMIT_ATTN_BWD = 60 * 1024 * 1024

ADAM_LR = 0.001
ADAM_B1 = 0.9
ADAM_B2 = 0.999
ADAM_EPS = 1e-08
ADAM_WD = 0.01
ADAM_STEP = 10

MESH = pl.DeviceIdType.MESH
ANY = pl.BlockSpec(memory_space=pl.ANY)


def _cparams(n_grid=0, parallel=0, vmem_limit=V7X_VMEM_LIMIT):
    sem = tuple(["parallel"] * parallel + ["arbitrary"] * (n_grid - parallel))
    return pltpu.CompilerParams(dimension_semantics=sem, vmem_limit_bytes=vmem_limit)


def _dot(a, b):
    return jnp.dot(a, b, preferred_element_type=F32)


def _dot_nt(a, b):
    return lax.dot_general(a, b, (((1,), (1,)), ((), ())), preferred_element_type=F32)


def _dot_tn(a, b):
    return lax.dot_general(a, b, (((0,), (0,)), ((), ())), preferred_element_type=F32)


def _split_hilo(a):
    hi = a.astype(BF16)
    lo = (a - hi.astype(F32)).astype(BF16)
    return hi, lo


def _dot_hilo(a, b):
    hi, lo = _split_hilo(a)
    return _dot(hi, b) + _dot(lo, b)


def _sigmoid(x):
    return 0.5 * (jnp.tanh(0.5 * x) + 1.0)


_GELU_C = math.sqrt(2.0 / math.pi)


def _gelu(x):
    return 0.5 * x * (1.0 + jnp.tanh(_GELU_C * (x + 0.044715 * (x * x * x))))


def _gelu_grad(x):
    t = jnp.tanh(_GELU_C * (x + 0.044715 * (x * x * x)))
    return 0.5 * (1.0 + t) + 0.5 * x * (1.0 - t * t) * (_GELU_C * (1.0 + 3.0 * 0.044715 * (x * x)))


def _row_tile(s, want):
    for t in range(min(s, want), 7, -1):
        if s % t == 0 and t % SUBLANES == 0:
            return t
    return s


def _coords():
    return lax.axis_index("x"), lax.axis_index("y"), lax.axis_index("c")


def _other_chips(x, y):
    return [(1 - x, y), (x, 1 - y), (1 - x, 1 - y)]


def _remote(src, dst, send_sem, recv_sem, dev):
    return pltpu.make_async_remote_copy(src_ref=src, dst_ref=dst, send_sem=send_sem, recv_sem=recv_sem,
                                        device_id=dev, device_id_type=MESH)


def _set_block(buf, block, index):
    return lax.dynamic_update_index_in_dim(buf, block, index, 0)


def _gather_sems(n):
    return [pltpu.SemaphoreType.DMA((3 * n,)) for _ in range(4)]


def _gather_copies(ins, bases, outs, sems):
    send_sems, recv_sems, fwd_send, fwd_recv = sems
    x, y, c = _coords()
    me_chip = 2 * x + y
    sibling = (x, y, 1 - c)
    first, landed, passed, from_sibling = [], [], [], []
    for k in range(len(ins)):
        for j, (cx, cy) in enumerate(_other_chips(x, y)):
            i = 3 * k + j
            first.append(_remote(ins[k].at[bases[k] + c], outs[k].at[me_chip, c], send_sems.at[i], recv_sems.at[i], (cx, cy, c)))
            blk = outs[k].at[2 * cx + cy, c]
            landed.append(_remote(blk, blk, send_sems.at[i], recv_sems.at[i], (cx, cy, c)))
            passed.append(_remote(blk, blk, fwd_send.at[i], fwd_recv.at[i], sibling))
            blk = outs[k].at[2 * cx + cy, 1 - c]
            from_sibling.append(_remote(blk, blk, fwd_send.at[i], fwd_recv.at[i], sibling))
    return first, landed, passed, from_sibling


def _gather_start(ins, bases, outs, sems):
    for cp in _gather_copies(ins, bases, outs, sems)[0]:
        cp.start()


def _gather_finish(ins, bases, outs, sems):
    first, landed, passed, from_sibling = _gather_copies(ins, bases, outs, sems)
    for arrived, forward in zip(landed, passed):
        arrived.wait_recv()
        forward.start()
    for cp in from_sibling:
        cp.wait_recv()
    for cp in first + passed:
        cp.wait_send()


def _gather_outputs(arrs):
    return [jax.ShapeDtypeStruct((N_CHIPS, 2) + a.shape[1:], a.dtype) for a in arrs]


def _gather_own(outs, arrs, bases):
    me_chip = 2 * lax.axis_index("x") + lax.axis_index("y")
    return [_set_block(o, lax.slice_in_dim(a, b, b + 2, axis=0), me_chip) for o, a, b in zip(outs, arrs, bases)]


def _chip_gather(arrs, name, bases=None):
    n = len(arrs)
    bases = [0] * n if bases is None else bases

    def body(*refs):
        ins, outs, sems = refs[:n], refs[n:2 * n], refs[2 * n:]
        _gather_start(ins, bases, outs, sems)
        _gather_finish(ins, bases, outs, sems)

    outs = pl.pallas_call(
        body, name=name, out_shape=_gather_outputs(arrs),
        in_specs=[ANY] * n, out_specs=[ANY] * n, scratch_shapes=_gather_sems(n),
    )(*arrs)
    return _gather_own(outs, arrs, bases)


def _sibling_push(arrs, name):
    n = len(arrs)

    def body(*refs):
        ins, outs = refs[:n], refs[n:2 * n]
        send_sems, recv_sems = refs[2 * n:]
        x, y, c = _coords()
        cps = [_remote(ins[k].at[pl.ds(0, N_CHIPS), 1 - c], outs[k], send_sems.at[k], recv_sems.at[k], (x, y, 1 - c))
               for k in range(n)]
        for cp in cps:
            cp.start()
        for cp in cps:
            cp.wait_recv()
        for cp in cps:
            cp.wait_send()

    return pl.pallas_call(
        body, name=name,
        out_shape=[jax.ShapeDtypeStruct((a.shape[0],) + a.shape[2:], a.dtype) for a in arrs],
        in_specs=[ANY] * n, out_specs=[ANY] * n,
        scratch_shapes=[pltpu.SemaphoreType.DMA((n,)), pltpu.SemaphoreType.DMA((n,))],
    )(*arrs)


def _sibling_join(pieces, out_shapes, name):
    n = len(pieces)
    no = len(out_shapes)

    def body(*refs):
        ins, outs = refs[:n], refs[n:n + no]
        send_sems, recv_sems = refs[n + no:]
        x, y, c = _coords()
        sibling = (x, y, 1 - c)
        cps = [_remote(ins[k], outs[o].at[lead + (c,)], send_sems.at[k], recv_sems.at[k], sibling)
               for k, (_, o, lead) in enumerate(pieces)]
        for cp in cps:
            cp.start()
        for k, (_, o, lead) in enumerate(pieces):
            blk = outs[o].at[lead + (1 - c,)]
            _remote(blk, blk, send_sems.at[k], recv_sems.at[k], sibling).wait_recv()
        for cp in cps:
            cp.wait_send()

    outs = pl.pallas_call(
        body, name=name,
        out_shape=[jax.ShapeDtypeStruct(sh, F32) for sh in out_shapes],
        in_specs=[ANY] * n, out_specs=[ANY] * no,
        scratch_shapes=[pltpu.SemaphoreType.DMA((n,)), pltpu.SemaphoreType.DMA((n,))],
    )(*[a for a, _, _ in pieces])
    outs = list(outs)
    c = lax.axis_index("c")
    for a, o, lead in pieces:
        block = a.reshape((1,) * (len(lead) + 1) + a.shape)
        outs[o] = lax.dynamic_update_slice(outs[o], block, lead + (c,) + (0,) * a.ndim)
    return outs


def _scatter_sems(n):
    return [pltpu.SemaphoreType.DMA((3 * n,)), pltpu.SemaphoreType.DMA((3 * n,))]


def _scatter_copies(ins, outs, sems):
    send_sems, recv_sems = sems
    x, y, c = _coords()
    me_chip = 2 * x + y
    sends, arrivals = [], []
    for k in range(len(ins)):
        for j, (cx, cy) in enumerate(_other_chips(x, y)):
            i = 3 * k + j
            sends.append(_remote(ins[k].at[2 * cx + cy], outs[k].at[me_chip], send_sems.at[i], recv_sems.at[i], (cx, cy, c)))
            blk = outs[k].at[2 * cx + cy]
            arrivals.append(_remote(blk, blk, send_sems.at[i], recv_sems.at[i], (cx, cy, c)))
    return sends, arrivals


def _scatter_start(ins, outs, sems):
    for cp in _scatter_copies(ins, outs, sems)[0]:
        cp.start()


def _scatter_finish(ins, outs, sems):
    sends, arrivals = _scatter_copies(ins, outs, sems)
    for cp in arrivals:
        cp.wait_recv()
    for cp in sends:
        cp.wait_send()


def _chip_scatter(arrs, name):
    n = len(arrs)

    def body(*refs):
        ins, outs, sems = refs[:n], refs[n:2 * n], refs[2 * n:]
        _scatter_start(ins, outs, sems)
        _scatter_finish(ins, outs, sems)

    outs = pl.pallas_call(
        body, name=name,
        out_shape=[jax.ShapeDtypeStruct(a.shape, a.dtype) for a in arrs],
        in_specs=[ANY] * n, out_specs=[ANY] * n, scratch_shapes=_scatter_sems(n),
    )(*arrs)
    return outs


def _as_rows(a, lanes):
    return a.reshape(-1, lanes)


def _add_my_half(v, recv, out_dtype, name):
    n_sh, _, h, cdim = v.shape
    tr = _row_tile(h, 512)

    def body(c_ref, a_ref, b_ref, o_ref):
        o_ref[...] = (a_ref[...].astype(F32) + b_ref[...].astype(F32)).astype(out_dtype)

    c = lax.axis_index("c").astype(jnp.int32).reshape(1)
    return pl.pallas_call(
        body, name=name,
        grid_spec=pltpu.PrefetchScalarGridSpec(
            num_scalar_prefetch=1, grid=(n_sh, h // tr),
            in_specs=[pl.BlockSpec((None, None, tr, cdim), lambda sh, i, c_ref: (sh, c_ref[0], i, 0)),
                      pl.BlockSpec((None, tr, cdim), lambda sh, i, c_ref: (sh, i, 0))],
            out_specs=pl.BlockSpec((None, tr, cdim), lambda sh, i, c_ref: (sh, i, 0))),
        out_shape=jax.ShapeDtypeStruct((n_sh, h, cdim), out_dtype),
        compiler_params=_cparams(2),
    )(c, v, recv)


def _sum4(got, own, name):
    _, r, cdim = got.shape
    tr = _row_tile(r, 512)

    def body(me_ref, p_ref, own_ref, o_ref):
        mine = own_ref[...].astype(F32)
        acc = None
        for j in range(N_CHIPS):
            term = jnp.where(me_ref[0] == j, mine, p_ref[j].astype(F32))
            acc = term if acc is None else acc + term
        o_ref[...] = acc

    me = (2 * lax.axis_index("x") + lax.axis_index("y")).astype(jnp.int32).reshape(1)
    return pl.pallas_call(
        body, name=name,
        grid_spec=pltpu.PrefetchScalarGridSpec(
            num_scalar_prefetch=1, grid=(r // tr,),
            in_specs=[pl.BlockSpec((N_CHIPS, tr, cdim), lambda i, me_ref: (0, i, 0)),
                      pl.BlockSpec((None, tr, cdim), lambda i, me_ref: (me_ref[0], i, 0))],
            out_specs=pl.BlockSpec((tr, cdim), lambda i, me_ref: (i, 0))),
        out_shape=jax.ShapeDtypeStruct((r, cdim), F32),
        compiler_params=_cparams(1),
    )(me, got, own)


def _adamw_math(w, g, m, v):
    c1 = 1.0 - ADAM_B1 ** ADAM_STEP
    c2 = 1.0 - ADAM_B2 ** ADAM_STEP
    nm = ADAM_B1 * m + (1.0 - ADAM_B1) * g
    nv = ADAM_B2 * v + (1.0 - ADAM_B2) * (g * g)
    delta = -ADAM_LR * ((nm / c1) / (jnp.sqrt(nv / c2) + ADAM_EPS) + ADAM_WD * w)
    return delta, nm, nv


def _adamw(w, g, m, v, name):
    r, cdim = w.shape
    tr = _row_tile(r, 256)

    def body(w_ref, g_ref, m_ref, v_ref, d_ref, nm_ref, nv_ref):
        d_ref[...], nm_ref[...], nv_ref[...] = _adamw_math(w_ref[...], g_ref[...], m_ref[...], v_ref[...])

    spec = pl.BlockSpec((tr, cdim), lambda i: (i, 0))
    return pl.pallas_call(
        body, name=name, grid=(r // tr,),
        in_specs=[spec] * 4, out_specs=[spec] * 3,
        out_shape=[jax.ShapeDtypeStruct((r, cdim), F32)] * 3,
        compiler_params=_cparams(1),
    )(w, g, m, v)


def _adamw_many(ws, gs, ms, vs, name, per_layer):
    n = len(ws)

    def body(*refs):
        for k in range(n):
            w, g, m, v = (refs[j * n + k][...] for j in range(4))
            outs = _adamw_math(w, g, m, v)
            for j in range(3):
                refs[(4 + j) * n + k][...] = outs[j]

    shapes = [jax.ShapeDtypeStruct(w.shape, F32) for w in ws]
    if per_layer:
        specs = [pl.BlockSpec((None,) + w.shape[1:], lambda l, nd=w.ndim: (l,) + (0,) * (nd - 1)) for w in ws]
        call = pl.pallas_call(body, name=name, grid=(N_LAYERS,), in_specs=specs * 4, out_specs=specs * 3,
                              out_shape=shapes * 3, compiler_params=_cparams(1))
    else:
        call = pl.pallas_call(body, name=name, out_shape=shapes * 3, compiler_params=_cparams())
    outs = call(*ws, *gs, *ms, *vs)
    return outs[0:n], outs[n:2 * n], outs[2 * n:3 * n]


def _cmul(ar, ai, br, bi):
    return ar * br - ai * bi, ar * bi + ai * br


def _discretise(a_re, a_im, log_dt, b_re, b_im):
    dt = jnp.exp(log_dt)
    mag = jnp.exp(a_re * dt)
    ab_re = mag * jnp.cos(a_im * dt)
    ab_im = mag * jnp.sin(a_im * dt)
    num_re = ab_re - 1.0
    num_im = ab_im
    den = a_re * a_re + a_im * a_im
    f_re = (num_re * a_re + num_im * a_im) / den
    f_im = (num_im * a_re - num_re * a_im) / den
    bb_re = f_re * b_re - f_im * b_im
    bb_im = f_re * b_im + f_im * b_re
    return ab_re, ab_im, bb_re, bb_im


def _disc_shapes():
    col = jax.ShapeDtypeStruct((1, N_STATES), F32)
    mat = jax.ShapeDtypeStruct((SSM_GROUP, N_STATES), F32)
    return col, mat


def _group_mask():
    row = lax.broadcasted_iota(jnp.int32, (CH_W, CH_S), 0)
    col = lax.broadcasted_iota(jnp.int32, (CH_W, CH_S), 1)
    return jnp.right_shift(row, SSM_GROUP.bit_length() - 1) == jnp.right_shift(col, SSM_STATE.bit_length() - 1)


def _block_diag(v, j):
    blk = v[:, CH_S * j:CH_S * (j + 1)]
    return jnp.where(_group_mask(), jnp.concatenate([blk] * (CH_W // SSM_GROUP), axis=0), 0.0)


def _block_diag_t(m):
    kept = jnp.where(_group_mask(), m, 0.0)
    return kept.reshape(CH_W // SSM_GROUP, SSM_GROUP, CH_S).sum(axis=0)


def _disc_fwd(a_re, a_im, log_dt, b_re, b_im, c_re, c_im, length):
    wide = jax.ShapeDtypeStruct((SSM_CHUNKS, CH_W, 2 * CH_S), BF16)
    tall = jax.ShapeDtypeStruct((SSM_CHUNKS, 2 * CH_S, CH_W), BF16)
    tab = jax.ShapeDtypeStruct((SSM_CHUNKS, length, 2 * CH_S), F32)

    def body(ar, ai, ld, br, bi, cr, ci, wb_ref, wbt_ref, wct_ref, wc_ref, tab_ref, rev_ref):
        ab_re, ab_im, bb_re, bb_im = _discretise(ar[...], ai[...], ld[...], br[...], bi[...])
        ccr, cci = cr[...], -ci[...]
        for j in range(SSM_CHUNKS):
            for lo, (vb, vc) in ((0, (bb_re, ccr)), (CH_S, (bb_im, cci))):
                mb, mc = _block_diag(vb, j), _block_diag(vc, j)
                wb_ref[j, :, lo:lo + CH_S] = mb.astype(BF16)
                wbt_ref[j, lo:lo + CH_S, :] = mb.T.astype(BF16)
                wct_ref[j, :, lo:lo + CH_S] = mc.astype(BF16)
                wc_ref[j, lo:lo + CH_S, :] = mc.T.astype(BF16)

        def step(j, carry):
            pr, pi = carry
            back = length - 1 - j
            for c in range(SSM_CHUNKS):
                lanes = slice(CH_S * c, CH_S * (c + 1))
                tab_ref[c, pl.ds(j, 1), 0:CH_S] = pr[:, lanes]
                tab_ref[c, pl.ds(j, 1), CH_S:2 * CH_S] = pi[:, lanes]
                rev_ref[c, pl.ds(back, 1), 0:CH_S] = pr[:, lanes]
                rev_ref[c, pl.ds(back, 1), CH_S:2 * CH_S] = -pi[:, lanes]
            return _cmul(pr, pi, ab_re, ab_im)

        lax.fori_loop(0, length, step, (ab_re, ab_im))

    return pl.pallas_call(body, name="ssm_discretise", out_shape=[wide, tall, wide, tall, tab, tab],
                          compiler_params=_cparams())(a_re, a_im, log_dt, b_re, b_im, c_re, c_im)


def _disc_bwd(a_re, a_im, log_dt, b_re, b_im, da, dwb, dwc):
    col, mat = _disc_shapes()

    def body(ar, ai, ld, br, bi, da_ref, dwb_ref, dwc_ref, o0, o1, o2, o3, o4, dcr_ref, dci_ref):
        g_ab = [jnp.concatenate([jnp.sum(da_ref[j, :, lo:lo + CH_S], axis=0, keepdims=True) for j in range(SSM_CHUNKS)],
                                axis=-1) for lo in (0, CH_S)]
        g_bb = [jnp.concatenate([_block_diag_t(dwb_ref[j, :, lo:lo + CH_S]) for j in range(SSM_CHUNKS)], axis=-1)
                for lo in (0, CH_S)]
        for ref, lo, sign in ((dcr_ref, 0, 1.0), (dci_ref, CH_S, -1.0)):
            ref[...] = sign * jnp.concatenate([_block_diag_t(dwc_ref[j, lo:lo + CH_S, :].T) for j in range(SSM_CHUNKS)],
                                              axis=-1)
        _, vjp = jax.vjp(_discretise, ar[...], ai[...], ld[...], br[...], bi[...])
        grads = vjp((g_ab[0], g_ab[1], g_bb[0], g_bb[1]))
        for o, val in zip((o0, o1, o2, o3, o4), grads):
            o[...] = val

    return pl.pallas_call(body, name="ssm_discretise_bwd", out_shape=[col, col, col, mat, mat, mat, mat],
                          compiler_params=_cparams())(a_re, a_im, log_dt, b_re, b_im, da, dwb, dwc)


def _interleave_chunks(v):
    rows, width = v.shape
    return pltpu.einshape("cjw->jcw", v.reshape(SUBLANES, rows // SUBLANES, width)).reshape(rows, width)


def _time_order(v):
    rows, width = v.shape
    return pltpu.einshape("jcw->cjw", v.reshape(rows // SUBLANES, SUBLANES, width)).reshape(rows, width)


def _head_ones():
    r = jnp.arange(ATTN_WIDTH) // HEAD_DIM
    return jnp.where(r[:, None] == r[None, :], 1.0 / HEAD_DIM, 0.0).astype(BF16)


def _in_proj(h, g1, w_in_l, qg, kg):
    s = h.shape[0]
    tm = _row_tile(s, 512)

    def body(h_ref, g_ref, w_ref, qg_ref, kg_ref, ones_ref, proj_ref, qkv_ref):
        x = h_ref[...]
        r = lax.rsqrt(jnp.mean(x * x, axis=-1, keepdims=True) + RMS_EPS)
        hn = (x * r * g_ref[...]).astype(BF16)
        for sh in range(N_CHIPS):
            proj_ref[:, IN_SHARD * sh:IN_SHARD * (sh + 1)] = _dot(hn, w_ref[sh])
        ones = ones_ref[...]
        q = proj_ref[:, 1024:1536]
        k = proj_ref[:, 1536:2048]
        rq = lax.rsqrt(_dot_hilo(q * q, ones) + RMS_EPS)
        rk = lax.rsqrt(_dot_hilo(k * k, ones) + RMS_EPS)
        qkv_ref[:, 0:512] = (q * rq * qg_ref[...] * ATTN_SCALE).astype(BF16)
        qkv_ref[:, 512:1024] = (k * rk * kg_ref[...]).astype(BF16)
        qkv_ref[:, 1024:1536] = proj_ref[:, 2048:2560].astype(BF16)

    full = lambda shape: pl.BlockSpec(shape, lambda i: (0,) * len(shape))
    return pl.pallas_call(
        body, name="in_proj", grid=(s // tm,),
        in_specs=[pl.BlockSpec((tm, D_MODEL), lambda i: (i, 0)), full((1, D_MODEL)),
                  full((N_CHIPS, D_MODEL, IN_SHARD)),
                  full((1, ATTN_WIDTH)), full((1, ATTN_WIDTH)), full((ATTN_WIDTH, ATTN_WIDTH))],
        out_specs=[pl.BlockSpec((tm, IN_COLS), lambda i: (i, 0)), pl.BlockSpec((tm, 3 * ATTN_WIDTH), lambda i: (i, 0))],
        out_shape=[jax.ShapeDtypeStruct((s, IN_COLS), F32), jax.ShapeDtypeStruct((s, 3 * ATTN_WIDTH), BF16)],
        compiler_params=_cparams(1),
    )(h, g1, w_in_l, qg, kg, _head_ones())


def _row_bcast(ref, k, lo):
    return jnp.broadcast_to(ref[pl.ds(k, 1), lo:lo + CH_S], (SUBLANES, CH_S))


def _chunk_scan(x_ref, tab_ref, carry_ref, length, reverse, tail=None):
    row = lax.broadcasted_iota(jnp.int32, (SUBLANES, CH_S), 0)
    one, full = (length - 1, 0) if reverse else (0, length - 1)
    ar, ai = _row_bcast(tab_ref, one, 0), _row_bcast(tab_ref, one, CH_S)
    fr, fi = _row_bcast(tab_ref, full, 0), _row_bcast(tab_ref, full, CH_S)
    step = lambda jj: (length - 1 - jj) if reverse else jj

    def local(jj, carry):
        cr, ci = carry
        r0 = pl.multiple_of(step(jj) * SUBLANES, SUBLANES)
        xr = x_ref[pl.ds(r0, SUBLANES), 0:CH_S] + (ar * cr - ai * ci)
        xi = x_ref[pl.ds(r0, SUBLANES), CH_S:2 * CH_S] + (ar * ci + ai * cr)
        x_ref[pl.ds(r0, SUBLANES), 0:CH_S] = xr
        x_ref[pl.ds(r0, SUBLANES), CH_S:2 * CH_S] = xi
        return xr, xi

    zero = jnp.zeros((SUBLANES, CH_S), F32)
    er, ei = lax.fori_loop(0, length, local, (zero, zero))

    first, shift = (SUBLANES - 1, SUBLANES - 1) if reverse else (0, 1)
    hr = jnp.where(row == first, carry_ref[:, 0:CH_S], 0.0)
    hi = jnp.where(row == first, carry_ref[:, CH_S:2 * CH_S], 0.0)
    sr, si = pltpu.roll(er, shift, 0), pltpu.roll(ei, shift, 0)
    for k in range(1, SUBLANES):
        tr, ti = pltpu.roll(hr, shift, 0), pltpu.roll(hi, shift, 0)
        here = row == ((SUBLANES - 1 - k) if reverse else k)
        hr, hi = (jnp.where(here, fr * tr - fi * ti + sr, hr), jnp.where(here, fr * ti + fi * tr + si, hi))
    last = 0 if reverse else SUBLANES - 1
    outr, outi = fr * hr - fi * hi + er, fr * hi + fi * hr + ei
    carry_ref[:, 0:CH_S] = jnp.broadcast_to(outr[last:last + 1, :], (SUBLANES, CH_S))
    carry_ref[:, CH_S:2 * CH_S] = jnp.broadcast_to(outi[last:last + 1, :], (SUBLANES, CH_S))

    def fix(jj, carry):
        j = step(jj)
        r0 = pl.multiple_of(j * SUBLANES, SUBLANES)
        pr, pi = _row_bcast(tab_ref, j, 0), _row_bcast(tab_ref, j, CH_S)
        xr = x_ref[pl.ds(r0, SUBLANES), 0:CH_S] + (pr * hr - pi * hi)
        xi = x_ref[pl.ds(r0, SUBLANES), CH_S:2 * CH_S] + (pr * hi + pi * hr)
        x_ref[pl.ds(r0, SUBLANES), 0:CH_S] = xr
        x_ref[pl.ds(r0, SUBLANES), CH_S:2 * CH_S] = xi
        if tail is None:
            return carry
        return tail(r0, xr, xi, carry)

    return fix, (hr, hi)


def _ssm_scan_fwd(proj, wb, tab, wc, gather=None, gather_bases=None):
    s = proj.shape[0]
    tm = _row_tile(s, SCAN_TILE)
    nt = s // tm
    length = tm // SUBLANES
    gather = [] if gather is None else gather
    ng = len(gather)

    def body(*refs):
        u_ref, wb_ref, tab_ref, wc_ref = refs[0:4]
        g_ins = refs[4:4 + ng]
        xs_ref, y_ref = refs[4 + ng:6 + ng]
        g_outs = refs[6 + ng:6 + 2 * ng]
        carry_ref = refs[6 + 2 * ng]
        sems = refs[7 + 2 * ng:]
        j, i = pl.program_id(0), pl.program_id(1)

        @pl.when(i == 0)
        def _():
            carry_ref[...] = jnp.zeros_like(carry_ref)

        if ng:
            @pl.when(jnp.logical_and(j == 0, i == 0))
            def _():
                _gather_start(g_ins, gather_bases, g_outs, sems)

        xs_ref[...] = _dot(_interleave_chunks(u_ref[...]).astype(BF16), wb_ref[...])
        fix, start = _chunk_scan(xs_ref, tab_ref, carry_ref, length, reverse=False)
        lax.fori_loop(0, length, fix, start, unroll=2)
        y_ref[...] = _time_order(_dot(xs_ref[...].astype(BF16), wc_ref[...]))

        if ng:
            @pl.when(jnp.logical_and(j == SSM_CHUNKS - 1, i == nt - 1))
            def _():
                _gather_finish(g_ins, gather_bases, g_outs, sems)

    outs = pl.pallas_call(
        body, name="ssm_scan_gather" if ng else "ssm_scan", grid=(SSM_CHUNKS, nt),
        in_specs=[pl.BlockSpec((tm, CH_W), lambda j, i: (i, j)),
                  pl.BlockSpec((None, CH_W, 2 * CH_S), lambda j, i: (j, 0, 0)),
                  pl.BlockSpec((None, length, 2 * CH_S), lambda j, i: (j, 0, 0)),
                  pl.BlockSpec((None, 2 * CH_S, CH_W), lambda j, i: (j, 0, 0))] + [ANY] * ng,
        out_specs=[pl.BlockSpec((None, tm, 2 * CH_S), lambda j, i: (j, i, 0)),
                   pl.BlockSpec((tm, CH_W), lambda j, i: (i, j))] + [ANY] * ng,
        out_shape=[jax.ShapeDtypeStruct((SSM_CHUNKS, s, 2 * CH_S), F32), jax.ShapeDtypeStruct((s, SSM_WIDTH), F32)]
        + _gather_outputs(gather),
        scratch_shapes=[pltpu.VMEM((SUBLANES, 2 * CH_S), F32)] + (_gather_sems(ng) if ng else []),
        compiler_params=_cparams(2),
    )(proj, wb, tab, wc, *gather)
    return outs[0], outs[1], (_gather_own(outs[2:], gather, gather_bases) if ng else [])


def _glu_forward(y, u, d, wg_ref, bg):
    yf = y + d * u
    z = _gelu(yf)
    zb = z.astype(BF16)
    zz = jnp.concatenate([_dot(zb, wg_ref[sh]) for sh in range(N_CHIPS)], axis=-1) + bg
    return yf, z, zz[:, 0:SSM_WIDTH], zz[:, SSM_WIDTH:2 * SSM_WIDTH]


def _ssm_glu_fwd(y, proj, d, w_glu_l, b_glu):
    s = y.shape[0]
    tm = _row_tile(s, 1024)

    def body(y_ref, u_ref, gs_ref, d_ref, wg_ref, bg_ref, o_ref):
        _, _, val, gate = _glu_forward(y_ref[...], u_ref[...], d_ref[...], wg_ref, bg_ref[...])
        gs = gs_ref[...]
        o_ref[...] = val * _sigmoid(gate) * (gs * _sigmoid(gs))

    row = lambda i: (i, 0)
    return pl.pallas_call(
        body, name="ssm_glu", grid=(s // tm,),
        in_specs=[pl.BlockSpec((tm, SSM_WIDTH), row), pl.BlockSpec((tm, SSM_WIDTH), row),
                  pl.BlockSpec((tm, SSM_WIDTH), lambda i: (i, 1)), pl.BlockSpec((1, SSM_WIDTH), lambda i: (0, 0)),
                  pl.BlockSpec((N_CHIPS, SSM_WIDTH, ROW_SHARD), lambda i: (0, 0, 0)),
                  pl.BlockSpec((1, 2 * SSM_WIDTH), lambda i: (0, 0))],
        out_specs=pl.BlockSpec((tm, SSM_WIDTH), row),
        out_shape=jax.ShapeDtypeStruct((s, SSM_WIDTH), F32),
        compiler_params=_cparams(1),
    )(y, proj, proj, d, w_glu_l, b_glu)


def _tri(kind):
    r = jnp.arange(ATTN_BLOCK)
    if kind == "suffix_incl":
        m = r[:, None] >= r[None, :]
    else:
        m = r[:, None] < r[None, :]
    return jnp.concatenate([m, jnp.ones_like(m)], axis=1).astype(BF16)


def _head_masks():
    lane = lax.broadcasted_iota(jnp.int32, (1, 2 * HEAD_DIM), 1)
    return [lane < HEAD_DIM, lane >= HEAD_DIM]


def _chain_step(t, base, n_sub, first, q_ref, k_ref, tri_ref, l_scr, per_chain):
    tb = ATTN_BLOCK
    row = lax.broadcasted_iota(jnp.int32, (tb, tb), 0)
    col = lax.broadcasted_iota(jnp.int32, (tb, tb), 1)
    masks = _head_masks()
    blks = [base + a - t for a in range(n_sub)]
    r0s = [pl.multiple_of(jnp.maximum(blk, 0) * tb, tb) for blk in blks]
    zs = []
    for a in range(n_sub):
        kb = k_ref[pl.ds(r0s[a], tb), :]
        qa = q_ref[a * tb:(a + 1) * tb, :]
        for mask in masks:
            zs.append(_dot_nt(jnp.where(mask, qa, jnp.zeros_like(qa)), kb))
    parts = []
    for z in zs:
        ls = jnp.minimum(-z, 0.0) - jnp.log(1.0 + jnp.exp(-jnp.abs(z)))
        if first:
            ls = jnp.where(col < row, ls, 0.0)
        parts.append(_split_hilo(ls))
    tri = tri_ref[...]
    sums = [_dot(hi, tri) + _dot(lo, tri) for hi, lo in parts]
    top = None
    ws = []
    for c, (z, sm) in enumerate(zip(zs, sums)):
        if first:
            lsum = jnp.zeros((tb, tb), F32)
        else:
            lsum = l_scr[c] + jnp.where(blks[c // 2] >= 0, 0.0, -1e30)
        w = jnp.exp(z + sm[:, 0:tb] + lsum)
        if first:
            w = jnp.where(col < row, w, 0.0)
        ws.append(w)
        lsum = lsum + sm[:, tb:2 * tb]
        l_scr[c] = lsum
        top = lsum if top is None else jnp.maximum(top, lsum)
    for c, (z, w) in enumerate(zip(zs, ws)):
        per_chain(c // 2, c % 2, c, r0s[c // 2], z, w)
    return jnp.max(top)


def _chain_sweep(base, n_sub, q_ref, k_ref, tri_ref, l_scr, per_chain):
    top = _chain_step(0, base, n_sub, True, q_ref, k_ref, tri_ref, l_scr, functools.partial(per_chain, 0))

    def cond(carry):
        t, top = carry
        return jnp.logical_and(t <= base + n_sub - 1, top > EXP_ZERO)

    def step(carry):
        t, _ = carry
        return t + 1, _chain_step(t, base, n_sub, False, q_ref, k_ref, tri_ref, l_scr, functools.partial(per_chain, t))

    steps, _ = lax.while_loop(cond, step, (jnp.int32(1), top))
    return steps


ATTN_SUB_FWD = 8
ATTN_SUB_BWD = 8


def _attn_fwd(qkv, proj, gather=None, gather_bases=None):
    s = qkv.shape[0]
    tb = ATTN_BLOCK
    n_sub = min(ATTN_SUB_FWD, s // tb)
    tq = n_sub * tb
    n_hp = ATTN_WIDTH // (2 * HEAD_DIM)
    gather = [] if gather is None else gather
    ng = len(gather)

    def body(*refs):
        q_ref, k_ref, v_ref, g_ref, tri_ref = refs[0:5]
        g_ins = refs[5:5 + ng]
        o_ref, ya_ref = refs[5 + ng:7 + ng]
        g_outs = refs[7 + ng:7 + 2 * ng]
        l_scr = refs[7 + 2 * ng]
        sems = refs[8 + 2 * ng:]
        i = pl.program_id(1)
        masks = _head_masks()
        o_ref[...] = jnp.zeros_like(o_ref)

        if ng:
            @pl.when(jnp.logical_and(pl.program_id(0) == 0, i == 0))
            def _():
                _gather_start(g_ins, gather_bases, g_outs, sems)

        def per_chain(t, a, h, c, r0, z, w):
            vb = v_ref[pl.ds(r0, tb), :]
            vb = jnp.where(masks[h], vb, jnp.zeros_like(vb))
            o_ref[a * tb:(a + 1) * tb, :] += _dot(w.astype(BF16), vb)

        _chain_sweep(i * n_sub, n_sub, q_ref, k_ref, tri_ref, l_scr, per_chain)
        g = g_ref[...]
        ya_ref[...] = o_ref[...] * (g * _sigmoid(g))

        if ng:
            @pl.when(jnp.logical_and(pl.program_id(0) == n_hp - 1, i == s // tq - 1))
            def _():
                _gather_finish(g_ins, gather_bases, g_outs, sems)

    hp_blk = lambda off: pl.BlockSpec((tq, 2 * HEAD_DIM), lambda hp, i: (i, off + hp))
    res = lambda off: pl.BlockSpec((s, 2 * HEAD_DIM), lambda hp, i: (0, off + hp))
    outs = pl.pallas_call(
        body, name="attn_fwd_gather" if ng else "attn_fwd", grid=(n_hp, s // tq),
        in_specs=[hp_blk(0), res(4), res(8), hp_blk(20), pl.BlockSpec((tb, 2 * tb), lambda hp, i: (0, 0))] + [ANY] * ng,
        out_specs=[hp_blk(0), hp_blk(0)] + [ANY] * ng,
        out_shape=[jax.ShapeDtypeStruct((s, ATTN_WIDTH), F32)] * 2 + _gather_outputs(gather),
        scratch_shapes=[pltpu.VMEM((2 * n_sub, tb, tb), F32)] + (_gather_sems(ng) if ng else []),
        compiler_params=_cparams(2),
    )(qkv, qkv, qkv, proj, _tri("suffix_incl"), *gather)
    return outs[0], outs[1], (_gather_own(outs[2:], gather, gather_bases) if ng else [])


def _rms_rows(x, g):
    r = lax.rsqrt(jnp.mean(x * x, axis=-1, keepdims=True) + RMS_EPS)
    return r, x * r * g


def _ple_forward(h1, p, g2, wpg_ref, wpp_ref):
    r2, hn2 = _rms_rows(h1, g2)
    hb = hn2.astype(BF16)
    gpre = _dot(hb[:, 0:ROW_SHARD], wpg_ref[0])
    for sh in range(1, N_CHIPS):
        gpre = gpre + _dot(hb[:, ROW_SHARD * sh:ROW_SHARD * (sh + 1)], wpg_ref[sh])
    gate = _sigmoid(gpre)
    pb = p.astype(BF16)
    pp = jnp.concatenate([_dot(pb, wpp_ref[sh]) for sh in range(N_CHIPS)], axis=-1)
    return r2, hb, gate, pp


def _colsum8(a):
    t = a.shape[0]
    return a.reshape(t // SUBLANES, SUBLANES, a.shape[1]).sum(axis=0)


def _sq_err_grad(y, target):
    e = y - target
    sq = _colsum8(e * e)
    part = sq[:, 0:128]
    for b in range(1, D_MODEL // 128):
        part = part + sq[:, 128 * b:128 * (b + 1)]
    return e / D_MODEL, part


def _out_ple(h, ys, ya, p, g2, w_out_l, w_pg_l, w_pp_l, target=None):
    s = h.shape[0]
    tm = _row_tile(s, 512)
    last = target is not None

    def body(*refs):
        h_ref, ys_ref, ya_ref, p_ref, g_ref, wo_ref, wpg_ref, wpp_ref = refs[0:8]
        h1_ref, h2_ref = refs[8 + last], refs[9 + last]
        ysb = ys_ref[...].astype(BF16)
        yab = ya_ref[...].astype(BF16)
        h1 = h_ref[...]
        for sh, src in enumerate((ysb[:, 0:ROW_SHARD], ysb[:, ROW_SHARD:], yab[:, 0:ROW_SHARD], yab[:, ROW_SHARD:])):
            h1 = h1 + _dot(src, wo_ref[sh])
        _, _, gate, pp = _ple_forward(h1, p_ref[...], g_ref[...], wpg_ref, wpp_ref)
        h1_ref[...] = h1
        h2 = h1 + gate * pp
        if last:
            acc_ref = refs[11]

            @pl.when(pl.program_id(0) == 0)
            def _():
                acc_ref[...] = jnp.zeros_like(acc_ref)

            h2_ref[...], part = _sq_err_grad(h2, refs[8][...])
            acc_ref[...] += part
        else:
            h2_ref[...] = h2

    row = lambda i: (i, 0)
    big = pl.BlockSpec((tm, D_MODEL), row)
    wspec = lambda r, cdim: pl.BlockSpec((N_CHIPS, r, cdim), lambda i: (0, 0, 0))
    acc = pl.BlockSpec((SUBLANES, 128), lambda i: (0, 0))
    return pl.pallas_call(
        body, name="out_ple_loss" if last else "out_ple", grid=(s // tm,),
        in_specs=[big, pl.BlockSpec((tm, SSM_WIDTH), row), pl.BlockSpec((tm, ATTN_WIDTH), row),
                  pl.BlockSpec((tm, PLE_DIM), row), pl.BlockSpec((1, D_MODEL), lambda i: (0, 0)),
                  wspec(ROW_SHARD, D_MODEL), wspec(ROW_SHARD, D_MODEL), wspec(PLE_DIM, ROW_SHARD)] + [big] * last,
        out_specs=[big] * 2 + [acc] * last,
        out_shape=[jax.ShapeDtypeStruct((s, D_MODEL), F32)] * 2 + [jax.ShapeDtypeStruct((SUBLANES, 128), F32)] * last,
        compiler_params=_cparams(1),
    )(h, ys, ya, p, g2, w_out_l, w_pg_l, w_pp_l, *([target] if last else []))


def _rms_bwd(x, r, g, dy):
    gdy = g * dy
    dx = r * gdy - x * (r * r * r) * jnp.mean(x * gdy, axis=-1, keepdims=True)
    return dx, x * r * dy


def _out_ple_bwd(dh2, h1, p, g2, w_out_l, w_pg_l, w_pp_l):
    s = h1.shape[0]
    tm = _row_tile(s, 512)

    def body(dh2_ref, h1_ref, p_ref, g_ref, wo_ref, wpg_ref, wpp_ref,
             dh1_ref, dmix_ref, hn_ref, dgp_ref, dpp_ref, dh1b_ref, dg_ref):
        @pl.when(pl.program_id(0) == 0)
        def _():
            dg_ref[...] = jnp.zeros_like(dg_ref)

        h1 = h1_ref[...]
        dh2 = dh2_ref[...]
        g2v = g_ref[...]
        r2, hb, gate, pp = _ple_forward(h1, p_ref[...], g2v, wpg_ref, wpp_ref)
        dgp = (dh2 * pp) * gate * (1.0 - gate)
        dgpb = dgp.astype(BF16)
        dhn = jnp.concatenate([_dot_nt(dgpb, wpg_ref[sh]) for sh in range(N_CHIPS)], axis=-1)
        dx, dgrow = _rms_bwd(h1, r2, g2v, dhn)
        dh1 = dh2 + dx
        dh1b = dh1.astype(BF16)
        dh1_ref[...] = dh1
        dh1b_ref[...] = dh1b
        hn_ref[...] = hb
        dgp_ref[...] = dgpb
        dpp_ref[...] = (dh2 * gate).astype(BF16)
        dg_ref[...] += _colsum8(dgrow)
        for sh in range(N_CHIPS):
            dmix_ref[:, ROW_SHARD * sh:ROW_SHARD * (sh + 1)] = _dot_nt(dh1b, wo_ref[sh])

    row = lambda i: (i, 0)
    wspec = lambda r, cdim: pl.BlockSpec((N_CHIPS, r, cdim), lambda i: (0, 0, 0))
    big = pl.BlockSpec((tm, D_MODEL), row)
    return pl.pallas_call(
        body, name="out_ple_bwd", grid=(s // tm,),
        in_specs=[big, big, pl.BlockSpec((tm, PLE_DIM), row), pl.BlockSpec((1, D_MODEL), lambda i: (0, 0)),
                  wspec(ROW_SHARD, D_MODEL), wspec(ROW_SHARD, D_MODEL), wspec(PLE_DIM, ROW_SHARD)],
        out_specs=[big] * 6 + [pl.BlockSpec((SUBLANES, D_MODEL), lambda i: (0, 0))],
        out_shape=[jax.ShapeDtypeStruct((s, D_MODEL), F32)] * 2 + [jax.ShapeDtypeStruct((s, D_MODEL), BF16)] * 4
        + [jax.ShapeDtypeStruct((SUBLANES, D_MODEL), F32)],
        compiler_params=_cparams(1),
    )(dh2, h1, p, g2, w_out_l, w_pg_l, w_pp_l)


def _tn_matmul(a, b, n_blocks, block_a, name, into=None, first_block=0, total_blocks=None):
    s = a.shape[0]
    tk = _row_tile(s, 1024)
    nk = s // tk
    total_blocks = n_blocks if total_blocks is None else total_blocks
    ka, nb = a.shape[1], b.shape[1]
    if block_a:
        ka //= n_blocks
    else:
        nb //= n_blocks

    def body(*refs):
        a_ref, b_ref, o_ref, acc_ref = refs[0], refs[1], refs[-2], refs[-1]

        @pl.when(pl.program_id(0) == 0)
        def _():
            acc_ref[...] = jnp.zeros_like(acc_ref)

        at = a_ref[...].astype(BF16).T
        bb = b_ref[...].astype(BF16)
        for sh in range(n_blocks):
            if block_a:
                acc_ref[sh] += _dot(at[ka * sh:ka * (sh + 1), :], bb)
            else:
                acc_ref[sh] += _dot(at, bb[:, nb * sh:nb * (sh + 1)])

        @pl.when(pl.program_id(0) == nk - 1)
        def _():
            o_ref[...] = acc_ref[...].astype(BF16)

    in_specs = [pl.BlockSpec((tk, a.shape[1]), lambda i: (i, 0)), pl.BlockSpec((tk, b.shape[1]), lambda i: (i, 0))]
    operands = [a, b]
    aliases = {}
    if into is not None:
        in_specs.append(ANY)
        operands.append(into)
        aliases = {2: 0}
    return pl.pallas_call(
        body, name=name, grid=(nk,),
        in_specs=in_specs,
        out_specs=pl.BlockSpec((n_blocks, ka, nb), lambda i: (first_block // n_blocks, 0, 0)),
        out_shape=jax.ShapeDtypeStruct((total_blocks, ka, nb), BF16),
        scratch_shapes=[pltpu.VMEM((n_blocks, ka, nb), F32)],
        input_output_aliases=aliases,
        compiler_params=_cparams(1),
    )(*operands)


def _attn_bwd(qkv, o, proj, dmix, scatter=None):
    scatter = [] if scatter is None else scatter
    nsc = len(scatter)
    s = qkv.shape[0]
    tb = ATTN_BLOCK
    nq = s // tb
    n_sub = min(ATTN_SUB_BWD, nq)
    tq = n_sub * tb
    n_chain = 2 * n_sub

    def body(*refs):
        q_ref, k_ref, v_ref, o_ref, g_ref, dya_ref, tri_s_ref, tri_p_ref = refs[0:8]
        sc_ins = refs[8:8 + nsc]
        dq_ref, dk_ref, dv_ref, dg_ref = refs[8 + nsc:12 + nsc]
        sc_outs = refs[12 + nsc:12 + 2 * nsc]
        do_scr, l_scr, g_scr, s_scr, w_scr = refs[12 + 2 * nsc:17 + 2 * nsc]
        sc_sems = refs[17 + 2 * nsc:]
        i = pl.program_id(1)
        base = i * n_sub

        if nsc:
            @pl.when(jnp.logical_and(pl.program_id(0) == 0, i == 0))
            def _():
                _scatter_start(sc_ins, sc_outs, sc_sems)

        @pl.when(i == 0)
        def _():
            dk_ref[...] = jnp.zeros_like(dk_ref)
            dv_ref[...] = jnp.zeros_like(dv_ref)

        g = g_ref[...]
        sg = _sigmoid(g)
        dya = dya_ref[...]
        do_scr[...] = (dya * (g * sg)).astype(BF16)
        dg_ref[...] = dya * o_ref[...] * (sg * (1.0 + g * (1.0 - sg)))
        dq_ref[...] = jnp.zeros_like(dq_ref)
        g_scr[...] = jnp.zeros_like(g_scr)
        masks = _head_masks()

        def keep(t, a, h, c, r0, z, w):
            s_scr[c, t] = _sigmoid(z).astype(BF16)
            w_scr[c, t] = w.astype(BF16)

        steps = _chain_sweep(base, n_sub, q_ref, k_ref, tri_s_ref, l_scr, keep)
        row = lax.broadcasted_iota(jnp.int32, (tb, tb), 0)
        col = lax.broadcasted_iota(jnp.int32, (tb, tb), 1)

        def back(it, carry):
            t = steps - 1 - it
            r0s = [pl.multiple_of(jnp.maximum(base + a - t, 0) * tb, tb) for a in range(n_sub)]
            qhs, dohs, khs, gws = [], [], [], []
            for a in range(n_sub):
                kb = k_ref[pl.ds(r0s[a], tb), :]
                vb = v_ref[pl.ds(r0s[a], tb), :]
                qa = q_ref[a * tb:(a + 1) * tb, :]
                doa = do_scr[a * tb:(a + 1) * tb, :]
                for h, mask in enumerate(masks):
                    qhs.append(jnp.where(mask, qa, jnp.zeros_like(qa)))
                    khs.append(jnp.where(mask, kb, jnp.zeros_like(kb)))
                    dohs.append(jnp.where(mask, doa, jnp.zeros_like(doa)))
                    gws.append(w_scr[2 * a + h, t].astype(F32) * _dot_nt(dohs[-1], vb))
            parts = [_split_hilo(gw) for gw in gws]
            tri = tri_p_ref[...]
            sums = [_dot(hi, tri) + _dot(lo, tri) for hi, lo in parts]
            dzs = []
            for c, (gw, sm) in enumerate(zip(gws, sums)):
                gsum = g_scr[c]
                dz = gw - (gw + sm[:, 0:tb] + gsum) * s_scr[c, t].astype(F32)
                dz = jnp.where(col < row + t * tb, dz, 0.0)
                g_scr[c] = gsum + sm[:, tb:2 * tb]
                dzs.append(dz.astype(BF16))
            for c, dzb in enumerate(dzs):
                a = c // 2
                dk_ref[pl.ds(r0s[a], tb), :] += _dot_tn(dzb, qhs[c])
                dv_ref[pl.ds(r0s[a], tb), :] += _dot_tn(w_scr[c, t], dohs[c])
                dq_ref[a * tb:(a + 1) * tb, :] += _dot(dzb, khs[c])
            return carry

        lax.fori_loop(0, steps, back, 0)

        if nsc:
            @pl.when(jnp.logical_and(pl.program_id(0) == n_hp - 1, i == s // tq - 1))
            def _():
                _scatter_finish(sc_ins, sc_outs, sc_sems)

    n_hp = ATTN_WIDTH // (2 * HEAD_DIM)
    hp_blk = lambda off: pl.BlockSpec((tq, 2 * HEAD_DIM), lambda hp, i: (i, off + hp))
    res = lambda off: pl.BlockSpec((s, 2 * HEAD_DIM), lambda hp, i: (0, off + hp))
    tri = pl.BlockSpec((tb, 2 * tb), lambda hp, i: (0, 0))
    outs = pl.pallas_call(
        body, name="attn_bwd_scatter" if nsc else "attn_bwd", grid=(n_hp, s // tq),
        in_specs=[hp_blk(0), res(4), res(8), hp_blk(0), hp_blk(20), hp_blk(4), tri, tri] + [ANY] * nsc,
        out_specs=[hp_blk(0), res(0), res(0), hp_blk(0)] + [ANY] * nsc,
        out_shape=[jax.ShapeDtypeStruct((s, ATTN_WIDTH), F32)] * 4 + [jax.ShapeDtypeStruct(a.shape, a.dtype) for a in scatter],
        scratch_shapes=[pltpu.VMEM((tq, 2 * HEAD_DIM), BF16), pltpu.VMEM((n_chain, tb, tb), F32),
                        pltpu.VMEM((n_chain, tb, tb), F32), pltpu.VMEM((n_chain, nq, tb, tb), BF16),
                        pltpu.VMEM((n_chain, nq, tb, tb), BF16)] + (_scatter_sems(nsc) if nsc else []),
        compiler_params=_cparams(2, vmem_limit=V7X_VMEM_LIMIT_ATTN_BWD),
    )(qkv, qkv, qkv, o, proj, dmix, _tri("suffix_incl"), _tri("prefix_strict"), *scatter)
    return outs[0], outs[1], outs[2], outs[3], outs[4:]


def _ssm_glu_bwd(dmix, y, proj, d, w_glu_l, b_glu):
    s = y.shape[0]
    tm = _row_tile(s, 1024)

    def body(dys_ref, y_ref, u_ref, gs_ref, d_ref, wg_ref, bg_ref,
             dyf_ref, du_ref, dgs_ref, z_ref, dzz_ref, dd_ref, db_ref):
        @pl.when(pl.program_id(0) == 0)
        def _():
            dd_ref[...] = jnp.zeros_like(dd_ref)
            db_ref[...] = jnp.zeros_like(db_ref)

        u = u_ref[...]
        dv = d_ref[...]
        yf, z, val, gate = _glu_forward(y_ref[...], u, dv, wg_ref, bg_ref[...])
        gs = gs_ref[...]
        sgs = _sigmoid(gs)
        sgate = _sigmoid(gate)
        dys = dys_ref[...]
        dgv = dys * (gs * sgs)
        dgs_ref[...] = dys * (val * sgate) * (sgs * (1.0 + gs * (1.0 - sgs)))
        dzz = jnp.concatenate([dgv * sgate, dgv * val * sgate * (1.0 - sgate)], axis=-1)
        dzzb = dzz.astype(BF16)
        dz = _dot_nt(dzzb[:, 0:ROW_SHARD], wg_ref[0])
        for sh in range(1, N_CHIPS):
            dz = dz + _dot_nt(dzzb[:, ROW_SHARD * sh:ROW_SHARD * (sh + 1)], wg_ref[sh])
        dyf = dz * _gelu_grad(yf)
        dyf_ref[...] = dyf
        du_ref[...] = dyf * dv
        z_ref[...] = z.astype(BF16)
        dzz_ref[...] = dzzb
        dd_ref[...] += _colsum8(dyf * u)
        db_ref[...] += _colsum8(dzz)

    row = lambda i: (i, 0)
    half = pl.BlockSpec((tm, SSM_WIDTH), row)
    return pl.pallas_call(
        body, name="ssm_glu_bwd", grid=(s // tm,),
        in_specs=[half, half, half, pl.BlockSpec((tm, SSM_WIDTH), lambda i: (i, 1)),
                  pl.BlockSpec((1, SSM_WIDTH), lambda i: (0, 0)),
                  pl.BlockSpec((N_CHIPS, SSM_WIDTH, ROW_SHARD), lambda i: (0, 0, 0)),
                  pl.BlockSpec((1, 2 * SSM_WIDTH), lambda i: (0, 0))],
        out_specs=[half, half, half, half, pl.BlockSpec((tm, 2 * SSM_WIDTH), row),
                   pl.BlockSpec((SUBLANES, SSM_WIDTH), lambda i: (0, 0)),
                   pl.BlockSpec((SUBLANES, 2 * SSM_WIDTH), lambda i: (0, 0))],
        out_shape=[jax.ShapeDtypeStruct((s, SSM_WIDTH), F32)] * 3
        + [jax.ShapeDtypeStruct((s, SSM_WIDTH), BF16), jax.ShapeDtypeStruct((s, 2 * SSM_WIDTH), BF16),
           jax.ShapeDtypeStruct((SUBLANES, SSM_WIDTH), F32), jax.ShapeDtypeStruct((SUBLANES, 2 * SSM_WIDTH), F32)],
        compiler_params=_cparams(1),
    )(dmix, y, proj, proj, d, w_glu_l, b_glu)


def _ssm_scan_bwd(dyf, xs, proj, wct, tab_rev, wbt):
    s = dyf.shape[0]
    tm = _row_tile(s, SCAN_TILE)
    nt = s // tm
    length = tm // SUBLANES

    def body(dy_ref, xs_ref, u_ref, wct_ref, tab_ref, wbt_ref, du_ref, dwc_ref, dwb_ref, da_ref, lam_ref, carry_ref):
        @pl.when(pl.program_id(1) == 0)
        def _():
            carry_ref[...] = jnp.zeros_like(carry_ref)
            dwc_ref[...] = jnp.zeros_like(dwc_ref)
            dwb_ref[...] = jnp.zeros_like(dwb_ref)
            da_ref[...] = jnp.zeros_like(da_ref)

        dyp = _interleave_chunks(dy_ref[...]).astype(BF16)
        up = _interleave_chunks(u_ref[...]).astype(BF16)
        lam_ref[...] = _dot(dyp, wct_ref[...])

        def tail(r0, lr, li, carry):
            er, ei, dar, dai = carry
            xr = xs_ref[pl.ds(r0, SUBLANES), 0:CH_S]
            xi = xs_ref[pl.ds(r0, SUBLANES), CH_S:2 * CH_S]
            return lr, li, dar + (xr * er + xi * ei), dai + (xr * ei - xi * er)

        fix, (gr, gi) = _chunk_scan(lam_ref, tab_ref, carry_ref, length, reverse=True, tail=tail)
        zero = jnp.zeros((SUBLANES, CH_S), F32)
        _, _, dar, dai = lax.fori_loop(0, length, fix, (gr, gi, zero, zero), unroll=2)
        da_ref[:, 0:CH_S] += dar
        da_ref[:, CH_S:2 * CH_S] += dai
        lamb = lam_ref[...].astype(BF16)
        du_ref[...] = _time_order(_dot(lamb, wbt_ref[...]))
        dwc_ref[...] += _dot_tn(xs_ref[...].astype(BF16), dyp)
        dwb_ref[...] += _dot_tn(up, lamb)

    rev = lambda j, i: (nt - 1 - i, j)
    return pl.pallas_call(
        body, name="ssm_scan_bwd", grid=(SSM_CHUNKS, nt),
        in_specs=[pl.BlockSpec((tm, CH_W), rev),
                  pl.BlockSpec((None, tm, 2 * CH_S), lambda j, i: (j, nt - 1 - i, 0)),
                  pl.BlockSpec((tm, CH_W), rev),
                  pl.BlockSpec((None, CH_W, 2 * CH_S), lambda j, i: (j, 0, 0)),
                  pl.BlockSpec((None, length, 2 * CH_S), lambda j, i: (j, 0, 0)),
                  pl.BlockSpec((None, 2 * CH_S, CH_W), lambda j, i: (j, 0, 0))],
        out_specs=[pl.BlockSpec((tm, CH_W), rev),
                   pl.BlockSpec((None, 2 * CH_S, CH_W), lambda j, i: (j, 0, 0)),
                   pl.BlockSpec((None, CH_W, 2 * CH_S), lambda j, i: (j, 0, 0)),
                   pl.BlockSpec((None, SUBLANES, 2 * CH_S), lambda j, i: (j, 0, 0))],
        out_shape=[jax.ShapeDtypeStruct((s, SSM_WIDTH), F32),
                   jax.ShapeDtypeStruct((SSM_CHUNKS, 2 * CH_S, CH_W), F32),
                   jax.ShapeDtypeStruct((SSM_CHUNKS, CH_W, 2 * CH_S), F32),
                   jax.ShapeDtypeStruct((SSM_CHUNKS, SUBLANES, 2 * CH_S), F32)],
        scratch_shapes=[pltpu.VMEM((tm, 2 * CH_S), F32), pltpu.VMEM((SUBLANES, 2 * CH_S), F32)],
        compiler_params=_cparams(2),
    )(dyf, xs, proj, wct, tab_rev, wbt)


def _in_proj_bwd(h, g1, w_in_l, qg, kg, proj, du_a, du_b, dgs, dq, dk, dv, dga, dh1):
    s = h.shape[0]
    tm = _row_tile(s, 256)

    def body(h_ref, g_ref, w_ref, qg_ref, kg_ref, ones_ref, q_ref, k_ref, dua_ref, dub_ref, dgs_ref, dq_ref, dk_ref,
             dv_ref, dga_ref, dh1_ref, dh_ref, hn_ref, dp_ref, dg1_ref, dqg_ref, dkg_ref):
        @pl.when(pl.program_id(0) == 0)
        def _():
            dg1_ref[...] = jnp.zeros_like(dg1_ref)
            dqg_ref[...] = jnp.zeros_like(dqg_ref)
            dkg_ref[...] = jnp.zeros_like(dkg_ref)

        ones = ones_ref[...]

        def head_norm_bwd(x, gain, dy):
            r = lax.rsqrt(_dot_hilo(x * x, ones) + RMS_EPS)
            gdy = gain * dy
            dx = r * gdy - x * (r * r * r) * _dot_hilo(x * gdy, ones)
            return dx, x * r * dy

        dqr, dqg_rows = head_norm_bwd(q_ref[...], qg_ref[...], dq_ref[...] * ATTN_SCALE)
        dkr, dkg_rows = head_norm_bwd(k_ref[...], kg_ref[...], dk_ref[...])
        dqg_ref[...] += _colsum8(dqg_rows)
        dkg_ref[...] += _colsum8(dkg_rows)
        dp_ref[:, 0:512] = (dua_ref[...] + dub_ref[...]).astype(BF16)
        dp_ref[:, 512:1024] = dgs_ref[...].astype(BF16)
        dp_ref[:, 1024:1536] = dqr.astype(BF16)
        dp_ref[:, 1536:2048] = dkr.astype(BF16)
        dp_ref[:, 2048:2560] = dv_ref[...].astype(BF16)
        dp_ref[:, 2560:3072] = dga_ref[...].astype(BF16)
        dhn = _dot_nt(dp_ref[:, 0:IN_SHARD], w_ref[0])
        for sh in range(1, N_CHIPS):
            dhn = dhn + _dot_nt(dp_ref[:, IN_SHARD * sh:IN_SHARD * (sh + 1)], w_ref[sh])
        x = h_ref[...]
        gv = g_ref[...]
        r, hn = _rms_rows(x, gv)
        dx, dg_rows = _rms_bwd(x, r, gv, dhn)
        dh_ref[...] = dh1_ref[...] + dx
        hn_ref[...] = hn.astype(BF16)
        dg1_ref[...] += _colsum8(dg_rows)

    row = lambda i: (i, 0)
    full = lambda shape: pl.BlockSpec(shape, lambda i: (0,) * len(shape))
    big = pl.BlockSpec((tm, D_MODEL), row)
    half = pl.BlockSpec((tm, 512), row)
    return pl.pallas_call(
        body, name="in_proj_bwd", grid=(s // tm,),
        in_specs=[big, full((1, D_MODEL)), full((N_CHIPS, D_MODEL, IN_SHARD)),
                  full((1, ATTN_WIDTH)), full((1, ATTN_WIDTH)), full((ATTN_WIDTH, ATTN_WIDTH)),
                  pl.BlockSpec((tm, 512), lambda i: (i, 2)), pl.BlockSpec((tm, 512), lambda i: (i, 3)),
                  half, half, half, half, half, half, half, big],
        out_specs=[big, big, pl.BlockSpec((tm, IN_COLS), row), pl.BlockSpec((SUBLANES, D_MODEL), lambda i: (0, 0)),
                   pl.BlockSpec((SUBLANES, ATTN_WIDTH), lambda i: (0, 0)), pl.BlockSpec((SUBLANES, ATTN_WIDTH), lambda i: (0, 0))],
        out_shape=[jax.ShapeDtypeStruct((s, D_MODEL), F32), jax.ShapeDtypeStruct((s, D_MODEL), BF16),
                   jax.ShapeDtypeStruct((s, IN_COLS), BF16), jax.ShapeDtypeStruct((SUBLANES, D_MODEL), F32),
                   jax.ShapeDtypeStruct((SUBLANES, ATTN_WIDTH), F32), jax.ShapeDtypeStruct((SUBLANES, ATTN_WIDTH), F32)],
        compiler_params=_cparams(1),
    )(h, g1, w_in_l, qg, kg, _head_ones(), proj, proj, du_a, du_b, dgs, dq, dk, dv, dga, dh1)


SMALL_NAMES = ("mix_norm_g", "ssm_a_re", "ssm_a_im", "ssm_log_dt", "ssm_b_re", "ssm_b_im", "ssm_c_re", "ssm_c_im",
               "ssm_d", "ssm_b_glu", "q_norm_g", "k_norm_g", "ple_norm_g")
SMALL_4D = ("ssm_b_re", "ssm_b_im", "ssm_c_re", "ssm_c_im")
BIG_NAMES = ("w_in", "ssm_w_glu", "w_out", "w_ple_gate", "w_ple_proj")


def _ssm_setup(sm, layer, length):
    col = lambda a: a[layer].reshape(1, N_STATES)
    a_re, a_im = col(sm["ssm_a_re"]), col(sm["ssm_a_im"])
    log_dt = jnp.repeat(sm["ssm_log_dt"][layer], SSM_STATE).reshape(1, N_STATES)
    b_re = sm["ssm_b_re"][layer].reshape(N_STATES, SSM_GROUP).T
    b_im = sm["ssm_b_im"][layer].reshape(N_STATES, SSM_GROUP).T
    by_channel = lambda c: c[layer].transpose(1, 0, 2).reshape(SSM_GROUP, N_STATES)
    disc_in = (a_re, a_im, log_dt, b_re, b_im)
    wb, wbt, wct, wc, tab, tab_rev = _disc_fwd(*disc_in, by_channel(sm["ssm_c_re"]), by_channel(sm["ssm_c_im"]), length)
    return dict(disc_in=disc_in, wb=wb, wbt=wbt, wc=wc, wct=wct, tab=tab, tab_rev=tab_rev)


def _whole_blocks(names, gathered):
    return {n: g.reshape(N_CHIPS, 2 * g.shape[2], g.shape[3]) for n, g in zip(names, gathered)}


def _local_step(x, p, target, sm, w_in0, local=None, gathered=None, layer1_hook=None):
    wg = [dict(w_in=w_in0), {}] if gathered is None else gathered
    tile8 = lambda a: jnp.tile(a, ATTN_WIDTH // HEAD_DIM).reshape(1, ATTN_WIDTH)
    saved = []
    h = x
    for l in range(N_LAYERS):
        ssm = _ssm_setup(sm, l, _row_tile(x.shape[0], SCAN_TILE) // SUBLANES)
        g1 = sm["mix_norm_g"][l].reshape(1, D_MODEL)
        g2 = sm["ple_norm_g"][l].reshape(1, D_MODEL)
        qg, kg = tile8(sm["q_norm_g"][l]), tile8(sm["k_norm_g"][l])
        dsk = sm["ssm_d"][l].reshape(1, SSM_WIDTH)
        bgl = sm["ssm_b_glu"][l].reshape(1, 2 * SSM_WIDTH)
        proj, qkv = _in_proj(h, g1, wg[l]["w_in"], qg, kg)
        if l == 0 and local is not None:
            rest = BIG_NAMES[1:]
            xs, y, got = _ssm_scan_fwd(proj, ssm["wb"], ssm["tab"], ssm["wc"], [local[n] for n in rest], [0] * len(rest))
            wg[0].update(_whole_blocks(rest, got))
            ys = _ssm_glu_fwd(y, proj, dsk, wg[0]["ssm_w_glu"], bgl)
            o, ya, got = _attn_fwd(qkv, proj, [local[n] for n in BIG_NAMES], [2] * len(BIG_NAMES))
            wg[1].update(_whole_blocks(BIG_NAMES, got))
        else:
            xs, y, _ = _ssm_scan_fwd(proj, ssm["wb"], ssm["tab"], ssm["wc"])
            ys = _ssm_glu_fwd(y, proj, dsk, wg[l]["ssm_w_glu"], bgl)
            o, ya, _ = _attn_fwd(qkv, proj)
        tail = (target,) if l == N_LAYERS - 1 else ()
        h1, h2, *sq = _out_ple(h, ys, ya, p[l], g2, wg[l]["w_out"], wg[l]["w_ple_gate"], wg[l]["w_ple_proj"], *tail)
        saved.append(dict(ssm=ssm, g1=g1, g2=g2, qg=qg, kg=kg, dsk=dsk, bgl=bgl, h=h, proj=proj, qkv=qkv, xs=xs, y=y,
                          ys=ys, o=o, ya=ya, h1=h1))
        h = h2
    dh = h
    loss = 0.5 * jnp.sum(sq[0]) / D_MODEL

    gbig = [{} for _ in range(N_LAYERS)]
    scattered = ([], [])
    gsm = {n: [None] * N_LAYERS for n in SMALL_NAMES}
    for l in reversed(range(N_LAYERS)):
        sv = saved[l]
        ssm = sv["ssm"]
        dh1, dmix, hn2b, dgpb, dppb, dh1b, dg2 = _out_ple_bwd(dh, sv["h1"], p[l], sv["g2"], wg[l]["w_out"],
                                                              wg[l]["w_ple_gate"], wg[l]["w_ple_proj"])
        gsm["ple_norm_g"][l] = dg2.sum(0)
        gbig[l]["w_ple_proj"] = _tn_matmul(p[l], dppb, N_CHIPS, False, "dw_ple_proj")
        gbig[l]["w_ple_gate"] = _tn_matmul(hn2b, dgpb, N_CHIPS, True, "dw_ple_gate")
        dwo = _tn_matmul(sv["ys"], dh1b, 2, True, "dw_out_ssm", None, 0, N_CHIPS)
        gbig[l]["w_out"] = _tn_matmul(sv["ya"], dh1b, 2, True, "dw_out_attn", dwo, 2, N_CHIPS)
        if l == 0 and layer1_hook is not None:
            chip1 = layer1_hook(gbig[1])
            dqs, dkn, dv, dga, got = _attn_bwd(sv["qkv"], sv["o"], sv["proj"], dmix, chip1)
            scattered = (chip1, got)
        else:
            dqs, dkn, dv, dga, _ = _attn_bwd(sv["qkv"], sv["o"], sv["proj"], dmix)
        dyf, du_a, dgs, zb, dzzb, dd, dbg = _ssm_glu_bwd(dmix, sv["y"], sv["proj"], sv["dsk"], wg[l]["ssm_w_glu"], sv["bgl"])
        gsm["ssm_d"][l] = dd.sum(0).reshape(SSM_GROUPS, SSM_GROUP)
        gsm["ssm_b_glu"][l] = dbg.sum(0)
        gbig[l]["ssm_w_glu"] = _tn_matmul(zb, dzzb, N_CHIPS, False, "dw_glu")
        du_b, dwc, dwb, da = _ssm_scan_bwd(dyf, sv["xs"], sv["proj"], ssm["wct"], ssm["tab_rev"], ssm["wbt"])
        d_are, d_aim, d_ldt, d_bre, d_bim, d_cre, d_cim = _disc_bwd(*ssm["disc_in"], da, dwb, dwc)
        by_group = lambda t: t.reshape(SSM_GROUP, SSM_GROUPS, SSM_STATE).transpose(1, 0, 2)
        gsm["ssm_c_re"][l] = by_group(d_cre)
        gsm["ssm_c_im"][l] = by_group(d_cim)
        gsm["ssm_a_re"][l] = d_are.reshape(SSM_GROUPS, SSM_STATE)
        gsm["ssm_a_im"][l] = d_aim.reshape(SSM_GROUPS, SSM_STATE)
        gsm["ssm_log_dt"][l] = d_ldt.reshape(SSM_GROUPS, SSM_STATE).sum(1)
        gsm["ssm_b_re"][l] = d_bre.T.reshape(SSM_GROUPS, SSM_STATE, SSM_GROUP)
        gsm["ssm_b_im"][l] = d_bim.T.reshape(SSM_GROUPS, SSM_STATE, SSM_GROUP)
        dh, hnb, dprojb, dg1, dqg, dkg = _in_proj_bwd(sv["h"], sv["g1"], wg[l]["w_in"], sv["qg"], sv["kg"], sv["proj"],
                                                      du_a, du_b, dgs, dqs, dkn, dv, dga, dh1)
        gsm["mix_norm_g"][l] = dg1.sum(0)
        gsm["q_norm_g"][l] = dqg.sum(0).reshape(-1, HEAD_DIM).sum(0)
        gsm["k_norm_g"][l] = dkg.sum(0).reshape(-1, HEAD_DIM).sum(0)
        gbig[l]["w_in"] = _tn_matmul(hnb, dprojb, N_CHIPS, False, "dw_in")
    gsm = {n: jnp.stack(v, 0) for n, v in gsm.items()}
    return loss, dh, gbig, gsm, scattered


_SMALL_PAD = 8 * 8 * 128


def _pack_small(d, extra):
    flat = jnp.concatenate([d[n].reshape(-1) for n in SMALL_NAMES] + [jnp.stack(extra)])
    n = flat.shape[0]
    padded = -(-n // _SMALL_PAD) * _SMALL_PAD
    return jnp.pad(flat, (0, padded - n))


def _unpack_small(flat, like):
    out, off = {}, 0
    for n in SMALL_NAMES:
        size = like[n].size
        out[n] = flat[off:off + size].reshape(like[n].shape)
        off += size
    return out, flat[off:]


def _half_views(arrs):
    return [a.reshape(a.shape[0], 2, a.shape[1] // 2, a.shape[2]) for a in arrs]


def _chip_sums(views, out_dtypes, tag):
    recv = _sibling_push(views, "grad_push_" + tag)
    return [_add_my_half(v, r, dt, "grad_half_add") for v, r, dt in zip(views, recv, out_dtypes)]


def kernel(x, p, mix_norm_g, w_in, ssm_a_re, ssm_a_im, ssm_log_dt, ssm_b_re, ssm_b_im, ssm_c_re, ssm_c_im, ssm_d, ssm_w_glu, ssm_b_glu, q_norm_g, k_norm_g, w_out, ple_norm_g, w_ple_gate, w_ple_proj, loss_target, m_mix_norm_g, m_w_in, m_ssm_a_re, m_ssm_a_im, m_ssm_log_dt, m_ssm_b_re, m_ssm_b_im, m_ssm_c_re, m_ssm_c_im, m_ssm_d, m_ssm_w_glu, m_ssm_b_glu, m_q_norm_g, m_k_norm_g, m_w_out, m_ple_norm_g, m_w_ple_gate, m_w_ple_proj, v_mix_norm_g, v_w_in, v_ssm_a_re, v_ssm_a_im, v_ssm_log_dt, v_ssm_b_re, v_ssm_b_im, v_ssm_c_re, v_ssm_c_im, v_ssm_d, v_ssm_w_glu, v_ssm_b_glu, v_q_norm_g, v_k_norm_g, v_w_out, v_ple_norm_g, v_w_ple_gate, v_w_ple_proj):
    args = dict(locals())
    names = ("mix_norm_g", "w_in", "ssm_a_re", "ssm_a_im", "ssm_log_dt", "ssm_b_re", "ssm_b_im", "ssm_c_re", "ssm_c_im",
             "ssm_d", "ssm_w_glu", "ssm_b_glu", "q_norm_g", "k_norm_g", "w_out", "ple_norm_g", "w_ple_gate", "w_ple_proj")
    w = {n: args[n] for n in names}
    m = {n: args["m_" + n] for n in names}
    v = {n: args["v_" + n] for n in names}

    local = {n: w[n].astype(BF16).reshape(2 * N_LAYERS, w[n].shape[1] // 2, w[n].shape[2]) for n in BIG_NAMES}
    w_in0 = _chip_gather([local["w_in"]], "w_in_gather")[0].reshape(N_CHIPS, D_MODEL, IN_SHARD)
    sm = {n: w[n] for n in SMALL_NAMES}
    nb = len(BIG_NAMES)
    loss, dx, gbig, gsm, (chip1, got1) = _local_step(
        x[0], p[:, 0], loss_target[0], sm, w_in0, local,
        layer1_hook=lambda g1: _chip_sums(_half_views([g1[n] for n in BIG_NAMES]), [BF16] * nb, "layer1"))

    small = _pack_small(gsm, [loss]).reshape(N_CHIPS, 2, SUBLANES, -1)
    chip0 = _chip_sums(_half_views([gbig[0][n] for n in BIG_NAMES]) + [small], [BF16] * nb + [F32], "layer0")
    got0 = _chip_scatter(chip0, "grad_chip_scatter")
    tot1 = [_sum4(a, own, "grad_chip_sum") for a, own in zip(got1, chip1)]
    tot0 = [_sum4(a, own, "grad_chip_sum") for a, own in zip(got0, chip0)]
    pieces = [(t, k, (l,)) for l, tots in enumerate((tot0[:nb], tot1)) for k, t in enumerate(tots)] + [(tot0[nb], nb, ())]
    joined = _sibling_join(pieces, [(N_LAYERS, 2) + t.shape for t in tot1] + [(2,) + tot0[nb].shape], "grad_sibling_join")
    small_all = _chip_gather([joined[nb]], "small_grad_gather")[0]
    small_tot = small_all.reshape(-1)
    g = {n: j.reshape(w[n].shape) for n, j in zip(BIG_NAMES, joined)}
    g_small, rest = _unpack_small(small_tot, sm)
    g.update(g_small)
    loss = rest[0]

    delta, new_m, new_v = {}, {}, {}
    for n in BIG_NAMES:
        lanes = w[n].shape[-1]
        outs = _adamw(_as_rows(w[n], lanes), _as_rows(g[n], lanes), _as_rows(m[n], lanes), _as_rows(v[n], lanes), "adamw_" + n)
        delta[n], new_m[n], new_v[n] = [o.reshape(w[n].shape) for o in outs]
    swap = lambda n, a: jnp.swapaxes(a, -1, -2) if n in ("ssm_b_re", "ssm_b_im") else a
    for group, per_layer in ((SMALL_4D, True), (tuple(n for n in SMALL_NAMES if n not in SMALL_4D), False)):
        outs = _adamw_many(*[[swap(n, d[n]) for n in group] for d in (w, g, m, v)],
                           "adamw_small_4d" if per_layer else "adamw_small", per_layer)
        for d, o in zip((delta, new_m, new_v), outs):
            d.update({n: swap(n, a) for n, a in zip(group, o)})

    return (loss, dx[None], *[g[n] for n in names], *[delta[n] for n in names],
            *[new_m[n] for n in names], *[new_v[n] for n in names])
```

```python
import functools
import math

import jax
import jax.numpy as jnp
from jax import lax
from jax.experimental import pallas as pl
from jax.experimental.pallas import tpu as pltpu

F32 = jnp.float32
BF16 = jnp.bfloat16

D_MODEL = 1024
N_LAYERS = 2
N_CHIPS = 4
IN_COLS = 3072
IN_SHARD = IN_COLS // N_CHIPS
SSM_WIDTH = 512
SSM_GROUP = 16
SSM_GROUPS = 32
SSM_STATE = 64
N_STATES = SSM_GROUPS * SSM_STATE
SSM_CHUNKS = 4
CH_W = SSM_WIDTH // SSM_CHUNKS
CH_S = N_STATES // SSM_CHUNKS
ATTN_WIDTH = 512
HEAD_DIM = 64
PLE_DIM = 256
ROW_SHARD = 256
RMS_EPS = 1e-6
ATTN_SCALE = HEAD_DIM ** -0.5
ATTN_BLOCK = 128
EXP_ZERO = -87.5
SUBLANES = 8
SCAN_TILE = 1024
V7X_VMEM_LIMIT = 32 * 1024 * 1024
V7X_VMEM_LIMIT_GLU_BWD = 52 * 1024 * 1024
V7X_VMEM_LIMIT_ATTN_BWD = 60 * 1024 * 1024

ADAM_LR = 0.001
ADAM_B1 = 0.9
ADAM_B2 = 0.999
ADAM_EPS = 1e-08
ADAM_WD = 0.01
ADAM_STEP = 10

MESH = pl.DeviceIdType.MESH
ANY = pl.BlockSpec(memory_space=pl.ANY)


def _cparams(n_grid=0, parallel=0, vmem_limit=V7X_VMEM_LIMIT):
    sem = tuple(["parallel"] * parallel + ["arbitrary"] * (n_grid - parallel))
    return pltpu.CompilerParams(dimension_semantics=sem, vmem_limit_bytes=vmem_limit)


def _dot(a, b):
    return jnp.dot(a, b, preferred_element_type=F32)


def _dot_nt(a, b):
    return lax.dot_general(a, b, (((1,), (1,)), ((), ())), preferred_element_type=F32)


def _dot_tn(a, b):
    return lax.dot_general(a, b, (((0,), (0,)), ((), ())), preferred_element_type=F32)


def _split_hilo(a):
    hi = a.astype(BF16)
    lo = (a - hi.astype(F32)).astype(BF16)
    return hi, lo


def _dot_hilo(a, b):
    hi, lo = _split_hilo(a)
    return _dot(hi, b) + _dot(lo, b)


def _sigmoid(x):
    return 0.5 * (jnp.tanh(0.5 * x) + 1.0)


_GELU_C = math.sqrt(2.0 / math.pi)


def _gelu(x):
    return 0.5 * x * (1.0 + jnp.tanh(_GELU_C * (x + 0.044715 * (x * x * x))))


def _gelu_grad(x):
    t = jnp.tanh(_GELU_C * (x + 0.044715 * (x * x * x)))
    return 0.5 * (1.0 + t) + 0.5 * x * (1.0 - t * t) * (_GELU_C * (1.0 + 3.0 * 0.044715 * (x * x)))


def _row_tile(s, want):
    for t in range(min(s, want), 7, -1):
        if s % t == 0 and t % SUBLANES == 0:
            return t
    return s


def _coords():
    return lax.axis_index("x"), lax.axis_index("y"), lax.axis_index("c")


def _other_chips(x, y):
    return [(1 - x, y), (x, 1 - y), (1 - x, 1 - y)]


def _remote(src, dst, send_sem, recv_sem, dev):
    return pltpu.make_async_remote_copy(src_ref=src, dst_ref=dst, send_sem=send_sem, recv_sem=recv_sem,
                                        device_id=dev, device_id_type=MESH)


def _set_block(buf, block, index):
    return lax.dynamic_update_index_in_dim(buf, block, index, 0)


def _gather_sems(n):
    return [pltpu.SemaphoreType.DMA((3 * n,)) for _ in range(4)]


def _gather_copies(ins, bases, outs, sems):
    send_sems, recv_sems, fwd_send, fwd_recv = sems
    x, y, c = _coords()
    me_chip = 2 * x + y
    sibling = (x, y, 1 - c)
    first, landed, passed, from_sibling = [], [], [], []
    for k in range(len(ins)):
        for j, (cx, cy) in enumerate(_other_chips(x, y)):
            i = 3 * k + j
            first.append(_remote(ins[k].at[bases[k] + c], outs[k].at[me_chip, c], send_sems.at[i], recv_sems.at[i], (cx, cy, c)))
            blk = outs[k].at[2 * cx + cy, c]
            landed.append(_remote(blk, blk, send_sems.at[i], recv_sems.at[i], (cx, cy, c)))
            passed.append(_remote(blk, blk, fwd_send.at[i], fwd_recv.at[i], sibling))
            blk = outs[k].at[2 * cx + cy, 1 - c]
            from_sibling.append(_remote(blk, blk, fwd_send.at[i], fwd_recv.at[i], sibling))
    return first, landed, passed, from_sibling


def _gather_start(ins, bases, outs, sems):
    for cp in _gather_copies(ins, bases, outs, sems)[0]:
        cp.start()


def _gather_finish(ins, bases, outs, sems):
    first, landed, passed, from_sibling = _gather_copies(ins, bases, outs, sems)
    for arrived, forward in zip(landed, passed):
        arrived.wait_recv()
        forward.start()
    for cp in from_sibling:
        cp.wait_recv()
    for cp in first + passed:
        cp.wait_send()


def _gather_outputs(arrs):
    return [jax.ShapeDtypeStruct((N_CHIPS, 2) + a.shape[1:], a.dtype) for a in arrs]


def _gather_own(outs, arrs, bases):
    me_chip = 2 * lax.axis_index("x") + lax.axis_index("y")
    return [_set_block(o, lax.slice_in_dim(a, b, b + 2, axis=0), me_chip) for o, a, b in zip(outs, arrs, bases)]


def _chip_gather(arrs, name, bases=None):
    n = len(arrs)
    bases = [0] * n if bases is None else bases

    def body(*refs):
        ins, outs, sems = refs[:n], refs[n:2 * n], refs[2 * n:]
        _gather_start(ins, bases, outs, sems)
        _gather_finish(ins, bases, outs, sems)

    outs = pl.pallas_call(
        body, name=name, out_shape=_gather_outputs(arrs),
        in_specs=[ANY] * n, out_specs=[ANY] * n, scratch_shapes=_gather_sems(n),
    )(*arrs)
    return _gather_own(outs, arrs, bases)


def _sibling_push(arrs, name):
    n = len(arrs)

    def body(*refs):
        ins, outs = refs[:n], refs[n:2 * n]
        send_sems, recv_sems = refs[2 * n:]
        x, y, c = _coords()
        cps = [_remote(ins[k].at[pl.ds(0, N_CHIPS), 1 - c], outs[k], send_sems.at[k], recv_sems.at[k], (x, y, 1 - c))
               for k in range(n)]
        for cp in cps:
            cp.start()
        for cp in cps:
            cp.wait_recv()
        for cp in cps:
            cp.wait_send()

    return pl.pallas_call(
        body, name=name,
        out_shape=[jax.ShapeDtypeStruct((a.shape[0],) + a.shape[2:], a.dtype) for a in arrs],
        in_specs=[ANY] * n, out_specs=[ANY] * n,
        scratch_shapes=[pltpu.SemaphoreType.DMA((n,)), pltpu.SemaphoreType.DMA((n,))],
    )(*arrs)


def _sibling_join(pieces, out_shapes, name):
    n = len(pieces)
    no = len(out_shapes)

    def body(*refs):
        ins, outs = refs[:n], refs[n:n + no]
        send_sems, recv_sems = refs[n + no:]
        x, y, c = _coords()
        sibling = (x, y, 1 - c)
        cps = [_remote(ins[k], outs[o].at[lead + (c,)], send_sems.at[k], recv_sems.at[k], sibling)
               for k, (_, o, lead) in enumerate(pieces)]
        for cp in cps:
            cp.start()
        for k, (_, o, lead) in enumerate(pieces):
            blk = outs[o].at[lead + (1 - c,)]
            _remote(blk, blk, send_sems.at[k], recv_sems.at[k], sibling).wait_recv()
        for cp in cps:
            cp.wait_send()

    outs = pl.pallas_call(
        body, name=name,
        out_shape=[jax.ShapeDtypeStruct(sh, F32) for sh in out_shapes],
        in_specs=[ANY] * n, out_specs=[ANY] * no,
        scratch_shapes=[pltpu.SemaphoreType.DMA((n,)), pltpu.SemaphoreType.DMA((n,))],
    )(*[a for a, _, _ in pieces])
    outs = list(outs)
    c = lax.axis_index("c")
    for a, o, lead in pieces:
        block = a.reshape((1,) * (len(lead) + 1) + a.shape)
        outs[o] = lax.dynamic_update_slice(outs[o], block, lead + (c,) + (0,) * a.ndim)
    return outs


def _scatter_sems(n):
    return [pltpu.SemaphoreType.DMA((3 * n,)), pltpu.SemaphoreType.DMA((3 * n,))]


def _scatter_copies(ins, outs, sems):
    send_sems, recv_sems = sems
    x, y, c = _coords()
    me_chip = 2 * x + y
    sends, arrivals = [], []
    for k in range(len(ins)):
        for j, (cx, cy) in enumerate(_other_chips(x, y)):
            i = 3 * k + j
            sends.append(_remote(ins[k].at[2 * cx + cy], outs[k].at[me_chip], send_sems.at[i], recv_sems.at[i], (cx, cy, c)))
            blk = outs[k].at[2 * cx + cy]
            arrivals.append(_remote(blk, blk, send_sems.at[i], recv_sems.at[i], (cx, cy, c)))
    return sends, arrivals


def _scatter_start(ins, outs, sems):
    for cp in _scatter_copies(ins, outs, sems)[0]:
        cp.start()


def _scatter_finish(ins, outs, sems):
    sends, arrivals = _scatter_copies(ins, outs, sems)
    for cp in arrivals:
        cp.wait_recv()
    for cp in sends:
        cp.wait_send()


def _chip_scatter(arrs, name):
    n = len(arrs)

    def body(*refs):
        ins, outs, sems = refs[:n], refs[n:2 * n], refs[2 * n:]
        _scatter_start(ins, outs, sems)
        _scatter_finish(ins, outs, sems)

    outs = pl.pallas_call(
        body, name=name,
        out_shape=[jax.ShapeDtypeStruct(a.shape, a.dtype) for a in arrs],
        in_specs=[ANY] * n, out_specs=[ANY] * n, scratch_shapes=_scatter_sems(n),
    )(*arrs)
    return outs


def _as_rows(a, lanes):
    return a.reshape(-1, lanes)


def _add_my_half(v, recv, out_dtype, name):
    n_sh, _, h, cdim = v.shape
    tr = _row_tile(h, 512)

    def body(c_ref, a_ref, b_ref, o_ref):
        o_ref[...] = (a_ref[...].astype(F32) + b_ref[...].astype(F32)).astype(out_dtype)

    c = lax.axis_index("c").astype(jnp.int32).reshape(1)
    return pl.pallas_call(
        body, name=name,
        grid_spec=pltpu.PrefetchScalarGridSpec(
            num_scalar_prefetch=1, grid=(n_sh, h // tr),
            in_specs=[pl.BlockSpec((None, None, tr, cdim), lambda sh, i, c_ref: (sh, c_ref[0], i, 0)),
                      pl.BlockSpec((None, tr, cdim), lambda sh, i, c_ref: (sh, i, 0))],
            out_specs=pl.BlockSpec((None, tr, cdim), lambda sh, i, c_ref: (sh, i, 0))),
        out_shape=jax.ShapeDtypeStruct((n_sh, h, cdim), out_dtype),
        compiler_params=_cparams(2),
    )(c, v, recv)


def _sum4(got, own, name):
    _, r, cdim = got.shape
    tr = _row_tile(r, 512)

    def body(me_ref, p_ref, own_ref, o_ref):
        mine = own_ref[...].astype(F32)
        acc = None
        for j in range(N_CHIPS):
            term = jnp.where(me_ref[0] == j, mine, p_ref[j].astype(F32))
            acc = term if acc is None else acc + term
        o_ref[...] = acc

    me = (2 * lax.axis_index("x") + lax.axis_index("y")).astype(jnp.int32).reshape(1)
    return pl.pallas_call(
        body, name=name,
        grid_spec=pltpu.PrefetchScalarGridSpec(
            num_scalar_prefetch=1, grid=(r // tr,),
            in_specs=[pl.BlockSpec((N_CHIPS, tr, cdim), lambda i, me_ref: (0, i, 0)),
                      pl.BlockSpec((None, tr, cdim), lambda i, me_ref: (me_ref[0], i, 0))],
            out_specs=pl.BlockSpec((tr, cdim), lambda i, me_ref: (i, 0))),
        out_shape=jax.ShapeDtypeStruct((r, cdim), F32),
        compiler_params=_cparams(1),
    )(me, got, own)


def _adamw_math(w, g, m, v):
    c1 = 1.0 - ADAM_B1 ** ADAM_STEP
    c2 = 1.0 - ADAM_B2 ** ADAM_STEP
    nm = ADAM_B1 * m + (1.0 - ADAM_B1) * g
    nv = ADAM_B2 * v + (1.0 - ADAM_B2) * (g * g)
    delta = -ADAM_LR * ((nm / c1) / (jnp.sqrt(nv / c2) + ADAM_EPS) + ADAM_WD * w)
    return delta, nm, nv


def _adamw(w, g, m, v, name):
    r, cdim = w.shape
    tr = _row_tile(r, 256)

    def body(w_ref, g_ref, m_ref, v_ref, d_ref, nm_ref, nv_ref):
        d_ref[...], nm_ref[...], nv_ref[...] = _adamw_math(w_ref[...], g_ref[...], m_ref[...], v_ref[...])

    spec = pl.BlockSpec((tr, cdim), lambda i: (i, 0))
    return pl.pallas_call(
        body, name=name, grid=(r // tr,),
        in_specs=[spec] * 4, out_specs=[spec] * 3,
        out_shape=[jax.ShapeDtypeStruct((r, cdim), F32)] * 3,
        compiler_params=_cparams(1),
    )(w, g, m, v)


def _adamw_many(ws, gs, ms, vs, name, per_layer):
    n = len(ws)

    def body(*refs):
        for k in range(n):
            w, g, m, v = (refs[j * n + k][...] for j in range(4))
            outs = _adamw_math(w, g, m, v)
            for j in range(3):
                refs[(4 + j) * n + k][...] = outs[j]

    shapes = [jax.ShapeDtypeStruct(w.shape, F32) for w in ws]
    if per_layer:
        specs = [pl.BlockSpec((None,) + w.shape[1:], lambda l, nd=w.ndim: (l,) + (0,) * (nd - 1)) for w in ws]
        call = pl.pallas_call(body, name=name, grid=(N_LAYERS,), in_specs=specs * 4, out_specs=specs * 3,
                              out_shape=shapes * 3, compiler_params=_cparams(1))
    else:
        call = pl.pallas_call(body, name=name, out_shape=shapes * 3, compiler_params=_cparams())
    outs = call(*ws, *gs, *ms, *vs)
    return outs[0:n], outs[n:2 * n], outs[2 * n:3 * n]


def _cmul(ar, ai, br, bi):
    return ar * br - ai * bi, ar * bi + ai * br


def _discretise(a_re, a_im, log_dt, b_re, b_im):
    dt = jnp.exp(log_dt)
    mag = jnp.exp(a_re * dt)
    ab_re = mag * jnp.cos(a_im * dt)
    ab_im = mag * jnp.sin(a_im * dt)
    num_re = ab_re - 1.0
    num_im = ab_im
    den = a_re * a_re + a_im * a_im
    f_re = (num_re * a_re + num_im * a_im) / den
    f_im = (num_im * a_re - num_re * a_im) / den
    bb_re = f_re * b_re - f_im * b_im
    bb_im = f_re * b_im + f_im * b_re
    return ab_re, ab_im, bb_re, bb_im


def _disc_shapes():
    col = jax.ShapeDtypeStruct((1, N_STATES), F32)
    mat = jax.ShapeDtypeStruct((SSM_GROUP, N_STATES), F32)
    return col, mat


def _group_mask():
    row = lax.broadcasted_iota(jnp.int32, (CH_W, CH_S), 0)
    col = lax.broadcasted_iota(jnp.int32, (CH_W, CH_S), 1)
    return jnp.right_shift(row, SSM_GROUP.bit_length() - 1) == jnp.right_shift(col, SSM_STATE.bit_length() - 1)


def _block_diag(v, j):
    blk = v[:, CH_S * j:CH_S * (j + 1)]
    return jnp.where(_group_mask(), jnp.concatenate([blk] * (CH_W // SSM_GROUP), axis=0), 0.0)


def _block_diag_t(m):
    kept = jnp.where(_group_mask(), m, 0.0)
    return kept.reshape(CH_W // SSM_GROUP, SSM_GROUP, CH_S).sum(axis=0)


def _disc_fwd(a_re, a_im, log_dt, b_re, b_im, c_re, c_im, length):
    wide = jax.ShapeDtypeStruct((SSM_CHUNKS, CH_W, 2 * CH_S), BF16)
    tall = jax.ShapeDtypeStruct((SSM_CHUNKS, 2 * CH_S, CH_W), BF16)
    tab = jax.ShapeDtypeStruct((SSM_CHUNKS, length, 2 * CH_S), F32)

    def body(ar, ai, ld, br, bi, cr, ci, wb_ref, wbt_ref, wct_ref, wc_ref, tab_ref, rev_ref):
        ab_re, ab_im, bb_re, bb_im = _discretise(ar[...], ai[...], ld[...], br[...], bi[...])
        ccr, cci = cr[...], -ci[...]
        for j in range(SSM_CHUNKS):
            for lo, (vb, vc) in ((0, (bb_re, ccr)), (CH_S, (bb_im, cci))):
                mb, mc = _block_diag(vb, j), _block_diag(vc, j)
                wb_ref[j, :, lo:lo + CH_S] = mb.astype(BF16)
                wbt_ref[j, lo:lo + CH_S, :] = mb.T.astype(BF16)
                wct_ref[j, :, lo:lo + CH_S] = mc.astype(BF16)
                wc_ref[j, lo:lo + CH_S, :] = mc.T.astype(BF16)

        def step(j, carry):
            pr, pi = carry
            back = length - 1 - j
            for c in range(SSM_CHUNKS):
                lanes = slice(CH_S * c, CH_S * (c + 1))
                tab_ref[c, pl.ds(j, 1), 0:CH_S] = pr[:, lanes]
                tab_ref[c, pl.ds(j, 1), CH_S:2 * CH_S] = pi[:, lanes]
                rev_ref[c, pl.ds(back, 1), 0:CH_S] = pr[:, lanes]
                rev_ref[c, pl.ds(back, 1), CH_S:2 * CH_S] = -pi[:, lanes]
            return _cmul(pr, pi, ab_re, ab_im)

        lax.fori_loop(0, length, step, (ab_re, ab_im))

    return pl.pallas_call(body, name="ssm_discretise", out_shape=[wide, tall, wide, tall, tab, tab],
                          compiler_params=_cparams())(a_re, a_im, log_dt, b_re, b_im, c_re, c_im)


def _disc_bwd(a_re, a_im, log_dt, b_re, b_im, da, dwb, dwc):
    col, mat = _disc_shapes()

    def body(ar, ai, ld, br, bi, da_ref, dwb_ref, dwc_ref, o0, o1, o2, o3, o4, dcr_ref, dci_ref):
        g_ab = [jnp.concatenate([jnp.sum(da_ref[j, :, lo:lo + CH_S], axis=0, keepdims=True) for j in range(SSM_CHUNKS)],
                                axis=-1) for lo in (0, CH_S)]
        g_bb = [jnp.concatenate([_block_diag_t(dwb_ref[j, :, lo:lo + CH_S]) for j in range(SSM_CHUNKS)], axis=-1)
                for lo in (0, CH_S)]
        for ref, lo, sign in ((dcr_ref, 0, 1.0), (dci_ref, CH_S, -1.0)):
            ref[...] = sign * jnp.concatenate([_block_diag_t(dwc_ref[j, lo:lo + CH_S, :].T) for j in range(SSM_CHUNKS)],
                                              axis=-1)
        _, vjp = jax.vjp(_discretise, ar[...], ai[...], ld[...], br[...], bi[...])
        grads = vjp((g_ab[0], g_ab[1], g_bb[0], g_bb[1]))
        for o, val in zip((o0, o1, o2, o3, o4), grads):
            o[...] = val

    return pl.pallas_call(body, name="ssm_discretise_bwd", out_shape=[col, col, col, mat, mat, mat, mat],
                          compiler_params=_cparams())(a_re, a_im, log_dt, b_re, b_im, da, dwb, dwc)


def _interleave_chunks(v):
    rows, width = v.shape
    return pltpu.einshape("cjw->jcw", v.reshape(SUBLANES, rows // SUBLANES, width)).reshape(rows, width)


def _time_order(v):
    rows, width = v.shape
    return pltpu.einshape("jcw->cjw", v.reshape(rows // SUBLANES, SUBLANES, width)).reshape(rows, width)


def _head_ones():
    r = jnp.arange(ATTN_WIDTH) // HEAD_DIM
    return jnp.where(r[:, None] == r[None, :], 1.0 / HEAD_DIM, 0.0).astype(BF16)


def _in_proj(h, g1, w_in_l, qg, kg):
    s = h.shape[0]
    tm = _row_tile(s, 512)

    def body(h_ref, g_ref, w_ref, qg_ref, kg_ref, ones_ref, proj_ref, qkv_ref):
        x = h_ref[...]
        r = lax.rsqrt(jnp.mean(x * x, axis=-1, keepdims=True) + RMS_EPS)
        hn = (x * r * g_ref[...]).astype(BF16)
        for sh in range(N_CHIPS):
            proj_ref[:, IN_SHARD * sh:IN_SHARD * (sh + 1)] = _dot(hn, w_ref[sh])
        ones = ones_ref[...]
        q = proj_ref[:, 1024:1536]
        k = proj_ref[:, 1536:2048]
        rq = lax.rsqrt(_dot_hilo(q * q, ones) + RMS_EPS)
        rk = lax.rsqrt(_dot_hilo(k * k, ones) + RMS_EPS)
        qkv_ref[:, 0:512] = (q * rq * qg_ref[...] * ATTN_SCALE).astype(BF16)
        qkv_ref[:, 512:1024] = (k * rk * kg_ref[...]).astype(BF16)
        qkv_ref[:, 1024:1536] = proj_ref[:, 2048:2560].astype(BF16)

    full = lambda shape: pl.BlockSpec(shape, lambda i: (0,) * len(shape))
    return pl.pallas_call(
        body, name="in_proj", grid=(s // tm,),
        in_specs=[pl.BlockSpec((tm, D_MODEL), lambda i: (i, 0)), full((1, D_MODEL)),
                  full((N_CHIPS, D_MODEL, IN_SHARD)),
                  full((1, ATTN_WIDTH)), full((1, ATTN_WIDTH)), full((ATTN_WIDTH, ATTN_WIDTH))],
        out_specs=[pl.BlockSpec((tm, IN_COLS), lambda i: (i, 0)), pl.BlockSpec((tm, 3 * ATTN_WIDTH), lambda i: (i, 0))],
        out_shape=[jax.ShapeDtypeStruct((s, IN_COLS), F32), jax.ShapeDtypeStruct((s, 3 * ATTN_WIDTH), BF16)],
        compiler_params=_cparams(1),
    )(h, g1, w_in_l, qg, kg, _head_ones())


def _row_bcast(ref, k, lo):
    return jnp.broadcast_to(ref[pl.ds(k, 1), lo:lo + CH_S], (SUBLANES, CH_S))


def _chunk_scan(x_ref, tab_ref, carry_ref, length, reverse, tail=None):
    row = lax.broadcasted_iota(jnp.int32, (SUBLANES, CH_S), 0)
    one, full = (length - 1, 0) if reverse else (0, length - 1)
    ar, ai = _row_bcast(tab_ref, one, 0), _row_bcast(tab_ref, one, CH_S)
    fr, fi = _row_bcast(tab_ref, full, 0), _row_bcast(tab_ref, full, CH_S)
    step = lambda jj: (length - 1 - jj) if reverse else jj

    def local(jj, carry):
        cr, ci = carry
        r0 = pl.multiple_of(step(jj) * SUBLANES, SUBLANES)
        xr = x_ref[pl.ds(r0, SUBLANES), 0:CH_S] + (ar * cr - ai * ci)
        xi = x_ref[pl.ds(r0, SUBLANES), CH_S:2 * CH_S] + (ar * ci + ai * cr)
        x_ref[pl.ds(r0, SUBLANES), 0:CH_S] = xr
        x_ref[pl.ds(r0, SUBLANES), CH_S:2 * CH_S] = xi
        return xr, xi

    zero = jnp.zeros((SUBLANES, CH_S), F32)
    er, ei = lax.fori_loop(0, length, local, (zero, zero))

    first, shift = (SUBLANES - 1, SUBLANES - 1) if reverse else (0, 1)
    hr = jnp.where(row == first, carry_ref[:, 0:CH_S], 0.0)
    hi = jnp.where(row == first, carry_ref[:, CH_S:2 * CH_S], 0.0)
    sr, si = pltpu.roll(er, shift, 0), pltpu.roll(ei, shift, 0)
    for k in range(1, SUBLANES):
        tr, ti = pltpu.roll(hr, shift, 0), pltpu.roll(hi, shift, 0)
        here = row == ((SUBLANES - 1 - k) if reverse else k)
        hr, hi = (jnp.where(here, fr * tr - fi * ti + sr, hr), jnp.where(here, fr * ti + fi * tr + si, hi))
    last = 0 if reverse else SUBLANES - 1
    outr, outi = fr * hr - fi * hi + er, fr * hi + fi * hr + ei
    carry_ref[:, 0:CH_S] = jnp.broadcast_to(outr[last:last + 1, :], (SUBLANES, CH_S))
    carry_ref[:, CH_S:2 * CH_S] = jnp.broadcast_to(outi[last:last + 1, :], (SUBLANES, CH_S))

    def fix(jj, carry):
        j = step(jj)
        r0 = pl.multiple_of(j * SUBLANES, SUBLANES)
        pr, pi = _row_bcast(tab_ref, j, 0), _row_bcast(tab_ref, j, CH_S)
        xr = x_ref[pl.ds(r0, SUBLANES), 0:CH_S] + (pr * hr - pi * hi)
        xi = x_ref[pl.ds(r0, SUBLANES), CH_S:2 * CH_S] + (pr * hi + pi * hr)
        x_ref[pl.ds(r0, SUBLANES), 0:CH_S] = xr
        x_ref[pl.ds(r0, SUBLANES), CH_S:2 * CH_S] = xi
        if tail is None:
            return carry
        return tail(r0, xr, xi, carry)

    return fix, (hr, hi)


def _ssm_scan_fwd(proj, wb, tab, wc, gather=None, gather_bases=None):
    s = proj.shape[0]
    tm = _row_tile(s, SCAN_TILE)
    nt = s // tm
    length = tm // SUBLANES
    gather = [] if gather is None else gather
    ng = len(gather)

    def body(*refs):
        u_ref, wb_ref, tab_ref, wc_ref = refs[0:4]
        g_ins = refs[4:4 + ng]
        xs_ref, y_ref = refs[4 + ng:6 + ng]
        g_outs = refs[6 + ng:6 + 2 * ng]
        carry_ref = refs[6 + 2 * ng]
        sems = refs[7 + 2 * ng:]
        j, i = pl.program_id(0), pl.program_id(1)

        @pl.when(i == 0)
        def _():
            carry_ref[...] = jnp.zeros_like(carry_ref)

        if ng:
            @pl.when(jnp.logical_and(j == 0, i == 0))
            def _():
                _gather_start(g_ins, gather_bases, g_outs, sems)

        xs_ref[...] = _dot(_interleave_chunks(u_ref[...]).astype(BF16), wb_ref[...])
        fix, start = _chunk_scan(xs_ref, tab_ref, carry_ref, length, reverse=False)
        lax.fori_loop(0, length, fix, start, unroll=2)
        y_ref[...] = _time_order(_dot(xs_ref[...].astype(BF16), wc_ref[...]))

        if ng:
            @pl.when(jnp.logical_and(j == SSM_CHUNKS - 1, i == nt - 1))
            def _():
                _gather_finish(g_ins, gather_bases, g_outs, sems)

    outs = pl.pallas_call(
        body, name="ssm_scan_gather" if ng else "ssm_scan", grid=(SSM_CHUNKS, nt),
        in_specs=[pl.BlockSpec((tm, CH_W), lambda j, i: (i, j)),
                  pl.BlockSpec((None, CH_W, 2 * CH_S), lambda j, i: (j, 0, 0)),
                  pl.BlockSpec((None, length, 2 * CH_S), lambda j, i: (j, 0, 0)),
                  pl.BlockSpec((None, 2 * CH_S, CH_W), lambda j, i: (j, 0, 0))] + [ANY] * ng,
        out_specs=[pl.BlockSpec((None, tm, 2 * CH_S), lambda j, i: (j, i, 0)),
                   pl.BlockSpec((tm, CH_W), lambda j, i: (i, j))] + [ANY] * ng,
        out_shape=[jax.ShapeDtypeStruct((SSM_CHUNKS, s, 2 * CH_S), F32), jax.ShapeDtypeStruct((s, SSM_WIDTH), F32)]
        + _gather_outputs(gather),
        scratch_shapes=[pltpu.VMEM((SUBLANES, 2 * CH_S), F32)] + (_gather_sems(ng) if ng else []),
        compiler_params=_cparams(2),
    )(proj, wb, tab, wc, *gather)
    return outs[0], outs[1], (_gather_own(outs[2:], gather, gather_bases) if ng else [])


def _glu_forward(y, u, d, wg_ref, bg):
    yf = y + d * u
    z = _gelu(yf)
    zb = z.astype(BF16)
    zz = jnp.concatenate([_dot(zb, wg_ref[sh]) for sh in range(N_CHIPS)], axis=-1) + bg
    return yf, z, zz[:, 0:SSM_WIDTH], zz[:, SSM_WIDTH:2 * SSM_WIDTH]


def _ssm_glu_fwd(y, proj, d, w_glu_l, b_glu):
    s = y.shape[0]
    tm = _row_tile(s, 1024)

    def body(y_ref, u_ref, gs_ref, d_ref, wg_ref, bg_ref, o_ref):
        _, _, val, gate = _glu_forward(y_ref[...], u_ref[...], d_ref[...], wg_ref, bg_ref[...])
        gs = gs_ref[...]
        o_ref[...] = val * _sigmoid(gate) * (gs * _sigmoid(gs))

    row = lambda i: (i, 0)
    return pl.pallas_call(
        body, name="ssm_glu", grid=(s // tm,),
        in_specs=[pl.BlockSpec((tm, SSM_WIDTH), row), pl.BlockSpec((tm, SSM_WIDTH), row),
                  pl.BlockSpec((tm, SSM_WIDTH), lambda i: (i, 1)), pl.BlockSpec((1, SSM_WIDTH), lambda i: (0, 0)),
                  pl.BlockSpec((N_CHIPS, SSM_WIDTH, ROW_SHARD), lambda i: (0, 0, 0)),
                  pl.BlockSpec((1, 2 * SSM_WIDTH), lambda i: (0, 0))],
        out_specs=pl.BlockSpec((tm, SSM_WIDTH), row),
        out_shape=jax.ShapeDtypeStruct((s, SSM_WIDTH), F32),
        compiler_params=_cparams(1),
    )(y, proj, proj, d, w_glu_l, b_glu)


def _tri(kind):
    r = jnp.arange(ATTN_BLOCK)
    if kind == "suffix_incl":
        m = r[:, None] >= r[None, :]
    else:
        m = r[:, None] < r[None, :]
    return jnp.concatenate([m, jnp.ones_like(m)], axis=1).astype(BF16)


def _head_masks():
    lane = lax.broadcasted_iota(jnp.int32, (1, 2 * HEAD_DIM), 1)
    return [lane < HEAD_DIM, lane >= HEAD_DIM]


def _chain_step(t, base, n_sub, first, q_ref, k_ref, tri_ref, l_scr, per_chain):
    tb = ATTN_BLOCK
    row = lax.broadcasted_iota(jnp.int32, (tb, tb), 0)
    col = lax.broadcasted_iota(jnp.int32, (tb, tb), 1)
    masks = _head_masks()
    blks = [base + a - t for a in range(n_sub)]
    r0s = [pl.multiple_of(jnp.maximum(blk, 0) * tb, tb) for blk in blks]
    zs = []
    for a in range(n_sub):
        kb = k_ref[pl.ds(r0s[a], tb), :]
        qa = q_ref[a * tb:(a + 1) * tb, :]
        for mask in masks:
            zs.append(_dot_nt(jnp.where(mask, qa, jnp.zeros_like(qa)), kb))
    parts = []
    for z in zs:
        ls = jnp.minimum(-z, 0.0) - jnp.log(1.0 + jnp.exp(-jnp.abs(z)))
        if first:
            ls = jnp.where(col < row, ls, 0.0)
        parts.append(_split_hilo(ls))
    tri = tri_ref[...]
    sums = [_dot(hi, tri) + _dot(lo, tri) for hi, lo in parts]
    top = None
    ws = []
    for c, (z, sm) in enumerate(zip(zs, sums)):
        if first:
            lsum = jnp.zeros((tb, tb), F32)
        else:
            lsum = l_scr[c] + jnp.where(blks[c // 2] >= 0, 0.0, -1e30)
        w = jnp.exp(z + sm[:, 0:tb] + lsum)
        if first:
            w = jnp.where(col < row, w, 0.0)
        ws.append(w)
        lsum = lsum + sm[:, tb:2 * tb]
        l_scr[c] = lsum
        top = lsum if top is None else jnp.maximum(top, lsum)
    for c, (z, w) in enumerate(zip(zs, ws)):
        per_chain(c // 2, c % 2, c, r0s[c // 2], z, w)
    return jnp.max(top)


def _chain_sweep(base, n_sub, q_ref, k_ref, tri_ref, l_scr, per_chain):
    top = _chain_step(0, base, n_sub, True, q_ref, k_ref, tri_ref, l_scr, functools.partial(per_chain, 0))

    def cond(carry):
        t, top = carry
        return jnp.logical_and(t <= base + n_sub - 1, top > EXP_ZERO)

    def step(carry):
        t, _ = carry
        return t + 1, _chain_step(t, base, n_sub, False, q_ref, k_ref, tri_ref, l_scr, functools.partial(per_chain, t))

    steps, _ = lax.while_loop(cond, step, (jnp.int32(1), top))
    return steps


ATTN_SUB_FWD = 8
ATTN_SUB_BWD = 8


def _attn_fwd(qkv, proj, gather=None, gather_bases=None):
    s = qkv.shape[0]
    tb = ATTN_BLOCK
    n_sub = min(ATTN_SUB_FWD, s // tb)
    tq = n_sub * tb
    n_hp = ATTN_WIDTH // (2 * HEAD_DIM)
    gather = [] if gather is None else gather
    ng = len(gather)

    def body(*refs):
        q_ref, k_ref, v_ref, g_ref, tri_ref = refs[0:5]
        g_ins = refs[5:5 + ng]
        o_ref, ya_ref = refs[5 + ng:7 + ng]
        g_outs = refs[7 + ng:7 + 2 * ng]
        l_scr = refs[7 + 2 * ng]
        sems = refs[8 + 2 * ng:]
        i = pl.program_id(1)
        masks = _head_masks()
        o_ref[...] = jnp.zeros_like(o_ref)

        if ng:
            @pl.when(jnp.logical_and(pl.program_id(0) == 0, i == 0))
            def _():
                _gather_start(g_ins, gather_bases, g_outs, sems)

        def per_chain(t, a, h, c, r0, z, w):
            vb = v_ref[pl.ds(r0, tb), :]
            vb = jnp.where(masks[h], vb, jnp.zeros_like(vb))
            o_ref[a * tb:(a + 1) * tb, :] += _dot(w.astype(BF16), vb)

        _chain_sweep(i * n_sub, n_sub, q_ref, k_ref, tri_ref, l_scr, per_chain)
        g = g_ref[...]
        ya_ref[...] = o_ref[...] * (g * _sigmoid(g))

        if ng:
            @pl.when(jnp.logical_and(pl.program_id(0) == n_hp - 1, i == s // tq - 1))
            def _():
                _gather_finish(g_ins, gather_bases, g_outs, sems)

    hp_blk = lambda off: pl.BlockSpec((tq, 2 * HEAD_DIM), lambda hp, i: (i, off + hp))
    res = lambda off: pl.BlockSpec((s, 2 * HEAD_DIM), lambda hp, i: (0, off + hp))
    outs = pl.pallas_call(
        body, name="attn_fwd_gather" if ng else "attn_fwd", grid=(n_hp, s // tq),
        in_specs=[hp_blk(0), res(4), res(8), hp_blk(20), pl.BlockSpec((tb, 2 * tb), lambda hp, i: (0, 0))] + [ANY] * ng,
        out_specs=[hp_blk(0), hp_blk(0)] + [ANY] * ng,
        out_shape=[jax.ShapeDtypeStruct((s, ATTN_WIDTH), F32)] * 2 + _gather_outputs(gather),
        scratch_shapes=[pltpu.VMEM((2 * n_sub, tb, tb), F32)] + (_gather_sems(ng) if ng else []),
        compiler_params=_cparams(2),
    )(qkv, qkv, qkv, proj, _tri("suffix_incl"), *gather)
    return outs[0], outs[1], (_gather_own(outs[2:], gather, gather_bases) if ng else [])


def _rms_rows(x, g):
    r = lax.rsqrt(jnp.mean(x * x, axis=-1, keepdims=True) + RMS_EPS)
    return r, x * r * g


def _ple_forward(h1, p, g2, wpg_ref, wpp_ref):
    r2, hn2 = _rms_rows(h1, g2)
    hb = hn2.astype(BF16)
    gpre = _dot(hb[:, 0:ROW_SHARD], wpg_ref[0])
    for sh in range(1, N_CHIPS):
        gpre = gpre + _dot(hb[:, ROW_SHARD * sh:ROW_SHARD * (sh + 1)], wpg_ref[sh])
    gate = _sigmoid(gpre)
    pb = p.astype(BF16)
    pp = jnp.concatenate([_dot(pb, wpp_ref[sh]) for sh in range(N_CHIPS)], axis=-1)
    return r2, hb, gate, pp


def _colsum8(a):
    t = a.shape[0]
    return a.reshape(t // SUBLANES, SUBLANES, a.shape[1]).sum(axis=0)


def _sq_err_grad(y, target):
    e = y - target
    sq = _colsum8(e * e)
    part = sq[:, 0:128]
    for b in range(1, D_MODEL // 128):
        part = part + sq[:, 128 * b:128 * (b + 1)]
    return e / D_MODEL, part


def _out_ple(h, ys, ya, p, g2, w_out_l, w_pg_l, w_pp_l, target=None):
    s = h.shape[0]
    tm = _row_tile(s, 512)
    last = target is not None

    def body(*refs):
        h_ref, ys_ref, ya_ref, p_ref, g_ref, wo_ref, wpg_ref, wpp_ref = refs[0:8]
        h1_ref, h2_ref = refs[8 + last], refs[9 + last]
        ysb = ys_ref[...].astype(BF16)
        yab = ya_ref[...].astype(BF16)
        h1 = h_ref[...]
        for sh, src in enumerate((ysb[:, 0:ROW_SHARD], ysb[:, ROW_SHARD:], yab[:, 0:ROW_SHARD], yab[:, ROW_SHARD:])):
            h1 = h1 + _dot(src, wo_ref[sh])
        _, _, gate, pp = _ple_forward(h1, p_ref[...], g_ref[...], wpg_ref, wpp_ref)
        h1_ref[...] = h1
        h2 = h1 + gate * pp
        if last:
            acc_ref = refs[11]

            @pl.when(pl.program_id(0) == 0)
            def _():
                acc_ref[...] = jnp.zeros_like(acc_ref)

            h2_ref[...], part = _sq_err_grad(h2, refs[8][...])
            acc_ref[...] += part
        else:
            h2_ref[...] = h2

    row = lambda i: (i, 0)
    big = pl.BlockSpec((tm, D_MODEL), row)
    wspec = lambda r, cdim: pl.BlockSpec((N_CHIPS, r, cdim), lambda i: (0, 0, 0))
    acc = pl.BlockSpec((SUBLANES, 128), lambda i: (0, 0))
    return pl.pallas_call(
        body, name="out_ple_loss" if last else "out_ple", grid=(s // tm,),
        in_specs=[big, pl.BlockSpec((tm, SSM_WIDTH), row), pl.BlockSpec((tm, ATTN_WIDTH), row),
                  pl.BlockSpec((tm, PLE_DIM), row), pl.BlockSpec((1, D_MODEL), lambda i: (0, 0)),
                  wspec(ROW_SHARD, D_MODEL), wspec(ROW_SHARD, D_MODEL), wspec(PLE_DIM, ROW_SHARD)] + [big] * last,
        out_specs=[big] * 2 + [acc] * last,
        out_shape=[jax.ShapeDtypeStruct((s, D_MODEL), F32)] * 2 + [jax.ShapeDtypeStruct((SUBLANES, 128), F32)] * last,
        compiler_params=_cparams(1),
    )(h, ys, ya, p, g2, w_out_l, w_pg_l, w_pp_l, *([target] if last else []))


def _rms_bwd(x, r, g, dy):
    gdy = g * dy
    dx = r * gdy - x * (r * r * r) * jnp.mean(x * gdy, axis=-1, keepdims=True)
    return dx, x * r * dy


def _out_ple_bwd(dh2, h1, p, g2, w_out_l, w_pg_l, w_pp_l):
    s = h1.shape[0]
    tm = _row_tile(s, 512)

    def body(dh2_ref, h1_ref, p_ref, g_ref, wo_ref, wpg_ref, wpp_ref,
             dh1_ref, dmix_ref, hn_ref, dgp_ref, dpp_ref, dh1b_ref, dg_ref):
        @pl.when(pl.program_id(0) == 0)
        def _():
            dg_ref[...] = jnp.zeros_like(dg_ref)

        h1 = h1_ref[...]
        dh2 = dh2_ref[...]
        g2v = g_ref[...]
        r2, hb, gate, pp = _ple_forward(h1, p_ref[...], g2v, wpg_ref, wpp_ref)
        dgp = (dh2 * pp) * gate * (1.0 - gate)
        dgpb = dgp.astype(BF16)
        dhn = jnp.concatenate([_dot_nt(dgpb, wpg_ref[sh]) for sh in range(N_CHIPS)], axis=-1)
        dx, dgrow = _rms_bwd(h1, r2, g2v, dhn)
        dh1 = dh2 + dx
        dh1b = dh1.astype(BF16)
        dh1_ref[...] = dh1
        dh1b_ref[...] = dh1b
        hn_ref[...] = hb
        dgp_ref[...] = dgpb
        dpp_ref[...] = (dh2 * gate).astype(BF16)
        dg_ref[...] += _colsum8(dgrow)
        for sh in range(N_CHIPS):
            dmix_ref[:, ROW_SHARD * sh:ROW_SHARD * (sh + 1)] = _dot_nt(dh1b, wo_ref[sh])

    row = lambda i: (i, 0)
    wspec = lambda r, cdim: pl.BlockSpec((N_CHIPS, r, cdim), lambda i: (0, 0, 0))
    big = pl.BlockSpec((tm, D_MODEL), row)
    return pl.pallas_call(
        body, name="out_ple_bwd", grid=(s // tm,),
        in_specs=[big, big, pl.BlockSpec((tm, PLE_DIM), row), pl.BlockSpec((1, D_MODEL), lambda i: (0, 0)),
                  wspec(ROW_SHARD, D_MODEL), wspec(ROW_SHARD, D_MODEL), wspec(PLE_DIM, ROW_SHARD)],
        out_specs=[big] * 6 + [pl.BlockSpec((SUBLANES, D_MODEL), lambda i: (0, 0))],
        out_shape=[jax.ShapeDtypeStruct((s, D_MODEL), F32)] * 2 + [jax.ShapeDtypeStruct((s, D_MODEL), BF16)] * 4
        + [jax.ShapeDtypeStruct((SUBLANES, D_MODEL), F32)],
        compiler_params=_cparams(1),
    )(dh2, h1, p, g2, w_out_l, w_pg_l, w_pp_l)


def _tn_matmul(a, b, n_blocks, block_a, name, into=None, first_block=0, total_blocks=None):
    s = a.shape[0]
    tk = _row_tile(s, 1024)
    nk = s // tk
    total_blocks = n_blocks if total_blocks is None else total_blocks
    ka, nb = a.shape[1], b.shape[1]
    if block_a:
        ka //= n_blocks
    else:
        nb //= n_blocks

    def body(*refs):
        a_ref, b_ref, o_ref, acc_ref = refs[0], refs[1], refs[-2], refs[-1]

        @pl.when(pl.program_id(0) == 0)
        def _():
            acc_ref[...] = jnp.zeros_like(acc_ref)

        at = a_ref[...].astype(BF16).T
        bb = b_ref[...].astype(BF16)
        for sh in range(n_blocks):
            if block_a:
                acc_ref[sh] += _dot(at[ka * sh:ka * (sh + 1), :], bb)
            else:
                acc_ref[sh] += _dot(at, bb[:, nb * sh:nb * (sh + 1)])

        @pl.when(pl.program_id(0) == nk - 1)
        def _():
            o_ref[...] = acc_ref[...].astype(BF16)

    in_specs = [pl.BlockSpec((tk, a.shape[1]), lambda i: (i, 0)), pl.BlockSpec((tk, b.shape[1]), lambda i: (i, 0))]
    operands = [a, b]
    aliases = {}
    if into is not None:
        in_specs.append(ANY)
        operands.append(into)
        aliases = {2: 0}
    return pl.pallas_call(
        body, name=name, grid=(nk,),
        in_specs=in_specs,
        out_specs=pl.BlockSpec((n_blocks, ka, nb), lambda i: (first_block // n_blocks, 0, 0)),
        out_shape=jax.ShapeDtypeStruct((total_blocks, ka, nb), BF16),
        scratch_shapes=[pltpu.VMEM((n_blocks, ka, nb), F32)],
        input_output_aliases=aliases,
        compiler_params=_cparams(1),
    )(*operands)


def _attn_bwd(qkv, o, proj, dmix, scatter=None):
    scatter = [] if scatter is None else scatter
    nsc = len(scatter)
    s = qkv.shape[0]
    tb = ATTN_BLOCK
    nq = s // tb
    n_sub = min(ATTN_SUB_BWD, nq)
    tq = n_sub * tb
    n_chain = 2 * n_sub

    def body(*refs):
        q_ref, k_ref, v_ref, o_ref, g_ref, dya_ref, tri_s_ref, tri_p_ref = refs[0:8]
        sc_ins = refs[8:8 + nsc]
        dq_ref, dk_ref, dv_ref, dg_ref = refs[8 + nsc:12 + nsc]
        sc_outs = refs[12 + nsc:12 + 2 * nsc]
        do_scr, l_scr, g_scr, s_scr, w_scr = refs[12 + 2 * nsc:17 + 2 * nsc]
        sc_sems = refs[17 + 2 * nsc:]
        i = pl.program_id(1)
        base = i * n_sub

        if nsc:
            @pl.when(jnp.logical_and(pl.program_id(0) == 0, i == 0))
            def _():
                _scatter_start(sc_ins, sc_outs, sc_sems)

        @pl.when(i == 0)
        def _():
            dk_ref[...] = jnp.zeros_like(dk_ref)
            dv_ref[...] = jnp.zeros_like(dv_ref)

        g = g_ref[...]
        sg = _sigmoid(g)
        dya = dya_ref[...]
        do_scr[...] = (dya * (g * sg)).astype(BF16)
        dg_ref[...] = dya * o_ref[...] * (sg * (1.0 + g * (1.0 - sg)))
        dq_ref[...] = jnp.zeros_like(dq_ref)
        g_scr[...] = jnp.zeros_like(g_scr)
        masks = _head_masks()

        def keep(t, a, h, c, r0, z, w):
            s_scr[c, t] = _sigmoid(z).astype(BF16)
            w_scr[c, t] = w.astype(BF16)

        steps = _chain_sweep(base, n_sub, q_ref, k_ref, tri_s_ref, l_scr, keep)
        row = lax.broadcasted_iota(jnp.int32, (tb, tb), 0)
        col = lax.broadcasted_iota(jnp.int32, (tb, tb), 1)

        def back(it, carry):
            t = steps - 1 - it
            r0s = [pl.multiple_of(jnp.maximum(base + a - t, 0) * tb, tb) for a in range(n_sub)]
            qhs, dohs, khs, gws = [], [], [], []
            for a in range(n_sub):
                kb = k_ref[pl.ds(r0s[a], tb), :]
                vb = v_ref[pl.ds(r0s[a], tb), :]
                qa = q_ref[a * tb:(a + 1) * tb, :]
                doa = do_scr[a * tb:(a + 1) * tb, :]
                for h, mask in enumerate(masks):
                    qhs.append(jnp.where(mask, qa, jnp.zeros_like(qa)))
                    khs.append(jnp.where(mask, kb, jnp.zeros_like(kb)))
                    dohs.append(jnp.where(mask, doa, jnp.zeros_like(doa)))
                    gws.append(w_scr[2 * a + h, t].astype(F32) * _dot_nt(dohs[-1], vb))
            parts = [_split_hilo(gw) for gw in gws]
            tri = tri_p_ref[...]
            sums = [_dot(hi, tri) + _dot(lo, tri) for hi, lo in parts]
            dzs = []
            for c, (gw, sm) in enumerate(zip(gws, sums)):
                gsum = g_scr[c]
                dz = gw - (gw + sm[:, 0:tb] + gsum) * s_scr[c, t].astype(F32)
                dz = jnp.where(col < row + t * tb, dz, 0.0)
                g_scr[c] = gsum + sm[:, tb:2 * tb]
                dzs.append(dz.astype(BF16))
            for c, dzb in enumerate(dzs):
                a = c // 2
                dk_ref[pl.ds(r0s[a], tb), :] += _dot_tn(dzb, qhs[c])
                dv_ref[pl.ds(r0s[a], tb), :] += _dot_tn(w_scr[c, t], dohs[c])
                dq_ref[a * tb:(a + 1) * tb, :] += _dot(dzb, khs[c])
            return carry

        lax.fori_loop(0, steps, back, 0)

        if nsc:
            @pl.when(jnp.logical_and(pl.program_id(0) == n_hp - 1, i == s // tq - 1))
            def _():
                _scatter_finish(sc_ins, sc_outs, sc_sems)

    n_hp = ATTN_WIDTH // (2 * HEAD_DIM)
    hp_blk = lambda off: pl.BlockSpec((tq, 2 * HEAD_DIM), lambda hp, i: (i, off + hp))
    res = lambda off: pl.BlockSpec((s, 2 * HEAD_DIM), lambda hp, i: (0, off + hp))
    tri = pl.BlockSpec((tb, 2 * tb), lambda hp, i: (0, 0))
    outs = pl.pallas_call(
        body, name="attn_bwd_scatter" if nsc else "attn_bwd", grid=(n_hp, s // tq),
        in_specs=[hp_blk(0), res(4), res(8), hp_blk(0), hp_blk(20), hp_blk(4), tri, tri] + [ANY] * nsc,
        out_specs=[hp_blk(0), res(0), res(0), hp_blk(0)] + [ANY] * nsc,
        out_shape=[jax.ShapeDtypeStruct((s, ATTN_WIDTH), F32)] * 4 + [jax.ShapeDtypeStruct(a.shape, a.dtype) for a in scatter],
        scratch_shapes=[pltpu.VMEM((tq, 2 * HEAD_DIM), BF16), pltpu.VMEM((n_chain, tb, tb), F32),
                        pltpu.VMEM((n_chain, tb, tb), F32), pltpu.VMEM((n_chain, nq, tb, tb), BF16),
                        pltpu.VMEM((n_chain, nq, tb, tb), BF16)] + (_scatter_sems(nsc) if nsc else []),
        compiler_params=_cparams(2, vmem_limit=V7X_VMEM_LIMIT_ATTN_BWD),
    )(qkv, qkv, qkv, o, proj, dmix, _tri("suffix_incl"), _tri("prefix_strict"), *scatter)
    return outs[0], outs[1], outs[2], outs[3], outs[4:]


def _ssm_glu_bwd(dmix, y, proj, d, w_glu_l, b_glu):
    s = y.shape[0]
    tm = _row_tile(s, 1024)

    def body(dys_ref, y_ref, u_ref, gs_ref, d_ref, wg_ref, bg_ref,
             dyf_ref, du_ref, dgs_ref, z_ref, dzz_ref, dd_ref, db_ref):
        @pl.when(pl.program_id(0) == 0)
        def _():
            dd_ref[...] = jnp.zeros_like(dd_ref)
            db_ref[...] = jnp.zeros_like(db_ref)

        u = u_ref[...]
        dv = d_ref[...]
        yf, z, val, gate = _glu_forward(y_ref[...], u, dv, wg_ref, bg_ref[...])
        gs = gs_ref[...]
        sgs = _sigmoid(gs)
        sgate = _sigmoid(gate)
        dys = dys_ref[...]
        dgv = dys * (gs * sgs)
        dgs_ref[...] = dys * (val * sgate) * (sgs * (1.0 + gs * (1.0 - sgs)))
        dzz = jnp.concatenate([dgv * sgate, dgv * val * sgate * (1.0 - sgate)], axis=-1)
        dzzb = dzz.astype(BF16)
        dz = _dot_nt(dzzb[:, 0:ROW_SHARD], wg_ref[0])
        for sh in range(1, N_CHIPS):
            dz = dz + _dot_nt(dzzb[:, ROW_SHARD * sh:ROW_SHARD * (sh + 1)], wg_ref[sh])
        dyf = dz * _gelu_grad(yf)
        dyf_ref[...] = dyf
        du_ref[...] = dyf * dv
        z_ref[...] = z.astype(BF16)
        dzz_ref[...] = dzzb
        dd_ref[...] += _colsum8(dyf * u)
        db_ref[...] += _colsum8(dzz)

    row = lambda i: (i, 0)
    half = pl.BlockSpec((tm, SSM_WIDTH), row)
    return pl.pallas_call(
        body, name="ssm_glu_bwd", grid=(s // tm,),
        in_specs=[half, half, half, pl.BlockSpec((tm, SSM_WIDTH), lambda i: (i, 1)),
                  pl.BlockSpec((1, SSM_WIDTH), lambda i: (0, 0)),
                  pl.BlockSpec((N_CHIPS, SSM_WIDTH, ROW_SHARD), lambda i: (0, 0, 0)),
                  pl.BlockSpec((1, 2 * SSM_WIDTH), lambda i: (0, 0))],
        out_specs=[half, half, half, half, pl.BlockSpec((tm, 2 * SSM_WIDTH), row),
                   pl.BlockSpec((SUBLANES, SSM_WIDTH), lambda i: (0, 0)),
                   pl.BlockSpec((SUBLANES, 2 * SSM_WIDTH), lambda i: (0, 0))],
        out_shape=[jax.ShapeDtypeStruct((s, SSM_WIDTH), F32)] * 3
        + [jax.ShapeDtypeStruct((s, SSM_WIDTH), BF16), jax.ShapeDtypeStruct((s, 2 * SSM_WIDTH), BF16),
           jax.ShapeDtypeStruct((SUBLANES, SSM_WIDTH), F32), jax.ShapeDtypeStruct((SUBLANES, 2 * SSM_WIDTH), F32)],
        compiler_params=_cparams(1, vmem_limit=V7X_VMEM_LIMIT_GLU_BWD),
    )(dmix, y, proj, proj, d, w_glu_l, b_glu)


def _ssm_scan_bwd(dyf, xs, proj, wct, tab_rev, wbt):
    s = dyf.shape[0]
    tm = _row_tile(s, SCAN_TILE)
    nt = s // tm
    length = tm // SUBLANES

    def body(dy_ref, xs_ref, u_ref, wct_ref, tab_ref, wbt_ref, du_ref, dwc_ref, dwb_ref, da_ref, lam_ref, carry_ref):
        @pl.when(pl.program_id(1) == 0)
        def _():
            carry_ref[...] = jnp.zeros_like(carry_ref)
            dwc_ref[...] = jnp.zeros_like(dwc_ref)
            dwb_ref[...] = jnp.zeros_like(dwb_ref)
            da_ref[...] = jnp.zeros_like(da_ref)

        dyp = _interleave_chunks(dy_ref[...]).astype(BF16)
        up = _interleave_chunks(u_ref[...]).astype(BF16)
        lam_ref[...] = _dot(dyp, wct_ref[...])

        def tail(r0, lr, li, carry):
            er, ei, dar, dai = carry
            xr = xs_ref[pl.ds(r0, SUBLANES), 0:CH_S]
            xi = xs_ref[pl.ds(r0, SUBLANES), CH_S:2 * CH_S]
            return lr, li, dar + (xr * er + xi * ei), dai + (xr * ei - xi * er)

        fix, (gr, gi) = _chunk_scan(lam_ref, tab_ref, carry_ref, length, reverse=True, tail=tail)
        zero = jnp.zeros((SUBLANES, CH_S), F32)
        _, _, dar, dai = lax.fori_loop(0, length, fix, (gr, gi, zero, zero), unroll=2)
        da_ref[:, 0:CH_S] += dar
        da_ref[:, CH_S:2 * CH_S] += dai
        lamb = lam_ref[...].astype(BF16)
        du_ref[...] = _time_order(_dot(lamb, wbt_ref[...]))
        dwc_ref[...] += _dot_tn(xs_ref[...].astype(BF16), dyp)
        dwb_ref[...] += _dot_tn(up, lamb)

    rev = lambda j, i: (nt - 1 - i, j)
    return pl.pallas_call(
        body, name="ssm_scan_bwd", grid=(SSM_CHUNKS, nt),
        in_specs=[pl.BlockSpec((tm, CH_W), rev),
                  pl.BlockSpec((None, tm, 2 * CH_S), lambda j, i: (j, nt - 1 - i, 0)),
                  pl.BlockSpec((tm, CH_W), rev),
                  pl.BlockSpec((None, CH_W, 2 * CH_S), lambda j, i: (j, 0, 0)),
                  pl.BlockSpec((None, length, 2 * CH_S), lambda j, i: (j, 0, 0)),
                  pl.BlockSpec((None, 2 * CH_S, CH_W), lambda j, i: (j, 0, 0))],
        out_specs=[pl.BlockSpec((tm, CH_W), rev),
                   pl.BlockSpec((None, 2 * CH_S, CH_W), lambda j, i: (j, 0, 0)),
                   pl.BlockSpec((None, CH_W, 2 * CH_S), lambda j, i: (j, 0, 0)),
                   pl.BlockSpec((None, SUBLANES, 2 * CH_S), lambda j, i: (j, 0, 0))],
        out_shape=[jax.ShapeDtypeStruct((s, SSM_WIDTH), F32),
                   jax.ShapeDtypeStruct((SSM_CHUNKS, 2 * CH_S, CH_W), F32),
                   jax.ShapeDtypeStruct((SSM_CHUNKS, CH_W, 2 * CH_S), F32),
                   jax.ShapeDtypeStruct((SSM_CHUNKS, SUBLANES, 2 * CH_S), F32)],
        scratch_shapes=[pltpu.VMEM((tm, 2 * CH_S), F32), pltpu.VMEM((SUBLANES, 2 * CH_S), F32)],
        compiler_params=_cparams(2),
    )(dyf, xs, proj, wct, tab_rev, wbt)


def _in_proj_bwd(h, g1, w_in_l, qg, kg, proj, du_a, du_b, dgs, dq, dk, dv, dga, dh1):
    s = h.shape[0]
    tm = _row_tile(s, 256)

    def body(h_ref, g_ref, w_ref, qg_ref, kg_ref, ones_ref, q_ref, k_ref, dua_ref, dub_ref, dgs_ref, dq_ref, dk_ref,
             dv_ref, dga_ref, dh1_ref, dh_ref, hn_ref, dp_ref, dg1_ref, dqg_ref, dkg_ref):
        @pl.when(pl.program_id(0) == 0)
        def _():
            dg1_ref[...] = jnp.zeros_like(dg1_ref)
            dqg_ref[...] = jnp.zeros_like(dqg_ref)
            dkg_ref[...] = jnp.zeros_like(dkg_ref)

        ones = ones_ref[...]

        def head_norm_bwd(x, gain, dy):
            r = lax.rsqrt(_dot_hilo(x * x, ones) + RMS_EPS)
            gdy = gain * dy
            dx = r * gdy - x * (r * r * r) * _dot_hilo(x * gdy, ones)
            return dx, x * r * dy

        dqr, dqg_rows = head_norm_bwd(q_ref[...], qg_ref[...], dq_ref[...] * ATTN_SCALE)
        dkr, dkg_rows = head_norm_bwd(k_ref[...], kg_ref[...], dk_ref[...])
        dqg_ref[...] += _colsum8(dqg_rows)
        dkg_ref[...] += _colsum8(dkg_rows)
        dp_ref[:, 0:512] = (dua_ref[...] + dub_ref[...]).astype(BF16)
        dp_ref[:, 512:1024] = dgs_ref[...].astype(BF16)
        dp_ref[:, 1024:1536] = dqr.astype(BF16)
        dp_ref[:, 1536:2048] = dkr.astype(BF16)
        dp_ref[:, 2048:2560] = dv_ref[...].astype(BF16)
        dp_ref[:, 2560:3072] = dga_ref[...].astype(BF16)
        dhn = _dot_nt(dp_ref[:, 0:IN_SHARD], w_ref[0])
        for sh in range(1, N_CHIPS):
            dhn = dhn + _dot_nt(dp_ref[:, IN_SHARD * sh:IN_SHARD * (sh + 1)], w_ref[sh])
        x = h_ref[...]
        gv = g_ref[...]
        r, hn = _rms_rows(x, gv)
        dx, dg_rows = _rms_bwd(x, r, gv, dhn)
        dh_ref[...] = dh1_ref[...] + dx
        hn_ref[...] = hn.astype(BF16)
        dg1_ref[...] += _colsum8(dg_rows)

    row = lambda i: (i, 0)
    full = lambda shape: pl.BlockSpec(shape, lambda i: (0,) * len(shape))
    big = pl.BlockSpec((tm, D_MODEL), row)
    half = pl.BlockSpec((tm, 512), row)
    return pl.pallas_call(
        body, name="in_proj_bwd", grid=(s // tm,),
        in_specs=[big, full((1, D_MODEL)), full((N_CHIPS, D_MODEL, IN_SHARD)),
                  full((1, ATTN_WIDTH)), full((1, ATTN_WIDTH)), full((ATTN_WIDTH, ATTN_WIDTH)),
                  pl.BlockSpec((tm, 512), lambda i: (i, 2)), pl.BlockSpec((tm, 512), lambda i: (i, 3)),
                  half, half, half, half, half, half, half, big],
        out_specs=[big, big, pl.BlockSpec((tm, IN_COLS), row), pl.BlockSpec((SUBLANES, D_MODEL), lambda i: (0, 0)),
                   pl.BlockSpec((SUBLANES, ATTN_WIDTH), lambda i: (0, 0)), pl.BlockSpec((SUBLANES, ATTN_WIDTH), lambda i: (0, 0))],
        out_shape=[jax.ShapeDtypeStruct((s, D_MODEL), F32), jax.ShapeDtypeStruct((s, D_MODEL), BF16),
                   jax.ShapeDtypeStruct((s, IN_COLS), BF16), jax.ShapeDtypeStruct((SUBLANES, D_MODEL), F32),
                   jax.ShapeDtypeStruct((SUBLANES, ATTN_WIDTH), F32), jax.ShapeDtypeStruct((SUBLANES, ATTN_WIDTH), F32)],
        compiler_params=_cparams(1),
    )(h, g1, w_in_l, qg, kg, _head_ones(), proj, proj, du_a, du_b, dgs, dq, dk, dv, dga, dh1)


SMALL_NAMES = ("mix_norm_g", "ssm_a_re", "ssm_a_im", "ssm_log_dt", "ssm_b_re", "ssm_b_im", "ssm_c_re", "ssm_c_im",
               "ssm_d", "ssm_b_glu", "q_norm_g", "k_norm_g", "ple_norm_g")
SMALL_4D = ("ssm_b_re", "ssm_b_im", "ssm_c_re", "ssm_c_im")
BIG_NAMES = ("w_in", "ssm_w_glu", "w_out", "w_ple_gate", "w_ple_proj")


def _ssm_setup(sm, layer, length):
    col = lambda a: a[layer].reshape(1, N_STATES)
    a_re, a_im = col(sm["ssm_a_re"]), col(sm["ssm_a_im"])
    log_dt = jnp.repeat(sm["ssm_log_dt"][layer], SSM_STATE).reshape(1, N_STATES)
    b_re = sm["ssm_b_re"][layer].reshape(N_STATES, SSM_GROUP).T
    b_im = sm["ssm_b_im"][layer].reshape(N_STATES, SSM_GROUP).T
    by_channel = lambda c: c[layer].transpose(1, 0, 2).reshape(SSM_GROUP, N_STATES)
    disc_in = (a_re, a_im, log_dt, b_re, b_im)
    wb, wbt, wct, wc, tab, tab_rev = _disc_fwd(*disc_in, by_channel(sm["ssm_c_re"]), by_channel(sm["ssm_c_im"]), length)
    return dict(disc_in=disc_in, wb=wb, wbt=wbt, wc=wc, wct=wct, tab=tab, tab_rev=tab_rev)


def _whole_blocks(names, gathered):
    return {n: g.reshape(N_CHIPS, 2 * g.shape[2], g.shape[3]) for n, g in zip(names, gathered)}


def _local_step(x, p, target, sm, w_in0, local=None, gathered=None, layer1_hook=None):
    wg = [dict(w_in=w_in0), {}] if gathered is None else gathered
    tile8 = lambda a: jnp.tile(a, ATTN_WIDTH // HEAD_DIM).reshape(1, ATTN_WIDTH)
    saved = []
    h = x
    for l in range(N_LAYERS):
        ssm = _ssm_setup(sm, l, _row_tile(x.shape[0], SCAN_TILE) // SUBLANES)
        g1 = sm["mix_norm_g"][l].reshape(1, D_MODEL)
        g2 = sm["ple_norm_g"][l].reshape(1, D_MODEL)
        qg, kg = tile8(sm["q_norm_g"][l]), tile8(sm["k_norm_g"][l])
        dsk = sm["ssm_d"][l].reshape(1, SSM_WIDTH)
        bgl = sm["ssm_b_glu"][l].reshape(1, 2 * SSM_WIDTH)
        proj, qkv = _in_proj(h, g1, wg[l]["w_in"], qg, kg)
        if l == 0 and local is not None:
            rest = BIG_NAMES[1:]
            xs, y, got = _ssm_scan_fwd(proj, ssm["wb"], ssm["tab"], ssm["wc"], [local[n] for n in rest], [0] * len(rest))
            wg[0].update(_whole_blocks(rest, got))
            ys = _ssm_glu_fwd(y, proj, dsk, wg[0]["ssm_w_glu"], bgl)
            o, ya, got = _attn_fwd(qkv, proj, [local[n] for n in BIG_NAMES], [2] * len(BIG_NAMES))
            wg[1].update(_whole_blocks(BIG_NAMES, got))
        else:
            xs, y, _ = _ssm_scan_fwd(proj, ssm["wb"], ssm["tab"], ssm["wc"])
            ys = _ssm_glu_fwd(y, proj, dsk, wg[l]["ssm_w_glu"], bgl)
            o, ya, _ = _attn_fwd(qkv, proj)
        tail = (target,) if l == N_LAYERS - 1 else ()
        h1, h2, *sq = _out_ple(h, ys, ya, p[l], g2, wg[l]["w_out"], wg[l]["w_ple_gate"], wg[l]["w_ple_proj"], *tail)
        saved.append(dict(ssm=ssm, g1=g1, g2=g2, qg=qg, kg=kg, dsk=dsk, bgl=bgl, h=h, proj=proj, qkv=qkv, xs=xs, y=y,
                          ys=ys, o=o, ya=ya, h1=h1))
        h = h2
    dh = h
    loss = 0.5 * jnp.sum(sq[0]) / D_MODEL

    gbig = [{} for _ in range(N_LAYERS)]
    scattered = ([], [])
    gsm = {n: [None] * N_LAYERS for n in SMALL_NAMES}
    for l in reversed(range(N_LAYERS)):
        sv = saved[l]
        ssm = sv["ssm"]
        dh1, dmix, hn2b, dgpb, dppb, dh1b, dg2 = _out_ple_bwd(dh, sv["h1"], p[l], sv["g2"], wg[l]["w_out"],
                                                              wg[l]["w_ple_gate"], wg[l]["w_ple_proj"])
        gsm["ple_norm_g"][l] = dg2.sum(0)
        gbig[l]["w_ple_proj"] = _tn_matmul(p[l], dppb, N_CHIPS, False, "dw_ple_proj")
        gbig[l]["w_ple_gate"] = _tn_matmul(hn2b, dgpb, N_CHIPS, True, "dw_ple_gate")
        dwo = _tn_matmul(sv["ys"], dh1b, 2, True, "dw_out_ssm", None, 0, N_CHIPS)
        gbig[l]["w_out"] = _tn_matmul(sv["ya"], dh1b, 2, True, "dw_out_attn", dwo, 2, N_CHIPS)
        if l == 0 and layer1_hook is not None:
            chip1 = layer1_hook(gbig[1])
            dqs, dkn, dv, dga, got = _attn_bwd(sv["qkv"], sv["o"], sv["proj"], dmix, chip1)
            scattered = (chip1, got)
        else:
            dqs, dkn, dv, dga, _ = _attn_bwd(sv["qkv"], sv["o"], sv["proj"], dmix)
        dyf, du_a, dgs, zb, dzzb, dd, dbg = _ssm_glu_bwd(dmix, sv["y"], sv["proj"], sv["dsk"], wg[l]["ssm_w_glu"], sv["bgl"])
        gsm["ssm_d"][l] = dd.sum(0).reshape(SSM_GROUPS, SSM_GROUP)
        gsm["ssm_b_glu"][l] = dbg.sum(0)
        gbig[l]["ssm_w_glu"] = _tn_matmul(zb, dzzb, N_CHIPS, False, "dw_glu")
        du_b, dwc, dwb, da = _ssm_scan_bwd(dyf, sv["xs"], sv["proj"], ssm["wct"], ssm["tab_rev"], ssm["wbt"])
        d_are, d_aim, d_ldt, d_bre, d_bim, d_cre, d_cim = _disc_bwd(*ssm["disc_in"], da, dwb, dwc)
        by_group = lambda t: t.reshape(SSM_GROUP, SSM_GROUPS, SSM_STATE).transpose(1, 0, 2)
        gsm["ssm_c_re"][l] = by_group(d_cre)
        gsm["ssm_c_im"][l] = by_group(d_cim)
        gsm["ssm_a_re"][l] = d_are.reshape(SSM_GROUPS, SSM_STATE)
        gsm["ssm_a_im"][l] = d_aim.reshape(SSM_GROUPS, SSM_STATE)
        gsm["ssm_log_dt"][l] = d_ldt.reshape(SSM_GROUPS, SSM_STATE).sum(1)
        gsm["ssm_b_re"][l] = d_bre.T.reshape(SSM_GROUPS, SSM_STATE, SSM_GROUP)
        gsm["ssm_b_im"][l] = d_bim.T.reshape(SSM_GROUPS, SSM_STATE, SSM_GROUP)
        dh, hnb, dprojb, dg1, dqg, dkg = _in_proj_bwd(sv["h"], sv["g1"], wg[l]["w_in"], sv["qg"], sv["kg"], sv["proj"],
                                                      du_a, du_b, dgs, dqs, dkn, dv, dga, dh1)
        gsm["mix_norm_g"][l] = dg1.sum(0)
        gsm["q_norm_g"][l] = dqg.sum(0).reshape(-1, HEAD_DIM).sum(0)
        gsm["k_norm_g"][l] = dkg.sum(0).reshape(-1, HEAD_DIM).sum(0)
        gbig[l]["w_in"] = _tn_matmul(hnb, dprojb, N_CHIPS, False, "dw_in")
    gsm = {n: jnp.stack(v, 0) for n, v in gsm.items()}
    return loss, dh, gbig, gsm, scattered


_SMALL_PAD = 8 * 8 * 128


def _pack_small(d, extra):
    flat = jnp.concatenate([d[n].reshape(-1) for n in SMALL_NAMES] + [jnp.stack(extra)])
    n = flat.shape[0]
    padded = -(-n // _SMALL_PAD) * _SMALL_PAD
    return jnp.pad(flat, (0, padded - n))


def _unpack_small(flat, like):
    out, off = {}, 0
    for n in SMALL_NAMES:
        size = like[n].size
        out[n] = flat[off:off + size].reshape(like[n].shape)
        off += size
    return out, flat[off:]


def _half_views(arrs):
    return [a.reshape(a.shape[0], 2, a.shape[1] // 2, a.shape[2]) for a in arrs]


def _chip_sums(views, out_dtypes, tag):
    recv = _sibling_push(views, "grad_push_" + tag)
    return [_add_my_half(v, r, dt, "grad_half_add") for v, r, dt in zip(views, recv, out_dtypes)]


def kernel(x, p, mix_norm_g, w_in, ssm_a_re, ssm_a_im, ssm_log_dt, ssm_b_re, ssm_b_im, ssm_c_re, ssm_c_im, ssm_d, ssm_w_glu, ssm_b_glu, q_norm_g, k_norm_g, w_out, ple_norm_g, w_ple_gate, w_ple_proj, loss_target, m_mix_norm_g, m_w_in, m_ssm_a_re, m_ssm_a_im, m_ssm_log_dt, m_ssm_b_re, m_ssm_b_im, m_ssm_c_re, m_ssm_c_im, m_ssm_d, m_ssm_w_glu, m_ssm_b_glu, m_q_norm_g, m_k_norm_g, m_w_out, m_ple_norm_g, m_w_ple_gate, m_w_ple_proj, v_mix_norm_g, v_w_in, v_ssm_a_re, v_ssm_a_im, v_ssm_log_dt, v_ssm_b_re, v_ssm_b_im, v_ssm_c_re, v_ssm_c_im, v_ssm_d, v_ssm_w_glu, v_ssm_b_glu, v_q_norm_g, v_k_norm_g, v_w_out, v_ple_norm_g, v_w_ple_gate, v_w_ple_proj):
    args = dict(locals())
    names = ("mix_norm_g", "w_in", "ssm_a_re", "ssm_a_im", "ssm_log_dt", "ssm_b_re", "ssm_b_im", "ssm_c_re", "ssm_c_im",
             "ssm_d", "ssm_w_glu", "ssm_b_glu", "q_norm_g", "k_norm_g", "w_out", "ple_norm_g", "w_ple_gate", "w_ple_proj")
    w = {n: args[n] for n in names}
    m = {n: args["m_" + n] for n in names}
    v = {n: args["v_" + n] for n in names}

    local = {n: w[n].astype(BF16).reshape(2 * N_LAYERS, w[n].shape[1] // 2, w[n].shape[2]) for n in BIG_NAMES}
    w_in0 = _chip_gather([local["w_in"]], "w_in_gather")[0].reshape(N_CHIPS, D_MODEL, IN_SHARD)
    sm = {n: w[n] for n in SMALL_NAMES}
    nb = len(BIG_NAMES)
    loss, dx, gbig, gsm, (chip1, got1) = _local_step(
        x[0], p[:, 0], loss_target[0], sm, w_in0, local,
        layer1_hook=lambda g1: _chip_sums(_half_views([g1[n] for n in BIG_NAMES]), [BF16] * nb, "layer1"))

    small = _pack_small(gsm, [loss]).reshape(N_CHIPS, 2, SUBLANES, -1)
    chip0 = _chip_sums(_half_views([gbig[0][n] for n in BIG_NAMES]) + [small], [BF16] * nb + [F32], "layer0")
    got0 = _chip_scatter(chip0, "grad_chip_scatter")
    tot1 = [_sum4(a, own, "grad_chip_sum") for a, own in zip(got1, chip1)]
    tot0 = [_sum4(a, own, "grad_chip_sum") for a, own in zip(got0, chip0)]
    pieces = [(t, k, (l,)) for l, tots in enumerate((tot0[:nb], tot1)) for k, t in enumerate(tots)] + [(tot0[nb], nb, ())]
    joined = _sibling_join(pieces, [(N_LAYERS, 2) + t.shape for t in tot1] + [(2,) + tot0[nb].shape], "grad_sibling_join")
    small_all = _chip_gather([joined[nb]], "small_grad_gather")[0]
    small_tot = small_all.reshape(-1)
    g = {n: j.reshape(w[n].shape) for n, j in zip(BIG_NAMES, joined)}
    g_small, rest = _unpack_small(small_tot, sm)
    g.update(g_small)
    loss = rest[0]

    delta, new_m, new_v = {}, {}, {}
    for n in BIG_NAMES:
        lanes = w[n].shape[-1]
        outs = _adamw(_as_rows(w[n], lanes), _as_rows(g[n], lanes), _as_rows(m[n], lanes), _as_rows(v[n], lanes), "adamw_" + n)
        delta[n], new_m[n], new_v[n] = [o.reshape(w[n].shape) for o in outs]
    swap = lambda n, a: jnp.swapaxes(a, -1, -2) if n in ("ssm_b_re", "ssm_b_im") else a
    for group, per_layer in ((SMALL_4D, True), (tuple(n for n in SMALL_NAMES if n not in SMALL_4D), False)):
        outs = _adamw_many(*[[swap(n, d[n]) for n in group] for d in (w, g, m, v)],
                           "adamw_small_4d" if per_layer else "adamw_small", per_layer)
        for d, o in zip((delta, new_m, new_v), outs):
            d.update({n: swap(n, a) for n, a in zip(group, o)})

    return (loss, dx[None], *[g[n] for n in names], *[delta[n] for n in names],
            *[new_m[n] for n in names], *[new_v[n] for n in names])
```

```python
import functools
import math

import jax
import jax.numpy as jnp
from jax import lax
from jax.experimental import pallas as pl
from jax.experimental.pallas import tpu as pltpu

F32 = jnp.float32
BF16 = jnp.bfloat16

D_MODEL = 1024
N_LAYERS = 2
N_CHIPS = 4
IN_COLS = 3072
IN_SHARD = IN_COLS // N_CHIPS
SSM_WIDTH = 512
SSM_GROUP = 16
SSM_GROUPS = 32
SSM_STATE = 64
N_STATES = SSM_GROUPS * SSM_STATE
SSM_CHUNKS = 4
CH_W = SSM_WIDTH // SSM_CHUNKS
CH_S = N_STATES // SSM_CHUNKS
ATTN_WIDTH = 512
HEAD_DIM = 64
PLE_DIM = 256
ROW_SHARD = 256
RMS_EPS = 1e-6
ATTN_SCALE = HEAD_DIM ** -0.5
ATTN_BLOCK = 128
EXP_ZERO = -87.5
SUBLANES = 8
SCAN_TILE = 1024
V7X_VMEM_LIMIT = 44 * 1024 * 1024
V7X_VMEM_LIMIT_GLU_BWD = 52 * 1024 * 1024
V7X_VMEM_LIMIT_ATTN_BWD = 60 * 1024 * 1024

ADAM_LR = 0.001
ADAM_B1 = 0.9
ADAM_B2 = 0.999
ADAM_EPS = 1e-08
ADAM_WD = 0.01
ADAM_STEP = 10

MESH = pl.DeviceIdType.MESH
ANY = pl.BlockSpec(memory_space=pl.ANY)


def _cparams(n_grid=0, parallel=0, vmem_limit=V7X_VMEM_LIMIT):
    sem = tuple(["parallel"] * parallel + ["arbitrary"] * (n_grid - parallel))
    return pltpu.CompilerParams(dimension_semantics=sem, vmem_limit_bytes=vmem_limit)


def _dot(a, b):
    return jnp.dot(a, b, preferred_element_type=F32)


def _dot_nt(a, b):
    return lax.dot_general(a, b, (((1,), (1,)), ((), ())), preferred_element_type=F32)


def _dot_tn(a, b):
    return lax.dot_general(a, b, (((0,), (0,)), ((), ())), preferred_element_type=F32)


def _split_hilo(a):
    hi = a.astype(BF16)
    lo = (a - hi.astype(F32)).astype(BF16)
    return hi, lo


def _dot_hilo(a, b):
    hi, lo = _split_hilo(a)
    return _dot(hi, b) + _dot(lo, b)


def _sigmoid(x):
    return 0.5 * (jnp.tanh(0.5 * x) + 1.0)


_GELU_C = math.sqrt(2.0 / math.pi)


def _gelu(x):
    return 0.5 * x * (1.0 + jnp.tanh(_GELU_C * (x + 0.044715 * (x * x * x))))


def _gelu_grad(x):
    t = jnp.tanh(_GELU_C * (x + 0.044715 * (x * x * x)))
    return 0.5 * (1.0 + t) + 0.5 * x * (1.0 - t * t) * (_GELU_C * (1.0 + 3.0 * 0.044715 * (x * x)))


def _row_tile(s, want):
    for t in range(min(s, want), 7, -1):
        if s % t == 0 and t % SUBLANES == 0:
            return t
    return s


def _coords():
    return lax.axis_index("x"), lax.axis_index("y"), lax.axis_index("c")


def _other_chips(x, y):
    return [(1 - x, y), (x, 1 - y), (1 - x, 1 - y)]


def _remote(src, dst, send_sem, recv_sem, dev):
    return pltpu.make_async_remote_copy(src_ref=src, dst_ref=dst, send_sem=send_sem, recv_sem=recv_sem,
                                        device_id=dev, device_id_type=MESH)


def _set_block(buf, block, index):
    return lax.dynamic_update_index_in_dim(buf, block, index, 0)


def _gather_sems(n):
    return [pltpu.SemaphoreType.DMA((3 * n,)) for _ in range(4)]


def _gather_copies(ins, bases, outs, sems):
    send_sems, recv_sems, fwd_send, fwd_recv = sems
    x, y, c = _coords()
    me_chip = 2 * x + y
    sibling = (x, y, 1 - c)
    first, landed, passed, from_sibling = [], [], [], []
    for k in range(len(ins)):
        for j, (cx, cy) in enumerate(_other_chips(x, y)):
            i = 3 * k + j
            first.append(_remote(ins[k].at[bases[k] + c], outs[k].at[me_chip, c], send_sems.at[i], recv_sems.at[i], (cx, cy, c)))
            blk = outs[k].at[2 * cx + cy, c]
            landed.append(_remote(blk, blk, send_sems.at[i], recv_sems.at[i], (cx, cy, c)))
            passed.append(_remote(blk, blk, fwd_send.at[i], fwd_recv.at[i], sibling))
            blk = outs[k].at[2 * cx + cy, 1 - c]
            from_sibling.append(_remote(blk, blk, fwd_send.at[i], fwd_recv.at[i], sibling))
    return first, landed, passed, from_sibling


def _gather_start(ins, bases, outs, sems):
    for cp in _gather_copies(ins, bases, outs, sems)[0]:
        cp.start()


def _gather_finish(ins, bases, outs, sems):
    first, landed, passed, from_sibling = _gather_copies(ins, bases, outs, sems)
    for arrived, forward in zip(landed, passed):
        arrived.wait_recv()
        forward.start()
    for cp in from_sibling:
        cp.wait_recv()
    for cp in first + passed:
        cp.wait_send()


def _gather_outputs(arrs):
    return [jax.ShapeDtypeStruct((N_CHIPS, 2) + a.shape[1:], a.dtype) for a in arrs]


def _gather_own(outs, arrs, bases):
    me_chip = 2 * lax.axis_index("x") + lax.axis_index("y")
    return [_set_block(o, lax.slice_in_dim(a, b, b + 2, axis=0), me_chip) for o, a, b in zip(outs, arrs, bases)]


def _chip_gather(arrs, name, bases=None):
    n = len(arrs)
    bases = [0] * n if bases is None else bases

    def body(*refs):
        ins, outs, sems = refs[:n], refs[n:2 * n], refs[2 * n:]
        _gather_start(ins, bases, outs, sems)
        _gather_finish(ins, bases, outs, sems)

    outs = pl.pallas_call(
        body, name=name, out_shape=_gather_outputs(arrs),
        in_specs=[ANY] * n, out_specs=[ANY] * n, scratch_shapes=_gather_sems(n),
    )(*arrs)
    return _gather_own(outs, arrs, bases)


def _sibling_push(arrs, name):
    n = len(arrs)

    def body(*refs):
        ins, outs = refs[:n], refs[n:2 * n]
        send_sems, recv_sems = refs[2 * n:]
        x, y, c = _coords()
        cps = [_remote(ins[k].at[pl.ds(0, N_CHIPS), 1 - c], outs[k], send_sems.at[k], recv_sems.at[k], (x, y, 1 - c))
               for k in range(n)]
        for cp in cps:
            cp.start()
        for cp in cps:
            cp.wait_recv()
        for cp in cps:
            cp.wait_send()

    return pl.pallas_call(
        body, name=name,
        out_shape=[jax.ShapeDtypeStruct((a.shape[0],) + a.shape[2:], a.dtype) for a in arrs],
        in_specs=[ANY] * n, out_specs=[ANY] * n,
        scratch_shapes=[pltpu.SemaphoreType.DMA((n,)), pltpu.SemaphoreType.DMA((n,))],
    )(*arrs)


def _sibling_join(pieces, out_shapes, name):
    n = len(pieces)
    no = len(out_shapes)

    def body(*refs):
        ins, outs = refs[:n], refs[n:n + no]
        send_sems, recv_sems = refs[n + no:]
        x, y, c = _coords()
        sibling = (x, y, 1 - c)
        cps = [_remote(ins[k], outs[o].at[lead + (c,)], send_sems.at[k], recv_sems.at[k], sibling)
               for k, (_, o, lead) in enumerate(pieces)]
        for cp in cps:
            cp.start()
        for k, (_, o, lead) in enumerate(pieces):
            blk = outs[o].at[lead + (1 - c,)]
            _remote(blk, blk, send_sems.at[k], recv_sems.at[k], sibling).wait_recv()
        for cp in cps:
            cp.wait_send()

    outs = pl.pallas_call(
        body, name=name,
        out_shape=[jax.ShapeDtypeStruct(sh, F32) for sh in out_shapes],
        in_specs=[ANY] * n, out_specs=[ANY] * no,
        scratch_shapes=[pltpu.SemaphoreType.DMA((n,)), pltpu.SemaphoreType.DMA((n,))],
    )(*[a for a, _, _ in pieces])
    outs = list(outs)
    c = lax.axis_index("c")
    for a, o, lead in pieces:
        block = a.reshape((1,) * (len(lead) + 1) + a.shape)
        outs[o] = lax.dynamic_update_slice(outs[o], block, lead + (c,) + (0,) * a.ndim)
    return outs


def _scatter_sems(n):
    return [pltpu.SemaphoreType.DMA((3 * n,)), pltpu.SemaphoreType.DMA((3 * n,))]


def _scatter_copies(ins, outs, sems):
    send_sems, recv_sems = sems
    x, y, c = _coords()
    me_chip = 2 * x + y
    sends, arrivals = [], []
    for k in range(len(ins)):
        for j, (cx, cy) in enumerate(_other_chips(x, y)):
            i = 3 * k + j
            sends.append(_remote(ins[k].at[2 * cx + cy], outs[k].at[me_chip], send_sems.at[i], recv_sems.at[i], (cx, cy, c)))
            blk = outs[k].at[2 * cx + cy]
            arrivals.append(_remote(blk, blk, send_sems.at[i], recv_sems.at[i], (cx, cy, c)))
    return sends, arrivals


def _scatter_start(ins, outs, sems):
    for cp in _scatter_copies(ins, outs, sems)[0]:
        cp.start()


def _scatter_finish(ins, outs, sems):
    sends, arrivals = _scatter_copies(ins, outs, sems)
    for cp in arrivals:
        cp.wait_recv()
    for cp in sends:
        cp.wait_send()


def _chip_scatter(arrs, name):
    n = len(arrs)

    def body(*refs):
        ins, outs, sems = refs[:n], refs[n:2 * n], refs[2 * n:]
        _scatter_start(ins, outs, sems)
        _scatter_finish(ins, outs, sems)

    outs = pl.pallas_call(
        body, name=name,
        out_shape=[jax.ShapeDtypeStruct(a.shape, a.dtype) for a in arrs],
        in_specs=[ANY] * n, out_specs=[ANY] * n, scratch_shapes=_scatter_sems(n),
    )(*arrs)
    return outs


def _as_rows(a, lanes):
    return a.reshape(-1, lanes)


def _add_my_half(v, recv, out_dtype, name):
    n_sh, _, h, cdim = v.shape
    tr = _row_tile(h, 512)

    def body(c_ref, a_ref, b_ref, o_ref):
        o_ref[...] = (a_ref[...].astype(F32) + b_ref[...].astype(F32)).astype(out_dtype)

    c = lax.axis_index("c").astype(jnp.int32).reshape(1)
    return pl.pallas_call(
        body, name=name,
        grid_spec=pltpu.PrefetchScalarGridSpec(
            num_scalar_prefetch=1, grid=(n_sh, h // tr),
            in_specs=[pl.BlockSpec((None, None, tr, cdim), lambda sh, i, c_ref: (sh, c_ref[0], i, 0)),
                      pl.BlockSpec((None, tr, cdim), lambda sh, i, c_ref: (sh, i, 0))],
            out_specs=pl.BlockSpec((None, tr, cdim), lambda sh, i, c_ref: (sh, i, 0))),
        out_shape=jax.ShapeDtypeStruct((n_sh, h, cdim), out_dtype),
        compiler_params=_cparams(2),
    )(c, v, recv)


def _sum4(got, own, name):
    _, r, cdim = got.shape
    tr = _row_tile(r, 512)

    def body(me_ref, p_ref, own_ref, o_ref):
        mine = own_ref[...].astype(F32)
        acc = None
        for j in range(N_CHIPS):
            term = jnp.where(me_ref[0] == j, mine, p_ref[j].astype(F32))
            acc = term if acc is None else acc + term
        o_ref[...] = acc

    me = (2 * lax.axis_index("x") + lax.axis_index("y")).astype(jnp.int32).reshape(1)
    return pl.pallas_call(
        body, name=name,
        grid_spec=pltpu.PrefetchScalarGridSpec(
            num_scalar_prefetch=1, grid=(r // tr,),
            in_specs=[pl.BlockSpec((N_CHIPS, tr, cdim), lambda i, me_ref: (0, i, 0)),
                      pl.BlockSpec((None, tr, cdim), lambda i, me_ref: (me_ref[0], i, 0))],
            out_specs=pl.BlockSpec((tr, cdim), lambda i, me_ref: (i, 0))),
        out_shape=jax.ShapeDtypeStruct((r, cdim), F32),
        compiler_params=_cparams(1),
    )(me, got, own)


def _adamw_math(w, g, m, v):
    c1 = 1.0 - ADAM_B1 ** ADAM_STEP
    c2 = 1.0 - ADAM_B2 ** ADAM_STEP
    nm = ADAM_B1 * m + (1.0 - ADAM_B1) * g
    nv = ADAM_B2 * v + (1.0 - ADAM_B2) * (g * g)
    delta = -ADAM_LR * ((nm / c1) / (jnp.sqrt(nv / c2) + ADAM_EPS) + ADAM_WD * w)
    return delta, nm, nv


def _adamw(w, g, m, v, name):
    r, cdim = w.shape
    tr = _row_tile(r, 256)

    def body(w_ref, g_ref, m_ref, v_ref, d_ref, nm_ref, nv_ref):
        d_ref[...], nm_ref[...], nv_ref[...] = _adamw_math(w_ref[...], g_ref[...], m_ref[...], v_ref[...])

    spec = pl.BlockSpec((tr, cdim), lambda i: (i, 0))
    return pl.pallas_call(
        body, name=name, grid=(r // tr,),
        in_specs=[spec] * 4, out_specs=[spec] * 3,
        out_shape=[jax.ShapeDtypeStruct((r, cdim), F32)] * 3,
        compiler_params=_cparams(1),
    )(w, g, m, v)


def _adamw_many(ws, gs, ms, vs, name, per_layer):
    n = len(ws)

    def body(*refs):
        for k in range(n):
            w, g, m, v = (refs[j * n + k][...] for j in range(4))
            outs = _adamw_math(w, g, m, v)
            for j in range(3):
                refs[(4 + j) * n + k][...] = outs[j]

    shapes = [jax.ShapeDtypeStruct(w.shape, F32) for w in ws]
    if per_layer:
        specs = [pl.BlockSpec((None,) + w.shape[1:], lambda l, nd=w.ndim: (l,) + (0,) * (nd - 1)) for w in ws]
        call = pl.pallas_call(body, name=name, grid=(N_LAYERS,), in_specs=specs * 4, out_specs=specs * 3,
                              out_shape=shapes * 3, compiler_params=_cparams(1))
    else:
        call = pl.pallas_call(body, name=name, out_shape=shapes * 3, compiler_params=_cparams())
    outs = call(*ws, *gs, *ms, *vs)
    return outs[0:n], outs[n:2 * n], outs[2 * n:3 * n]


def _cmul(ar, ai, br, bi):
    return ar * br - ai * bi, ar * bi + ai * br


def _discretise(a_re, a_im, log_dt, b_re, b_im):
    dt = jnp.exp(log_dt)
    mag = jnp.exp(a_re * dt)
    ab_re = mag * jnp.cos(a_im * dt)
    ab_im = mag * jnp.sin(a_im * dt)
    num_re = ab_re - 1.0
    num_im = ab_im
    den = a_re * a_re + a_im * a_im
    f_re = (num_re * a_re + num_im * a_im) / den
    f_im = (num_im * a_re - num_re * a_im) / den
    bb_re = f_re * b_re - f_im * b_im
    bb_im = f_re * b_im + f_im * b_re
    return ab_re, ab_im, bb_re, bb_im


def _disc_shapes():
    col = jax.ShapeDtypeStruct((1, N_STATES), F32)
    mat = jax.ShapeDtypeStruct((SSM_GROUP, N_STATES), F32)
    return col, mat


def _group_mask():
    row = lax.broadcasted_iota(jnp.int32, (CH_W, CH_S), 0)
    col = lax.broadcasted_iota(jnp.int32, (CH_W, CH_S), 1)
    return jnp.right_shift(row, SSM_GROUP.bit_length() - 1) == jnp.right_shift(col, SSM_STATE.bit_length() - 1)


def _block_diag(v, j):
    blk = v[:, CH_S * j:CH_S * (j + 1)]
    return jnp.where(_group_mask(), jnp.concatenate([blk] * (CH_W // SSM_GROUP), axis=0), 0.0)


def _block_diag_t(m):
    kept = jnp.where(_group_mask(), m, 0.0)
    return kept.reshape(CH_W // SSM_GROUP, SSM_GROUP, CH_S).sum(axis=0)


def _disc_fwd(a_re, a_im, log_dt, b_re, b_im, c_re, c_im, length):
    wide = jax.ShapeDtypeStruct((SSM_CHUNKS, CH_W, 2 * CH_S), BF16)
    tall = jax.ShapeDtypeStruct((SSM_CHUNKS, 2 * CH_S, CH_W), BF16)
    tab = jax.ShapeDtypeStruct((SSM_CHUNKS, length, 2 * CH_S), F32)

    def body(ar, ai, ld, br, bi, cr, ci, wb_ref, wbt_ref, wct_ref, wc_ref, tab_ref, rev_ref):
        ab_re, ab_im, bb_re, bb_im = _discretise(ar[...], ai[...], ld[...], br[...], bi[...])
        ccr, cci = cr[...], -ci[...]
        for j in range(SSM_CHUNKS):
            for lo, (vb, vc) in ((0, (bb_re, ccr)), (CH_S, (bb_im, cci))):
                mb, mc = _block_diag(vb, j), _block_diag(vc, j)
                wb_ref[j, :, lo:lo + CH_S] = mb.astype(BF16)
                wbt_ref[j, lo:lo + CH_S, :] = mb.T.astype(BF16)
                wct_ref[j, :, lo:lo + CH_S] = mc.astype(BF16)
                wc_ref[j, lo:lo + CH_S, :] = mc.T.astype(BF16)

        def step(j, carry):
            pr, pi = carry
            back = length - 1 - j
            for c in range(SSM_CHUNKS):
                lanes = slice(CH_S * c, CH_S * (c + 1))
                tab_ref[c, pl.ds(j, 1), 0:CH_S] = pr[:, lanes]
                tab_ref[c, pl.ds(j, 1), CH_S:2 * CH_S] = pi[:, lanes]
                rev_ref[c, pl.ds(back, 1), 0:CH_S] = pr[:, lanes]
                rev_ref[c, pl.ds(back, 1), CH_S:2 * CH_S] = -pi[:, lanes]
            return _cmul(pr, pi, ab_re, ab_im)

        lax.fori_loop(0, length, step, (ab_re, ab_im))

    return pl.pallas_call(body, name="ssm_discretise", out_shape=[wide, tall, wide, tall, tab, tab],
                          compiler_params=_cparams())(a_re, a_im, log_dt, b_re, b_im, c_re, c_im)


def _disc_bwd(a_re, a_im, log_dt, b_re, b_im, da, dwb, dwc):
    col, mat = _disc_shapes()

    def body(ar, ai, ld, br, bi, da_ref, dwb_ref, dwc_ref, o0, o1, o2, o3, o4, dcr_ref, dci_ref):
        g_ab = [jnp.concatenate([jnp.sum(da_ref[j, :, lo:lo + CH_S], axis=0, keepdims=True) for j in range(SSM_CHUNKS)],
                                axis=-1) for lo in (0, CH_S)]
        g_bb = [jnp.concatenate([_block_diag_t(dwb_ref[j, :, lo:lo + CH_S]) for j in range(SSM_CHUNKS)], axis=-1)
                for lo in (0, CH_S)]
        for ref, lo, sign in ((dcr_ref, 0, 1.0), (dci_ref, CH_S, -1.0)):
            ref[...] = sign * jnp.concatenate([_block_diag_t(dwc_ref[j, lo:lo + CH_S, :].T) for j in range(SSM_CHUNKS)],
                                              axis=-1)
        _, vjp = jax.vjp(_discretise, ar[...], ai[...], ld[...], br[...], bi[...])
        grads = vjp((g_ab[0], g_ab[1], g_bb[0], g_bb[1]))
        for o, val in zip((o0, o1, o2, o3, o4), grads):
            o[...] = val

    return pl.pallas_call(body, name="ssm_discretise_bwd", out_shape=[col, col, col, mat, mat, mat, mat],
                          compiler_params=_cparams())(a_re, a_im, log_dt, b_re, b_im, da, dwb, dwc)


def _interleave_chunks(v):
    rows, width = v.shape
    return pltpu.einshape("cjw->jcw", v.reshape(SUBLANES, rows // SUBLANES, width)).reshape(rows, width)


def _time_order(v):
    rows, width = v.shape
    return pltpu.einshape("jcw->cjw", v.reshape(rows // SUBLANES, SUBLANES, width)).reshape(rows, width)


def _head_ones():
    r = jnp.arange(ATTN_WIDTH) // HEAD_DIM
    return jnp.where(r[:, None] == r[None, :], 1.0 / HEAD_DIM, 0.0).astype(BF16)


def _in_proj(h, g1, w_in_l, qg, kg):
    s = h.shape[0]
    tm = _row_tile(s, 512)

    def body(h_ref, g_ref, w_ref, qg_ref, kg_ref, ones_ref, proj_ref, qkv_ref):
        x = h_ref[...]
        r = lax.rsqrt(jnp.mean(x * x, axis=-1, keepdims=True) + RMS_EPS)
        hn = (x * r * g_ref[...]).astype(BF16)
        for sh in range(N_CHIPS):
            proj_ref[:, IN_SHARD * sh:IN_SHARD * (sh + 1)] = _dot(hn, w_ref[sh])
        ones = ones_ref[...]
        q = proj_ref[:, 1024:1536]
        k = proj_ref[:, 1536:2048]
        rq = lax.rsqrt(_dot_hilo(q * q, ones) + RMS_EPS)
        rk = lax.rsqrt(_dot_hilo(k * k, ones) + RMS_EPS)
        qkv_ref[:, 0:512] = (q * rq * qg_ref[...] * ATTN_SCALE).astype(BF16)
        qkv_ref[:, 512:1024] = (k * rk * kg_ref[...]).astype(BF16)
        qkv_ref[:, 1024:1536] = proj_ref[:, 2048:2560].astype(BF16)

    full = lambda shape: pl.BlockSpec(shape, lambda i: (0,) * len(shape))
    return pl.pallas_call(
        body, name="in_proj", grid=(s // tm,),
        in_specs=[pl.BlockSpec((tm, D_MODEL), lambda i: (i, 0)), full((1, D_MODEL)),
                  full((N_CHIPS, D_MODEL, IN_SHARD)),
                  full((1, ATTN_WIDTH)), full((1, ATTN_WIDTH)), full((ATTN_WIDTH, ATTN_WIDTH))],
        out_specs=[pl.BlockSpec((tm, IN_COLS), lambda i: (i, 0)), pl.BlockSpec((tm, 3 * ATTN_WIDTH), lambda i: (i, 0))],
        out_shape=[jax.ShapeDtypeStruct((s, IN_COLS), F32), jax.ShapeDtypeStruct((s, 3 * ATTN_WIDTH), BF16)],
        compiler_params=_cparams(1),
    )(h, g1, w_in_l, qg, kg, _head_ones())


def _row_bcast(ref, k, lo):
    return jnp.broadcast_to(ref[pl.ds(k, 1), lo:lo + CH_S], (SUBLANES, CH_S))


def _chunk_scan(x_ref, tab_ref, carry_ref, length, reverse, tail=None):
    row = lax.broadcasted_iota(jnp.int32, (SUBLANES, CH_S), 0)
    one, full = (length - 1, 0) if reverse else (0, length - 1)
    ar, ai = _row_bcast(tab_ref, one, 0), _row_bcast(tab_ref, one, CH_S)
    fr, fi = _row_bcast(tab_ref, full, 0), _row_bcast(tab_ref, full, CH_S)
    step = lambda jj: (length - 1 - jj) if reverse else jj

    def local(jj, carry):
        cr, ci = carry
        r0 = pl.multiple_of(step(jj) * SUBLANES, SUBLANES)
        xr = x_ref[pl.ds(r0, SUBLANES), 0:CH_S] + (ar * cr - ai * ci)
        xi = x_ref[pl.ds(r0, SUBLANES), CH_S:2 * CH_S] + (ar * ci + ai * cr)
        x_ref[pl.ds(r0, SUBLANES), 0:CH_S] = xr
        x_ref[pl.ds(r0, SUBLANES), CH_S:2 * CH_S] = xi
        return xr, xi

    zero = jnp.zeros((SUBLANES, CH_S), F32)
    er, ei = lax.fori_loop(0, length, local, (zero, zero))

    first, shift = (SUBLANES - 1, SUBLANES - 1) if reverse else (0, 1)
    hr = jnp.where(row == first, carry_ref[:, 0:CH_S], 0.0)
    hi = jnp.where(row == first, carry_ref[:, CH_S:2 * CH_S], 0.0)
    sr, si = pltpu.roll(er, shift, 0), pltpu.roll(ei, shift, 0)
    for k in range(1, SUBLANES):
        tr, ti = pltpu.roll(hr, shift, 0), pltpu.roll(hi, shift, 0)
        here = row == ((SUBLANES - 1 - k) if reverse else k)
        hr, hi = (jnp.where(here, fr * tr - fi * ti + sr, hr), jnp.where(here, fr * ti + fi * tr + si, hi))
    last = 0 if reverse else SUBLANES - 1
    outr, outi = fr * hr - fi * hi + er, fr * hi + fi * hr + ei
    carry_ref[:, 0:CH_S] = jnp.broadcast_to(outr[last:last + 1, :], (SUBLANES, CH_S))
    carry_ref[:, CH_S:2 * CH_S] = jnp.broadcast_to(outi[last:last + 1, :], (SUBLANES, CH_S))

    def fix(jj, carry):
        j = step(jj)
        r0 = pl.multiple_of(j * SUBLANES, SUBLANES)
        pr, pi = _row_bcast(tab_ref, j, 0), _row_bcast(tab_ref, j, CH_S)
        xr = x_ref[pl.ds(r0, SUBLANES), 0:CH_S] + (pr * hr - pi * hi)
        xi = x_ref[pl.ds(r0, SUBLANES), CH_S:2 * CH_S] + (pr * hi + pi * hr)
        x_ref[pl.ds(r0, SUBLANES), 0:CH_S] = xr
        x_ref[pl.ds(r0, SUBLANES), CH_S:2 * CH_S] = xi
        if tail is None:
            return carry
        return tail(r0, xr, xi, carry)

    return fix, (hr, hi)


def _ssm_scan_fwd(proj, wb, tab, wc, gather=None, gather_bases=None):
    s = proj.shape[0]
    tm = _row_tile(s, SCAN_TILE)
    nt = s // tm
    length = tm // SUBLANES
    gather = [] if gather is None else gather
    ng = len(gather)

    def body(*refs):
        u_ref, wb_ref, tab_ref, wc_ref = refs[0:4]
        g_ins = refs[4:4 + ng]
        xs_ref, y_ref = refs[4 + ng:6 + ng]
        g_outs = refs[6 + ng:6 + 2 * ng]
        carry_ref = refs[6 + 2 * ng]
        sems = refs[7 + 2 * ng:]
        j, i = pl.program_id(0), pl.program_id(1)

        @pl.when(i == 0)
        def _():
            carry_ref[...] = jnp.zeros_like(carry_ref)

        if ng:
            @pl.when(jnp.logical_and(j == 0, i == 0))
            def _():
                _gather_start(g_ins, gather_bases, g_outs, sems)

        xs_ref[...] = _dot(_interleave_chunks(u_ref[...]).astype(BF16), wb_ref[...])
        fix, start = _chunk_scan(xs_ref, tab_ref, carry_ref, length, reverse=False)
        lax.fori_loop(0, length, fix, start, unroll=2)
        y_ref[...] = _time_order(_dot(xs_ref[...].astype(BF16), wc_ref[...]))

        if ng:
            @pl.when(jnp.logical_and(j == SSM_CHUNKS - 1, i == nt - 1))
            def _():
                _gather_finish(g_ins, gather_bases, g_outs, sems)

    outs = pl.pallas_call(
        body, name="ssm_scan_gather" if ng else "ssm_scan", grid=(SSM_CHUNKS, nt),
        in_specs=[pl.BlockSpec((tm, CH_W), lambda j, i: (i, j)),
                  pl.BlockSpec((None, CH_W, 2 * CH_S), lambda j, i: (j, 0, 0)),
                  pl.BlockSpec((None, length, 2 * CH_S), lambda j, i: (j, 0, 0)),
                  pl.BlockSpec((None, 2 * CH_S, CH_W), lambda j, i: (j, 0, 0))] + [ANY] * ng,
        out_specs=[pl.BlockSpec((None, tm, 2 * CH_S), lambda j, i: (j, i, 0)),
                   pl.BlockSpec((tm, CH_W), lambda j, i: (i, j))] + [ANY] * ng,
        out_shape=[jax.ShapeDtypeStruct((SSM_CHUNKS, s, 2 * CH_S), F32), jax.ShapeDtypeStruct((s, SSM_WIDTH), F32)]
        + _gather_outputs(gather),
        scratch_shapes=[pltpu.VMEM((SUBLANES, 2 * CH_S), F32)] + (_gather_sems(ng) if ng else []),
        compiler_params=_cparams(2),
    )(proj, wb, tab, wc, *gather)
    return outs[0], outs[1], (_gather_own(outs[2:], gather, gather_bases) if ng else [])


def _glu_forward(y, u, d, wg_ref, bg):
    yf = y + d * u
    z = _gelu(yf)
    zb = z.astype(BF16)
    zz = jnp.concatenate([_dot(zb, wg_ref[sh]) for sh in range(N_CHIPS)], axis=-1) + bg
    return yf, z, zz[:, 0:SSM_WIDTH], zz[:, SSM_WIDTH:2 * SSM_WIDTH]


def _ssm_glu_fwd(y, proj, d, w_glu_l, b_glu):
    s = y.shape[0]
    tm = _row_tile(s, 1024)

    def body(y_ref, u_ref, gs_ref, d_ref, wg_ref, bg_ref, o_ref):
        _, _, val, gate = _glu_forward(y_ref[...], u_ref[...], d_ref[...], wg_ref, bg_ref[...])
        gs = gs_ref[...]
        o_ref[...] = val * _sigmoid(gate) * (gs * _sigmoid(gs))

    row = lambda i: (i, 0)
    return pl.pallas_call(
        body, name="ssm_glu", grid=(s // tm,),
        in_specs=[pl.BlockSpec((tm, SSM_WIDTH), row), pl.BlockSpec((tm, SSM_WIDTH), row),
                  pl.BlockSpec((tm, SSM_WIDTH), lambda i: (i, 1)), pl.BlockSpec((1, SSM_WIDTH), lambda i: (0, 0)),
                  pl.BlockSpec((N_CHIPS, SSM_WIDTH, ROW_SHARD), lambda i: (0, 0, 0)),
                  pl.BlockSpec((1, 2 * SSM_WIDTH), lambda i: (0, 0))],
        out_specs=pl.BlockSpec((tm, SSM_WIDTH), row),
        out_shape=jax.ShapeDtypeStruct((s, SSM_WIDTH), F32),
        compiler_params=_cparams(1),
    )(y, proj, proj, d, w_glu_l, b_glu)


def _tri(kind):
    r = jnp.arange(ATTN_BLOCK)
    if kind == "suffix_incl":
        m = r[:, None] >= r[None, :]
    else:
        m = r[:, None] < r[None, :]
    return jnp.concatenate([m, jnp.ones_like(m)], axis=1).astype(BF16)


def _head_masks():
    lane = lax.broadcasted_iota(jnp.int32, (1, 2 * HEAD_DIM), 1)
    return [lane < HEAD_DIM, lane >= HEAD_DIM]


def _chain_step(t, base, n_sub, first, q_ref, k_ref, tri_ref, l_scr, per_chain):
    tb = ATTN_BLOCK
    row = lax.broadcasted_iota(jnp.int32, (tb, tb), 0)
    col = lax.broadcasted_iota(jnp.int32, (tb, tb), 1)
    masks = _head_masks()
    blks = [base + a - t for a in range(n_sub)]
    r0s = [pl.multiple_of(jnp.maximum(blk, 0) * tb, tb) for blk in blks]
    zs = []
    for a in range(n_sub):
        kb = k_ref[pl.ds(r0s[a], tb), :]
        qa = q_ref[a * tb:(a + 1) * tb, :]
        for mask in masks:
            zs.append(_dot_nt(jnp.where(mask, qa, jnp.zeros_like(qa)), kb))
    parts = []
    for z in zs:
        ls = jnp.minimum(-z, 0.0) - jnp.log(1.0 + jnp.exp(-jnp.abs(z)))
        if first:
            ls = jnp.where(col < row, ls, 0.0)
        parts.append(_split_hilo(ls))
    tri = tri_ref[...]
    sums = [_dot(hi, tri) + _dot(lo, tri) for hi, lo in parts]
    top = None
    ws = []
    for c, (z, sm) in enumerate(zip(zs, sums)):
        if first:
            lsum = jnp.zeros((tb, tb), F32)
        else:
            lsum = l_scr[c] + jnp.where(blks[c // 2] >= 0, 0.0, -1e30)
        w = jnp.exp(z + sm[:, 0:tb] + lsum)
        if first:
            w = jnp.where(col < row, w, 0.0)
        ws.append(w)
        lsum = lsum + sm[:, tb:2 * tb]
        l_scr[c] = lsum
        top = lsum if top is None else jnp.maximum(top, lsum)
    for c, (z, w) in enumerate(zip(zs, ws)):
        per_chain(c // 2, c % 2, c, r0s[c // 2], z, w)
    return jnp.max(top)


def _chain_sweep(base, n_sub, q_ref, k_ref, tri_ref, l_scr, per_chain):
    top = _chain_step(0, base, n_sub, True, q_ref, k_ref, tri_ref, l_scr, functools.partial(per_chain, 0))

    def cond(carry):
        t, top = carry
        return jnp.logical_and(t <= base + n_sub - 1, top > EXP_ZERO)

    def step(carry):
        t, _ = carry
        return t + 1, _chain_step(t, base, n_sub, False, q_ref, k_ref, tri_ref, l_scr, functools.partial(per_chain, t))

    steps, _ = lax.while_loop(cond, step, (jnp.int32(1), top))
    return steps


ATTN_SUB_FWD = 8
ATTN_SUB_BWD = 8


def _attn_fwd(qkv, proj, gather=None, gather_bases=None):
    s = qkv.shape[0]
    tb = ATTN_BLOCK
    n_sub = min(ATTN_SUB_FWD, s // tb)
    tq = n_sub * tb
    n_hp = ATTN_WIDTH // (2 * HEAD_DIM)
    gather = [] if gather is None else gather
    ng = len(gather)

    def body(*refs):
        q_ref, k_ref, v_ref, g_ref, tri_ref = refs[0:5]
        g_ins = refs[5:5 + ng]
        o_ref, ya_ref = refs[5 + ng:7 + ng]
        g_outs = refs[7 + ng:7 + 2 * ng]
        l_scr = refs[7 + 2 * ng]
        sems = refs[8 + 2 * ng:]
        i = pl.program_id(1)
        masks = _head_masks()
        o_ref[...] = jnp.zeros_like(o_ref)

        if ng:
            @pl.when(jnp.logical_and(pl.program_id(0) == 0, i == 0))
            def _():
                _gather_start(g_ins, gather_bases, g_outs, sems)

        def per_chain(t, a, h, c, r0, z, w):
            vb = v_ref[pl.ds(r0, tb), :]
            vb = jnp.where(masks[h], vb, jnp.zeros_like(vb))
            o_ref[a * tb:(a + 1) * tb, :] += _dot(w.astype(BF16), vb)

        _chain_sweep(i * n_sub, n_sub, q_ref, k_ref, tri_ref, l_scr, per_chain)
        g = g_ref[...]
        ya_ref[...] = o_ref[...] * (g * _sigmoid(g))

        if ng:
            @pl.when(jnp.logical_and(pl.program_id(0) == n_hp - 1, i == s // tq - 1))
            def _():
                _gather_finish(g_ins, gather_bases, g_outs, sems)

    hp_blk = lambda off: pl.BlockSpec((tq, 2 * HEAD_DIM), lambda hp, i: (i, off + hp))
    res = lambda off: pl.BlockSpec((s, 2 * HEAD_DIM), lambda hp, i: (0, off + hp))
    outs = pl.pallas_call(
        body, name="attn_fwd_gather" if ng else "attn_fwd", grid=(n_hp, s // tq),
        in_specs=[hp_blk(0), res(4), res(8), hp_blk(20), pl.BlockSpec((tb, 2 * tb), lambda hp, i: (0, 0))] + [ANY] * ng,
        out_specs=[hp_blk(0), hp_blk(0)] + [ANY] * ng,
        out_shape=[jax.ShapeDtypeStruct((s, ATTN_WIDTH), F32)] * 2 + _gather_outputs(gather),
        scratch_shapes=[pltpu.VMEM((2 * n_sub, tb, tb), F32)] + (_gather_sems(ng) if ng else []),
        compiler_params=_cparams(2),
    )(qkv, qkv, qkv, proj, _tri("suffix_incl"), *gather)
    return outs[0], outs[1], (_gather_own(outs[2:], gather, gather_bases) if ng else [])


def _rms_rows(x, g):
    r = lax.rsqrt(jnp.mean(x * x, axis=-1, keepdims=True) + RMS_EPS)
    return r, x * r * g


def _ple_forward(h1, p, g2, wpg_ref, wpp_ref):
    r2, hn2 = _rms_rows(h1, g2)
    hb = hn2.astype(BF16)
    gpre = _dot(hb[:, 0:ROW_SHARD], wpg_ref[0])
    for sh in range(1, N_CHIPS):
        gpre = gpre + _dot(hb[:, ROW_SHARD * sh:ROW_SHARD * (sh + 1)], wpg_ref[sh])
    gate = _sigmoid(gpre)
    pb = p.astype(BF16)
    pp = jnp.concatenate([_dot(pb, wpp_ref[sh]) for sh in range(N_CHIPS)], axis=-1)
    return r2, hb, gate, pp


def _colsum8(a):
    t = a.shape[0]
    return a.reshape(t // SUBLANES, SUBLANES, a.shape[1]).sum(axis=0)


def _sq_err_grad(y, target):
    e = y - target
    sq = _colsum8(e * e)
    part = sq[:, 0:128]
    for b in range(1, D_MODEL // 128):
        part = part + sq[:, 128 * b:128 * (b + 1)]
    return e / D_MODEL, part


def _out_ple(h, ys, ya, p, g2, w_out_l, w_pg_l, w_pp_l, target=None):
    s = h.shape[0]
    tm = _row_tile(s, 512)
    last = target is not None

    def body(*refs):
        h_ref, ys_ref, ya_ref, p_ref, g_ref, wo_ref, wpg_ref, wpp_ref = refs[0:8]
        h1_ref, h2_ref = refs[8 + last], refs[9 + last]
        ysb = ys_ref[...].astype(BF16)
        yab = ya_ref[...].astype(BF16)
        h1 = h_ref[...]
        for sh, src in enumerate((ysb[:, 0:ROW_SHARD], ysb[:, ROW_SHARD:], yab[:, 0:ROW_SHARD], yab[:, ROW_SHARD:])):
            h1 = h1 + _dot(src, wo_ref[sh])
        _, _, gate, pp = _ple_forward(h1, p_ref[...], g_ref[...], wpg_ref, wpp_ref)
        h1_ref[...] = h1
        h2 = h1 + gate * pp
        if last:
            acc_ref = refs[11]

            @pl.when(pl.program_id(0) == 0)
            def _():
                acc_ref[...] = jnp.zeros_like(acc_ref)

            h2_ref[...], part = _sq_err_grad(h2, refs[8][...])
            acc_ref[...] += part
        else:
            h2_ref[...] = h2

    row = lambda i: (i, 0)
    big = pl.BlockSpec((tm, D_MODEL), row)
    wspec = lambda r, cdim: pl.BlockSpec((N_CHIPS, r, cdim), lambda i: (0, 0, 0))
    acc = pl.BlockSpec((SUBLANES, 128), lambda i: (0, 0))
    return pl.pallas_call(
        body, name="out_ple_loss" if last else "out_ple", grid=(s // tm,),
        in_specs=[big, pl.BlockSpec((tm, SSM_WIDTH), row), pl.BlockSpec((tm, ATTN_WIDTH), row),
                  pl.BlockSpec((tm, PLE_DIM), row), pl.BlockSpec((1, D_MODEL), lambda i: (0, 0)),
                  wspec(ROW_SHARD, D_MODEL), wspec(ROW_SHARD, D_MODEL), wspec(PLE_DIM, ROW_SHARD)] + [big] * last,
        out_specs=[big] * 2 + [acc] * last,
        out_shape=[jax.ShapeDtypeStruct((s, D_MODEL), F32)] * 2 + [jax.ShapeDtypeStruct((SUBLANES, 128), F32)] * last,
        compiler_params=_cparams(1),
    )(h, ys, ya, p, g2, w_out_l, w_pg_l, w_pp_l, *([target] if last else []))


def _rms_bwd(x, r, g, dy):
    gdy = g * dy
    dx = r * gdy - x * (r * r * r) * jnp.mean(x * gdy, axis=-1, keepdims=True)
    return dx, x * r * dy


def _out_ple_bwd(dh2, h1, p, g2, w_out_l, w_pg_l, w_pp_l):
    s = h1.shape[0]
    tm = _row_tile(s, 512)

    def body(dh2_ref, h1_ref, p_ref, g_ref, wo_ref, wpg_ref, wpp_ref,
             dh1_ref, dmix_ref, hn_ref, dgp_ref, dpp_ref, dh1b_ref, dg_ref):
        @pl.when(pl.program_id(0) == 0)
        def _():
            dg_ref[...] = jnp.zeros_like(dg_ref)

        h1 = h1_ref[...]
        dh2 = dh2_ref[...]
        g2v = g_ref[...]
        r2, hb, gate, pp = _ple_forward(h1, p_ref[...], g2v, wpg_ref, wpp_ref)
        dgp = (dh2 * pp) * gate * (1.0 - gate)
        dgpb = dgp.astype(BF16)
        dhn = jnp.concatenate([_dot_nt(dgpb, wpg_ref[sh]) for sh in range(N_CHIPS)], axis=-1)
        dx, dgrow = _rms_bwd(h1, r2, g2v, dhn)
        dh1 = dh2 + dx
        dh1b = dh1.astype(BF16)
        dh1_ref[...] = dh1
        dh1b_ref[...] = dh1b
        hn_ref[...] = hb
        dgp_ref[...] = dgpb
        dpp_ref[...] = (dh2 * gate).astype(BF16)
        dg_ref[...] += _colsum8(dgrow)
        for sh in range(N_CHIPS):
            dmix_ref[:, ROW_SHARD * sh:ROW_SHARD * (sh + 1)] = _dot_nt(dh1b, wo_ref[sh])

    row = lambda i: (i, 0)
    wspec = lambda r, cdim: pl.BlockSpec((N_CHIPS, r, cdim), lambda i: (0, 0, 0))
    big = pl.BlockSpec((tm, D_MODEL), row)
    return pl.pallas_call(
        body, name="out_ple_bwd", grid=(s // tm,),
        in_specs=[big, big, pl.BlockSpec((tm, PLE_DIM), row), pl.BlockSpec((1, D_MODEL), lambda i: (0, 0)),
                  wspec(ROW_SHARD, D_MODEL), wspec(ROW_SHARD, D_MODEL), wspec(PLE_DIM, ROW_SHARD)],
        out_specs=[big] * 6 + [pl.BlockSpec((SUBLANES, D_MODEL), lambda i: (0, 0))],
        out_shape=[jax.ShapeDtypeStruct((s, D_MODEL), F32)] * 2 + [jax.ShapeDtypeStruct((s, D_MODEL), BF16)] * 4
        + [jax.ShapeDtypeStruct((SUBLANES, D_MODEL), F32)],
        compiler_params=_cparams(1),
    )(dh2, h1, p, g2, w_out_l, w_pg_l, w_pp_l)


def _tn_matmul(a, b, n_blocks, block_a, name, into=None, first_block=0, total_blocks=None):
    s = a.shape[0]
    tk = _row_tile(s, 1024)
    nk = s // tk
    total_blocks = n_blocks if total_blocks is None else total_blocks
    ka, nb = a.shape[1], b.shape[1]
    if block_a:
        ka //= n_blocks
    else:
        nb //= n_blocks

    def body(*refs):
        a_ref, b_ref, o_ref, acc_ref = refs[0], refs[1], refs[-2], refs[-1]

        @pl.when(pl.program_id(0) == 0)
        def _():
            acc_ref[...] = jnp.zeros_like(acc_ref)

        at = a_ref[...].astype(BF16).T
        bb = b_ref[...].astype(BF16)
        for sh in range(n_blocks):
            if block_a:
                acc_ref[sh] += _dot(at[ka * sh:ka * (sh + 1), :], bb)
            else:
                acc_ref[sh] += _dot(at, bb[:, nb * sh:nb * (sh + 1)])

        @pl.when(pl.program_id(0) == nk - 1)
        def _():
            o_ref[...] = acc_ref[...].astype(BF16)

    in_specs = [pl.BlockSpec((tk, a.shape[1]), lambda i: (i, 0)), pl.BlockSpec((tk, b.shape[1]), lambda i: (i, 0))]
    operands = [a, b]
    aliases = {}
    if into is not None:
        in_specs.append(ANY)
        operands.append(into)
        aliases = {2: 0}
    return pl.pallas_call(
        body, name=name, grid=(nk,),
        in_specs=in_specs,
        out_specs=pl.BlockSpec((n_blocks, ka, nb), lambda i: (first_block // n_blocks, 0, 0)),
        out_shape=jax.ShapeDtypeStruct((total_blocks, ka, nb), BF16),
        scratch_shapes=[pltpu.VMEM((n_blocks, ka, nb), F32)],
        input_output_aliases=aliases,
        compiler_params=_cparams(1),
    )(*operands)


def _attn_bwd(qkv, o, proj, dmix, scatter=None):
    scatter = [] if scatter is None else scatter
    nsc = len(scatter)
    s = qkv.shape[0]
    tb = ATTN_BLOCK
    nq = s // tb
    n_sub = min(ATTN_SUB_BWD, nq)
    tq = n_sub * tb
    n_chain = 2 * n_sub

    def body(*refs):
        q_ref, k_ref, v_ref, o_ref, g_ref, dya_ref, tri_s_ref, tri_p_ref = refs[0:8]
        sc_ins = refs[8:8 + nsc]
        dq_ref, dk_ref, dv_ref, dg_ref = refs[8 + nsc:12 + nsc]
        sc_outs = refs[12 + nsc:12 + 2 * nsc]
        do_scr, l_scr, g_scr, s_scr, w_scr = refs[12 + 2 * nsc:17 + 2 * nsc]
        sc_sems = refs[17 + 2 * nsc:]
        i = pl.program_id(1)
        base = i * n_sub

        if nsc:
            @pl.when(jnp.logical_and(pl.program_id(0) == 0, i == 0))
            def _():
                _scatter_start(sc_ins, sc_outs, sc_sems)

        @pl.when(i == 0)
        def _():
            dk_ref[...] = jnp.zeros_like(dk_ref)
            dv_ref[...] = jnp.zeros_like(dv_ref)

        g = g_ref[...]
        sg = _sigmoid(g)
        dya = dya_ref[...]
        do_scr[...] = (dya * (g * sg)).astype(BF16)
        dg_ref[...] = dya * o_ref[...] * (sg * (1.0 + g * (1.0 - sg)))
        dq_ref[...] = jnp.zeros_like(dq_ref)
        g_scr[...] = jnp.zeros_like(g_scr)
        masks = _head_masks()

        def keep(t, a, h, c, r0, z, w):
            s_scr[c, t] = _sigmoid(z).astype(BF16)
            w_scr[c, t] = w.astype(BF16)

        steps = _chain_sweep(base, n_sub, q_ref, k_ref, tri_s_ref, l_scr, keep)
        row = lax.broadcasted_iota(jnp.int32, (tb, tb), 0)
        col = lax.broadcasted_iota(jnp.int32, (tb, tb), 1)

        def back(it, carry):
            t = steps - 1 - it
            r0s = [pl.multiple_of(jnp.maximum(base + a - t, 0) * tb, tb) for a in range(n_sub)]
            qhs, dohs, khs, gws = [], [], [], []
            for a in range(n_sub):
                kb = k_ref[pl.ds(r0s[a], tb), :]
                vb = v_ref[pl.ds(r0s[a], tb), :]
                qa = q_ref[a * tb:(a + 1) * tb, :]
                doa = do_scr[a * tb:(a + 1) * tb, :]
                for h, mask in enumerate(masks):
                    qhs.append(jnp.where(mask, qa, jnp.zeros_like(qa)))
                    khs.append(jnp.where(mask, kb, jnp.zeros_like(kb)))
                    dohs.append(jnp.where(mask, doa, jnp.zeros_like(doa)))
                    gws.append(w_scr[2 * a + h, t].astype(F32) * _dot_nt(dohs[-1], vb))
            parts = [_split_hilo(gw) for gw in gws]
            tri = tri_p_ref[...]
            sums = [_dot(hi, tri) + _dot(lo, tri) for hi, lo in parts]
            dzs = []
            for c, (gw, sm) in enumerate(zip(gws, sums)):
                gsum = g_scr[c]
                dz = gw - (gw + sm[:, 0:tb] + gsum) * s_scr[c, t].astype(F32)
                dz = jnp.where(col < row + t * tb, dz, 0.0)
                g_scr[c] = gsum + sm[:, tb:2 * tb]
                dzs.append(dz.astype(BF16))
            for c, dzb in enumerate(dzs):
                a = c // 2
                dk_ref[pl.ds(r0s[a], tb), :] += _dot_tn(dzb, qhs[c])
                dv_ref[pl.ds(r0s[a], tb), :] += _dot_tn(w_scr[c, t], dohs[c])
                dq_ref[a * tb:(a + 1) * tb, :] += _dot(dzb, khs[c])
            return carry

        lax.fori_loop(0, steps, back, 0)

        if nsc:
            @pl.when(jnp.logical_and(pl.program_id(0) == n_hp - 1, i == s // tq - 1))
            def _():
                _scatter_finish(sc_ins, sc_outs, sc_sems)

    n_hp = ATTN_WIDTH // (2 * HEAD_DIM)
    hp_blk = lambda off: pl.BlockSpec((tq, 2 * HEAD_DIM), lambda hp, i: (i, off + hp))
    res = lambda off: pl.BlockSpec((s, 2 * HEAD_DIM), lambda hp, i: (0, off + hp))
    tri = pl.BlockSpec((tb, 2 * tb), lambda hp, i: (0, 0))
    outs = pl.pallas_call(
        body, name="attn_bwd_scatter" if nsc else "attn_bwd", grid=(n_hp, s // tq),
        in_specs=[hp_blk(0), res(4), res(8), hp_blk(0), hp_blk(20), hp_blk(4), tri, tri] + [ANY] * nsc,
        out_specs=[hp_blk(0), res(0), res(0), hp_blk(0)] + [ANY] * nsc,
        out_shape=[jax.ShapeDtypeStruct((s, ATTN_WIDTH), F32)] * 4 + [jax.ShapeDtypeStruct(a.shape, a.dtype) for a in scatter],
        scratch_shapes=[pltpu.VMEM((tq, 2 * HEAD_DIM), BF16), pltpu.VMEM((n_chain, tb, tb), F32),
                        pltpu.VMEM((n_chain, tb, tb), F32), pltpu.VMEM((n_chain, nq, tb, tb), BF16),
                        pltpu.VMEM((n_chain, nq, tb, tb), BF16)] + (_scatter_sems(nsc) if nsc else []),
        compiler_params=_cparams(2, vmem_limit=V7X_VMEM_LIMIT_ATTN_BWD),
    )(qkv, qkv, qkv, o, proj, dmix, _tri("suffix_incl"), _tri("prefix_strict"), *scatter)
    return outs[0], outs[1], outs[2], outs[3], outs[4:]


def _ssm_glu_bwd(dmix, y, proj, d, w_glu_l, b_glu):
    s = y.shape[0]
    tm = _row_tile(s, 1024)

    def body(dys_ref, y_ref, u_ref, gs_ref, d_ref, wg_ref, bg_ref,
             dyf_ref, du_ref, dgs_ref, z_ref, dzz_ref, dd_ref, db_ref):
        @pl.when(pl.program_id(0) == 0)
        def _():
            dd_ref[...] = jnp.zeros_like(dd_ref)
            db_ref[...] = jnp.zeros_like(db_ref)

        u = u_ref[...]
        dv = d_ref[...]
        yf, z, val, gate = _glu_forward(y_ref[...], u, dv, wg_ref, bg_ref[...])
        gs = gs_ref[...]
        sgs = _sigmoid(gs)
        sgate = _sigmoid(gate)
        dys = dys_ref[...]
        dgv = dys * (gs * sgs)
        dgs_ref[...] = dys * (val * sgate) * (sgs * (1.0 + gs * (1.0 - sgs)))
        dzz = jnp.concatenate([dgv * sgate, dgv * val * sgate * (1.0 - sgate)], axis=-1)
        dzzb = dzz.astype(BF16)
        dz = _dot_nt(dzzb[:, 0:ROW_SHARD], wg_ref[0])
        for sh in range(1, N_CHIPS):
            dz = dz + _dot_nt(dzzb[:, ROW_SHARD * sh:ROW_SHARD * (sh + 1)], wg_ref[sh])
        dyf = dz * _gelu_grad(yf)
        dyf_ref[...] = dyf
        du_ref[...] = dyf * dv
        z_ref[...] = z.astype(BF16)
        dzz_ref[...] = dzzb
        dd_ref[...] += _colsum8(dyf * u)
        db_ref[...] += _colsum8(dzz)

    row = lambda i: (i, 0)
    half = pl.BlockSpec((tm, SSM_WIDTH), row)
    return pl.pallas_call(
        body, name="ssm_glu_bwd", grid=(s // tm,),
        in_specs=[half, half, half, pl.BlockSpec((tm, SSM_WIDTH), lambda i: (i, 1)),
                  pl.BlockSpec((1, SSM_WIDTH), lambda i: (0, 0)),
                  pl.BlockSpec((N_CHIPS, SSM_WIDTH, ROW_SHARD), lambda i: (0, 0, 0)),
                  pl.BlockSpec((1, 2 * SSM_WIDTH), lambda i: (0, 0))],
        out_specs=[half, half, half, half, pl.BlockSpec((tm, 2 * SSM_WIDTH), row),
                   pl.BlockSpec((SUBLANES, SSM_WIDTH), lambda i: (0, 0)),
                   pl.BlockSpec((SUBLANES, 2 * SSM_WIDTH), lambda i: (0, 0))],
        out_shape=[jax.ShapeDtypeStruct((s, SSM_WIDTH), F32)] * 3
        + [jax.ShapeDtypeStruct((s, SSM_WIDTH), BF16), jax.ShapeDtypeStruct((s, 2 * SSM_WIDTH), BF16),
           jax.ShapeDtypeStruct((SUBLANES, SSM_WIDTH), F32), jax.ShapeDtypeStruct((SUBLANES, 2 * SSM_WIDTH), F32)],
        compiler_params=_cparams(1, vmem_limit=V7X_VMEM_LIMIT_GLU_BWD),
    )(dmix, y, proj, proj, d, w_glu_l, b_glu)


def _ssm_scan_bwd(dyf, xs, proj, wct, tab_rev, wbt):
    s = dyf.shape[0]
    tm = _row_tile(s, SCAN_TILE)
    nt = s // tm
    length = tm // SUBLANES

    def body(dy_ref, xs_ref, u_ref, wct_ref, tab_ref, wbt_ref, du_ref, dwc_ref, dwb_ref, da_ref, lam_ref, carry_ref):
        @pl.when(pl.program_id(1) == 0)
        def _():
            carry_ref[...] = jnp.zeros_like(carry_ref)
            dwc_ref[...] = jnp.zeros_like(dwc_ref)
            dwb_ref[...] = jnp.zeros_like(dwb_ref)
            da_ref[...] = jnp.zeros_like(da_ref)

        dyp = _interleave_chunks(dy_ref[...]).astype(BF16)
        up = _interleave_chunks(u_ref[...]).astype(BF16)
        lam_ref[...] = _dot(dyp, wct_ref[...])

        def tail(r0, lr, li, carry):
            er, ei, dar, dai = carry
            xr = xs_ref[pl.ds(r0, SUBLANES), 0:CH_S]
            xi = xs_ref[pl.ds(r0, SUBLANES), CH_S:2 * CH_S]
            return lr, li, dar + (xr * er + xi * ei), dai + (xr * ei - xi * er)

        fix, (gr, gi) = _chunk_scan(lam_ref, tab_ref, carry_ref, length, reverse=True, tail=tail)
        zero = jnp.zeros((SUBLANES, CH_S), F32)
        _, _, dar, dai = lax.fori_loop(0, length, fix, (gr, gi, zero, zero), unroll=2)
        da_ref[:, 0:CH_S] += dar
        da_ref[:, CH_S:2 * CH_S] += dai
        lamb = lam_ref[...].astype(BF16)
        du_ref[...] = _time_order(_dot(lamb, wbt_ref[...]))
        dwc_ref[...] += _dot_tn(xs_ref[...].astype(BF16), dyp)
        dwb_ref[...] += _dot_tn(up, lamb)

    rev = lambda j, i: (nt - 1 - i, j)
    return pl.pallas_call(
        body, name="ssm_scan_bwd", grid=(SSM_CHUNKS, nt),
        in_specs=[pl.BlockSpec((tm, CH_W), rev),
                  pl.BlockSpec((None, tm, 2 * CH_S), lambda j, i: (j, nt - 1 - i, 0)),
                  pl.BlockSpec((tm, CH_W), rev),
                  pl.BlockSpec((None, CH_W, 2 * CH_S), lambda j, i: (j, 0, 0)),
                  pl.BlockSpec((None, length, 2 * CH_S), lambda j, i: (j, 0, 0)),
                  pl.BlockSpec((None, 2 * CH_S, CH_W), lambda j, i: (j, 0, 0))],
        out_specs=[pl.BlockSpec((tm, CH_W), rev),
                   pl.BlockSpec((None, 2 * CH_S, CH_W), lambda j, i: (j, 0, 0)),
                   pl.BlockSpec((None, CH_W, 2 * CH_S), lambda j, i: (j, 0, 0)),
                   pl.BlockSpec((None, SUBLANES, 2 * CH_S), lambda j, i: (j, 0, 0))],
        out_shape=[jax.ShapeDtypeStruct((s, SSM_WIDTH), F32),
                   jax.ShapeDtypeStruct((SSM_CHUNKS, 2 * CH_S, CH_W), F32),
                   jax.ShapeDtypeStruct((SSM_CHUNKS, CH_W, 2 * CH_S), F32),
                   jax.ShapeDtypeStruct((SSM_CHUNKS, SUBLANES, 2 * CH_S), F32)],
        scratch_shapes=[pltpu.VMEM((tm, 2 * CH_S), F32), pltpu.VMEM((SUBLANES, 2 * CH_S), F32)],
        compiler_params=_cparams(2),
    )(dyf, xs, proj, wct, tab_rev, wbt)


def _in_proj_bwd(h, g1, w_in_l, qg, kg, proj, du_a, du_b, dgs, dq, dk, dv, dga, dh1):
    s = h.shape[0]
    tm = _row_tile(s, 256)

    def body(h_ref, g_ref, w_ref, qg_ref, kg_ref, ones_ref, q_ref, k_ref, dua_ref, dub_ref, dgs_ref, dq_ref, dk_ref,
             dv_ref, dga_ref, dh1_ref, dh_ref, hn_ref, dp_ref, dg1_ref, dqg_ref, dkg_ref):
        @pl.when(pl.program_id(0) == 0)
        def _():
            dg1_ref[...] = jnp.zeros_like(dg1_ref)
            dqg_ref[...] = jnp.zeros_like(dqg_ref)
            dkg_ref[...] = jnp.zeros_like(dkg_ref)

        ones = ones_ref[...]

        def head_norm_bwd(x, gain, dy):
            r = lax.rsqrt(_dot_hilo(x * x, ones) + RMS_EPS)
            gdy = gain * dy
            dx = r * gdy - x * (r * r * r) * _dot_hilo(x * gdy, ones)
            return dx, x * r * dy

        dqr, dqg_rows = head_norm_bwd(q_ref[...], qg_ref[...], dq_ref[...] * ATTN_SCALE)
        dkr, dkg_rows = head_norm_bwd(k_ref[...], kg_ref[...], dk_ref[...])
        dqg_ref[...] += _colsum8(dqg_rows)
        dkg_ref[...] += _colsum8(dkg_rows)
        dp_ref[:, 0:512] = (dua_ref[...] + dub_ref[...]).astype(BF16)
        dp_ref[:, 512:1024] = dgs_ref[...].astype(BF16)
        dp_ref[:, 1024:1536] = dqr.astype(BF16)
        dp_ref[:, 1536:2048] = dkr.astype(BF16)
        dp_ref[:, 2048:2560] = dv_ref[...].astype(BF16)
        dp_ref[:, 2560:3072] = dga_ref[...].astype(BF16)
        dhn = _dot_nt(dp_ref[:, 0:IN_SHARD], w_ref[0])
        for sh in range(1, N_CHIPS):
            dhn = dhn + _dot_nt(dp_ref[:, IN_SHARD * sh:IN_SHARD * (sh + 1)], w_ref[sh])
        x = h_ref[...]
        gv = g_ref[...]
        r, hn = _rms_rows(x, gv)
        dx, dg_rows = _rms_bwd(x, r, gv, dhn)
        dh_ref[...] = dh1_ref[...] + dx
        hn_ref[...] = hn.astype(BF16)
        dg1_ref[...] += _colsum8(dg_rows)

    row = lambda i: (i, 0)
    full = lambda shape: pl.BlockSpec(shape, lambda i: (0,) * len(shape))
    big = pl.BlockSpec((tm, D_MODEL), row)
    half = pl.BlockSpec((tm, 512), row)
    return pl.pallas_call(
        body, name="in_proj_bwd", grid=(s // tm,),
        in_specs=[big, full((1, D_MODEL)), full((N_CHIPS, D_MODEL, IN_SHARD)),
                  full((1, ATTN_WIDTH)), full((1, ATTN_WIDTH)), full((ATTN_WIDTH, ATTN_WIDTH)),
                  pl.BlockSpec((tm, 512), lambda i: (i, 2)), pl.BlockSpec((tm, 512), lambda i: (i, 3)),
                  half, half, half, half, half, half, half, big],
        out_specs=[big, big, pl.BlockSpec((tm, IN_COLS), row), pl.BlockSpec((SUBLANES, D_MODEL), lambda i: (0, 0)),
                   pl.BlockSpec((SUBLANES, ATTN_WIDTH), lambda i: (0, 0)), pl.BlockSpec((SUBLANES, ATTN_WIDTH), lambda i: (0, 0))],
        out_shape=[jax.ShapeDtypeStruct((s, D_MODEL), F32), jax.ShapeDtypeStruct((s, D_MODEL), BF16),
                   jax.ShapeDtypeStruct((s, IN_COLS), BF16), jax.ShapeDtypeStruct((SUBLANES, D_MODEL), F32),
                   jax.ShapeDtypeStruct((SUBLANES, ATTN_WIDTH), F32), jax.ShapeDtypeStruct((SUBLANES, ATTN_WIDTH), F32)],
        compiler_params=_cparams(1),
    )(h, g1, w_in_l, qg, kg, _head_ones(), proj, proj, du_a, du_b, dgs, dq, dk, dv, dga, dh1)


SMALL_NAMES = ("mix_norm_g", "ssm_a_re", "ssm_a_im", "ssm_log_dt", "ssm_b_re", "ssm_b_im", "ssm_c_re", "ssm_c_im",
               "ssm_d", "ssm_b_glu", "q_norm_g", "k_norm_g", "ple_norm_g")
SMALL_4D = ("ssm_b_re", "ssm_b_im", "ssm_c_re", "ssm_c_im")
BIG_NAMES = ("w_in", "ssm_w_glu", "w_out", "w_ple_gate", "w_ple_proj")


def _ssm_setup(sm, layer, length):
    col = lambda a: a[layer].reshape(1, N_STATES)
    a_re, a_im = col(sm["ssm_a_re"]), col(sm["ssm_a_im"])
    log_dt = jnp.repeat(sm["ssm_log_dt"][layer], SSM_STATE).reshape(1, N_STATES)
    b_re = sm["ssm_b_re"][layer].reshape(N_STATES, SSM_GROUP).T
    b_im = sm["ssm_b_im"][layer].reshape(N_STATES, SSM_GROUP).T
    by_channel = lambda c: c[layer].transpose(1, 0, 2).reshape(SSM_GROUP, N_STATES)
    disc_in = (a_re, a_im, log_dt, b_re, b_im)
    wb, wbt, wct, wc, tab, tab_rev = _disc_fwd(*disc_in, by_channel(sm["ssm_c_re"]), by_channel(sm["ssm_c_im"]), length)
    return dict(disc_in=disc_in, wb=wb, wbt=wbt, wc=wc, wct=wct, tab=tab, tab_rev=tab_rev)


def _whole_blocks(names, gathered):
    return {n: g.reshape(N_CHIPS, 2 * g.shape[2], g.shape[3]) for n, g in zip(names, gathered)}


def _local_step(x, p, target, sm, w_in0, local=None, gathered=None, layer1_hook=None):
    wg = [dict(w_in=w_in0), {}] if gathered is None else gathered
    tile8 = lambda a: jnp.tile(a, ATTN_WIDTH // HEAD_DIM).reshape(1, ATTN_WIDTH)
    saved = []
    h = x
    for l in range(N_LAYERS):
        ssm = _ssm_setup(sm, l, _row_tile(x.shape[0], SCAN_TILE) // SUBLANES)
        g1 = sm["mix_norm_g"][l].reshape(1, D_MODEL)
        g2 = sm["ple_norm_g"][l].reshape(1, D_MODEL)
        qg, kg = tile8(sm["q_norm_g"][l]), tile8(sm["k_norm_g"][l])
        dsk = sm["ssm_d"][l].reshape(1, SSM_WIDTH)
        bgl = sm["ssm_b_glu"][l].reshape(1, 2 * SSM_WIDTH)
        proj, qkv = _in_proj(h, g1, wg[l]["w_in"], qg, kg)
        if l == 0 and local is not None:
            rest = BIG_NAMES[1:]
            xs, y, got = _ssm_scan_fwd(proj, ssm["wb"], ssm["tab"], ssm["wc"], [local[n] for n in rest], [0] * len(rest))
            wg[0].update(_whole_blocks(rest, got))
            ys = _ssm_glu_fwd(y, proj, dsk, wg[0]["ssm_w_glu"], bgl)
            o, ya, got = _attn_fwd(qkv, proj, [local[n] for n in BIG_NAMES], [2] * len(BIG_NAMES))
            wg[1].update(_whole_blocks(BIG_NAMES, got))
        else:
            xs, y, _ = _ssm_scan_fwd(proj, ssm["wb"], ssm["tab"], ssm["wc"])
            ys = _ssm_glu_fwd(y, proj, dsk, wg[l]["ssm_w_glu"], bgl)
            o, ya, _ = _attn_fwd(qkv, proj)
        tail = (target,) if l == N_LAYERS - 1 else ()
        h1, h2, *sq = _out_ple(h, ys, ya, p[l], g2, wg[l]["w_out"], wg[l]["w_ple_gate"], wg[l]["w_ple_proj"], *tail)
        saved.append(dict(ssm=ssm, g1=g1, g2=g2, qg=qg, kg=kg, dsk=dsk, bgl=bgl, h=h, proj=proj, qkv=qkv, xs=xs, y=y,
                          ys=ys, o=o, ya=ya, h1=h1))
        h = h2
    dh = h
    loss = 0.5 * jnp.sum(sq[0]) / D_MODEL

    gbig = [{} for _ in range(N_LAYERS)]
    scattered = ([], [])
    gsm = {n: [None] * N_LAYERS for n in SMALL_NAMES}
    for l in reversed(range(N_LAYERS)):
        sv = saved[l]
        ssm = sv["ssm"]
        dh1, dmix, hn2b, dgpb, dppb, dh1b, dg2 = _out_ple_bwd(dh, sv["h1"], p[l], sv["g2"], wg[l]["w_out"],
                                                              wg[l]["w_ple_gate"], wg[l]["w_ple_proj"])
        gsm["ple_norm_g"][l] = dg2.sum(0)
        gbig[l]["w_ple_proj"] = _tn_matmul(p[l], dppb, N_CHIPS, False, "dw_ple_proj")
        gbig[l]["w_ple_gate"] = _tn_matmul(hn2b, dgpb, N_CHIPS, True, "dw_ple_gate")
        dwo = _tn_matmul(sv["ys"], dh1b, 2, True, "dw_out_ssm", None, 0, N_CHIPS)
        gbig[l]["w_out"] = _tn_matmul(sv["ya"], dh1b, 2, True, "dw_out_attn", dwo, 2, N_CHIPS)
        if l == 0 and layer1_hook is not None:
            chip1 = layer1_hook(gbig[1])
            dqs, dkn, dv, dga, got = _attn_bwd(sv["qkv"], sv["o"], sv["proj"], dmix, chip1)
            scattered = (chip1, got)
        else:
            dqs, dkn, dv, dga, _ = _attn_bwd(sv["qkv"], sv["o"], sv["proj"], dmix)
        dyf, du_a, dgs, zb, dzzb, dd, dbg = _ssm_glu_bwd(dmix, sv["y"], sv["proj"], sv["dsk"], wg[l]["ssm_w_glu"], sv["bgl"])
        gsm["ssm_d"][l] = dd.sum(0).reshape(SSM_GROUPS, SSM_GROUP)
        gsm["ssm_b_glu"][l] = dbg.sum(0)
        gbig[l]["ssm_w_glu"] = _tn_matmul(zb, dzzb, N_CHIPS, False, "dw_glu")
        du_b, dwc, dwb, da = _ssm_scan_bwd(dyf, sv["xs"], sv["proj"], ssm["wct"], ssm["tab_rev"], ssm["wbt"])
        d_are, d_aim, d_ldt, d_bre, d_bim, d_cre, d_cim = _disc_bwd(*ssm["disc_in"], da, dwb, dwc)
        by_group = lambda t: t.reshape(SSM_GROUP, SSM_GROUPS, SSM_STATE).transpose(1, 0, 2)
        gsm["ssm_c_re"][l] = by_group(d_cre)
        gsm["ssm_c_im"][l] = by_group(d_cim)
        gsm["ssm_a_re"][l] = d_are.reshape(SSM_GROUPS, SSM_STATE)
        gsm["ssm_a_im"][l] = d_aim.reshape(SSM_GROUPS, SSM_STATE)
        gsm["ssm_log_dt"][l] = d_ldt.reshape(SSM_GROUPS, SSM_STATE).sum(1)
        gsm["ssm_b_re"][l] = d_bre.T.reshape(SSM_GROUPS, SSM_STATE, SSM_GROUP)
        gsm["ssm_b_im"][l] = d_bim.T.reshape(SSM_GROUPS, SSM_STATE, SSM_GROUP)
        dh, hnb, dprojb, dg1, dqg, dkg = _in_proj_bwd(sv["h"], sv["g1"], wg[l]["w_in"], sv["qg"], sv["kg"], sv["proj"],
                                                      du_a, du_b, dgs, dqs, dkn, dv, dga, dh1)
        gsm["mix_norm_g"][l] = dg1.sum(0)
        gsm["q_norm_g"][l] = dqg.sum(0).reshape(-1, HEAD_DIM).sum(0)
        gsm["k_norm_g"][l] = dkg.sum(0).reshape(-1, HEAD_DIM).sum(0)
        gbig[l]["w_in"] = _tn_matmul(hnb, dprojb, N_CHIPS, False, "dw_in")
    gsm = {n: jnp.stack(v, 0) for n, v in gsm.items()}
    return loss, dh, gbig, gsm, scattered


_SMALL_PAD = 8 * 8 * 128


def _pack_small(d, extra):
    flat = jnp.concatenate([d[n].reshape(-1) for n in SMALL_NAMES] + [jnp.stack(extra)])
    n = flat.shape[0]
    padded = -(-n // _SMALL_PAD) * _SMALL_PAD
    return jnp.pad(flat, (0, padded - n))


def _unpack_small(flat, like):
    out, off = {}, 0
    for n in SMALL_NAMES:
        size = like[n].size
        out[n] = flat[off:off + size].reshape(like[n].shape)
        off += size
    return out, flat[off:]


def _half_views(arrs):
    return [a.reshape(a.shape[0], 2, a.shape[1] // 2, a.shape[2]) for a in arrs]


def _chip_sums(views, out_dtypes, tag):
    recv = _sibling_push(views, "grad_push_" + tag)
    return [_add_my_half(v, r, dt, "grad_half_add") for v, r, dt in zip(views, recv, out_dtypes)]


def kernel(x, p, mix_norm_g, w_in, ssm_a_re, ssm_a_im, ssm_log_dt, ssm_b_re, ssm_b_im, ssm_c_re, ssm_c_im, ssm_d, ssm_w_glu, ssm_b_glu, q_norm_g, k_norm_g, w_out, ple_norm_g, w_ple_gate, w_ple_proj, loss_target, m_mix_norm_g, m_w_in, m_ssm_a_re, m_ssm_a_im, m_ssm_log_dt, m_ssm_b_re, m_ssm_b_im, m_ssm_c_re, m_ssm_c_im, m_ssm_d, m_ssm_w_glu, m_ssm_b_glu, m_q_norm_g, m_k_norm_g, m_w_out, m_ple_norm_g, m_w_ple_gate, m_w_ple_proj, v_mix_norm_g, v_w_in, v_ssm_a_re, v_ssm_a_im, v_ssm_log_dt, v_ssm_b_re, v_ssm_b_im, v_ssm_c_re, v_ssm_c_im, v_ssm_d, v_ssm_w_glu, v_ssm_b_glu, v_q_norm_g, v_k_norm_g, v_w_out, v_ple_norm_g, v_w_ple_gate, v_w_ple_proj):
    args = dict(locals())
    names = ("mix_norm_g", "w_in", "ssm_a_re", "ssm_a_im", "ssm_log_dt", "ssm_b_re", "ssm_b_im", "ssm_c_re", "ssm_c_im",
             "ssm_d", "ssm_w_glu", "ssm_b_glu", "q_norm_g", "k_norm_g", "w_out", "ple_norm_g", "w_ple_gate", "w_ple_proj")
    w = {n: args[n] for n in names}
    m = {n: args["m_" + n] for n in names}
    v = {n: args["v_" + n] for n in names}

    local = {n: w[n].astype(BF16).reshape(2 * N_LAYERS, w[n].shape[1] // 2, w[n].shape[2]) for n in BIG_NAMES}
    w_in0 = _chip_gather([local["w_in"]], "w_in_gather")[0].reshape(N_CHIPS, D_MODEL, IN_SHARD)
    sm = {n: w[n] for n in SMALL_NAMES}
    nb = len(BIG_NAMES)
    loss, dx, gbig, gsm, (chip1, got1) = _local_step(
        x[0], p[:, 0], loss_target[0], sm, w_in0, local,
        layer1_hook=lambda g1: _chip_sums(_half_views([g1[n] for n in BIG_NAMES]), [BF16] * nb, "layer1"))

    small = _pack_small(gsm, [loss]).reshape(N_CHIPS, 2, SUBLANES, -1)
    chip0 = _chip_sums(_half_views([gbig[0][n] for n in BIG_NAMES]) + [small], [BF16] * nb + [F32], "layer0")
    got0 = _chip_scatter(chip0, "grad_chip_scatter")
    tot1 = [_sum4(a, own, "grad_chip_sum") for a, own in zip(got1, chip1)]
    tot0 = [_sum4(a, own, "grad_chip_sum") for a, own in zip(got0, chip0)]
    pieces = [(t, k, (l,)) for l, tots in enumerate((tot0[:nb], tot1)) for k, t in enumerate(tots)] + [(tot0[nb], nb, ())]
    joined = _sibling_join(pieces, [(N_LAYERS, 2) + t.shape for t in tot1] + [(2,) + tot0[nb].shape], "grad_sibling_join")
    small_all = _chip_gather([joined[nb]], "small_grad_gather")[0]
    small_tot = small_all.reshape(-1)
    g = {n: j.reshape(w[n].shape) for n, j in zip(BIG_NAMES, joined)}
    g_small, rest = _unpack_small(small_tot, sm)
    g.update(g_small)
    loss = rest[0]

    delta, new_m, new_v = {}, {}, {}
    for n in BIG_NAMES:
        lanes = w[n].shape[-1]
        outs = _adamw(_as_rows(w[n], lanes), _as_rows(g[n], lanes), _as_rows(m[n], lanes), _as_rows(v[n], lanes), "adamw_" + n)
        delta[n], new_m[n], new_v[n] = [o.reshape(w[n].shape) for o in outs]
    swap = lambda n, a: jnp.swapaxes(a, -1, -2) if n in ("ssm_b_re", "ssm_b_im") else a
    for group, per_layer in ((SMALL_4D, True), (tuple(n for n in SMALL_NAMES if n not in SMALL_4D), False)):
        outs = _adamw_many(*[[swap(n, d[n]) for n in group] for d in (w, g, m, v)],
                           "adamw_small_4d" if per_layer else "adamw_small", per_layer)
        for d, o in zip((delta, new_m, new_v), outs):
            d.update({n: swap(n, a) for n, a in zip(group, o)})

    return (loss, dx[None], *[g[n] for n in names], *[delta[n] for n in names],
            *[new_m[n] for n in names], *[new_v[n] for n in names])
```

```python
import functools
import math

import jax
import jax.numpy as jnp
from jax import lax
from jax.experimental import pallas as pl
from jax.experimental.pallas import tpu as pltpu

F32 = jnp.float32
BF16 = jnp.bfloat16

D_MODEL = 1024
N_LAYERS = 2
N_CHIPS = 4
IN_COLS = 3072
IN_SHARD = IN_COLS // N_CHIPS
SSM_WIDTH = 512
SSM_GROUP = 16
SSM_GROUPS = 32
SSM_STATE = 64
N_STATES = SSM_GROUPS * SSM_STATE
SSM_CHUNKS = 4
CH_W = SSM_WIDTH // SSM_CHUNKS
CH_S = N_STATES // SSM_CHUNKS
ATTN_WIDTH = 512
HEAD_DIM = 64
PLE_DIM = 256
ROW_SHARD = 256
RMS_EPS = 1e-6
ATTN_SCALE = HEAD_DIM ** -0.5
ATTN_BLOCK = 128
EXP_ZERO = -87.5
SUBLANES = 8
SCAN_TILE = 1024
V7X_VMEM_LIMIT = 56 * 1024 * 1024
V7X_VMEM_LIMIT_ATTN_BWD = 60 * 1024 * 1024

ADAM_LR = 0.001
ADAM_B1 = 0.9
ADAM_B2 = 0.999
ADAM_EPS = 1e-08
ADAM_WD = 0.01
ADAM_STEP = 10

MESH = pl.DeviceIdType.MESH
ANY = pl.BlockSpec(memory_space=pl.ANY)


def _cparams(n_grid=0, parallel=0, vmem_limit=V7X_VMEM_LIMIT):
    sem = tuple(["parallel"] * parallel + ["arbitrary"] * (n_grid - parallel))
    return pltpu.CompilerParams(dimension_semantics=sem, vmem_limit_bytes=vmem_limit)


def _dot(a, b):
    return jnp.dot(a, b, preferred_element_type=F32)


def _dot_nt(a, b):
    return lax.dot_general(a, b, (((1,), (1,)), ((), ())), preferred_element_type=F32)


def _dot_tn(a, b):
    return lax.dot_general(a, b, (((0,), (0,)), ((), ())), preferred_element_type=F32)


def _split_hilo(a):
    hi = a.astype(BF16)
    lo = (a - hi.astype(F32)).astype(BF16)
    return hi, lo


def _dot_hilo(a, b):
    hi, lo = _split_hilo(a)
    return _dot(hi, b) + _dot(lo, b)


def _sigmoid(x):
    return 0.5 * (jnp.tanh(0.5 * x) + 1.0)


_GELU_C = math.sqrt(2.0 / math.pi)


def _gelu(x):
    return 0.5 * x * (1.0 + jnp.tanh(_GELU_C * (x + 0.044715 * (x * x * x))))


def _gelu_grad(x):
    t = jnp.tanh(_GELU_C * (x + 0.044715 * (x * x * x)))
    return 0.5 * (1.0 + t) + 0.5 * x * (1.0 - t * t) * (_GELU_C * (1.0 + 3.0 * 0.044715 * (x * x)))


def _row_tile(s, want):
    for t in range(min(s, want), 7, -1):
        if s % t == 0 and t % SUBLANES == 0:
            return t
    return s


def _coords():
    return lax.axis_index("x"), lax.axis_index("y"), lax.axis_index("c")


def _other_chips(x, y):
    return [(1 - x, y), (x, 1 - y), (1 - x, 1 - y)]


def _remote(src, dst, send_sem, recv_sem, dev):
    return pltpu.make_async_remote_copy(src_ref=src, dst_ref=dst, send_sem=send_sem, recv_sem=recv_sem,
                                        device_id=dev, device_id_type=MESH)


def _set_block(buf, block, index):
    return lax.dynamic_update_index_in_dim(buf, block, index, 0)


def _gather_sems(n):
    return [pltpu.SemaphoreType.DMA((3 * n,)) for _ in range(4)]


def _gather_copies(ins, bases, outs, sems):
    send_sems, recv_sems, fwd_send, fwd_recv = sems
    x, y, c = _coords()
    me_chip = 2 * x + y
    sibling = (x, y, 1 - c)
    first, landed, passed, from_sibling = [], [], [], []
    for k in range(len(ins)):
        for j, (cx, cy) in enumerate(_other_chips(x, y)):
            i = 3 * k + j
            first.append(_remote(ins[k].at[bases[k] + c], outs[k].at[me_chip, c], send_sems.at[i], recv_sems.at[i], (cx, cy, c)))
            blk = outs[k].at[2 * cx + cy, c]
            landed.append(_remote(blk, blk, send_sems.at[i], recv_sems.at[i], (cx, cy, c)))
            passed.append(_remote(blk, blk, fwd_send.at[i], fwd_recv.at[i], sibling))
            blk = outs[k].at[2 * cx + cy, 1 - c]
            from_sibling.append(_remote(blk, blk, fwd_send.at[i], fwd_recv.at[i], sibling))
    return first, landed, passed, from_sibling


def _gather_start(ins, bases, outs, sems):
    for cp in _gather_copies(ins, bases, outs, sems)[0]:
        cp.start()


def _gather_finish(ins, bases, outs, sems):
    first, landed, passed, from_sibling = _gather_copies(ins, bases, outs, sems)
    for arrived, forward in zip(landed, passed):
        arrived.wait_recv()
        forward.start()
    for cp in from_sibling:
        cp.wait_recv()
    for cp in first + passed:
        cp.wait_send()


def _gather_outputs(arrs):
    return [jax.ShapeDtypeStruct((N_CHIPS, 2) + a.shape[1:], a.dtype) for a in arrs]


def _gather_own(outs, arrs, bases):
    me_chip = 2 * lax.axis_index("x") + lax.axis_index("y")
    return [_set_block(o, lax.slice_in_dim(a, b, b + 2, axis=0), me_chip) for o, a, b in zip(outs, arrs, bases)]


def _chip_gather(arrs, name, bases=None):
    n = len(arrs)
    bases = [0] * n if bases is None else bases

    def body(*refs):
        ins, outs, sems = refs[:n], refs[n:2 * n], refs[2 * n:]
        _gather_start(ins, bases, outs, sems)
        _gather_finish(ins, bases, outs, sems)

    outs = pl.pallas_call(
        body, name=name, out_shape=_gather_outputs(arrs),
        in_specs=[ANY] * n, out_specs=[ANY] * n, scratch_shapes=_gather_sems(n),
    )(*arrs)
    return _gather_own(outs, arrs, bases)


def _sibling_push(arrs, name):
    n = len(arrs)

    def body(*refs):
        ins, outs = refs[:n], refs[n:2 * n]
        send_sems, recv_sems = refs[2 * n:]
        x, y, c = _coords()
        cps = [_remote(ins[k].at[pl.ds(0, N_CHIPS), 1 - c], outs[k], send_sems.at[k], recv_sems.at[k], (x, y, 1 - c))
               for k in range(n)]
        for cp in cps:
            cp.start()
        for cp in cps:
            cp.wait_recv()
        for cp in cps:
            cp.wait_send()

    return pl.pallas_call(
        body, name=name,
        out_shape=[jax.ShapeDtypeStruct((a.shape[0],) + a.shape[2:], a.dtype) for a in arrs],
        in_specs=[ANY] * n, out_specs=[ANY] * n,
        scratch_shapes=[pltpu.SemaphoreType.DMA((n,)), pltpu.SemaphoreType.DMA((n,))],
    )(*arrs)


def _sibling_join(pieces, out_shapes, name):
    n = len(pieces)
    no = len(out_shapes)

    def body(*refs):
        ins, outs = refs[:n], refs[n:n + no]
        send_sems, recv_sems = refs[n + no:]
        x, y, c = _coords()
        sibling = (x, y, 1 - c)
        cps = [_remote(ins[k], outs[o].at[lead + (c,)], send_sems.at[k], recv_sems.at[k], sibling)
               for k, (_, o, lead) in enumerate(pieces)]
        for cp in cps:
            cp.start()
        for k, (_, o, lead) in enumerate(pieces):
            blk = outs[o].at[lead + (1 - c,)]
            _remote(blk, blk, send_sems.at[k], recv_sems.at[k], sibling).wait_recv()
        for cp in cps:
            cp.wait_send()

    outs = pl.pallas_call(
        body, name=name,
        out_shape=[jax.ShapeDtypeStruct(sh, F32) for sh in out_shapes],
        in_specs=[ANY] * n, out_specs=[ANY] * no,
        scratch_shapes=[pltpu.SemaphoreType.DMA((n,)), pltpu.SemaphoreType.DMA((n,))],
    )(*[a for a, _, _ in pieces])
    outs = list(outs)
    c = lax.axis_index("c")
    for a, o, lead in pieces:
        block = a.reshape((1,) * (len(lead) + 1) + a.shape)
        outs[o] = lax.dynamic_update_slice(outs[o], block, lead + (c,) + (0,) * a.ndim)
    return outs


def _scatter_sems(n):
    return [pltpu.SemaphoreType.DMA((3 * n,)), pltpu.SemaphoreType.DMA((3 * n,))]


def _scatter_copies(ins, outs, sems):
    send_sems, recv_sems = sems
    x, y, c = _coords()
    me_chip = 2 * x + y
    sends, arrivals = [], []
    for k in range(len(ins)):
        for j, (cx, cy) in enumerate(_other_chips(x, y)):
            i = 3 * k + j
            sends.append(_remote(ins[k].at[2 * cx + cy], outs[k].at[me_chip], send_sems.at[i], recv_sems.at[i], (cx, cy, c)))
            blk = outs[k].at[2 * cx + cy]
            arrivals.append(_remote(blk, blk, send_sems.at[i], recv_sems.at[i], (cx, cy, c)))
    return sends, arrivals


def _scatter_start(ins, outs, sems):
    for cp in _scatter_copies(ins, outs, sems)[0]:
        cp.start()


def _scatter_finish(ins, outs, sems):
    sends, arrivals = _scatter_copies(ins, outs, sems)
    for cp in arrivals:
        cp.wait_recv()
    for cp in sends:
        cp.wait_send()


def _chip_scatter(arrs, name):
    n = len(arrs)

    def body(*refs):
        ins, outs, sems = refs[:n], refs[n:2 * n], refs[2 * n:]
        _scatter_start(ins, outs, sems)
        _scatter_finish(ins, outs, sems)

    outs = pl.pallas_call(
        body, name=name,
        out_shape=[jax.ShapeDtypeStruct(a.shape, a.dtype) for a in arrs],
        in_specs=[ANY] * n, out_specs=[ANY] * n, scratch_shapes=_scatter_sems(n),
    )(*arrs)
    return outs


def _as_rows(a, lanes):
    return a.reshape(-1, lanes)


def _add_my_half(v, recv, out_dtype, name):
    n_sh, _, h, cdim = v.shape
    tr = _row_tile(h, 512)

    def body(c_ref, a_ref, b_ref, o_ref):
        o_ref[...] = (a_ref[...].astype(F32) + b_ref[...].astype(F32)).astype(out_dtype)

    c = lax.axis_index("c").astype(jnp.int32).reshape(1)
    return pl.pallas_call(
        body, name=name,
        grid_spec=pltpu.PrefetchScalarGridSpec(
            num_scalar_prefetch=1, grid=(n_sh, h // tr),
            in_specs=[pl.BlockSpec((None, None, tr, cdim), lambda sh, i, c_ref: (sh, c_ref[0], i, 0)),
                      pl.BlockSpec((None, tr, cdim), lambda sh, i, c_ref: (sh, i, 0))],
            out_specs=pl.BlockSpec((None, tr, cdim), lambda sh, i, c_ref: (sh, i, 0))),
        out_shape=jax.ShapeDtypeStruct((n_sh, h, cdim), out_dtype),
        compiler_params=_cparams(2),
    )(c, v, recv)


def _sum4(got, own, name):
    _, r, cdim = got.shape
    tr = _row_tile(r, 512)

    def body(me_ref, p_ref, own_ref, o_ref):
        mine = own_ref[...].astype(F32)
        acc = None
        for j in range(N_CHIPS):
            term = jnp.where(me_ref[0] == j, mine, p_ref[j].astype(F32))
            acc = term if acc is None else acc + term
        o_ref[...] = acc

    me = (2 * lax.axis_index("x") + lax.axis_index("y")).astype(jnp.int32).reshape(1)
    return pl.pallas_call(
        body, name=name,
        grid_spec=pltpu.PrefetchScalarGridSpec(
            num_scalar_prefetch=1, grid=(r // tr,),
            in_specs=[pl.BlockSpec((N_CHIPS, tr, cdim), lambda i, me_ref: (0, i, 0)),
                      pl.BlockSpec((None, tr, cdim), lambda i, me_ref: (me_ref[0], i, 0))],
            out_specs=pl.BlockSpec((tr, cdim), lambda i, me_ref: (i, 0))),
        out_shape=jax.ShapeDtypeStruct((r, cdim), F32),
        compiler_params=_cparams(1),
    )(me, got, own)


def _adamw_math(w, g, m, v):
    c1 = 1.0 - ADAM_B1 ** ADAM_STEP
    c2 = 1.0 - ADAM_B2 ** ADAM_STEP
    nm = ADAM_B1 * m + (1.0 - ADAM_B1) * g
    nv = ADAM_B2 * v + (1.0 - ADAM_B2) * (g * g)
    delta = -ADAM_LR * ((nm / c1) / (jnp.sqrt(nv / c2) + ADAM_EPS) + ADAM_WD * w)
    return delta, nm, nv


def _adamw(w, g, m, v, name):
    r, cdim = w.shape
    tr = _row_tile(r, 256)

    def body(w_ref, g_ref, m_ref, v_ref, d_ref, nm_ref, nv_ref):
        d_ref[...], nm_ref[...], nv_ref[...] = _adamw_math(w_ref[...], g_ref[...], m_ref[...], v_ref[...])

    spec = pl.BlockSpec((tr, cdim), lambda i: (i, 0))
    return pl.pallas_call(
        body, name=name, grid=(r // tr,),
        in_specs=[spec] * 4, out_specs=[spec] * 3,
        out_shape=[jax.ShapeDtypeStruct((r, cdim), F32)] * 3,
        compiler_params=_cparams(1),
    )(w, g, m, v)


def _adamw_many(ws, gs, ms, vs, name, per_layer):
    n = len(ws)

    def body(*refs):
        for k in range(n):
            w, g, m, v = (refs[j * n + k][...] for j in range(4))
            outs = _adamw_math(w, g, m, v)
            for j in range(3):
                refs[(4 + j) * n + k][...] = outs[j]

    shapes = [jax.ShapeDtypeStruct(w.shape, F32) for w in ws]
    if per_layer:
        specs = [pl.BlockSpec((None,) + w.shape[1:], lambda l, nd=w.ndim: (l,) + (0,) * (nd - 1)) for w in ws]
        call = pl.pallas_call(body, name=name, grid=(N_LAYERS,), in_specs=specs * 4, out_specs=specs * 3,
                              out_shape=shapes * 3, compiler_params=_cparams(1))
    else:
        call = pl.pallas_call(body, name=name, out_shape=shapes * 3, compiler_params=_cparams())
    outs = call(*ws, *gs, *ms, *vs)
    return outs[0:n], outs[n:2 * n], outs[2 * n:3 * n]


def _cmul(ar, ai, br, bi):
    return ar * br - ai * bi, ar * bi + ai * br


def _discretise(a_re, a_im, log_dt, b_re, b_im):
    dt = jnp.exp(log_dt)
    mag = jnp.exp(a_re * dt)
    ab_re = mag * jnp.cos(a_im * dt)
    ab_im = mag * jnp.sin(a_im * dt)
    num_re = ab_re - 1.0
    num_im = ab_im
    den = a_re * a_re + a_im * a_im
    f_re = (num_re * a_re + num_im * a_im) / den
    f_im = (num_im * a_re - num_re * a_im) / den
    bb_re = f_re * b_re - f_im * b_im
    bb_im = f_re * b_im + f_im * b_re
    return ab_re, ab_im, bb_re, bb_im


def _disc_shapes():
    col = jax.ShapeDtypeStruct((1, N_STATES), F32)
    mat = jax.ShapeDtypeStruct((SSM_GROUP, N_STATES), F32)
    return col, mat


def _group_mask():
    row = lax.broadcasted_iota(jnp.int32, (CH_W, CH_S), 0)
    col = lax.broadcasted_iota(jnp.int32, (CH_W, CH_S), 1)
    return jnp.right_shift(row, SSM_GROUP.bit_length() - 1) == jnp.right_shift(col, SSM_STATE.bit_length() - 1)


def _block_diag(v, j):
    blk = v[:, CH_S * j:CH_S * (j + 1)]
    return jnp.where(_group_mask(), jnp.concatenate([blk] * (CH_W // SSM_GROUP), axis=0), 0.0)


def _block_diag_t(m):
    kept = jnp.where(_group_mask(), m, 0.0)
    return kept.reshape(CH_W // SSM_GROUP, SSM_GROUP, CH_S).sum(axis=0)


def _disc_fwd(a_re, a_im, log_dt, b_re, b_im, c_re, c_im, length):
    wide = jax.ShapeDtypeStruct((SSM_CHUNKS, CH_W, 2 * CH_S), BF16)
    tall = jax.ShapeDtypeStruct((SSM_CHUNKS, 2 * CH_S, CH_W), BF16)
    tab = jax.ShapeDtypeStruct((SSM_CHUNKS, length, 2 * CH_S), F32)

    def body(ar, ai, ld, br, bi, cr, ci, wb_ref, wbt_ref, wct_ref, wc_ref, tab_ref, rev_ref):
        ab_re, ab_im, bb_re, bb_im = _discretise(ar[...], ai[...], ld[...], br[...], bi[...])
        ccr, cci = cr[...], -ci[...]
        for j in range(SSM_CHUNKS):
            for lo, (vb, vc) in ((0, (bb_re, ccr)), (CH_S, (bb_im, cci))):
                mb, mc = _block_diag(vb, j), _block_diag(vc, j)
                wb_ref[j, :, lo:lo + CH_S] = mb.astype(BF16)
                wbt_ref[j, lo:lo + CH_S, :] = mb.T.astype(BF16)
                wct_ref[j, :, lo:lo + CH_S] = mc.astype(BF16)
                wc_ref[j, lo:lo + CH_S, :] = mc.T.astype(BF16)

        def step(j, carry):
            pr, pi = carry
            back = length - 1 - j
            for c in range(SSM_CHUNKS):
                lanes = slice(CH_S * c, CH_S * (c + 1))
                tab_ref[c, pl.ds(j, 1), 0:CH_S] = pr[:, lanes]
                tab_ref[c, pl.ds(j, 1), CH_S:2 * CH_S] = pi[:, lanes]
                rev_ref[c, pl.ds(back, 1), 0:CH_S] = pr[:, lanes]
                rev_ref[c, pl.ds(back, 1), CH_S:2 * CH_S] = -pi[:, lanes]
            return _cmul(pr, pi, ab_re, ab_im)

        lax.fori_loop(0, length, step, (ab_re, ab_im))

    return pl.pallas_call(body, name="ssm_discretise", out_shape=[wide, tall, wide, tall, tab, tab],
                          compiler_params=_cparams())(a_re, a_im, log_dt, b_re, b_im, c_re, c_im)


def _disc_bwd(a_re, a_im, log_dt, b_re, b_im, da, dwb, dwc):
    col, mat = _disc_shapes()

    def body(ar, ai, ld, br, bi, da_ref, dwb_ref, dwc_ref, o0, o1, o2, o3, o4, dcr_ref, dci_ref):
        g_ab = [jnp.concatenate([jnp.sum(da_ref[j, :, lo:lo + CH_S], axis=0, keepdims=True) for j in range(SSM_CHUNKS)],
                                axis=-1) for lo in (0, CH_S)]
        g_bb = [jnp.concatenate([_block_diag_t(dwb_ref[j, :, lo:lo + CH_S]) for j in range(SSM_CHUNKS)], axis=-1)
                for lo in (0, CH_S)]
        for ref, lo, sign in ((dcr_ref, 0, 1.0), (dci_ref, CH_S, -1.0)):
            ref[...] = sign * jnp.concatenate([_block_diag_t(dwc_ref[j, lo:lo + CH_S, :].T) for j in range(SSM_CHUNKS)],
                                              axis=-1)
        _, vjp = jax.vjp(_discretise, ar[...], ai[...], ld[...], br[...], bi[...])
        grads = vjp((g_ab[0], g_ab[1], g_bb[0], g_bb[1]))
        for o, val in zip((o0, o1, o2, o3, o4), grads):
            o[...] = val

    return pl.pallas_call(body, name="ssm_discretise_bwd", out_shape=[col, col, col, mat, mat, mat, mat],
                          compiler_params=_cparams())(a_re, a_im, log_dt, b_re, b_im, da, dwb, dwc)


def _interleave_chunks(v):
    rows, width = v.shape
    return pltpu.einshape("cjw->jcw", v.reshape(SUBLANES, rows // SUBLANES, width)).reshape(rows, width)


def _time_order(v):
    rows, width = v.shape
    return pltpu.einshape("jcw->cjw", v.reshape(rows // SUBLANES, SUBLANES, width)).reshape(rows, width)


def _head_ones():
    r = jnp.arange(ATTN_WIDTH) // HEAD_DIM
    return jnp.where(r[:, None] == r[None, :], 1.0 / HEAD_DIM, 0.0).astype(BF16)


def _in_proj(h, g1, w_in_l, qg, kg):
    s = h.shape[0]
    tm = _row_tile(s, 512)

    def body(h_ref, g_ref, w_ref, qg_ref, kg_ref, ones_ref, proj_ref, qkv_ref):
        x = h_ref[...]
        r = lax.rsqrt(jnp.mean(x * x, axis=-1, keepdims=True) + RMS_EPS)
        hn = (x * r * g_ref[...]).astype(BF16)
        for sh in range(N_CHIPS):
            proj_ref[:, IN_SHARD * sh:IN_SHARD * (sh + 1)] = _dot(hn, w_ref[sh])
        ones = ones_ref[...]
        q = proj_ref[:, 1024:1536]
        k = proj_ref[:, 1536:2048]
        rq = lax.rsqrt(_dot_hilo(q * q, ones) + RMS_EPS)
        rk = lax.rsqrt(_dot_hilo(k * k, ones) + RMS_EPS)
        qkv_ref[:, 0:512] = (q * rq * qg_ref[...] * ATTN_SCALE).astype(BF16)
        qkv_ref[:, 512:1024] = (k * rk * kg_ref[...]).astype(BF16)
        qkv_ref[:, 1024:1536] = proj_ref[:, 2048:2560].astype(BF16)

    full = lambda shape: pl.BlockSpec(shape, lambda i: (0,) * len(shape))
    return pl.pallas_call(
        body, name="in_proj", grid=(s // tm,),
        in_specs=[pl.BlockSpec((tm, D_MODEL), lambda i: (i, 0)), full((1, D_MODEL)),
                  full((N_CHIPS, D_MODEL, IN_SHARD)),
                  full((1, ATTN_WIDTH)), full((1, ATTN_WIDTH)), full((ATTN_WIDTH, ATTN_WIDTH))],
        out_specs=[pl.BlockSpec((tm, IN_COLS), lambda i: (i, 0)), pl.BlockSpec((tm, 3 * ATTN_WIDTH), lambda i: (i, 0))],
        out_shape=[jax.ShapeDtypeStruct((s, IN_COLS), F32), jax.ShapeDtypeStruct((s, 3 * ATTN_WIDTH), BF16)],
        compiler_params=_cparams(1),
    )(h, g1, w_in_l, qg, kg, _head_ones())


def _row_bcast(ref, k, lo):
    return jnp.broadcast_to(ref[pl.ds(k, 1), lo:lo + CH_S], (SUBLANES, CH_S))


def _chunk_scan(x_ref, tab_ref, carry_ref, length, reverse, tail=None):
    row = lax.broadcasted_iota(jnp.int32, (SUBLANES, CH_S), 0)
    one, full = (length - 1, 0) if reverse else (0, length - 1)
    ar, ai = _row_bcast(tab_ref, one, 0), _row_bcast(tab_ref, one, CH_S)
    fr, fi = _row_bcast(tab_ref, full, 0), _row_bcast(tab_ref, full, CH_S)
    step = lambda jj: (length - 1 - jj) if reverse else jj

    def local(jj, carry):
        cr, ci = carry
        r0 = pl.multiple_of(step(jj) * SUBLANES, SUBLANES)
        xr = x_ref[pl.ds(r0, SUBLANES), 0:CH_S] + (ar * cr - ai * ci)
        xi = x_ref[pl.ds(r0, SUBLANES), CH_S:2 * CH_S] + (ar * ci + ai * cr)
        x_ref[pl.ds(r0, SUBLANES), 0:CH_S] = xr
        x_ref[pl.ds(r0, SUBLANES), CH_S:2 * CH_S] = xi
        return xr, xi

    zero = jnp.zeros((SUBLANES, CH_S), F32)
    er, ei = lax.fori_loop(0, length, local, (zero, zero))

    first, shift = (SUBLANES - 1, SUBLANES - 1) if reverse else (0, 1)
    hr = jnp.where(row == first, carry_ref[:, 0:CH_S], 0.0)
    hi = jnp.where(row == first, carry_ref[:, CH_S:2 * CH_S], 0.0)
    sr, si = pltpu.roll(er, shift, 0), pltpu.roll(ei, shift, 0)
    for k in range(1, SUBLANES):
        tr, ti = pltpu.roll(hr, shift, 0), pltpu.roll(hi, shift, 0)
        here = row == ((SUBLANES - 1 - k) if reverse else k)
        hr, hi = (jnp.where(here, fr * tr - fi * ti + sr, hr), jnp.where(here, fr * ti + fi * tr + si, hi))
    last = 0 if reverse else SUBLANES - 1
    outr, outi = fr * hr - fi * hi + er, fr * hi + fi * hr + ei
    carry_ref[:, 0:CH_S] = jnp.broadcast_to(outr[last:last + 1, :], (SUBLANES, CH_S))
    carry_ref[:, CH_S:2 * CH_S] = jnp.broadcast_to(outi[last:last + 1, :], (SUBLANES, CH_S))

    def fix(jj, carry):
        j = step(jj)
        r0 = pl.multiple_of(j * SUBLANES, SUBLANES)
        pr, pi = _row_bcast(tab_ref, j, 0), _row_bcast(tab_ref, j, CH_S)
        xr = x_ref[pl.ds(r0, SUBLANES), 0:CH_S] + (pr * hr - pi * hi)
        xi = x_ref[pl.ds(r0, SUBLANES), CH_S:2 * CH_S] + (pr * hi + pi * hr)
        x_ref[pl.ds(r0, SUBLANES), 0:CH_S] = xr
        x_ref[pl.ds(r0, SUBLANES), CH_S:2 * CH_S] = xi
        if tail is None:
            return carry
        return tail(r0, xr, xi, carry)

    return fix, (hr, hi)


def _ssm_scan_fwd(proj, wb, tab, wc, gather=None, gather_bases=None):
    s = proj.shape[0]
    tm = _row_tile(s, SCAN_TILE)
    nt = s // tm
    length = tm // SUBLANES
    gather = [] if gather is None else gather
    ng = len(gather)

    def body(*refs):
        u_ref, wb_ref, tab_ref, wc_ref = refs[0:4]
        g_ins = refs[4:4 + ng]
        xs_ref, y_ref = refs[4 + ng:6 + ng]
        g_outs = refs[6 + ng:6 + 2 * ng]
        carry_ref = refs[6 + 2 * ng]
        sems = refs[7 + 2 * ng:]
        j, i = pl.program_id(0), pl.program_id(1)

        @pl.when(i == 0)
        def _():
            carry_ref[...] = jnp.zeros_like(carry_ref)

        if ng:
            @pl.when(jnp.logical_and(j == 0, i == 0))
            def _():
                _gather_start(g_ins, gather_bases, g_outs, sems)

        xs_ref[...] = _dot(_interleave_chunks(u_ref[...]).astype(BF16), wb_ref[...])
        fix, start = _chunk_scan(xs_ref, tab_ref, carry_ref, length, reverse=False)
        lax.fori_loop(0, length, fix, start, unroll=2)
        y_ref[...] = _time_order(_dot(xs_ref[...].astype(BF16), wc_ref[...]))

        if ng:
            @pl.when(jnp.logical_and(j == SSM_CHUNKS - 1, i == nt - 1))
            def _():
                _gather_finish(g_ins, gather_bases, g_outs, sems)

    outs = pl.pallas_call(
        body, name="ssm_scan_gather" if ng else "ssm_scan", grid=(SSM_CHUNKS, nt),
        in_specs=[pl.BlockSpec((tm, CH_W), lambda j, i: (i, j)),
                  pl.BlockSpec((None, CH_W, 2 * CH_S), lambda j, i: (j, 0, 0)),
                  pl.BlockSpec((None, length, 2 * CH_S), lambda j, i: (j, 0, 0)),
                  pl.BlockSpec((None, 2 * CH_S, CH_W), lambda j, i: (j, 0, 0))] + [ANY] * ng,
        out_specs=[pl.BlockSpec((None, tm, 2 * CH_S), lambda j, i: (j, i, 0)),
                   pl.BlockSpec((tm, CH_W), lambda j, i: (i, j))] + [ANY] * ng,
        out_shape=[jax.ShapeDtypeStruct((SSM_CHUNKS, s, 2 * CH_S), F32), jax.ShapeDtypeStruct((s, SSM_WIDTH), F32)]
        + _gather_outputs(gather),
        scratch_shapes=[pltpu.VMEM((SUBLANES, 2 * CH_S), F32)] + (_gather_sems(ng) if ng else []),
        compiler_params=_cparams(2),
    )(proj, wb, tab, wc, *gather)
    return outs[0], outs[1], (_gather_own(outs[2:], gather, gather_bases) if ng else [])


def _glu_forward(y, u, d, wg_ref, bg):
    yf = y + d * u
    z = _gelu(yf)
    zb = z.astype(BF16)
    zz = jnp.concatenate([_dot(zb, wg_ref[sh]) for sh in range(N_CHIPS)], axis=-1) + bg
    return yf, z, zz[:, 0:SSM_WIDTH], zz[:, SSM_WIDTH:2 * SSM_WIDTH]


def _ssm_glu_fwd(y, proj, d, w_glu_l, b_glu):
    s = y.shape[0]
    tm = _row_tile(s, 1024)

    def body(y_ref, u_ref, gs_ref, d_ref, wg_ref, bg_ref, o_ref):
        _, _, val, gate = _glu_forward(y_ref[...], u_ref[...], d_ref[...], wg_ref, bg_ref[...])
        gs = gs_ref[...]
        o_ref[...] = val * _sigmoid(gate) * (gs * _sigmoid(gs))

    row = lambda i: (i, 0)
    return pl.pallas_call(
        body, name="ssm_glu", grid=(s // tm,),
        in_specs=[pl.BlockSpec((tm, SSM_WIDTH), row), pl.BlockSpec((tm, SSM_WIDTH), row),
                  pl.BlockSpec((tm, SSM_WIDTH), lambda i: (i, 1)), pl.BlockSpec((1, SSM_WIDTH), lambda i: (0, 0)),
                  pl.BlockSpec((N_CHIPS, SSM_WIDTH, ROW_SHARD), lambda i: (0, 0, 0)),
                  pl.BlockSpec((1, 2 * SSM_WIDTH), lambda i: (0, 0))],
        out_specs=pl.BlockSpec((tm, SSM_WIDTH), row),
        out_shape=jax.ShapeDtypeStruct((s, SSM_WIDTH), F32),
        compiler_params=_cparams(1),
    )(y, proj, proj, d, w_glu_l, b_glu)


def _tri(kind):
    r = jnp.arange(ATTN_BLOCK)
    if kind == "suffix_incl":
        m = r[:, None] >= r[None, :]
    else:
        m = r[:, None] < r[None, :]
    return jnp.concatenate([m, jnp.ones_like(m)], axis=1).astype(BF16)


def _head_masks():
    lane = lax.broadcasted_iota(jnp.int32, (1, 2 * HEAD_DIM), 1)
    return [lane < HEAD_DIM, lane >= HEAD_DIM]


def _chain_step(t, base, n_sub, first, q_ref, k_ref, tri_ref, l_scr, per_chain):
    tb = ATTN_BLOCK
    row = lax.broadcasted_iota(jnp.int32, (tb, tb), 0)
    col = lax.broadcasted_iota(jnp.int32, (tb, tb), 1)
    masks = _head_masks()
    blks = [base + a - t for a in range(n_sub)]
    r0s = [pl.multiple_of(jnp.maximum(blk, 0) * tb, tb) for blk in blks]
    zs = []
    for a in range(n_sub):
        kb = k_ref[pl.ds(r0s[a], tb), :]
        qa = q_ref[a * tb:(a + 1) * tb, :]
        for mask in masks:
            zs.append(_dot_nt(jnp.where(mask, qa, jnp.zeros_like(qa)), kb))
    parts = []
    for z in zs:
        ls = jnp.minimum(-z, 0.0) - jnp.log(1.0 + jnp.exp(-jnp.abs(z)))
        if first:
            ls = jnp.where(col < row, ls, 0.0)
        parts.append(_split_hilo(ls))
    tri = tri_ref[...]
    sums = [_dot(hi, tri) + _dot(lo, tri) for hi, lo in parts]
    top = None
    ws = []
    for c, (z, sm) in enumerate(zip(zs, sums)):
        if first:
            lsum = jnp.zeros((tb, tb), F32)
        else:
            lsum = l_scr[c] + jnp.where(blks[c // 2] >= 0, 0.0, -1e30)
        w = jnp.exp(z + sm[:, 0:tb] + lsum)
        if first:
            w = jnp.where(col < row, w, 0.0)
        ws.append(w)
        lsum = lsum + sm[:, tb:2 * tb]
        l_scr[c] = lsum
        top = lsum if top is None else jnp.maximum(top, lsum)
    for c, (z, w) in enumerate(zip(zs, ws)):
        per_chain(c // 2, c % 2, c, r0s[c // 2], z, w)
    return jnp.max(top)


def _chain_sweep(base, n_sub, q_ref, k_ref, tri_ref, l_scr, per_chain):
    top = _chain_step(0, base, n_sub, True, q_ref, k_ref, tri_ref, l_scr, functools.partial(per_chain, 0))

    def cond(carry):
        t, top = carry
        return jnp.logical_and(t <= base + n_sub - 1, top > EXP_ZERO)

    def step(carry):
        t, _ = carry
        return t + 1, _chain_step(t, base, n_sub, False, q_ref, k_ref, tri_ref, l_scr, functools.partial(per_chain, t))

    steps, _ = lax.while_loop(cond, step, (jnp.int32(1), top))
    return steps


ATTN_SUB_FWD = 8
ATTN_SUB_BWD = 8


def _attn_fwd(qkv, proj, gather=None, gather_bases=None):
    s = qkv.shape[0]
    tb = ATTN_BLOCK
    n_sub = min(ATTN_SUB_FWD, s // tb)
    tq = n_sub * tb
    n_hp = ATTN_WIDTH // (2 * HEAD_DIM)
    gather = [] if gather is None else gather
    ng = len(gather)

    def body(*refs):
        q_ref, k_ref, v_ref, g_ref, tri_ref = refs[0:5]
        g_ins = refs[5:5 + ng]
        o_ref, ya_ref = refs[5 + ng:7 + ng]
        g_outs = refs[7 + ng:7 + 2 * ng]
        l_scr = refs[7 + 2 * ng]
        sems = refs[8 + 2 * ng:]
        i = pl.program_id(1)
        masks = _head_masks()
        o_ref[...] = jnp.zeros_like(o_ref)

        if ng:
            @pl.when(jnp.logical_and(pl.program_id(0) == 0, i == 0))
            def _():
                _gather_start(g_ins, gather_bases, g_outs, sems)

        def per_chain(t, a, h, c, r0, z, w):
            vb = v_ref[pl.ds(r0, tb), :]
            vb = jnp.where(masks[h], vb, jnp.zeros_like(vb))
            o_ref[a * tb:(a + 1) * tb, :] += _dot(w.astype(BF16), vb)

        _chain_sweep(i * n_sub, n_sub, q_ref, k_ref, tri_ref, l_scr, per_chain)
        g = g_ref[...]
        ya_ref[...] = o_ref[...] * (g * _sigmoid(g))

        if ng:
            @pl.when(jnp.logical_and(pl.program_id(0) == n_hp - 1, i == s // tq - 1))
            def _():
                _gather_finish(g_ins, gather_bases, g_outs, sems)

    hp_blk = lambda off: pl.BlockSpec((tq, 2 * HEAD_DIM), lambda hp, i: (i, off + hp))
    res = lambda off: pl.BlockSpec((s, 2 * HEAD_DIM), lambda hp, i: (0, off + hp))
    outs = pl.pallas_call(
        body, name="attn_fwd_gather" if ng else "attn_fwd", grid=(n_hp, s // tq),
        in_specs=[hp_blk(0), res(4), res(8), hp_blk(20), pl.BlockSpec((tb, 2 * tb), lambda hp, i: (0, 0))] + [ANY] * ng,
        out_specs=[hp_blk(0), hp_blk(0)] + [ANY] * ng,
        out_shape=[jax.ShapeDtypeStruct((s, ATTN_WIDTH), F32)] * 2 + _gather_outputs(gather),
        scratch_shapes=[pltpu.VMEM((2 * n_sub, tb, tb), F32)] + (_gather_sems(ng) if ng else []),
        compiler_params=_cparams(2),
    )(qkv, qkv, qkv, proj, _tri("suffix_incl"), *gather)
    return outs[0], outs[1], (_gather_own(outs[2:], gather, gather_bases) if ng else [])


def _rms_rows(x, g):
    r = lax.rsqrt(jnp.mean(x * x, axis=-1, keepdims=True) + RMS_EPS)
    return r, x * r * g


def _ple_forward(h1, p, g2, wpg_ref, wpp_ref):
    r2, hn2 = _rms_rows(h1, g2)
    hb = hn2.astype(BF16)
    gpre = _dot(hb[:, 0:ROW_SHARD], wpg_ref[0])
    for sh in range(1, N_CHIPS):
        gpre = gpre + _dot(hb[:, ROW_SHARD * sh:ROW_SHARD * (sh + 1)], wpg_ref[sh])
    gate = _sigmoid(gpre)
    pb = p.astype(BF16)
    pp = jnp.concatenate([_dot(pb, wpp_ref[sh]) for sh in range(N_CHIPS)], axis=-1)
    return r2, hb, gate, pp


def _colsum8(a):
    t = a.shape[0]
    return a.reshape(t // SUBLANES, SUBLANES, a.shape[1]).sum(axis=0)


def _sq_err_grad(y, target):
    e = y - target
    sq = _colsum8(e * e)
    part = sq[:, 0:128]
    for b in range(1, D_MODEL // 128):
        part = part + sq[:, 128 * b:128 * (b + 1)]
    return e / D_MODEL, part


def _out_ple(h, ys, ya, p, g2, w_out_l, w_pg_l, w_pp_l, target=None):
    s = h.shape[0]
    tm = _row_tile(s, 512)
    last = target is not None

    def body(*refs):
        h_ref, ys_ref, ya_ref, p_ref, g_ref, wo_ref, wpg_ref, wpp_ref = refs[0:8]
        h1_ref, h2_ref = refs[8 + last], refs[9 + last]
        ysb = ys_ref[...].astype(BF16)
        yab = ya_ref[...].astype(BF16)
        h1 = h_ref[...]
        for sh, src in enumerate((ysb[:, 0:ROW_SHARD], ysb[:, ROW_SHARD:], yab[:, 0:ROW_SHARD], yab[:, ROW_SHARD:])):
            h1 = h1 + _dot(src, wo_ref[sh])
        _, _, gate, pp = _ple_forward(h1, p_ref[...], g_ref[...], wpg_ref, wpp_ref)
        h1_ref[...] = h1
        h2 = h1 + gate * pp
        if last:
            acc_ref = refs[11]

            @pl.when(pl.program_id(0) == 0)
            def _():
                acc_ref[...] = jnp.zeros_like(acc_ref)

            h2_ref[...], part = _sq_err_grad(h2, refs[8][...])
            acc_ref[...] += part
        else:
            h2_ref[...] = h2

    row = lambda i: (i, 0)
    big = pl.BlockSpec((tm, D_MODEL), row)
    wspec = lambda r, cdim: pl.BlockSpec((N_CHIPS, r, cdim), lambda i: (0, 0, 0))
    acc = pl.BlockSpec((SUBLANES, 128), lambda i: (0, 0))
    return pl.pallas_call(
        body, name="out_ple_loss" if last else "out_ple", grid=(s // tm,),
        in_specs=[big, pl.BlockSpec((tm, SSM_WIDTH), row), pl.BlockSpec((tm, ATTN_WIDTH), row),
                  pl.BlockSpec((tm, PLE_DIM), row), pl.BlockSpec((1, D_MODEL), lambda i: (0, 0)),
                  wspec(ROW_SHARD, D_MODEL), wspec(ROW_SHARD, D_MODEL), wspec(PLE_DIM, ROW_SHARD)] + [big] * last,
        out_specs=[big] * 2 + [acc] * last,
        out_shape=[jax.ShapeDtypeStruct((s, D_MODEL), F32)] * 2 + [jax.ShapeDtypeStruct((SUBLANES, 128), F32)] * last,
        compiler_params=_cparams(1),
    )(h, ys, ya, p, g2, w_out_l, w_pg_l, w_pp_l, *([target] if last else []))


def _rms_bwd(x, r, g, dy):
    gdy = g * dy
    dx = r * gdy - x * (r * r * r) * jnp.mean(x * gdy, axis=-1, keepdims=True)
    return dx, x * r * dy


def _out_ple_bwd(dh2, h1, p, g2, w_out_l, w_pg_l, w_pp_l):
    s = h1.shape[0]
    tm = _row_tile(s, 512)

    def body(dh2_ref, h1_ref, p_ref, g_ref, wo_ref, wpg_ref, wpp_ref,
             dh1_ref, dmix_ref, hn_ref, dgp_ref, dpp_ref, dh1b_ref, dg_ref):
        @pl.when(pl.program_id(0) == 0)
        def _():
            dg_ref[...] = jnp.zeros_like(dg_ref)

        h1 = h1_ref[...]
        dh2 = dh2_ref[...]
        g2v = g_ref[...]
        r2, hb, gate, pp = _ple_forward(h1, p_ref[...], g2v, wpg_ref, wpp_ref)
        dgp = (dh2 * pp) * gate * (1.0 - gate)
        dgpb = dgp.astype(BF16)
        dhn = jnp.concatenate([_dot_nt(dgpb, wpg_ref[sh]) for sh in range(N_CHIPS)], axis=-1)
        dx, dgrow = _rms_bwd(h1, r2, g2v, dhn)
        dh1 = dh2 + dx
        dh1b = dh1.astype(BF16)
        dh1_ref[...] = dh1
        dh1b_ref[...] = dh1b
        hn_ref[...] = hb
        dgp_ref[...] = dgpb
        dpp_ref[...] = (dh2 * gate).astype(BF16)
        dg_ref[...] += _colsum8(dgrow)
        for sh in range(N_CHIPS):
            dmix_ref[:, ROW_SHARD * sh:ROW_SHARD * (sh + 1)] = _dot_nt(dh1b, wo_ref[sh])

    row = lambda i: (i, 0)
    wspec = lambda r, cdim: pl.BlockSpec((N_CHIPS, r, cdim), lambda i: (0, 0, 0))
    big = pl.BlockSpec((tm, D_MODEL), row)
    return pl.pallas_call(
        body, name="out_ple_bwd", grid=(s // tm,),
        in_specs=[big, big, pl.BlockSpec((tm, PLE_DIM), row), pl.BlockSpec((1, D_MODEL), lambda i: (0, 0)),
                  wspec(ROW_SHARD, D_MODEL), wspec(ROW_SHARD, D_MODEL), wspec(PLE_DIM, ROW_SHARD)],
        out_specs=[big] * 6 + [pl.BlockSpec((SUBLANES, D_MODEL), lambda i: (0, 0))],
        out_shape=[jax.ShapeDtypeStruct((s, D_MODEL), F32)] * 2 + [jax.ShapeDtypeStruct((s, D_MODEL), BF16)] * 4
        + [jax.ShapeDtypeStruct((SUBLANES, D_MODEL), F32)],
        compiler_params=_cparams(1),
    )(dh2, h1, p, g2, w_out_l, w_pg_l, w_pp_l)


def _tn_matmul(a, b, n_blocks, block_a, name, into=None, first_block=0, total_blocks=None):
    s = a.shape[0]
    tk = _row_tile(s, 1024)
    nk = s // tk
    total_blocks = n_blocks if total_blocks is None else total_blocks
    ka, nb = a.shape[1], b.shape[1]
    if block_a:
        ka //= n_blocks
    else:
        nb //= n_blocks

    def body(*refs):
        a_ref, b_ref, o_ref, acc_ref = refs[0], refs[1], refs[-2], refs[-1]

        @pl.when(pl.program_id(0) == 0)
        def _():
            acc_ref[...] = jnp.zeros_like(acc_ref)

        at = a_ref[...].astype(BF16).T
        bb = b_ref[...].astype(BF16)
        for sh in range(n_blocks):
            if block_a:
                acc_ref[sh] += _dot(at[ka * sh:ka * (sh + 1), :], bb)
            else:
                acc_ref[sh] += _dot(at, bb[:, nb * sh:nb * (sh + 1)])

        @pl.when(pl.program_id(0) == nk - 1)
        def _():
            o_ref[...] = acc_ref[...].astype(BF16)

    in_specs = [pl.BlockSpec((tk, a.shape[1]), lambda i: (i, 0)), pl.BlockSpec((tk, b.shape[1]), lambda i: (i, 0))]
    operands = [a, b]
    aliases = {}
    if into is not None:
        in_specs.append(ANY)
        operands.append(into)
        aliases = {2: 0}
    return pl.pallas_call(
        body, name=name, grid=(nk,),
        in_specs=in_specs,
        out_specs=pl.BlockSpec((n_blocks, ka, nb), lambda i: (first_block // n_blocks, 0, 0)),
        out_shape=jax.ShapeDtypeStruct((total_blocks, ka, nb), BF16),
        scratch_shapes=[pltpu.VMEM((n_blocks, ka, nb), F32)],
        input_output_aliases=aliases,
        compiler_params=_cparams(1),
    )(*operands)


def _attn_bwd(qkv, o, proj, dmix, scatter=None):
    scatter = [] if scatter is None else scatter
    nsc = len(scatter)
    s = qkv.shape[0]
    tb = ATTN_BLOCK
    nq = s // tb
    n_sub = min(ATTN_SUB_BWD, nq)
    tq = n_sub * tb
    n_chain = 2 * n_sub

    def body(*refs):
        q_ref, k_ref, v_ref, o_ref, g_ref, dya_ref, tri_s_ref, tri_p_ref = refs[0:8]
        sc_ins = refs[8:8 + nsc]
        dq_ref, dk_ref, dv_ref, dg_ref = refs[8 + nsc:12 + nsc]
        sc_outs = refs[12 + nsc:12 + 2 * nsc]
        do_scr, l_scr, g_scr, s_scr, w_scr = refs[12 + 2 * nsc:17 + 2 * nsc]
        sc_sems = refs[17 + 2 * nsc:]
        i = pl.program_id(1)
        base = i * n_sub

        if nsc:
            @pl.when(jnp.logical_and(pl.program_id(0) == 0, i == 0))
            def _():
                _scatter_start(sc_ins, sc_outs, sc_sems)

        @pl.when(i == 0)
        def _():
            dk_ref[...] = jnp.zeros_like(dk_ref)
            dv_ref[...] = jnp.zeros_like(dv_ref)

        g = g_ref[...]
        sg = _sigmoid(g)
        dya = dya_ref[...]
        do_scr[...] = (dya * (g * sg)).astype(BF16)
        dg_ref[...] = dya * o_ref[...] * (sg * (1.0 + g * (1.0 - sg)))
        dq_ref[...] = jnp.zeros_like(dq_ref)
        g_scr[...] = jnp.zeros_like(g_scr)
        masks = _head_masks()

        def keep(t, a, h, c, r0, z, w):
            s_scr[c, t] = _sigmoid(z).astype(BF16)
            w_scr[c, t] = w.astype(BF16)

        steps = _chain_sweep(base, n_sub, q_ref, k_ref, tri_s_ref, l_scr, keep)
        row = lax.broadcasted_iota(jnp.int32, (tb, tb), 0)
        col = lax.broadcasted_iota(jnp.int32, (tb, tb), 1)

        def back(it, carry):
            t = steps - 1 - it
            r0s = [pl.multiple_of(jnp.maximum(base + a - t, 0) * tb, tb) for a in range(n_sub)]
            qhs, dohs, khs, gws = [], [], [], []
            for a in range(n_sub):
                kb = k_ref[pl.ds(r0s[a], tb), :]
                vb = v_ref[pl.ds(r0s[a], tb), :]
                qa = q_ref[a * tb:(a + 1) * tb, :]
                doa = do_scr[a * tb:(a + 1) * tb, :]
                for h, mask in enumerate(masks):
                    qhs.append(jnp.where(mask, qa, jnp.zeros_like(qa)))
                    khs.append(jnp.where(mask, kb, jnp.zeros_like(kb)))
                    dohs.append(jnp.where(mask, doa, jnp.zeros_like(doa)))
                    gws.append(w_scr[2 * a + h, t].astype(F32) * _dot_nt(dohs[-1], vb))
            parts = [_split_hilo(gw) for gw in gws]
            tri = tri_p_ref[...]
            sums = [_dot(hi, tri) + _dot(lo, tri) for hi, lo in parts]
            dzs = []
            for c, (gw, sm) in enumerate(zip(gws, sums)):
                gsum = g_scr[c]
                dz = gw - (gw + sm[:, 0:tb] + gsum) * s_scr[c, t].astype(F32)
                dz = jnp.where(col < row + t * tb, dz, 0.0)
                g_scr[c] = gsum + sm[:, tb:2 * tb]
                dzs.append(dz.astype(BF16))
            for c, dzb in enumerate(dzs):
                a = c // 2
                dk_ref[pl.ds(r0s[a], tb), :] += _dot_tn(dzb, qhs[c])
                dv_ref[pl.ds(r0s[a], tb), :] += _dot_tn(w_scr[c, t], dohs[c])
                dq_ref[a * tb:(a + 1) * tb, :] += _dot(dzb, khs[c])
            return carry

        lax.fori_loop(0, steps, back, 0)

        if nsc:
            @pl.when(jnp.logical_and(pl.program_id(0) == n_hp - 1, i == s // tq - 1))
            def _():
                _scatter_finish(sc_ins, sc_outs, sc_sems)

    n_hp = ATTN_WIDTH // (2 * HEAD_DIM)
    hp_blk = lambda off: pl.BlockSpec((tq, 2 * HEAD_DIM), lambda hp, i: (i, off + hp))
    res = lambda off: pl.BlockSpec((s, 2 * HEAD_DIM), lambda hp, i: (0, off + hp))
    tri = pl.BlockSpec((tb, 2 * tb), lambda hp, i: (0, 0))
    outs = pl.pallas_call(
        body, name="attn_bwd_scatter" if nsc else "attn_bwd", grid=(n_hp, s // tq),
        in_specs=[hp_blk(0), res(4), res(8), hp_blk(0), hp_blk(20), hp_blk(4), tri, tri] + [ANY] * nsc,
        out_specs=[hp_blk(0), res(0), res(0), hp_blk(0)] + [ANY] * nsc,
        out_shape=[jax.ShapeDtypeStruct((s, ATTN_WIDTH), F32)] * 4 + [jax.ShapeDtypeStruct(a.shape, a.dtype) for a in scatter],
        scratch_shapes=[pltpu.VMEM((tq, 2 * HEAD_DIM), BF16), pltpu.VMEM((n_chain, tb, tb), F32),
                        pltpu.VMEM((n_chain, tb, tb), F32), pltpu.VMEM((n_chain, nq, tb, tb), BF16),
                        pltpu.VMEM((n_chain, nq, tb, tb), BF16)] + (_scatter_sems(nsc) if nsc else []),
        compiler_params=_cparams(2, vmem_limit=V7X_VMEM_LIMIT_ATTN_BWD),
    )(qkv, qkv, qkv, o, proj, dmix, _tri("suffix_incl"), _tri("prefix_strict"), *scatter)
    return outs[0], outs[1], outs[2], outs[3], outs[4:]


def _ssm_glu_bwd(dmix, y, proj, d, w_glu_l, b_glu):
    s = y.shape[0]
    tm = _row_tile(s, 1024)

    def body(dys_ref, y_ref, u_ref, gs_ref, d_ref, wg_ref, bg_ref,
             dyf_ref, du_ref, dgs_ref, z_ref, dzz_ref, dd_ref, db_ref):
        @pl.when(pl.program_id(0) == 0)
        def _():
            dd_ref[...] = jnp.zeros_like(dd_ref)
            db_ref[...] = jnp.zeros_like(db_ref)

        u = u_ref[...]
        dv = d_ref[...]
        yf, z, val, gate = _glu_forward(y_ref[...], u, dv, wg_ref, bg_ref[...])
        gs = gs_ref[...]
        sgs = _sigmoid(gs)
        sgate = _sigmoid(gate)
        dys = dys_ref[...]
        dgv = dys * (gs * sgs)
        dgs_ref[...] = dys * (val * sgate) * (sgs * (1.0 + gs * (1.0 - sgs)))
        dzz = jnp.concatenate([dgv * sgate, dgv * val * sgate * (1.0 - sgate)], axis=-1)
        dzzb = dzz.astype(BF16)
        dz = _dot_nt(dzzb[:, 0:ROW_SHARD], wg_ref[0])
        for sh in range(1, N_CHIPS):
            dz = dz + _dot_nt(dzzb[:, ROW_SHARD * sh:ROW_SHARD * (sh + 1)], wg_ref[sh])
        dyf = dz * _gelu_grad(yf)
        dyf_ref[...] = dyf
        du_ref[...] = dyf * dv
        z_ref[...] = z.astype(BF16)
        dzz_ref[...] = dzzb
        dd_ref[...] += _colsum8(dyf * u)
        db_ref[...] += _colsum8(dzz)

    row = lambda i: (i, 0)
    half = pl.BlockSpec((tm, SSM_WIDTH), row)
    return pl.pallas_call(
        body, name="ssm_glu_bwd", grid=(s // tm,),
        in_specs=[half, half, half, pl.BlockSpec((tm, SSM_WIDTH), lambda i: (i, 1)),
                  pl.BlockSpec((1, SSM_WIDTH), lambda i: (0, 0)),
                  pl.BlockSpec((N_CHIPS, SSM_WIDTH, ROW_SHARD), lambda i: (0, 0, 0)),
                  pl.BlockSpec((1, 2 * SSM_WIDTH), lambda i: (0, 0))],
        out_specs=[half, half, half, half, pl.BlockSpec((tm, 2 * SSM_WIDTH), row),
                   pl.BlockSpec((SUBLANES, SSM_WIDTH), lambda i: (0, 0)),
                   pl.BlockSpec((SUBLANES, 2 * SSM_WIDTH), lambda i: (0, 0))],
        out_shape=[jax.ShapeDtypeStruct((s, SSM_WIDTH), F32)] * 3
        + [jax.ShapeDtypeStruct((s, SSM_WIDTH), BF16), jax.ShapeDtypeStruct((s, 2 * SSM_WIDTH), BF16),
           jax.ShapeDtypeStruct((SUBLANES, SSM_WIDTH), F32), jax.ShapeDtypeStruct((SUBLANES, 2 * SSM_WIDTH), F32)],
        compiler_params=_cparams(1),
    )(dmix, y, proj, proj, d, w_glu_l, b_glu)


def _ssm_scan_bwd(dyf, xs, proj, wct, tab_rev, wbt):
    s = dyf.shape[0]
    tm = _row_tile(s, SCAN_TILE)
    nt = s // tm
    length = tm // SUBLANES

    def body(dy_ref, xs_ref, u_ref, wct_ref, tab_ref, wbt_ref, du_ref, dwc_ref, dwb_ref, da_ref, lam_ref, carry_ref):
        @pl.when(pl.program_id(1) == 0)
        def _():
            carry_ref[...] = jnp.zeros_like(carry_ref)
            dwc_ref[...] = jnp.zeros_like(dwc_ref)
            dwb_ref[...] = jnp.zeros_like(dwb_ref)
            da_ref[...] = jnp.zeros_like(da_ref)

        dyp = _interleave_chunks(dy_ref[...]).astype(BF16)
        up = _interleave_chunks(u_ref[...]).astype(BF16)
        lam_ref[...] = _dot(dyp, wct_ref[...])

        def tail(r0, lr, li, carry):
            er, ei, dar, dai = carry
            xr = xs_ref[pl.ds(r0, SUBLANES), 0:CH_S]
            xi = xs_ref[pl.ds(r0, SUBLANES), CH_S:2 * CH_S]
            return lr, li, dar + (xr * er + xi * ei), dai + (xr * ei - xi * er)

        fix, (gr, gi) = _chunk_scan(lam_ref, tab_ref, carry_ref, length, reverse=True, tail=tail)
        zero = jnp.zeros((SUBLANES, CH_S), F32)
        _, _, dar, dai = lax.fori_loop(0, length, fix, (gr, gi, zero, zero), unroll=2)
        da_ref[:, 0:CH_S] += dar
        da_ref[:, CH_S:2 * CH_S] += dai
        lamb = lam_ref[...].astype(BF16)
        du_ref[...] = _time_order(_dot(lamb, wbt_ref[...]))
        dwc_ref[...] += _dot_tn(xs_ref[...].astype(BF16), dyp)
        dwb_ref[...] += _dot_tn(up, lamb)

    rev = lambda j, i: (nt - 1 - i, j)
    return pl.pallas_call(
        body, name="ssm_scan_bwd", grid=(SSM_CHUNKS, nt),
        in_specs=[pl.BlockSpec((tm, CH_W), rev),
                  pl.BlockSpec((None, tm, 2 * CH_S), lambda j, i: (j, nt - 1 - i, 0)),
                  pl.BlockSpec((tm, CH_W), rev),
                  pl.BlockSpec((None, CH_W, 2 * CH_S), lambda j, i: (j, 0, 0)),
                  pl.BlockSpec((None, length, 2 * CH_S), lambda j, i: (j, 0, 0)),
                  pl.BlockSpec((None, 2 * CH_S, CH_W), lambda j, i: (j, 0, 0))],
        out_specs=[pl.BlockSpec((tm, CH_W), rev),
                   pl.BlockSpec((None, 2 * CH_S, CH_W), lambda j, i: (j, 0, 0)),
                   pl.BlockSpec((None, CH_W, 2 * CH_S), lambda j, i: (j, 0, 0)),
                   pl.BlockSpec((None, SUBLANES, 2 * CH_S), lambda j, i: (j, 0, 0))],
        out_shape=[jax.ShapeDtypeStruct((s, SSM_WIDTH), F32),
                   jax.ShapeDtypeStruct((SSM_CHUNKS, 2 * CH_S, CH_W), F32),
                   jax.ShapeDtypeStruct((SSM_CHUNKS, CH_W, 2 * CH_S), F32),
                   jax.ShapeDtypeStruct((SSM_CHUNKS, SUBLANES, 2 * CH_S), F32)],
        scratch_shapes=[pltpu.VMEM((tm, 2 * CH_S), F32), pltpu.VMEM((SUBLANES, 2 * CH_S), F32)],
        compiler_params=_cparams(2),
    )(dyf, xs, proj, wct, tab_rev, wbt)


def _in_proj_bwd(h, g1, w_in_l, qg, kg, proj, du_a, du_b, dgs, dq, dk, dv, dga, dh1):
    s = h.shape[0]
    tm = _row_tile(s, 256)

    def body(h_ref, g_ref, w_ref, qg_ref, kg_ref, ones_ref, q_ref, k_ref, dua_ref, dub_ref, dgs_ref, dq_ref, dk_ref,
             dv_ref, dga_ref, dh1_ref, dh_ref, hn_ref, dp_ref, dg1_ref, dqg_ref, dkg_ref):
        @pl.when(pl.program_id(0) == 0)
        def _():
            dg1_ref[...] = jnp.zeros_like(dg1_ref)
            dqg_ref[...] = jnp.zeros_like(dqg_ref)
            dkg_ref[...] = jnp.zeros_like(dkg_ref)

        ones = ones_ref[...]

        def head_norm_bwd(x, gain, dy):
            r = lax.rsqrt(_dot_hilo(x * x, ones) + RMS_EPS)
            gdy = gain * dy
            dx = r * gdy - x * (r * r * r) * _dot_hilo(x * gdy, ones)
            return dx, x * r * dy

        dqr, dqg_rows = head_norm_bwd(q_ref[...], qg_ref[...], dq_ref[...] * ATTN_SCALE)
        dkr, dkg_rows = head_norm_bwd(k_ref[...], kg_ref[...], dk_ref[...])
        dqg_ref[...] += _colsum8(dqg_rows)
        dkg_ref[...] += _colsum8(dkg_rows)
        dp_ref[:, 0:512] = (dua_ref[...] + dub_ref[...]).astype(BF16)
        dp_ref[:, 512:1024] = dgs_ref[...].astype(BF16)
        dp_ref[:, 1024:1536] = dqr.astype(BF16)
        dp_ref[:, 1536:2048] = dkr.astype(BF16)
        dp_ref[:, 2048:2560] = dv_ref[...].astype(BF16)
        dp_ref[:, 2560:3072] = dga_ref[...].astype(BF16)
        dhn = _dot_nt(dp_ref[:, 0:IN_SHARD], w_ref[0])
        for sh in range(1, N_CHIPS):
            dhn = dhn + _dot_nt(dp_ref[:, IN_SHARD * sh:IN_SHARD * (sh + 1)], w_ref[sh])
        x = h_ref[...]
        gv = g_ref[...]
        r, hn = _rms_rows(x, gv)
        dx, dg_rows = _rms_bwd(x, r, gv, dhn)
        dh_ref[...] = dh1_ref[...] + dx
        hn_ref[...] = hn.astype(BF16)
        dg1_ref[...] += _colsum8(dg_rows)

    row = lambda i: (i, 0)
    full = lambda shape: pl.BlockSpec(shape, lambda i: (0,) * len(shape))
    big = pl.BlockSpec((tm, D_MODEL), row)
    half = pl.BlockSpec((tm, 512), row)
    return pl.pallas_call(
        body, name="in_proj_bwd", grid=(s // tm,),
        in_specs=[big, full((1, D_MODEL)), full((N_CHIPS, D_MODEL, IN_SHARD)),
                  full((1, ATTN_WIDTH)), full((1, ATTN_WIDTH)), full((ATTN_WIDTH, ATTN_WIDTH)),
                  pl.BlockSpec((tm, 512), lambda i: (i, 2)), pl.BlockSpec((tm, 512), lambda i: (i, 3)),
                  half, half, half, half, half, half, half, big],
        out_specs=[big, big, pl.BlockSpec((tm, IN_COLS), row), pl.BlockSpec((SUBLANES, D_MODEL), lambda i: (0, 0)),
                   pl.BlockSpec((SUBLANES, ATTN_WIDTH), lambda i: (0, 0)), pl.BlockSpec((SUBLANES, ATTN_WIDTH), lambda i: (0, 0))],
        out_shape=[jax.ShapeDtypeStruct((s, D_MODEL), F32), jax.ShapeDtypeStruct((s, D_MODEL), BF16),
                   jax.ShapeDtypeStruct((s, IN_COLS), BF16), jax.ShapeDtypeStruct((SUBLANES, D_MODEL), F32),
                   jax.ShapeDtypeStruct((SUBLANES, ATTN_WIDTH), F32), jax.ShapeDtypeStruct((SUBLANES, ATTN_WIDTH), F32)],
        compiler_params=_cparams(1),
    )(h, g1, w_in_l, qg, kg, _head_ones(), proj, proj, du_a, du_b, dgs, dq, dk, dv, dga, dh1)


SMALL_NAMES = ("mix_norm_g", "ssm_a_re", "ssm_a_im", "ssm_log_dt", "ssm_b_re", "ssm_b_im", "ssm_c_re", "ssm_c_im",
               "ssm_d", "ssm_b_glu", "q_norm_g", "k_norm_g", "ple_norm_g")
SMALL_4D = ("ssm_b_re", "ssm_b_im", "ssm_c_re", "ssm_c_im")
BIG_NAMES = ("w_in", "ssm_w_glu", "w_out", "w_ple_gate", "w_ple_proj")


def _ssm_setup(sm, layer, length):
    col = lambda a: a[layer].reshape(1, N_STATES)
    a_re, a_im = col(sm["ssm_a_re"]), col(sm["ssm_a_im"])
    log_dt = jnp.repeat(sm["ssm_log_dt"][layer], SSM_STATE).reshape(1, N_STATES)
    b_re = sm["ssm_b_re"][layer].reshape(N_STATES, SSM_GROUP).T
    b_im = sm["ssm_b_im"][layer].reshape(N_STATES, SSM_GROUP).T
    by_channel = lambda c: c[layer].transpose(1, 0, 2).reshape(SSM_GROUP, N_STATES)
    disc_in = (a_re, a_im, log_dt, b_re, b_im)
    wb, wbt, wct, wc, tab, tab_rev = _disc_fwd(*disc_in, by_channel(sm["ssm_c_re"]), by_channel(sm["ssm_c_im"]), length)
    return dict(disc_in=disc_in, wb=wb, wbt=wbt, wc=wc, wct=wct, tab=tab, tab_rev=tab_rev)


def _whole_blocks(names, gathered):
    return {n: g.reshape(N_CHIPS, 2 * g.shape[2], g.shape[3]) for n, g in zip(names, gathered)}


def _local_step(x, p, target, sm, w_in0, local=None, gathered=None, layer1_hook=None):
    wg = [dict(w_in=w_in0), {}] if gathered is None else gathered
    tile8 = lambda a: jnp.tile(a, ATTN_WIDTH // HEAD_DIM).reshape(1, ATTN_WIDTH)
    saved = []
    h = x
    for l in range(N_LAYERS):
        ssm = _ssm_setup(sm, l, _row_tile(x.shape[0], SCAN_TILE) // SUBLANES)
        g1 = sm["mix_norm_g"][l].reshape(1, D_MODEL)
        g2 = sm["ple_norm_g"][l].reshape(1, D_MODEL)
        qg, kg = tile8(sm["q_norm_g"][l]), tile8(sm["k_norm_g"][l])
        dsk = sm["ssm_d"][l].reshape(1, SSM_WIDTH)
        bgl = sm["ssm_b_glu"][l].reshape(1, 2 * SSM_WIDTH)
        proj, qkv = _in_proj(h, g1, wg[l]["w_in"], qg, kg)
        if l == 0 and local is not None:
            rest = BIG_NAMES[1:]
            xs, y, got = _ssm_scan_fwd(proj, ssm["wb"], ssm["tab"], ssm["wc"], [local[n] for n in rest], [0] * len(rest))
            wg[0].update(_whole_blocks(rest, got))
            ys = _ssm_glu_fwd(y, proj, dsk, wg[0]["ssm_w_glu"], bgl)
            o, ya, got = _attn_fwd(qkv, proj, [local[n] for n in BIG_NAMES], [2] * len(BIG_NAMES))
            wg[1].update(_whole_blocks(BIG_NAMES, got))
        else:
            xs, y, _ = _ssm_scan_fwd(proj, ssm["wb"], ssm["tab"], ssm["wc"])
            ys = _ssm_glu_fwd(y, proj, dsk, wg[l]["ssm_w_glu"], bgl)
            o, ya, _ = _attn_fwd(qkv, proj)
        tail = (target,) if l == N_LAYERS - 1 else ()
        h1, h2, *sq = _out_ple(h, ys, ya, p[l], g2, wg[l]["w_out"], wg[l]["w_ple_gate"], wg[l]["w_ple_proj"], *tail)
        saved.append(dict(ssm=ssm, g1=g1, g2=g2, qg=qg, kg=kg, dsk=dsk, bgl=bgl, h=h, proj=proj, qkv=qkv, xs=xs, y=y,
                          ys=ys, o=o, ya=ya, h1=h1))
        h = h2
    dh = h
    loss = 0.5 * jnp.sum(sq[0]) / D_MODEL

    gbig = [{} for _ in range(N_LAYERS)]
    scattered = ([], [])
    gsm = {n: [None] * N_LAYERS for n in SMALL_NAMES}
    for l in reversed(range(N_LAYERS)):
        sv = saved[l]
        ssm = sv["ssm"]
        dh1, dmix, hn2b, dgpb, dppb, dh1b, dg2 = _out_ple_bwd(dh, sv["h1"], p[l], sv["g2"], wg[l]["w_out"],
                                                              wg[l]["w_ple_gate"], wg[l]["w_ple_proj"])
        gsm["ple_norm_g"][l] = dg2.sum(0)
        gbig[l]["w_ple_proj"] = _tn_matmul(p[l], dppb, N_CHIPS, False, "dw_ple_proj")
        gbig[l]["w_ple_gate"] = _tn_matmul(hn2b, dgpb, N_CHIPS, True, "dw_ple_gate")
        dwo = _tn_matmul(sv["ys"], dh1b, 2, True, "dw_out_ssm", None, 0, N_CHIPS)
        gbig[l]["w_out"] = _tn_matmul(sv["ya"], dh1b, 2, True, "dw_out_attn", dwo, 2, N_CHIPS)
        if l == 0 and layer1_hook is not None:
            chip1 = layer1_hook(gbig[1])
            dqs, dkn, dv, dga, got = _attn_bwd(sv["qkv"], sv["o"], sv["proj"], dmix, chip1)
            scattered = (chip1, got)
        else:
            dqs, dkn, dv, dga, _ = _attn_bwd(sv["qkv"], sv["o"], sv["proj"], dmix)
        dyf, du_a, dgs, zb, dzzb, dd, dbg = _ssm_glu_bwd(dmix, sv["y"], sv["proj"], sv["dsk"], wg[l]["ssm_w_glu"], sv["bgl"])
        gsm["ssm_d"][l] = dd.sum(0).reshape(SSM_GROUPS, SSM_GROUP)
        gsm["ssm_b_glu"][l] = dbg.sum(0)
        gbig[l]["ssm_w_glu"] = _tn_matmul(zb, dzzb, N_CHIPS, False, "dw_glu")
        du_b, dwc, dwb, da = _ssm_scan_bwd(dyf, sv["xs"], sv["proj"], ssm["wct"], ssm["tab_rev"], ssm["wbt"])
        d_are, d_aim, d_ldt, d_bre, d_bim, d_cre, d_cim = _disc_bwd(*ssm["disc_in"], da, dwb, dwc)
        by_group = lambda t: t.reshape(SSM_GROUP, SSM_GROUPS, SSM_STATE).transpose(1, 0, 2)
        gsm["ssm_c_re"][l] = by_group(d_cre)
        gsm["ssm_c_im"][l] = by_group(d_cim)
        gsm["ssm_a_re"][l] = d_are.reshape(SSM_GROUPS, SSM_STATE)
        gsm["ssm_a_im"][l] = d_aim.reshape(SSM_GROUPS, SSM_STATE)
        gsm["ssm_log_dt"][l] = d_ldt.reshape(SSM_GROUPS, SSM_STATE).sum(1)
        gsm["ssm_b_re"][l] = d_bre.T.reshape(SSM_GROUPS, SSM_STATE, SSM_GROUP)
        gsm["ssm_b_im"][l] = d_bim.T.reshape(SSM_GROUPS, SSM_STATE, SSM_GROUP)
        dh, hnb, dprojb, dg1, dqg, dkg = _in_proj_bwd(sv["h"], sv["g1"], wg[l]["w_in"], sv["qg"], sv["kg"], sv["proj"],
                                                      du_a, du_b, dgs, dqs, dkn, dv, dga, dh1)
        gsm["mix_norm_g"][l] = dg1.sum(0)
        gsm["q_norm_g"][l] = dqg.sum(0).reshape(-1, HEAD_DIM).sum(0)
        gsm["k_norm_g"][l] = dkg.sum(0).reshape(-1, HEAD_DIM).sum(0)
        gbig[l]["w_in"] = _tn_matmul(hnb, dprojb, N_CHIPS, False, "dw_in")
    gsm = {n: jnp.stack(v, 0) for n, v in gsm.items()}
    return loss, dh, gbig, gsm, scattered


_SMALL_PAD = 8 * 8 * 128


def _pack_small(d, extra):
    flat = jnp.concatenate([d[n].reshape(-1) for n in SMALL_NAMES] + [jnp.stack(extra)])
    n = flat.shape[0]
    padded = -(-n // _SMALL_PAD) * _SMALL_PAD
    return jnp.pad(flat, (0, padded - n))


def _unpack_small(flat, like):
    out, off = {}, 0
    for n in SMALL_NAMES:
        size = like[n].size
        out[n] = flat[off:off + size].reshape(like[n].shape)
        off += size
    return out, flat[off:]


def _half_views(arrs):
    return [a.reshape(a.shape[0], 2, a.shape[1] // 2, a.shape[2]) for a in arrs]


def _chip_sums(views, out_dtypes, tag):
    recv = _sibling_push(views, "grad_push_" + tag)
    return [_add_my_half(v, r, dt, "grad_half_add") for v, r, dt in zip(views, recv, out_dtypes)]


def kernel(x, p, mix_norm_g, w_in, ssm_a_re, ssm_a_im, ssm_log_dt, ssm_b_re, ssm_b_im, ssm_c_re, ssm_c_im, ssm_d, ssm_w_glu, ssm_b_glu, q_norm_g, k_norm_g, w_out, ple_norm_g, w_ple_gate, w_ple_proj, loss_target, m_mix_norm_g, m_w_in, m_ssm_a_re, m_ssm_a_im, m_ssm_log_dt, m_ssm_b_re, m_ssm_b_im, m_ssm_c_re, m_ssm_c_im, m_ssm_d, m_ssm_w_glu, m_ssm_b_glu, m_q_norm_g, m_k_norm_g, m_w_out, m_ple_norm_g, m_w_ple_gate, m_w_ple_proj, v_mix_norm_g, v_w_in, v_ssm_a_re, v_ssm_a_im, v_ssm_log_dt, v_ssm_b_re, v_ssm_b_im, v_ssm_c_re, v_ssm_c_im, v_ssm_d, v_ssm_w_glu, v_ssm_b_glu, v_q_norm_g, v_k_norm_g, v_w_out, v_ple_norm_g, v_w_ple_gate, v_w_ple_proj):
    args = dict(locals())
    names = ("mix_norm_g", "w_in", "ssm_a_re", "ssm_a_im", "ssm_log_dt", "ssm_b_re", "ssm_b_im", "ssm_c_re", "ssm_c_im",
             "ssm_d", "ssm_w_glu", "ssm_b_glu", "q_norm_g", "k_norm_g", "w_out", "ple_norm_g", "w_ple_gate", "w_ple_proj")
    w = {n: args[n] for n in names}
    m = {n: args["m_" + n] for n in names}
    v = {n: args["v_" + n] for n in names}

    local = {n: w[n].astype(BF16).reshape(2 * N_LAYERS, w[n].shape[1] // 2, w[n].shape[2]) for n in BIG_NAMES}
    w_in0 = _chip_gather([local["w_in"]], "w_in_gather")[0].reshape(N_CHIPS, D_MODEL, IN_SHARD)
    sm = {n: w[n] for n in SMALL_NAMES}
    nb = len(BIG_NAMES)
    loss, dx, gbig, gsm, (chip1, got1) = _local_step(
        x[0], p[:, 0], loss_target[0], sm, w_in0, local,
        layer1_hook=lambda g1: _chip_sums(_half_views([g1[n] for n in BIG_NAMES]), [BF16] * nb, "layer1"))

    small = _pack_small(gsm, [loss]).reshape(N_CHIPS, 2, SUBLANES, -1)
    chip0 = _chip_sums(_half_views([gbig[0][n] for n in BIG_NAMES]) + [small], [BF16] * nb + [F32], "layer0")
    got0 = _chip_scatter(chip0, "grad_chip_scatter")
    tot1 = [_sum4(a, own, "grad_chip_sum") for a, own in zip(got1, chip1)]
    tot0 = [_sum4(a, own, "grad_chip_sum") for a, own in zip(got0, chip0)]
    pieces = [(t, k, (l,)) for l, tots in enumerate((tot0[:nb], tot1)) for k, t in enumerate(tots)] + [(tot0[nb], nb, ())]
    joined = _sibling_join(pieces, [(N_LAYERS, 2) + t.shape for t in tot1] + [(2,) + tot0[nb].shape], "grad_sibling_join")
    small_all = _chip_gather([joined[nb]], "small_grad_gather")[0]
    small_tot = small_all.reshape(-1)
    g = {n: j.reshape(w[n].shape) for n, j in zip(BIG_NAMES, joined)}
    g_small, rest = _unpack_small(small_tot, sm)
    g.update(g_small)
    loss = rest[0]

    delta, new_m, new_v = {}, {}, {}
    for n in BIG_NAMES:
        lanes = w[n].shape[-1]
        outs = _adamw(_as_rows(w[n], lanes), _as_rows(g[n], lanes), _as_rows(m[n], lanes), _as_rows(v[n], lanes), "adamw_" + n)
        delta[n], new_m[n], new_v[n] = [o.reshape(w[n].shape) for o in outs]
    swap = lambda n, a: jnp.swapaxes(a, -1, -2) if n in ("ssm_b_re", "ssm_b_im") else a
    for group, per_layer in ((SMALL_4D, True), (tuple(n for n in SMALL_NAMES if n not in SMALL_4D), False)):
        outs = _adamw_many(*[[swap(n, d[n]) for n in group] for d in (w, g, m, v)],
                           "adamw_small_4d" if per_layer else "adamw_small", per_layer)
        for d, o in zip((delta, new_m, new_v), outs):
            d.update({n: swap(n, a) for n, a in zip(group, o)})

    return (loss, dx[None], *[g[n] for n in names], *[delta[n] for n in names],
            *[new_m[n] for n in names], *[new_v[n] for n in names])
```

```python
import functools
import math

import jax
import jax.numpy as jnp
from jax import lax
from jax.experimental import pallas as pl
from jax.experimental.pallas import tpu as pltpu

F32 = jnp.float32
BF16 = jnp.bfloat16

D_MODEL = 1024
N_LAYERS = 2
N_CHIPS = 4
IN_COLS = 3072
IN_SHARD = IN_COLS // N_CHIPS
SSM_WIDTH = 512
SSM_GROUP = 16
SSM_GROUPS = 32
SSM_STATE = 64
N_STATES = SSM_GROUPS * SSM_STATE
SSM_CHUNKS = 4
CH_W = SSM_WIDTH // SSM_CHUNKS
CH_S = N_STATES // SSM_CHUNKS
ATTN_WIDTH = 512
HEAD_DIM = 64
PLE_DIM = 256
ROW_SHARD = 256
RMS_EPS = 1e-6
ATTN_SCALE = HEAD_DIM ** -0.5
ATTN_BLOCK = 128
EXP_ZERO = -87.5
SUBLANES = 8
SCAN_TILE = 1024
V7X_VMEM_LIMIT = 52 * 1024 * 1024
V7X_VMEM_LIMIT_ATTN_BWD = 60 * 1024 * 1024

ADAM_LR = 0.001
ADAM_B1 = 0.9
ADAM_B2 = 0.999
ADAM_EPS = 1e-08
ADAM_WD = 0.01
ADAM_STEP = 10

MESH = pl.DeviceIdType.MESH
ANY = pl.BlockSpec(memory_space=pl.ANY)


def _cparams(n_grid=0, parallel=0, vmem_limit=V7X_VMEM_LIMIT):
    sem = tuple(["parallel"] * parallel + ["arbitrary"] * (n_grid - parallel))
    return pltpu.CompilerParams(dimension_semantics=sem, vmem_limit_bytes=vmem_limit)


def _dot(a, b):
    return jnp.dot(a, b, preferred_element_type=F32)


def _dot_nt(a, b):
    return lax.dot_general(a, b, (((1,), (1,)), ((), ())), preferred_element_type=F32)


def _dot_tn(a, b):
    return lax.dot_general(a, b, (((0,), (0,)), ((), ())), preferred_element_type=F32)


def _split_hilo(a):
    hi = a.astype(BF16)
    lo = (a - hi.astype(F32)).astype(BF16)
    return hi, lo


def _dot_hilo(a, b):
    hi, lo = _split_hilo(a)
    return _dot(hi, b) + _dot(lo, b)


def _sigmoid(x):
    return 0.5 * (jnp.tanh(0.5 * x) + 1.0)


_GELU_C = math.sqrt(2.0 / math.pi)


def _gelu(x):
    return 0.5 * x * (1.0 + jnp.tanh(_GELU_C * (x + 0.044715 * (x * x * x))))


def _gelu_grad(x):
    t = jnp.tanh(_GELU_C * (x + 0.044715 * (x * x * x)))
    return 0.5 * (1.0 + t) + 0.5 * x * (1.0 - t * t) * (_GELU_C * (1.0 + 3.0 * 0.044715 * (x * x)))


def _row_tile(s, want):
    for t in range(min(s, want), 7, -1):
        if s % t == 0 and t % SUBLANES == 0:
            return t
    return s


def _coords():
    return lax.axis_index("x"), lax.axis_index("y"), lax.axis_index("c")


def _other_chips(x, y):
    return [(1 - x, y), (x, 1 - y), (1 - x, 1 - y)]


def _remote(src, dst, send_sem, recv_sem, dev):
    return pltpu.make_async_remote_copy(src_ref=src, dst_ref=dst, send_sem=send_sem, recv_sem=recv_sem,
                                        device_id=dev, device_id_type=MESH)


def _set_block(buf, block, index):
    return lax.dynamic_update_index_in_dim(buf, block, index, 0)


def _gather_sems(n):
    return [pltpu.SemaphoreType.DMA((3 * n,)) for _ in range(4)]


def _gather_copies(ins, bases, outs, sems):
    send_sems, recv_sems, fwd_send, fwd_recv = sems
    x, y, c = _coords()
    me_chip = 2 * x + y
    sibling = (x, y, 1 - c)
    first, landed, passed, from_sibling = [], [], [], []
    for k in range(len(ins)):
        for j, (cx, cy) in enumerate(_other_chips(x, y)):
            i = 3 * k + j
            first.append(_remote(ins[k].at[bases[k] + c], outs[k].at[me_chip, c], send_sems.at[i], recv_sems.at[i], (cx, cy, c)))
            blk = outs[k].at[2 * cx + cy, c]
            landed.append(_remote(blk, blk, send_sems.at[i], recv_sems.at[i], (cx, cy, c)))
            passed.append(_remote(blk, blk, fwd_send.at[i], fwd_recv.at[i], sibling))
            blk = outs[k].at[2 * cx + cy, 1 - c]
            from_sibling.append(_remote(blk, blk, fwd_send.at[i], fwd_recv.at[i], sibling))
    return first, landed, passed, from_sibling


def _gather_start(ins, bases, outs, sems):
    for cp in _gather_copies(ins, bases, outs, sems)[0]:
        cp.start()


def _gather_finish(ins, bases, outs, sems):
    first, landed, passed, from_sibling = _gather_copies(ins, bases, outs, sems)
    for arrived, forward in zip(landed, passed):
        arrived.wait_recv()
        forward.start()
    for cp in from_sibling:
        cp.wait_recv()
    for cp in first + passed:
        cp.wait_send()


def _gather_outputs(arrs):
    return [jax.ShapeDtypeStruct((N_CHIPS, 2) + a.shape[1:], a.dtype) for a in arrs]


def _gather_own(outs, arrs, bases):
    me_chip = 2 * lax.axis_index("x") + lax.axis_index("y")
    return [_set_block(o, lax.slice_in_dim(a, b, b + 2, axis=0), me_chip) for o, a, b in zip(outs, arrs, bases)]


def _chip_gather(arrs, name, bases=None):
    n = len(arrs)
    bases = [0] * n if bases is None else bases

    def body(*refs):
        ins, outs, sems = refs[:n], refs[n:2 * n], refs[2 * n:]
        _gather_start(ins, bases, outs, sems)
        _gather_finish(ins, bases, outs, sems)

    outs = pl.pallas_call(
        body, name=name, out_shape=_gather_outputs(arrs),
        in_specs=[ANY] * n, out_specs=[ANY] * n, scratch_shapes=_gather_sems(n),
    )(*arrs)
    return _gather_own(outs, arrs, bases)


def _sibling_push(arrs, name):
    n = len(arrs)

    def body(*refs):
        ins, outs = refs[:n], refs[n:2 * n]
        send_sems, recv_sems = refs[2 * n:]
        x, y, c = _coords()
        cps = [_remote(ins[k].at[pl.ds(0, N_CHIPS), 1 - c], outs[k], send_sems.at[k], recv_sems.at[k], (x, y, 1 - c))
               for k in range(n)]
        for cp in cps:
            cp.start()
        for cp in cps:
            cp.wait_recv()
        for cp in cps:
            cp.wait_send()

    return pl.pallas_call(
        body, name=name,
        out_shape=[jax.ShapeDtypeStruct((a.shape[0],) + a.shape[2:], a.dtype) for a in arrs],
        in_specs=[ANY] * n, out_specs=[ANY] * n,
        scratch_shapes=[pltpu.SemaphoreType.DMA((n,)), pltpu.SemaphoreType.DMA((n,))],
    )(*arrs)


def _sibling_join(pieces, out_shapes, name):
    n = len(pieces)
    no = len(out_shapes)

    def body(*refs):
        ins, outs = refs[:n], refs[n:n + no]
        send_sems, recv_sems = refs[n + no:]
        x, y, c = _coords()
        sibling = (x, y, 1 - c)
        cps = [_remote(ins[k], outs[o].at[lead + (c,)], send_sems.at[k], recv_sems.at[k], sibling)
               for k, (_, o, lead) in enumerate(pieces)]
        for cp in cps:
            cp.start()
        for k, (_, o, lead) in enumerate(pieces):
            blk = outs[o].at[lead + (1 - c,)]
            _remote(blk, blk, send_sems.at[k], recv_sems.at[k], sibling).wait_recv()
        for cp in cps:
            cp.wait_send()

    outs = pl.pallas_call(
        body, name=name,
        out_shape=[jax.ShapeDtypeStruct(sh, F32) for sh in out_shapes],
        in_specs=[ANY] * n, out_specs=[ANY] * no,
        scratch_shapes=[pltpu.SemaphoreType.DMA((n,)), pltpu.SemaphoreType.DMA((n,))],
    )(*[a for a, _, _ in pieces])
    outs = list(outs)
    c = lax.axis_index("c")
    for a, o, lead in pieces:
        block = a.reshape((1,) * (len(lead) + 1) + a.shape)
        outs[o] = lax.dynamic_update_slice(outs[o], block, lead + (c,) + (0,) * a.ndim)
    return outs


def _scatter_sems(n):
    return [pltpu.SemaphoreType.DMA((3 * n,)), pltpu.SemaphoreType.DMA((3 * n,))]


def _scatter_copies(ins, outs, sems):
    send_sems, recv_sems = sems
    x, y, c = _coords()
    me_chip = 2 * x + y
    sends, arrivals = [], []
    for k in range(len(ins)):
        for j, (cx, cy) in enumerate(_other_chips(x, y)):
            i = 3 * k + j
            sends.append(_remote(ins[k].at[2 * cx + cy], outs[k].at[me_chip], send_sems.at[i], recv_sems.at[i], (cx, cy, c)))
            blk = outs[k].at[2 * cx + cy]
            arrivals.append(_remote(blk, blk, send_sems.at[i], recv_sems.at[i], (cx, cy, c)))
    return sends, arrivals


def _scatter_start(ins, outs, sems):
    for cp in _scatter_copies(ins, outs, sems)[0]:
        cp.start()


def _scatter_finish(ins, outs, sems):
    sends, arrivals = _scatter_copies(ins, outs, sems)
    for cp in arrivals:
        cp.wait_recv()
    for cp in sends:
        cp.wait_send()


def _chip_scatter(arrs, name):
    n = len(arrs)

    def body(*refs):
        ins, outs, sems = refs[:n], refs[n:2 * n], refs[2 * n:]
        _scatter_start(ins, outs, sems)
        _scatter_finish(ins, outs, sems)

    outs = pl.pallas_call(
        body, name=name,
        out_shape=[jax.ShapeDtypeStruct(a.shape, a.dtype) for a in arrs],
        in_specs=[ANY] * n, out_specs=[ANY] * n, scratch_shapes=_scatter_sems(n),
    )(*arrs)
    return outs


def _as_rows(a, lanes):
    return a.reshape(-1, lanes)


def _add_my_half(v, recv, out_dtype, name):
    n_sh, _, h, cdim = v.shape
    tr = _row_tile(h, 512)

    def body(c_ref, a_ref, b_ref, o_ref):
        o_ref[...] = (a_ref[...].astype(F32) + b_ref[...].astype(F32)).astype(out_dtype)

    c = lax.axis_index("c").astype(jnp.int32).reshape(1)
    return pl.pallas_call(
        body, name=name,
        grid_spec=pltpu.PrefetchScalarGridSpec(
            num_scalar_prefetch=1, grid=(n_sh, h // tr),
            in_specs=[pl.BlockSpec((None, None, tr, cdim), lambda sh, i, c_ref: (sh, c_ref[0], i, 0)),
                      pl.BlockSpec((None, tr, cdim), lambda sh, i, c_ref: (sh, i, 0))],
            out_specs=pl.BlockSpec((None, tr, cdim), lambda sh, i, c_ref: (sh, i, 0))),
        out_shape=jax.ShapeDtypeStruct((n_sh, h, cdim), out_dtype),
        compiler_params=_cparams(2),
    )(c, v, recv)


def _sum4(got, own, name):
    _, r, cdim = got.shape
    tr = _row_tile(r, 512)

    def body(me_ref, p_ref, own_ref, o_ref):
        mine = own_ref[...].astype(F32)
        acc = None
        for j in range(N_CHIPS):
            term = jnp.where(me_ref[0] == j, mine, p_ref[j].astype(F32))
            acc = term if acc is None else acc + term
        o_ref[...] = acc

    me = (2 * lax.axis_index("x") + lax.axis_index("y")).astype(jnp.int32).reshape(1)
    return pl.pallas_call(
        body, name=name,
        grid_spec=pltpu.PrefetchScalarGridSpec(
            num_scalar_prefetch=1, grid=(r // tr,),
            in_specs=[pl.BlockSpec((N_CHIPS, tr, cdim), lambda i, me_ref: (0, i, 0)),
                      pl.BlockSpec((None, tr, cdim), lambda i, me_ref: (me_ref[0], i, 0))],
            out_specs=pl.BlockSpec((tr, cdim), lambda i, me_ref: (i, 0))),
        out_shape=jax.ShapeDtypeStruct((r, cdim), F32),
        compiler_params=_cparams(1),
    )(me, got, own)


def _adamw_math(w, g, m, v):
    c1 = 1.0 - ADAM_B1 ** ADAM_STEP
    c2 = 1.0 - ADAM_B2 ** ADAM_STEP
    nm = ADAM_B1 * m + (1.0 - ADAM_B1) * g
    nv = ADAM_B2 * v + (1.0 - ADAM_B2) * (g * g)
    delta = -ADAM_LR * ((nm / c1) / (jnp.sqrt(nv / c2) + ADAM_EPS) + ADAM_WD * w)
    return delta, nm, nv


def _adamw(w, g, m, v, name):
    r, cdim = w.shape
    tr = _row_tile(r, 256)

    def body(w_ref, g_ref, m_ref, v_ref, d_ref, nm_ref, nv_ref):
        d_ref[...], nm_ref[...], nv_ref[...] = _adamw_math(w_ref[...], g_ref[...], m_ref[...], v_ref[...])

    spec = pl.BlockSpec((tr, cdim), lambda i: (i, 0))
    return pl.pallas_call(
        body, name=name, grid=(r // tr,),
        in_specs=[spec] * 4, out_specs=[spec] * 3,
        out_shape=[jax.ShapeDtypeStruct((r, cdim), F32)] * 3,
        compiler_params=_cparams(1),
    )(w, g, m, v)


def _adamw_many(ws, gs, ms, vs, name, per_layer):
    n = len(ws)

    def body(*refs):
        for k in range(n):
            w, g, m, v = (refs[j * n + k][...] for j in range(4))
            outs = _adamw_math(w, g, m, v)
            for j in range(3):
                refs[(4 + j) * n + k][...] = outs[j]

    shapes = [jax.ShapeDtypeStruct(w.shape, F32) for w in ws]
    if per_layer:
        specs = [pl.BlockSpec((None,) + w.shape[1:], lambda l, nd=w.ndim: (l,) + (0,) * (nd - 1)) for w in ws]
        call = pl.pallas_call(body, name=name, grid=(N_LAYERS,), in_specs=specs * 4, out_specs=specs * 3,
                              out_shape=shapes * 3, compiler_params=_cparams(1))
    else:
        call = pl.pallas_call(body, name=name, out_shape=shapes * 3, compiler_params=_cparams())
    outs = call(*ws, *gs, *ms, *vs)
    return outs[0:n], outs[n:2 * n], outs[2 * n:3 * n]


def _cmul(ar, ai, br, bi):
    return ar * br - ai * bi, ar * bi + ai * br


def _discretise(a_re, a_im, log_dt, b_re, b_im):
    dt = jnp.exp(log_dt)
    mag = jnp.exp(a_re * dt)
    ab_re = mag * jnp.cos(a_im * dt)
    ab_im = mag * jnp.sin(a_im * dt)
    num_re = ab_re - 1.0
    num_im = ab_im
    den = a_re * a_re + a_im * a_im
    f_re = (num_re * a_re + num_im * a_im) / den
    f_im = (num_im * a_re - num_re * a_im) / den
    bb_re = f_re * b_re - f_im * b_im
    bb_im = f_re * b_im + f_im * b_re
    return ab_re, ab_im, bb_re, bb_im


def _disc_shapes():
    col = jax.ShapeDtypeStruct((1, N_STATES), F32)
    mat = jax.ShapeDtypeStruct((SSM_GROUP, N_STATES), F32)
    return col, mat


def _group_mask():
    row = lax.broadcasted_iota(jnp.int32, (CH_W, CH_S), 0)
    col = lax.broadcasted_iota(jnp.int32, (CH_W, CH_S), 1)
    return jnp.right_shift(row, SSM_GROUP.bit_length() - 1) == jnp.right_shift(col, SSM_STATE.bit_length() - 1)


def _block_diag(v, j):
    blk = v[:, CH_S * j:CH_S * (j + 1)]
    return jnp.where(_group_mask(), jnp.concatenate([blk] * (CH_W // SSM_GROUP), axis=0), 0.0)


def _block_diag_t(m):
    kept = jnp.where(_group_mask(), m, 0.0)
    return kept.reshape(CH_W // SSM_GROUP, SSM_GROUP, CH_S).sum(axis=0)


def _disc_fwd(a_re, a_im, log_dt, b_re, b_im, c_re, c_im, length):
    wide = jax.ShapeDtypeStruct((SSM_CHUNKS, CH_W, 2 * CH_S), BF16)
    tall = jax.ShapeDtypeStruct((SSM_CHUNKS, 2 * CH_S, CH_W), BF16)
    tab = jax.ShapeDtypeStruct((SSM_CHUNKS, length, 2 * CH_S), F32)

    def body(ar, ai, ld, br, bi, cr, ci, wb_ref, wbt_ref, wct_ref, wc_ref, tab_ref, rev_ref):
        ab_re, ab_im, bb_re, bb_im = _discretise(ar[...], ai[...], ld[...], br[...], bi[...])
        ccr, cci = cr[...], -ci[...]
        for j in range(SSM_CHUNKS):
            for lo, (vb, vc) in ((0, (bb_re, ccr)), (CH_S, (bb_im, cci))):
                mb, mc = _block_diag(vb, j), _block_diag(vc, j)
                wb_ref[j, :, lo:lo + CH_S] = mb.astype(BF16)
                wbt_ref[j, lo:lo + CH_S, :] = mb.T.astype(BF16)
                wct_ref[j, :, lo:lo + CH_S] = mc.astype(BF16)
                wc_ref[j, lo:lo + CH_S, :] = mc.T.astype(BF16)

        def step(j, carry):
            pr, pi = carry
            back = length - 1 - j
            for c in range(SSM_CHUNKS):
                lanes = slice(CH_S * c, CH_S * (c + 1))
                tab_ref[c, pl.ds(j, 1), 0:CH_S] = pr[:, lanes]
                tab_ref[c, pl.ds(j, 1), CH_S:2 * CH_S] = pi[:, lanes]
                rev_ref[c, pl.ds(back, 1), 0:CH_S] = pr[:, lanes]
                rev_ref[c, pl.ds(back, 1), CH_S:2 * CH_S] = -pi[:, lanes]
            return _cmul(pr, pi, ab_re, ab_im)

        lax.fori_loop(0, length, step, (ab_re, ab_im))

    return pl.pallas_call(body, name="ssm_discretise", out_shape=[wide, tall, wide, tall, tab, tab],
                          compiler_params=_cparams())(a_re, a_im, log_dt, b_re, b_im, c_re, c_im)


def _disc_bwd(a_re, a_im, log_dt, b_re, b_im, da, dwb, dwc):
    col, mat = _disc_shapes()

    def body(ar, ai, ld, br, bi, da_ref, dwb_ref, dwc_ref, o0, o1, o2, o3, o4, dcr_ref, dci_ref):
        g_ab = [jnp.concatenate([jnp.sum(da_ref[j, :, lo:lo + CH_S], axis=0, keepdims=True) for j in range(SSM_CHUNKS)],
                                axis=-1) for lo in (0, CH_S)]
        g_bb = [jnp.concatenate([_block_diag_t(dwb_ref[j, :, lo:lo + CH_S]) for j in range(SSM_CHUNKS)], axis=-1)
                for lo in (0, CH_S)]
        for ref, lo, sign in ((dcr_ref, 0, 1.0), (dci_ref, CH_S, -1.0)):
            ref[...] = sign * jnp.concatenate([_block_diag_t(dwc_ref[j, lo:lo + CH_S, :].T) for j in range(SSM_CHUNKS)],
                                              axis=-1)
        _, vjp = jax.vjp(_discretise, ar[...], ai[...], ld[...], br[...], bi[...])
        grads = vjp((g_ab[0], g_ab[1], g_bb[0], g_bb[1]))
        for o, val in zip((o0, o1, o2, o3, o4), grads):
            o[...] = val

    return pl.pallas_call(body, name="ssm_discretise_bwd", out_shape=[col, col, col, mat, mat, mat, mat],
                          compiler_params=_cparams())(a_re, a_im, log_dt, b_re, b_im, da, dwb, dwc)


def _interleave_chunks(v):
    rows, width = v.shape
    return pltpu.einshape("cjw->jcw", v.reshape(SUBLANES, rows // SUBLANES, width)).reshape(rows, width)


def _time_order(v):
    rows, width = v.shape
    return pltpu.einshape("jcw->cjw", v.reshape(rows // SUBLANES, SUBLANES, width)).reshape(rows, width)


def _head_ones():
    r = jnp.arange(ATTN_WIDTH) // HEAD_DIM
    return jnp.where(r[:, None] == r[None, :], 1.0 / HEAD_DIM, 0.0).astype(BF16)


def _in_proj(h, g1, w_in_l, qg, kg):
    s = h.shape[0]
    tm = _row_tile(s, 512)

    def body(h_ref, g_ref, w_ref, qg_ref, kg_ref, ones_ref, proj_ref, qkv_ref):
        x = h_ref[...]
        r = lax.rsqrt(jnp.mean(x * x, axis=-1, keepdims=True) + RMS_EPS)
        hn = (x * r * g_ref[...]).astype(BF16)
        for sh in range(N_CHIPS):
            proj_ref[:, IN_SHARD * sh:IN_SHARD * (sh + 1)] = _dot(hn, w_ref[sh])
        ones = ones_ref[...]
        q = proj_ref[:, 1024:1536]
        k = proj_ref[:, 1536:2048]
        rq = lax.rsqrt(_dot_hilo(q * q, ones) + RMS_EPS)
        rk = lax.rsqrt(_dot_hilo(k * k, ones) + RMS_EPS)
        qkv_ref[:, 0:512] = (q * rq * qg_ref[...] * ATTN_SCALE).astype(BF16)
        qkv_ref[:, 512:1024] = (k * rk * kg_ref[...]).astype(BF16)
        qkv_ref[:, 1024:1536] = proj_ref[:, 2048:2560].astype(BF16)

    full = lambda shape: pl.BlockSpec(shape, lambda i: (0,) * len(shape))
    return pl.pallas_call(
        body, name="in_proj", grid=(s // tm,),
        in_specs=[pl.BlockSpec((tm, D_MODEL), lambda i: (i, 0)), full((1, D_MODEL)),
                  full((N_CHIPS, D_MODEL, IN_SHARD)),
                  full((1, ATTN_WIDTH)), full((1, ATTN_WIDTH)), full((ATTN_WIDTH, ATTN_WIDTH))],
        out_specs=[pl.BlockSpec((tm, IN_COLS), lambda i: (i, 0)), pl.BlockSpec((tm, 3 * ATTN_WIDTH), lambda i: (i, 0))],
        out_shape=[jax.ShapeDtypeStruct((s, IN_COLS), F32), jax.ShapeDtypeStruct((s, 3 * ATTN_WIDTH), BF16)],
        compiler_params=_cparams(1),
    )(h, g1, w_in_l, qg, kg, _head_ones())


def _row_bcast(ref, k, lo):
    return jnp.broadcast_to(ref[pl.ds(k, 1), lo:lo + CH_S], (SUBLANES, CH_S))


def _chunk_scan(x_ref, tab_ref, carry_ref, length, reverse, tail=None):
    row = lax.broadcasted_iota(jnp.int32, (SUBLANES, CH_S), 0)
    one, full = (length - 1, 0) if reverse else (0, length - 1)
    ar, ai = _row_bcast(tab_ref, one, 0), _row_bcast(tab_ref, one, CH_S)
    fr, fi = _row_bcast(tab_ref, full, 0), _row_bcast(tab_ref, full, CH_S)
    step = lambda jj: (length - 1 - jj) if reverse else jj

    def local(jj, carry):
        cr, ci = carry
        r0 = pl.multiple_of(step(jj) * SUBLANES, SUBLANES)
        xr = x_ref[pl.ds(r0, SUBLANES), 0:CH_S] + (ar * cr - ai * ci)
        xi = x_ref[pl.ds(r0, SUBLANES), CH_S:2 * CH_S] + (ar * ci + ai * cr)
        x_ref[pl.ds(r0, SUBLANES), 0:CH_S] = xr
        x_ref[pl.ds(r0, SUBLANES), CH_S:2 * CH_S] = xi
        return xr, xi

    zero = jnp.zeros((SUBLANES, CH_S), F32)
    er, ei = lax.fori_loop(0, length, local, (zero, zero))

    first, shift = (SUBLANES - 1, SUBLANES - 1) if reverse else (0, 1)
    hr = jnp.where(row == first, carry_ref[:, 0:CH_S], 0.0)
    hi = jnp.where(row == first, carry_ref[:, CH_S:2 * CH_S], 0.0)
    sr, si = pltpu.roll(er, shift, 0), pltpu.roll(ei, shift, 0)
    for k in range(1, SUBLANES):
        tr, ti = pltpu.roll(hr, shift, 0), pltpu.roll(hi, shift, 0)
        here = row == ((SUBLANES - 1 - k) if reverse else k)
        hr, hi = (jnp.where(here, fr * tr - fi * ti + sr, hr), jnp.where(here, fr * ti + fi * tr + si, hi))
    last = 0 if reverse else SUBLANES - 1
    outr, outi = fr * hr - fi * hi + er, fr * hi + fi * hr + ei
    carry_ref[:, 0:CH_S] = jnp.broadcast_to(outr[last:last + 1, :], (SUBLANES, CH_S))
    carry_ref[:, CH_S:2 * CH_S] = jnp.broadcast_to(outi[last:last + 1, :], (SUBLANES, CH_S))

    def fix(jj, carry):
        j = step(jj)
        r0 = pl.multiple_of(j * SUBLANES, SUBLANES)
        pr, pi = _row_bcast(tab_ref, j, 0), _row_bcast(tab_ref, j, CH_S)
        xr = x_ref[pl.ds(r0, SUBLANES), 0:CH_S] + (pr * hr - pi * hi)
        xi = x_ref[pl.ds(r0, SUBLANES), CH_S:2 * CH_S] + (pr * hi + pi * hr)
        x_ref[pl.ds(r0, SUBLANES), 0:CH_S] = xr
        x_ref[pl.ds(r0, SUBLANES), CH_S:2 * CH_S] = xi
        if tail is None:
            return carry
        return tail(r0, xr, xi, carry)

    return fix, (hr, hi)


def _ssm_scan_fwd(proj, wb, tab, wc, gather=None, gather_bases=None):
    s = proj.shape[0]
    tm = _row_tile(s, SCAN_TILE)
    nt = s // tm
    length = tm // SUBLANES
    gather = [] if gather is None else gather
    ng = len(gather)

    def body(*refs):
        u_ref, wb_ref, tab_ref, wc_ref = refs[0:4]
        g_ins = refs[4:4 + ng]
        xs_ref, y_ref = refs[4 + ng:6 + ng]
        g_outs = refs[6 + ng:6 + 2 * ng]
        carry_ref = refs[6 + 2 * ng]
        sems = refs[7 + 2 * ng:]
        j, i = pl.program_id(0), pl.program_id(1)

        @pl.when(i == 0)
        def _():
            carry_ref[...] = jnp.zeros_like(carry_ref)

        if ng:
            @pl.when(jnp.logical_and(j == 0, i == 0))
            def _():
                _gather_start(g_ins, gather_bases, g_outs, sems)

        xs_ref[...] = _dot(_interleave_chunks(u_ref[...]).astype(BF16), wb_ref[...])
        fix, start = _chunk_scan(xs_ref, tab_ref, carry_ref, length, reverse=False)
        lax.fori_loop(0, length, fix, start, unroll=2)
        y_ref[...] = _time_order(_dot(xs_ref[...].astype(BF16), wc_ref[...]))

        if ng:
            @pl.when(jnp.logical_and(j == SSM_CHUNKS - 1, i == nt - 1))
            def _():
                _gather_finish(g_ins, gather_bases, g_outs, sems)

    outs = pl.pallas_call(
        body, name="ssm_scan_gather" if ng else "ssm_scan", grid=(SSM_CHUNKS, nt),
        in_specs=[pl.BlockSpec((tm, CH_W), lambda j, i: (i, j)),
                  pl.BlockSpec((None, CH_W, 2 * CH_S), lambda j, i: (j, 0, 0)),
                  pl.BlockSpec((None, length, 2 * CH_S), lambda j, i: (j, 0, 0)),
                  pl.BlockSpec((None, 2 * CH_S, CH_W), lambda j, i: (j, 0, 0))] + [ANY] * ng,
        out_specs=[pl.BlockSpec((None, tm, 2 * CH_S), lambda j, i: (j, i, 0)),
                   pl.BlockSpec((tm, CH_W), lambda j, i: (i, j))] + [ANY] * ng,
        out_shape=[jax.ShapeDtypeStruct((SSM_CHUNKS, s, 2 * CH_S), F32), jax.ShapeDtypeStruct((s, SSM_WIDTH), F32)]
        + _gather_outputs(gather),
        scratch_shapes=[pltpu.VMEM((SUBLANES, 2 * CH_S), F32)] + (_gather_sems(ng) if ng else []),
        compiler_params=_cparams(2),
    )(proj, wb, tab, wc, *gather)
    return outs[0], outs[1], (_gather_own(outs[2:], gather, gather_bases) if ng else [])


def _glu_forward(y, u, d, wg_ref, bg):
    yf = y + d * u
    z = _gelu(yf)
    zb = z.astype(BF16)
    zz = jnp.concatenate([_dot(zb, wg_ref[sh]) for sh in range(N_CHIPS)], axis=-1) + bg
    return yf, z, zz[:, 0:SSM_WIDTH], zz[:, SSM_WIDTH:2 * SSM_WIDTH]


def _ssm_glu_fwd(y, proj, d, w_glu_l, b_glu):
    s = y.shape[0]
    tm = _row_tile(s, 1024)

    def body(y_ref, u_ref, gs_ref, d_ref, wg_ref, bg_ref, o_ref):
        _, _, val, gate = _glu_forward(y_ref[...], u_ref[...], d_ref[...], wg_ref, bg_ref[...])
        gs = gs_ref[...]
        o_ref[...] = val * _sigmoid(gate) * (gs * _sigmoid(gs))

    row = lambda i: (i, 0)
    return pl.pallas_call(
        body, name="ssm_glu", grid=(s // tm,),
        in_specs=[pl.BlockSpec((tm, SSM_WIDTH), row), pl.BlockSpec((tm, SSM_WIDTH), row),
                  pl.BlockSpec((tm, SSM_WIDTH), lambda i: (i, 1)), pl.BlockSpec((1, SSM_WIDTH), lambda i: (0, 0)),
                  pl.BlockSpec((N_CHIPS, SSM_WIDTH, ROW_SHARD), lambda i: (0, 0, 0)),
                  pl.BlockSpec((1, 2 * SSM_WIDTH), lambda i: (0, 0))],
        out_specs=pl.BlockSpec((tm, SSM_WIDTH), row),
        out_shape=jax.ShapeDtypeStruct((s, SSM_WIDTH), F32),
        compiler_params=_cparams(1),
    )(y, proj, proj, d, w_glu_l, b_glu)


def _tri(kind):
    r = jnp.arange(ATTN_BLOCK)
    if kind == "suffix_incl":
        m = r[:, None] >= r[None, :]
    else:
        m = r[:, None] < r[None, :]
    return jnp.concatenate([m, jnp.ones_like(m)], axis=1).astype(BF16)


def _head_masks():
    lane = lax.broadcasted_iota(jnp.int32, (1, 2 * HEAD_DIM), 1)
    return [lane < HEAD_DIM, lane >= HEAD_DIM]


def _chain_step(t, base, n_sub, first, q_ref, k_ref, tri_ref, l_scr, per_chain):
    tb = ATTN_BLOCK
    row = lax.broadcasted_iota(jnp.int32, (tb, tb), 0)
    col = lax.broadcasted_iota(jnp.int32, (tb, tb), 1)
    masks = _head_masks()
    blks = [base + a - t for a in range(n_sub)]
    r0s = [pl.multiple_of(jnp.maximum(blk, 0) * tb, tb) for blk in blks]
    zs = []
    for a in range(n_sub):
        kb = k_ref[pl.ds(r0s[a], tb), :]
        qa = q_ref[a * tb:(a + 1) * tb, :]
        for mask in masks:
            zs.append(_dot_nt(jnp.where(mask, qa, jnp.zeros_like(qa)), kb))
    parts = []
    for z in zs:
        ls = jnp.minimum(-z, 0.0) - jnp.log(1.0 + jnp.exp(-jnp.abs(z)))
        if first:
            ls = jnp.where(col < row, ls, 0.0)
        parts.append(_split_hilo(ls))
    tri = tri_ref[...]
    sums = [_dot(hi, tri) + _dot(lo, tri) for hi, lo in parts]
    top = None
    ws = []
    for c, (z, sm) in enumerate(zip(zs, sums)):
        if first:
            lsum = jnp.zeros((tb, tb), F32)
        else:
            lsum = l_scr[c] + jnp.where(blks[c // 2] >= 0, 0.0, -1e30)
        w = jnp.exp(z + sm[:, 0:tb] + lsum)
        if first:
            w = jnp.where(col < row, w, 0.0)
        ws.append(w)
        lsum = lsum + sm[:, tb:2 * tb]
        l_scr[c] = lsum
        top = lsum if top is None else jnp.maximum(top, lsum)
    for c, (z, w) in enumerate(zip(zs, ws)):
        per_chain(c // 2, c % 2, c, r0s[c // 2], z, w)
    return jnp.max(top)


def _chain_sweep(base, n_sub, q_ref, k_ref, tri_ref, l_scr, per_chain):
    top = _chain_step(0, base, n_sub, True, q_ref, k_ref, tri_ref, l_scr, functools.partial(per_chain, 0))

    def cond(carry):
        t, top = carry
        return jnp.logical_and(t <= base + n_sub - 1, top > EXP_ZERO)

    def step(carry):
        t, _ = carry
        return t + 1, _chain_step(t, base, n_sub, False, q_ref, k_ref, tri_ref, l_scr, functools.partial(per_chain, t))

    steps, _ = lax.while_loop(cond, step, (jnp.int32(1), top))
    return steps


ATTN_SUB_FWD = 16
ATTN_SUB_BWD = 8


def _attn_fwd(qkv, proj, gather=None, gather_bases=None):
    s = qkv.shape[0]
    tb = ATTN_BLOCK
    n_sub = min(ATTN_SUB_FWD, s // tb)
    tq = n_sub * tb
    n_hp = ATTN_WIDTH // (2 * HEAD_DIM)
    gather = [] if gather is None else gather
    ng = len(gather)

    def body(*refs):
        q_ref, k_ref, v_ref, g_ref, tri_ref = refs[0:5]
        g_ins = refs[5:5 + ng]
        o_ref, ya_ref = refs[5 + ng:7 + ng]
        g_outs = refs[7 + ng:7 + 2 * ng]
        l_scr = refs[7 + 2 * ng]
        sems = refs[8 + 2 * ng:]
        i = pl.program_id(1)
        masks = _head_masks()
        o_ref[...] = jnp.zeros_like(o_ref)

        if ng:
            @pl.when(jnp.logical_and(pl.program_id(0) == 0, i == 0))
            def _():
                _gather_start(g_ins, gather_bases, g_outs, sems)

        def per_chain(t, a, h, c, r0, z, w):
            vb = v_ref[pl.ds(r0, tb), :]
            vb = jnp.where(masks[h], vb, jnp.zeros_like(vb))
            o_ref[a * tb:(a + 1) * tb, :] += _dot(w.astype(BF16), vb)

        _chain_sweep(i * n_sub, n_sub, q_ref, k_ref, tri_ref, l_scr, per_chain)
        g = g_ref[...]
        ya_ref[...] = o_ref[...] * (g * _sigmoid(g))

        if ng:
            @pl.when(jnp.logical_and(pl.program_id(0) == n_hp - 1, i == s // tq - 1))
            def _():
                _gather_finish(g_ins, gather_bases, g_outs, sems)

    hp_blk = lambda off: pl.BlockSpec((tq, 2 * HEAD_DIM), lambda hp, i: (i, off + hp))
    res = lambda off: pl.BlockSpec((s, 2 * HEAD_DIM), lambda hp, i: (0, off + hp))
    outs = pl.pallas_call(
        body, name="attn_fwd_gather" if ng else "attn_fwd", grid=(n_hp, s // tq),
        in_specs=[hp_blk(0), res(4), res(8), hp_blk(20), pl.BlockSpec((tb, 2 * tb), lambda hp, i: (0, 0))] + [ANY] * ng,
        out_specs=[hp_blk(0), hp_blk(0)] + [ANY] * ng,
        out_shape=[jax.ShapeDtypeStruct((s, ATTN_WIDTH), F32)] * 2 + _gather_outputs(gather),
        scratch_shapes=[pltpu.VMEM((2 * n_sub, tb, tb), F32)] + (_gather_sems(ng) if ng else []),
        compiler_params=_cparams(2),
    )(qkv, qkv, qkv, proj, _tri("suffix_incl"), *gather)
    return outs[0], outs[1], (_gather_own(outs[2:], gather, gather_bases) if ng else [])


def _rms_rows(x, g):
    r = lax.rsqrt(jnp.mean(x * x, axis=-1, keepdims=True) + RMS_EPS)
    return r, x * r * g


def _ple_forward(h1, p, g2, wpg_ref, wpp_ref):
    r2, hn2 = _rms_rows(h1, g2)
    hb = hn2.astype(BF16)
    gpre = _dot(hb[:, 0:ROW_SHARD], wpg_ref[0])
    for sh in range(1, N_CHIPS):
        gpre = gpre + _dot(hb[:, ROW_SHARD * sh:ROW_SHARD * (sh + 1)], wpg_ref[sh])
    gate = _sigmoid(gpre)
    pb = p.astype(BF16)
    pp = jnp.concatenate([_dot(pb, wpp_ref[sh]) for sh in range(N_CHIPS)], axis=-1)
    return r2, hb, gate, pp


def _colsum8(a):
    t = a.shape[0]
    return a.reshape(t // SUBLANES, SUBLANES, a.shape[1]).sum(axis=0)


def _sq_err_grad(y, target):
    e = y - target
    sq = _colsum8(e * e)
    part = sq[:, 0:128]
    for b in range(1, D_MODEL // 128):
        part = part + sq[:, 128 * b:128 * (b + 1)]
    return e / D_MODEL, part


def _out_ple(h, ys, ya, p, g2, w_out_l, w_pg_l, w_pp_l, target=None):
    s = h.shape[0]
    tm = _row_tile(s, 512)
    last = target is not None

    def body(*refs):
        h_ref, ys_ref, ya_ref, p_ref, g_ref, wo_ref, wpg_ref, wpp_ref = refs[0:8]
        h1_ref, h2_ref = refs[8 + last], refs[9 + last]
        ysb = ys_ref[...].astype(BF16)
        yab = ya_ref[...].astype(BF16)
        h1 = h_ref[...]
        for sh, src in enumerate((ysb[:, 0:ROW_SHARD], ysb[:, ROW_SHARD:], yab[:, 0:ROW_SHARD], yab[:, ROW_SHARD:])):
            h1 = h1 + _dot(src, wo_ref[sh])
        _, _, gate, pp = _ple_forward(h1, p_ref[...], g_ref[...], wpg_ref, wpp_ref)
        h1_ref[...] = h1
        h2 = h1 + gate * pp
        if last:
            acc_ref = refs[11]

            @pl.when(pl.program_id(0) == 0)
            def _():
                acc_ref[...] = jnp.zeros_like(acc_ref)

            h2_ref[...], part = _sq_err_grad(h2, refs[8][...])
            acc_ref[...] += part
        else:
            h2_ref[...] = h2

    row = lambda i: (i, 0)
    big = pl.BlockSpec((tm, D_MODEL), row)
    wspec = lambda r, cdim: pl.BlockSpec((N_CHIPS, r, cdim), lambda i: (0, 0, 0))
    acc = pl.BlockSpec((SUBLANES, 128), lambda i: (0, 0))
    return pl.pallas_call(
        body, name="out_ple_loss" if last else "out_ple", grid=(s // tm,),
        in_specs=[big, pl.BlockSpec((tm, SSM_WIDTH), row), pl.BlockSpec((tm, ATTN_WIDTH), row),
                  pl.BlockSpec((tm, PLE_DIM), row), pl.BlockSpec((1, D_MODEL), lambda i: (0, 0)),
                  wspec(ROW_SHARD, D_MODEL), wspec(ROW_SHARD, D_MODEL), wspec(PLE_DIM, ROW_SHARD)] + [big] * last,
        out_specs=[big] * 2 + [acc] * last,
        out_shape=[jax.ShapeDtypeStruct((s, D_MODEL), F32)] * 2 + [jax.ShapeDtypeStruct((SUBLANES, 128), F32)] * last,
        compiler_params=_cparams(1),
    )(h, ys, ya, p, g2, w_out_l, w_pg_l, w_pp_l, *([target] if last else []))


def _rms_bwd(x, r, g, dy):
    gdy = g * dy
    dx = r * gdy - x * (r * r * r) * jnp.mean(x * gdy, axis=-1, keepdims=True)
    return dx, x * r * dy


def _out_ple_bwd(dh2, h1, p, g2, w_out_l, w_pg_l, w_pp_l):
    s = h1.shape[0]
    tm = _row_tile(s, 512)

    def body(dh2_ref, h1_ref, p_ref, g_ref, wo_ref, wpg_ref, wpp_ref,
             dh1_ref, dmix_ref, hn_ref, dgp_ref, dpp_ref, dh1b_ref, dg_ref):
        @pl.when(pl.program_id(0) == 0)
        def _():
            dg_ref[...] = jnp.zeros_like(dg_ref)

        h1 = h1_ref[...]
        dh2 = dh2_ref[...]
        g2v = g_ref[...]
        r2, hb, gate, pp = _ple_forward(h1, p_ref[...], g2v, wpg_ref, wpp_ref)
        dgp = (dh2 * pp) * gate * (1.0 - gate)
        dgpb = dgp.astype(BF16)
        dhn = jnp.concatenate([_dot_nt(dgpb, wpg_ref[sh]) for sh in range(N_CHIPS)], axis=-1)
        dx, dgrow = _rms_bwd(h1, r2, g2v, dhn)
        dh1 = dh2 + dx
        dh1b = dh1.astype(BF16)
        dh1_ref[...] = dh1
        dh1b_ref[...] = dh1b
        hn_ref[...] = hb
        dgp_ref[...] = dgpb
        dpp_ref[...] = (dh2 * gate).astype(BF16)
        dg_ref[...] += _colsum8(dgrow)
        for sh in range(N_CHIPS):
            dmix_ref[:, ROW_SHARD * sh:ROW_SHARD * (sh + 1)] = _dot_nt(dh1b, wo_ref[sh])

    row = lambda i: (i, 0)
    wspec = lambda r, cdim: pl.BlockSpec((N_CHIPS, r, cdim), lambda i: (0, 0, 0))
    big = pl.BlockSpec((tm, D_MODEL), row)
    return pl.pallas_call(
        body, name="out_ple_bwd", grid=(s // tm,),
        in_specs=[big, big, pl.BlockSpec((tm, PLE_DIM), row), pl.BlockSpec((1, D_MODEL), lambda i: (0, 0)),
                  wspec(ROW_SHARD, D_MODEL), wspec(ROW_SHARD, D_MODEL), wspec(PLE_DIM, ROW_SHARD)],
        out_specs=[big] * 6 + [pl.BlockSpec((SUBLANES, D_MODEL), lambda i: (0, 0))],
        out_shape=[jax.ShapeDtypeStruct((s, D_MODEL), F32)] * 2 + [jax.ShapeDtypeStruct((s, D_MODEL), BF16)] * 4
        + [jax.ShapeDtypeStruct((SUBLANES, D_MODEL), F32)],
        compiler_params=_cparams(1),
    )(dh2, h1, p, g2, w_out_l, w_pg_l, w_pp_l)


def _tn_matmul(a, b, n_blocks, block_a, name, into=None, first_block=0, total_blocks=None):
    s = a.shape[0]
    tk = _row_tile(s, 1024)
    nk = s // tk
    total_blocks = n_blocks if total_blocks is None else total_blocks
    ka, nb = a.shape[1], b.shape[1]
    if block_a:
        ka //= n_blocks
    else:
        nb //= n_blocks

    def body(*refs):
        a_ref, b_ref, o_ref, acc_ref = refs[0], refs[1], refs[-2], refs[-1]

        @pl.when(pl.program_id(0) == 0)
        def _():
            acc_ref[...] = jnp.zeros_like(acc_ref)

        at = a_ref[...].astype(BF16).T
        bb = b_ref[...].astype(BF16)
        for sh in range(n_blocks):
            if block_a:
                acc_ref[sh] += _dot(at[ka * sh:ka * (sh + 1), :], bb)
            else:
                acc_ref[sh] += _dot(at, bb[:, nb * sh:nb * (sh + 1)])

        @pl.when(pl.program_id(0) == nk - 1)
        def _():
            o_ref[...] = acc_ref[...].astype(BF16)

    in_specs = [pl.BlockSpec((tk, a.shape[1]), lambda i: (i, 0)), pl.BlockSpec((tk, b.shape[1]), lambda i: (i, 0))]
    operands = [a, b]
    aliases = {}
    if into is not None:
        in_specs.append(ANY)
        operands.append(into)
        aliases = {2: 0}
    return pl.pallas_call(
        body, name=name, grid=(nk,),
        in_specs=in_specs,
        out_specs=pl.BlockSpec((n_blocks, ka, nb), lambda i: (first_block // n_blocks, 0, 0)),
        out_shape=jax.ShapeDtypeStruct((total_blocks, ka, nb), BF16),
        scratch_shapes=[pltpu.VMEM((n_blocks, ka, nb), F32)],
        input_output_aliases=aliases,
        compiler_params=_cparams(1),
    )(*operands)


def _attn_bwd(qkv, o, proj, dmix, scatter=None):
    scatter = [] if scatter is None else scatter
    nsc = len(scatter)
    s = qkv.shape[0]
    tb = ATTN_BLOCK
    nq = s // tb
    n_sub = min(ATTN_SUB_BWD, nq)
    tq = n_sub * tb
    n_chain = 2 * n_sub

    def body(*refs):
        q_ref, k_ref, v_ref, o_ref, g_ref, dya_ref, tri_s_ref, tri_p_ref = refs[0:8]
        sc_ins = refs[8:8 + nsc]
        dq_ref, dk_ref, dv_ref, dg_ref = refs[8 + nsc:12 + nsc]
        sc_outs = refs[12 + nsc:12 + 2 * nsc]
        do_scr, l_scr, g_scr, s_scr, w_scr = refs[12 + 2 * nsc:17 + 2 * nsc]
        sc_sems = refs[17 + 2 * nsc:]
        i = pl.program_id(1)
        base = i * n_sub

        if nsc:
            @pl.when(jnp.logical_and(pl.program_id(0) == 0, i == 0))
            def _():
                _scatter_start(sc_ins, sc_outs, sc_sems)

        @pl.when(i == 0)
        def _():
            dk_ref[...] = jnp.zeros_like(dk_ref)
            dv_ref[...] = jnp.zeros_like(dv_ref)

        g = g_ref[...]
        sg = _sigmoid(g)
        dya = dya_ref[...]
        do_scr[...] = (dya * (g * sg)).astype(BF16)
        dg_ref[...] = dya * o_ref[...] * (sg * (1.0 + g * (1.0 - sg)))
        dq_ref[...] = jnp.zeros_like(dq_ref)
        g_scr[...] = jnp.zeros_like(g_scr)
        masks = _head_masks()

        def keep(t, a, h, c, r0, z, w):
            s_scr[c, t] = _sigmoid(z).astype(BF16)
            w_scr[c, t] = w.astype(BF16)

        steps = _chain_sweep(base, n_sub, q_ref, k_ref, tri_s_ref, l_scr, keep)
        row = lax.broadcasted_iota(jnp.int32, (tb, tb), 0)
        col = lax.broadcasted_iota(jnp.int32, (tb, tb), 1)

        def back(it, carry):
            t = steps - 1 - it
            r0s = [pl.multiple_of(jnp.maximum(base + a - t, 0) * tb, tb) for a in range(n_sub)]
            qhs, dohs, khs, gws = [], [], [], []
            for a in range(n_sub):
                kb = k_ref[pl.ds(r0s[a], tb), :]
                vb = v_ref[pl.ds(r0s[a], tb), :]
                qa = q_ref[a * tb:(a + 1) * tb, :]
                doa = do_scr[a * tb:(a + 1) * tb, :]
                for h, mask in enumerate(masks):
                    qhs.append(jnp.where(mask, qa, jnp.zeros_like(qa)))
                    khs.append(jnp.where(mask, kb, jnp.zeros_like(kb)))
                    dohs.append(jnp.where(mask, doa, jnp.zeros_like(doa)))
                    gws.append(w_scr[2 * a + h, t].astype(F32) * _dot_nt(dohs[-1], vb))
            parts = [_split_hilo(gw) for gw in gws]
            tri = tri_p_ref[...]
            sums = [_dot(hi, tri) + _dot(lo, tri) for hi, lo in parts]
            dzs = []
            for c, (gw, sm) in enumerate(zip(gws, sums)):
                gsum = g_scr[c]
                dz = gw - (gw + sm[:, 0:tb] + gsum) * s_scr[c, t].astype(F32)
                dz = jnp.where(col < row + t * tb, dz, 0.0)
                g_scr[c] = gsum + sm[:, tb:2 * tb]
                dzs.append(dz.astype(BF16))
            for c, dzb in enumerate(dzs):
                a = c // 2
                dk_ref[pl.ds(r0s[a], tb), :] += _dot_tn(dzb, qhs[c])
                dv_ref[pl.ds(r0s[a], tb), :] += _dot_tn(w_scr[c, t], dohs[c])
                dq_ref[a * tb:(a + 1) * tb, :] += _dot(dzb, khs[c])
            return carry

        lax.fori_loop(0, steps, back, 0)

        if nsc:
            @pl.when(jnp.logical_and(pl.program_id(0) == n_hp - 1, i == s // tq - 1))
            def _():
                _scatter_finish(sc_ins, sc_outs, sc_sems)

    n_hp = ATTN_WIDTH // (2 * HEAD_DIM)
    hp_blk = lambda off: pl.BlockSpec((tq, 2 * HEAD_DIM), lambda hp, i: (i, off + hp))
    res = lambda off: pl.BlockSpec((s, 2 * HEAD_DIM), lambda hp, i: (0, off + hp))
    tri = pl.BlockSpec((tb, 2 * tb), lambda hp, i: (0, 0))
    outs = pl.pallas_call(
        body, name="attn_bwd_scatter" if nsc else "attn_bwd", grid=(n_hp, s // tq),
        in_specs=[hp_blk(0), res(4), res(8), hp_blk(0), hp_blk(20), hp_blk(4), tri, tri] + [ANY] * nsc,
        out_specs=[hp_blk(0), res(0), res(0), hp_blk(0)] + [ANY] * nsc,
        out_shape=[jax.ShapeDtypeStruct((s, ATTN_WIDTH), F32)] * 4 + [jax.ShapeDtypeStruct(a.shape, a.dtype) for a in scatter],
        scratch_shapes=[pltpu.VMEM((tq, 2 * HEAD_DIM), BF16), pltpu.VMEM((n_chain, tb, tb), F32),
                        pltpu.VMEM((n_chain, tb, tb), F32), pltpu.VMEM((n_chain, nq, tb, tb), BF16),
                        pltpu.VMEM((n_chain, nq, tb, tb), BF16)] + (_scatter_sems(nsc) if nsc else []),
        compiler_params=_cparams(2, vmem_limit=V7X_VMEM_LIMIT_ATTN_BWD),
    )(qkv, qkv, qkv, o, proj, dmix, _tri("suffix_incl"), _tri("prefix_strict"), *scatter)
    return outs[0], outs[1], outs[2], outs[3], outs[4:]


def _ssm_glu_bwd(dmix, y, proj, d, w_glu_l, b_glu):
    s = y.shape[0]
    tm = _row_tile(s, 1024)

    def body(dys_ref, y_ref, u_ref, gs_ref, d_ref, wg_ref, bg_ref,
             dyf_ref, du_ref, dgs_ref, z_ref, dzz_ref, dd_ref, db_ref):
        @pl.when(pl.program_id(0) == 0)
        def _():
            dd_ref[...] = jnp.zeros_like(dd_ref)
            db_ref[...] = jnp.zeros_like(db_ref)

        u = u_ref[...]
        dv = d_ref[...]
        yf, z, val, gate = _glu_forward(y_ref[...], u, dv, wg_ref, bg_ref[...])
        gs = gs_ref[...]
        sgs = _sigmoid(gs)
        sgate = _sigmoid(gate)
        dys = dys_ref[...]
        dgv = dys * (gs * sgs)
        dgs_ref[...] = dys * (val * sgate) * (sgs * (1.0 + gs * (1.0 - sgs)))
        dzz = jnp.concatenate([dgv * sgate, dgv * val * sgate * (1.0 - sgate)], axis=-1)
        dzzb = dzz.astype(BF16)
        dz = _dot_nt(dzzb[:, 0:ROW_SHARD], wg_ref[0])
        for sh in range(1, N_CHIPS):
            dz = dz + _dot_nt(dzzb[:, ROW_SHARD * sh:ROW_SHARD * (sh + 1)], wg_ref[sh])
        dyf = dz * _gelu_grad(yf)
        dyf_ref[...] = dyf
        du_ref[...] = dyf * dv
        z_ref[...] = z.astype(BF16)
        dzz_ref[...] = dzzb
        dd_ref[...] += _colsum8(dyf * u)
        db_ref[...] += _colsum8(dzz)

    row = lambda i: (i, 0)
    half = pl.BlockSpec((tm, SSM_WIDTH), row)
    return pl.pallas_call(
        body, name="ssm_glu_bwd", grid=(s // tm,),
        in_specs=[half, half, half, pl.BlockSpec((tm, SSM_WIDTH), lambda i: (i, 1)),
                  pl.BlockSpec((1, SSM_WIDTH), lambda i: (0, 0)),
                  pl.BlockSpec((N_CHIPS, SSM_WIDTH, ROW_SHARD), lambda i: (0, 0, 0)),
                  pl.BlockSpec((1, 2 * SSM_WIDTH), lambda i: (0, 0))],
        out_specs=[half, half, half, half, pl.BlockSpec((tm, 2 * SSM_WIDTH), row),
                   pl.BlockSpec((SUBLANES, SSM_WIDTH), lambda i: (0, 0)),
                   pl.BlockSpec((SUBLANES, 2 * SSM_WIDTH), lambda i: (0, 0))],
        out_shape=[jax.ShapeDtypeStruct((s, SSM_WIDTH), F32)] * 3
        + [jax.ShapeDtypeStruct((s, SSM_WIDTH), BF16), jax.ShapeDtypeStruct((s, 2 * SSM_WIDTH), BF16),
           jax.ShapeDtypeStruct((SUBLANES, SSM_WIDTH), F32), jax.ShapeDtypeStruct((SUBLANES, 2 * SSM_WIDTH), F32)],
        compiler_params=_cparams(1),
    )(dmix, y, proj, proj, d, w_glu_l, b_glu)


def _ssm_scan_bwd(dyf, xs, proj, wct, tab_rev, wbt):
    s = dyf.shape[0]
    tm = _row_tile(s, SCAN_TILE)
    nt = s // tm
    length = tm // SUBLANES

    def body(dy_ref, xs_ref, u_ref, wct_ref, tab_ref, wbt_ref, du_ref, dwc_ref, dwb_ref, da_ref, lam_ref, carry_ref):
        @pl.when(pl.program_id(1) == 0)
        def _():
            carry_ref[...] = jnp.zeros_like(carry_ref)
            dwc_ref[...] = jnp.zeros_like(dwc_ref)
            dwb_ref[...] = jnp.zeros_like(dwb_ref)
            da_ref[...] = jnp.zeros_like(da_ref)

        dyp = _interleave_chunks(dy_ref[...]).astype(BF16)
        up = _interleave_chunks(u_ref[...]).astype(BF16)
        lam_ref[...] = _dot(dyp, wct_ref[...])

        def tail(r0, lr, li, carry):
            er, ei, dar, dai = carry
            xr = xs_ref[pl.ds(r0, SUBLANES), 0:CH_S]
            xi = xs_ref[pl.ds(r0, SUBLANES), CH_S:2 * CH_S]
            return lr, li, dar + (xr * er + xi * ei), dai + (xr * ei - xi * er)

        fix, (gr, gi) = _chunk_scan(lam_ref, tab_ref, carry_ref, length, reverse=True, tail=tail)
        zero = jnp.zeros((SUBLANES, CH_S), F32)
        _, _, dar, dai = lax.fori_loop(0, length, fix, (gr, gi, zero, zero), unroll=2)
        da_ref[:, 0:CH_S] += dar
        da_ref[:, CH_S:2 * CH_S] += dai
        lamb = lam_ref[...].astype(BF16)
        du_ref[...] = _time_order(_dot(lamb, wbt_ref[...]))
        dwc_ref[...] += _dot_tn(xs_ref[...].astype(BF16), dyp)
        dwb_ref[...] += _dot_tn(up, lamb)

    rev = lambda j, i: (nt - 1 - i, j)
    return pl.pallas_call(
        body, name="ssm_scan_bwd", grid=(SSM_CHUNKS, nt),
        in_specs=[pl.BlockSpec((tm, CH_W), rev),
                  pl.BlockSpec((None, tm, 2 * CH_S), lambda j, i: (j, nt - 1 - i, 0)),
                  pl.BlockSpec((tm, CH_W), rev),
                  pl.BlockSpec((None, CH_W, 2 * CH_S), lambda j, i: (j, 0, 0)),
                  pl.BlockSpec((None, length, 2 * CH_S), lambda j, i: (j, 0, 0)),
                  pl.BlockSpec((None, 2 * CH_S, CH_W), lambda j, i: (j, 0, 0))],
        out_specs=[pl.BlockSpec((tm, CH_W), rev),
                   pl.BlockSpec((None, 2 * CH_S, CH_W), lambda j, i: (j, 0, 0)),
                   pl.BlockSpec((None, CH_W, 2 * CH_S), lambda j, i: (j, 0, 0)),
                   pl.BlockSpec((None, SUBLANES, 2 * CH_S), lambda j, i: (j, 0, 0))],
        out_shape=[jax.ShapeDtypeStruct((s, SSM_WIDTH), F32),
                   jax.ShapeDtypeStruct((SSM_CHUNKS, 2 * CH_S, CH_W), F32),
                   jax.ShapeDtypeStruct((SSM_CHUNKS, CH_W, 2 * CH_S), F32),
                   jax.ShapeDtypeStruct((SSM_CHUNKS, SUBLANES, 2 * CH_S), F32)],
        scratch_shapes=[pltpu.VMEM((tm, 2 * CH_S), F32), pltpu.VMEM((SUBLANES, 2 * CH_S), F32)],
        compiler_params=_cparams(2),
    )(dyf, xs, proj, wct, tab_rev, wbt)


def _in_proj_bwd(h, g1, w_in_l, qg, kg, proj, du_a, du_b, dgs, dq, dk, dv, dga, dh1):
    s = h.shape[0]
    tm = _row_tile(s, 256)

    def body(h_ref, g_ref, w_ref, qg_ref, kg_ref, ones_ref, q_ref, k_ref, dua_ref, dub_ref, dgs_ref, dq_ref, dk_ref,
             dv_ref, dga_ref, dh1_ref, dh_ref, hn_ref, dp_ref, dg1_ref, dqg_ref, dkg_ref):
        @pl.when(pl.program_id(0) == 0)
        def _():
            dg1_ref[...] = jnp.zeros_like(dg1_ref)
            dqg_ref[...] = jnp.zeros_like(dqg_ref)
            dkg_ref[...] = jnp.zeros_like(dkg_ref)

        ones = ones_ref[...]

        def head_norm_bwd(x, gain, dy):
            r = lax.rsqrt(_dot_hilo(x * x, ones) + RMS_EPS)
            gdy = gain * dy
            dx = r * gdy - x * (r * r * r) * _dot_hilo(x * gdy, ones)
            return dx, x * r * dy

        dqr, dqg_rows = head_norm_bwd(q_ref[...], qg_ref[...], dq_ref[...] * ATTN_SCALE)
        dkr, dkg_rows = head_norm_bwd(k_ref[...], kg_ref[...], dk_ref[...])
        dqg_ref[...] += _colsum8(dqg_rows)
        dkg_ref[...] += _colsum8(dkg_rows)
        dp_ref[:, 0:512] = (dua_ref[...] + dub_ref[...]).astype(BF16)
        dp_ref[:, 512:1024] = dgs_ref[...].astype(BF16)
        dp_ref[:, 1024:1536] = dqr.astype(BF16)
        dp_ref[:, 1536:2048] = dkr.astype(BF16)
        dp_ref[:, 2048:2560] = dv_ref[...].astype(BF16)
        dp_ref[:, 2560:3072] = dga_ref[...].astype(BF16)
        dhn = _dot_nt(dp_ref[:, 0:IN_SHARD], w_ref[0])
        for sh in range(1, N_CHIPS):
            dhn = dhn + _dot_nt(dp_ref[:, IN_SHARD * sh:IN_SHARD * (sh + 1)], w_ref[sh])
        x = h_ref[...]
        gv = g_ref[...]
        r, hn = _rms_rows(x, gv)
        dx, dg_rows = _rms_bwd(x, r, gv, dhn)
        dh_ref[...] = dh1_ref[...] + dx
        hn_ref[...] = hn.astype(BF16)
        dg1_ref[...] += _colsum8(dg_rows)

    row = lambda i: (i, 0)
    full = lambda shape: pl.BlockSpec(shape, lambda i: (0,) * len(shape))
    big = pl.BlockSpec((tm, D_MODEL), row)
    half = pl.BlockSpec((tm, 512), row)
    return pl.pallas_call(
        body, name="in_proj_bwd", grid=(s // tm,),
        in_specs=[big, full((1, D_MODEL)), full((N_CHIPS, D_MODEL, IN_SHARD)),
                  full((1, ATTN_WIDTH)), full((1, ATTN_WIDTH)), full((ATTN_WIDTH, ATTN_WIDTH)),
                  pl.BlockSpec((tm, 512), lambda i: (i, 2)), pl.BlockSpec((tm, 512), lambda i: (i, 3)),
                  half, half, half, half, half, half, half, big],
        out_specs=[big, big, pl.BlockSpec((tm, IN_COLS), row), pl.BlockSpec((SUBLANES, D_MODEL), lambda i: (0, 0)),
                   pl.BlockSpec((SUBLANES, ATTN_WIDTH), lambda i: (0, 0)), pl.BlockSpec((SUBLANES, ATTN_WIDTH), lambda i: (0, 0))],
        out_shape=[jax.ShapeDtypeStruct((s, D_MODEL), F32), jax.ShapeDtypeStruct((s, D_MODEL), BF16),
                   jax.ShapeDtypeStruct((s, IN_COLS), BF16), jax.ShapeDtypeStruct((SUBLANES, D_MODEL), F32),
                   jax.ShapeDtypeStruct((SUBLANES, ATTN_WIDTH), F32), jax.ShapeDtypeStruct((SUBLANES, ATTN_WIDTH), F32)],
        compiler_params=_cparams(1),
    )(h, g1, w_in_l, qg, kg, _head_ones(), proj, proj, du_a, du_b, dgs, dq, dk, dv, dga, dh1)


SMALL_NAMES = ("mix_norm_g", "ssm_a_re", "ssm_a_im", "ssm_log_dt", "ssm_b_re", "ssm_b_im", "ssm_c_re", "ssm_c_im",
               "ssm_d", "ssm_b_glu", "q_norm_g", "k_norm_g", "ple_norm_g")
SMALL_4D = ("ssm_b_re", "ssm_b_im", "ssm_c_re", "ssm_c_im")
BIG_NAMES = ("w_in", "ssm_w_glu", "w_out", "w_ple_gate", "w_ple_proj")


def _ssm_setup(sm, layer, length):
    col = lambda a: a[layer].reshape(1, N_STATES)
    a_re, a_im = col(sm["ssm_a_re"]), col(sm["ssm_a_im"])
    log_dt = jnp.repeat(sm["ssm_log_dt"][layer], SSM_STATE).reshape(1, N_STATES)
    b_re = sm["ssm_b_re"][layer].reshape(N_STATES, SSM_GROUP).T
    b_im = sm["ssm_b_im"][layer].reshape(N_STATES, SSM_GROUP).T
    by_channel = lambda c: c[layer].transpose(1, 0, 2).reshape(SSM_GROUP, N_STATES)
    disc_in = (a_re, a_im, log_dt, b_re, b_im)
    wb, wbt, wct, wc, tab, tab_rev = _disc_fwd(*disc_in, by_channel(sm["ssm_c_re"]), by_channel(sm["ssm_c_im"]), length)
    return dict(disc_in=disc_in, wb=wb, wbt=wbt, wc=wc, wct=wct, tab=tab, tab_rev=tab_rev)


def _whole_blocks(names, gathered):
    return {n: g.reshape(N_CHIPS, 2 * g.shape[2], g.shape[3]) for n, g in zip(names, gathered)}


def _local_step(x, p, target, sm, w_in0, local=None, gathered=None, layer1_hook=None):
    wg = [dict(w_in=w_in0), {}] if gathered is None else gathered
    tile8 = lambda a: jnp.tile(a, ATTN_WIDTH // HEAD_DIM).reshape(1, ATTN_WIDTH)
    saved = []
    h = x
    for l in range(N_LAYERS):
        ssm = _ssm_setup(sm, l, _row_tile(x.shape[0], SCAN_TILE) // SUBLANES)
        g1 = sm["mix_norm_g"][l].reshape(1, D_MODEL)
        g2 = sm["ple_norm_g"][l].reshape(1, D_MODEL)
        qg, kg = tile8(sm["q_norm_g"][l]), tile8(sm["k_norm_g"][l])
        dsk = sm["ssm_d"][l].reshape(1, SSM_WIDTH)
        bgl = sm["ssm_b_glu"][l].reshape(1, 2 * SSM_WIDTH)
        proj, qkv = _in_proj(h, g1, wg[l]["w_in"], qg, kg)
        if l == 0 and local is not None:
            rest = BIG_NAMES[1:]
            xs, y, got = _ssm_scan_fwd(proj, ssm["wb"], ssm["tab"], ssm["wc"], [local[n] for n in rest], [0] * len(rest))
            wg[0].update(_whole_blocks(rest, got))
            ys = _ssm_glu_fwd(y, proj, dsk, wg[0]["ssm_w_glu"], bgl)
            o, ya, got = _attn_fwd(qkv, proj, [local[n] for n in BIG_NAMES], [2] * len(BIG_NAMES))
            wg[1].update(_whole_blocks(BIG_NAMES, got))
        else:
            xs, y, _ = _ssm_scan_fwd(proj, ssm["wb"], ssm["tab"], ssm["wc"])
            ys = _ssm_glu_fwd(y, proj, dsk, wg[l]["ssm_w_glu"], bgl)
            o, ya, _ = _attn_fwd(qkv, proj)
        tail = (target,) if l == N_LAYERS - 1 else ()
        h1, h2, *sq = _out_ple(h, ys, ya, p[l], g2, wg[l]["w_out"], wg[l]["w_ple_gate"], wg[l]["w_ple_proj"], *tail)
        saved.append(dict(ssm=ssm, g1=g1, g2=g2, qg=qg, kg=kg, dsk=dsk, bgl=bgl, h=h, proj=proj, qkv=qkv, xs=xs, y=y,
                          ys=ys, o=o, ya=ya, h1=h1))
        h = h2
    dh = h
    loss = 0.5 * jnp.sum(sq[0]) / D_MODEL

    gbig = [{} for _ in range(N_LAYERS)]
    scattered = ([], [])
    gsm = {n: [None] * N_LAYERS for n in SMALL_NAMES}
    for l in reversed(range(N_LAYERS)):
        sv = saved[l]
        ssm = sv["ssm"]
        dh1, dmix, hn2b, dgpb, dppb, dh1b, dg2 = _out_ple_bwd(dh, sv["h1"], p[l], sv["g2"], wg[l]["w_out"],
                                                              wg[l]["w_ple_gate"], wg[l]["w_ple_proj"])
        gsm["ple_norm_g"][l] = dg2.sum(0)
        gbig[l]["w_ple_proj"] = _tn_matmul(p[l], dppb, N_CHIPS, False, "dw_ple_proj")
        gbig[l]["w_ple_gate"] = _tn_matmul(hn2b, dgpb, N_CHIPS, True, "dw_ple_gate")
        dwo = _tn_matmul(sv["ys"], dh1b, 2, True, "dw_out_ssm", None, 0, N_CHIPS)
        gbig[l]["w_out"] = _tn_matmul(sv["ya"], dh1b, 2, True, "dw_out_attn", dwo, 2, N_CHIPS)
        if l == 0 and layer1_hook is not None:
            chip1 = layer1_hook(gbig[1])
            dqs, dkn, dv, dga, got = _attn_bwd(sv["qkv"], sv["o"], sv["proj"], dmix, chip1)
            scattered = (chip1, got)
        else:
            dqs, dkn, dv, dga, _ = _attn_bwd(sv["qkv"], sv["o"], sv["proj"], dmix)
        dyf, du_a, dgs, zb, dzzb, dd, dbg = _ssm_glu_bwd(dmix, sv["y"], sv["proj"], sv["dsk"], wg[l]["ssm_w_glu"], sv["bgl"])
        gsm["ssm_d"][l] = dd.sum(0).reshape(SSM_GROUPS, SSM_GROUP)
        gsm["ssm_b_glu"][l] = dbg.sum(0)
        gbig[l]["ssm_w_glu"] = _tn_matmul(zb, dzzb, N_CHIPS, False, "dw_glu")
        du_b, dwc, dwb, da = _ssm_scan_bwd(dyf, sv["xs"], sv["proj"], ssm["wct"], ssm["tab_rev"], ssm["wbt"])
        d_are, d_aim, d_ldt, d_bre, d_bim, d_cre, d_cim = _disc_bwd(*ssm["disc_in"], da, dwb, dwc)
        by_group = lambda t: t.reshape(SSM_GROUP, SSM_GROUPS, SSM_STATE).transpose(1, 0, 2)
        gsm["ssm_c_re"][l] = by_group(d_cre)
        gsm["ssm_c_im"][l] = by_group(d_cim)
        gsm["ssm_a_re"][l] = d_are.reshape(SSM_GROUPS, SSM_STATE)
        gsm["ssm_a_im"][l] = d_aim.reshape(SSM_GROUPS, SSM_STATE)
        gsm["ssm_log_dt"][l] = d_ldt.reshape(SSM_GROUPS, SSM_STATE).sum(1)
        gsm["ssm_b_re"][l] = d_bre.T.reshape(SSM_GROUPS, SSM_STATE, SSM_GROUP)
        gsm["ssm_b_im"][l] = d_bim.T.reshape(SSM_GROUPS, SSM_STATE, SSM_GROUP)
        dh, hnb, dprojb, dg1, dqg, dkg = _in_proj_bwd(sv["h"], sv["g1"], wg[l]["w_in"], sv["qg"], sv["kg"], sv["proj"],
                                                      du_a, du_b, dgs, dqs, dkn, dv, dga, dh1)
        gsm["mix_norm_g"][l] = dg1.sum(0)
        gsm["q_norm_g"][l] = dqg.sum(0).reshape(-1, HEAD_DIM).sum(0)
        gsm["k_norm_g"][l] = dkg.sum(0).reshape(-1, HEAD_DIM).sum(0)
        gbig[l]["w_in"] = _tn_matmul(hnb, dprojb, N_CHIPS, False, "dw_in")
    gsm = {n: jnp.stack(v, 0) for n, v in gsm.items()}
    return loss, dh, gbig, gsm, scattered


_SMALL_PAD = 8 * 8 * 128


def _pack_small(d, extra):
    flat = jnp.concatenate([d[n].reshape(-1) for n in SMALL_NAMES] + [jnp.stack(extra)])
    n = flat.shape[0]
    padded = -(-n // _SMALL_PAD) * _SMALL_PAD
    return jnp.pad(flat, (0, padded - n))


def _unpack_small(flat, like):
    out, off = {}, 0
    for n in SMALL_NAMES:
        size = like[n].size
        out[n] = flat[off:off + size].reshape(like[n].shape)
        off += size
    return out, flat[off:]


def _half_views(arrs):
    return [a.reshape(a.shape[0], 2, a.shape[1] // 2, a.shape[2]) for a in arrs]


def _chip_sums(views, out_dtypes, tag):
    recv = _sibling_push(views, "grad_push_" + tag)
    return [_add_my_half(v, r, dt, "grad_half_add") for v, r, dt in zip(views, recv, out_dtypes)]


def kernel(x, p, mix_norm_g, w_in, ssm_a_re, ssm_a_im, ssm_log_dt, ssm_b_re, ssm_b_im, ssm_c_re, ssm_c_im, ssm_d, ssm_w_glu, ssm_b_glu, q_norm_g, k_norm_g, w_out, ple_norm_g, w_ple_gate, w_ple_proj, loss_target, m_mix_norm_g, m_w_in, m_ssm_a_re, m_ssm_a_im, m_ssm_log_dt, m_ssm_b_re, m_ssm_b_im, m_ssm_c_re, m_ssm_c_im, m_ssm_d, m_ssm_w_glu, m_ssm_b_glu, m_q_norm_g, m_k_norm_g, m_w_out, m_ple_norm_g, m_w_ple_gate, m_w_ple_proj, v_mix_norm_g, v_w_in, v_ssm_a_re, v_ssm_a_im, v_ssm_log_dt, v_ssm_b_re, v_ssm_b_im, v_ssm_c_re, v_ssm_c_im, v_ssm_d, v_ssm_w_glu, v_ssm_b_glu, v_q_norm_g, v_k_norm_g, v_w_out, v_ple_norm_g, v_w_ple_gate, v_w_ple_proj):
    args = dict(locals())
    names = ("mix_norm_g", "w_in", "ssm_a_re", "ssm_a_im", "ssm_log_dt", "ssm_b_re", "ssm_b_im", "ssm_c_re", "ssm_c_im",
             "ssm_d", "ssm_w_glu", "ssm_b_glu", "q_norm_g", "k_norm_g", "w_out", "ple_norm_g", "w_ple_gate", "w_ple_proj")
    w = {n: args[n] for n in names}
    m = {n: args["m_" + n] for n in names}
    v = {n: args["v_" + n] for n in names}

    local = {n: w[n].astype(BF16).reshape(2 * N_LAYERS, w[n].shape[1] // 2, w[n].shape[2]) for n in BIG_NAMES}
    w_in0 = _chip_gather([local["w_in"]], "w_in_gather")[0].reshape(N_CHIPS, D_MODEL, IN_SHARD)
    sm = {n: w[n] for n in SMALL_NAMES}
    nb = len(BIG_NAMES)
    loss, dx, gbig, gsm, (chip1, got1) = _local_step(
        x[0], p[:, 0], loss_target[0], sm, w_in0, local,
        layer1_hook=lambda g1: _chip_sums(_half_views([g1[n] for n in BIG_NAMES]), [BF16] * nb, "layer1"))

    small = _pack_small(gsm, [loss]).reshape(N_CHIPS, 2, SUBLANES, -1)
    chip0 = _chip_sums(_half_views([gbig[0][n] for n in BIG_NAMES]) + [small], [BF16] * nb + [F32], "layer0")
    got0 = _chip_scatter(chip0, "grad_chip_scatter")
    tot1 = [_sum4(a, own, "grad_chip_sum") for a, own in zip(got1, chip1)]
    tot0 = [_sum4(a, own, "grad_chip_sum") for a, own in zip(got0, chip0)]
    pieces = [(t, k, (l,)) for l, tots in enumerate((tot0[:nb], tot1)) for k, t in enumerate(tots)] + [(tot0[nb], nb, ())]
    joined = _sibling_join(pieces, [(N_LAYERS, 2) + t.shape for t in tot1] + [(2,) + tot0[nb].shape], "grad_sibling_join")
    small_all = _chip_gather([joined[nb]], "small_grad_gather")[0]
    small_tot = small_all.reshape(-1)
    g = {n: j.reshape(w[n].shape) for n, j in zip(BIG_NAMES, joined)}
    g_small, rest = _unpack_small(small_tot, sm)
    g.update(g_small)
    loss = rest[0]

    delta, new_m, new_v = {}, {}, {}
    for n in BIG_NAMES:
        lanes = w[n].shape[-1]
        outs = _adamw(_as_rows(w[n], lanes), _as_rows(g[n], lanes), _as_rows(m[n], lanes), _as_rows(v[n], lanes), "adamw_" + n)
        delta[n], new_m[n], new_v[n] = [o.reshape(w[n].shape) for o in outs]
    swap = lambda n, a: jnp.swapaxes(a, -1, -2) if n in ("ssm_b_re", "ssm_b_im") else a
    for group, per_layer in ((SMALL_4D, True), (tuple(n for n in SMALL_NAMES if n not in SMALL_4D), False)):
        outs = _adamw_many(*[[swap(n, d[n]) for n in group] for d in (w, g, m, v)],
                           "adamw_small_4d" if per_layer else "adamw_small", per_layer)
        for d, o in zip((delta, new_m, new_v), outs):
            d.update({n: swap(n, a) for n, a in zip(group, o)})

    return (loss, dx[None], *[g[n] for n in names], *[delta[n] for n in names],
            *[new_m[n] for n in names], *[new_v[n] for n in names])
```
